```python
import math
import jax, jax.numpy as jnp
from jax import lax
import numpy as np

D_MODEL = 1024
BATCH = 8
SEQ = 2048
DEPTH = 4

N_MIXERS = 3
EPS = 1e-6
SB_HEAD_DIM = 64
SB_HEADS = D_MODEL // SB_HEAD_DIM
Q_BLOCK = 128
CHUNK = 128
SG_WIDTH = D_MODEL
SG_GROUPS = 8
SG_HEAD_DIM = SG_WIDTH // SG_GROUPS
SSM_GROUP = 16
SSM_GROUPS = D_MODEL // SSM_GROUP
SSM_STATE = 64
DT_MIN = 1e-3
DT_MAX = 1e-1
D_FF = 2816
CONV_K = 3
N_A = (DEPTH + 2) // 3
N_B = (DEPTH + 1) // 3
N_C = DEPTH // 3

kernel_name = "hybrid_stickbreak_gmlp_s5_trunk"


def rmsnorm(x, g):
    xf = x.astype(jnp.float32)
    y = xf * lax.rsqrt(jnp.mean(xf * xf, axis=-1, keepdims=True) + EPS) * g.astype(jnp.float32)
    return y.astype(x.dtype)


def stick_breaking_attention(xn, w_qkv, w_o):
    B, L, _ = xn.shape
    qkv = (xn @ w_qkv).reshape(B, L, 3, SB_HEADS, SB_HEAD_DIM).astype(jnp.float32)
    q = qkv[:, :, 0] * (SB_HEAD_DIM ** -0.5)
    k = qkv[:, :, 1]
    v = qkv[:, :, 2]
    blocks = []
    for i in range(L // Q_BLOCK):
        t0 = i * Q_BLOCK
        t1 = t0 + Q_BLOCK
        qb, kb, vb = q[:, t0:t1], k[:, :t1], v[:, :t1]
        z = jnp.einsum('bthd,bshd->bhts', qb, kb)
        t_pos = t0 + jnp.arange(Q_BLOCK)
        s_pos = jnp.arange(t1)
        mask = s_pos[None, :] < t_pos[:, None]
        log_1m = jnp.where(mask, jax.nn.log_sigmoid(-z), 0.0)
        log_w = jax.nn.log_sigmoid(z) + lax.cumsum(log_1m, axis=3, reverse=True) - log_1m
        w = jnp.where(mask, jnp.exp(log_w), 0.0)
        blocks.append(jnp.einsum('bhts,bshd->bthd', w, vb))
    o = jnp.concatenate(blocks, axis=1).reshape(B, L, D_MODEL).astype(xn.dtype)
    return o @ w_o


def chunked_spatial_gating(xn, w_in, v_norm_g, w_s, b_s, w_o):
    B, L, _ = xn.shape
    h = jax.nn.gelu(xn @ w_in)
    u, v = jnp.split(h, 2, axis=-1)
    v = rmsnorm(v, v_norm_g)
    v = v.reshape(B, L // CHUNK, CHUNK, SG_GROUPS, SG_HEAD_DIM)
    w_causal = jnp.tril(w_s)
    sv = jnp.einsum('gts,bcsgd->bctgd', w_causal, v) + b_s.T[:, :, None]
    return (u * sv.reshape(B, L, SG_WIDTH)) @ w_o


def _complex_affine_combine(e1, e2):
    a1r, a1i, b1r, b1i = e1
    a2r, a2i, b2r, b2i = e2
    ar = a2r * a1r - a2i * a1i
    ai = a2r * a1i + a2i * a1r
    br = a2r * b1r - a2i * b1i + b2r
    bi = a2r * b1i + a2i * b1r + b2i
    return (ar, ai, br, bi)


def s5_layer(xn, w_in, lam_re, lam_im, log_dt, b_re, b_im, c_re, c_im, d_skip, w_glu):
    B, L, _ = xn.shape
    u = (xn @ w_in).astype(jnp.float32)
    ug = u.reshape(B, L, SSM_GROUPS, SSM_GROUP)
    lr = jnp.minimum(lam_re.astype(jnp.float32), -1e-4)
    li = lam_im.astype(jnp.float32)
    dt = jnp.exp(log_dt.astype(jnp.float32))[:, None]
    mag = jnp.exp(dt * lr)
    ar = mag * jnp.cos(dt * li)
    ai = mag * jnp.sin(dt * li)
    den = lr * lr + li * li
    coef_re = ((ar - 1.0) * lr + ai * li) / den
    coef_im = (ai * lr - (ar - 1.0) * li) / den
    br32, bi32 = b_re.astype(jnp.float32), b_im.astype(jnp.float32)
    bbar_re = coef_re[..., None] * br32 - coef_im[..., None] * bi32
    bbar_im = coef_re[..., None] * bi32 + coef_im[..., None] * br32
    bu_re = jnp.einsum('gph,blgh->blgp', bbar_re, ug)
    bu_im = jnp.einsum('gph,blgh->blgp', bbar_im, ug)
    a_re = jnp.broadcast_to(ar, (1, L, SSM_GROUPS, SSM_STATE))
    a_im = jnp.broadcast_to(ai, (1, L, SSM_GROUPS, SSM_STATE))
    _, _, xr, xi = lax.associative_scan(_complex_affine_combine, (a_re, a_im, bu_re, bu_im), axis=1)
    y = (jnp.einsum('ghp,blgp->blgh', c_re.astype(jnp.float32), xr)
         - jnp.einsum('ghp,blgp->blgh', c_im.astype(jnp.float32), xi))
    y = y.reshape(B, L, D_MODEL) + d_skip.astype(jnp.float32) * u
    hg = jax.nn.gelu(y).astype(xn.dtype) @ w_glu
    a, g = jnp.split(hg, 2, axis=-1)
    return a * jax.nn.sigmoid(g)


def causal_depthwise_conv(h, w, b):
    L = h.shape[1]
    hp = jnp.pad(h, ((0, 0), (CONV_K - 1, 0), (0, 0)))
    y = hp[:, 0:L] * w[0]
    for kk in range(1, CONV_K):
        y = y + hp[:, kk:kk + L] * w[kk]
    return y + b


def conv_gated_ffn(xn, w_up, conv_w, conv_b, w_down):
    h = causal_depthwise_conv(xn @ w_up, conv_w, conv_b)
    a, g = jnp.split(h, 2, axis=-1)
    return (jax.nn.silu(g) * a) @ w_down


def _fwd_setup_inputs(seed: int = 0) -> dict:
    key = jax.random.key(seed)
    ks = jax.random.split(key, 32)
    f32 = jnp.float32

    def nrm(k, shape, scale):
        return jax.random.normal(k, shape, f32) * scale

    x = nrm(ks[0], (BATCH, SEQ, D_MODEL), 1.0)
    norm_g = 1.0 + nrm(ks[1], (DEPTH, 2, D_MODEL), 0.05)
    final_norm_g = 1.0 + nrm(ks[2], (D_MODEL,), 0.05)
    sb_w_qkv = nrm(ks[3], (N_A, D_MODEL, 3 * D_MODEL), D_MODEL ** -0.5)
    sb_w_o = nrm(ks[4], (N_A, D_MODEL, D_MODEL), D_MODEL ** -0.5)
    sg_w_in = nrm(ks[5], (N_B, D_MODEL, 2 * SG_WIDTH), D_MODEL ** -0.5)
    sg_norm_g = 1.0 + nrm(ks[6], (N_B, SG_WIDTH), 0.05)
    sg_w_s = nrm(ks[7], (N_B, SG_GROUPS, CHUNK, CHUNK), CHUNK ** -0.5)
    sg_b = 1.0 + nrm(ks[8], (N_B, SG_GROUPS, CHUNK), 0.1)
    sg_w_o = nrm(ks[9], (N_B, SG_WIDTH, D_MODEL), SG_WIDTH ** -0.5)
    ssm_w_in = nrm(ks[10], (N_C, D_MODEL, D_MODEL), D_MODEL ** -0.5)
    ssm_lam_re = -0.5 + nrm(ks[11], (N_C, SSM_GROUPS, SSM_STATE), 0.01)
    ssm_lam_im = (jnp.pi * jnp.arange(SSM_STATE, dtype=f32)
                  + nrm(ks[12], (N_C, SSM_GROUPS, SSM_STATE), 0.01))
    ssm_log_dt = jax.random.uniform(ks[13], (N_C, SSM_GROUPS), f32,
                                    minval=math.log(DT_MIN), maxval=math.log(DT_MAX))
    ssm_b_re = nrm(ks[14], (N_C, SSM_GROUPS, SSM_STATE, SSM_GROUP), (2 * SSM_GROUP) ** -0.5)
    ssm_b_im = nrm(ks[15], (N_C, SSM_GROUPS, SSM_STATE, SSM_GROUP), (2 * SSM_GROUP) ** -0.5)
    ssm_c_re = nrm(ks[16], (N_C, SSM_GROUPS, SSM_GROUP, SSM_STATE), SSM_STATE ** -0.5)
    ssm_c_im = nrm(ks[17], (N_C, SSM_GROUPS, SSM_GROUP, SSM_STATE), SSM_STATE ** -0.5)
    ssm_d = nrm(ks[18], (N_C, D_MODEL), 1.0)
    ssm_w_glu = nrm(ks[19], (N_C, D_MODEL, 2 * D_MODEL), D_MODEL ** -0.5)
    ffn_w_up = nrm(ks[20], (DEPTH, D_MODEL, 2 * D_FF), D_MODEL ** -0.5)
    ffn_conv_w = nrm(ks[21], (DEPTH, CONV_K, 2 * D_FF), CONV_K ** -0.5)
    ffn_conv_b = nrm(ks[22], (DEPTH, 2 * D_FF), 0.01)
    ffn_w_down = nrm(ks[23], (DEPTH, D_FF, D_MODEL), D_FF ** -0.5)
    return {
        "x": x, "norm_g": norm_g, "final_norm_g": final_norm_g,
        "sb_w_qkv": sb_w_qkv, "sb_w_o": sb_w_o,
        "sg_w_in": sg_w_in, "sg_norm_g": sg_norm_g, "sg_w_s": sg_w_s, "sg_b": sg_b, "sg_w_o": sg_w_o,
        "ssm_w_in": ssm_w_in, "ssm_lam_re": ssm_lam_re, "ssm_lam_im": ssm_lam_im,
        "ssm_log_dt": ssm_log_dt, "ssm_b_re": ssm_b_re, "ssm_b_im": ssm_b_im,
        "ssm_c_re": ssm_c_re, "ssm_c_im": ssm_c_im, "ssm_d": ssm_d, "ssm_w_glu": ssm_w_glu,
        "ffn_w_up": ffn_w_up, "ffn_conv_w": ffn_conv_w, "ffn_conv_b": ffn_conv_b,
        "ffn_w_down": ffn_w_down,
    }


def _fwd_reference(x, norm_g, final_norm_g, sb_w_qkv, sb_w_o, sg_w_in, sg_norm_g, sg_w_s, sg_b, sg_w_o,
              ssm_w_in, ssm_lam_re, ssm_lam_im, ssm_log_dt, ssm_b_re, ssm_b_im, ssm_c_re, ssm_c_im,
              ssm_d, ssm_w_glu, ffn_w_up, ffn_conv_w, ffn_conv_b, ffn_w_down):
    for i in range(DEPTH):
        mixer = i % N_MIXERS
        j = i // N_MIXERS
        xn = rmsnorm(x, norm_g[i, 0])
        if mixer == 0:
            m = stick_breaking_attention(xn, sb_w_qkv[j], sb_w_o[j])
        elif mixer == 1:
            m = chunked_spatial_gating(xn, sg_w_in[j], sg_norm_g[j], sg_w_s[j], sg_b[j], sg_w_o[j])
        else:
            m = s5_layer(xn, ssm_w_in[j], ssm_lam_re[j], ssm_lam_im[j], ssm_log_dt[j],
                         ssm_b_re[j], ssm_b_im[j], ssm_c_re[j], ssm_c_im[j], ssm_d[j], ssm_w_glu[j])
        x = x + m.astype(x.dtype)
        f = conv_gated_ffn(rmsnorm(x, norm_g[i, 1]), ffn_w_up[i], ffn_conv_w[i], ffn_conv_b[i],
                           ffn_w_down[i])
        x = x + f.astype(x.dtype)
    return rmsnorm(x, final_norm_g)


import jax as _jax
import jax.numpy as _jnp

TWIN_FORMAT = 'train_step'
FWD_PARAMS = ['x', 'norm_g', 'final_norm_g', 'sb_w_qkv', 'sb_w_o', 'sg_w_in', 'sg_norm_g', 'sg_w_s', 'sg_b', 'sg_w_o', 'ssm_w_in', 'ssm_lam_re', 'ssm_lam_im', 'ssm_log_dt', 'ssm_b_re', 'ssm_b_im', 'ssm_c_re', 'ssm_c_im', 'ssm_d', 'ssm_w_glu', 'ffn_w_up', 'ffn_conv_w', 'ffn_conv_b', 'ffn_w_down']
TWIN_WEIGHTS = ['norm_g', 'final_norm_g', 'sb_w_qkv', 'sb_w_o', 'sg_w_in', 'sg_norm_g', 'sg_w_s', 'sg_b', 'sg_w_o', 'ssm_w_in', 'ssm_lam_re', 'ssm_lam_im', 'ssm_log_dt', 'ssm_b_re', 'ssm_b_im', 'ssm_c_re', 'ssm_c_im', 'ssm_d', 'ssm_w_glu', 'ffn_w_up', 'ffn_conv_w', 'ffn_conv_b', 'ffn_w_down']
TWIN_DIFF_INPUT = 'x'
TWIN_INPUTS = ['x', 'norm_g', 'final_norm_g', 'sb_w_qkv', 'sb_w_o', 'sg_w_in', 'sg_norm_g', 'sg_w_s', 'sg_b', 'sg_w_o', 'ssm_w_in', 'ssm_lam_re', 'ssm_lam_im', 'ssm_log_dt', 'ssm_b_re', 'ssm_b_im', 'ssm_c_re', 'ssm_c_im', 'ssm_d', 'ssm_w_glu', 'ffn_w_up', 'ffn_conv_w', 'ffn_conv_b', 'ffn_w_down', 'loss_target', 'm_norm_g', 'm_final_norm_g', 'm_sb_w_qkv', 'm_sb_w_o', 'm_sg_w_in', 'm_sg_norm_g', 'm_sg_w_s', 'm_sg_b', 'm_sg_w_o', 'm_ssm_w_in', 'm_ssm_lam_re', 'm_ssm_lam_im', 'm_ssm_log_dt', 'm_ssm_b_re', 'm_ssm_b_im', 'm_ssm_c_re', 'm_ssm_c_im', 'm_ssm_d', 'm_ssm_w_glu', 'm_ffn_w_up', 'm_ffn_conv_w', 'm_ffn_conv_b', 'm_ffn_w_down', 'v_norm_g', 'v_final_norm_g', 'v_sb_w_qkv', 'v_sb_w_o', 'v_sg_w_in', 'v_sg_norm_g', 'v_sg_w_s', 'v_sg_b', 'v_sg_w_o', 'v_ssm_w_in', 'v_ssm_lam_re', 'v_ssm_lam_im', 'v_ssm_log_dt', 'v_ssm_b_re', 'v_ssm_b_im', 'v_ssm_c_re', 'v_ssm_c_im', 'v_ssm_d', 'v_ssm_w_glu', 'v_ffn_w_up', 'v_ffn_conv_w', 'v_ffn_conv_b', 'v_ffn_w_down']
TWIN_OUTPUTS = ['loss', 'grad_x', 'grad_norm_g', 'grad_final_norm_g', 'grad_sb_w_qkv', 'grad_sb_w_o', 'grad_sg_w_in', 'grad_sg_norm_g', 'grad_sg_w_s', 'grad_sg_b', 'grad_sg_w_o', 'grad_ssm_w_in', 'grad_ssm_lam_re', 'grad_ssm_lam_im', 'grad_ssm_log_dt', 'grad_ssm_b_re', 'grad_ssm_b_im', 'grad_ssm_c_re', 'grad_ssm_c_im', 'grad_ssm_d', 'grad_ssm_w_glu', 'grad_ffn_w_up', 'grad_ffn_conv_w', 'grad_ffn_conv_b', 'grad_ffn_w_down', 'delta_norm_g', 'delta_final_norm_g', 'delta_sb_w_qkv', 'delta_sb_w_o', 'delta_sg_w_in', 'delta_sg_norm_g', 'delta_sg_w_s', 'delta_sg_b', 'delta_sg_w_o', 'delta_ssm_w_in', 'delta_ssm_lam_re', 'delta_ssm_lam_im', 'delta_ssm_log_dt', 'delta_ssm_b_re', 'delta_ssm_b_im', 'delta_ssm_c_re', 'delta_ssm_c_im', 'delta_ssm_d', 'delta_ssm_w_glu', 'delta_ffn_w_up', 'delta_ffn_conv_w', 'delta_ffn_conv_b', 'delta_ffn_w_down', 'new_m_norm_g', 'new_m_final_norm_g', 'new_m_sb_w_qkv', 'new_m_sb_w_o', 'new_m_sg_w_in', 'new_m_sg_norm_g', 'new_m_sg_w_s', 'new_m_sg_b', 'new_m_sg_w_o', 'new_m_ssm_w_in', 'new_m_ssm_lam_re', 'new_m_ssm_lam_im', 'new_m_ssm_log_dt', 'new_m_ssm_b_re', 'new_m_ssm_b_im', 'new_m_ssm_c_re', 'new_m_ssm_c_im', 'new_m_ssm_d', 'new_m_ssm_w_glu', 'new_m_ffn_w_up', 'new_m_ffn_conv_w', 'new_m_ffn_conv_b', 'new_m_ffn_w_down', 'new_v_norm_g', 'new_v_final_norm_g', 'new_v_sb_w_qkv', 'new_v_sb_w_o', 'new_v_sg_w_in', 'new_v_sg_norm_g', 'new_v_sg_w_s', 'new_v_sg_b', 'new_v_sg_w_o', 'new_v_ssm_w_in', 'new_v_ssm_lam_re', 'new_v_ssm_lam_im', 'new_v_ssm_log_dt', 'new_v_ssm_b_re', 'new_v_ssm_b_im', 'new_v_ssm_c_re', 'new_v_ssm_c_im', 'new_v_ssm_d', 'new_v_ssm_w_glu', 'new_v_ffn_w_up', 'new_v_ffn_conv_w', 'new_v_ffn_conv_b', 'new_v_ffn_w_down']
TWIN_LEAF_KINDS = {'loss': 'loss', 'grad_x': 'grad_x', 'grad_norm_g': 'grad_w', 'grad_final_norm_g': 'grad_w', 'grad_sb_w_qkv': 'grad_w', 'grad_sb_w_o': 'grad_w', 'grad_sg_w_in': 'grad_w', 'grad_sg_norm_g': 'grad_w', 'grad_sg_w_s': 'grad_w', 'grad_sg_b': 'grad_w', 'grad_sg_w_o': 'grad_w', 'grad_ssm_w_in': 'grad_w', 'grad_ssm_lam_re': 'grad_w', 'grad_ssm_lam_im': 'grad_w', 'grad_ssm_log_dt': 'grad_w', 'grad_ssm_b_re': 'grad_w', 'grad_ssm_b_im': 'grad_w', 'grad_ssm_c_re': 'grad_w', 'grad_ssm_c_im': 'grad_w', 'grad_ssm_d': 'grad_w', 'grad_ssm_w_glu': 'grad_w', 'grad_ffn_w_up': 'grad_w', 'grad_ffn_conv_w': 'grad_w', 'grad_ffn_conv_b': 'grad_w', 'grad_ffn_w_down': 'grad_w', 'delta_norm_g': 'delta_w', 'delta_final_norm_g': 'delta_w', 'delta_sb_w_qkv': 'delta_w', 'delta_sb_w_o': 'delta_w', 'delta_sg_w_in': 'delta_w', 'delta_sg_norm_g': 'delta_w', 'delta_sg_w_s': 'delta_w', 'delta_sg_b': 'delta_w', 'delta_sg_w_o': 'delta_w', 'delta_ssm_w_in': 'delta_w', 'delta_ssm_lam_re': 'delta_w', 'delta_ssm_lam_im': 'delta_w', 'delta_ssm_log_dt': 'delta_w', 'delta_ssm_b_re': 'delta_w', 'delta_ssm_b_im': 'delta_w', 'delta_ssm_c_re': 'delta_w', 'delta_ssm_c_im': 'delta_w', 'delta_ssm_d': 'delta_w', 'delta_ssm_w_glu': 'delta_w', 'delta_ffn_w_up': 'delta_w', 'delta_ffn_conv_w': 'delta_w', 'delta_ffn_conv_b': 'delta_w', 'delta_ffn_w_down': 'delta_w', 'new_m_norm_g': 'new_m', 'new_m_final_norm_g': 'new_m', 'new_m_sb_w_qkv': 'new_m', 'new_m_sb_w_o': 'new_m', 'new_m_sg_w_in': 'new_m', 'new_m_sg_norm_g': 'new_m', 'new_m_sg_w_s': 'new_m', 'new_m_sg_b': 'new_m', 'new_m_sg_w_o': 'new_m', 'new_m_ssm_w_in': 'new_m', 'new_m_ssm_lam_re': 'new_m', 'new_m_ssm_lam_im': 'new_m', 'new_m_ssm_log_dt': 'new_m', 'new_m_ssm_b_re': 'new_m', 'new_m_ssm_b_im': 'new_m', 'new_m_ssm_c_re': 'new_m', 'new_m_ssm_c_im': 'new_m', 'new_m_ssm_d': 'new_m', 'new_m_ssm_w_glu': 'new_m', 'new_m_ffn_w_up': 'new_m', 'new_m_ffn_conv_w': 'new_m', 'new_m_ffn_conv_b': 'new_m', 'new_m_ffn_w_down': 'new_m', 'new_v_norm_g': 'new_v', 'new_v_final_norm_g': 'new_v', 'new_v_sb_w_qkv': 'new_v', 'new_v_sb_w_o': 'new_v', 'new_v_sg_w_in': 'new_v', 'new_v_sg_norm_g': 'new_v', 'new_v_sg_w_s': 'new_v', 'new_v_sg_b': 'new_v', 'new_v_sg_w_o': 'new_v', 'new_v_ssm_w_in': 'new_v', 'new_v_ssm_lam_re': 'new_v', 'new_v_ssm_lam_im': 'new_v', 'new_v_ssm_log_dt': 'new_v', 'new_v_ssm_b_re': 'new_v', 'new_v_ssm_b_im': 'new_v', 'new_v_ssm_c_re': 'new_v', 'new_v_ssm_c_im': 'new_v', 'new_v_ssm_d': 'new_v', 'new_v_ssm_w_glu': 'new_v', 'new_v_ffn_w_up': 'new_v', 'new_v_ffn_conv_w': 'new_v', 'new_v_ffn_conv_b': 'new_v', 'new_v_ffn_w_down': 'new_v'}


def _forward(args):
    return _fwd_reference(*[args[k] for k in FWD_PARAMS])


def _output_shape():
    out = _jax.eval_shape(lambda: _forward(_fwd_setup_inputs(0)))
    return out.shape, out.dtype

N_MICROBATCH = 1
ADAM_LR = 0.001
ADAM_B1 = 0.9
ADAM_B2 = 0.999
ADAM_EPS = 1e-08
ADAM_WD = 0.01
ADAM_STEP = 10
PER_EXAMPLE_BATCH_AXIS = {'x': 0, 'loss_target': 0}
SHARED_INPUTS = []
_WEIGHT_DTYPES = {'norm_g': _jnp.float32, 'final_norm_g': _jnp.float32, 'sb_w_qkv': _jnp.float32, 'sb_w_o': _jnp.float32, 'sg_w_in': _jnp.float32, 'sg_norm_g': _jnp.float32, 'sg_w_s': _jnp.float32, 'sg_b': _jnp.float32, 'sg_w_o': _jnp.float32, 'ssm_w_in': _jnp.float32, 'ssm_lam_re': _jnp.float32, 'ssm_lam_im': _jnp.float32, 'ssm_log_dt': _jnp.float32, 'ssm_b_re': _jnp.float32, 'ssm_b_im': _jnp.float32, 'ssm_c_re': _jnp.float32, 'ssm_c_im': _jnp.float32, 'ssm_d': _jnp.float32, 'ssm_w_glu': _jnp.float32, 'ffn_w_up': _jnp.float32, 'ffn_conv_w': _jnp.float32, 'ffn_conv_b': _jnp.float32, 'ffn_w_down': _jnp.float32}
MOMENT_SCALE = {'norm_g': 8.018008e-02, 'final_norm_g': 1.604787e+01, 'sb_w_qkv': 5.293813e-02, 'sb_w_o': 7.798609e-02, 'sg_w_in': 6.582694e-02, 'sg_norm_g': 4.722990e-02, 'sg_w_s': 4.435285e-02, 'sg_b': 6.536365e-02, 'sg_w_o': 8.678356e-02, 'ssm_w_in': 3.153526e-02, 'ssm_lam_re': 3.043837e-03, 'ssm_lam_im': 2.824427e-03, 'ssm_log_dt': 1.041015e+00, 'ssm_b_re': 1.753805e-03, 'ssm_b_im': 1.720385e-03, 'ssm_c_re': 2.392078e-03, 'ssm_c_im': 2.494300e-03, 'ssm_d': 3.559813e-02, 'ssm_w_glu': 2.696672e-02, 'ffn_w_up': 3.349699e-02, 'ffn_conv_w': 3.299278e-02, 'ffn_conv_b': 3.611744e-02, 'ffn_w_down': 5.486059e-02}


def _to_microbatches(a, axis):
    t = _jnp.moveaxis(a, axis, 0)
    t = t.reshape((N_MICROBATCH, t.shape[0] // N_MICROBATCH) + t.shape[1:])
    return _jnp.moveaxis(t, 1, axis + 1)


def setup_inputs(seed: int = 0) -> dict:
    inp = _fwd_setup_inputs(seed)
    key = _jax.random.fold_in(_jax.random.key(seed), 7919)
    shape, _ = _output_shape()
    out = dict(inp)
    out["loss_target"] = _jax.random.normal(_jax.random.fold_in(key, 0), shape, _jnp.float32)
    for i, name in enumerate(TWIN_WEIGHTS):
        w = inp[name].astype(_jnp.float32)
        if MOMENT_SCALE is None:
            s = _jnp.sqrt(_jnp.mean(_jnp.square(w)) + 1e-30)
        else:
            s = MOMENT_SCALE[name]
        km, kv = _jax.random.split(_jax.random.fold_in(key, i + 1))
        out[name] = w
        out["m_" + name] = s * _jax.random.normal(km, w.shape, _jnp.float32)
        out["v_" + name] = (s * s) * _jax.random.uniform(kv, w.shape, _jnp.float32, 0.5, 1.5)
    if N_MICROBATCH > 1:
        for name, axis in PER_EXAMPLE_BATCH_AXIS.items():
            out[name] = _to_microbatches(out[name], axis)
    return {'x': out['x'], 'norm_g': out['norm_g'], 'final_norm_g': out['final_norm_g'], 'sb_w_qkv': out['sb_w_qkv'], 'sb_w_o': out['sb_w_o'], 'sg_w_in': out['sg_w_in'], 'sg_norm_g': out['sg_norm_g'], 'sg_w_s': out['sg_w_s'], 'sg_b': out['sg_b'], 'sg_w_o': out['sg_w_o'], 'ssm_w_in': out['ssm_w_in'], 'ssm_lam_re': out['ssm_lam_re'], 'ssm_lam_im': out['ssm_lam_im'], 'ssm_log_dt': out['ssm_log_dt'], 'ssm_b_re': out['ssm_b_re'], 'ssm_b_im': out['ssm_b_im'], 'ssm_c_re': out['ssm_c_re'], 'ssm_c_im': out['ssm_c_im'], 'ssm_d': out['ssm_d'], 'ssm_w_glu': out['ssm_w_glu'], 'ffn_w_up': out['ffn_w_up'], 'ffn_conv_w': out['ffn_conv_w'], 'ffn_conv_b': out['ffn_conv_b'], 'ffn_w_down': out['ffn_w_down'], 'loss_target': out['loss_target'], 'm_norm_g': out['m_norm_g'], 'm_final_norm_g': out['m_final_norm_g'], 'm_sb_w_qkv': out['m_sb_w_qkv'], 'm_sb_w_o': out['m_sb_w_o'], 'm_sg_w_in': out['m_sg_w_in'], 'm_sg_norm_g': out['m_sg_norm_g'], 'm_sg_w_s': out['m_sg_w_s'], 'm_sg_b': out['m_sg_b'], 'm_sg_w_o': out['m_sg_w_o'], 'm_ssm_w_in': out['m_ssm_w_in'], 'm_ssm_lam_re': out['m_ssm_lam_re'], 'm_ssm_lam_im': out['m_ssm_lam_im'], 'm_ssm_log_dt': out['m_ssm_log_dt'], 'm_ssm_b_re': out['m_ssm_b_re'], 'm_ssm_b_im': out['m_ssm_b_im'], 'm_ssm_c_re': out['m_ssm_c_re'], 'm_ssm_c_im': out['m_ssm_c_im'], 'm_ssm_d': out['m_ssm_d'], 'm_ssm_w_glu': out['m_ssm_w_glu'], 'm_ffn_w_up': out['m_ffn_w_up'], 'm_ffn_conv_w': out['m_ffn_conv_w'], 'm_ffn_conv_b': out['m_ffn_conv_b'], 'm_ffn_w_down': out['m_ffn_w_down'], 'v_norm_g': out['v_norm_g'], 'v_final_norm_g': out['v_final_norm_g'], 'v_sb_w_qkv': out['v_sb_w_qkv'], 'v_sb_w_o': out['v_sb_w_o'], 'v_sg_w_in': out['v_sg_w_in'], 'v_sg_norm_g': out['v_sg_norm_g'], 'v_sg_w_s': out['v_sg_w_s'], 'v_sg_b': out['v_sg_b'], 'v_sg_w_o': out['v_sg_w_o'], 'v_ssm_w_in': out['v_ssm_w_in'], 'v_ssm_lam_re': out['v_ssm_lam_re'], 'v_ssm_lam_im': out['v_ssm_lam_im'], 'v_ssm_log_dt': out['v_ssm_log_dt'], 'v_ssm_b_re': out['v_ssm_b_re'], 'v_ssm_b_im': out['v_ssm_b_im'], 'v_ssm_c_re': out['v_ssm_c_re'], 'v_ssm_c_im': out['v_ssm_c_im'], 'v_ssm_d': out['v_ssm_d'], 'v_ssm_w_glu': out['v_ssm_w_glu'], 'v_ffn_w_up': out['v_ffn_w_up'], 'v_ffn_conv_w': out['v_ffn_conv_w'], 'v_ffn_conv_b': out['v_ffn_conv_b'], 'v_ffn_w_down': out['v_ffn_w_down']}


def _loss(weights, diff, rest, loss_target):
    with _jax.named_scope("forward"):
        args = {**rest, TWIN_DIFF_INPUT: diff, **{k: w.astype(_WEIGHT_DTYPES[k]) for k, w in weights.items()}}
        y = _forward(args)
    with _jax.named_scope("loss_head"):
        err = _jnp.square(y.astype(_jnp.float32) - loss_target)
        return 0.5 * _jnp.sum(_jnp.mean(err, axis=-1)) if err.ndim else 0.5 * err


def _adamw(w, g, m, v):
    m = ADAM_B1 * m + (1.0 - ADAM_B1) * g
    v = ADAM_B2 * v + (1.0 - ADAM_B2) * _jnp.square(g)
    m_hat = m / (1.0 - ADAM_B1 ** ADAM_STEP)
    v_hat = v / (1.0 - ADAM_B2 ** ADAM_STEP)
    delta = -ADAM_LR * (m_hat / (_jnp.sqrt(v_hat) + ADAM_EPS) + ADAM_WD * w)
    return delta, m, v


def reference(x, norm_g, final_norm_g, sb_w_qkv, sb_w_o, sg_w_in, sg_norm_g, sg_w_s, sg_b, sg_w_o, ssm_w_in, ssm_lam_re, ssm_lam_im, ssm_log_dt, ssm_b_re, ssm_b_im, ssm_c_re, ssm_c_im, ssm_d, ssm_w_glu, ffn_w_up, ffn_conv_w, ffn_conv_b, ffn_w_down, loss_target, m_norm_g, m_final_norm_g, m_sb_w_qkv, m_sb_w_o, m_sg_w_in, m_sg_norm_g, m_sg_w_s, m_sg_b, m_sg_w_o, m_ssm_w_in, m_ssm_lam_re, m_ssm_lam_im, m_ssm_log_dt, m_ssm_b_re, m_ssm_b_im, m_ssm_c_re, m_ssm_c_im, m_ssm_d, m_ssm_w_glu, m_ffn_w_up, m_ffn_conv_w, m_ffn_conv_b, m_ffn_w_down, v_norm_g, v_final_norm_g, v_sb_w_qkv, v_sb_w_o, v_sg_w_in, v_sg_norm_g, v_sg_w_s, v_sg_b, v_sg_w_o, v_ssm_w_in, v_ssm_lam_re, v_ssm_lam_im, v_ssm_log_dt, v_ssm_b_re, v_ssm_b_im, v_ssm_c_re, v_ssm_c_im, v_ssm_d, v_ssm_w_glu, v_ffn_w_up, v_ffn_conv_w, v_ffn_conv_b, v_ffn_w_down):
    given = dict(x=x, norm_g=norm_g, final_norm_g=final_norm_g, sb_w_qkv=sb_w_qkv, sb_w_o=sb_w_o, sg_w_in=sg_w_in, sg_norm_g=sg_norm_g, sg_w_s=sg_w_s, sg_b=sg_b, sg_w_o=sg_w_o, ssm_w_in=ssm_w_in, ssm_lam_re=ssm_lam_re, ssm_lam_im=ssm_lam_im, ssm_log_dt=ssm_log_dt, ssm_b_re=ssm_b_re, ssm_b_im=ssm_b_im, ssm_c_re=ssm_c_re, ssm_c_im=ssm_c_im, ssm_d=ssm_d, ssm_w_glu=ssm_w_glu, ffn_w_up=ffn_w_up, ffn_conv_w=ffn_conv_w, ffn_conv_b=ffn_conv_b, ffn_w_down=ffn_w_down, loss_target=loss_target, m_norm_g=m_norm_g, m_final_norm_g=m_final_norm_g, m_sb_w_qkv=m_sb_w_qkv, m_sb_w_o=m_sb_w_o, m_sg_w_in=m_sg_w_in, m_sg_norm_g=m_sg_norm_g, m_sg_w_s=m_sg_w_s, m_sg_b=m_sg_b, m_sg_w_o=m_sg_w_o, m_ssm_w_in=m_ssm_w_in, m_ssm_lam_re=m_ssm_lam_re, m_ssm_lam_im=m_ssm_lam_im, m_ssm_log_dt=m_ssm_log_dt, m_ssm_b_re=m_ssm_b_re, m_ssm_b_im=m_ssm_b_im, m_ssm_c_re=m_ssm_c_re, m_ssm_c_im=m_ssm_c_im, m_ssm_d=m_ssm_d, m_ssm_w_glu=m_ssm_w_glu, m_ffn_w_up=m_ffn_w_up, m_ffn_conv_w=m_ffn_conv_w, m_ffn_conv_b=m_ffn_conv_b, m_ffn_w_down=m_ffn_w_down, v_norm_g=v_norm_g, v_final_norm_g=v_final_norm_g, v_sb_w_qkv=v_sb_w_qkv, v_sb_w_o=v_sb_w_o, v_sg_w_in=v_sg_w_in, v_sg_norm_g=v_sg_norm_g, v_sg_w_s=v_sg_w_s, v_sg_b=v_sg_b, v_sg_w_o=v_sg_w_o, v_ssm_w_in=v_ssm_w_in, v_ssm_lam_re=v_ssm_lam_re, v_ssm_lam_im=v_ssm_lam_im, v_ssm_log_dt=v_ssm_log_dt, v_ssm_b_re=v_ssm_b_re, v_ssm_b_im=v_ssm_b_im, v_ssm_c_re=v_ssm_c_re, v_ssm_c_im=v_ssm_c_im, v_ssm_d=v_ssm_d, v_ssm_w_glu=v_ssm_w_glu, v_ffn_w_up=v_ffn_w_up, v_ffn_conv_w=v_ffn_conv_w, v_ffn_conv_b=v_ffn_conv_b, v_ffn_w_down=v_ffn_w_down)
    weights = {n: given[n] for n in TWIN_WEIGHTS}
    shared = {n: given[n] for n in SHARED_INPUTS}
    per_example = {n: given[n] for n in ['x']}
    grad_fn = _jax.value_and_grad(_loss, argnums=(0, 1))

    def one_microbatch(ex, loss_target):
        ex = dict(ex)
        diff = ex.pop(TWIN_DIFF_INPUT)
        return grad_fn(weights, diff, {**shared, **ex}, loss_target)

    if N_MICROBATCH == 1:
        loss, (grad_w, grad_x) = one_microbatch(per_example, given["loss_target"])
    else:
        def body(carry, xs):
            loss_sum, grad_sum = carry
            l_k, (gw_k, gx_k) = one_microbatch(xs[0], xs[1])
            with _jax.named_scope("update"):
                return (loss_sum + l_k, _jax.tree.map(_jnp.add, grad_sum, gw_k)), gx_k

        init = (_jnp.zeros((), _jnp.float32), _jax.tree.map(_jnp.zeros_like, weights))
        (loss, grad_w), grad_x = _jax.lax.scan(body, init, (per_example, given["loss_target"]))
    with _jax.named_scope("update"):
        delta_w, new_m, new_v = {}, {}, {}
        for n in TWIN_WEIGHTS:
            delta_w[n], new_m[n], new_v[n] = _adamw(weights[n], grad_w[n], given["m_" + n], given["v_" + n])
    return (loss, grad_x, *[grad_w[n] for n in TWIN_WEIGHTS], *[delta_w[n] for n in TWIN_WEIGHTS],
            *[new_m[n] for n in TWIN_WEIGHTS], *[new_v[n] for n in TWIN_WEIGHTS])
```

```python
import functools

import jax
import jax.numpy as jnp
from jax import lax
from jax.experimental import pallas as pl
from jax.experimental.pallas import tpu as pltpu

F32, BF16 = jnp.float32, jnp.bfloat16
MESH_AXES = ("x", "y", "c")
NDEV = 8
EPS = 1e-6
HEAD_DIM = 64
LANES = 128
ATT_BLOCK = 128
CHUNK = 128
SG_GROUPS = 8
SSM_GROUPS, SSM_STATE, SSM_GROUP = 64, 64, 16
SSM_PACK = 8
CONV_K = 3
HALO = 16
SCAN_COLS = 256
ADAM_LR, ADAM_B1, ADAM_B2, ADAM_EPS, ADAM_WD, ADAM_STEP = 0.001, 0.9, 0.999, 1e-08, 0.01, 10
VMEM_LIMIT = 56 * 1024 * 1024

_MM = (((1,), (0,)), ((), ()))
_MM_TB = (((1,), (1,)), ((), ()))
_MM_TA = (((0,), (0,)), ((), ()))


def _params(sem):
    return pltpu.CompilerParams(dimension_semantics=sem, vmem_limit_bytes=VMEM_LIMIT)


def _rows(total, cap, mult=16):
    best = None
    for d in range(mult, min(total, cap) + 1, mult):
        if total % d == 0:
            best = d
    return best if best is not None else total


def _dot(a, b, dims, passes):
    if passes == 1:
        return lax.dot_general(a.astype(BF16), b.astype(BF16), dims, preferred_element_type=F32)
    a = a.astype(F32)
    b = b.astype(F32)
    ah = a.astype(BF16)
    bh = b.astype(BF16)
    al = (a - ah.astype(F32)).astype(BF16)
    bl = (b - bh.astype(F32)).astype(BF16)
    out = lax.dot_general(ah, bh, dims, preferred_element_type=F32)
    out = out + lax.dot_general(al, bh, dims, preferred_element_type=F32)
    return out + lax.dot_general(ah, bl, dims, preferred_element_type=F32)


def _mm(a, b, *, grid, a_blk, a_map, b_blk, b_map, o_blk, o_map, out_shape, out_dtype, name,
        dims=_MM, passes=1, res=None, res_blk=None, res_map=None, b_2d=None, acc_2d=None):
    nk = grid[2]
    has_res = res is not None

    def body(*refs):
        if has_res:
            a_ref, b_ref, r_ref, o_ref = refs[:4]
        else:
            a_ref, b_ref, o_ref = refs[:3]
        bv = b_ref[...]
        if b_2d is not None:
            bv = bv.reshape(b_2d)
        part = _dot(a_ref[...], bv, dims, passes)

        def finish(total):
            if has_res:
                total = total + r_ref[...].astype(F32)
            o_ref[...] = total.reshape(o_ref.shape).astype(o_ref.dtype)

        if nk == 1:
            finish(part)
        else:
            acc_ref = refs[-1]
            k = pl.program_id(2)

            @pl.when(k == 0)
            def _():
                acc_ref[...] = part

            @pl.when(k > 0)
            def _():
                acc_ref[...] += part

            @pl.when(k == nk - 1)
            def _():
                finish(acc_ref[...])

    in_specs = [pl.BlockSpec(a_blk, a_map), pl.BlockSpec(b_blk, b_map)]
    args = [a, b]
    if has_res:
        in_specs.append(pl.BlockSpec(res_blk, res_map))
        args.append(res)
    scratch = [pltpu.VMEM(acc_2d, F32)] if nk > 1 else []
    return pl.pallas_call(
        body, grid=grid, in_specs=in_specs, out_specs=pl.BlockSpec(o_blk, o_map),
        out_shape=jax.ShapeDtypeStruct(out_shape, out_dtype), scratch_shapes=scratch, name=name,
        compiler_params=_params(("parallel", "parallel", "arbitrary")),
    )(*args)


def _cs_act_spec(ns, tm, row_of, col_of):
    if ns % LANES == 0:
        return (tm, ns), lambda *g: (row_of(*g), col_of(*g))
    return (None, tm, ns), lambda *g: (col_of(*g), row_of(*g), 0)


def mm_cs_fwd(a, w4, l, out_dtype, name, tm=512):
    m, k = a.shape
    ns = w4.shape[3]
    o_blk, o_map = _cs_act_spec(ns, tm, lambda j, i, kk: i, lambda j, i, kk: j)
    out_shape = (m, NDEV * ns) if ns % LANES == 0 else (NDEV, m, ns)
    return _mm(a, w4, grid=(NDEV, m // tm, 1), a_blk=(tm, k), a_map=lambda j, i, kk: (i, 0),
               b_blk=(None, None, k, ns), b_map=lambda j, i, kk: (j, l, 0, 0),
               o_blk=o_blk, o_map=o_map, out_shape=out_shape, out_dtype=out_dtype, name=name)


def mm_cs_da(dc, w4, l, m, name, tm=512):
    k, ns = w4.shape[2], w4.shape[3]
    a_blk, a_map = _cs_act_spec(ns, tm, lambda i, _, j: i, lambda i, _, j: j)
    return _mm(dc, w4, grid=(m // tm, 1, NDEV), a_blk=a_blk, a_map=a_map,
               b_blk=(None, None, k, ns), b_map=lambda i, _, j: (j, l, 0, 0),
               o_blk=(tm, k), o_map=lambda i, _, j: (i, 0), out_shape=(m, k), out_dtype=F32,
               dims=_MM_TB, acc_2d=(tm, k), name=name)


def mm_cs_dw(a, dc, ns, name, tk=512):
    m, k = a.shape
    b_blk, b_map = _cs_act_spec(ns, tk, lambda j, _, kk: kk, lambda j, _, kk: j)
    return _mm(a, dc, grid=(NDEV, 1, m // tk), a_blk=(tk, k), a_map=lambda j, _, kk: (kk, 0),
               b_blk=b_blk, b_map=b_map, o_blk=(None, k, ns), o_map=lambda j, _, kk: (j, 0, 0),
               out_shape=(NDEV, k, ns), out_dtype=BF16, dims=_MM_TA, acc_2d=(k, ns), name=name)


def mm_rs_fwd(a, w4, l, res, out_dtype, name, tm=512):
    m, k = a.shape
    ks, n = w4.shape[2], w4.shape[3]
    return _mm(a, w4, grid=(m // tm, 1, 1), a_blk=(tm, k), a_map=lambda i, _, kk: (i, 0),
               b_blk=(NDEV, None, ks, n), b_map=lambda i, _, kk: (0, l, 0, 0), b_2d=(k, n),
               o_blk=(tm, n), o_map=lambda i, _, kk: (i, 0), out_shape=(m, n), out_dtype=out_dtype,
               res=res, res_blk=(tm, n), res_map=lambda i, _, kk: (i, 0), name=name)


def mm_rs_da(dc, w4, l, out_dtype, name, tm=512):
    m, n = dc.shape
    ks = w4.shape[2]
    k = NDEV * ks
    return _mm(dc, w4, grid=(m // tm, 1, 1), a_blk=(tm, n), a_map=lambda i, _, kk: (i, 0),
               b_blk=(NDEV, None, ks, n), b_map=lambda i, _, kk: (0, l, 0, 0), b_2d=(k, n),
               o_blk=(tm, k), o_map=lambda i, _, kk: (i, 0), out_shape=(m, k), out_dtype=out_dtype,
               dims=_MM_TB, name=name)


def mm_rs_dw(a, dc, name, tk=512):
    m, k = a.shape
    n = dc.shape[1]
    ks = k // NDEV
    return _mm(a, dc, grid=(1, 1, m // tk), a_blk=(tk, k), a_map=lambda _, __, kk: (kk, 0),
               b_blk=(tk, n), b_map=lambda _, __, kk: (kk, 0),
               o_blk=(NDEV, ks, n), o_map=lambda _, __, kk: (0, 0, 0), out_shape=(NDEV, ks, n),
               out_dtype=BF16, dims=_MM_TA, acc_2d=(k, n), name=name)


def mm_down_fwd(a3, w4, l, res, name, tm=512):
    nj, m, kc = a3.shape
    ks, n = w4.shape[2], w4.shape[3]
    return _mm(a3, w4, grid=(m // tm, 1, nj), a_blk=(None, tm, kc), a_map=lambda i, _, j: (j, i, 0),
               b_blk=(2, None, ks, n), b_map=lambda i, _, j: (j, l, 0, 0), b_2d=(kc, n),
               o_blk=(tm, n), o_map=lambda i, _, j: (i, 0), out_shape=(m, n), out_dtype=F32,
               res=res, res_blk=(tm, n), res_map=lambda i, _, j: (i, 0), acc_2d=(tm, n), name=name)


def mm_down_da(dc, w4, l, name, tm=512):
    m, n = dc.shape
    ks = w4.shape[2]
    kc = 2 * ks
    nj = NDEV // 2
    return _mm(dc, w4, grid=(nj, m // tm, 1), a_blk=(tm, n), a_map=lambda j, i, _: (i, 0),
               b_blk=(2, None, ks, n), b_map=lambda j, i, _: (j, l, 0, 0), b_2d=(kc, n),
               o_blk=(None, tm, kc), o_map=lambda j, i, _: (j, i, 0), out_shape=(nj, m, kc),
               out_dtype=BF16, dims=_MM_TB, name=name)


def mm_down_dw(a3, dc, name, tk=512):
    nj, m, kc = a3.shape
    n = dc.shape[1]
    return _mm(a3, dc, grid=(nj, 1, m // tk), a_blk=(None, tk, kc), a_map=lambda j, _, kk: (j, kk, 0),
               b_blk=(tk, n), b_map=lambda j, _, kk: (kk, 0),
               o_blk=(2, kc // 2, n), o_map=lambda j, _, kk: (j, 0, 0), out_shape=(NDEV, kc // 2, n),
               out_dtype=BF16, dims=_MM_TA, acc_2d=(kc, n), name=name)


def mm_qkv_da(d3, w4, l, name, tm=512):
    _, m, d = d3.shape
    k, ns = w4.shape[2], w4.shape[3]
    per_arr, per_piece = d // LANES, ns // LANES
    ngroups = 3 * per_arr
    return _mm(d3, w4, grid=(m // tm, 1, ngroups),
               a_blk=(None, tm, LANES), a_map=lambda i, _, g: (g // per_arr, i, g % per_arr),
               b_blk=(None, None, k, LANES), b_map=lambda i, _, g: (g // per_piece, l, 0, g % per_piece),
               o_blk=(tm, k), o_map=lambda i, _, g: (i, 0), out_shape=(m, k), out_dtype=F32,
               dims=_MM_TB, acc_2d=(tm, k), name=name)


def mm_qkv_dw(a, d3, ns, name, tk=512):
    m, k = a.shape
    d = d3.shape[2]
    per_arr, per_piece = d // LANES, ns // LANES
    ngroups = 3 * per_arr
    return _mm(a, d3, grid=(ngroups, 1, m // tk), a_blk=(tk, k), a_map=lambda g, _, kk: (kk, 0),
               b_blk=(None, tk, LANES), b_map=lambda g, _, kk: (g // per_arr, kk, g % per_arr),
               o_blk=(None, k, LANES), o_map=lambda g, _, kk: (g // per_piece, 0, g % per_piece),
               out_shape=(NDEV, k, ns), out_dtype=BF16, dims=_MM_TA, acc_2d=(k, LANES), name=name)


def _rowwise(fn, ins, outs, *, tr, name, acc_outs=()):
    rows = next(a.shape[0] if kind == "row" else a.shape[1] for a, kind in ins if kind != "full")
    n_in, n_out = len(ins), len(outs)

    def body(*refs):
        vals = fn(*[r[...] for r in refs[:n_in]])
        if not isinstance(vals, (tuple, list)):
            vals = (vals,)
        for ref, val in zip(refs[n_in:n_in + n_out], vals[:n_out]):
            ref[...] = val.astype(ref.dtype)
        i = pl.program_id(0)
        for ref, val in zip(refs[n_in + n_out:], vals[n_out:]):
            val = val.astype(ref.dtype)

            @pl.when(i == 0)
            def _(ref=ref, val=val):
                ref[...] = val

            @pl.when(i > 0)
            def _(ref=ref, val=val):
                ref[...] += val

    in_specs = []
    for a, kind in ins:
        if kind == "row":
            in_specs.append(pl.BlockSpec((tr, a.shape[1]), lambda i: (i, 0)))
        elif kind == "row3":
            in_specs.append(pl.BlockSpec((a.shape[0], tr, a.shape[2]), lambda i: (0, i, 0)))
        else:
            in_specs.append(pl.BlockSpec(a.shape, lambda i, nd=a.ndim: (0,) * nd))
    out_specs = [pl.BlockSpec((tr, c), lambda i: (i, 0)) for c, _ in outs]
    out_specs += [pl.BlockSpec(s, lambda i, nd=len(s): (0,) * nd) for s, _ in acc_outs]
    out_shape = [jax.ShapeDtypeStruct((rows, c), dt) for c, dt in outs]
    out_shape += [jax.ShapeDtypeStruct(s, dt) for s, dt in acc_outs]
    res = pl.pallas_call(
        body, grid=(rows // tr,), in_specs=in_specs, out_specs=out_specs, out_shape=out_shape, name=name,
        compiler_params=_params(("arbitrary",) if acc_outs else ("parallel",)),
    )(*[a for a, _ in ins])
    return res


def _rms(x, g):
    return x * lax.rsqrt(jnp.mean(x * x, axis=-1, keepdims=True) + EPS) * g


def cast_bf16(w):
    w2 = w.reshape(-1, w.shape[-1])
    out, = _rowwise(lambda v: v, [(w2, "row")], [(w2.shape[1], BF16)], tr=_rows(w2.shape[0], 512), name="cast_bf16")
    return out.reshape(w.shape)


def rms_fwd(x, g, name):
    out, = _rowwise(_rms, [(x, "row"), (g, "full")], [(x.shape[1], BF16)], tr=256, name=name)
    return out


def rms_bwd(x, g, dy, dres, name):
    def fn(xv, gv, dyv, drv):
        _, vjp = jax.vjp(_rms, xv, gv)
        dx, dg = vjp(dyv.astype(F32))
        return drv + dx, dg

    d = x.shape[1]
    return _rowwise(fn, [(x, "row"), (g, "full"), (dy, "row"), (dres, "row")], [(d, F32)], tr=256, name=name,
                    acc_outs=[((1, d), F32)])


def loss_head(x, g, tgt):
    def f(xv, gv, tv):
        err = jnp.square(_rms(xv, gv) - tv)
        return 0.5 * jnp.sum(jnp.mean(err, axis=-1))

    def fn(xv, gv, tv):
        val, (dx, dg) = jax.value_and_grad(f, argnums=(0, 1))(xv, gv, tv)
        return dx, jnp.full((1, LANES), val, F32), dg

    d = x.shape[1]
    return _rowwise(fn, [(x, "row"), (g, "full"), (tgt, "row")], [(d, F32)], tr=256, name="loss_head",
                    acc_outs=[((1, LANES), F32), ((1, d), F32)])


def _glu(hg, x):
    half = hg.shape[1] // 2
    return x + hg[:, :half] * jax.nn.sigmoid(hg[:, half:])


def glu_fwd(hg, x):
    out, = _rowwise(lambda h, xv: _glu(h.astype(F32), xv), [(hg, "row"), (x, "row")], [(x.shape[1], F32)],
                    tr=256, name="glu_fwd")
    return out


def glu_bwd(hg, dx1):
    def fn(h, d):
        _, vjp = jax.vjp(lambda hv: _glu(hv, jnp.zeros_like(d)), h.astype(F32))
        return vjp(d)[0]

    out, = _rowwise(fn, [(hg, "row"), (dx1, "row")], [(hg.shape[1], BF16)], tr=256, name="glu_bwd")
    return out


def _s5_post(yc, u, d):
    return jax.nn.gelu(yc + d * u)


def s5_post_fwd(yc, u, d):
    out, = _rowwise(_s5_post, [(yc, "row"), (u, "row"), (d, "full")], [(yc.shape[1], BF16)], tr=256,
                    name="s5_post_fwd")
    return out


def s5_post_bwd(yc, u, d, dyg):
    def fn(ycv, uv, dv, g):
        _, vjp = jax.vjp(_s5_post, ycv, uv, dv)
        return vjp(g.astype(F32))

    dm = yc.shape[1]
    return _rowwise(fn, [(yc, "row"), (u, "row"), (d, "full"), (dyg, "row")], [(dm, F32), (dm, F32)], tr=256,
                    name="s5_post_bwd", acc_outs=[((1, dm), F32)])


def adamw(w, m, v, g_parts, name):
    def fn(wv, mv, vv, gp):
        g = gp[0].astype(F32)
        for p in range(1, gp.shape[0]):
            g = g + gp[p].astype(F32)
        m2 = ADAM_B1 * mv + (1.0 - ADAM_B1) * g
        v2 = ADAM_B2 * vv + (1.0 - ADAM_B2) * jnp.square(g)
        m_hat = m2 / (1.0 - ADAM_B1 ** ADAM_STEP)
        v_hat = v2 / (1.0 - ADAM_B2 ** ADAM_STEP)
        delta = -ADAM_LR * (m_hat / (jnp.sqrt(v_hat) + ADAM_EPS) + ADAM_WD * wv)
        return g, delta, m2, v2

    c = w.shape[1]
    return _rowwise(fn, [(w, "row"), (m, "row"), (v, "row"), (g_parts, "row3")], [(c, F32)] * 4,
                    tr=_rows(w.shape[0], 256), name=name)


def _conv_rows(cur, halo, p, first):
    r = cur.shape[0]
    ext = jnp.concatenate([jnp.where(first, 0.0, halo), cur], axis=0)
    s1 = pltpu.roll(ext, 1, 0)[HALO:]
    s2 = pltpu.roll(ext, 2, 0)[HALO:]
    return p[0:1] * s2 + p[1:2] * s1 + p[2:3] * cur + p[3:4], s1, s2


def ffn_gate_fwd(h3, p3, tr=256):
    _, t, c = h3.shape
    half = NDEV // 2

    def body(a_ref, ah_ref, g_ref, gh_ref, pa_ref, pg_ref, o_ref):
        first = pl.program_id(1) == 0
        ya, _, _ = _conv_rows(a_ref[...].astype(F32), ah_ref[...].astype(F32), pa_ref[...], first)
        yg, _, _ = _conv_rows(g_ref[...].astype(F32), gh_ref[...].astype(F32), pg_ref[...], first)
        o_ref[...] = (jax.nn.silu(yg) * ya).astype(o_ref.dtype)

    main = lambda off: pl.BlockSpec((None, tr, c), lambda j, i: (j + off, i, 0))
    halo = lambda off: pl.BlockSpec((None, HALO, c), lambda j, i: (j + off, jnp.maximum(i * (tr // HALO) - 1, 0), 0))
    par = lambda off: pl.BlockSpec((None, 8, c), lambda j, i: (j + off, 0, 0))
    return pl.pallas_call(
        body, grid=(half, t // tr),
        in_specs=[main(0), halo(0), main(half), halo(half), par(0), par(half)],
        out_specs=pl.BlockSpec((None, tr, c), lambda j, i: (j, i, 0)),
        out_shape=jax.ShapeDtypeStruct((half, t, c), BF16), name="ffn_gate_fwd",
        compiler_params=_params(("parallel", "parallel")),
    )(h3, h3, h3, h3, p3, p3)


def ffn_gate_bwd(h3, dgated3, p3, tr=256):
    _, t, c = h3.shape
    half = NDEV // 2

    def body(a_ref, ah_ref, g_ref, gh_ref, dg_ref, pa_ref, pg_ref, o_ref, dp_ref):
        j, i = pl.program_id(0), pl.program_id(1)
        first = i == 0
        a = a_ref[...].astype(F32)
        g = g_ref[...].astype(F32)
        ya, a1, a2 = _conv_rows(a, ah_ref[...].astype(F32), pa_ref[...], first)
        yg, g1, g2 = _conv_rows(g, gh_ref[...].astype(F32), pg_ref[...], first)
        d = dg_ref[...].astype(F32)
        sig = jax.nn.sigmoid(yg)
        d_ya = d * (yg * sig)
        d_yg = d * ya * (sig * (1.0 + yg * (1.0 - sig)))
        is_a = j < half
        dy = jnp.where(is_a, d_ya, d_yg)
        o_ref[...] = dy.astype(o_ref.dtype)
        dyr = dy.astype(o_ref.dtype).astype(F32)
        cur, s1, s2 = jnp.where(is_a, a, g), jnp.where(is_a, a1, g1), jnp.where(is_a, a2, g2)
        rows = [jnp.sum(dyr * s2, axis=0, keepdims=True), jnp.sum(dyr * s1, axis=0, keepdims=True),
                jnp.sum(dyr * cur, axis=0, keepdims=True), jnp.sum(dyr, axis=0, keepdims=True)]
        dp = jnp.concatenate(rows + [jnp.zeros((4, c), F32)], axis=0)

        @pl.when(first)
        def _():
            dp_ref[...] = dp

        @pl.when(i > 0)
        def _():
            dp_ref[...] += dp

    main = lambda f: pl.BlockSpec((None, tr, c), lambda j, i: (f(j), i, 0))
    halo = lambda f: pl.BlockSpec((None, HALO, c), lambda j, i: (f(j), jnp.maximum(i * (tr // HALO) - 1, 0), 0))
    par = lambda f: pl.BlockSpec((None, 8, c), lambda j, i: (f(j), 0, 0))
    fa = lambda j: j % half
    fg = lambda j: j % half + half
    return pl.pallas_call(
        body, grid=(NDEV, t // tr),
        in_specs=[main(fa), halo(fa), main(fg), halo(fg), main(fa), par(fa), par(fg)],
        out_specs=[pl.BlockSpec((None, tr, c), lambda j, i: (j, i, 0)), pl.BlockSpec((None, 8, c), lambda j, i: (j, 0, 0))],
        out_shape=[jax.ShapeDtypeStruct((NDEV, t, c), BF16), jax.ShapeDtypeStruct((NDEV, 8, c), F32)],
        name="ffn_gate_bwd", compiler_params=_params(("parallel", "arbitrary")),
    )(h3, h3, h3, h3, dgated3, p3, p3)


def ffn_conv_t(dy3, p3, tr=256):
    _, t, c = dy3.shape
    nblk = t // tr

    def body(d_ref, dh_ref, p_ref, o_ref):
        last = pl.program_id(1) == nblk - 1
        cur = d_ref[...].astype(F32)
        ext = jnp.concatenate([cur, jnp.where(last, 0.0, dh_ref[...].astype(F32))], axis=0)
        n = tr + HALO
        s1 = pltpu.roll(ext, n - 1, 0)[:tr]
        s2 = pltpu.roll(ext, n - 2, 0)[:tr]
        p = p_ref[...]
        o_ref[...] = (p[2:3] * cur + p[1:2] * s1 + p[0:1] * s2).astype(o_ref.dtype)

    return pl.pallas_call(
        body, grid=(NDEV, nblk),
        in_specs=[pl.BlockSpec((None, tr, c), lambda j, i: (j, i, 0)),
                  pl.BlockSpec((None, HALO, c), lambda j, i: (j, jnp.minimum((i + 1) * (tr // HALO), t // HALO - 1), 0)),
                  pl.BlockSpec((None, 8, c), lambda j, i: (j, 0, 0))],
        out_specs=pl.BlockSpec((None, tr, c), lambda j, i: (j, i, 0)),
        out_shape=jax.ShapeDtypeStruct((NDEV, t, c), BF16), name="ffn_conv_t",
        compiler_params=_params(("parallel", "parallel")),
    )(dy3, dy3, p3)


def _att_consts():
    bq = ATT_BLOCK
    lane = lax.broadcasted_iota(jnp.int32, (1, LANES), 1)
    heads = (lane < HEAD_DIM, lane >= HEAD_DIM)
    rr = lax.broadcasted_iota(jnp.int32, (bq, bq), 0)
    cc = lax.broadcasted_iota(jnp.int32, (bq, bq), 1)
    return heads, rr, cc


def _split_dot(x, tri, parts):
    out = None
    for _ in range(parts):
        piece = x.astype(BF16)
        x = x - piece.astype(F32)
        term = jnp.dot(piece, tri, preferred_element_type=F32)
        out = term if out is None else out + term
    return out


def _att_logits(qh, k):
    z = lax.dot_general(qh, k, _MM_TB, preferred_element_type=F32)
    lsp = jnp.minimum(z, 0.0) - jnp.log(1.0 + jnp.exp(-jnp.abs(z)))
    return lsp, lsp - z


def _per_head(heads, a, b):
    return jnp.where(heads[0], a, b)


def sb_attn_fwd(qkv):
    t, d3 = qkv.shape
    d = d3 // 3
    npair = d // LANES
    bq = ATT_BLOCK

    def body(q_ref, k_ref, v_ref, o_ref, lt_ref):
        heads, rr, cc = _att_consts()
        strict = cc < rr
        suffix = (rr > cc).astype(BF16)

        def q_loop(qb, _):
            q0 = pl.multiple_of(qb * bq, bq)
            q = q_ref[pl.ds(q0, bq), :] * 0.125
            qh = [jnp.where(hm, q, 0.0).astype(BF16) for hm in heads]

            def k_loop(i, carry):
                accs, runs = carry
                k0 = pl.multiple_of((qb - i) * bq, bq)
                k = k_ref[pl.ds(k0, bq), :]
                v = v_ref[pl.ds(k0, bq), :]
                valid = jnp.logical_or(i > 0, strict)
                new_accs, new_runs = [], []
                for h in range(2):
                    lsp, lraw = _att_logits(qh[h], k)
                    lm = jnp.where(valid, lraw, 0.0)
                    w = jnp.where(valid, jnp.exp(lsp + _split_dot(lm, suffix, 2) + runs[h]), 0.0)
                    new_accs.append(accs[h] + jnp.dot(w.astype(BF16), v, preferred_element_type=F32))
                    new_runs.append(runs[h] + jnp.sum(lm, axis=1, keepdims=True))
                return tuple(new_accs), tuple(new_runs)

            zero = (jnp.zeros((bq, LANES), F32),) * 2, (jnp.zeros((bq, 1), F32),) * 2
            accs, runs = lax.fori_loop(0, qb + 1, k_loop, zero)
            o_ref[pl.ds(q0, bq), :] = _per_head(heads, accs[0], accs[1])
            lt_ref[pl.ds(q0, bq), :] = _per_head(heads, runs[0], runs[1])
            return 0

        lax.fori_loop(0, t // bq, q_loop, 0)

    col = lambda off: pl.BlockSpec((t, LANES), lambda p: (0, p + off))
    return pl.pallas_call(
        body, grid=(npair,), in_specs=[col(0), col(npair), col(2 * npair)], out_specs=[col(0), col(0)],
        out_shape=[jax.ShapeDtypeStruct((t, d), F32)] * 2, name="sb_attn_fwd", compiler_params=_params(("parallel",)),
    )(qkv, qkv, qkv)


def sb_attn_bwd(qkv, ltot, do):
    t, d3 = qkv.shape
    d = d3 // 3
    npair = d // LANES
    bq = ATT_BLOCK

    def body(q_ref, k_ref, v_ref, lt_ref, do_ref, d_ref, dk_acc, dv_acc):
        heads, rr, cc = _att_consts()
        strict = cc < rr
        prefix_incl = (rr <= cc).astype(BF16)
        prefix_excl = (rr < cc).astype(BF16)
        dk_acc[...] = jnp.zeros_like(dk_acc)
        dv_acc[...] = jnp.zeros_like(dv_acc)

        def q_loop(qb, _):
            q0 = pl.multiple_of(qb * bq, bq)
            q = q_ref[pl.ds(q0, bq), :] * 0.125
            dout = do_ref[pl.ds(q0, bq), :]
            lt2 = lt_ref[pl.ds(q0, bq), :]
            qh = [jnp.where(hm, q, 0.0).astype(BF16) for hm in heads]
            doh = [jnp.where(hm, dout, 0.0).astype(BF16) for hm in heads]
            lt = [jnp.max(jnp.where(hm, lt2, -jnp.inf), axis=1, keepdims=True) for hm in heads]

            def k_loop(kb, carry):
                dqs, lruns, gruns = carry
                k0 = pl.multiple_of(kb * bq, bq)
                k = k_ref[pl.ds(k0, bq), :]
                v = v_ref[pl.ds(k0, bq), :]
                valid = jnp.logical_or(kb < qb, strict)
                new_dqs, new_lruns, new_gruns = [], [], []
                dk_blk = jnp.zeros((bq, LANES), F32)
                dv_blk = jnp.zeros((bq, LANES), F32)
                for h in range(2):
                    lsp, lraw = _att_logits(qh[h], k)
                    lm = jnp.where(valid, lraw, 0.0)
                    right = lt[h] - (lruns[h] + _split_dot(lm, prefix_incl, 3))
                    w = jnp.where(valid, jnp.exp(lsp + right), 0.0)
                    g = lax.dot_general(doh[h], v, _MM_TB, preferred_element_type=F32) * w
                    left = gruns[h] + _split_dot(g, prefix_excl, 2)
                    dz = jnp.where(valid, g * jnp.exp(lraw) - jnp.exp(lsp) * left, 0.0).astype(BF16)
                    kh = jnp.where(heads[h], k, 0.0).astype(BF16)
                    new_dqs.append(dqs[h] + jnp.dot(dz, kh, preferred_element_type=F32))
                    dk_blk = dk_blk + lax.dot_general(dz, qh[h], _MM_TA, preferred_element_type=F32)
                    dv_blk = dv_blk + lax.dot_general(w.astype(BF16), doh[h], _MM_TA, preferred_element_type=F32)
                    new_lruns.append(lruns[h] + jnp.sum(lm, axis=1, keepdims=True))
                    new_gruns.append(gruns[h] + jnp.sum(g, axis=1, keepdims=True))
                dk_acc[pl.ds(k0, bq), :] += dk_blk
                dv_acc[pl.ds(k0, bq), :] += dv_blk
                return tuple(new_dqs), tuple(new_lruns), tuple(new_gruns)

            col = (jnp.zeros((bq, 1), F32),) * 2
            zero = (jnp.zeros((bq, LANES), F32),) * 2, col, col
            dqs, _, _ = lax.fori_loop(0, qb + 1, k_loop, zero)
            d_ref[0, pl.ds(q0, bq), :] = ((dqs[0] + dqs[1]) * 0.125).astype(d_ref.dtype)
            return 0

        lax.fori_loop(0, t // bq, q_loop, 0)
        d_ref[1] = dk_acc[...].astype(d_ref.dtype)
        d_ref[2] = dv_acc[...].astype(d_ref.dtype)

    col = lambda off: pl.BlockSpec((t, LANES), lambda p: (0, p + off))
    return pl.pallas_call(
        body, grid=(npair,), in_specs=[col(0), col(npair), col(2 * npair), col(0), col(0)],
        out_specs=pl.BlockSpec((3, t, LANES), lambda p: (0, 0, p)),
        out_shape=jax.ShapeDtypeStruct((3, t, d), BF16),
        scratch_shapes=[pltpu.VMEM((t, LANES), F32), pltpu.VMEM((t, LANES), F32)],
        name="sb_attn_bwd", compiler_params=_params(("parallel",)),
    )(qkv, qkv, qkv, ltot, do)


def _sgu_parts(hin, g, ws_ref, bf_ref):
    width = hin.shape[1] // 2
    h = jax.nn.gelu(hin)
    u, v = h[:, :width], h[:, width:]
    r = lax.rsqrt(jnp.mean(v * v, axis=-1, keepdims=True) + EPS)
    vn = v * r * g
    rr = lax.broadcasted_iota(jnp.int32, (CHUNK, CHUNK), 0)
    cc = lax.broadcasted_iota(jnp.int32, (CHUNK, CHUNK), 1)
    causal = cc <= rr
    wcs = [jnp.where(causal, ws_ref[gi], 0.0).astype(BF16) for gi in range(SG_GROUPS)]
    sv = jnp.concatenate(
        [jnp.dot(wcs[gi], vn[:, gi * CHUNK:(gi + 1) * CHUNK].astype(BF16), preferred_element_type=F32) + bf_ref[gi]
         for gi in range(SG_GROUPS)], axis=1)
    return u, v, r, vn, wcs, sv, causal


def sgu_fwd(hin, g, ws, bfull):
    t, w2 = hin.shape
    width = w2 // 2

    def body(h_ref, g_ref, ws_ref, bf_ref, o_ref):
        u, _, _, _, _, sv, _ = _sgu_parts(h_ref[...].astype(F32), g_ref[...], ws_ref, bf_ref)
        o_ref[...] = (u * sv).astype(o_ref.dtype)

    full = lambda a: pl.BlockSpec(a.shape, lambda i, nd=a.ndim: (0,) * nd)
    return pl.pallas_call(
        body, grid=(t // CHUNK,), in_specs=[pl.BlockSpec((CHUNK, w2), lambda i: (i, 0)), full(g), full(ws), full(bfull)],
        out_specs=pl.BlockSpec((CHUNK, width), lambda i: (i, 0)), out_shape=jax.ShapeDtypeStruct((t, width), BF16),
        name="sgu_fwd", compiler_params=_params(("parallel",)),
    )(hin, g, ws, bfull)


def sgu_bwd(hin, dp, g, ws, bfull):
    t, w2 = hin.shape
    width = w2 // 2

    def body(h_ref, dp_ref, g_ref, ws_ref, bf_ref, dh_ref, dws_ref, dbf_ref, dg_ref):
        i = pl.program_id(0)
        hin_v = h_ref[...].astype(F32)
        gv = g_ref[...]
        u, v, r, vn, wcs, sv, causal = _sgu_parts(hin_v, gv, ws_ref, bf_ref)
        dpv = dp_ref[...].astype(F32)
        du = dpv * sv
        dsv = dpv * u
        dvn_parts, dws_parts, dbf_parts = [], [], []
        for gi in range(SG_GROUPS):
            dsv_g = dsv[:, gi * CHUNK:(gi + 1) * CHUNK]
            dsv_b = dsv_g.astype(BF16)
            dvn_parts.append(lax.dot_general(wcs[gi], dsv_b, _MM_TA, preferred_element_type=F32))
            vn_b = vn[:, gi * CHUNK:(gi + 1) * CHUNK].astype(BF16)
            dws_parts.append(jnp.where(causal, lax.dot_general(dsv_b, vn_b, _MM_TB, preferred_element_type=F32), 0.0))
            dbf_parts.append(jnp.broadcast_to(jnp.sum(dsv_g, axis=1, keepdims=True), (CHUNK, CHUNK)))
        dvn = jnp.concatenate(dvn_parts, axis=1)
        dgain = jnp.sum(dvn * v * r, axis=0, keepdims=True)
        gvv = dvn * gv
        dv = r * gvv - v * (r * r * r) * jnp.mean(v * gvv, axis=-1, keepdims=True)
        _, vjp = jax.vjp(jax.nn.gelu, hin_v)
        dh_ref[...] = vjp(jnp.concatenate([du, dv], axis=1))[0].astype(dh_ref.dtype)

        @pl.when(i == 0)
        def _():
            for gi in range(SG_GROUPS):
                dws_ref[gi] = dws_parts[gi]
                dbf_ref[gi] = dbf_parts[gi]
            dg_ref[...] = dgain

        @pl.when(i > 0)
        def _():
            for gi in range(SG_GROUPS):
                dws_ref[gi] += dws_parts[gi]
                dbf_ref[gi] += dbf_parts[gi]
            dg_ref[...] += dgain

    full = lambda a: pl.BlockSpec(a.shape, lambda i, nd=a.ndim: (0,) * nd)
    sq = (SG_GROUPS, CHUNK, CHUNK)
    return pl.pallas_call(
        body, grid=(t // CHUNK,),
        in_specs=[pl.BlockSpec((CHUNK, w2), lambda i: (i, 0)), pl.BlockSpec((CHUNK, width), lambda i: (i, 0)),
                  full(g), full(ws), full(bfull)],
        out_specs=[pl.BlockSpec((CHUNK, w2), lambda i: (i, 0)), pl.BlockSpec(sq, lambda i: (0, 0, 0)),
                   pl.BlockSpec(sq, lambda i: (0, 0, 0)), pl.BlockSpec((1, width), lambda i: (0, 0))],
        out_shape=[jax.ShapeDtypeStruct((t, w2), BF16), jax.ShapeDtypeStruct(sq, F32), jax.ShapeDtypeStruct(sq, F32),
                   jax.ShapeDtypeStruct((1, width), F32)],
        name="sgu_bwd", compiler_params=_params(("arbitrary",)),
    )(hin, dp, g, ws, bfull)


def _disc1(lam_re, lam_im, log_dt):
    lr = jnp.minimum(lam_re, -1e-4)
    li = lam_im
    dt = jnp.exp(log_dt)
    mag = jnp.exp(dt * lr)
    ar = mag * jnp.cos(dt * li)
    ai = mag * jnp.sin(dt * li)
    den = lr * lr + li * li
    return ar, ai, ((ar - 1.0) * lr + ai * li) / den, (ai * lr - (ar - 1.0) * li) / den


def _disc2(cre, cim, b_re, b_im):
    return cre * b_re - cim * b_im, cre * b_im + cim * b_re


def _single(fn, ins, out_shapes, name):
    n = len(ins)

    def body(*refs):
        vals = fn(*[r[...] for r in refs[:n]])
        for ref, val in zip(refs[n:], vals):
            ref[...] = val

    return pl.pallas_call(body, out_shape=[jax.ShapeDtypeStruct(s, F32) for s in out_shapes], name=name)(*ins)


def s5_scan_fwd(bu2, a2):
    _, t, n = bu2.shape
    cb = SCAN_COLS

    def body(bu_ref, a_ref, x_ref):
        ar, ai = a_ref[0:1, :], a_ref[1:2, :]

        def step(b, carry):
            xr, xi = carry
            t0 = pl.multiple_of(b * 8, 8)
            br = bu_ref[0, pl.ds(t0, 8), :]
            bi = bu_ref[1, pl.ds(t0, 8), :]
            rows_r, rows_i = [], []
            for r in range(8):
                xr, xi = ar * xr - ai * xi + br[r:r + 1], ar * xi + ai * xr + bi[r:r + 1]
                rows_r.append(xr)
                rows_i.append(xi)
            x_ref[0, pl.ds(t0, 8), :] = jnp.concatenate(rows_r, axis=0)
            x_ref[1, pl.ds(t0, 8), :] = jnp.concatenate(rows_i, axis=0)
            return xr, xi

        zero = jnp.zeros((1, cb), F32)
        lax.fori_loop(0, t // 8, step, (zero, zero))

    return pl.pallas_call(
        body, grid=(n // cb,),
        in_specs=[pl.BlockSpec((2, t, cb), lambda j: (0, 0, j)), pl.BlockSpec((2, cb), lambda j: (0, j))],
        out_specs=pl.BlockSpec((2, t, cb), lambda j: (0, 0, j)), out_shape=jax.ShapeDtypeStruct((2, t, n), F32),
        name="s5_scan_fwd", compiler_params=_params(("parallel",)),
    )(bu2, a2)


def s5_scan_bwd(dx2, x2, a2):
    _, t, n = dx2.shape
    cb = SCAN_COLS
    nb = t // 8

    def body(dx_ref, x_ref, a_ref, g_ref, da_ref):
        ar, ai = a_ref[0:1, :], a_ref[1:2, :]
        row = lax.broadcasted_iota(jnp.int32, (8, cb), 0)

        def step(s, carry):
            gr, gi, dar, dai = carry
            b = nb - 1 - s
            t0 = pl.multiple_of(b * 8, 8)
            dr = dx_ref[0, pl.ds(t0, 8), :]
            di = dx_ref[1, pl.ds(t0, 8), :]
            rows_r, rows_i = [None] * 8, [None] * 8
            for r in range(7, -1, -1):
                gr, gi = dr[r:r + 1] + ar * gr + ai * gi, di[r:r + 1] - ai * gr + ar * gi
                rows_r[r] = gr
                rows_i[r] = gi
            g8r = jnp.concatenate(rows_r, axis=0)
            g8i = jnp.concatenate(rows_i, axis=0)
            g_ref[0, pl.ds(t0, 8), :] = g8r
            g_ref[1, pl.ds(t0, 8), :] = g8i
            tp = pl.multiple_of(jnp.maximum(t0 - 8, 0), 8)
            keep = b > 0
            prev_r = jnp.where(keep, x_ref[0, pl.ds(tp, 8), :], 0.0)[7:8]
            prev_i = jnp.where(keep, x_ref[1, pl.ds(tp, 8), :], 0.0)[7:8]
            xpr = jnp.where(row == 0, prev_r, pltpu.roll(x_ref[0, pl.ds(t0, 8), :], 1, 0))
            xpi = jnp.where(row == 0, prev_i, pltpu.roll(x_ref[1, pl.ds(t0, 8), :], 1, 0))
            return gr, gi, dar + g8r * xpr + g8i * xpi, dai + g8i * xpr - g8r * xpi

        z1, z8 = jnp.zeros((1, cb), F32), jnp.zeros((8, cb), F32)
        _, _, dar, dai = lax.fori_loop(0, nb, step, (z1, z1, z8, z8))
        da_ref[0:1, :] = jnp.sum(dar, axis=0, keepdims=True)
        da_ref[1:2, :] = jnp.sum(dai, axis=0, keepdims=True)

    blk = pl.BlockSpec((2, t, cb), lambda j: (0, 0, j))
    vec = pl.BlockSpec((2, cb), lambda j: (0, j))
    return pl.pallas_call(
        body, grid=(n // cb,), in_specs=[blk, blk, vec], out_specs=[blk, vec],
        out_shape=[jax.ShapeDtypeStruct((2, t, n), F32), jax.ShapeDtypeStruct((2, n), F32)],
        name="s5_scan_bwd", compiler_params=_params(("parallel",)),
    )(dx2, x2, a2)


_SP_U = SSM_PACK * SSM_GROUP
_SP_X = SSM_PACK * SSM_STATE
_NKB = SSM_GROUPS // SSM_PACK


def mm_s5(kind, a, b, m, name, res=None, tm=512):
    kw = dict(passes=3, name=name)
    xblk = lambda row, sel, col: ((None, tm, _SP_X), lambda *g: (sel(*g), row(*g), col(*g)))
    if kind == "bu":
        o_blk, o_map = xblk(lambda g, i, k: i, lambda g, i, k: g // _NKB, lambda g, i, k: g % _NKB)
        return _mm(a, b, grid=(2 * _NKB, m // tm, 1), a_blk=(tm, _SP_U), a_map=lambda g, i, k: (i, g % _NKB),
                   b_blk=(None, None, _SP_U, _SP_X), b_map=lambda g, i, k: (g // _NKB, g % _NKB, 0, 0),
                   o_blk=o_blk, o_map=o_map, out_shape=(2, m, _NKB * _SP_X), out_dtype=F32, **kw)
    if kind == "yc":
        a_blk, a_map = xblk(lambda j, i, k: i, lambda j, i, k: k, lambda j, i, k: j)
        return _mm(a, b, grid=(_NKB, m // tm, 2), a_blk=a_blk, a_map=a_map,
                   b_blk=(None, None, _SP_X, _SP_U), b_map=lambda j, i, k: (k, j, 0, 0),
                   o_blk=(tm, _SP_U), o_map=lambda j, i, k: (i, j), out_shape=(m, _NKB * _SP_U), out_dtype=F32,
                   acc_2d=(tm, _SP_U), **kw)
    if kind == "dx":
        o_blk, o_map = xblk(lambda g, i, k: i, lambda g, i, k: g // _NKB, lambda g, i, k: g % _NKB)
        return _mm(a, b, grid=(2 * _NKB, m // tm, 1), a_blk=(tm, _SP_U), a_map=lambda g, i, k: (i, g % _NKB),
                   b_blk=(None, None, _SP_X, _SP_U), b_map=lambda g, i, k: (g // _NKB, g % _NKB, 0, 0),
                   o_blk=o_blk, o_map=o_map, out_shape=(2, m, _NKB * _SP_X), out_dtype=F32, dims=_MM_TB, **kw)
    if kind == "dcd":
        a_blk, a_map = xblk(lambda g, _, k: k, lambda g, _, k: g // _NKB, lambda g, _, k: g % _NKB)
        return _mm(a, b, grid=(2 * _NKB, 1, m // tm), a_blk=a_blk, a_map=a_map,
                   b_blk=(tm, _SP_U), b_map=lambda g, _, k: (k, g % _NKB),
                   o_blk=(None, None, _SP_X, _SP_U), o_map=lambda g, _, k: (g // _NKB, g % _NKB, 0, 0),
                   out_shape=(2, _NKB, _SP_X, _SP_U), out_dtype=F32, dims=_MM_TA, acc_2d=(_SP_X, _SP_U), **kw)
    if kind == "du":
        a_blk, a_map = xblk(lambda j, i, k: i, lambda j, i, k: k, lambda j, i, k: j)
        return _mm(a, b, grid=(_NKB, m // tm, 2), a_blk=a_blk, a_map=a_map,
                   b_blk=(None, None, _SP_U, _SP_X), b_map=lambda j, i, k: (k, j, 0, 0),
                   o_blk=(tm, _SP_U), o_map=lambda j, i, k: (i, j), out_shape=(m, _NKB * _SP_U), out_dtype=F32,
                   dims=_MM_TB, acc_2d=(tm, _SP_U), res=res, res_blk=(tm, _SP_U), res_map=lambda j, i, k: (i, j), **kw)
    assert kind == "dbd"
    b_blk, b_map = xblk(lambda g, _, k: k, lambda g, _, k: g // _NKB, lambda g, _, k: g % _NKB)
    return _mm(a, b, grid=(2 * _NKB, 1, m // tm), a_blk=(tm, _SP_U), a_map=lambda g, _, k: (k, g % _NKB),
               b_blk=b_blk, b_map=b_map,
               o_blk=(None, None, _SP_U, _SP_X), o_map=lambda g, _, k: (g // _NKB, g % _NKB, 0, 0),
               out_shape=(2, _NKB, _SP_U, _SP_X), out_dtype=F32, dims=_MM_TA, acc_2d=(_SP_U, _SP_X), **kw)


def _block_diag(w):
    g, a, b = w.shape
    eye = jnp.eye(SSM_PACK, dtype=w.dtype)
    wp = w.reshape(g // SSM_PACK, SSM_PACK, a, b)
    return jnp.einsum("kgab,gh->kgahb", wp, eye).reshape(g // SSM_PACK, SSM_PACK * a, SSM_PACK * b)


def _block_diag_t(d, a, b):
    k = d.shape[0]
    eye = jnp.eye(SSM_PACK, dtype=d.dtype)
    dp = d.reshape(k, SSM_PACK, a, SSM_PACK, b)
    return jnp.einsum("kgahb,gh->kgab", dp, eye).reshape(k * SSM_PACK, a, b)


def _coords():
    return lax.axis_index("x"), lax.axis_index("y"), lax.axis_index("c")


def all_gather(tensors, name):
    n = len(tensors)
    any_spec = pl.BlockSpec(memory_space=pl.ANY)

    def body(*refs):
        ins, outs = refs[:n], refs[n:2 * n]
        send, recv, local = refs[2 * n:]
        x, y, c = _coords()
        me, sibling = (x, y, c), (x, y, 1 - c)
        chips = [(1 - x, y), (x, 1 - y), (1 - x, 1 - y)]

        def slot(p):
            return 4 * p[0] + 2 * p[1] + p[2]

        def copy(t, k, block, to, src=None):
            dst = outs[t].at[slot(block)]
            return pltpu.make_async_remote_copy(
                src_ref=dst if src is None else src, dst_ref=dst, send_sem=send.at[7 * t + k],
                recv_sem=recv.at[7 * t + k], device_id=to, device_id_type=pl.DeviceIdType.MESH)

        own, sent = [], []
        for t in range(n):
            mine = pltpu.make_async_copy(ins[t], outs[t].at[slot(me)], local.at[t])
            mine.start()
            own.append(mine)
            first = [copy(t, 0, me, sibling, src=ins[t])]
            first += [copy(t, 1 + j, me, (*chip, c), src=ins[t]) for j, chip in enumerate(chips)]
            for cp in first:
                cp.start()
            sent += first
        for t in range(n):
            for j, chip in enumerate(chips):
                copy(t, 1 + j, (*chip, c), me).wait_recv()
                passed = copy(t, 4 + j, (*chip, c), sibling)
                passed.start()
                sent.append(passed)
        for t in range(n):
            copy(t, 0, sibling, me).wait_recv()
            for j, chip in enumerate(chips):
                copy(t, 4 + j, (*chip, 1 - c), me).wait_recv()
        for cp in sent:
            cp.wait_send()
        for cp in own:
            cp.wait()

    return pl.pallas_call(
        body, in_specs=[any_spec] * n, out_specs=[any_spec] * n,
        out_shape=[jax.ShapeDtypeStruct((NDEV,) + a.shape, a.dtype) for a in tensors],
        scratch_shapes=[pltpu.SemaphoreType.DMA((7 * n,)), pltpu.SemaphoreType.DMA((7 * n,)),
                        pltpu.SemaphoreType.DMA((n,))],
        name=name,
    )(*tensors)


def all_to_all(groups, name):
    flat = [a for grp in groups for a in grp]
    n = len(flat)
    where = [(gi, li) for gi, grp in enumerate(groups) for li in range(len(grp))]
    any_spec = pl.BlockSpec(memory_space=pl.ANY)

    def body(*refs):
        ins, outs = refs[:n], refs[n:n + len(groups)]
        send, recv, local = refs[n + len(groups):]
        x, y, c = _coords()
        me = 4 * x + 2 * y + c
        waits = []
        for e in range(n):
            gi, li = where[e]
            mine = pltpu.make_async_copy(ins[e].at[me], outs[gi].at[me, li], local.at[e])
            mine.start()
            waits.append(mine)
        sent, landing = [], []
        for rel in range(1, NDEV):
            px = 1 - x if rel & 4 else x
            py = 1 - y if rel & 2 else y
            pc = 1 - c if rel & 1 else c
            peer = 4 * px + 2 * py + pc
            for e in range(n):
                gi, li = where[e]

                def copy(dst_slot, e=e, gi=gi, li=li, rel=rel, peer=peer, to=(px, py, pc)):
                    return pltpu.make_async_remote_copy(
                        src_ref=ins[e].at[peer], dst_ref=outs[gi].at[dst_slot, li], send_sem=send.at[7 * e + rel - 1],
                        recv_sem=recv.at[7 * e + rel - 1], device_id=to, device_id_type=pl.DeviceIdType.MESH)

                cp = copy(me)
                cp.start()
                sent.append(cp)
                landing.append(copy(peer))
        for cp in landing:
            cp.wait_recv()
        for cp in sent:
            cp.wait_send()
        for cp in waits:
            cp.wait()

    return pl.pallas_call(
        body, in_specs=[any_spec] * n, out_specs=[any_spec] * len(groups),
        out_shape=[jax.ShapeDtypeStruct((NDEV, len(grp)) + grp[0].shape[1:], grp[0].dtype) for grp in groups],
        scratch_shapes=[pltpu.SemaphoreType.DMA((7 * n,)), pltpu.SemaphoreType.DMA((7 * n,)),
                        pltpu.SemaphoreType.DMA((n,))],
        name=name,
    )(*flat)


_PACK_QUANTUM = 8 * LANES


def _pack(parts, lead=0):
    out = []
    for p in parts:
        head = p.shape[:lead]
        f = p.astype(F32).reshape(head + (-1,))
        pad = (-f.shape[-1]) % _PACK_QUANTUM
        if pad:
            f = jnp.concatenate([f, jnp.zeros(head + (pad,), F32)], axis=-1)
        out.append(f.reshape(head + (-1, LANES)))
    return jnp.concatenate(out, axis=lead)


def _unpack(buf, shapes):
    head = buf.shape[:-2]
    out, r = [], 0
    for s in shapes:
        n = 1
        for v in s:
            n *= v
        nr = -(-n // _PACK_QUANTUM) * 8
        flat = buf[..., r:r + nr, :].reshape(head + (nr * LANES,))[..., :n]
        out.append(flat.reshape(head + tuple(s)))
        r += nr
    return out


BIG = ("sb_w_qkv", "sb_w_o", "sg_w_in", "sg_w_o", "ssm_w_in", "ssm_w_glu", "ffn_w_up", "ffn_w_down")
SMALL_SHARDED = ("norm_g", "ssm_d", "ffn_conv_w")
REPLICATED = ("final_norm_g", "sg_norm_g", "sg_w_s", "sg_b", "ssm_lam_re", "ssm_lam_im", "ssm_log_dt",
              "ssm_b_re", "ssm_b_im", "ssm_c_re", "ssm_c_im", "ffn_conv_b")
WEIGHTS = ("norm_g", "final_norm_g", "sb_w_qkv", "sb_w_o", "sg_w_in", "sg_norm_g", "sg_w_s", "sg_b", "sg_w_o",
           "ssm_w_in", "ssm_lam_re", "ssm_lam_im", "ssm_log_dt", "ssm_b_re", "ssm_b_im", "ssm_c_re", "ssm_c_im",
           "ssm_d", "ssm_w_glu", "ffn_w_up", "ffn_conv_w", "ffn_conv_b", "ffn_w_down")


def _step(x, loss_target, w, m, v):
    t, d = x.shape[1], x.shape[2]
    depth = w["norm_g"].shape[0]
    x0 = x.reshape(t, d)
    tgt = loss_target.reshape(t, d)

    shard_pack = _pack([w[k] for k in SMALL_SHARDED])
    gathered = all_gather([cast_bf16(w[k]) for k in BIG] + [shard_pack], name="gather_weights")
    wg = dict(zip(BIG, gathered[:-1]))
    ng, sd, cw = _unpack(gathered[-1], [w[k].shape for k in SMALL_SHARDED])
    norm_full = jnp.transpose(ng, (1, 2, 0, 3)).reshape(depth, 2, d)
    ssm_d_full = jnp.transpose(sd, (1, 0, 2)).reshape(1, d)
    nc = cw.shape[-1]
    conv_b3 = w["ffn_conv_b"].reshape(depth, NDEV, nc)
    p3 = [jnp.concatenate([cw[:, l], conv_b3[l][:, None, :], jnp.zeros((NDEV, 8 - CONV_K - 1, nc), F32)], axis=1)
          for l in range(depth)]

    g_, p_, h_ = SSM_GROUPS, SSM_STATE, SSM_GROUP
    lam_re, lam_im = w["ssm_lam_re"][0], w["ssm_lam_im"][0]
    log_dt = w["ssm_log_dt"][0].reshape(g_, 1)
    b_re, b_im = w["ssm_b_re"][0].reshape(g_ * p_, h_), w["ssm_b_im"][0].reshape(g_ * p_, h_)
    ar, ai, cre, cim = _single(_disc1, [lam_re, lam_im, log_dt], [(g_, p_)] * 4, "s5_disc1")
    cre_c, cim_c = cre.reshape(g_ * p_, 1), cim.reshape(g_ * p_, 1)
    bbr, bbi = _single(_disc2, [cre_c, cim_c, b_re, b_im], [(g_ * p_, h_)] * 2, "s5_disc2")
    per_group_t = lambda a, r, c: jnp.swapaxes(a.reshape(g_, r, c), 1, 2)
    bd = jnp.stack([_block_diag(per_group_t(bbr, p_, h_)), _block_diag(per_group_t(bbi, p_, h_))])
    cd = jnp.stack([_block_diag(per_group_t(w["ssm_c_re"][0], h_, p_)),
                    -_block_diag(per_group_t(w["ssm_c_im"][0], h_, p_))])
    a2 = jnp.stack([ar.reshape(g_ * p_), ai.reshape(g_ * p_)])

    sg_gain = w["sg_norm_g"]
    sg_ws = w["sg_w_s"][0]
    sg_bfull = jnp.broadcast_to(w["sg_b"][0][:, :, None], sg_ws.shape)

    acts = []
    xc = x0
    for i in range(depth):
        mixer, j = i % 3, i // 3
        st = {"x": xc}
        g0 = norm_full[i, 0][None]
        xn = rms_fwd(xc, g0, "rms_fwd")
        st["xn"] = xn
        if mixer == 0:
            qkv = mm_cs_fwd(xn, wg["sb_w_qkv"], j, BF16, "qkv_fwd")
            o, ltot = sb_attn_fwd(qkv)
            x1 = mm_rs_fwd(o, wg["sb_w_o"], j, xc, F32, "attn_out_fwd")
            st.update(qkv=qkv, o=o, ltot=ltot)
        elif mixer == 1:
            hin = mm_cs_fwd(xn, wg["sg_w_in"], j, BF16, "sg_in_fwd")
            p = sgu_fwd(hin, sg_gain, sg_ws, sg_bfull)
            x1 = mm_rs_fwd(p, wg["sg_w_o"], j, xc, F32, "sg_out_fwd")
            st.update(hin=hin, p=p)
        else:
            u = mm_rs_fwd(xn, wg["ssm_w_in"], j, None, F32, "ssm_in_fwd")
            bu2 = mm_s5("bu", u, bd, t, "s5_bu")
            x2 = s5_scan_fwd(bu2, a2)
            yc = mm_s5("yc", x2, cd, t, "s5_yc")
            yg = s5_post_fwd(yc, u, ssm_d_full)
            hg = mm_cs_fwd(yg, wg["ssm_w_glu"], j, BF16, "ssm_glu_fwd")
            x1 = glu_fwd(hg, xc)
            st.update(u=u, x2=x2, yc=yc, yg=yg, hg=hg)
        g1 = norm_full[i, 1][None]
        xn2 = rms_fwd(x1, g1, "rms_fwd")
        h3 = mm_cs_fwd(xn2, wg["ffn_w_up"], i, BF16, "ffn_up_fwd")
        gated = ffn_gate_fwd(h3, p3[i])
        xc = mm_down_fwd(gated, wg["ffn_w_down"], i, x1, "ffn_down_fwd")
        st.update(x1=x1, xn2=xn2, h3=h3, gated=gated, g0=g0, g1=g1)
        acts.append(st)

    dx, loss_lanes, d_final_g = loss_head(xc, w["final_norm_g"][None], tgt)
    loss = lax.psum(loss_lanes[0, 0], MESH_AXES)

    gw = {k: [None] * w[k].shape[0] for k in BIG}
    d_norm = [[None, None] for _ in range(depth)]
    d_p3 = [None] * depth
    rep = {}
    d_ssm_d = None
    for i in reversed(range(depth)):
        mixer, j = i % 3, i // 3
        st = acts[i]
        dgated = mm_down_da(dx, wg["ffn_w_down"], i, "ffn_down_da")
        gw["ffn_w_down"][i] = mm_down_dw(st["gated"], dx, "ffn_down_dw")
        dy3, d_p3[i] = ffn_gate_bwd(st["h3"], dgated, p3[i])
        dh3 = ffn_conv_t(dy3, p3[i])
        dxn2 = mm_cs_da(dh3, wg["ffn_w_up"], i, t, "ffn_up_da")
        gw["ffn_w_up"][i] = mm_cs_dw(st["xn2"], dh3, nc, "ffn_up_dw")
        dx1, d_norm[i][1] = rms_bwd(st["x1"], st["g1"], dxn2, dx, "rms_bwd")
        if mixer == 0:
            do = mm_rs_da(dx1, wg["sb_w_o"], j, BF16, "attn_out_da")
            gw["sb_w_o"][j] = mm_rs_dw(st["o"], dx1, "attn_out_dw")
            d3 = sb_attn_bwd(st["qkv"], st["ltot"], do)
            dxn = mm_qkv_da(d3, wg["sb_w_qkv"], j, "qkv_da")
            gw["sb_w_qkv"][j] = mm_qkv_dw(st["xn"], d3, wg["sb_w_qkv"].shape[3], "qkv_dw")
        elif mixer == 1:
            dp = mm_rs_da(dx1, wg["sg_w_o"], j, BF16, "sg_out_da")
            gw["sg_w_o"][j] = mm_rs_dw(st["p"], dx1, "sg_out_dw")
            dhin, d_ws, d_bfull, d_gain = sgu_bwd(st["hin"], dp, sg_gain, sg_ws, sg_bfull)
            rep.update(sg_w_s=d_ws[None], sg_b=d_bfull[None, :, :, 0], sg_norm_g=d_gain)
            dxn = mm_cs_da(dhin, wg["sg_w_in"], j, t, "sg_in_da")
            gw["sg_w_in"][j] = mm_cs_dw(st["xn"], dhin, wg["sg_w_in"].shape[3], "sg_in_dw")
        else:
            dhg = glu_bwd(st["hg"], dx1)
            dyg = mm_cs_da(dhg, wg["ssm_w_glu"], j, t, "ssm_glu_da")
            gw["ssm_w_glu"][j] = mm_cs_dw(st["yg"], dhg, wg["ssm_w_glu"].shape[3], "ssm_glu_dw")
            dyc, du_skip, d_ssm_d = s5_post_bwd(st["yc"], st["u"], ssm_d_full, dyg)
            dx2 = mm_s5("dx", dyc, cd, t, "s5_dx")
            dcd = mm_s5("dcd", st["x2"], dyc, t, "s5_dcd")
            g2, da2 = s5_scan_bwd(dx2, st["x2"], a2)
            du = mm_s5("du", g2, bd, t, "s5_du", res=du_skip)
            dbd = mm_s5("dbd", st["u"], g2, t, "s5_dbd")
            from_bd = lambda blk: jnp.swapaxes(_block_diag_t(blk, h_, p_), 1, 2).reshape(g_ * p_, h_)

            def disc2_bwd(c1, c2, b1, b2, t1, t2):
                return jax.vjp(_disc2, c1, c2, b1, b2)[1]((t1, t2))

            d_cre, d_cim, d_b_re, d_b_im = _single(
                disc2_bwd, [cre_c, cim_c, b_re, b_im, from_bd(dbd[0]), from_bd(dbd[1])],
                [(g_ * p_, 1)] * 2 + [(g_ * p_, h_)] * 2, "s5_disc2_bwd")

            def disc1_bwd(l1, l2, ld, t1, t2, t3, t4):
                return jax.vjp(_disc1, l1, l2, ld)[1]((t1, t2, t3, t4))

            d_lam_re, d_lam_im, d_log_dt = _single(
                disc1_bwd, [lam_re, lam_im, log_dt, da2[0].reshape(g_, p_), da2[1].reshape(g_, p_),
                            d_cre.reshape(g_, p_), d_cim.reshape(g_, p_)],
                [(g_, p_), (g_, p_), (g_, 1)], "s5_disc1_bwd")
            from_cd = lambda blk: jnp.swapaxes(_block_diag_t(blk, p_, h_), 1, 2)
            rep.update(ssm_lam_re=d_lam_re[None], ssm_lam_im=d_lam_im[None], ssm_log_dt=d_log_dt.reshape(1, g_),
                       ssm_b_re=d_b_re.reshape(1, g_, p_, h_), ssm_b_im=d_b_im.reshape(1, g_, p_, h_),
                       ssm_c_re=from_cd(dcd[0])[None], ssm_c_im=-from_cd(dcd[1])[None])
            dxn = mm_rs_da(du, wg["ssm_w_in"], j, F32, "ssm_in_da")
            gw["ssm_w_in"][j] = mm_rs_dw(st["xn"], du, "ssm_in_dw")
        dx, d_norm[i][0] = rms_bwd(st["x"], st["g0"], dxn, dx1, "rms_bwd")

    rep["final_norm_g"] = d_final_g.reshape(d)
    rep["ffn_conv_b"] = jnp.stack([d_p3[l][:, CONV_K, :].reshape(NDEV * nc) for l in range(depth)])

    d_norm_full = jnp.stack([jnp.concatenate(pair, axis=0) for pair in d_norm])
    d_norm_pieces = jnp.transpose(d_norm_full.reshape(depth, 2, NDEV, d // NDEV), (2, 0, 1, 3))
    d_ssm_d_pieces = jnp.transpose(d_ssm_d.reshape(1, NDEV, d // NDEV), (1, 0, 2))
    d_conv_w_pieces = jnp.stack([d_p3[l][:, :CONV_K, :] for l in range(depth)], axis=1)
    small_pieces = _pack([d_norm_pieces, d_ssm_d_pieces, d_conv_w_pieces], lead=1)
    received = all_to_all([gw[k] for k in BIG] + [[small_pieces]], name="scatter_grads")
    rep_parts, = all_gather([_pack([rep[k] for k in REPLICATED])], name="gather_small_grads")

    grads, deltas, new_m, new_v = {}, {}, {}, {}
    for k, parts in zip(BIG, received[:-1]):
        shape = w[k].shape
        cols = shape[-1]
        res = adamw(w[k].reshape(-1, cols), m[k].reshape(-1, cols), v[k].reshape(-1, cols),
                    parts.reshape(NDEV, -1, cols), "adamw")
        grads[k], deltas[k], new_m[k], new_v[k] = [r.reshape(shape) for r in res]
    for names, parts in ((SMALL_SHARDED, received[-1][:, 0]), (REPLICATED, rep_parts)):
        res = adamw(_pack([w[k] for k in names]), _pack([m[k] for k in names]), _pack([v[k] for k in names]), parts,
                    "adamw_small")
        for tree, buf in zip((grads, deltas, new_m, new_v), res):
            for k, val in zip(names, _unpack(buf, [w[k].shape for k in names])):
                tree[k] = val
    grad_x = dx.reshape(x.shape)
    return (loss, grad_x, *[grads[k] for k in WEIGHTS], *[deltas[k] for k in WEIGHTS],
            *[new_m[k] for k in WEIGHTS], *[new_v[k] for k in WEIGHTS])


def kernel(x, norm_g, final_norm_g, sb_w_qkv, sb_w_o, sg_w_in, sg_norm_g, sg_w_s, sg_b, sg_w_o, ssm_w_in, ssm_lam_re, ssm_lam_im, ssm_log_dt, ssm_b_re, ssm_b_im, ssm_c_re, ssm_c_im, ssm_d, ssm_w_glu, ffn_w_up, ffn_conv_w, ffn_conv_b, ffn_w_down, loss_target, m_norm_g, m_final_norm_g, m_sb_w_qkv, m_sb_w_o, m_sg_w_in, m_sg_norm_g, m_sg_w_s, m_sg_b, m_sg_w_o, m_ssm_w_in, m_ssm_lam_re, m_ssm_lam_im, m_ssm_log_dt, m_ssm_b_re, m_ssm_b_im, m_ssm_c_re, m_ssm_c_im, m_ssm_d, m_ssm_w_glu, m_ffn_w_up, m_ffn_conv_w, m_ffn_conv_b, m_ffn_w_down, v_norm_g, v_final_norm_g, v_sb_w_qkv, v_sb_w_o, v_sg_w_in, v_sg_norm_g, v_sg_w_s, v_sg_b, v_sg_w_o, v_ssm_w_in, v_ssm_lam_re, v_ssm_lam_im, v_ssm_log_dt, v_ssm_b_re, v_ssm_b_im, v_ssm_c_re, v_ssm_c_im, v_ssm_d, v_ssm_w_glu, v_ffn_w_up, v_ffn_conv_w, v_ffn_conv_b, v_ffn_w_down):
    w = dict(zip(WEIGHTS, (norm_g, final_norm_g, sb_w_qkv, sb_w_o, sg_w_in, sg_norm_g, sg_w_s, sg_b, sg_w_o, ssm_w_in,
                           ssm_lam_re, ssm_lam_im, ssm_log_dt, ssm_b_re, ssm_b_im, ssm_c_re, ssm_c_im, ssm_d, ssm_w_glu,
                           ffn_w_up, ffn_conv_w, ffn_conv_b, ffn_w_down)))
    m = dict(zip(WEIGHTS, (m_norm_g, m_final_norm_g, m_sb_w_qkv, m_sb_w_o, m_sg_w_in, m_sg_norm_g, m_sg_w_s, m_sg_b,
                           m_sg_w_o, m_ssm_w_in, m_ssm_lam_re, m_ssm_lam_im, m_ssm_log_dt, m_ssm_b_re, m_ssm_b_im,
                           m_ssm_c_re, m_ssm_c_im, m_ssm_d, m_ssm_w_glu, m_ffn_w_up, m_ffn_conv_w, m_ffn_conv_b,
                           m_ffn_w_down)))
    v = dict(zip(WEIGHTS, (v_norm_g, v_final_norm_g, v_sb_w_qkv, v_sb_w_o, v_sg_w_in, v_sg_norm_g, v_sg_w_s, v_sg_b,
                           v_sg_w_o, v_ssm_w_in, v_ssm_lam_re, v_ssm_lam_im, v_ssm_log_dt, v_ssm_b_re, v_ssm_b_im,
                           v_ssm_c_re, v_ssm_c_im, v_ssm_d, v_ssm_w_glu, v_ffn_w_up, v_ffn_conv_w, v_ffn_conv_b,
                           v_ffn_w_down)))
    return _step(x, loss_target, w, m, v)
```

```python
import functools

import jax
import jax.numpy as jnp
from jax import lax
from jax.experimental import pallas as pl
from jax.experimental.pallas import tpu as pltpu

F32, BF16 = jnp.float32, jnp.bfloat16
MESH_AXES = ("x", "y", "c")
NDEV = 8
EPS = 1e-6
HEAD_DIM = 64
LANES = 128
ATT_BQ, ATT_BK = 512, 256
CHUNK = 128
SG_GROUPS = 8
SSM_GROUPS, SSM_STATE, SSM_GROUP = 64, 64, 16
SSM_PACK = 8
CONV_K = 3
HALO = 16
SCAN_COLS = 256
ADAM_LR, ADAM_B1, ADAM_B2, ADAM_EPS, ADAM_WD, ADAM_STEP = 0.001, 0.9, 0.999, 1e-08, 0.01, 10
VMEM_LIMIT = 56 * 1024 * 1024

_MM = (((1,), (0,)), ((), ()))
_MM_TB = (((1,), (1,)), ((), ()))
_MM_TA = (((0,), (0,)), ((), ()))


def _params(sem):
    return pltpu.CompilerParams(dimension_semantics=sem, vmem_limit_bytes=VMEM_LIMIT)


def _rows(total, cap, mult=16):
    best = None
    for d in range(mult, min(total, cap) + 1, mult):
        if total % d == 0:
            best = d
    return best if best is not None else total


def _dot(a, b, dims, passes):
    if passes == 1:
        return lax.dot_general(a.astype(BF16), b.astype(BF16), dims, preferred_element_type=F32)
    a = a.astype(F32)
    b = b.astype(F32)
    ah = a.astype(BF16)
    bh = b.astype(BF16)
    al = (a - ah.astype(F32)).astype(BF16)
    bl = (b - bh.astype(F32)).astype(BF16)
    out = lax.dot_general(ah, bh, dims, preferred_element_type=F32)
    out = out + lax.dot_general(al, bh, dims, preferred_element_type=F32)
    return out + lax.dot_general(ah, bl, dims, preferred_element_type=F32)


def _mm(a, b, *, grid, a_blk, a_map, b_blk, b_map, o_blk, o_map, out_shape, out_dtype, name,
        dims=_MM, passes=1, res=None, res_blk=None, res_map=None, b_2d=None, acc_2d=None):
    nk = grid[2]
    has_res = res is not None

    def body(*refs):
        if has_res:
            a_ref, b_ref, r_ref, o_ref = refs[:4]
        else:
            a_ref, b_ref, o_ref = refs[:3]
        bv = b_ref[...]
        if b_2d is not None:
            bv = bv.reshape(b_2d)
        part = _dot(a_ref[...], bv, dims, passes)

        def finish(total):
            if has_res:
                total = total + r_ref[...].astype(F32)
            o_ref[...] = total.reshape(o_ref.shape).astype(o_ref.dtype)

        if nk == 1:
            finish(part)
        else:
            acc_ref = refs[-1]
            k = pl.program_id(2)

            @pl.when(k == 0)
            def _():
                acc_ref[...] = part

            @pl.when(k > 0)
            def _():
                acc_ref[...] += part

            @pl.when(k == nk - 1)
            def _():
                finish(acc_ref[...])

    in_specs = [pl.BlockSpec(a_blk, a_map), pl.BlockSpec(b_blk, b_map)]
    args = [a, b]
    if has_res:
        in_specs.append(pl.BlockSpec(res_blk, res_map))
        args.append(res)
    scratch = [pltpu.VMEM(acc_2d, F32)] if nk > 1 else []
    return pl.pallas_call(
        body, grid=grid, in_specs=in_specs, out_specs=pl.BlockSpec(o_blk, o_map),
        out_shape=jax.ShapeDtypeStruct(out_shape, out_dtype), scratch_shapes=scratch, name=name,
        compiler_params=_params(("parallel", "parallel", "arbitrary")),
    )(*args)


def _cs_act_spec(ns, tm, row_of, col_of):
    if ns % LANES == 0:
        return (tm, ns), lambda *g: (row_of(*g), col_of(*g))
    return (None, tm, ns), lambda *g: (col_of(*g), row_of(*g), 0)


def mm_cs_fwd(a, w4, l, out_dtype, name, tm=512):
    m, k = a.shape
    ns = w4.shape[3]
    o_blk, o_map = _cs_act_spec(ns, tm, lambda j, i, kk: i, lambda j, i, kk: j)
    out_shape = (m, NDEV * ns) if ns % LANES == 0 else (NDEV, m, ns)
    return _mm(a, w4, grid=(NDEV, m // tm, 1), a_blk=(tm, k), a_map=lambda j, i, kk: (i, 0),
               b_blk=(None, None, k, ns), b_map=lambda j, i, kk: (j, l, 0, 0),
               o_blk=o_blk, o_map=o_map, out_shape=out_shape, out_dtype=out_dtype, name=name)


def mm_cs_da(dc, w4, l, m, name, tm=512):
    k, ns = w4.shape[2], w4.shape[3]
    a_blk, a_map = _cs_act_spec(ns, tm, lambda i, _, j: i, lambda i, _, j: j)
    return _mm(dc, w4, grid=(m // tm, 1, NDEV), a_blk=a_blk, a_map=a_map,
               b_blk=(None, None, k, ns), b_map=lambda i, _, j: (j, l, 0, 0),
               o_blk=(tm, k), o_map=lambda i, _, j: (i, 0), out_shape=(m, k), out_dtype=F32,
               dims=_MM_TB, acc_2d=(tm, k), name=name)


def mm_cs_dw(a, dc, ns, name, tk=512):
    m, k = a.shape
    b_blk, b_map = _cs_act_spec(ns, tk, lambda j, _, kk: kk, lambda j, _, kk: j)
    return _mm(a, dc, grid=(NDEV, 1, m // tk), a_blk=(tk, k), a_map=lambda j, _, kk: (kk, 0),
               b_blk=b_blk, b_map=b_map, o_blk=(None, k, ns), o_map=lambda j, _, kk: (j, 0, 0),
               out_shape=(NDEV, k, ns), out_dtype=BF16, dims=_MM_TA, acc_2d=(k, ns), name=name)


def mm_rs_fwd(a, w4, l, res, out_dtype, name, tm=512):
    m, k = a.shape
    ks, n = w4.shape[2], w4.shape[3]
    return _mm(a, w4, grid=(m // tm, 1, 1), a_blk=(tm, k), a_map=lambda i, _, kk: (i, 0),
               b_blk=(NDEV, None, ks, n), b_map=lambda i, _, kk: (0, l, 0, 0), b_2d=(k, n),
               o_blk=(tm, n), o_map=lambda i, _, kk: (i, 0), out_shape=(m, n), out_dtype=out_dtype,
               res=res, res_blk=(tm, n), res_map=lambda i, _, kk: (i, 0), name=name)


def mm_rs_da(dc, w4, l, out_dtype, name, tm=512):
    m, n = dc.shape
    ks = w4.shape[2]
    k = NDEV * ks
    return _mm(dc, w4, grid=(m // tm, 1, 1), a_blk=(tm, n), a_map=lambda i, _, kk: (i, 0),
               b_blk=(NDEV, None, ks, n), b_map=lambda i, _, kk: (0, l, 0, 0), b_2d=(k, n),
               o_blk=(tm, k), o_map=lambda i, _, kk: (i, 0), out_shape=(m, k), out_dtype=out_dtype,
               dims=_MM_TB, name=name)


def mm_rs_dw(a, dc, name, tk=512):
    m, k = a.shape
    n = dc.shape[1]
    ks = k // NDEV
    return _mm(a, dc, grid=(1, 1, m // tk), a_blk=(tk, k), a_map=lambda _, __, kk: (kk, 0),
               b_blk=(tk, n), b_map=lambda _, __, kk: (kk, 0),
               o_blk=(NDEV, ks, n), o_map=lambda _, __, kk: (0, 0, 0), out_shape=(NDEV, ks, n),
               out_dtype=BF16, dims=_MM_TA, acc_2d=(k, n), name=name)


def mm_down_fwd(a3, w4, l, res, name, tm=512):
    nj, m, kc = a3.shape
    ks, n = w4.shape[2], w4.shape[3]
    return _mm(a3, w4, grid=(m // tm, 1, nj), a_blk=(None, tm, kc), a_map=lambda i, _, j: (j, i, 0),
               b_blk=(2, None, ks, n), b_map=lambda i, _, j: (j, l, 0, 0), b_2d=(kc, n),
               o_blk=(tm, n), o_map=lambda i, _, j: (i, 0), out_shape=(m, n), out_dtype=F32,
               res=res, res_blk=(tm, n), res_map=lambda i, _, j: (i, 0), acc_2d=(tm, n), name=name)


def mm_down_da(dc, w4, l, name, tm=512):
    m, n = dc.shape
    ks = w4.shape[2]
    kc = 2 * ks
    nj = NDEV // 2
    return _mm(dc, w4, grid=(nj, m // tm, 1), a_blk=(tm, n), a_map=lambda j, i, _: (i, 0),
               b_blk=(2, None, ks, n), b_map=lambda j, i, _: (j, l, 0, 0), b_2d=(kc, n),
               o_blk=(None, tm, kc), o_map=lambda j, i, _: (j, i, 0), out_shape=(nj, m, kc),
               out_dtype=BF16, dims=_MM_TB, name=name)


def mm_down_dw(a3, dc, name, tk=512):
    nj, m, kc = a3.shape
    n = dc.shape[1]
    return _mm(a3, dc, grid=(nj, 1, m // tk), a_blk=(None, tk, kc), a_map=lambda j, _, kk: (j, kk, 0),
               b_blk=(tk, n), b_map=lambda j, _, kk: (kk, 0),
               o_blk=(2, kc // 2, n), o_map=lambda j, _, kk: (j, 0, 0), out_shape=(NDEV, kc // 2, n),
               out_dtype=BF16, dims=_MM_TA, acc_2d=(kc, n), name=name)


def mm_qkv_da(d3, w4, l, name, tm=512):
    _, m, d = d3.shape
    k, ns = w4.shape[2], w4.shape[3]
    per_arr, per_piece = d // LANES, ns // LANES
    ngroups = 3 * per_arr
    return _mm(d3, w4, grid=(m // tm, 1, ngroups),
               a_blk=(None, tm, LANES), a_map=lambda i, _, g: (g // per_arr, i, g % per_arr),
               b_blk=(None, None, k, LANES), b_map=lambda i, _, g: (g // per_piece, l, 0, g % per_piece),
               o_blk=(tm, k), o_map=lambda i, _, g: (i, 0), out_shape=(m, k), out_dtype=F32,
               dims=_MM_TB, acc_2d=(tm, k), name=name)


def mm_qkv_dw(a, d3, ns, name, tk=512):
    m, k = a.shape
    d = d3.shape[2]
    per_arr, per_piece = d // LANES, ns // LANES
    ngroups = 3 * per_arr
    return _mm(a, d3, grid=(ngroups, 1, m // tk), a_blk=(tk, k), a_map=lambda g, _, kk: (kk, 0),
               b_blk=(None, tk, LANES), b_map=lambda g, _, kk: (g // per_arr, kk, g % per_arr),
               o_blk=(None, k, LANES), o_map=lambda g, _, kk: (g // per_piece, 0, g % per_piece),
               out_shape=(NDEV, k, ns), out_dtype=BF16, dims=_MM_TA, acc_2d=(k, LANES), name=name)


def _rowwise(fn, ins, outs, *, tr, name, acc_outs=()):
    rows = next(a.shape[0] if kind == "row" else a.shape[1] for a, kind in ins if kind != "full")
    n_in, n_out = len(ins), len(outs)

    def body(*refs):
        vals = fn(*[r[...] for r in refs[:n_in]])
        if not isinstance(vals, (tuple, list)):
            vals = (vals,)
        for ref, val in zip(refs[n_in:n_in + n_out], vals[:n_out]):
            ref[...] = val.astype(ref.dtype)
        i = pl.program_id(0)
        for ref, val in zip(refs[n_in + n_out:], vals[n_out:]):
            val = val.astype(ref.dtype)

            @pl.when(i == 0)
            def _(ref=ref, val=val):
                ref[...] = val

            @pl.when(i > 0)
            def _(ref=ref, val=val):
                ref[...] += val

    in_specs = []
    for a, kind in ins:
        if kind == "row":
            in_specs.append(pl.BlockSpec((tr, a.shape[1]), lambda i: (i, 0)))
        elif kind == "row3":
            in_specs.append(pl.BlockSpec((a.shape[0], tr, a.shape[2]), lambda i: (0, i, 0)))
        else:
            in_specs.append(pl.BlockSpec(a.shape, lambda i, nd=a.ndim: (0,) * nd))
    out_specs = [pl.BlockSpec((tr, c), lambda i: (i, 0)) for c, _ in outs]
    out_specs += [pl.BlockSpec(s, lambda i, nd=len(s): (0,) * nd) for s, _ in acc_outs]
    out_shape = [jax.ShapeDtypeStruct((rows, c), dt) for c, dt in outs]
    out_shape += [jax.ShapeDtypeStruct(s, dt) for s, dt in acc_outs]
    res = pl.pallas_call(
        body, grid=(rows // tr,), in_specs=in_specs, out_specs=out_specs, out_shape=out_shape, name=name,
        compiler_params=_params(("arbitrary",) if acc_outs else ("parallel",)),
    )(*[a for a, _ in ins])
    return res


def _rms(x, g):
    return x * lax.rsqrt(jnp.mean(x * x, axis=-1, keepdims=True) + EPS) * g


def cast_bf16(w):
    w2 = w.reshape(-1, w.shape[-1])
    out, = _rowwise(lambda v: v, [(w2, "row")], [(w2.shape[1], BF16)], tr=_rows(w2.shape[0], 512), name="cast_bf16")
    return out.reshape(w.shape)


def rms_fwd(x, g, name):
    out, = _rowwise(_rms, [(x, "row"), (g, "full")], [(x.shape[1], BF16)], tr=256, name=name)
    return out


def rms_bwd(x, g, dy, dres, name):
    def fn(xv, gv, dyv, drv):
        _, vjp = jax.vjp(_rms, xv, gv)
        dx, dg = vjp(dyv.astype(F32))
        return drv + dx, dg

    d = x.shape[1]
    return _rowwise(fn, [(x, "row"), (g, "full"), (dy, "row"), (dres, "row")], [(d, F32)], tr=256, name=name,
                    acc_outs=[((1, d), F32)])


def loss_head(x, g, tgt):
    def f(xv, gv, tv):
        err = jnp.square(_rms(xv, gv) - tv)
        return 0.5 * jnp.sum(jnp.mean(err, axis=-1))

    def fn(xv, gv, tv):
        val, (dx, dg) = jax.value_and_grad(f, argnums=(0, 1))(xv, gv, tv)
        return dx, jnp.full((1, LANES), val, F32), dg

    d = x.shape[1]
    return _rowwise(fn, [(x, "row"), (g, "full"), (tgt, "row")], [(d, F32)], tr=256, name="loss_head",
                    acc_outs=[((1, LANES), F32), ((1, d), F32)])


def _glu(hg, x):
    half = hg.shape[1] // 2
    return x + hg[:, :half] * jax.nn.sigmoid(hg[:, half:])


def glu_fwd(hg, x):
    out, = _rowwise(lambda h, xv: _glu(h.astype(F32), xv), [(hg, "row"), (x, "row")], [(x.shape[1], F32)],
                    tr=256, name="glu_fwd")
    return out


def glu_bwd(hg, dx1):
    def fn(h, d):
        _, vjp = jax.vjp(lambda hv: _glu(hv, jnp.zeros_like(d)), h.astype(F32))
        return vjp(d)[0]

    out, = _rowwise(fn, [(hg, "row"), (dx1, "row")], [(hg.shape[1], BF16)], tr=256, name="glu_bwd")
    return out


def _s5_post(yc, u, d):
    return jax.nn.gelu(yc + d * u)


def s5_post_fwd(yc, u, d):
    out, = _rowwise(_s5_post, [(yc, "row"), (u, "row"), (d, "full")], [(yc.shape[1], BF16)], tr=256,
                    name="s5_post_fwd")
    return out


def s5_post_bwd(yc, u, d, dyg):
    def fn(ycv, uv, dv, g):
        _, vjp = jax.vjp(_s5_post, ycv, uv, dv)
        return vjp(g.astype(F32))

    dm = yc.shape[1]
    return _rowwise(fn, [(yc, "row"), (u, "row"), (d, "full"), (dyg, "row")], [(dm, F32), (dm, F32)], tr=256,
                    name="s5_post_bwd", acc_outs=[((1, dm), F32)])


def adamw(w, m, v, g_parts, name):
    def fn(wv, mv, vv, gp):
        g = gp[0].astype(F32)
        for p in range(1, gp.shape[0]):
            g = g + gp[p].astype(F32)
        m2 = ADAM_B1 * mv + (1.0 - ADAM_B1) * g
        v2 = ADAM_B2 * vv + (1.0 - ADAM_B2) * jnp.square(g)
        m_hat = m2 / (1.0 - ADAM_B1 ** ADAM_STEP)
        v_hat = v2 / (1.0 - ADAM_B2 ** ADAM_STEP)
        delta = -ADAM_LR * (m_hat / (jnp.sqrt(v_hat) + ADAM_EPS) + ADAM_WD * wv)
        return g, delta, m2, v2

    c = w.shape[1]
    return _rowwise(fn, [(w, "row"), (m, "row"), (v, "row"), (g_parts, "row3")], [(c, F32)] * 4,
                    tr=_rows(w.shape[0], 256), name=name)


def _conv_rows(cur, halo, p, first):
    r = cur.shape[0]
    ext = jnp.concatenate([jnp.where(first, 0.0, halo), cur], axis=0)
    s1 = pltpu.roll(ext, 1, 0)[HALO:]
    s2 = pltpu.roll(ext, 2, 0)[HALO:]
    return p[0:1] * s2 + p[1:2] * s1 + p[2:3] * cur + p[3:4], s1, s2


def ffn_gate_fwd(h3, p3, tr=256):
    _, t, c = h3.shape
    half = NDEV // 2

    def body(a_ref, ah_ref, g_ref, gh_ref, pa_ref, pg_ref, o_ref):
        first = pl.program_id(1) == 0
        ya, _, _ = _conv_rows(a_ref[...].astype(F32), ah_ref[...].astype(F32), pa_ref[...], first)
        yg, _, _ = _conv_rows(g_ref[...].astype(F32), gh_ref[...].astype(F32), pg_ref[...], first)
        o_ref[...] = (jax.nn.silu(yg) * ya).astype(o_ref.dtype)

    main = lambda off: pl.BlockSpec((None, tr, c), lambda j, i: (j + off, i, 0))
    halo = lambda off: pl.BlockSpec((None, HALO, c), lambda j, i: (j + off, jnp.maximum(i * (tr // HALO) - 1, 0), 0))
    par = lambda off: pl.BlockSpec((None, 8, c), lambda j, i: (j + off, 0, 0))
    return pl.pallas_call(
        body, grid=(half, t // tr),
        in_specs=[main(0), halo(0), main(half), halo(half), par(0), par(half)],
        out_specs=pl.BlockSpec((None, tr, c), lambda j, i: (j, i, 0)),
        out_shape=jax.ShapeDtypeStruct((half, t, c), BF16), name="ffn_gate_fwd",
        compiler_params=_params(("parallel", "parallel")),
    )(h3, h3, h3, h3, p3, p3)


def ffn_gate_bwd(h3, dgated3, p3, tr=256):
    _, t, c = h3.shape
    half = NDEV // 2

    def body(a_ref, ah_ref, g_ref, gh_ref, dg_ref, pa_ref, pg_ref, o_ref, dp_ref):
        j, i = pl.program_id(0), pl.program_id(1)
        first = i == 0
        a = a_ref[...].astype(F32)
        g = g_ref[...].astype(F32)
        ya, a1, a2 = _conv_rows(a, ah_ref[...].astype(F32), pa_ref[...], first)
        yg, g1, g2 = _conv_rows(g, gh_ref[...].astype(F32), pg_ref[...], first)
        d = dg_ref[...].astype(F32)
        sig = jax.nn.sigmoid(yg)
        d_ya = d * (yg * sig)
        d_yg = d * ya * (sig * (1.0 + yg * (1.0 - sig)))
        is_a = j < half
        dy = jnp.where(is_a, d_ya, d_yg)
        o_ref[...] = dy.astype(o_ref.dtype)
        dyr = dy.astype(o_ref.dtype).astype(F32)
        cur, s1, s2 = jnp.where(is_a, a, g), jnp.where(is_a, a1, g1), jnp.where(is_a, a2, g2)
        rows = [jnp.sum(dyr * s2, axis=0, keepdims=True), jnp.sum(dyr * s1, axis=0, keepdims=True),
                jnp.sum(dyr * cur, axis=0, keepdims=True), jnp.sum(dyr, axis=0, keepdims=True)]
        dp = jnp.concatenate(rows + [jnp.zeros((4, c), F32)], axis=0)

        @pl.when(first)
        def _():
            dp_ref[...] = dp

        @pl.when(i > 0)
        def _():
            dp_ref[...] += dp

    main = lambda f: pl.BlockSpec((None, tr, c), lambda j, i: (f(j), i, 0))
    halo = lambda f: pl.BlockSpec((None, HALO, c), lambda j, i: (f(j), jnp.maximum(i * (tr // HALO) - 1, 0), 0))
    par = lambda f: pl.BlockSpec((None, 8, c), lambda j, i: (f(j), 0, 0))
    fa = lambda j: j % half
    fg = lambda j: j % half + half
    return pl.pallas_call(
        body, grid=(NDEV, t // tr),
        in_specs=[main(fa), halo(fa), main(fg), halo(fg), main(fa), par(fa), par(fg)],
        out_specs=[pl.BlockSpec((None, tr, c), lambda j, i: (j, i, 0)), pl.BlockSpec((None, 8, c), lambda j, i: (j, 0, 0))],
        out_shape=[jax.ShapeDtypeStruct((NDEV, t, c), BF16), jax.ShapeDtypeStruct((NDEV, 8, c), F32)],
        name="ffn_gate_bwd", compiler_params=_params(("parallel", "arbitrary")),
    )(h3, h3, h3, h3, dgated3, p3, p3)


def ffn_conv_t(dy3, p3, tr=256):
    _, t, c = dy3.shape
    nblk = t // tr

    def body(d_ref, dh_ref, p_ref, o_ref):
        last = pl.program_id(1) == nblk - 1
        cur = d_ref[...].astype(F32)
        ext = jnp.concatenate([cur, jnp.where(last, 0.0, dh_ref[...].astype(F32))], axis=0)
        n = tr + HALO
        s1 = pltpu.roll(ext, n - 1, 0)[:tr]
        s2 = pltpu.roll(ext, n - 2, 0)[:tr]
        p = p_ref[...]
        o_ref[...] = (p[2:3] * cur + p[1:2] * s1 + p[0:1] * s2).astype(o_ref.dtype)

    return pl.pallas_call(
        body, grid=(NDEV, nblk),
        in_specs=[pl.BlockSpec((None, tr, c), lambda j, i: (j, i, 0)),
                  pl.BlockSpec((None, HALO, c), lambda j, i: (j, jnp.minimum((i + 1) * (tr // HALO), t // HALO - 1), 0)),
                  pl.BlockSpec((None, 8, c), lambda j, i: (j, 0, 0))],
        out_specs=pl.BlockSpec((None, tr, c), lambda j, i: (j, i, 0)),
        out_shape=jax.ShapeDtypeStruct((NDEV, t, c), BF16), name="ffn_conv_t",
        compiler_params=_params(("parallel", "parallel")),
    )(dy3, dy3, p3)


def _att_consts(bq, bk):
    lane = lax.broadcasted_iota(jnp.int32, (1, LANES), 1)
    heads = (lane < HEAD_DIM, lane >= HEAD_DIM)
    rr = lax.broadcasted_iota(jnp.int32, (bq, bk), 0)
    cc = lax.broadcasted_iota(jnp.int32, (bq, bk), 1)
    kr = lax.broadcasted_iota(jnp.int32, (bk, bk), 0)
    kc = lax.broadcasted_iota(jnp.int32, (bk, bk), 1)
    return heads, rr, cc, kr, kc


def _split_dot(x, tri, parts):
    out = None
    for _ in range(parts):
        piece = x.astype(BF16)
        x = x - piece.astype(F32)
        term = jnp.dot(piece, tri, preferred_element_type=F32)
        out = term if out is None else out + term
    return out


def _att_logits(qh, k):
    z = lax.dot_general(qh, k, _MM_TB, preferred_element_type=F32)
    lsp = jnp.minimum(z, 0.0) - jnp.log(1.0 + jnp.exp(-jnp.abs(z)))
    return lsp, lsp - z


def _per_head(heads, a, b):
    return jnp.where(heads[0], a, b)


def sb_attn_fwd(qkv):
    t, d3 = qkv.shape
    d = d3 // 3
    npair = d // LANES
    bq, bk = min(ATT_BQ, t), min(ATT_BK, t)
    kpq = bq // bk

    def body(q_ref, k_ref, v_ref, o_ref, lt_ref, acc_ref):
        heads, rr, cc, kr, kc = _att_consts(bq, bk)
        suffix = (kr > kc).astype(BF16)

        def trip(qh, k0, valid, runs):
            k = k_ref[pl.ds(k0, bk), :]
            v = v_ref[pl.ds(k0, bk), :]
            new_runs = []
            for h in range(2):
                lsp, lraw = _att_logits(qh[h], k)
                lm = lraw if valid is None else jnp.where(valid, lraw, 0.0)
                w = jnp.exp(lsp + _split_dot(lm, suffix, 2) + runs[h])
                if valid is not None:
                    w = jnp.where(valid, w, 0.0)
                acc_ref[h] += jnp.dot(w.astype(BF16), v, preferred_element_type=F32)
                new_runs.append(runs[h] + jnp.sum(lm, axis=1, keepdims=True))
            return tuple(new_runs)

        def q_loop(qb, _):
            q0 = pl.multiple_of(qb * bq, bq)
            q = q_ref[pl.ds(q0, bq), :] * 0.125
            qh = [jnp.where(hm, q, 0.0).astype(BF16) for hm in heads]
            acc_ref[...] = jnp.zeros_like(acc_ref)
            runs = (jnp.zeros((bq, 1), F32),) * 2
            for dblk in reversed(range(kpq)):
                runs = trip(qh, pl.multiple_of(q0 + dblk * bk, bk), dblk * bk + cc < rr, runs)
            nleft = qb * kpq
            runs = lax.fori_loop(
                0, nleft, lambda i, r: trip(qh, pl.multiple_of((nleft - 1 - i) * bk, bk), None, r), runs)
            o_ref[pl.ds(q0, bq), :] = _per_head(heads, acc_ref[0], acc_ref[1])
            lt_ref[pl.ds(q0, bq), :] = _per_head(heads, runs[0], runs[1])
            return 0

        lax.fori_loop(0, t // bq, q_loop, 0)

    col = lambda off: pl.BlockSpec((t, LANES), lambda p: (0, p + off))
    return pl.pallas_call(
        body, grid=(npair,), in_specs=[col(0), col(npair), col(2 * npair)], out_specs=[col(0), col(0)],
        out_shape=[jax.ShapeDtypeStruct((t, d), F32)] * 2, scratch_shapes=[pltpu.VMEM((2, bq, LANES), F32)],
        name="sb_attn_fwd", compiler_params=_params(("parallel",)),
    )(qkv, qkv, qkv)


def sb_attn_bwd(qkv, ltot, do):
    t, d3 = qkv.shape
    d = d3 // 3
    npair = d // LANES
    bq, bk = min(ATT_BQ, t), min(ATT_BK, t)
    kpq = bq // bk

    def body(q_ref, k_ref, v_ref, lt_ref, do_ref, d_ref, dk_acc, dv_acc, dq_acc):
        heads, rr, cc, kr, kc = _att_consts(bq, bk)
        prefix_incl = (kr <= kc).astype(BF16)
        prefix_excl = (kr < kc).astype(BF16)
        dk_acc[...] = jnp.zeros_like(dk_acc)
        dv_acc[...] = jnp.zeros_like(dv_acc)

        def trip(qh, doh, lt, k0, valid, carry):
            lruns, gruns = carry
            k = k_ref[pl.ds(k0, bk), :]
            v = v_ref[pl.ds(k0, bk), :]
            new_lruns, new_gruns = [], []
            dk_blk = jnp.zeros((bk, LANES), F32)
            dv_blk = jnp.zeros((bk, LANES), F32)
            for h in range(2):
                lsp, lraw = _att_logits(qh[h], k)
                lm = lraw if valid is None else jnp.where(valid, lraw, 0.0)
                right = lt[h] - (lruns[h] + _split_dot(lm, prefix_incl, 2))
                w = jnp.exp(lsp + right)
                if valid is not None:
                    w = jnp.where(valid, w, 0.0)
                g = lax.dot_general(doh[h], v, _MM_TB, preferred_element_type=F32) * w
                left = gruns[h] + _split_dot(g, prefix_excl, 2)
                dz = g * jnp.exp(lraw) - jnp.exp(lsp) * left
                if valid is not None:
                    dz = jnp.where(valid, dz, 0.0)
                dz = dz.astype(BF16)
                kh = jnp.where(heads[h], k, 0.0).astype(BF16)
                dq_acc[h] += jnp.dot(dz, kh, preferred_element_type=F32)
                dk_blk = dk_blk + lax.dot_general(dz, qh[h], _MM_TA, preferred_element_type=F32)
                dv_blk = dv_blk + lax.dot_general(w.astype(BF16), doh[h], _MM_TA, preferred_element_type=F32)
                new_lruns.append(lruns[h] + jnp.sum(lm, axis=1, keepdims=True))
                new_gruns.append(gruns[h] + jnp.sum(g, axis=1, keepdims=True))
            dk_acc[pl.ds(k0, bk), :] += dk_blk
            dv_acc[pl.ds(k0, bk), :] += dv_blk
            return tuple(new_lruns), tuple(new_gruns)

        def q_loop(qb, _):
            q0 = pl.multiple_of(qb * bq, bq)
            q = q_ref[pl.ds(q0, bq), :] * 0.125
            dout = do_ref[pl.ds(q0, bq), :]
            lt2 = lt_ref[pl.ds(q0, bq), :]
            qh = [jnp.where(hm, q, 0.0).astype(BF16) for hm in heads]
            doh = [jnp.where(hm, dout, 0.0).astype(BF16) for hm in heads]
            lt = [jnp.max(jnp.where(hm, lt2, -jnp.inf), axis=1, keepdims=True) for hm in heads]
            dq_acc[...] = jnp.zeros_like(dq_acc)
            col = (jnp.zeros((bq, 1), F32),) * 2
            carry = lax.fori_loop(
                0, qb * kpq, lambda kb, c: trip(qh, doh, lt, pl.multiple_of(kb * bk, bk), None, c), (col, col))
            for dblk in range(kpq):
                carry = trip(qh, doh, lt, pl.multiple_of(q0 + dblk * bk, bk), dblk * bk + cc < rr, carry)
            d_ref[0, pl.ds(q0, bq), :] = ((dq_acc[0] + dq_acc[1]) * 0.125).astype(d_ref.dtype)
            return 0

        lax.fori_loop(0, t // bq, q_loop, 0)
        d_ref[1] = dk_acc[...].astype(d_ref.dtype)
        d_ref[2] = dv_acc[...].astype(d_ref.dtype)

    col = lambda off: pl.BlockSpec((t, LANES), lambda p: (0, p + off))
    return pl.pallas_call(
        body, grid=(npair,), in_specs=[col(0), col(npair), col(2 * npair), col(0), col(0)],
        out_specs=pl.BlockSpec((3, t, LANES), lambda p: (0, 0, p)),
        out_shape=jax.ShapeDtypeStruct((3, t, d), BF16),
        scratch_shapes=[pltpu.VMEM((t, LANES), F32), pltpu.VMEM((t, LANES), F32), pltpu.VMEM((2, bq, LANES), F32)],
        name="sb_attn_bwd", compiler_params=_params(("parallel",)),
    )(qkv, qkv, qkv, ltot, do)


def _sgu_parts(hin, g, ws_ref, bf_ref):
    width = hin.shape[1] // 2
    h = jax.nn.gelu(hin)
    u, v = h[:, :width], h[:, width:]
    r = lax.rsqrt(jnp.mean(v * v, axis=-1, keepdims=True) + EPS)
    vn = v * r * g
    rr = lax.broadcasted_iota(jnp.int32, (CHUNK, CHUNK), 0)
    cc = lax.broadcasted_iota(jnp.int32, (CHUNK, CHUNK), 1)
    causal = cc <= rr
    wcs = [jnp.where(causal, ws_ref[gi], 0.0).astype(BF16) for gi in range(SG_GROUPS)]
    sv = jnp.concatenate(
        [jnp.dot(wcs[gi], vn[:, gi * CHUNK:(gi + 1) * CHUNK].astype(BF16), preferred_element_type=F32) + bf_ref[gi]
         for gi in range(SG_GROUPS)], axis=1)
    return u, v, r, vn, wcs, sv, causal


def sgu_fwd(hin, g, ws, bfull):
    t, w2 = hin.shape
    width = w2 // 2

    def body(h_ref, g_ref, ws_ref, bf_ref, o_ref):
        u, _, _, _, _, sv, _ = _sgu_parts(h_ref[...].astype(F32), g_ref[...], ws_ref, bf_ref)
        o_ref[...] = (u * sv).astype(o_ref.dtype)

    full = lambda a: pl.BlockSpec(a.shape, lambda i, nd=a.ndim: (0,) * nd)
    return pl.pallas_call(
        body, grid=(t // CHUNK,), in_specs=[pl.BlockSpec((CHUNK, w2), lambda i: (i, 0)), full(g), full(ws), full(bfull)],
        out_specs=pl.BlockSpec((CHUNK, width), lambda i: (i, 0)), out_shape=jax.ShapeDtypeStruct((t, width), BF16),
        name="sgu_fwd", compiler_params=_params(("parallel",)),
    )(hin, g, ws, bfull)


def sgu_bwd(hin, dp, g, ws, bfull):
    t, w2 = hin.shape
    width = w2 // 2

    def body(h_ref, dp_ref, g_ref, ws_ref, bf_ref, dh_ref, dws_ref, dbf_ref, dg_ref):
        i = pl.program_id(0)
        hin_v = h_ref[...].astype(F32)
        gv = g_ref[...]
        u, v, r, vn, wcs, sv, causal = _sgu_parts(hin_v, gv, ws_ref, bf_ref)
        dpv = dp_ref[...].astype(F32)
        du = dpv * sv
        dsv = dpv * u
        dvn_parts, dws_parts, dbf_parts = [], [], []
        for gi in range(SG_GROUPS):
            dsv_g = dsv[:, gi * CHUNK:(gi + 1) * CHUNK]
            dsv_b = dsv_g.astype(BF16)
            dvn_parts.append(lax.dot_general(wcs[gi], dsv_b, _MM_TA, preferred_element_type=F32))
            vn_b = vn[:, gi * CHUNK:(gi + 1) * CHUNK].astype(BF16)
            dws_parts.append(jnp.where(causal, lax.dot_general(dsv_b, vn_b, _MM_TB, preferred_element_type=F32), 0.0))
            dbf_parts.append(jnp.broadcast_to(jnp.sum(dsv_g, axis=1, keepdims=True), (CHUNK, CHUNK)))
        dvn = jnp.concatenate(dvn_parts, axis=1)
        dgain = jnp.sum(dvn * v * r, axis=0, keepdims=True)
        gvv = dvn * gv
        dv = r * gvv - v * (r * r * r) * jnp.mean(v * gvv, axis=-1, keepdims=True)
        _, vjp = jax.vjp(jax.nn.gelu, hin_v)
        dh_ref[...] = vjp(jnp.concatenate([du, dv], axis=1))[0].astype(dh_ref.dtype)

        @pl.when(i == 0)
        def _():
            for gi in range(SG_GROUPS):
                dws_ref[gi] = dws_parts[gi]
                dbf_ref[gi] = dbf_parts[gi]
            dg_ref[...] = dgain

        @pl.when(i > 0)
        def _():
            for gi in range(SG_GROUPS):
                dws_ref[gi] += dws_parts[gi]
                dbf_ref[gi] += dbf_parts[gi]
            dg_ref[...] += dgain

    full = lambda a: pl.BlockSpec(a.shape, lambda i, nd=a.ndim: (0,) * nd)
    sq = (SG_GROUPS, CHUNK, CHUNK)
    return pl.pallas_call(
        body, grid=(t // CHUNK,),
        in_specs=[pl.BlockSpec((CHUNK, w2), lambda i: (i, 0)), pl.BlockSpec((CHUNK, width), lambda i: (i, 0)),
                  full(g), full(ws), full(bfull)],
        out_specs=[pl.BlockSpec((CHUNK, w2), lambda i: (i, 0)), pl.BlockSpec(sq, lambda i: (0, 0, 0)),
                   pl.BlockSpec(sq, lambda i: (0, 0, 0)), pl.BlockSpec((1, width), lambda i: (0, 0))],
        out_shape=[jax.ShapeDtypeStruct((t, w2), BF16), jax.ShapeDtypeStruct(sq, F32), jax.ShapeDtypeStruct(sq, F32),
                   jax.ShapeDtypeStruct((1, width), F32)],
        name="sgu_bwd", compiler_params=_params(("arbitrary",)),
    )(hin, dp, g, ws, bfull)


def _disc1(lam_re, lam_im, log_dt):
    lr = jnp.minimum(lam_re, -1e-4)
    li = lam_im
    dt = jnp.exp(log_dt)
    mag = jnp.exp(dt * lr)
    ar = mag * jnp.cos(dt * li)
    ai = mag * jnp.sin(dt * li)
    den = lr * lr + li * li
    return ar, ai, ((ar - 1.0) * lr + ai * li) / den, (ai * lr - (ar - 1.0) * li) / den


def _disc2(cre, cim, b_re, b_im):
    return cre * b_re - cim * b_im, cre * b_im + cim * b_re


def _single(fn, ins, out_shapes, name):
    n = len(ins)

    def body(*refs):
        vals = fn(*[r[...] for r in refs[:n]])
        for ref, val in zip(refs[n:], vals):
            ref[...] = val

    return pl.pallas_call(body, out_shape=[jax.ShapeDtypeStruct(s, F32) for s in out_shapes], name=name)(*ins)


def s5_scan_fwd(bu2, a2):
    _, t, n = bu2.shape
    cb = SCAN_COLS

    def body(bu_ref, a_ref, x_ref):
        ar, ai = a_ref[0:1, :], a_ref[1:2, :]

        def step(b, carry):
            xr, xi = carry
            t0 = pl.multiple_of(b * 8, 8)
            br = bu_ref[0, pl.ds(t0, 8), :]
            bi = bu_ref[1, pl.ds(t0, 8), :]
            rows_r, rows_i = [], []
            for r in range(8):
                xr, xi = ar * xr - ai * xi + br[r:r + 1], ar * xi + ai * xr + bi[r:r + 1]
                rows_r.append(xr)
                rows_i.append(xi)
            x_ref[0, pl.ds(t0, 8), :] = jnp.concatenate(rows_r, axis=0)
            x_ref[1, pl.ds(t0, 8), :] = jnp.concatenate(rows_i, axis=0)
            return xr, xi

        zero = jnp.zeros((1, cb), F32)
        lax.fori_loop(0, t // 8, step, (zero, zero))

    return pl.pallas_call(
        body, grid=(n // cb,),
        in_specs=[pl.BlockSpec((2, t, cb), lambda j: (0, 0, j)), pl.BlockSpec((2, cb), lambda j: (0, j))],
        out_specs=pl.BlockSpec((2, t, cb), lambda j: (0, 0, j)), out_shape=jax.ShapeDtypeStruct((2, t, n), F32),
        name="s5_scan_fwd", compiler_params=_params(("parallel",)),
    )(bu2, a2)


def s5_scan_bwd(dx2, x2, a2):
    _, t, n = dx2.shape
    cb = SCAN_COLS
    nb = t // 8

    def body(dx_ref, x_ref, a_ref, g_ref, da_ref):
        ar, ai = a_ref[0:1, :], a_ref[1:2, :]
        row = lax.broadcasted_iota(jnp.int32, (8, cb), 0)

        def step(s, carry):
            gr, gi, dar, dai = carry
            b = nb - 1 - s
            t0 = pl.multiple_of(b * 8, 8)
            dr = dx_ref[0, pl.ds(t0, 8), :]
            di = dx_ref[1, pl.ds(t0, 8), :]
            rows_r, rows_i = [None] * 8, [None] * 8
            for r in range(7, -1, -1):
                gr, gi = dr[r:r + 1] + ar * gr + ai * gi, di[r:r + 1] - ai * gr + ar * gi
                rows_r[r] = gr
                rows_i[r] = gi
            g8r = jnp.concatenate(rows_r, axis=0)
            g8i = jnp.concatenate(rows_i, axis=0)
            g_ref[0, pl.ds(t0, 8), :] = g8r
            g_ref[1, pl.ds(t0, 8), :] = g8i
            tp = pl.multiple_of(jnp.maximum(t0 - 8, 0), 8)
            keep = b > 0
            prev_r = jnp.where(keep, x_ref[0, pl.ds(tp, 8), :], 0.0)[7:8]
            prev_i = jnp.where(keep, x_ref[1, pl.ds(tp, 8), :], 0.0)[7:8]
            xpr = jnp.where(row == 0, prev_r, pltpu.roll(x_ref[0, pl.ds(t0, 8), :], 1, 0))
            xpi = jnp.where(row == 0, prev_i, pltpu.roll(x_ref[1, pl.ds(t0, 8), :], 1, 0))
            return gr, gi, dar + g8r * xpr + g8i * xpi, dai + g8i * xpr - g8r * xpi

        z1, z8 = jnp.zeros((1, cb), F32), jnp.zeros((8, cb), F32)
        _, _, dar, dai = lax.fori_loop(0, nb, step, (z1, z1, z8, z8))
        da_ref[0:1, :] = jnp.sum(dar, axis=0, keepdims=True)
        da_ref[1:2, :] = jnp.sum(dai, axis=0, keepdims=True)

    blk = pl.BlockSpec((2, t, cb), lambda j: (0, 0, j))
    vec = pl.BlockSpec((2, cb), lambda j: (0, j))
    return pl.pallas_call(
        body, grid=(n // cb,), in_specs=[blk, blk, vec], out_specs=[blk, vec],
        out_shape=[jax.ShapeDtypeStruct((2, t, n), F32), jax.ShapeDtypeStruct((2, n), F32)],
        name="s5_scan_bwd", compiler_params=_params(("parallel",)),
    )(dx2, x2, a2)


_SP_U = SSM_PACK * SSM_GROUP
_SP_X = SSM_PACK * SSM_STATE
_NKB = SSM_GROUPS // SSM_PACK


def mm_s5(kind, a, b, m, name, res=None, tm=512):
    kw = dict(passes=3, name=name)
    xblk = lambda row, sel, col: ((None, tm, _SP_X), lambda *g: (sel(*g), row(*g), col(*g)))
    if kind == "bu":
        o_blk, o_map = xblk(lambda g, i, k: i, lambda g, i, k: g // _NKB, lambda g, i, k: g % _NKB)
        return _mm(a, b, grid=(2 * _NKB, m // tm, 1), a_blk=(tm, _SP_U), a_map=lambda g, i, k: (i, g % _NKB),
                   b_blk=(None, None, _SP_U, _SP_X), b_map=lambda g, i, k: (g // _NKB, g % _NKB, 0, 0),
                   o_blk=o_blk, o_map=o_map, out_shape=(2, m, _NKB * _SP_X), out_dtype=F32, **kw)
    if kind == "yc":
        a_blk, a_map = xblk(lambda j, i, k: i, lambda j, i, k: k, lambda j, i, k: j)
        return _mm(a, b, grid=(_NKB, m // tm, 2), a_blk=a_blk, a_map=a_map,
                   b_blk=(None, None, _SP_X, _SP_U), b_map=lambda j, i, k: (k, j, 0, 0),
                   o_blk=(tm, _SP_U), o_map=lambda j, i, k: (i, j), out_shape=(m, _NKB * _SP_U), out_dtype=F32,
                   acc_2d=(tm, _SP_U), **kw)
    if kind == "dx":
        o_blk, o_map = xblk(lambda g, i, k: i, lambda g, i, k: g // _NKB, lambda g, i, k: g % _NKB)
        return _mm(a, b, grid=(2 * _NKB, m // tm, 1), a_blk=(tm, _SP_U), a_map=lambda g, i, k: (i, g % _NKB),
                   b_blk=(None, None, _SP_X, _SP_U), b_map=lambda g, i, k: (g // _NKB, g % _NKB, 0, 0),
                   o_blk=o_blk, o_map=o_map, out_shape=(2, m, _NKB * _SP_X), out_dtype=F32, dims=_MM_TB, **kw)
    if kind == "dcd":
        a_blk, a_map = xblk(lambda g, _, k: k, lambda g, _, k: g // _NKB, lambda g, _, k: g % _NKB)
        return _mm(a, b, grid=(2 * _NKB, 1, m // tm), a_blk=a_blk, a_map=a_map,
                   b_blk=(tm, _SP_U), b_map=lambda g, _, k: (k, g % _NKB),
                   o_blk=(None, None, _SP_X, _SP_U), o_map=lambda g, _, k: (g // _NKB, g % _NKB, 0, 0),
                   out_shape=(2, _NKB, _SP_X, _SP_U), out_dtype=F32, dims=_MM_TA, acc_2d=(_SP_X, _SP_U), **kw)
    if kind == "du":
        a_blk, a_map = xblk(lambda j, i, k: i, lambda j, i, k: k, lambda j, i, k: j)
        return _mm(a, b, grid=(_NKB, m // tm, 2), a_blk=a_blk, a_map=a_map,
                   b_blk=(None, None, _SP_U, _SP_X), b_map=lambda j, i, k: (k, j, 0, 0),
                   o_blk=(tm, _SP_U), o_map=lambda j, i, k: (i, j), out_shape=(m, _NKB * _SP_U), out_dtype=F32,
                   dims=_MM_TB, acc_2d=(tm, _SP_U), res=res, res_blk=(tm, _SP_U), res_map=lambda j, i, k: (i, j), **kw)
    assert kind == "dbd"
    b_blk, b_map = xblk(lambda g, _, k: k, lambda g, _, k: g // _NKB, lambda g, _, k: g % _NKB)
    return _mm(a, b, grid=(2 * _NKB, 1, m // tm), a_blk=(tm, _SP_U), a_map=lambda g, _, k: (k, g % _NKB),
               b_blk=b_blk, b_map=b_map,
               o_blk=(None, None, _SP_U, _SP_X), o_map=lambda g, _, k: (g // _NKB, g % _NKB, 0, 0),
               out_shape=(2, _NKB, _SP_U, _SP_X), out_dtype=F32, dims=_MM_TA, acc_2d=(_SP_U, _SP_X), **kw)


def _block_diag(w):
    g, a, b = w.shape
    eye = jnp.eye(SSM_PACK, dtype=w.dtype)
    wp = w.reshape(g // SSM_PACK, SSM_PACK, a, b)
    return jnp.einsum("kgab,gh->kgahb", wp, eye).reshape(g // SSM_PACK, SSM_PACK * a, SSM_PACK * b)


def _block_diag_t(d, a, b):
    k = d.shape[0]
    eye = jnp.eye(SSM_PACK, dtype=d.dtype)
    dp = d.reshape(k, SSM_PACK, a, SSM_PACK, b)
    return jnp.einsum("kgahb,gh->kgab", dp, eye).reshape(k * SSM_PACK, a, b)


def _coords():
    return lax.axis_index("x"), lax.axis_index("y"), lax.axis_index("c")


def all_gather(tensors, name):
    n = len(tensors)
    any_spec = pl.BlockSpec(memory_space=pl.ANY)

    def body(*refs):
        ins, outs = refs[:n], refs[n:2 * n]
        send, recv, local = refs[2 * n:]
        x, y, c = _coords()
        me, sibling = (x, y, c), (x, y, 1 - c)
        chips = [(1 - x, y), (x, 1 - y), (1 - x, 1 - y)]

        def slot(p):
            return 4 * p[0] + 2 * p[1] + p[2]

        def copy(t, k, block, to, src=None):
            dst = outs[t].at[slot(block)]
            return pltpu.make_async_remote_copy(
                src_ref=dst if src is None else src, dst_ref=dst, send_sem=send.at[7 * t + k],
                recv_sem=recv.at[7 * t + k], device_id=to, device_id_type=pl.DeviceIdType.MESH)

        own, sent = [], []
        for t in range(n):
            mine = pltpu.make_async_copy(ins[t], outs[t].at[slot(me)], local.at[t])
            mine.start()
            own.append(mine)
            first = [copy(t, 0, me, sibling, src=ins[t])]
            first += [copy(t, 1 + j, me, (*chip, c), src=ins[t]) for j, chip in enumerate(chips)]
            for cp in first:
                cp.start()
            sent += first
        for t in range(n):
            for j, chip in enumerate(chips):
                copy(t, 1 + j, (*chip, c), me).wait_recv()
                passed = copy(t, 4 + j, (*chip, c), sibling)
                passed.start()
                sent.append(passed)
        for t in range(n):
            copy(t, 0, sibling, me).wait_recv()
            for j, chip in enumerate(chips):
                copy(t, 4 + j, (*chip, 1 - c), me).wait_recv()
        for cp in sent:
            cp.wait_send()
        for cp in own:
            cp.wait()

    return pl.pallas_call(
        body, in_specs=[any_spec] * n, out_specs=[any_spec] * n,
        out_shape=[jax.ShapeDtypeStruct((NDEV,) + a.shape, a.dtype) for a in tensors],
        scratch_shapes=[pltpu.SemaphoreType.DMA((7 * n,)), pltpu.SemaphoreType.DMA((7 * n,)),
                        pltpu.SemaphoreType.DMA((n,))],
        name=name,
    )(*tensors)


def all_to_all(groups, name):
    flat = [a for grp in groups for a in grp]
    n = len(flat)
    where = [(gi, li) for gi, grp in enumerate(groups) for li in range(len(grp))]
    any_spec = pl.BlockSpec(memory_space=pl.ANY)

    def body(*refs):
        ins, outs = refs[:n], refs[n:n + len(groups)]
        send, recv, local = refs[n + len(groups):]
        x, y, c = _coords()
        me = 4 * x + 2 * y + c
        waits = []
        for e in range(n):
            gi, li = where[e]
            mine = pltpu.make_async_copy(ins[e].at[me], outs[gi].at[me, li], local.at[e])
            mine.start()
            waits.append(mine)
        sent, landing = [], []
        for rel in range(1, NDEV):
            px = 1 - x if rel & 4 else x
            py = 1 - y if rel & 2 else y
            pc = 1 - c if rel & 1 else c
            peer = 4 * px + 2 * py + pc
            for e in range(n):
                gi, li = where[e]

                def copy(dst_slot, e=e, gi=gi, li=li, rel=rel, peer=peer, to=(px, py, pc)):
                    return pltpu.make_async_remote_copy(
                        src_ref=ins[e].at[peer], dst_ref=outs[gi].at[dst_slot, li], send_sem=send.at[7 * e + rel - 1],
                        recv_sem=recv.at[7 * e + rel - 1], device_id=to, device_id_type=pl.DeviceIdType.MESH)

                cp = copy(me)
                cp.start()
                sent.append(cp)
                landing.append(copy(peer))
        for cp in landing:
            cp.wait_recv()
        for cp in sent:
            cp.wait_send()
        for cp in waits:
            cp.wait()

    return pl.pallas_call(
        body, in_specs=[any_spec] * n, out_specs=[any_spec] * len(groups),
        out_shape=[jax.ShapeDtypeStruct((NDEV, len(grp)) + grp[0].shape[1:], grp[0].dtype) for grp in groups],
        scratch_shapes=[pltpu.SemaphoreType.DMA((7 * n,)), pltpu.SemaphoreType.DMA((7 * n,)),
                        pltpu.SemaphoreType.DMA((n,))],
        name=name,
    )(*flat)


_PACK_QUANTUM = 8 * LANES


def _pack(parts, lead=0):
    out = []
    for p in parts:
        head = p.shape[:lead]
        f = p.astype(F32).reshape(head + (-1,))
        pad = (-f.shape[-1]) % _PACK_QUANTUM
        if pad:
            f = jnp.concatenate([f, jnp.zeros(head + (pad,), F32)], axis=-1)
        out.append(f.reshape(head + (-1, LANES)))
    return jnp.concatenate(out, axis=lead)


def _unpack(buf, shapes):
    head = buf.shape[:-2]
    out, r = [], 0
    for s in shapes:
        n = 1
        for v in s:
            n *= v
        nr = -(-n // _PACK_QUANTUM) * 8
        flat = buf[..., r:r + nr, :].reshape(head + (nr * LANES,))[..., :n]
        out.append(flat.reshape(head + tuple(s)))
        r += nr
    return out


BIG = ("sb_w_qkv", "sb_w_o", "sg_w_in", "sg_w_o", "ssm_w_in", "ssm_w_glu", "ffn_w_up", "ffn_w_down")
SMALL_SHARDED = ("norm_g", "ssm_d", "ffn_conv_w")
REPLICATED = ("final_norm_g", "sg_norm_g", "sg_w_s", "sg_b", "ssm_lam_re", "ssm_lam_im", "ssm_log_dt",
              "ssm_b_re", "ssm_b_im", "ssm_c_re", "ssm_c_im", "ffn_conv_b")
WEIGHTS = ("norm_g", "final_norm_g", "sb_w_qkv", "sb_w_o", "sg_w_in", "sg_norm_g", "sg_w_s", "sg_b", "sg_w_o",
           "ssm_w_in", "ssm_lam_re", "ssm_lam_im", "ssm_log_dt", "ssm_b_re", "ssm_b_im", "ssm_c_re", "ssm_c_im",
           "ssm_d", "ssm_w_glu", "ffn_w_up", "ffn_conv_w", "ffn_conv_b", "ffn_w_down")


def _step(x, loss_target, w, m, v):
    t, d = x.shape[1], x.shape[2]
    depth = w["norm_g"].shape[0]
    x0 = x.reshape(t, d)
    tgt = loss_target.reshape(t, d)

    shard_pack = _pack([w[k] for k in SMALL_SHARDED])
    gathered = all_gather([cast_bf16(w[k]) for k in BIG] + [shard_pack], name="gather_weights")
    wg = dict(zip(BIG, gathered[:-1]))
    ng, sd, cw = _unpack(gathered[-1], [w[k].shape for k in SMALL_SHARDED])
    norm_full = jnp.transpose(ng, (1, 2, 0, 3)).reshape(depth, 2, d)
    ssm_d_full = jnp.transpose(sd, (1, 0, 2)).reshape(1, d)
    nc = cw.shape[-1]
    conv_b3 = w["ffn_conv_b"].reshape(depth, NDEV, nc)
    p3 = [jnp.concatenate([cw[:, l], conv_b3[l][:, None, :], jnp.zeros((NDEV, 8 - CONV_K - 1, nc), F32)], axis=1)
          for l in range(depth)]

    g_, p_, h_ = SSM_GROUPS, SSM_STATE, SSM_GROUP
    lam_re, lam_im = w["ssm_lam_re"][0], w["ssm_lam_im"][0]
    log_dt = w["ssm_log_dt"][0].reshape(g_, 1)
    b_re, b_im = w["ssm_b_re"][0].reshape(g_ * p_, h_), w["ssm_b_im"][0].reshape(g_ * p_, h_)
    ar, ai, cre, cim = _single(_disc1, [lam_re, lam_im, log_dt], [(g_, p_)] * 4, "s5_disc1")
    cre_c, cim_c = cre.reshape(g_ * p_, 1), cim.reshape(g_ * p_, 1)
    bbr, bbi = _single(_disc2, [cre_c, cim_c, b_re, b_im], [(g_ * p_, h_)] * 2, "s5_disc2")
    per_group_t = lambda a, r, c: jnp.swapaxes(a.reshape(g_, r, c), 1, 2)
    bd = jnp.stack([_block_diag(per_group_t(bbr, p_, h_)), _block_diag(per_group_t(bbi, p_, h_))])
    cd = jnp.stack([_block_diag(per_group_t(w["ssm_c_re"][0], h_, p_)),
                    -_block_diag(per_group_t(w["ssm_c_im"][0], h_, p_))])
    a2 = jnp.stack([ar.reshape(g_ * p_), ai.reshape(g_ * p_)])

    sg_gain = w["sg_norm_g"]
    sg_ws = w["sg_w_s"][0]
    sg_bfull = jnp.broadcast_to(w["sg_b"][0][:, :, None], sg_ws.shape)

    acts = []
    xc = x0
    for i in range(depth):
        mixer, j = i % 3, i // 3
        st = {"x": xc}
        g0 = norm_full[i, 0][None]
        xn = rms_fwd(xc, g0, "rms_fwd")
        st["xn"] = xn
        if mixer == 0:
            qkv = mm_cs_fwd(xn, wg["sb_w_qkv"], j, BF16, "qkv_fwd")
            o, ltot = sb_attn_fwd(qkv)
            x1 = mm_rs_fwd(o, wg["sb_w_o"], j, xc, F32, "attn_out_fwd")
            st.update(qkv=qkv, o=o, ltot=ltot)
        elif mixer == 1:
            hin = mm_cs_fwd(xn, wg["sg_w_in"], j, BF16, "sg_in_fwd")
            p = sgu_fwd(hin, sg_gain, sg_ws, sg_bfull)
            x1 = mm_rs_fwd(p, wg["sg_w_o"], j, xc, F32, "sg_out_fwd")
            st.update(hin=hin, p=p)
        else:
            u = mm_rs_fwd(xn, wg["ssm_w_in"], j, None, F32, "ssm_in_fwd")
            bu2 = mm_s5("bu", u, bd, t, "s5_bu")
            x2 = s5_scan_fwd(bu2, a2)
            yc = mm_s5("yc", x2, cd, t, "s5_yc")
            yg = s5_post_fwd(yc, u, ssm_d_full)
            hg = mm_cs_fwd(yg, wg["ssm_w_glu"], j, BF16, "ssm_glu_fwd")
            x1 = glu_fwd(hg, xc)
            st.update(u=u, x2=x2, yc=yc, yg=yg, hg=hg)
        g1 = norm_full[i, 1][None]
        xn2 = rms_fwd(x1, g1, "rms_fwd")
        h3 = mm_cs_fwd(xn2, wg["ffn_w_up"], i, BF16, "ffn_up_fwd")
        gated = ffn_gate_fwd(h3, p3[i])
        xc = mm_down_fwd(gated, wg["ffn_w_down"], i, x1, "ffn_down_fwd")
        st.update(x1=x1, xn2=xn2, h3=h3, gated=gated, g0=g0, g1=g1)
        acts.append(st)

    dx, loss_lanes, d_final_g = loss_head(xc, w["final_norm_g"][None], tgt)
    loss = lax.psum(loss_lanes[0, 0], MESH_AXES)

    gw = {k: [None] * w[k].shape[0] for k in BIG}
    d_norm = [[None, None] for _ in range(depth)]
    d_p3 = [None] * depth
    rep = {}
    d_ssm_d = None
    for i in reversed(range(depth)):
        mixer, j = i % 3, i // 3
        st = acts[i]
        dgated = mm_down_da(dx, wg["ffn_w_down"], i, "ffn_down_da")
        gw["ffn_w_down"][i] = mm_down_dw(st["gated"], dx, "ffn_down_dw")
        dy3, d_p3[i] = ffn_gate_bwd(st["h3"], dgated, p3[i])
        dh3 = ffn_conv_t(dy3, p3[i])
        dxn2 = mm_cs_da(dh3, wg["ffn_w_up"], i, t, "ffn_up_da")
        gw["ffn_w_up"][i] = mm_cs_dw(st["xn2"], dh3, nc, "ffn_up_dw")
        dx1, d_norm[i][1] = rms_bwd(st["x1"], st["g1"], dxn2, dx, "rms_bwd")
        if mixer == 0:
            do = mm_rs_da(dx1, wg["sb_w_o"], j, BF16, "attn_out_da")
            gw["sb_w_o"][j] = mm_rs_dw(st["o"], dx1, "attn_out_dw")
            d3 = sb_attn_bwd(st["qkv"], st["ltot"], do)
            dxn = mm_qkv_da(d3, wg["sb_w_qkv"], j, "qkv_da")
            gw["sb_w_qkv"][j] = mm_qkv_dw(st["xn"], d3, wg["sb_w_qkv"].shape[3], "qkv_dw")
        elif mixer == 1:
            dp = mm_rs_da(dx1, wg["sg_w_o"], j, BF16, "sg_out_da")
            gw["sg_w_o"][j] = mm_rs_dw(st["p"], dx1, "sg_out_dw")
            dhin, d_ws, d_bfull, d_gain = sgu_bwd(st["hin"], dp, sg_gain, sg_ws, sg_bfull)
            rep.update(sg_w_s=d_ws[None], sg_b=d_bfull[None, :, :, 0], sg_norm_g=d_gain)
            dxn = mm_cs_da(dhin, wg["sg_w_in"], j, t, "sg_in_da")
            gw["sg_w_in"][j] = mm_cs_dw(st["xn"], dhin, wg["sg_w_in"].shape[3], "sg_in_dw")
        else:
            dhg = glu_bwd(st["hg"], dx1)
            dyg = mm_cs_da(dhg, wg["ssm_w_glu"], j, t, "ssm_glu_da")
            gw["ssm_w_glu"][j] = mm_cs_dw(st["yg"], dhg, wg["ssm_w_glu"].shape[3], "ssm_glu_dw")
            dyc, du_skip, d_ssm_d = s5_post_bwd(st["yc"], st["u"], ssm_d_full, dyg)
            dx2 = mm_s5("dx", dyc, cd, t, "s5_dx")
            dcd = mm_s5("dcd", st["x2"], dyc, t, "s5_dcd")
            g2, da2 = s5_scan_bwd(dx2, st["x2"], a2)
            du = mm_s5("du", g2, bd, t, "s5_du", res=du_skip)
            dbd = mm_s5("dbd", st["u"], g2, t, "s5_dbd")
            from_bd = lambda blk: jnp.swapaxes(_block_diag_t(blk, h_, p_), 1, 2).reshape(g_ * p_, h_)

            def disc2_bwd(c1, c2, b1, b2, t1, t2):
                return jax.vjp(_disc2, c1, c2, b1, b2)[1]((t1, t2))

            d_cre, d_cim, d_b_re, d_b_im = _single(
                disc2_bwd, [cre_c, cim_c, b_re, b_im, from_bd(dbd[0]), from_bd(dbd[1])],
                [(g_ * p_, 1)] * 2 + [(g_ * p_, h_)] * 2, "s5_disc2_bwd")

            def disc1_bwd(l1, l2, ld, t1, t2, t3, t4):
                return jax.vjp(_disc1, l1, l2, ld)[1]((t1, t2, t3, t4))

            d_lam_re, d_lam_im, d_log_dt = _single(
                disc1_bwd, [lam_re, lam_im, log_dt, da2[0].reshape(g_, p_), da2[1].reshape(g_, p_),
                            d_cre.reshape(g_, p_), d_cim.reshape(g_, p_)],
                [(g_, p_), (g_, p_), (g_, 1)], "s5_disc1_bwd")
            from_cd = lambda blk: jnp.swapaxes(_block_diag_t(blk, p_, h_), 1, 2)
            rep.update(ssm_lam_re=d_lam_re[None], ssm_lam_im=d_lam_im[None], ssm_log_dt=d_log_dt.reshape(1, g_),
                       ssm_b_re=d_b_re.reshape(1, g_, p_, h_), ssm_b_im=d_b_im.reshape(1, g_, p_, h_),
                       ssm_c_re=from_cd(dcd[0])[None], ssm_c_im=-from_cd(dcd[1])[None])
            dxn = mm_rs_da(du, wg["ssm_w_in"], j, F32, "ssm_in_da")
            gw["ssm_w_in"][j] = mm_rs_dw(st["xn"], du, "ssm_in_dw")
        dx, d_norm[i][0] = rms_bwd(st["x"], st["g0"], dxn, dx1, "rms_bwd")

    rep["final_norm_g"] = d_final_g.reshape(d)
    rep["ffn_conv_b"] = jnp.stack([d_p3[l][:, CONV_K, :].reshape(NDEV * nc) for l in range(depth)])

    d_norm_full = jnp.stack([jnp.concatenate(pair, axis=0) for pair in d_norm])
    d_norm_pieces = jnp.transpose(d_norm_full.reshape(depth, 2, NDEV, d // NDEV), (2, 0, 1, 3))
    d_ssm_d_pieces = jnp.transpose(d_ssm_d.reshape(1, NDEV, d // NDEV), (1, 0, 2))
    d_conv_w_pieces = jnp.stack([d_p3[l][:, :CONV_K, :] for l in range(depth)], axis=1)
    small_pieces = _pack([d_norm_pieces, d_ssm_d_pieces, d_conv_w_pieces], lead=1)
    received = all_to_all([gw[k] for k in BIG] + [[small_pieces]], name="scatter_grads")
    rep_parts, = all_gather([_pack([rep[k] for k in REPLICATED])], name="gather_small_grads")

    grads, deltas, new_m, new_v = {}, {}, {}, {}
    for k, parts in zip(BIG, received[:-1]):
        shape = w[k].shape
        cols = shape[-1]
        res = adamw(w[k].reshape(-1, cols), m[k].reshape(-1, cols), v[k].reshape(-1, cols),
                    parts.reshape(NDEV, -1, cols), "adamw")
        grads[k], deltas[k], new_m[k], new_v[k] = [r.reshape(shape) for r in res]
    for names, parts in ((SMALL_SHARDED, received[-1][:, 0]), (REPLICATED, rep_parts)):
        res = adamw(_pack([w[k] for k in names]), _pack([m[k] for k in names]), _pack([v[k] for k in names]), parts,
                    "adamw_small")
        for tree, buf in zip((grads, deltas, new_m, new_v), res):
            for k, val in zip(names, _unpack(buf, [w[k].shape for k in names])):
                tree[k] = val
    grad_x = dx.reshape(x.shape)
    return (loss, grad_x, *[grads[k] for k in WEIGHTS], *[deltas[k] for k in WEIGHTS],
            *[new_m[k] for k in WEIGHTS], *[new_v[k] for k in WEIGHTS])


def kernel(x, norm_g, final_norm_g, sb_w_qkv, sb_w_o, sg_w_in, sg_norm_g, sg_w_s, sg_b, sg_w_o, ssm_w_in, ssm_lam_re, ssm_lam_im, ssm_log_dt, ssm_b_re, ssm_b_im, ssm_c_re, ssm_c_im, ssm_d, ssm_w_glu, ffn_w_up, ffn_conv_w, ffn_conv_b, ffn_w_down, loss_target, m_norm_g, m_final_norm_g, m_sb_w_qkv, m_sb_w_o, m_sg_w_in, m_sg_norm_g, m_sg_w_s, m_sg_b, m_sg_w_o, m_ssm_w_in, m_ssm_lam_re, m_ssm_lam_im, m_ssm_log_dt, m_ssm_b_re, m_ssm_b_im, m_ssm_c_re, m_ssm_c_im, m_ssm_d, m_ssm_w_glu, m_ffn_w_up, m_ffn_conv_w, m_ffn_conv_b, m_ffn_w_down, v_norm_g, v_final_norm_g, v_sb_w_qkv, v_sb_w_o, v_sg_w_in, v_sg_norm_g, v_sg_w_s, v_sg_b, v_sg_w_o, v_ssm_w_in, v_ssm_lam_re, v_ssm_lam_im, v_ssm_log_dt, v_ssm_b_re, v_ssm_b_im, v_ssm_c_re, v_ssm_c_im, v_ssm_d, v_ssm_w_glu, v_ffn_w_up, v_ffn_conv_w, v_ffn_conv_b, v_ffn_w_down):
    w = dict(zip(WEIGHTS, (norm_g, final_norm_g, sb_w_qkv, sb_w_o, sg_w_in, sg_norm_g, sg_w_s, sg_b, sg_w_o, ssm_w_in,
                           ssm_lam_re, ssm_lam_im, ssm_log_dt, ssm_b_re, ssm_b_im, ssm_c_re, ssm_c_im, ssm_d, ssm_w_glu,
                           ffn_w_up, ffn_conv_w, ffn_conv_b, ffn_w_down)))
    m = dict(zip(WEIGHTS, (m_norm_g, m_final_norm_g, m_sb_w_qkv, m_sb_w_o, m_sg_w_in, m_sg_norm_g, m_sg_w_s, m_sg_b,
                           m_sg_w_o, m_ssm_w_in, m_ssm_lam_re, m_ssm_lam_im, m_ssm_log_dt, m_ssm_b_re, m_ssm_b_im,
                           m_ssm_c_re, m_ssm_c_im, m_ssm_d, m_ssm_w_glu, m_ffn_w_up, m_ffn_conv_w, m_ffn_conv_b,
                           m_ffn_w_down)))
    v = dict(zip(WEIGHTS, (v_norm_g, v_final_norm_g, v_sb_w_qkv, v_sb_w_o, v_sg_w_in, v_sg_norm_g, v_sg_w_s, v_sg_b,
                           v_sg_w_o, v_ssm_w_in, v_ssm_lam_re, v_ssm_lam_im, v_ssm_log_dt, v_ssm_b_re, v_ssm_b_im,
                           v_ssm_c_re, v_ssm_c_im, v_ssm_d, v_ssm_w_glu, v_ffn_w_up, v_ffn_conv_w, v_ffn_conv_b,
                           v_ffn_w_down)))
    return _step(x, loss_target, w, m, v)
```

```python
import functools

import jax
import jax.numpy as jnp
from jax import lax
from jax.experimental import pallas as pl
from jax.experimental.pallas import tpu as pltpu

F32, BF16 = jnp.float32, jnp.bfloat16
MESH_AXES = ("x", "y", "c")
NDEV = 8
EPS = 1e-6
HEAD_DIM = 64
LANES = 128
ATT_BQ, ATT_BK = 512, 256
CHUNK = 128
SG_GROUPS = 8
SSM_GROUPS, SSM_STATE, SSM_GROUP = 64, 64, 16
SSM_PACK = 8
CONV_K = 3
HALO = 16
SCAN_COLS = 256
ADAM_LR, ADAM_B1, ADAM_B2, ADAM_EPS, ADAM_WD, ADAM_STEP = 0.001, 0.9, 0.999, 1e-08, 0.01, 10
VMEM_LIMIT = 56 * 1024 * 1024

_MM = (((1,), (0,)), ((), ()))
_MM_TB = (((1,), (1,)), ((), ()))
_MM_TA = (((0,), (0,)), ((), ()))


def _params(sem):
    return pltpu.CompilerParams(dimension_semantics=sem, vmem_limit_bytes=VMEM_LIMIT)


def _rows(total, cap, mult=16):
    best = None
    for d in range(mult, min(total, cap) + 1, mult):
        if total % d == 0:
            best = d
    return best if best is not None else total


def _dot(a, b, dims, passes):
    if passes == 1:
        return lax.dot_general(a.astype(BF16), b.astype(BF16), dims, preferred_element_type=F32)
    a = a.astype(F32)
    b = b.astype(F32)
    ah = a.astype(BF16)
    bh = b.astype(BF16)
    al = (a - ah.astype(F32)).astype(BF16)
    bl = (b - bh.astype(F32)).astype(BF16)
    out = lax.dot_general(ah, bh, dims, preferred_element_type=F32)
    out = out + lax.dot_general(al, bh, dims, preferred_element_type=F32)
    return out + lax.dot_general(ah, bl, dims, preferred_element_type=F32)


def _mm(a, b, *, grid, a_blk, a_map, b_blk, b_map, o_blk, o_map, out_shape, out_dtype, name,
        dims=_MM, passes=1, res=None, res_blk=None, res_map=None, b_2d=None, acc_2d=None, dep=None):
    nk = grid[2]
    has_res = res is not None
    n_in = 2 + has_res + (dep is not None)

    def body(*refs):
        a_ref, b_ref, o_ref = refs[0], refs[1], refs[n_in]
        r_ref = refs[2] if has_res else None
        bv = b_ref[...]
        if b_2d is not None:
            bv = bv.reshape(b_2d)
        part = _dot(a_ref[...], bv, dims, passes)

        def finish(total):
            if has_res:
                total = total + r_ref[...].astype(F32)
            o_ref[...] = total.reshape(o_ref.shape).astype(o_ref.dtype)

        if nk == 1:
            finish(part)
        else:
            acc_ref = refs[-1]
            k = pl.program_id(2)

            @pl.when(k == 0)
            def _():
                acc_ref[...] = part

            @pl.when(k > 0)
            def _():
                acc_ref[...] += part

            @pl.when(k == nk - 1)
            def _():
                finish(acc_ref[...])

    in_specs = [pl.BlockSpec(a_blk, a_map), pl.BlockSpec(b_blk, b_map)]
    args = [a, b]
    if has_res:
        in_specs.append(pl.BlockSpec(res_blk, res_map))
        args.append(res)
    if dep is not None:
        in_specs.append(pl.BlockSpec(memory_space=pl.ANY))
        args.append(dep)
    scratch = [pltpu.VMEM(acc_2d, F32)] if nk > 1 else []
    return pl.pallas_call(
        body, grid=grid, in_specs=in_specs, out_specs=pl.BlockSpec(o_blk, o_map),
        out_shape=jax.ShapeDtypeStruct(out_shape, out_dtype), scratch_shapes=scratch, name=name,
        compiler_params=_params(("parallel", "parallel", "arbitrary")),
    )(*args)


def _cs_act_spec(ns, tm, row_of, col_of):
    if ns % LANES == 0:
        return (tm, ns), lambda *g: (row_of(*g), col_of(*g))
    return (None, tm, ns), lambda *g: (col_of(*g), row_of(*g), 0)


def mm_cs_fwd(a, w4, l, out_dtype, name, tm=512):
    m, k = a.shape
    ns = w4.shape[3]
    o_blk, o_map = _cs_act_spec(ns, tm, lambda j, i, kk: i, lambda j, i, kk: j)
    out_shape = (m, NDEV * ns) if ns % LANES == 0 else (NDEV, m, ns)
    return _mm(a, w4, grid=(NDEV, m // tm, 1), a_blk=(tm, k), a_map=lambda j, i, kk: (i, 0),
               b_blk=(None, None, k, ns), b_map=lambda j, i, kk: (j, l, 0, 0),
               o_blk=o_blk, o_map=o_map, out_shape=out_shape, out_dtype=out_dtype, name=name)


def mm_cs_da(dc, w4, l, m, name, tm=512):
    k, ns = w4.shape[2], w4.shape[3]
    a_blk, a_map = _cs_act_spec(ns, tm, lambda i, _, j: i, lambda i, _, j: j)
    return _mm(dc, w4, grid=(m // tm, 1, NDEV), a_blk=a_blk, a_map=a_map,
               b_blk=(None, None, k, ns), b_map=lambda i, _, j: (j, l, 0, 0),
               o_blk=(tm, k), o_map=lambda i, _, j: (i, 0), out_shape=(m, k), out_dtype=F32,
               dims=_MM_TB, acc_2d=(tm, k), name=name)


def mm_cs_dw(a, dc, ns, name, tk=512):
    m, k = a.shape
    b_blk, b_map = _cs_act_spec(ns, tk, lambda j, _, kk: kk, lambda j, _, kk: j)
    return _mm(a, dc, grid=(NDEV, 1, m // tk), a_blk=(tk, k), a_map=lambda j, _, kk: (kk, 0),
               b_blk=b_blk, b_map=b_map, o_blk=(None, k, ns), o_map=lambda j, _, kk: (j, 0, 0),
               out_shape=(NDEV, k, ns), out_dtype=BF16, dims=_MM_TA, acc_2d=(k, ns), name=name)


def mm_rs_fwd(a, w4, l, res, out_dtype, name, tm=512):
    m, k = a.shape
    ks, n = w4.shape[2], w4.shape[3]
    return _mm(a, w4, grid=(m // tm, 1, 1), a_blk=(tm, k), a_map=lambda i, _, kk: (i, 0),
               b_blk=(NDEV, None, ks, n), b_map=lambda i, _, kk: (0, l, 0, 0), b_2d=(k, n),
               o_blk=(tm, n), o_map=lambda i, _, kk: (i, 0), out_shape=(m, n), out_dtype=out_dtype,
               res=res, res_blk=(tm, n), res_map=lambda i, _, kk: (i, 0), name=name)


def mm_rs_da(dc, w4, l, out_dtype, name, tm=512, dep=None):
    m, n = dc.shape
    ks = w4.shape[2]
    k = NDEV * ks
    return _mm(dc, w4, grid=(m // tm, 1, 1), a_blk=(tm, n), a_map=lambda i, _, kk: (i, 0),
               b_blk=(NDEV, None, ks, n), b_map=lambda i, _, kk: (0, l, 0, 0), b_2d=(k, n),
               o_blk=(tm, k), o_map=lambda i, _, kk: (i, 0), out_shape=(m, k), out_dtype=out_dtype,
               dims=_MM_TB, name=name, dep=dep)


def mm_rs_dw(a, dc, name, tk=512):
    m, k = a.shape
    n = dc.shape[1]
    ks = k // NDEV
    return _mm(a, dc, grid=(1, 1, m // tk), a_blk=(tk, k), a_map=lambda _, __, kk: (kk, 0),
               b_blk=(tk, n), b_map=lambda _, __, kk: (kk, 0),
               o_blk=(NDEV, ks, n), o_map=lambda _, __, kk: (0, 0, 0), out_shape=(NDEV, ks, n),
               out_dtype=BF16, dims=_MM_TA, acc_2d=(k, n), name=name)


def mm_down_fwd(a3, w4, l, res, name, tm=512):
    nj, m, kc = a3.shape
    ks, n = w4.shape[2], w4.shape[3]
    return _mm(a3, w4, grid=(m // tm, 1, nj), a_blk=(None, tm, kc), a_map=lambda i, _, j: (j, i, 0),
               b_blk=(2, None, ks, n), b_map=lambda i, _, j: (j, l, 0, 0), b_2d=(kc, n),
               o_blk=(tm, n), o_map=lambda i, _, j: (i, 0), out_shape=(m, n), out_dtype=F32,
               res=res, res_blk=(tm, n), res_map=lambda i, _, j: (i, 0), acc_2d=(tm, n), name=name)


def mm_down_da(dc, w4, l, name, tm=512, dep=None):
    m, n = dc.shape
    ks = w4.shape[2]
    kc = 2 * ks
    nj = NDEV // 2
    return _mm(dc, w4, grid=(nj, m // tm, 1), a_blk=(tm, n), a_map=lambda j, i, _: (i, 0),
               b_blk=(2, None, ks, n), b_map=lambda j, i, _: (j, l, 0, 0), b_2d=(kc, n),
               o_blk=(None, tm, kc), o_map=lambda j, i, _: (j, i, 0), out_shape=(nj, m, kc),
               out_dtype=BF16, dims=_MM_TB, name=name, dep=dep)


def mm_down_dw(a3, dc, name, tk=512):
    nj, m, kc = a3.shape
    n = dc.shape[1]
    return _mm(a3, dc, grid=(nj, 1, m // tk), a_blk=(None, tk, kc), a_map=lambda j, _, kk: (j, kk, 0),
               b_blk=(tk, n), b_map=lambda j, _, kk: (kk, 0),
               o_blk=(2, kc // 2, n), o_map=lambda j, _, kk: (j, 0, 0), out_shape=(NDEV, kc // 2, n),
               out_dtype=BF16, dims=_MM_TA, acc_2d=(kc, n), name=name)


def mm_qkv_da(d3, w4, l, name, tm=512):
    _, m, d = d3.shape
    k, ns = w4.shape[2], w4.shape[3]
    per_arr, per_piece = d // LANES, ns // LANES
    ngroups = 3 * per_arr
    return _mm(d3, w4, grid=(m // tm, 1, ngroups),
               a_blk=(None, tm, LANES), a_map=lambda i, _, g: (g // per_arr, i, g % per_arr),
               b_blk=(None, None, k, LANES), b_map=lambda i, _, g: (g // per_piece, l, 0, g % per_piece),
               o_blk=(tm, k), o_map=lambda i, _, g: (i, 0), out_shape=(m, k), out_dtype=F32,
               dims=_MM_TB, acc_2d=(tm, k), name=name)


def mm_qkv_dw(a, d3, ns, name, tk=512):
    m, k = a.shape
    d = d3.shape[2]
    per_arr, per_piece = d // LANES, ns // LANES
    ngroups = 3 * per_arr
    return _mm(a, d3, grid=(ngroups, 1, m // tk), a_blk=(tk, k), a_map=lambda g, _, kk: (kk, 0),
               b_blk=(None, tk, LANES), b_map=lambda g, _, kk: (g // per_arr, kk, g % per_arr),
               o_blk=(None, k, LANES), o_map=lambda g, _, kk: (g // per_piece, 0, g % per_piece),
               out_shape=(NDEV, k, ns), out_dtype=BF16, dims=_MM_TA, acc_2d=(k, LANES), name=name)


def _rowwise(fn, ins, outs, *, tr, name, acc_outs=()):
    rows = next(a.shape[0] if kind == "row" else a.shape[1] for a, kind in ins if kind in ("row", "row3"))
    n_in, n_out = len(ins), len(outs)
    n_read = sum(kind != "dep" for _, kind in ins)

    def body(*refs):
        vals = fn(*[r[...] for r in refs[:n_read]])
        if not isinstance(vals, (tuple, list)):
            vals = (vals,)
        for ref, val in zip(refs[n_in:n_in + n_out], vals[:n_out]):
            ref[...] = val.astype(ref.dtype)
        i = pl.program_id(0)
        for ref, val in zip(refs[n_in + n_out:], vals[n_out:]):
            val = val.astype(ref.dtype)

            @pl.when(i == 0)
            def _(ref=ref, val=val):
                ref[...] = val

            @pl.when(i > 0)
            def _(ref=ref, val=val):
                ref[...] += val

    in_specs = []
    for a, kind in ins:
        if kind == "row":
            in_specs.append(pl.BlockSpec((tr, a.shape[1]), lambda i: (i, 0)))
        elif kind == "row3":
            in_specs.append(pl.BlockSpec((a.shape[0], tr, a.shape[2]), lambda i: (0, i, 0)))
        elif kind == "dep":
            in_specs.append(pl.BlockSpec(memory_space=pl.ANY))
        else:
            in_specs.append(pl.BlockSpec(a.shape, lambda i, nd=a.ndim: (0,) * nd))
    out_specs = [pl.BlockSpec((tr, c), lambda i: (i, 0)) for c, _ in outs]
    out_specs += [pl.BlockSpec(s, lambda i, nd=len(s): (0,) * nd) for s, _ in acc_outs]
    out_shape = [jax.ShapeDtypeStruct((rows, c), dt) for c, dt in outs]
    out_shape += [jax.ShapeDtypeStruct(s, dt) for s, dt in acc_outs]
    res = pl.pallas_call(
        body, grid=(rows // tr,), in_specs=in_specs, out_specs=out_specs, out_shape=out_shape, name=name,
        compiler_params=_params(("arbitrary",) if acc_outs else ("parallel",)),
    )(*[a for a, _ in ins])
    return res


def _rms(x, g):
    return x * lax.rsqrt(jnp.mean(x * x, axis=-1, keepdims=True) + EPS) * g


def cast_into_slot(w, l, me):
    _, r, c = w.shape
    tr = _rows(r, 512)

    def body(me_ref, w_ref, o_ref):
        o_ref[...] = w_ref[...].astype(o_ref.dtype)

    return pl.pallas_call(
        body,
        grid_spec=pltpu.PrefetchScalarGridSpec(
            num_scalar_prefetch=1, grid=(r // tr,),
            in_specs=[pl.BlockSpec((None, tr, c), lambda i, me_ref: (l, i, 0))],
            out_specs=pl.BlockSpec((None, tr, c), lambda i, me_ref: (me_ref[0], i, 0))),
        out_shape=jax.ShapeDtypeStruct((NDEV, r, c), BF16), name="cast_into_slot",
        compiler_params=_params(("parallel",)),
    )(me, w)


def rms_fwd(x, g, name):
    out, = _rowwise(_rms, [(x, "row"), (g, "full")], [(x.shape[1], BF16)], tr=256, name=name)
    return out


def rms_bwd(x, g, dy, dres, name):
    def fn(xv, gv, dyv, drv):
        _, vjp = jax.vjp(_rms, xv, gv)
        dx, dg = vjp(dyv.astype(F32))
        return drv + dx, dg

    d = x.shape[1]
    return _rowwise(fn, [(x, "row"), (g, "full"), (dy, "row"), (dres, "row")], [(d, F32)], tr=256, name=name,
                    acc_outs=[((1, d), F32)])


def loss_head(x, g, tgt):
    def f(xv, gv, tv):
        err = jnp.square(_rms(xv, gv) - tv)
        return 0.5 * jnp.sum(jnp.mean(err, axis=-1))

    def fn(xv, gv, tv):
        val, (dx, dg) = jax.value_and_grad(f, argnums=(0, 1))(xv, gv, tv)
        return dx, jnp.full((1, LANES), val, F32), dg

    d = x.shape[1]
    return _rowwise(fn, [(x, "row"), (g, "full"), (tgt, "row")], [(d, F32)], tr=256, name="loss_head",
                    acc_outs=[((1, LANES), F32), ((1, d), F32)])


def _glu(hg, x):
    half = hg.shape[1] // 2
    return x + hg[:, :half] * jax.nn.sigmoid(hg[:, half:])


def glu_fwd(hg, x):
    out, = _rowwise(lambda h, xv: _glu(h.astype(F32), xv), [(hg, "row"), (x, "row")], [(x.shape[1], F32)],
                    tr=256, name="glu_fwd")
    return out


def glu_bwd(hg, dx1, dep):
    def fn(h, d):
        _, vjp = jax.vjp(lambda hv: _glu(hv, jnp.zeros_like(d)), h.astype(F32))
        return vjp(d)[0]

    out, = _rowwise(fn, [(hg, "row"), (dx1, "row"), (dep, "dep")], [(hg.shape[1], BF16)], tr=256, name="glu_bwd")
    return out


def _s5_post(yc, u, d):
    return jax.nn.gelu(yc + d * u)


def s5_post_fwd(yc, u, d):
    out, = _rowwise(_s5_post, [(yc, "row"), (u, "row"), (d, "full")], [(yc.shape[1], BF16)], tr=256,
                    name="s5_post_fwd")
    return out


def s5_post_bwd(yc, u, d, dyg):
    def fn(ycv, uv, dv, g):
        _, vjp = jax.vjp(_s5_post, ycv, uv, dv)
        return vjp(g.astype(F32))

    dm = yc.shape[1]
    return _rowwise(fn, [(yc, "row"), (u, "row"), (d, "full"), (dyg, "row")], [(dm, F32), (dm, F32)], tr=256,
                    name="s5_post_bwd", acc_outs=[((1, dm), F32)])


def _adam_update(wv, mv, vv, g):
    m2 = ADAM_B1 * mv + (1.0 - ADAM_B1) * g
    v2 = ADAM_B2 * vv + (1.0 - ADAM_B2) * jnp.square(g)
    m_hat = m2 / (1.0 - ADAM_B1 ** ADAM_STEP)
    v_hat = v2 / (1.0 - ADAM_B2 ** ADAM_STEP)
    delta = -ADAM_LR * (m_hat / (jnp.sqrt(v_hat) + ADAM_EPS) + ADAM_WD * wv)
    return g, delta, m2, v2


def adamw(w, m, v, g_parts, name):
    def fn(wv, mv, vv, gp):
        g = gp[0].astype(F32)
        for p in range(1, gp.shape[0]):
            g = g + gp[p].astype(F32)
        return _adam_update(wv, mv, vv, g)

    c = w.shape[1]
    return _rowwise(fn, [(w, "row"), (m, "row"), (v, "row"), (g_parts, "row3")], [(c, F32)] * 4,
                    tr=_rows(w.shape[0], 256), name=name)


def adamw_layers(w, m, v, lands, owns, me, name):
    nl, r, c = w.shape
    tr = _rows(r, 256)

    def body(me_ref, w_ref, m_ref, v_ref, *rest):
        land_refs, own_refs, out_refs = rest[:nl], rest[nl:2 * nl], rest[2 * nl:]
        for l in range(nl):
            @pl.when(pl.program_id(0) == l)
            def _(l=l):
                g = own_refs[l][...].astype(F32)
                for p in range(NDEV - 1):
                    g = g + land_refs[l][p].astype(F32)
                for ref, val in zip(out_refs, _adam_update(w_ref[...], m_ref[...], v_ref[...], g)):
                    ref[...] = val

    def rows_of(l):
        return lambda li, i, me_ref: jnp.where(li == l, i, 0)

    wspec = pl.BlockSpec((None, tr, c), lambda li, i, me_ref: (li, i, 0))
    in_specs = [wspec] * 3
    in_specs += [pl.BlockSpec((NDEV - 1, tr, c), lambda li, i, me_ref, f=rows_of(l): (0, f(li, i, me_ref), 0))
                 for l in range(nl)]
    in_specs += [pl.BlockSpec((None, tr, c), lambda li, i, me_ref, f=rows_of(l): (me_ref[0], f(li, i, me_ref), 0))
                 for l in range(nl)]
    return pl.pallas_call(
        body,
        grid_spec=pltpu.PrefetchScalarGridSpec(
            num_scalar_prefetch=1, grid=(nl, r // tr), in_specs=in_specs, out_specs=[wspec] * 4),
        out_shape=[jax.ShapeDtypeStruct(w.shape, F32)] * 4, name=name, compiler_params=_params(("parallel", "parallel")),
    )(me, w, m, v, *lands, *owns)


def _conv_rows(cur, halo, p, first):
    r = cur.shape[0]
    ext = jnp.concatenate([jnp.where(first, 0.0, halo), cur], axis=0)
    s1 = pltpu.roll(ext, 1, 0)[HALO:]
    s2 = pltpu.roll(ext, 2, 0)[HALO:]
    return p[0:1] * s2 + p[1:2] * s1 + p[2:3] * cur + p[3:4], s1, s2


def ffn_gate_fwd(h3, p3, tr=256):
    _, t, c = h3.shape
    half = NDEV // 2

    def body(a_ref, ah_ref, g_ref, gh_ref, pa_ref, pg_ref, o_ref):
        first = pl.program_id(1) == 0
        ya, _, _ = _conv_rows(a_ref[...].astype(F32), ah_ref[...].astype(F32), pa_ref[...], first)
        yg, _, _ = _conv_rows(g_ref[...].astype(F32), gh_ref[...].astype(F32), pg_ref[...], first)
        o_ref[...] = (jax.nn.silu(yg) * ya).astype(o_ref.dtype)

    main = lambda off: pl.BlockSpec((None, tr, c), lambda j, i: (j + off, i, 0))
    halo = lambda off: pl.BlockSpec((None, HALO, c), lambda j, i: (j + off, jnp.maximum(i * (tr // HALO) - 1, 0), 0))
    par = lambda off: pl.BlockSpec((None, 8, c), lambda j, i: (j + off, 0, 0))
    return pl.pallas_call(
        body, grid=(half, t // tr),
        in_specs=[main(0), halo(0), main(half), halo(half), par(0), par(half)],
        out_specs=pl.BlockSpec((None, tr, c), lambda j, i: (j, i, 0)),
        out_shape=jax.ShapeDtypeStruct((half, t, c), BF16), name="ffn_gate_fwd",
        compiler_params=_params(("parallel", "parallel")),
    )(h3, h3, h3, h3, p3, p3)


def ffn_gate_bwd(h3, dgated3, p3, tr=256):
    _, t, c = h3.shape
    half = NDEV // 2

    def body(a_ref, ah_ref, g_ref, gh_ref, dg_ref, pa_ref, pg_ref, o_ref, dp_ref):
        j, i = pl.program_id(0), pl.program_id(1)
        first = i == 0
        a = a_ref[...].astype(F32)
        g = g_ref[...].astype(F32)
        ya, a1, a2 = _conv_rows(a, ah_ref[...].astype(F32), pa_ref[...], first)
        yg, g1, g2 = _conv_rows(g, gh_ref[...].astype(F32), pg_ref[...], first)
        d = dg_ref[...].astype(F32)
        sig = jax.nn.sigmoid(yg)
        d_ya = d * (yg * sig)
        d_yg = d * ya * (sig * (1.0 + yg * (1.0 - sig)))
        is_a = j < half
        dy = jnp.where(is_a, d_ya, d_yg)
        o_ref[...] = dy.astype(o_ref.dtype)
        dyr = dy.astype(o_ref.dtype).astype(F32)
        cur, s1, s2 = jnp.where(is_a, a, g), jnp.where(is_a, a1, g1), jnp.where(is_a, a2, g2)
        rows = [jnp.sum(dyr * s2, axis=0, keepdims=True), jnp.sum(dyr * s1, axis=0, keepdims=True),
                jnp.sum(dyr * cur, axis=0, keepdims=True), jnp.sum(dyr, axis=0, keepdims=True)]
        dp = jnp.concatenate(rows + [jnp.zeros((4, c), F32)], axis=0)

        @pl.when(first)
        def _():
            dp_ref[...] = dp

        @pl.when(i > 0)
        def _():
            dp_ref[...] += dp

    main = lambda f: pl.BlockSpec((None, tr, c), lambda j, i: (f(j), i, 0))
    halo = lambda f: pl.BlockSpec((None, HALO, c), lambda j, i: (f(j), jnp.maximum(i * (tr // HALO) - 1, 0), 0))
    par = lambda f: pl.BlockSpec((None, 8, c), lambda j, i: (f(j), 0, 0))
    fa = lambda j: j % half
    fg = lambda j: j % half + half
    return pl.pallas_call(
        body, grid=(NDEV, t // tr),
        in_specs=[main(fa), halo(fa), main(fg), halo(fg), main(fa), par(fa), par(fg)],
        out_specs=[pl.BlockSpec((None, tr, c), lambda j, i: (j, i, 0)), pl.BlockSpec((None, 8, c), lambda j, i: (j, 0, 0))],
        out_shape=[jax.ShapeDtypeStruct((NDEV, t, c), BF16), jax.ShapeDtypeStruct((NDEV, 8, c), F32)],
        name="ffn_gate_bwd", compiler_params=_params(("parallel", "arbitrary")),
    )(h3, h3, h3, h3, dgated3, p3, p3)


def ffn_conv_t(dy3, p3, tr=256):
    _, t, c = dy3.shape
    nblk = t // tr

    def body(d_ref, dh_ref, p_ref, o_ref):
        last = pl.program_id(1) == nblk - 1
        cur = d_ref[...].astype(F32)
        ext = jnp.concatenate([cur, jnp.where(last, 0.0, dh_ref[...].astype(F32))], axis=0)
        n = tr + HALO
        s1 = pltpu.roll(ext, n - 1, 0)[:tr]
        s2 = pltpu.roll(ext, n - 2, 0)[:tr]
        p = p_ref[...]
        o_ref[...] = (p[2:3] * cur + p[1:2] * s1 + p[0:1] * s2).astype(o_ref.dtype)

    return pl.pallas_call(
        body, grid=(NDEV, nblk),
        in_specs=[pl.BlockSpec((None, tr, c), lambda j, i: (j, i, 0)),
                  pl.BlockSpec((None, HALO, c), lambda j, i: (j, jnp.minimum((i + 1) * (tr // HALO), t // HALO - 1), 0)),
                  pl.BlockSpec((None, 8, c), lambda j, i: (j, 0, 0))],
        out_specs=pl.BlockSpec((None, tr, c), lambda j, i: (j, i, 0)),
        out_shape=jax.ShapeDtypeStruct((NDEV, t, c), BF16), name="ffn_conv_t",
        compiler_params=_params(("parallel", "parallel")),
    )(dy3, dy3, p3)


def _att_consts(bq, bk):
    lane = lax.broadcasted_iota(jnp.int32, (1, LANES), 1)
    heads = (lane < HEAD_DIM, lane >= HEAD_DIM)
    rr = lax.broadcasted_iota(jnp.int32, (bq, bk), 0)
    cc = lax.broadcasted_iota(jnp.int32, (bq, bk), 1)
    kr = lax.broadcasted_iota(jnp.int32, (bk, bk), 0)
    kc = lax.broadcasted_iota(jnp.int32, (bk, bk), 1)
    return heads, rr, cc, kr, kc


def _split_dot(x, tri, parts):
    out = None
    for _ in range(parts):
        piece = x.astype(BF16)
        x = x - piece.astype(F32)
        term = jnp.dot(piece, tri, preferred_element_type=F32)
        out = term if out is None else out + term
    return out


def _att_logits(qh, k):
    z = lax.dot_general(qh, k, _MM_TB, preferred_element_type=F32)
    lsp = jnp.minimum(z, 0.0) - jnp.log(1.0 + jnp.exp(-jnp.abs(z)))
    return lsp, lsp - z


def _per_head(heads, a, b):
    return jnp.where(heads[0], a, b)


def sb_attn_fwd(qkv):
    t, d3 = qkv.shape
    d = d3 // 3
    npair = d // LANES
    bq, bk = min(ATT_BQ, t), min(ATT_BK, t)
    kpq = bq // bk

    def body(q_ref, k_ref, v_ref, o_ref, lt_ref, acc_ref):
        heads, rr, cc, kr, kc = _att_consts(bq, bk)
        suffix = (kr > kc).astype(BF16)

        def trip(qh, k0, valid, runs):
            k = k_ref[pl.ds(k0, bk), :]
            v = v_ref[pl.ds(k0, bk), :]
            new_runs = []
            for h in range(2):
                lsp, lraw = _att_logits(qh[h], k)
                lm = lraw if valid is None else jnp.where(valid, lraw, 0.0)
                w = jnp.exp(lsp + _split_dot(lm, suffix, 2) + runs[h])
                if valid is not None:
                    w = jnp.where(valid, w, 0.0)
                acc_ref[h] += jnp.dot(w.astype(BF16), v, preferred_element_type=F32)
                new_runs.append(runs[h] + jnp.sum(lm, axis=1, keepdims=True))
            return tuple(new_runs)

        def q_loop(qb, _):
            q0 = pl.multiple_of(qb * bq, bq)
            q = q_ref[pl.ds(q0, bq), :] * 0.125
            qh = [jnp.where(hm, q, 0.0).astype(BF16) for hm in heads]
            acc_ref[...] = jnp.zeros_like(acc_ref)
            runs = (jnp.zeros((bq, 1), F32),) * 2
            for dblk in reversed(range(kpq)):
                runs = trip(qh, pl.multiple_of(q0 + dblk * bk, bk), dblk * bk + cc < rr, runs)
            nleft = qb * kpq
            runs = lax.fori_loop(
                0, nleft, lambda i, r: trip(qh, pl.multiple_of((nleft - 1 - i) * bk, bk), None, r), runs)
            o_ref[pl.ds(q0, bq), :] = _per_head(heads, acc_ref[0], acc_ref[1])
            lt_ref[pl.ds(q0, bq), :] = _per_head(heads, runs[0], runs[1])
            return 0

        lax.fori_loop(0, t // bq, q_loop, 0)

    col = lambda off: pl.BlockSpec((t, LANES), lambda p: (0, p + off))
    return pl.pallas_call(
        body, grid=(npair,), in_specs=[col(0), col(npair), col(2 * npair)], out_specs=[col(0), col(0)],
        out_shape=[jax.ShapeDtypeStruct((t, d), F32)] * 2, scratch_shapes=[pltpu.VMEM((2, bq, LANES), F32)],
        name="sb_attn_fwd", compiler_params=_params(("parallel",)),
    )(qkv, qkv, qkv)


def sb_attn_bwd(qkv, ltot, do):
    t, d3 = qkv.shape
    d = d3 // 3
    npair = d // LANES
    bq, bk = min(ATT_BQ, t), min(ATT_BK, t)
    kpq = bq // bk

    def body(q_ref, k_ref, v_ref, lt_ref, do_ref, d_ref, dk_acc, dv_acc, dq_acc):
        heads, rr, cc, kr, kc = _att_consts(bq, bk)
        prefix_incl = (kr <= kc).astype(BF16)
        prefix_excl = (kr < kc).astype(BF16)
        dk_acc[...] = jnp.zeros_like(dk_acc)
        dv_acc[...] = jnp.zeros_like(dv_acc)

        def trip(qh, doh, lt, k0, valid, carry):
            lruns, gruns = carry
            k = k_ref[pl.ds(k0, bk), :]
            v = v_ref[pl.ds(k0, bk), :]
            new_lruns, new_gruns = [], []
            dk_blk = jnp.zeros((bk, LANES), F32)
            dv_blk = jnp.zeros((bk, LANES), F32)
            for h in range(2):
                lsp, lraw = _att_logits(qh[h], k)
                lm = lraw if valid is None else jnp.where(valid, lraw, 0.0)
                right = lt[h] - (lruns[h] + _split_dot(lm, prefix_incl, 2))
                w = jnp.exp(lsp + right)
                if valid is not None:
                    w = jnp.where(valid, w, 0.0)
                g = lax.dot_general(doh[h], v, _MM_TB, preferred_element_type=F32) * w
                left = gruns[h] + _split_dot(g, prefix_excl, 2)
                dz = g * jnp.exp(lraw) - jnp.exp(lsp) * left
                if valid is not None:
                    dz = jnp.where(valid, dz, 0.0)
                dz = dz.astype(BF16)
                kh = jnp.where(heads[h], k, 0.0).astype(BF16)
                dq_acc[h] += jnp.dot(dz, kh, preferred_element_type=F32)
                dk_blk = dk_blk + lax.dot_general(dz, qh[h], _MM_TA, preferred_element_type=F32)
                dv_blk = dv_blk + lax.dot_general(w.astype(BF16), doh[h], _MM_TA, preferred_element_type=F32)
                new_lruns.append(lruns[h] + jnp.sum(lm, axis=1, keepdims=True))
                new_gruns.append(gruns[h] + jnp.sum(g, axis=1, keepdims=True))
            dk_acc[pl.ds(k0, bk), :] += dk_blk
            dv_acc[pl.ds(k0, bk), :] += dv_blk
            return tuple(new_lruns), tuple(new_gruns)

        def q_loop(qb, _):
            q0 = pl.multiple_of(qb * bq, bq)
            q = q_ref[pl.ds(q0, bq), :] * 0.125
            dout = do_ref[pl.ds(q0, bq), :]
            lt2 = lt_ref[pl.ds(q0, bq), :]
            qh = [jnp.where(hm, q, 0.0).astype(BF16) for hm in heads]
            doh = [jnp.where(hm, dout, 0.0).astype(BF16) for hm in heads]
            lt = [jnp.max(jnp.where(hm, lt2, -jnp.inf), axis=1, keepdims=True) for hm in heads]
            dq_acc[...] = jnp.zeros_like(dq_acc)
            col = (jnp.zeros((bq, 1), F32),) * 2
            carry = lax.fori_loop(
                0, qb * kpq, lambda kb, c: trip(qh, doh, lt, pl.multiple_of(kb * bk, bk), None, c), (col, col))
            for dblk in range(kpq):
                carry = trip(qh, doh, lt, pl.multiple_of(q0 + dblk * bk, bk), dblk * bk + cc < rr, carry)
            d_ref[0, pl.ds(q0, bq), :] = ((dq_acc[0] + dq_acc[1]) * 0.125).astype(d_ref.dtype)
            return 0

        lax.fori_loop(0, t // bq, q_loop, 0)
        d_ref[1] = dk_acc[...].astype(d_ref.dtype)
        d_ref[2] = dv_acc[...].astype(d_ref.dtype)

    col = lambda off: pl.BlockSpec((t, LANES), lambda p: (0, p + off))
    return pl.pallas_call(
        body, grid=(npair,), in_specs=[col(0), col(npair), col(2 * npair), col(0), col(0)],
        out_specs=pl.BlockSpec((3, t, LANES), lambda p: (0, 0, p)),
        out_shape=jax.ShapeDtypeStruct((3, t, d), BF16),
        scratch_shapes=[pltpu.VMEM((t, LANES), F32), pltpu.VMEM((t, LANES), F32), pltpu.VMEM((2, bq, LANES), F32)],
        name="sb_attn_bwd", compiler_params=_params(("parallel",)),
    )(qkv, qkv, qkv, ltot, do)


def _sgu_parts(hin, g, ws_ref, bf_ref):
    width = hin.shape[1] // 2
    h = jax.nn.gelu(hin)
    u, v = h[:, :width], h[:, width:]
    r = lax.rsqrt(jnp.mean(v * v, axis=-1, keepdims=True) + EPS)
    vn = v * r * g
    rr = lax.broadcasted_iota(jnp.int32, (CHUNK, CHUNK), 0)
    cc = lax.broadcasted_iota(jnp.int32, (CHUNK, CHUNK), 1)
    causal = cc <= rr
    wcs = [jnp.where(causal, ws_ref[gi], 0.0).astype(BF16) for gi in range(SG_GROUPS)]
    sv = jnp.concatenate(
        [jnp.dot(wcs[gi], vn[:, gi * CHUNK:(gi + 1) * CHUNK].astype(BF16), preferred_element_type=F32) + bf_ref[gi]
         for gi in range(SG_GROUPS)], axis=1)
    return u, v, r, vn, wcs, sv, causal


def sgu_fwd(hin, g, ws, bfull):
    t, w2 = hin.shape
    width = w2 // 2

    def body(h_ref, g_ref, ws_ref, bf_ref, o_ref):
        u, _, _, _, _, sv, _ = _sgu_parts(h_ref[...].astype(F32), g_ref[...], ws_ref, bf_ref)
        o_ref[...] = (u * sv).astype(o_ref.dtype)

    full = lambda a: pl.BlockSpec(a.shape, lambda i, nd=a.ndim: (0,) * nd)
    return pl.pallas_call(
        body, grid=(t // CHUNK,), in_specs=[pl.BlockSpec((CHUNK, w2), lambda i: (i, 0)), full(g), full(ws), full(bfull)],
        out_specs=pl.BlockSpec((CHUNK, width), lambda i: (i, 0)), out_shape=jax.ShapeDtypeStruct((t, width), BF16),
        name="sgu_fwd", compiler_params=_params(("parallel",)),
    )(hin, g, ws, bfull)


def sgu_bwd(hin, dp, g, ws, bfull):
    t, w2 = hin.shape
    width = w2 // 2

    def body(h_ref, dp_ref, g_ref, ws_ref, bf_ref, dh_ref, dws_ref, dbf_ref, dg_ref):
        i = pl.program_id(0)
        hin_v = h_ref[...].astype(F32)
        gv = g_ref[...]
        u, v, r, vn, wcs, sv, causal = _sgu_parts(hin_v, gv, ws_ref, bf_ref)
        dpv = dp_ref[...].astype(F32)
        du = dpv * sv
        dsv = dpv * u
        dvn_parts, dws_parts, dbf_parts = [], [], []
        for gi in range(SG_GROUPS):
            dsv_g = dsv[:, gi * CHUNK:(gi + 1) * CHUNK]
            dsv_b = dsv_g.astype(BF16)
            dvn_parts.append(lax.dot_general(wcs[gi], dsv_b, _MM_TA, preferred_element_type=F32))
            vn_b = vn[:, gi * CHUNK:(gi + 1) * CHUNK].astype(BF16)
            dws_parts.append(jnp.where(causal, lax.dot_general(dsv_b, vn_b, _MM_TB, preferred_element_type=F32), 0.0))
            dbf_parts.append(jnp.broadcast_to(jnp.sum(dsv_g, axis=1, keepdims=True), (CHUNK, CHUNK)))
        dvn = jnp.concatenate(dvn_parts, axis=1)
        dgain = jnp.sum(dvn * v * r, axis=0, keepdims=True)
        gvv = dvn * gv
        dv = r * gvv - v * (r * r * r) * jnp.mean(v * gvv, axis=-1, keepdims=True)
        _, vjp = jax.vjp(jax.nn.gelu, hin_v)
        dh_ref[...] = vjp(jnp.concatenate([du, dv], axis=1))[0].astype(dh_ref.dtype)

        @pl.when(i == 0)
        def _():
            for gi in range(SG_GROUPS):
                dws_ref[gi] = dws_parts[gi]
                dbf_ref[gi] = dbf_parts[gi]
            dg_ref[...] = dgain

        @pl.when(i > 0)
        def _():
            for gi in range(SG_GROUPS):
                dws_ref[gi] += dws_parts[gi]
                dbf_ref[gi] += dbf_parts[gi]
            dg_ref[...] += dgain

    full = lambda a: pl.BlockSpec(a.shape, lambda i, nd=a.ndim: (0,) * nd)
    sq = (SG_GROUPS, CHUNK, CHUNK)
    return pl.pallas_call(
        body, grid=(t // CHUNK,),
        in_specs=[pl.BlockSpec((CHUNK, w2), lambda i: (i, 0)), pl.BlockSpec((CHUNK, width), lambda i: (i, 0)),
                  full(g), full(ws), full(bfull)],
        out_specs=[pl.BlockSpec((CHUNK, w2), lambda i: (i, 0)), pl.BlockSpec(sq, lambda i: (0, 0, 0)),
                   pl.BlockSpec(sq, lambda i: (0, 0, 0)), pl.BlockSpec((1, width), lambda i: (0, 0))],
        out_shape=[jax.ShapeDtypeStruct((t, w2), BF16), jax.ShapeDtypeStruct(sq, F32), jax.ShapeDtypeStruct(sq, F32),
                   jax.ShapeDtypeStruct((1, width), F32)],
        name="sgu_bwd", compiler_params=_params(("arbitrary",)),
    )(hin, dp, g, ws, bfull)


def _disc1(lam_re, lam_im, log_dt):
    lr = jnp.minimum(lam_re, -1e-4)
    li = lam_im
    dt = jnp.exp(log_dt)
    mag = jnp.exp(dt * lr)
    ar = mag * jnp.cos(dt * li)
    ai = mag * jnp.sin(dt * li)
    den = lr * lr + li * li
    return ar, ai, ((ar - 1.0) * lr + ai * li) / den, (ai * lr - (ar - 1.0) * li) / den


def _disc2(cre, cim, b_re, b_im):
    return cre * b_re - cim * b_im, cre * b_im + cim * b_re


def _single(fn, ins, out_shapes, name):
    n = len(ins)

    def body(*refs):
        vals = fn(*[r[...] for r in refs[:n]])
        for ref, val in zip(refs[n:], vals):
            ref[...] = val

    return pl.pallas_call(body, out_shape=[jax.ShapeDtypeStruct(s, F32) for s in out_shapes], name=name)(*ins)


def s5_scan_fwd(bu2, a2):
    _, t, n = bu2.shape
    cb = SCAN_COLS

    def body(bu_ref, a_ref, x_ref):
        ar, ai = a_ref[0:1, :], a_ref[1:2, :]

        def step(b, carry):
            xr, xi = carry
            t0 = pl.multiple_of(b * 8, 8)
            br = bu_ref[0, pl.ds(t0, 8), :]
            bi = bu_ref[1, pl.ds(t0, 8), :]
            rows_r, rows_i = [], []
            for r in range(8):
                xr, xi = ar * xr - ai * xi + br[r:r + 1], ar * xi + ai * xr + bi[r:r + 1]
                rows_r.append(xr)
                rows_i.append(xi)
            x_ref[0, pl.ds(t0, 8), :] = jnp.concatenate(rows_r, axis=0)
            x_ref[1, pl.ds(t0, 8), :] = jnp.concatenate(rows_i, axis=0)
            return xr, xi

        zero = jnp.zeros((1, cb), F32)
        lax.fori_loop(0, t // 8, step, (zero, zero))

    return pl.pallas_call(
        body, grid=(n // cb,),
        in_specs=[pl.BlockSpec((2, t, cb), lambda j: (0, 0, j)), pl.BlockSpec((2, cb), lambda j: (0, j))],
        out_specs=pl.BlockSpec((2, t, cb), lambda j: (0, 0, j)), out_shape=jax.ShapeDtypeStruct((2, t, n), F32),
        name="s5_scan_fwd", compiler_params=_params(("parallel",)),
    )(bu2, a2)


def s5_scan_bwd(dx2, x2, a2):
    _, t, n = dx2.shape
    cb = SCAN_COLS
    nb = t // 8

    def body(dx_ref, x_ref, a_ref, g_ref, da_ref):
        ar, ai = a_ref[0:1, :], a_ref[1:2, :]
        row = lax.broadcasted_iota(jnp.int32, (8, cb), 0)

        def step(s, carry):
            gr, gi, dar, dai = carry
            b = nb - 1 - s
            t0 = pl.multiple_of(b * 8, 8)
            dr = dx_ref[0, pl.ds(t0, 8), :]
            di = dx_ref[1, pl.ds(t0, 8), :]
            rows_r, rows_i = [None] * 8, [None] * 8
            for r in range(7, -1, -1):
                gr, gi = dr[r:r + 1] + ar * gr + ai * gi, di[r:r + 1] - ai * gr + ar * gi
                rows_r[r] = gr
                rows_i[r] = gi
            g8r = jnp.concatenate(rows_r, axis=0)
            g8i = jnp.concatenate(rows_i, axis=0)
            g_ref[0, pl.ds(t0, 8), :] = g8r
            g_ref[1, pl.ds(t0, 8), :] = g8i
            tp = pl.multiple_of(jnp.maximum(t0 - 8, 0), 8)
            keep = b > 0
            prev_r = jnp.where(keep, x_ref[0, pl.ds(tp, 8), :], 0.0)[7:8]
            prev_i = jnp.where(keep, x_ref[1, pl.ds(tp, 8), :], 0.0)[7:8]
            xpr = jnp.where(row == 0, prev_r, pltpu.roll(x_ref[0, pl.ds(t0, 8), :], 1, 0))
            xpi = jnp.where(row == 0, prev_i, pltpu.roll(x_ref[1, pl.ds(t0, 8), :], 1, 0))
            return gr, gi, dar + g8r * xpr + g8i * xpi, dai + g8i * xpr - g8r * xpi

        z1, z8 = jnp.zeros((1, cb), F32), jnp.zeros((8, cb), F32)
        _, _, dar, dai = lax.fori_loop(0, nb, step, (z1, z1, z8, z8))
        da_ref[0:1, :] = jnp.sum(dar, axis=0, keepdims=True)
        da_ref[1:2, :] = jnp.sum(dai, axis=0, keepdims=True)

    blk = pl.BlockSpec((2, t, cb), lambda j: (0, 0, j))
    vec = pl.BlockSpec((2, cb), lambda j: (0, j))
    return pl.pallas_call(
        body, grid=(n // cb,), in_specs=[blk, blk, vec], out_specs=[blk, vec],
        out_shape=[jax.ShapeDtypeStruct((2, t, n), F32), jax.ShapeDtypeStruct((2, n), F32)],
        name="s5_scan_bwd", compiler_params=_params(("parallel",)),
    )(dx2, x2, a2)


_SP_U = SSM_PACK * SSM_GROUP
_SP_X = SSM_PACK * SSM_STATE
_NKB = SSM_GROUPS // SSM_PACK


def mm_s5(kind, a, b, m, name, res=None, tm=512):
    kw = dict(passes=3, name=name)
    xblk = lambda row, sel, col: ((None, tm, _SP_X), lambda *g: (sel(*g), row(*g), col(*g)))
    if kind == "bu":
        o_blk, o_map = xblk(lambda g, i, k: i, lambda g, i, k: g // _NKB, lambda g, i, k: g % _NKB)
        return _mm(a, b, grid=(2 * _NKB, m // tm, 1), a_blk=(tm, _SP_U), a_map=lambda g, i, k: (i, g % _NKB),
                   b_blk=(None, None, _SP_U, _SP_X), b_map=lambda g, i, k: (g // _NKB, g % _NKB, 0, 0),
                   o_blk=o_blk, o_map=o_map, out_shape=(2, m, _NKB * _SP_X), out_dtype=F32, **kw)
    if kind == "yc":
        a_blk, a_map = xblk(lambda j, i, k: i, lambda j, i, k: k, lambda j, i, k: j)
        return _mm(a, b, grid=(_NKB, m // tm, 2), a_blk=a_blk, a_map=a_map,
                   b_blk=(None, None, _SP_X, _SP_U), b_map=lambda j, i, k: (k, j, 0, 0),
                   o_blk=(tm, _SP_U), o_map=lambda j, i, k: (i, j), out_shape=(m, _NKB * _SP_U), out_dtype=F32,
                   acc_2d=(tm, _SP_U), **kw)
    if kind == "dx":
        o_blk, o_map = xblk(lambda g, i, k: i, lambda g, i, k: g // _NKB, lambda g, i, k: g % _NKB)
        return _mm(a, b, grid=(2 * _NKB, m // tm, 1), a_blk=(tm, _SP_U), a_map=lambda g, i, k: (i, g % _NKB),
                   b_blk=(None, None, _SP_X, _SP_U), b_map=lambda g, i, k: (g // _NKB, g % _NKB, 0, 0),
                   o_blk=o_blk, o_map=o_map, out_shape=(2, m, _NKB * _SP_X), out_dtype=F32, dims=_MM_TB, **kw)
    if kind == "dcd":
        a_blk, a_map = xblk(lambda g, _, k: k, lambda g, _, k: g // _NKB, lambda g, _, k: g % _NKB)
        return _mm(a, b, grid=(2 * _NKB, 1, m // tm), a_blk=a_blk, a_map=a_map,
                   b_blk=(tm, _SP_U), b_map=lambda g, _, k: (k, g % _NKB),
                   o_blk=(None, None, _SP_X, _SP_U), o_map=lambda g, _, k: (g // _NKB, g % _NKB, 0, 0),
                   out_shape=(2, _NKB, _SP_X, _SP_U), out_dtype=F32, dims=_MM_TA, acc_2d=(_SP_X, _SP_U), **kw)
    if kind == "du":
        a_blk, a_map = xblk(lambda j, i, k: i, lambda j, i, k: k, lambda j, i, k: j)
        return _mm(a, b, grid=(_NKB, m // tm, 2), a_blk=a_blk, a_map=a_map,
                   b_blk=(None, None, _SP_U, _SP_X), b_map=lambda j, i, k: (k, j, 0, 0),
                   o_blk=(tm, _SP_U), o_map=lambda j, i, k: (i, j), out_shape=(m, _NKB * _SP_U), out_dtype=F32,
                   dims=_MM_TB, acc_2d=(tm, _SP_U), res=res, res_blk=(tm, _SP_U), res_map=lambda j, i, k: (i, j), **kw)
    assert kind == "dbd"
    b_blk, b_map = xblk(lambda g, _, k: k, lambda g, _, k: g // _NKB, lambda g, _, k: g % _NKB)
    return _mm(a, b, grid=(2 * _NKB, 1, m // tm), a_blk=(tm, _SP_U), a_map=lambda g, _, k: (k, g % _NKB),
               b_blk=b_blk, b_map=b_map,
               o_blk=(None, None, _SP_U, _SP_X), o_map=lambda g, _, k: (g // _NKB, g % _NKB, 0, 0),
               out_shape=(2, _NKB, _SP_U, _SP_X), out_dtype=F32, dims=_MM_TA, acc_2d=(_SP_U, _SP_X), **kw)


def _block_diag(w):
    g, a, b = w.shape
    eye = jnp.eye(SSM_PACK, dtype=w.dtype)
    wp = w.reshape(g // SSM_PACK, SSM_PACK, a, b)
    return jnp.einsum("kgab,gh->kgahb", wp, eye).reshape(g // SSM_PACK, SSM_PACK * a, SSM_PACK * b)


def _block_diag_t(d, a, b):
    k = d.shape[0]
    eye = jnp.eye(SSM_PACK, dtype=d.dtype)
    dp = d.reshape(k, SSM_PACK, a, SSM_PACK, b)
    return jnp.einsum("kgahb,gh->kgab", dp, eye).reshape(k * SSM_PACK, a, b)


def _coords():
    return lax.axis_index("x"), lax.axis_index("y"), lax.axis_index("c")


def all_gather(tensors, name):
    n = len(tensors)
    any_spec = pl.BlockSpec(memory_space=pl.ANY)

    def body(*refs):
        ins, outs = refs[:n], refs[n:2 * n]
        send, recv, local = refs[2 * n:]
        x, y, c = _coords()
        me, sibling = (x, y, c), (x, y, 1 - c)
        chips = [(1 - x, y), (x, 1 - y), (1 - x, 1 - y)]

        def slot(p):
            return 4 * p[0] + 2 * p[1] + p[2]

        def copy(t, k, block, to, src=None):
            dst = outs[t].at[slot(block)]
            return pltpu.make_async_remote_copy(
                src_ref=dst if src is None else src, dst_ref=dst, send_sem=send.at[7 * t + k],
                recv_sem=recv.at[7 * t + k], device_id=to, device_id_type=pl.DeviceIdType.MESH)

        own, sent = [], []
        for t in range(n):
            mine = pltpu.make_async_copy(ins[t], outs[t].at[slot(me)], local.at[t])
            mine.start()
            own.append(mine)
            first = [copy(t, 0, me, sibling, src=ins[t])]
            first += [copy(t, 1 + j, me, (*chip, c), src=ins[t]) for j, chip in enumerate(chips)]
            for cp in first:
                cp.start()
            sent += first
        for t in range(n):
            for j, chip in enumerate(chips):
                copy(t, 1 + j, (*chip, c), me).wait_recv()
                passed = copy(t, 4 + j, (*chip, c), sibling)
                passed.start()
                sent.append(passed)
        for t in range(n):
            copy(t, 0, sibling, me).wait_recv()
            for j, chip in enumerate(chips):
                copy(t, 4 + j, (*chip, 1 - c), me).wait_recv()
        for cp in sent:
            cp.wait_send()
        for cp in own:
            cp.wait()

    return pl.pallas_call(
        body, in_specs=[any_spec] * n, out_specs=[any_spec] * n,
        out_shape=[jax.ShapeDtypeStruct((NDEV,) + a.shape, a.dtype) for a in tensors],
        scratch_shapes=[pltpu.SemaphoreType.DMA((7 * n,)), pltpu.SemaphoreType.DMA((7 * n,)),
                        pltpu.SemaphoreType.DMA((n,))],
        name=name,
    )(*tensors)


_HBM_SPEC = pl.BlockSpec(memory_space=pltpu.HBM)
_SEM_SPEC = pl.BlockSpec(memory_space=pltpu.SEMAPHORE)
_NPEER = NDEV - 1


def _split_copy_params():
    return pltpu.CompilerParams(has_side_effects=pltpu.SideEffectType.DATAFLOW_SIDE_EFFECTING)


def _me_and_peers():
    x, y, c = _coords()
    peers = []
    for rel in range(1, NDEV):
        p = (1 - x if rel & 4 else x, 1 - y if rel & 2 else y, 1 - c if rel & 1 else c)
        peers.append((p, 4 * p[0] + 2 * p[1] + p[2]))
    return 4 * x + 2 * y + c, peers


def _hbm(a):
    return pltpu.with_memory_space_constraint(a, pltpu.HBM)


def gather_start(bufs, name):
    n = len(bufs)

    def body(*refs):
        ins, outs = refs[:n], refs[n:]
        me, peers = _me_and_peers()
        for t in range(n):
            for k, (dev, _) in enumerate(peers):
                pltpu.make_async_remote_copy(
                    src_ref=ins[t].at[me], dst_ref=ins[t].at[me], send_sem=outs[3 * t].at[k],
                    recv_sem=outs[3 * t + 1].at[k], device_id=dev, device_id_type=pl.DeviceIdType.MESH).start()

    out_shape, out_specs = [], []
    for b in bufs:
        out_shape += [pltpu.SemaphoreType.DMA((_NPEER,)), pltpu.SemaphoreType.DMA((_NPEER,)), pltpu.HBM(b.shape, b.dtype)]
        out_specs += [_SEM_SPEC, _SEM_SPEC, _HBM_SPEC]
    res = pl.pallas_call(
        body, name=name, out_shape=tuple(out_shape), in_specs=[_HBM_SPEC] * n, out_specs=tuple(out_specs),
        input_output_aliases={t: 3 * t + 2 for t in range(n)}, compiler_params=_split_copy_params(),
    )(*[_hbm(b) for b in bufs])
    return [tuple(res[3 * t:3 * t + 3]) for t in range(n)]


def gather_wait(started, after, name):
    n = len(started)

    def body(*refs):
        bufs, sems = refs[:n], refs[n:3 * n]
        me, peers = _me_and_peers()
        for t in range(n):
            for k, (dev, slot) in enumerate(peers):
                cp = pltpu.make_async_remote_copy(
                    src_ref=bufs[t].at[me], dst_ref=bufs[t].at[slot], send_sem=sems[2 * t].at[k],
                    recv_sem=sems[2 * t + 1].at[k], device_id=dev, device_id_type=pl.DeviceIdType.MESH)
                cp.wait_recv()
                cp.wait_send()

    args = [s[2] for s in started] + [sem for s in started for sem in s[:2]] + [after]
    res = pl.pallas_call(
        body, name=name, out_shape=tuple(pltpu.HBM(s[2].shape, s[2].dtype) for s in started),
        in_specs=[_HBM_SPEC] * n + [_SEM_SPEC] * (2 * n) + [pl.BlockSpec(memory_space=pl.ANY)],
        out_specs=tuple([_HBM_SPEC] * n), input_output_aliases={t: t for t in range(n)},
        compiler_params=_split_copy_params(),
    )(*args)
    return list(res)


def scatter_start(srcs, name):
    n = len(srcs)
    lands = [lax.empty((_NPEER,) + s.shape[1:], s.dtype) for s in srcs]

    def body(*refs):
        ins, land_refs, outs = refs[:n], refs[n:2 * n], refs[2 * n:]
        _, peers = _me_and_peers()
        for t in range(n):
            for k, (dev, slot) in enumerate(peers):
                pltpu.make_async_remote_copy(
                    src_ref=ins[t].at[slot], dst_ref=land_refs[t].at[k], send_sem=outs[4 * t].at[k],
                    recv_sem=outs[4 * t + 1].at[k], device_id=dev, device_id_type=pl.DeviceIdType.MESH).start()
        outs[4 * n][...] = jnp.zeros_like(outs[4 * n])

    out_shape, out_specs = [], []
    for s, land in zip(srcs, lands):
        out_shape += [pltpu.SemaphoreType.DMA((_NPEER,)), pltpu.SemaphoreType.DMA((_NPEER,)),
                      pltpu.HBM(s.shape, s.dtype), pltpu.HBM(land.shape, land.dtype)]
        out_specs += [_SEM_SPEC, _SEM_SPEC, _HBM_SPEC, _HBM_SPEC]
    out_shape.append(jax.ShapeDtypeStruct((8, LANES), F32))
    out_specs.append(pl.BlockSpec(memory_space=pltpu.VMEM))
    aliases = {t: 4 * t + 2 for t in range(n)}
    aliases.update({n + t: 4 * t + 3 for t in range(n)})
    res = pl.pallas_call(
        body, name=name, out_shape=tuple(out_shape), in_specs=[_HBM_SPEC] * (2 * n), out_specs=tuple(out_specs),
        input_output_aliases=aliases, compiler_params=_split_copy_params(),
    )(*[_hbm(s) for s in srcs], *[_hbm(land) for land in lands])
    return [tuple(res[4 * t:4 * t + 4]) for t in range(n)], res[4 * n]


def scatter_wait(started, after, name):
    n = len(started)

    def body(*refs):
        srcs, land_refs, sems = refs[:n], refs[n:2 * n], refs[2 * n:4 * n]
        _, peers = _me_and_peers()
        for t in range(n):
            for k, (dev, slot) in enumerate(peers):
                cp = pltpu.make_async_remote_copy(
                    src_ref=srcs[t].at[slot], dst_ref=land_refs[t].at[k], send_sem=sems[2 * t].at[k],
                    recv_sem=sems[2 * t + 1].at[k], device_id=dev, device_id_type=pl.DeviceIdType.MESH)
                cp.wait_recv()
                cp.wait_send()

    args = [s[2] for s in started] + [s[3] for s in started] + [sem for s in started for sem in s[:2]] + [after]
    res = pl.pallas_call(
        body, name=name,
        out_shape=tuple([pltpu.HBM(s[2].shape, s[2].dtype) for s in started]
                        + [pltpu.HBM(s[3].shape, s[3].dtype) for s in started]),
        in_specs=[_HBM_SPEC] * (2 * n) + [_SEM_SPEC] * (2 * n) + [pl.BlockSpec(memory_space=pl.ANY)],
        out_specs=tuple([_HBM_SPEC] * (2 * n)), input_output_aliases={t: t for t in range(2 * n)},
        compiler_params=_split_copy_params(),
    )(*args)
    return [(res[t], res[n + t]) for t in range(n)]


def all_to_all(groups, name):
    flat = [a for grp in groups for a in grp]
    n = len(flat)
    where = [(gi, li) for gi, grp in enumerate(groups) for li in range(len(grp))]
    any_spec = pl.BlockSpec(memory_space=pl.ANY)

    def body(*refs):
        ins, outs = refs[:n], refs[n:n + len(groups)]
        send, recv, local = refs[n + len(groups):]
        x, y, c = _coords()
        me = 4 * x + 2 * y + c
        waits = []
        for e in range(n):
            gi, li = where[e]
            mine = pltpu.make_async_copy(ins[e].at[me], outs[gi].at[me, li], local.at[e])
            mine.start()
            waits.append(mine)
        sent, landing = [], []
        for rel in range(1, NDEV):
            px = 1 - x if rel & 4 else x
            py = 1 - y if rel & 2 else y
            pc = 1 - c if rel & 1 else c
            peer = 4 * px + 2 * py + pc
            for e in range(n):
                gi, li = where[e]

                def copy(dst_slot, e=e, gi=gi, li=li, rel=rel, peer=peer, to=(px, py, pc)):
                    return pltpu.make_async_remote_copy(
                        src_ref=ins[e].at[peer], dst_ref=outs[gi].at[dst_slot, li], send_sem=send.at[7 * e + rel - 1],
                        recv_sem=recv.at[7 * e + rel - 1], device_id=to, device_id_type=pl.DeviceIdType.MESH)

                cp = copy(me)
                cp.start()
                sent.append(cp)
                landing.append(copy(peer))
        for cp in landing:
            cp.wait_recv()
        for cp in sent:
            cp.wait_send()
        for cp in waits:
            cp.wait()

    return pl.pallas_call(
        body, in_specs=[any_spec] * n, out_specs=[any_spec] * len(groups),
        out_shape=[jax.ShapeDtypeStruct((NDEV, len(grp)) + grp[0].shape[1:], grp[0].dtype) for grp in groups],
        scratch_shapes=[pltpu.SemaphoreType.DMA((7 * n,)), pltpu.SemaphoreType.DMA((7 * n,)),
                        pltpu.SemaphoreType.DMA((n,))],
        name=name,
    )(*flat)


_PACK_QUANTUM = 8 * LANES


def _pack(parts, lead=0):
    out = []
    for p in parts:
        head = p.shape[:lead]
        f = p.astype(F32).reshape(head + (-1,))
        pad = (-f.shape[-1]) % _PACK_QUANTUM
        if pad:
            f = jnp.concatenate([f, jnp.zeros(head + (pad,), F32)], axis=-1)
        out.append(f.reshape(head + (-1, LANES)))
    return jnp.concatenate(out, axis=lead)


def _unpack(buf, shapes):
    head = buf.shape[:-2]
    out, r = [], 0
    for s in shapes:
        n = 1
        for v in s:
            n *= v
        nr = -(-n // _PACK_QUANTUM) * 8
        flat = buf[..., r:r + nr, :].reshape(head + (nr * LANES,))[..., :n]
        out.append(flat.reshape(head + tuple(s)))
        r += nr
    return out


BIG = ("sb_w_qkv", "sb_w_o", "sg_w_in", "sg_w_o", "ssm_w_in", "ssm_w_glu", "ffn_w_up", "ffn_w_down")
SMALL_SHARDED = ("norm_g", "ssm_d", "ffn_conv_w")
REPLICATED = ("final_norm_g", "sg_norm_g", "sg_w_s", "sg_b", "ssm_lam_re", "ssm_lam_im", "ssm_log_dt",
              "ssm_b_re", "ssm_b_im", "ssm_c_re", "ssm_c_im", "ffn_conv_b")
WEIGHTS = ("norm_g", "final_norm_g", "sb_w_qkv", "sb_w_o", "sg_w_in", "sg_norm_g", "sg_w_s", "sg_b", "sg_w_o",
           "ssm_w_in", "ssm_lam_re", "ssm_lam_im", "ssm_log_dt", "ssm_b_re", "ssm_b_im", "ssm_c_re", "ssm_c_im",
           "ssm_d", "ssm_w_glu", "ffn_w_up", "ffn_conv_w", "ffn_conv_b", "ffn_w_down")


def _step(x, loss_target, w, m, v):
    t, d = x.shape[1], x.shape[2]
    depth = w["norm_g"].shape[0]
    x0 = x.reshape(t, d)
    tgt = loss_target.reshape(t, d)

    mx, my, mc = _coords()
    me = (4 * mx + 2 * my + mc).astype(jnp.int32).reshape(1)
    shard_pack = _pack([w[k] for k in SMALL_SHARDED])
    gathered_small, = all_gather([shard_pack], name="gather_small_weights")
    mixer_weights = (("sb_w_qkv", "sb_w_o"), ("sg_w_in", "sg_w_o"), ("ssm_w_in", "ssm_w_glu"))
    order = []
    for i in range(depth):
        order += [(k, i // 3) for k in mixer_weights[i % 3]] + [("ffn_w_up", i), ("ffn_w_down", i)]
    pending = dict(zip(order, gather_start([cast_into_slot(w[k], l, me) for k, l in order], "gather_weights_start")))
    wg = {}

    def weights(keys, after):
        missing = [key for key in keys if key not in wg]
        if missing:
            for key, buf in zip(missing, gather_wait([pending[key] for key in missing], after, "gather_weights_wait")):
                wg[key] = buf[:, None]
        return [wg[key] for key in keys]

    ng, sd, cw = _unpack(gathered_small, [w[k].shape for k in SMALL_SHARDED])
    norm_full = jnp.transpose(ng, (1, 2, 0, 3)).reshape(depth, 2, d)
    ssm_d_full = jnp.transpose(sd, (1, 0, 2)).reshape(1, d)
    nc = cw.shape[-1]
    conv_b3 = w["ffn_conv_b"].reshape(depth, NDEV, nc)
    p3 = [jnp.concatenate([cw[:, l], conv_b3[l][:, None, :], jnp.zeros((NDEV, 8 - CONV_K - 1, nc), F32)], axis=1)
          for l in range(depth)]

    g_, p_, h_ = SSM_GROUPS, SSM_STATE, SSM_GROUP
    lam_re, lam_im = w["ssm_lam_re"][0], w["ssm_lam_im"][0]
    log_dt = w["ssm_log_dt"][0].reshape(g_, 1)
    b_re, b_im = w["ssm_b_re"][0].reshape(g_ * p_, h_), w["ssm_b_im"][0].reshape(g_ * p_, h_)
    ar, ai, cre, cim = _single(_disc1, [lam_re, lam_im, log_dt], [(g_, p_)] * 4, "s5_disc1")
    cre_c, cim_c = cre.reshape(g_ * p_, 1), cim.reshape(g_ * p_, 1)
    bbr, bbi = _single(_disc2, [cre_c, cim_c, b_re, b_im], [(g_ * p_, h_)] * 2, "s5_disc2")
    per_group_t = lambda a, r, c: jnp.swapaxes(a.reshape(g_, r, c), 1, 2)
    bd = jnp.stack([_block_diag(per_group_t(bbr, p_, h_)), _block_diag(per_group_t(bbi, p_, h_))])
    cd = jnp.stack([_block_diag(per_group_t(w["ssm_c_re"][0], h_, p_)),
                    -_block_diag(per_group_t(w["ssm_c_im"][0], h_, p_))])
    a2 = jnp.stack([ar.reshape(g_ * p_), ai.reshape(g_ * p_)])

    sg_gain = w["sg_norm_g"]
    sg_ws = w["sg_w_s"][0]
    sg_bfull = jnp.broadcast_to(w["sg_b"][0][:, :, None], sg_ws.shape)

    acts = []
    xc = x0
    for i in range(depth):
        mixer, j = i % 3, i // 3
        st = {"x": xc}
        g0 = norm_full[i, 0][None]
        xn = rms_fwd(xc, g0, "rms_fwd")
        st["xn"] = xn
        w_in, w_out = weights([(k, j) for k in mixer_weights[mixer]], xn)
        if mixer == 0:
            qkv = mm_cs_fwd(xn, w_in, 0, BF16, "qkv_fwd")
            o, ltot = sb_attn_fwd(qkv)
            x1 = mm_rs_fwd(o, w_out, 0, xc, F32, "attn_out_fwd")
            st.update(qkv=qkv, o=o, ltot=ltot)
        elif mixer == 1:
            hin = mm_cs_fwd(xn, w_in, 0, BF16, "sg_in_fwd")
            p = sgu_fwd(hin, sg_gain, sg_ws, sg_bfull)
            x1 = mm_rs_fwd(p, w_out, 0, xc, F32, "sg_out_fwd")
            st.update(hin=hin, p=p)
        else:
            u = mm_rs_fwd(xn, w_in, 0, None, F32, "ssm_in_fwd")
            bu2 = mm_s5("bu", u, bd, t, "s5_bu")
            x2 = s5_scan_fwd(bu2, a2)
            yc = mm_s5("yc", x2, cd, t, "s5_yc")
            yg = s5_post_fwd(yc, u, ssm_d_full)
            hg = mm_cs_fwd(yg, w_out, 0, BF16, "ssm_glu_fwd")
            x1 = glu_fwd(hg, xc)
            st.update(u=u, x2=x2, yc=yc, yg=yg, hg=hg)
        g1 = norm_full[i, 1][None]
        xn2 = rms_fwd(x1, g1, "rms_fwd")
        w_up, w_down = weights([("ffn_w_up", i), ("ffn_w_down", i)], xn2)
        h3 = mm_cs_fwd(xn2, w_up, 0, BF16, "ffn_up_fwd")
        gated = ffn_gate_fwd(h3, p3[i])
        xc = mm_down_fwd(gated, w_down, 0, x1, "ffn_down_fwd")
        st.update(x1=x1, xn2=xn2, h3=h3, gated=gated, g0=g0, g1=g1)
        acts.append(st)

    dx, loss_lanes, d_final_g = loss_head(xc, w["final_norm_g"][None], tgt)
    loss = lax.psum(loss_lanes[0, 0], MESH_AXES)

    scattering = {}
    d_norm = [[None, None] for _ in range(depth)]
    d_p3 = [None] * depth
    rep = {}
    d_ssm_d = None
    token = None

    def scatter(grads_by_key):
        keys = list(grads_by_key)
        started, tok = scatter_start([grads_by_key[key] for key in keys], "scatter_grads_start")
        scattering[tuple(keys)] = started
        return tok

    for i in reversed(range(depth)):
        mixer, j = i % 3, i // 3
        st = acts[i]
        k_in, k_out = [(k, j) for k in mixer_weights[mixer]]
        w_in, w_out, w_up, w_down = weights([k_in, k_out, ("ffn_w_up", i), ("ffn_w_down", i)], None)
        dgated = mm_down_da(dx, w_down, 0, "ffn_down_da", dep=token)
        g_down = mm_down_dw(st["gated"], dx, "ffn_down_dw")
        dy3, d_p3[i] = ffn_gate_bwd(st["h3"], dgated, p3[i])
        dh3 = ffn_conv_t(dy3, p3[i])
        dxn2 = mm_cs_da(dh3, w_up, 0, t, "ffn_up_da")
        g_up = mm_cs_dw(st["xn2"], dh3, nc, "ffn_up_dw")
        dx1, d_norm[i][1] = rms_bwd(st["x1"], st["g1"], dxn2, dx, "rms_bwd")
        token = scatter({("ffn_w_down", i): g_down, ("ffn_w_up", i): g_up})
        if mixer == 0:
            do = mm_rs_da(dx1, w_out, 0, BF16, "attn_out_da", dep=token)
            g_out = mm_rs_dw(st["o"], dx1, "attn_out_dw")
            d3 = sb_attn_bwd(st["qkv"], st["ltot"], do)
            dxn = mm_qkv_da(d3, w_in, 0, "qkv_da")
            g_in = mm_qkv_dw(st["xn"], d3, w_in.shape[3], "qkv_dw")
        elif mixer == 1:
            dp = mm_rs_da(dx1, w_out, 0, BF16, "sg_out_da", dep=token)
            g_out = mm_rs_dw(st["p"], dx1, "sg_out_dw")
            dhin, d_ws, d_bfull, d_gain = sgu_bwd(st["hin"], dp, sg_gain, sg_ws, sg_bfull)
            rep.update(sg_w_s=d_ws[None], sg_b=d_bfull[None, :, :, 0], sg_norm_g=d_gain)
            dxn = mm_cs_da(dhin, w_in, 0, t, "sg_in_da")
            g_in = mm_cs_dw(st["xn"], dhin, w_in.shape[3], "sg_in_dw")
        else:
            dhg = glu_bwd(st["hg"], dx1, token)
            dyg = mm_cs_da(dhg, w_out, 0, t, "ssm_glu_da")
            g_out = mm_cs_dw(st["yg"], dhg, w_out.shape[3], "ssm_glu_dw")
            dyc, du_skip, d_ssm_d = s5_post_bwd(st["yc"], st["u"], ssm_d_full, dyg)
            dx2 = mm_s5("dx", dyc, cd, t, "s5_dx")
            dcd = mm_s5("dcd", st["x2"], dyc, t, "s5_dcd")
            g2, da2 = s5_scan_bwd(dx2, st["x2"], a2)
            du = mm_s5("du", g2, bd, t, "s5_du", res=du_skip)
            dbd = mm_s5("dbd", st["u"], g2, t, "s5_dbd")
            from_bd = lambda blk: jnp.swapaxes(_block_diag_t(blk, h_, p_), 1, 2).reshape(g_ * p_, h_)

            def disc2_bwd(c1, c2, b1, b2, t1, t2):
                return jax.vjp(_disc2, c1, c2, b1, b2)[1]((t1, t2))

            d_cre, d_cim, d_b_re, d_b_im = _single(
                disc2_bwd, [cre_c, cim_c, b_re, b_im, from_bd(dbd[0]), from_bd(dbd[1])],
                [(g_ * p_, 1)] * 2 + [(g_ * p_, h_)] * 2, "s5_disc2_bwd")

            def disc1_bwd(l1, l2, ld, t1, t2, t3, t4):
                return jax.vjp(_disc1, l1, l2, ld)[1]((t1, t2, t3, t4))

            d_lam_re, d_lam_im, d_log_dt = _single(
                disc1_bwd, [lam_re, lam_im, log_dt, da2[0].reshape(g_, p_), da2[1].reshape(g_, p_),
                            d_cre.reshape(g_, p_), d_cim.reshape(g_, p_)],
                [(g_, p_), (g_, p_), (g_, 1)], "s5_disc1_bwd")
            from_cd = lambda blk: jnp.swapaxes(_block_diag_t(blk, p_, h_), 1, 2)
            rep.update(ssm_lam_re=d_lam_re[None], ssm_lam_im=d_lam_im[None], ssm_log_dt=d_log_dt.reshape(1, g_),
                       ssm_b_re=d_b_re.reshape(1, g_, p_, h_), ssm_b_im=d_b_im.reshape(1, g_, p_, h_),
                       ssm_c_re=from_cd(dcd[0])[None], ssm_c_im=-from_cd(dcd[1])[None])
            dxn = mm_rs_da(du, w_in, 0, F32, "ssm_in_da")
            g_in = mm_rs_dw(st["xn"], du, "ssm_in_dw")
        dx, d_norm[i][0] = rms_bwd(st["x"], st["g0"], dxn, dx1, "rms_bwd")
        token = scatter({k_in: g_in, k_out: g_out})

    rep["final_norm_g"] = d_final_g.reshape(d)
    rep["ffn_conv_b"] = jnp.stack([d_p3[l][:, CONV_K, :].reshape(NDEV * nc) for l in range(depth)])

    d_norm_full = jnp.stack([jnp.concatenate(pair, axis=0) for pair in d_norm])
    d_norm_pieces = jnp.transpose(d_norm_full.reshape(depth, 2, NDEV, d // NDEV), (2, 0, 1, 3))
    d_ssm_d_pieces = jnp.transpose(d_ssm_d.reshape(1, NDEV, d // NDEV), (1, 0, 2))
    d_conv_w_pieces = jnp.stack([d_p3[l][:, :CONV_K, :] for l in range(depth)], axis=1)
    small_pieces = _pack([d_norm_pieces, d_ssm_d_pieces, d_conv_w_pieces], lead=1)
    small_received, = all_to_all([[small_pieces]], name="scatter_small_grads")
    rep_parts, = all_gather([_pack([rep[k] for k in REPLICATED])], name="gather_small_grads")
    own, landed = {}, {}
    for keys, started in scattering.items():
        for key, (src, land) in zip(keys, scatter_wait(started, dx, "scatter_grads_wait")):
            own[key], landed[key] = src, land

    grads, deltas, new_m, new_v = {}, {}, {}, {}
    for k in BIG:
        layers = range(w[k].shape[0])
        grads[k], deltas[k], new_m[k], new_v[k] = adamw_layers(
            w[k], m[k], v[k], [landed[(k, l)] for l in layers], [own[(k, l)] for l in layers], me, "adamw")
    for names, parts in ((SMALL_SHARDED, small_received[:, 0]), (REPLICATED, rep_parts)):
        res = adamw(_pack([w[k] for k in names]), _pack([m[k] for k in names]), _pack([v[k] for k in names]), parts,
                    "adamw_small")
        for tree, buf in zip((grads, deltas, new_m, new_v), res):
            for k, val in zip(names, _unpack(buf, [w[k].shape for k in names])):
                tree[k] = val
    grad_x = dx.reshape(x.shape)
    return (loss, grad_x, *[grads[k] for k in WEIGHTS], *[deltas[k] for k in WEIGHTS],
            *[new_m[k] for k in WEIGHTS], *[new_v[k] for k in WEIGHTS])


def kernel(x, norm_g, final_norm_g, sb_w_qkv, sb_w_o, sg_w_in, sg_norm_g, sg_w_s, sg_b, sg_w_o, ssm_w_in, ssm_lam_re, ssm_lam_im, ssm_log_dt, ssm_b_re, ssm_b_im, ssm_c_re, ssm_c_im, ssm_d, ssm_w_glu, ffn_w_up, ffn_conv_w, ffn_conv_b, ffn_w_down, loss_target, m_norm_g, m_final_norm_g, m_sb_w_qkv, m_sb_w_o, m_sg_w_in, m_sg_norm_g, m_sg_w_s, m_sg_b, m_sg_w_o, m_ssm_w_in, m_ssm_lam_re, m_ssm_lam_im, m_ssm_log_dt, m_ssm_b_re, m_ssm_b_im, m_ssm_c_re, m_ssm_c_im, m_ssm_d, m_ssm_w_glu, m_ffn_w_up, m_ffn_conv_w, m_ffn_conv_b, m_ffn_w_down, v_norm_g, v_final_norm_g, v_sb_w_qkv, v_sb_w_o, v_sg_w_in, v_sg_norm_g, v_sg_w_s, v_sg_b, v_sg_w_o, v_ssm_w_in, v_ssm_lam_re, v_ssm_lam_im, v_ssm_log_dt, v_ssm_b_re, v_ssm_b_im, v_ssm_c_re, v_ssm_c_im, v_ssm_d, v_ssm_w_glu, v_ffn_w_up, v_ffn_conv_w, v_ffn_conv_b, v_ffn_w_down):
    w = dict(zip(WEIGHTS, (norm_g, final_norm_g, sb_w_qkv, sb_w_o, sg_w_in, sg_norm_g, sg_w_s, sg_b, sg_w_o, ssm_w_in,
                           ssm_lam_re, ssm_lam_im, ssm_log_dt, ssm_b_re, ssm_b_im, ssm_c_re, ssm_c_im, ssm_d, ssm_w_glu,
                           ffn_w_up, ffn_conv_w, ffn_conv_b, ffn_w_down)))
    m = dict(zip(WEIGHTS, (m_norm_g, m_final_norm_g, m_sb_w_qkv, m_sb_w_o, m_sg_w_in, m_sg_norm_g, m_sg_w_s, m_sg_b,
                           m_sg_w_o, m_ssm_w_in, m_ssm_lam_re, m_ssm_lam_im, m_ssm_log_dt, m_ssm_b_re, m_ssm_b_im,
                           m_ssm_c_re, m_ssm_c_im, m_ssm_d, m_ssm_w_glu, m_ffn_w_up, m_ffn_conv_w, m_ffn_conv_b,
                           m_ffn_w_down)))
    v = dict(zip(WEIGHTS, (v_norm_g, v_final_norm_g, v_sb_w_qkv, v_sb_w_o, v_sg_w_in, v_sg_norm_g, v_sg_w_s, v_sg_b,
                           v_sg_w_o, v_ssm_w_in, v_ssm_lam_re, v_ssm_lam_im, v_ssm_log_dt, v_ssm_b_re, v_ssm_b_im,
                           v_ssm_c_re, v_ssm_c_im, v_ssm_d, v_ssm_w_glu, v_ffn_w_up, v_ffn_conv_w, v_ffn_conv_b,
                           v_ffn_w_down)))
    return _step(x, loss_target, w, m, v)
```

```python
import functools

import jax
import jax.numpy as jnp
from jax import lax
from jax.experimental import pallas as pl
from jax.experimental.pallas import tpu as pltpu

F32, BF16 = jnp.float32, jnp.bfloat16
MESH_AXES = ("x", "y", "c")
NDEV = 8
EPS = 1e-6
HEAD_DIM = 64
LANES = 128
ATT_BQ, ATT_BK = 512, 256
CHUNK = 128
SG_GROUPS = 8
SSM_GROUPS, SSM_STATE, SSM_GROUP = 64, 64, 16
SSM_PACK = 8
S5_PASSES = 1
CONV_K = 3
HALO = 16
SCAN_COLS = 256
ADAM_LR, ADAM_B1, ADAM_B2, ADAM_EPS, ADAM_WD, ADAM_STEP = 0.001, 0.9, 0.999, 1e-08, 0.01, 10
VMEM_LIMIT = 56 * 1024 * 1024

_MM = (((1,), (0,)), ((), ()))
_MM_TB = (((1,), (1,)), ((), ()))
_MM_TA = (((0,), (0,)), ((), ()))


def _params(sem):
    return pltpu.CompilerParams(dimension_semantics=sem, vmem_limit_bytes=VMEM_LIMIT)


def _rows(total, cap, mult=16):
    best = None
    for d in range(mult, min(total, cap) + 1, mult):
        if total % d == 0:
            best = d
    return best if best is not None else total


def _dot(a, b, dims, passes):
    if passes == 1:
        return lax.dot_general(a.astype(BF16), b.astype(BF16), dims, preferred_element_type=F32)
    a = a.astype(F32)
    b = b.astype(F32)
    ah = a.astype(BF16)
    bh = b.astype(BF16)
    al = (a - ah.astype(F32)).astype(BF16)
    bl = (b - bh.astype(F32)).astype(BF16)
    out = lax.dot_general(ah, bh, dims, preferred_element_type=F32)
    out = out + lax.dot_general(al, bh, dims, preferred_element_type=F32)
    return out + lax.dot_general(ah, bl, dims, preferred_element_type=F32)


def _mm(a, b, *, grid, a_blk, a_map, b_blk, b_map, o_blk, o_map, out_shape, out_dtype, name,
        dims=_MM, passes=1, res=None, res_blk=None, res_map=None, b_2d=None, acc_2d=None, dep=None):
    nk = grid[2]
    has_res = res is not None
    a_maps = list(a_map) if isinstance(a_map, (list, tuple)) else [a_map]
    b_maps = list(b_map) if isinstance(b_map, (list, tuple)) else [b_map]
    na, nb = len(a_maps), len(b_maps)
    n_in = na + nb + has_res + (dep is not None)

    def body(*refs):
        o_ref = refs[n_in]
        r_ref = refs[na + nb] if has_res else None
        av = refs[0][...] if na == 1 else jnp.concatenate([r[...] for r in refs[:na]], axis=-1)
        bv = refs[na][...] if nb == 1 else jnp.concatenate([r[...] for r in refs[na:na + nb]], axis=-1)
        if b_2d is not None:
            bv = bv.reshape(b_2d)
        part = _dot(av, bv, dims, passes)

        def finish(total):
            if has_res:
                total = total + r_ref[...].astype(F32)
            o_ref[...] = total.reshape(o_ref.shape).astype(o_ref.dtype)

        if nk == 1:
            finish(part)
        else:
            acc_ref = refs[-1]
            k = pl.program_id(2)

            @pl.when(k == 0)
            def _():
                acc_ref[...] = part

            @pl.when(k > 0)
            def _():
                acc_ref[...] += part

            @pl.when(k == nk - 1)
            def _():
                finish(acc_ref[...])

    in_specs = [pl.BlockSpec(a_blk, f) for f in a_maps] + [pl.BlockSpec(b_blk, f) for f in b_maps]
    args = [a] * na + [b] * nb
    if has_res:
        in_specs.append(pl.BlockSpec(res_blk, res_map))
        args.append(res)
    if dep is not None:
        in_specs.append(pl.BlockSpec(memory_space=pl.ANY))
        args.append(dep)
    scratch = [pltpu.VMEM(acc_2d, F32)] if nk > 1 else []
    return pl.pallas_call(
        body, grid=grid, in_specs=in_specs, out_specs=pl.BlockSpec(o_blk, o_map),
        out_shape=jax.ShapeDtypeStruct(out_shape, out_dtype), scratch_shapes=scratch, name=name,
        compiler_params=_params(("parallel", "parallel", "arbitrary")),
    )(*args)


def _cs_act_spec(ns, tm, row_of, col_of):
    if ns % LANES == 0:
        return (tm, ns), lambda *g: (row_of(*g), col_of(*g))
    return (None, tm, ns), lambda *g: (col_of(*g), row_of(*g), 0)


def mm_cs_fwd(a, w4, l, out_dtype, name, tm=512):
    m, k = a.shape
    ns = w4.shape[3]
    o_blk, o_map = _cs_act_spec(ns, tm, lambda j, i, kk: i, lambda j, i, kk: j)
    out_shape = (m, NDEV * ns) if ns % LANES == 0 else (NDEV, m, ns)
    return _mm(a, w4, grid=(NDEV, m // tm, 1), a_blk=(tm, k), a_map=lambda j, i, kk: (i, 0),
               b_blk=(None, None, k, ns), b_map=lambda j, i, kk: (j, l, 0, 0),
               o_blk=o_blk, o_map=o_map, out_shape=out_shape, out_dtype=out_dtype, name=name)


def mm_cs_da(dc, w4, l, m, name, tm=512):
    k, ns = w4.shape[2], w4.shape[3]
    a_blk, a_map = _cs_act_spec(ns, tm, lambda i, _, j: i, lambda i, _, j: j)
    return _mm(dc, w4, grid=(m // tm, 1, NDEV), a_blk=a_blk, a_map=a_map,
               b_blk=(None, None, k, ns), b_map=lambda i, _, j: (j, l, 0, 0),
               o_blk=(tm, k), o_map=lambda i, _, j: (i, 0), out_shape=(m, k), out_dtype=F32,
               dims=_MM_TB, acc_2d=(tm, k), name=name)


def mm_cs_dw(a, dc, ns, name, tk=512):
    m, k = a.shape
    b_blk, b_map = _cs_act_spec(ns, tk, lambda j, _, kk: kk, lambda j, _, kk: j)
    return _mm(a, dc, grid=(NDEV, 1, m // tk), a_blk=(tk, k), a_map=lambda j, _, kk: (kk, 0),
               b_blk=b_blk, b_map=b_map, o_blk=(None, k, ns), o_map=lambda j, _, kk: (j, 0, 0),
               out_shape=(NDEV, k, ns), out_dtype=BF16, dims=_MM_TA, acc_2d=(k, ns), name=name)


def mm_rs_fwd(a, w4, l, res, out_dtype, name, tm=512):
    m, k = a.shape
    ks, n = w4.shape[2], w4.shape[3]
    return _mm(a, w4, grid=(m // tm, 1, 1), a_blk=(tm, k), a_map=lambda i, _, kk: (i, 0),
               b_blk=(NDEV, None, ks, n), b_map=lambda i, _, kk: (0, l, 0, 0), b_2d=(k, n),
               o_blk=(tm, n), o_map=lambda i, _, kk: (i, 0), out_shape=(m, n), out_dtype=out_dtype,
               res=res, res_blk=(tm, n), res_map=lambda i, _, kk: (i, 0), name=name)


def mm_rs_da(dc, w4, l, out_dtype, name, tm=512, dep=None):
    m, n = dc.shape
    ks = w4.shape[2]
    k = NDEV * ks
    return _mm(dc, w4, grid=(m // tm, 1, 1), a_blk=(tm, n), a_map=lambda i, _, kk: (i, 0),
               b_blk=(NDEV, None, ks, n), b_map=lambda i, _, kk: (0, l, 0, 0), b_2d=(k, n),
               o_blk=(tm, k), o_map=lambda i, _, kk: (i, 0), out_shape=(m, k), out_dtype=out_dtype,
               dims=_MM_TB, name=name, dep=dep)


def mm_rs_dw(a, dc, name, tk=512):
    m, k = a.shape
    n = dc.shape[1]
    ks = k // NDEV
    return _mm(a, dc, grid=(1, 1, m // tk), a_blk=(tk, k), a_map=lambda _, __, kk: (kk, 0),
               b_blk=(tk, n), b_map=lambda _, __, kk: (kk, 0),
               o_blk=(NDEV, ks, n), o_map=lambda _, __, kk: (0, 0, 0), out_shape=(NDEV, ks, n),
               out_dtype=BF16, dims=_MM_TA, acc_2d=(k, n), name=name)


def mm_down_fwd(a3, w4, l, res, name, tm=512):
    nj, m, kc = a3.shape
    ks, n = w4.shape[2], w4.shape[3]
    return _mm(a3, w4, grid=(m // tm, 1, nj), a_blk=(None, tm, kc), a_map=lambda i, _, j: (j, i, 0),
               b_blk=(2, None, ks, n), b_map=lambda i, _, j: (j, l, 0, 0), b_2d=(kc, n),
               o_blk=(tm, n), o_map=lambda i, _, j: (i, 0), out_shape=(m, n), out_dtype=F32,
               res=res, res_blk=(tm, n), res_map=lambda i, _, j: (i, 0), acc_2d=(tm, n), name=name)


def mm_down_da(dc, w4, l, name, tm=512, dep=None):
    m, n = dc.shape
    ks = w4.shape[2]
    kc = 2 * ks
    nj = NDEV // 2
    return _mm(dc, w4, grid=(nj, m // tm, 1), a_blk=(tm, n), a_map=lambda j, i, _: (i, 0),
               b_blk=(2, None, ks, n), b_map=lambda j, i, _: (j, l, 0, 0), b_2d=(kc, n),
               o_blk=(None, tm, kc), o_map=lambda j, i, _: (j, i, 0), out_shape=(nj, m, kc),
               out_dtype=BF16, dims=_MM_TB, name=name, dep=dep)


def mm_down_dw(a3, dc, name, tk=512):
    nj, m, kc = a3.shape
    n = dc.shape[1]
    return _mm(a3, dc, grid=(nj, 1, m // tk), a_blk=(None, tk, kc), a_map=lambda j, _, kk: (j, kk, 0),
               b_blk=(tk, n), b_map=lambda j, _, kk: (kk, 0),
               o_blk=(2, kc // 2, n), o_map=lambda j, _, kk: (j, 0, 0), out_shape=(NDEV, kc // 2, n),
               out_dtype=BF16, dims=_MM_TA, acc_2d=(kc, n), name=name)


def _qkv_group_maps(d, ns, row_of, piece_of):
    per_arr, per_piece = d // LANES, ns // LANES

    def group_map(q):
        def f(*g):
            grp = piece_of(*g) * per_piece + q
            return grp // per_arr, row_of(*g), grp % per_arr
        return f

    return [group_map(q) for q in range(per_piece)]


def mm_qkv_da(d3, w4, l, name, tm=512):
    _, m, d = d3.shape
    k, ns = w4.shape[2], w4.shape[3]
    return _mm(d3, w4, grid=(m // tm, 1, NDEV),
               a_blk=(None, tm, LANES), a_map=_qkv_group_maps(d, ns, lambda i, _, j: i, lambda i, _, j: j),
               b_blk=(None, None, k, ns), b_map=lambda i, _, j: (j, l, 0, 0),
               o_blk=(tm, k), o_map=lambda i, _, j: (i, 0), out_shape=(m, k), out_dtype=F32,
               dims=_MM_TB, acc_2d=(tm, k), name=name)


def mm_qkv_dw(a, d3, ns, name, tk=512):
    m, k = a.shape
    d = d3.shape[2]
    return _mm(a, d3, grid=(NDEV, 1, m // tk), a_blk=(tk, k), a_map=lambda j, _, kk: (kk, 0),
               b_blk=(None, tk, LANES), b_map=_qkv_group_maps(d, ns, lambda j, _, kk: kk, lambda j, _, kk: j),
               o_blk=(None, k, ns), o_map=lambda j, _, kk: (j, 0, 0),
               out_shape=(NDEV, k, ns), out_dtype=BF16, dims=_MM_TA, acc_2d=(k, ns), name=name)


def _rowwise(fn, ins, outs, *, tr, name, acc_outs=()):
    rows = next(a.shape[0] if kind == "row" else a.shape[1] for a, kind in ins if kind in ("row", "row3"))
    n_in, n_out = len(ins), len(outs)
    n_read = sum(kind != "dep" for _, kind in ins)

    def body(*refs):
        vals = fn(*[r[...] for r in refs[:n_read]])
        if not isinstance(vals, (tuple, list)):
            vals = (vals,)
        for ref, val in zip(refs[n_in:n_in + n_out], vals[:n_out]):
            ref[...] = val.astype(ref.dtype)
        i = pl.program_id(0)
        for ref, val in zip(refs[n_in + n_out:], vals[n_out:]):
            val = val.astype(ref.dtype)

            @pl.when(i == 0)
            def _(ref=ref, val=val):
                ref[...] = val

            @pl.when(i > 0)
            def _(ref=ref, val=val):
                ref[...] += val

    in_specs = []
    for a, kind in ins:
        if kind == "row":
            in_specs.append(pl.BlockSpec((tr, a.shape[1]), lambda i: (i, 0)))
        elif kind == "row3":
            in_specs.append(pl.BlockSpec((a.shape[0], tr, a.shape[2]), lambda i: (0, i, 0)))
        elif kind == "dep":
            in_specs.append(pl.BlockSpec(memory_space=pl.ANY))
        else:
            in_specs.append(pl.BlockSpec(a.shape, lambda i, nd=a.ndim: (0,) * nd))
    out_specs = [pl.BlockSpec((tr, c), lambda i: (i, 0)) for c, _ in outs]
    out_specs += [pl.BlockSpec(s, lambda i, nd=len(s): (0,) * nd) for s, _ in acc_outs]
    out_shape = [jax.ShapeDtypeStruct((rows, c), dt) for c, dt in outs]
    out_shape += [jax.ShapeDtypeStruct(s, dt) for s, dt in acc_outs]
    res = pl.pallas_call(
        body, grid=(rows // tr,), in_specs=in_specs, out_specs=out_specs, out_shape=out_shape, name=name,
        compiler_params=_params(("arbitrary",) if acc_outs else ("parallel",)),
    )(*[a for a, _ in ins])
    return res


def _rms(x, g):
    return x * lax.rsqrt(jnp.mean(x * x, axis=-1, keepdims=True) + EPS) * g


def cast_into_slot(w, l, me):
    _, r, c = w.shape
    tr = _rows(r, 512)

    def body(me_ref, w_ref, o_ref):
        o_ref[...] = w_ref[...].astype(o_ref.dtype)

    return pl.pallas_call(
        body,
        grid_spec=pltpu.PrefetchScalarGridSpec(
            num_scalar_prefetch=1, grid=(r // tr,),
            in_specs=[pl.BlockSpec((None, tr, c), lambda i, me_ref: (l, i, 0))],
            out_specs=pl.BlockSpec((None, tr, c), lambda i, me_ref: (me_ref[0], i, 0))),
        out_shape=jax.ShapeDtypeStruct((NDEV, r, c), BF16), name="cast_into_slot",
        compiler_params=_params(("parallel",)),
    )(me, w)


def rms_fwd(x, g, name):
    out, = _rowwise(_rms, [(x, "row"), (g, "full")], [(x.shape[1], BF16)], tr=256, name=name)
    return out


def rms_bwd(x, g, dy, dres, name):
    def fn(xv, gv, dyv, drv):
        _, vjp = jax.vjp(_rms, xv, gv)
        dx, dg = vjp(dyv.astype(F32))
        return drv + dx, dg

    d = x.shape[1]
    return _rowwise(fn, [(x, "row"), (g, "full"), (dy, "row"), (dres, "row")], [(d, F32)], tr=256, name=name,
                    acc_outs=[((1, d), F32)])


def loss_head(x, g, tgt):
    def f(xv, gv, tv):
        err = jnp.square(_rms(xv, gv) - tv)
        return 0.5 * jnp.sum(jnp.mean(err, axis=-1))

    def fn(xv, gv, tv):
        val, (dx, dg) = jax.value_and_grad(f, argnums=(0, 1))(xv, gv, tv)
        return dx, jnp.full((1, LANES), val, F32), dg

    d = x.shape[1]
    return _rowwise(fn, [(x, "row"), (g, "full"), (tgt, "row")], [(d, F32)], tr=256, name="loss_head",
                    acc_outs=[((1, LANES), F32), ((1, d), F32)])


def _glu(hg, x):
    half = hg.shape[1] // 2
    return x + hg[:, :half] * jax.nn.sigmoid(hg[:, half:])


def glu_fwd(hg, x):
    out, = _rowwise(lambda h, xv: _glu(h.astype(F32), xv), [(hg, "row"), (x, "row")], [(x.shape[1], F32)],
                    tr=256, name="glu_fwd")
    return out


def glu_bwd(hg, dx1, dep):
    def fn(h, d):
        _, vjp = jax.vjp(lambda hv: _glu(hv, jnp.zeros_like(d)), h.astype(F32))
        return vjp(d)[0]

    out, = _rowwise(fn, [(hg, "row"), (dx1, "row"), (dep, "dep")], [(hg.shape[1], BF16)], tr=256, name="glu_bwd")
    return out


def _s5_post(yc, u, d):
    return jax.nn.gelu(yc + d * u)


def s5_post_fwd(yc, u, d):
    out, = _rowwise(_s5_post, [(yc, "row"), (u, "row"), (d, "full")], [(yc.shape[1], BF16)], tr=256,
                    name="s5_post_fwd")
    return out


def s5_post_bwd(yc, u, d, dyg):
    def fn(ycv, uv, dv, g):
        _, vjp = jax.vjp(_s5_post, ycv, uv, dv)
        return vjp(g.astype(F32))

    dm = yc.shape[1]
    return _rowwise(fn, [(yc, "row"), (u, "row"), (d, "full"), (dyg, "row")], [(dm, F32), (dm, F32)], tr=256,
                    name="s5_post_bwd", acc_outs=[((1, dm), F32)])


def _adam_update(wv, mv, vv, g):
    m2 = ADAM_B1 * mv + (1.0 - ADAM_B1) * g
    v2 = ADAM_B2 * vv + (1.0 - ADAM_B2) * jnp.square(g)
    m_hat = m2 / (1.0 - ADAM_B1 ** ADAM_STEP)
    v_hat = v2 / (1.0 - ADAM_B2 ** ADAM_STEP)
    delta = -ADAM_LR * (m_hat / (jnp.sqrt(v_hat) + ADAM_EPS) + ADAM_WD * wv)
    return g, delta, m2, v2


def adamw(w, m, v, g_parts, name):
    def fn(wv, mv, vv, gp):
        g = gp[0].astype(F32)
        for p in range(1, gp.shape[0]):
            g = g + gp[p].astype(F32)
        return _adam_update(wv, mv, vv, g)

    c = w.shape[1]
    return _rowwise(fn, [(w, "row"), (m, "row"), (v, "row"), (g_parts, "row3")], [(c, F32)] * 4,
                    tr=_rows(w.shape[0], 256), name=name)


def adamw_layers(w, m, v, lands, owns, me, name):
    nl, r, c = w.shape
    tr = _rows(r, 256)

    def body(me_ref, w_ref, m_ref, v_ref, *rest):
        land_refs, own_refs, out_refs = rest[:nl], rest[nl:2 * nl], rest[2 * nl:]
        for l in range(nl):
            @pl.when(pl.program_id(0) == l)
            def _(l=l):
                g = own_refs[l][...].astype(F32)
                for p in range(NDEV - 1):
                    g = g + land_refs[l][p].astype(F32)
                for ref, val in zip(out_refs, _adam_update(w_ref[...], m_ref[...], v_ref[...], g)):
                    ref[...] = val

    def rows_of(l):
        return lambda li, i, me_ref: jnp.where(li == l, i, 0)

    wspec = pl.BlockSpec((None, tr, c), lambda li, i, me_ref: (li, i, 0))
    in_specs = [wspec] * 3
    in_specs += [pl.BlockSpec((NDEV - 1, tr, c), lambda li, i, me_ref, f=rows_of(l): (0, f(li, i, me_ref), 0))
                 for l in range(nl)]
    in_specs += [pl.BlockSpec((None, tr, c), lambda li, i, me_ref, f=rows_of(l): (me_ref[0], f(li, i, me_ref), 0))
                 for l in range(nl)]
    return pl.pallas_call(
        body,
        grid_spec=pltpu.PrefetchScalarGridSpec(
            num_scalar_prefetch=1, grid=(nl, r // tr), in_specs=in_specs, out_specs=[wspec] * 4),
        out_shape=[jax.ShapeDtypeStruct(w.shape, F32)] * 4, name=name, compiler_params=_params(("parallel", "parallel")),
    )(me, w, m, v, *lands, *owns)


def _conv_rows(cur, halo, p, first):
    r = cur.shape[0]
    ext = jnp.concatenate([jnp.where(first, 0.0, halo), cur], axis=0)
    s1 = pltpu.roll(ext, 1, 0)[HALO:]
    s2 = pltpu.roll(ext, 2, 0)[HALO:]
    return p[0:1] * s2 + p[1:2] * s1 + p[2:3] * cur + p[3:4], s1, s2


def ffn_gate_fwd(h3, p3, tr=256):
    _, t, c = h3.shape
    half = NDEV // 2

    def body(a_ref, ah_ref, g_ref, gh_ref, pa_ref, pg_ref, o_ref):
        first = pl.program_id(1) == 0
        ya, _, _ = _conv_rows(a_ref[...].astype(F32), ah_ref[...].astype(F32), pa_ref[...], first)
        yg, _, _ = _conv_rows(g_ref[...].astype(F32), gh_ref[...].astype(F32), pg_ref[...], first)
        o_ref[...] = (jax.nn.silu(yg) * ya).astype(o_ref.dtype)

    main = lambda off: pl.BlockSpec((None, tr, c), lambda j, i: (j + off, i, 0))
    halo = lambda off: pl.BlockSpec((None, HALO, c), lambda j, i: (j + off, jnp.maximum(i * (tr // HALO) - 1, 0), 0))
    par = lambda off: pl.BlockSpec((None, 8, c), lambda j, i: (j + off, 0, 0))
    return pl.pallas_call(
        body, grid=(half, t // tr),
        in_specs=[main(0), halo(0), main(half), halo(half), par(0), par(half)],
        out_specs=pl.BlockSpec((None, tr, c), lambda j, i: (j, i, 0)),
        out_shape=jax.ShapeDtypeStruct((half, t, c), BF16), name="ffn_gate_fwd",
        compiler_params=_params(("parallel", "parallel")),
    )(h3, h3, h3, h3, p3, p3)


def ffn_gate_bwd(h3, dgated3, p3, tr=256):
    _, t, c = h3.shape
    half = NDEV // 2

    def body(a_ref, ah_ref, g_ref, gh_ref, dg_ref, pa_ref, pg_ref, dya_ref, dyg_ref, dpa_ref, dpg_ref):
        i = pl.program_id(1)
        first = i == 0
        a = a_ref[...].astype(F32)
        g = g_ref[...].astype(F32)
        ya, a1, a2 = _conv_rows(a, ah_ref[...].astype(F32), pa_ref[...], first)
        yg, g1, g2 = _conv_rows(g, gh_ref[...].astype(F32), pg_ref[...], first)
        d = dg_ref[...].astype(F32)
        sig = jax.nn.sigmoid(yg)
        d_ya = (d * (yg * sig)).astype(dya_ref.dtype)
        d_yg = (d * ya * (sig * (1.0 + yg * (1.0 - sig)))).astype(dyg_ref.dtype)
        dya_ref[...] = d_ya
        dyg_ref[...] = d_yg
        for dy, cur, s1, s2, dp_ref in ((d_ya.astype(F32), a, a1, a2, dpa_ref), (d_yg.astype(F32), g, g1, g2, dpg_ref)):
            rows = [jnp.sum(dy * s2, axis=0, keepdims=True), jnp.sum(dy * s1, axis=0, keepdims=True),
                    jnp.sum(dy * cur, axis=0, keepdims=True), jnp.sum(dy, axis=0, keepdims=True)]
            dp = jnp.concatenate(rows + [jnp.zeros((4, c), F32)], axis=0)

            @pl.when(first)
            def _(dp_ref=dp_ref, dp=dp):
                dp_ref[...] = dp

            @pl.when(i > 0)
            def _(dp_ref=dp_ref, dp=dp):
                dp_ref[...] += dp

    main = lambda off: pl.BlockSpec((None, tr, c), lambda j, i: (j + off, i, 0))
    halo = lambda off: pl.BlockSpec((None, HALO, c), lambda j, i: (j + off, jnp.maximum(i * (tr // HALO) - 1, 0), 0))
    par = lambda off: pl.BlockSpec((None, 8, c), lambda j, i: (j + off, 0, 0))
    return pl.pallas_call(
        body, grid=(half, t // tr),
        in_specs=[main(0), halo(0), main(half), halo(half), main(0), par(0), par(half)],
        out_specs=[main(0), main(0), par(0), par(0)],
        out_shape=[jax.ShapeDtypeStruct((half, t, c), BF16)] * 2 + [jax.ShapeDtypeStruct((half, 8, c), F32)] * 2,
        name="ffn_gate_bwd", compiler_params=_params(("parallel", "arbitrary")),
    )(h3, h3, h3, h3, dgated3, p3, p3)


def ffn_conv_t(dy_a, dy_g, p3, tr=256):
    half, t, c = dy_a.shape
    nblk = t // tr

    def body(a_ref, ah_ref, g_ref, gh_ref, p_ref, o_ref):
        is_a = pl.program_id(0) < half
        last = pl.program_id(1) == nblk - 1
        cur = jnp.where(is_a, a_ref[...], g_ref[...]).astype(F32)
        nxt = jnp.where(is_a, ah_ref[...], gh_ref[...]).astype(F32)
        ext = jnp.concatenate([cur, jnp.where(last, 0.0, nxt)], axis=0)
        n = tr + HALO
        s1 = pltpu.roll(ext, n - 1, 0)[:tr]
        s2 = pltpu.roll(ext, n - 2, 0)[:tr]
        p = p_ref[...]
        o_ref[...] = (p[2:3] * cur + p[1:2] * s1 + p[0:1] * s2).astype(o_ref.dtype)

    main = pl.BlockSpec((None, tr, c), lambda j, i: (j % half, i, 0))
    halo = pl.BlockSpec((None, HALO, c), lambda j, i: (j % half, jnp.minimum((i + 1) * (tr // HALO), t // HALO - 1), 0))
    return pl.pallas_call(
        body, grid=(NDEV, nblk),
        in_specs=[main, halo, main, halo, pl.BlockSpec((None, 8, c), lambda j, i: (j, 0, 0))],
        out_specs=pl.BlockSpec((None, tr, c), lambda j, i: (j, i, 0)),
        out_shape=jax.ShapeDtypeStruct((NDEV, t, c), BF16), name="ffn_conv_t",
        compiler_params=_params(("parallel", "parallel")),
    )(dy_a, dy_a, dy_g, dy_g, p3)


def _att_consts(bq, bk):
    lane = lax.broadcasted_iota(jnp.int32, (1, LANES), 1)
    heads = (lane < HEAD_DIM, lane >= HEAD_DIM)
    rr = lax.broadcasted_iota(jnp.int32, (bq, bk), 0)
    cc = lax.broadcasted_iota(jnp.int32, (bq, bk), 1)
    kr = lax.broadcasted_iota(jnp.int32, (bk, bk), 0)
    kc = lax.broadcasted_iota(jnp.int32, (bk, bk), 1)
    return heads, rr, cc, kr, kc


def _split_dot(x, tri, parts):
    out = None
    for _ in range(parts):
        piece = x.astype(BF16)
        x = x - piece.astype(F32)
        term = jnp.dot(piece, tri, preferred_element_type=F32)
        out = term if out is None else out + term
    return out


def _att_logits(qh, k):
    z = lax.dot_general(qh, k, _MM_TB, preferred_element_type=F32)
    lsp = jnp.minimum(z, 0.0) - jnp.log(1.0 + jnp.exp(-jnp.abs(z)))
    return lsp, lsp - z


def _per_head(heads, a, b):
    return jnp.where(heads[0], a, b)


def sb_attn_fwd(qkv):
    t, d3 = qkv.shape
    d = d3 // 3
    npair = d // LANES
    bq, bk = min(ATT_BQ, t), min(ATT_BK, t)
    kpq = bq // bk

    def body(q_ref, k_ref, v_ref, o_ref, lt_ref, acc_ref):
        heads, rr, cc, kr, kc = _att_consts(bq, bk)
        suffix = (kr > kc).astype(BF16)

        def trip(qh, k0, valid, runs):
            k = k_ref[pl.ds(k0, bk), :]
            v = v_ref[pl.ds(k0, bk), :]
            new_runs = []
            for h in range(2):
                lsp, lraw = _att_logits(qh[h], k)
                lm = lraw if valid is None else jnp.where(valid, lraw, 0.0)
                w = jnp.exp(lsp + _split_dot(lm, suffix, 2) + runs[h])
                if valid is not None:
                    w = jnp.where(valid, w, 0.0)
                acc_ref[h] += jnp.dot(w.astype(BF16), v, preferred_element_type=F32)
                new_runs.append(runs[h] + jnp.sum(lm, axis=1, keepdims=True))
            return tuple(new_runs)

        def q_loop(qb, _):
            q0 = pl.multiple_of(qb * bq, bq)
            q = q_ref[pl.ds(q0, bq), :] * 0.125
            qh = [jnp.where(hm, q, 0.0).astype(BF16) for hm in heads]
            acc_ref[...] = jnp.zeros_like(acc_ref)
            runs = (jnp.zeros((bq, 1), F32),) * 2
            for dblk in reversed(range(kpq)):
                runs = trip(qh, pl.multiple_of(q0 + dblk * bk, bk), dblk * bk + cc < rr, runs)
            nleft = qb * kpq
            runs = lax.fori_loop(
                0, nleft, lambda i, r: trip(qh, pl.multiple_of((nleft - 1 - i) * bk, bk), None, r), runs)
            o_ref[pl.ds(q0, bq), :] = _per_head(heads, acc_ref[0], acc_ref[1])
            lt_ref[pl.ds(q0, bq), :] = _per_head(heads, runs[0], runs[1])
            return 0

        lax.fori_loop(0, t // bq, q_loop, 0)

    col = lambda off: pl.BlockSpec((t, LANES), lambda p: (0, p + off))
    return pl.pallas_call(
        body, grid=(npair,), in_specs=[col(0), col(npair), col(2 * npair)], out_specs=[col(0), col(0)],
        out_shape=[jax.ShapeDtypeStruct((t, d), F32)] * 2, scratch_shapes=[pltpu.VMEM((2, bq, LANES), F32)],
        name="sb_attn_fwd", compiler_params=_params(("parallel",)),
    )(qkv, qkv, qkv)


def sb_attn_bwd(qkv, ltot, do):
    t, d3 = qkv.shape
    d = d3 // 3
    npair = d // LANES
    bq, bk = min(ATT_BQ, t), min(ATT_BK, t)
    kpq = bq // bk

    def body(q_ref, k_ref, v_ref, lt_ref, do_ref, d_ref, dk_acc, dv_acc, dq_acc):
        heads, rr, cc, kr, kc = _att_consts(bq, bk)
        prefix_incl = (kr <= kc).astype(BF16)
        prefix_excl = (kr < kc).astype(BF16)
        dk_acc[...] = jnp.zeros_like(dk_acc)
        dv_acc[...] = jnp.zeros_like(dv_acc)

        def trip(qh, doh, lt, k0, valid, carry):
            lruns, gruns = carry
            k = k_ref[pl.ds(k0, bk), :]
            v = v_ref[pl.ds(k0, bk), :]
            new_lruns, new_gruns = [], []
            dk_blk = jnp.zeros((bk, LANES), F32)
            dv_blk = jnp.zeros((bk, LANES), F32)
            for h in range(2):
                lsp, lraw = _att_logits(qh[h], k)
                lm = lraw if valid is None else jnp.where(valid, lraw, 0.0)
                right = lt[h] - (lruns[h] + _split_dot(lm, prefix_incl, 2))
                w = jnp.exp(lsp + right)
                if valid is not None:
                    w = jnp.where(valid, w, 0.0)
                g = lax.dot_general(doh[h], v, _MM_TB, preferred_element_type=F32) * w
                left = gruns[h] + _split_dot(g, prefix_excl, 2)
                dz = g * jnp.exp(lraw) - jnp.exp(lsp) * left
                if valid is not None:
                    dz = jnp.where(valid, dz, 0.0)
                dz = dz.astype(BF16)
                kh = jnp.where(heads[h], k, 0.0).astype(BF16)
                dq_acc[h] += jnp.dot(dz, kh, preferred_element_type=F32)
                dk_blk = dk_blk + lax.dot_general(dz, qh[h], _MM_TA, preferred_element_type=F32)
                dv_blk = dv_blk + lax.dot_general(w.astype(BF16), doh[h], _MM_TA, preferred_element_type=F32)
                new_lruns.append(lruns[h] + jnp.sum(lm, axis=1, keepdims=True))
                new_gruns.append(gruns[h] + jnp.sum(g, axis=1, keepdims=True))
            dk_acc[pl.ds(k0, bk), :] += dk_blk
            dv_acc[pl.ds(k0, bk), :] += dv_blk
            return tuple(new_lruns), tuple(new_gruns)

        def q_loop(qb, _):
            q0 = pl.multiple_of(qb * bq, bq)
            q = q_ref[pl.ds(q0, bq), :] * 0.125
            dout = do_ref[pl.ds(q0, bq), :]
            lt2 = lt_ref[pl.ds(q0, bq), :]
            qh = [jnp.where(hm, q, 0.0).astype(BF16) for hm in heads]
            doh = [jnp.where(hm, dout, 0.0).astype(BF16) for hm in heads]
            lt = [jnp.max(jnp.where(hm, lt2, -jnp.inf), axis=1, keepdims=True) for hm in heads]
            dq_acc[...] = jnp.zeros_like(dq_acc)
            col = (jnp.zeros((bq, 1), F32),) * 2
            carry = lax.fori_loop(
                0, qb * kpq, lambda kb, c: trip(qh, doh, lt, pl.multiple_of(kb * bk, bk), None, c), (col, col))
            for dblk in range(kpq):
                carry = trip(qh, doh, lt, pl.multiple_of(q0 + dblk * bk, bk), dblk * bk + cc < rr, carry)
            d_ref[0, pl.ds(q0, bq), :] = ((dq_acc[0] + dq_acc[1]) * 0.125).astype(d_ref.dtype)
            return 0

        lax.fori_loop(0, t // bq, q_loop, 0)
        d_ref[1] = dk_acc[...].astype(d_ref.dtype)
        d_ref[2] = dv_acc[...].astype(d_ref.dtype)

    col = lambda off: pl.BlockSpec((t, LANES), lambda p: (0, p + off))
    return pl.pallas_call(
        body, grid=(npair,), in_specs=[col(0), col(npair), col(2 * npair), col(0), col(0)],
        out_specs=pl.BlockSpec((3, t, LANES), lambda p: (0, 0, p)),
        out_shape=jax.ShapeDtypeStruct((3, t, d), BF16),
        scratch_shapes=[pltpu.VMEM((t, LANES), F32), pltpu.VMEM((t, LANES), F32), pltpu.VMEM((2, bq, LANES), F32)],
        name="sb_attn_bwd", compiler_params=_params(("parallel",)),
    )(qkv, qkv, qkv, ltot, do)


def _sgu_parts(hin, g, ws_ref, bf_ref):
    width = hin.shape[1] // 2
    h = jax.nn.gelu(hin)
    u, v = h[:, :width], h[:, width:]
    r = lax.rsqrt(jnp.mean(v * v, axis=-1, keepdims=True) + EPS)
    vn = v * r * g
    rr = lax.broadcasted_iota(jnp.int32, (CHUNK, CHUNK), 0)
    cc = lax.broadcasted_iota(jnp.int32, (CHUNK, CHUNK), 1)
    causal = cc <= rr
    wcs = [jnp.where(causal, ws_ref[gi], 0.0).astype(BF16) for gi in range(SG_GROUPS)]
    sv = jnp.concatenate(
        [jnp.dot(wcs[gi], vn[:, gi * CHUNK:(gi + 1) * CHUNK].astype(BF16), preferred_element_type=F32) + bf_ref[gi]
         for gi in range(SG_GROUPS)], axis=1)
    return u, v, r, vn, wcs, sv, causal


def sgu_fwd(hin, g, ws, bfull):
    t, w2 = hin.shape
    width = w2 // 2

    def body(h_ref, g_ref, ws_ref, bf_ref, o_ref):
        u, _, _, _, _, sv, _ = _sgu_parts(h_ref[...].astype(F32), g_ref[...], ws_ref, bf_ref)
        o_ref[...] = (u * sv).astype(o_ref.dtype)

    full = lambda a: pl.BlockSpec(a.shape, lambda i, nd=a.ndim: (0,) * nd)
    return pl.pallas_call(
        body, grid=(t // CHUNK,), in_specs=[pl.BlockSpec((CHUNK, w2), lambda i: (i, 0)), full(g), full(ws), full(bfull)],
        out_specs=pl.BlockSpec((CHUNK, width), lambda i: (i, 0)), out_shape=jax.ShapeDtypeStruct((t, width), BF16),
        name="sgu_fwd", compiler_params=_params(("parallel",)),
    )(hin, g, ws, bfull)


def sgu_bwd(hin, dp, g, ws, bfull):
    t, w2 = hin.shape
    width = w2 // 2

    def body(h_ref, dp_ref, g_ref, ws_ref, bf_ref, dh_ref, dws_ref, dbf_ref, dg_ref):
        i = pl.program_id(0)
        hin_v = h_ref[...].astype(F32)
        gv = g_ref[...]
        u, v, r, vn, wcs, sv, causal = _sgu_parts(hin_v, gv, ws_ref, bf_ref)
        dpv = dp_ref[...].astype(F32)
        du = dpv * sv
        dsv = dpv * u
        dvn_parts, dws_parts, dbf_parts = [], [], []
        for gi in range(SG_GROUPS):
            dsv_g = dsv[:, gi * CHUNK:(gi + 1) * CHUNK]
            dsv_b = dsv_g.astype(BF16)
            dvn_parts.append(lax.dot_general(wcs[gi], dsv_b, _MM_TA, preferred_element_type=F32))
            vn_b = vn[:, gi * CHUNK:(gi + 1) * CHUNK].astype(BF16)
            dws_parts.append(jnp.where(causal, lax.dot_general(dsv_b, vn_b, _MM_TB, preferred_element_type=F32), 0.0))
            dbf_parts.append(jnp.broadcast_to(jnp.sum(dsv_g, axis=1, keepdims=True), (CHUNK, CHUNK)))
        dvn = jnp.concatenate(dvn_parts, axis=1)
        dgain = jnp.sum(dvn * v * r, axis=0, keepdims=True)
        gvv = dvn * gv
        dv = r * gvv - v * (r * r * r) * jnp.mean(v * gvv, axis=-1, keepdims=True)
        _, vjp = jax.vjp(jax.nn.gelu, hin_v)
        dh_ref[...] = vjp(jnp.concatenate([du, dv], axis=1))[0].astype(dh_ref.dtype)

        @pl.when(i == 0)
        def _():
            for gi in range(SG_GROUPS):
                dws_ref[gi] = dws_parts[gi]
                dbf_ref[gi] = dbf_parts[gi]
            dg_ref[...] = dgain

        @pl.when(i > 0)
        def _():
            for gi in range(SG_GROUPS):
                dws_ref[gi] += dws_parts[gi]
                dbf_ref[gi] += dbf_parts[gi]
            dg_ref[...] += dgain

    full = lambda a: pl.BlockSpec(a.shape, lambda i, nd=a.ndim: (0,) * nd)
    sq = (SG_GROUPS, CHUNK, CHUNK)
    return pl.pallas_call(
        body, grid=(t // CHUNK,),
        in_specs=[pl.BlockSpec((CHUNK, w2), lambda i: (i, 0)), pl.BlockSpec((CHUNK, width), lambda i: (i, 0)),
                  full(g), full(ws), full(bfull)],
        out_specs=[pl.BlockSpec((CHUNK, w2), lambda i: (i, 0)), pl.BlockSpec(sq, lambda i: (0, 0, 0)),
                   pl.BlockSpec(sq, lambda i: (0, 0, 0)), pl.BlockSpec((1, width), lambda i: (0, 0))],
        out_shape=[jax.ShapeDtypeStruct((t, w2), BF16), jax.ShapeDtypeStruct(sq, F32), jax.ShapeDtypeStruct(sq, F32),
                   jax.ShapeDtypeStruct((1, width), F32)],
        name="sgu_bwd", compiler_params=_params(("arbitrary",)),
    )(hin, dp, g, ws, bfull)


def _disc1(lam_re, lam_im, log_dt):
    lr = jnp.minimum(lam_re, -1e-4)
    li = lam_im
    dt = jnp.exp(log_dt)
    mag = jnp.exp(dt * lr)
    ar = mag * jnp.cos(dt * li)
    ai = mag * jnp.sin(dt * li)
    den = lr * lr + li * li
    return ar, ai, ((ar - 1.0) * lr + ai * li) / den, (ai * lr - (ar - 1.0) * li) / den


def _disc2(cre, cim, b_re, b_im):
    return cre * b_re - cim * b_im, cre * b_im + cim * b_re


def _single(fn, ins, out_shapes, name):
    n = len(ins)

    def body(*refs):
        vals = fn(*[r[...] for r in refs[:n]])
        for ref, val in zip(refs[n:], vals):
            ref[...] = val

    return pl.pallas_call(body, out_shape=[jax.ShapeDtypeStruct(s, F32) for s in out_shapes], name=name)(*ins)


SCAN_SEGMENTS = 8


def _cpow(ar, ai, n):
    rr, ri = None, None
    while n:
        if n & 1:
            rr, ri = (ar, ai) if rr is None else (rr * ar - ri * ai, rr * ai + ri * ar)
        ar, ai = ar * ar - ai * ai, 2.0 * ar * ai
        n >>= 1
    return rr, ri


def _edge_states(er, ei, pr, pi, reverse):
    ns = SCAN_SEGMENTS
    zero = jnp.zeros_like(er[0:1])
    rows_r, rows_i = [None] * ns, [None] * ns
    order = range(ns - 1, -1, -1) if reverse else range(ns)
    prev = None
    for s in order:
        if prev is None:
            rows_r[s], rows_i[s] = zero, zero
        else:
            cr, ci = rows_r[prev], rows_i[prev]
            rows_r[s] = er[prev:prev + 1] + pr * cr - pi * ci
            rows_i[s] = ei[prev:prev + 1] + pr * ci + pi * cr
        prev = s
    return jnp.concatenate(rows_r, axis=0), jnp.concatenate(rows_i, axis=0)


def s5_scan_fwd(bu2, a2):
    _, t, n = bu2.shape
    cb, ns = SCAN_COLS, SCAN_SEGMENTS
    seg = t // ns

    def body(bu_ref, a_ref, x_ref):
        ar, ai = a_ref[0:1, :], a_ref[1:2, :]

        def local(i, carry):
            xr, xi = carry
            xr, xi = ar * xr - ai * xi + bu_ref[0, :, i, :], ar * xi + ai * xr + bu_ref[1, :, i, :]
            x_ref[0, :, i, :] = xr
            x_ref[1, :, i, :] = xi
            return xr, xi

        zero = jnp.zeros((ns, cb), F32)
        er, ei = lax.fori_loop(0, seg, local, (zero, zero))
        cr, ci = _edge_states(er, ei, *_cpow(ar, ai, seg), reverse=False)

        def fix(i, carry):
            wr, wi = carry
            wr, wi = wr * ar - wi * ai, wr * ai + wi * ar
            x_ref[0, :, i, :] += wr * cr - wi * ci
            x_ref[1, :, i, :] += wr * ci + wi * cr
            return wr, wi

        lax.fori_loop(0, seg, fix, (jnp.ones((1, cb), F32), jnp.zeros((1, cb), F32)))

    blk = pl.BlockSpec((2, ns, seg, cb), lambda j: (0, 0, 0, j))
    out = pl.pallas_call(
        body, grid=(n // cb,), in_specs=[blk, pl.BlockSpec((2, cb), lambda j: (0, j))], out_specs=blk,
        out_shape=jax.ShapeDtypeStruct((2, ns, seg, n), F32), name="s5_scan_fwd", compiler_params=_params(("parallel",)),
    )(bu2.reshape(2, ns, seg, n), a2)
    return out.reshape(2, t, n)


def s5_scan_bwd(dx2, x2, a2):
    _, t, n = dx2.shape
    cb, ns = SCAN_COLS, SCAN_SEGMENTS
    seg = t // ns

    def body(dx_ref, x_ref, a_ref, g_ref, da_ref):
        ar, ai = a_ref[0:1, :], a_ref[1:2, :]

        def local(s, carry):
            gr, gi = carry
            i = seg - 1 - s
            gr, gi = dx_ref[0, :, i, :] + ar * gr + ai * gi, dx_ref[1, :, i, :] - ai * gr + ar * gi
            g_ref[0, :, i, :] = gr
            g_ref[1, :, i, :] = gi
            return gr, gi

        zero = jnp.zeros((ns, cb), F32)
        er, ei = lax.fori_loop(0, seg, local, (zero, zero))
        cr, ci = _edge_states(er, ei, *_cpow(ar, -ai, seg), reverse=True)
        row = lax.broadcasted_iota(jnp.int32, (ns, cb), 0)
        before_r = jnp.where(row == 0, 0.0, pltpu.roll(x_ref[0, :, seg - 1, :], 1, 0))
        before_i = jnp.where(row == 0, 0.0, pltpu.roll(x_ref[1, :, seg - 1, :], 1, 0))

        def fix(s, carry):
            wr, wi, dar, dai = carry
            i = seg - 1 - s
            wr, wi = wr * ar + wi * ai, wi * ar - wr * ai
            gr = g_ref[0, :, i, :] + wr * cr - wi * ci
            gi = g_ref[1, :, i, :] + wr * ci + wi * cr
            g_ref[0, :, i, :] = gr
            g_ref[1, :, i, :] = gi
            ip = jnp.maximum(i - 1, 0)
            xpr = jnp.where(i == 0, before_r, x_ref[0, :, ip, :])
            xpi = jnp.where(i == 0, before_i, x_ref[1, :, ip, :])
            return wr, wi, dar + gr * xpr + gi * xpi, dai + gi * xpr - gr * xpi

        one, z1 = jnp.ones((1, cb), F32), jnp.zeros((1, cb), F32)
        _, _, dar, dai = lax.fori_loop(0, seg, fix, (one, z1, zero, zero))
        da_ref[0:1, :] = jnp.sum(dar, axis=0, keepdims=True)
        da_ref[1:2, :] = jnp.sum(dai, axis=0, keepdims=True)

    blk = pl.BlockSpec((2, ns, seg, cb), lambda j: (0, 0, 0, j))
    vec = pl.BlockSpec((2, cb), lambda j: (0, j))
    g4, da = pl.pallas_call(
        body, grid=(n // cb,), in_specs=[blk, blk, vec], out_specs=[blk, vec],
        out_shape=[jax.ShapeDtypeStruct((2, ns, seg, n), F32), jax.ShapeDtypeStruct((2, n), F32)],
        name="s5_scan_bwd", compiler_params=_params(("parallel",)),
    )(dx2.reshape(2, ns, seg, n), x2.reshape(2, ns, seg, n), a2)
    return g4.reshape(2, t, n), da


_SP_U = SSM_PACK * SSM_GROUP
_SP_X = SSM_PACK * SSM_STATE
_NKB = SSM_GROUPS // SSM_PACK


def mm_s5(kind, a, b, m, name, res=None, tm=512):
    kw = dict(passes=S5_PASSES, name=name)
    xblk = lambda row, sel, col: ((None, tm, _SP_X), lambda *g: (sel(*g), row(*g), col(*g)))
    if kind == "bu":
        o_blk, o_map = xblk(lambda g, i, k: i, lambda g, i, k: g // _NKB, lambda g, i, k: g % _NKB)
        return _mm(a, b, grid=(2 * _NKB, m // tm, 1), a_blk=(tm, _SP_U), a_map=lambda g, i, k: (i, g % _NKB),
                   b_blk=(None, None, _SP_U, _SP_X), b_map=lambda g, i, k: (g // _NKB, g % _NKB, 0, 0),
                   o_blk=o_blk, o_map=o_map, out_shape=(2, m, _NKB * _SP_X), out_dtype=F32, **kw)
    if kind == "yc":
        a_blk, a_map = xblk(lambda j, i, k: i, lambda j, i, k: k, lambda j, i, k: j)
        return _mm(a, b, grid=(_NKB, m // tm, 2), a_blk=a_blk, a_map=a_map,
                   b_blk=(None, None, _SP_X, _SP_U), b_map=lambda j, i, k: (k, j, 0, 0),
                   o_blk=(tm, _SP_U), o_map=lambda j, i, k: (i, j), out_shape=(m, _NKB * _SP_U), out_dtype=F32,
                   acc_2d=(tm, _SP_U), **kw)
    if kind == "dx":
        o_blk, o_map = xblk(lambda g, i, k: i, lambda g, i, k: g // _NKB, lambda g, i, k: g % _NKB)
        return _mm(a, b, grid=(2 * _NKB, m // tm, 1), a_blk=(tm, _SP_U), a_map=lambda g, i, k: (i, g % _NKB),
                   b_blk=(None, None, _SP_X, _SP_U), b_map=lambda g, i, k: (g // _NKB, g % _NKB, 0, 0),
                   o_blk=o_blk, o_map=o_map, out_shape=(2, m, _NKB * _SP_X), out_dtype=F32, dims=_MM_TB, **kw)
    if kind == "dcd":
        a_blk, a_map = xblk(lambda g, _, k: k, lambda g, _, k: g // _NKB, lambda g, _, k: g % _NKB)
        return _mm(a, b, grid=(2 * _NKB, 1, m // tm), a_blk=a_blk, a_map=a_map,
                   b_blk=(tm, _SP_U), b_map=lambda g, _, k: (k, g % _NKB),
                   o_blk=(None, None, _SP_X, _SP_U), o_map=lambda g, _, k: (g // _NKB, g % _NKB, 0, 0),
                   out_shape=(2, _NKB, _SP_X, _SP_U), out_dtype=F32, dims=_MM_TA, acc_2d=(_SP_X, _SP_U), **kw)
    if kind == "du":
        a_blk, a_map = xblk(lambda j, i, k: i, lambda j, i, k: k, lambda j, i, k: j)
        return _mm(a, b, grid=(_NKB, m // tm, 2), a_blk=a_blk, a_map=a_map,
                   b_blk=(None, None, _SP_U, _SP_X), b_map=lambda j, i, k: (k, j, 0, 0),
                   o_blk=(tm, _SP_U), o_map=lambda j, i, k: (i, j), out_shape=(m, _NKB * _SP_U), out_dtype=F32,
                   dims=_MM_TB, acc_2d=(tm, _SP_U), res=res, res_blk=(tm, _SP_U), res_map=lambda j, i, k: (i, j), **kw)
    assert kind == "dbd"
    b_blk, b_map = xblk(lambda g, _, k: k, lambda g, _, k: g // _NKB, lambda g, _, k: g % _NKB)
    return _mm(a, b, grid=(2 * _NKB, 1, m // tm), a_blk=(tm, _SP_U), a_map=lambda g, _, k: (k, g % _NKB),
               b_blk=b_blk, b_map=b_map,
               o_blk=(None, None, _SP_U, _SP_X), o_map=lambda g, _, k: (g // _NKB, g % _NKB, 0, 0),
               out_shape=(2, _NKB, _SP_U, _SP_X), out_dtype=F32, dims=_MM_TA, acc_2d=(_SP_U, _SP_X), **kw)


def _block_diag(w):
    g, a, b = w.shape
    eye = jnp.eye(SSM_PACK, dtype=w.dtype)
    wp = w.reshape(g // SSM_PACK, SSM_PACK, a, b)
    return jnp.einsum("kgab,gh->kgahb", wp, eye).reshape(g // SSM_PACK, SSM_PACK * a, SSM_PACK * b)


def _block_diag_t(d, a, b):
    k = d.shape[0]
    eye = jnp.eye(SSM_PACK, dtype=d.dtype)
    dp = d.reshape(k, SSM_PACK, a, SSM_PACK, b)
    return jnp.einsum("kgahb,gh->kgab", dp, eye).reshape(k * SSM_PACK, a, b)


def _coords():
    return lax.axis_index("x"), lax.axis_index("y"), lax.axis_index("c")


def all_gather(tensors, name):
    n = len(tensors)
    any_spec = pl.BlockSpec(memory_space=pl.ANY)

    def body(*refs):
        ins, outs = refs[:n], refs[n:2 * n]
        send, recv, local = refs[2 * n:]
        x, y, c = _coords()
        me, sibling = (x, y, c), (x, y, 1 - c)
        chips = [(1 - x, y), (x, 1 - y), (1 - x, 1 - y)]

        def slot(p):
            return 4 * p[0] + 2 * p[1] + p[2]

        def copy(t, k, block, to, src=None):
            dst = outs[t].at[slot(block)]
            return pltpu.make_async_remote_copy(
                src_ref=dst if src is None else src, dst_ref=dst, send_sem=send.at[7 * t + k],
                recv_sem=recv.at[7 * t + k], device_id=to, device_id_type=pl.DeviceIdType.MESH)

        own, sent = [], []
        for t in range(n):
            mine = pltpu.make_async_copy(ins[t], outs[t].at[slot(me)], local.at[t])
            mine.start()
            own.append(mine)
            first = [copy(t, 0, me, sibling, src=ins[t])]
            first += [copy(t, 1 + j, me, (*chip, c), src=ins[t]) for j, chip in enumerate(chips)]
            for cp in first:
                cp.start()
            sent += first
        for t in range(n):
            for j, chip in enumerate(chips):
                copy(t, 1 + j, (*chip, c), me).wait_recv()
                passed = copy(t, 4 + j, (*chip, c), sibling)
                passed.start()
                sent.append(passed)
        for t in range(n):
            copy(t, 0, sibling, me).wait_recv()
            for j, chip in enumerate(chips):
                copy(t, 4 + j, (*chip, 1 - c), me).wait_recv()
        for cp in sent:
            cp.wait_send()
        for cp in own:
            cp.wait()

    return pl.pallas_call(
        body, in_specs=[any_spec] * n, out_specs=[any_spec] * n,
        out_shape=[jax.ShapeDtypeStruct((NDEV,) + a.shape, a.dtype) for a in tensors],
        scratch_shapes=[pltpu.SemaphoreType.DMA((7 * n,)), pltpu.SemaphoreType.DMA((7 * n,)),
                        pltpu.SemaphoreType.DMA((n,))],
        name=name,
    )(*tensors)


_HBM_SPEC = pl.BlockSpec(memory_space=pltpu.HBM)
_SEM_SPEC = pl.BlockSpec(memory_space=pltpu.SEMAPHORE)
_NPEER = NDEV - 1


def _split_copy_params():
    return pltpu.CompilerParams(has_side_effects=pltpu.SideEffectType.DATAFLOW_SIDE_EFFECTING)


def _me_and_peers():
    x, y, c = _coords()
    peers = []
    for rel in range(1, NDEV):
        p = (1 - x if rel & 4 else x, 1 - y if rel & 2 else y, 1 - c if rel & 1 else c)
        peers.append((p, 4 * p[0] + 2 * p[1] + p[2]))
    return 4 * x + 2 * y + c, peers


def _hbm(a):
    return pltpu.with_memory_space_constraint(a, pltpu.HBM)


def gather_start(bufs, name):
    n = len(bufs)

    def body(*refs):
        ins, outs = refs[:n], refs[n:]
        me, peers = _me_and_peers()
        for t in range(n):
            for k, (dev, _) in enumerate(peers):
                pltpu.make_async_remote_copy(
                    src_ref=ins[t].at[me], dst_ref=ins[t].at[me], send_sem=outs[3 * t].at[k],
                    recv_sem=outs[3 * t + 1].at[k], device_id=dev, device_id_type=pl.DeviceIdType.MESH).start()

    out_shape, out_specs = [], []
    for b in bufs:
        out_shape += [pltpu.SemaphoreType.DMA((_NPEER,)), pltpu.SemaphoreType.DMA((_NPEER,)), pltpu.HBM(b.shape, b.dtype)]
        out_specs += [_SEM_SPEC, _SEM_SPEC, _HBM_SPEC]
    res = pl.pallas_call(
        body, name=name, out_shape=tuple(out_shape), in_specs=[_HBM_SPEC] * n, out_specs=tuple(out_specs),
        input_output_aliases={t: 3 * t + 2 for t in range(n)}, compiler_params=_split_copy_params(),
    )(*[_hbm(b) for b in bufs])
    return [tuple(res[3 * t:3 * t + 3]) for t in range(n)]


def gather_wait(started, after, name):
    n = len(started)

    def body(*refs):
        bufs, sems = refs[:n], refs[n:3 * n]
        me, peers = _me_and_peers()
        for t in range(n):
            for k, (dev, slot) in enumerate(peers):
                cp = pltpu.make_async_remote_copy(
                    src_ref=bufs[t].at[me], dst_ref=bufs[t].at[slot], send_sem=sems[2 * t].at[k],
                    recv_sem=sems[2 * t + 1].at[k], device_id=dev, device_id_type=pl.DeviceIdType.MESH)
                cp.wait_recv()
                cp.wait_send()

    args = [s[2] for s in started] + [sem for s in started for sem in s[:2]] + [after]
    res = pl.pallas_call(
        body, name=name, out_shape=tuple(pltpu.HBM(s[2].shape, s[2].dtype) for s in started),
        in_specs=[_HBM_SPEC] * n + [_SEM_SPEC] * (2 * n) + [pl.BlockSpec(memory_space=pl.ANY)],
        out_specs=tuple([_HBM_SPEC] * n), input_output_aliases={t: t for t in range(n)},
        compiler_params=_split_copy_params(),
    )(*args)
    return list(res)


def scatter_start(srcs, name):
    n = len(srcs)
    lands = [lax.empty((_NPEER,) + s.shape[1:], s.dtype) for s in srcs]

    def body(*refs):
        ins, land_refs, outs = refs[:n], refs[n:2 * n], refs[2 * n:]
        _, peers = _me_and_peers()
        for t in range(n):
            for k, (dev, slot) in enumerate(peers):
                pltpu.make_async_remote_copy(
                    src_ref=ins[t].at[slot], dst_ref=land_refs[t].at[k], send_sem=outs[4 * t].at[k],
                    recv_sem=outs[4 * t + 1].at[k], device_id=dev, device_id_type=pl.DeviceIdType.MESH).start()
        outs[4 * n][...] = jnp.zeros_like(outs[4 * n])

    out_shape, out_specs = [], []
    for s, land in zip(srcs, lands):
        out_shape += [pltpu.SemaphoreType.DMA((_NPEER,)), pltpu.SemaphoreType.DMA((_NPEER,)),
                      pltpu.HBM(s.shape, s.dtype), pltpu.HBM(land.shape, land.dtype)]
        out_specs += [_SEM_SPEC, _SEM_SPEC, _HBM_SPEC, _HBM_SPEC]
    out_shape.append(jax.ShapeDtypeStruct((8, LANES), F32))
    out_specs.append(pl.BlockSpec(memory_space=pltpu.VMEM))
    aliases = {t: 4 * t + 2 for t in range(n)}
    aliases.update({n + t: 4 * t + 3 for t in range(n)})
    res = pl.pallas_call(
        body, name=name, out_shape=tuple(out_shape), in_specs=[_HBM_SPEC] * (2 * n), out_specs=tuple(out_specs),
        input_output_aliases=aliases, compiler_params=_split_copy_params(),
    )(*[_hbm(s) for s in srcs], *[_hbm(land) for land in lands])
    return [tuple(res[4 * t:4 * t + 4]) for t in range(n)], res[4 * n]


def scatter_wait(started, after, name):
    n = len(started)

    def body(*refs):
        srcs, land_refs, sems = refs[:n], refs[n:2 * n], refs[2 * n:4 * n]
        _, peers = _me_and_peers()
        for t in range(n):
            for k, (dev, slot) in enumerate(peers):
                cp = pltpu.make_async_remote_copy(
                    src_ref=srcs[t].at[slot], dst_ref=land_refs[t].at[k], send_sem=sems[2 * t].at[k],
                    recv_sem=sems[2 * t + 1].at[k], device_id=dev, device_id_type=pl.DeviceIdType.MESH)
                cp.wait_recv()
                cp.wait_send()

    args = [s[2] for s in started] + [s[3] for s in started] + [sem for s in started for sem in s[:2]] + [after]
    res = pl.pallas_call(
        body, name=name,
        out_shape=tuple([pltpu.HBM(s[2].shape, s[2].dtype) for s in started]
                        + [pltpu.HBM(s[3].shape, s[3].dtype) for s in started]),
        in_specs=[_HBM_SPEC] * (2 * n) + [_SEM_SPEC] * (2 * n) + [pl.BlockSpec(memory_space=pl.ANY)],
        out_specs=tuple([_HBM_SPEC] * (2 * n)), input_output_aliases={t: t for t in range(2 * n)},
        compiler_params=_split_copy_params(),
    )(*args)
    return [(res[t], res[n + t]) for t in range(n)]


def all_to_all(groups, name):
    flat = [a for grp in groups for a in grp]
    n = len(flat)
    where = [(gi, li) for gi, grp in enumerate(groups) for li in range(len(grp))]
    any_spec = pl.BlockSpec(memory_space=pl.ANY)

    def body(*refs):
        ins, outs = refs[:n], refs[n:n + len(groups)]
        send, recv, local = refs[n + len(groups):]
        x, y, c = _coords()
        me = 4 * x + 2 * y + c
        waits = []
        for e in range(n):
            gi, li = where[e]
            mine = pltpu.make_async_copy(ins[e].at[me], outs[gi].at[me, li], local.at[e])
            mine.start()
            waits.append(mine)
        sent, landing = [], []
        for rel in range(1, NDEV):
            px = 1 - x if rel & 4 else x
            py = 1 - y if rel & 2 else y
            pc = 1 - c if rel & 1 else c
            peer = 4 * px + 2 * py + pc
            for e in range(n):
                gi, li = where[e]

                def copy(dst_slot, e=e, gi=gi, li=li, rel=rel, peer=peer, to=(px, py, pc)):
                    return pltpu.make_async_remote_copy(
                        src_ref=ins[e].at[peer], dst_ref=outs[gi].at[dst_slot, li], send_sem=send.at[7 * e + rel - 1],
                        recv_sem=recv.at[7 * e + rel - 1], device_id=to, device_id_type=pl.DeviceIdType.MESH)

                cp = copy(me)
                cp.start()
                sent.append(cp)
                landing.append(copy(peer))
        for cp in landing:
            cp.wait_recv()
        for cp in sent:
            cp.wait_send()
        for cp in waits:
            cp.wait()

    return pl.pallas_call(
        body, in_specs=[any_spec] * n, out_specs=[any_spec] * len(groups),
        out_shape=[jax.ShapeDtypeStruct((NDEV, len(grp)) + grp[0].shape[1:], grp[0].dtype) for grp in groups],
        scratch_shapes=[pltpu.SemaphoreType.DMA((7 * n,)), pltpu.SemaphoreType.DMA((7 * n,)),
                        pltpu.SemaphoreType.DMA((n,))],
        name=name,
    )(*flat)


_PACK_QUANTUM = 8 * LANES


def _pack(parts, lead=0):
    out = []
    for p in parts:
        head = p.shape[:lead]
        f = p.astype(F32).reshape(head + (-1,))
        pad = (-f.shape[-1]) % _PACK_QUANTUM
        if pad:
            f = jnp.concatenate([f, jnp.zeros(head + (pad,), F32)], axis=-1)
        out.append(f.reshape(head + (-1, LANES)))
    return jnp.concatenate(out, axis=lead)


def _unpack(buf, shapes):
    head = buf.shape[:-2]
    out, r = [], 0
    for s in shapes:
        n = 1
        for v in s:
            n *= v
        nr = -(-n // _PACK_QUANTUM) * 8
        flat = buf[..., r:r + nr, :].reshape(head + (nr * LANES,))[..., :n]
        out.append(flat.reshape(head + tuple(s)))
        r += nr
    return out


BIG = ("sb_w_qkv", "sb_w_o", "sg_w_in", "sg_w_o", "ssm_w_in", "ssm_w_glu", "ffn_w_up", "ffn_w_down")
SMALL_SHARDED = ("norm_g", "ssm_d", "ffn_conv_w")
REPLICATED = ("final_norm_g", "sg_norm_g", "sg_w_s", "sg_b", "ssm_lam_re", "ssm_lam_im", "ssm_log_dt",
              "ssm_b_re", "ssm_b_im", "ssm_c_re", "ssm_c_im", "ffn_conv_b")
WEIGHTS = ("norm_g", "final_norm_g", "sb_w_qkv", "sb_w_o", "sg_w_in", "sg_norm_g", "sg_w_s", "sg_b", "sg_w_o",
           "ssm_w_in", "ssm_lam_re", "ssm_lam_im", "ssm_log_dt", "ssm_b_re", "ssm_b_im", "ssm_c_re", "ssm_c_im",
           "ssm_d", "ssm_w_glu", "ffn_w_up", "ffn_conv_w", "ffn_conv_b", "ffn_w_down")


def _step(x, loss_target, w, m, v):
    t, d = x.shape[1], x.shape[2]
    depth = w["norm_g"].shape[0]
    x0 = x.reshape(t, d)
    tgt = loss_target.reshape(t, d)

    mx, my, mc = _coords()
    me = (4 * mx + 2 * my + mc).astype(jnp.int32).reshape(1)
    shard_pack = _pack([w[k] for k in SMALL_SHARDED])
    gathered_small, = all_gather([shard_pack], name="gather_small_weights")
    mixer_weights = (("sb_w_qkv", "sb_w_o"), ("sg_w_in", "sg_w_o"), ("ssm_w_in", "ssm_w_glu"))
    order = []
    for i in range(depth):
        order += [(k, i // 3) for k in mixer_weights[i % 3]] + [("ffn_w_up", i), ("ffn_w_down", i)]
    pending = dict(zip(order, gather_start([cast_into_slot(w[k], l, me) for k, l in order], "gather_weights_start")))
    wg = {}

    def weights(keys, after):
        missing = [key for key in keys if key not in wg]
        if missing:
            for key, buf in zip(missing, gather_wait([pending[key] for key in missing], after, "gather_weights_wait")):
                wg[key] = buf[:, None]
        return [wg[key] for key in keys]

    ng, sd, cw = _unpack(gathered_small, [w[k].shape for k in SMALL_SHARDED])
    norm_full = jnp.transpose(ng, (1, 2, 0, 3)).reshape(depth, 2, d)
    ssm_d_full = jnp.transpose(sd, (1, 0, 2)).reshape(1, d)
    nc = cw.shape[-1]
    conv_b3 = w["ffn_conv_b"].reshape(depth, NDEV, nc)
    p3 = [jnp.concatenate([cw[:, l], conv_b3[l][:, None, :], jnp.zeros((NDEV, 8 - CONV_K - 1, nc), F32)], axis=1)
          for l in range(depth)]

    g_, p_, h_ = SSM_GROUPS, SSM_STATE, SSM_GROUP
    lam_re, lam_im = w["ssm_lam_re"][0], w["ssm_lam_im"][0]
    log_dt = w["ssm_log_dt"][0].reshape(g_, 1)
    b_re, b_im = w["ssm_b_re"][0].reshape(g_ * p_, h_), w["ssm_b_im"][0].reshape(g_ * p_, h_)
    ar, ai, cre, cim = _single(_disc1, [lam_re, lam_im, log_dt], [(g_, p_)] * 4, "s5_disc1")
    cre_c, cim_c = cre.reshape(g_ * p_, 1), cim.reshape(g_ * p_, 1)
    bbr, bbi = _single(_disc2, [cre_c, cim_c, b_re, b_im], [(g_ * p_, h_)] * 2, "s5_disc2")
    per_group_t = lambda a, r, c: jnp.swapaxes(a.reshape(g_, r, c), 1, 2)
    bd = jnp.stack([_block_diag(per_group_t(bbr, p_, h_)), _block_diag(per_group_t(bbi, p_, h_))])
    cd = jnp.stack([_block_diag(per_group_t(w["ssm_c_re"][0], h_, p_)),
                    -_block_diag(per_group_t(w["ssm_c_im"][0], h_, p_))])
    a2 = jnp.stack([ar.reshape(g_ * p_), ai.reshape(g_ * p_)])

    sg_gain = w["sg_norm_g"]
    sg_ws = w["sg_w_s"][0]
    sg_bfull = jnp.broadcast_to(w["sg_b"][0][:, :, None], sg_ws.shape)

    acts = []
    xc = x0
    for i in range(depth):
        mixer, j = i % 3, i // 3
        st = {"x": xc}
        g0 = norm_full[i, 0][None]
        xn = rms_fwd(xc, g0, "rms_fwd")
        st["xn"] = xn
        w_in, w_out = weights([(k, j) for k in mixer_weights[mixer]], xn)
        if mixer == 0:
            qkv = mm_cs_fwd(xn, w_in, 0, BF16, "qkv_fwd")
            o, ltot = sb_attn_fwd(qkv)
            x1 = mm_rs_fwd(o, w_out, 0, xc, F32, "attn_out_fwd")
            st.update(qkv=qkv, o=o, ltot=ltot)
        elif mixer == 1:
            hin = mm_cs_fwd(xn, w_in, 0, BF16, "sg_in_fwd")
            p = sgu_fwd(hin, sg_gain, sg_ws, sg_bfull)
            x1 = mm_rs_fwd(p, w_out, 0, xc, F32, "sg_out_fwd")
            st.update(hin=hin, p=p)
        else:
            u = mm_rs_fwd(xn, w_in, 0, None, F32, "ssm_in_fwd")
            bu2 = mm_s5("bu", u, bd, t, "s5_bu")
            x2 = s5_scan_fwd(bu2, a2)
            yc = mm_s5("yc", x2, cd, t, "s5_yc")
            yg = s5_post_fwd(yc, u, ssm_d_full)
            hg = mm_cs_fwd(yg, w_out, 0, BF16, "ssm_glu_fwd")
            x1 = glu_fwd(hg, xc)
            st.update(u=u, x2=x2, yc=yc, yg=yg, hg=hg)
        g1 = norm_full[i, 1][None]
        xn2 = rms_fwd(x1, g1, "rms_fwd")
        w_up, w_down = weights([("ffn_w_up", i), ("ffn_w_down", i)], xn2)
        h3 = mm_cs_fwd(xn2, w_up, 0, BF16, "ffn_up_fwd")
        gated = ffn_gate_fwd(h3, p3[i])
        xc = mm_down_fwd(gated, w_down, 0, x1, "ffn_down_fwd")
        st.update(x1=x1, xn2=xn2, h3=h3, gated=gated, g0=g0, g1=g1)
        acts.append(st)

    dx, loss_lanes, d_final_g = loss_head(xc, w["final_norm_g"][None], tgt)
    loss = lax.psum(loss_lanes[0, 0], MESH_AXES)

    scattering = {}
    d_norm = [[None, None] for _ in range(depth)]
    d_p3 = [None] * depth
    rep = {}
    d_ssm_d = None
    token = None

    def scatter(grads_by_key):
        keys = list(grads_by_key)
        started, tok = scatter_start([grads_by_key[key] for key in keys], "scatter_grads_start")
        scattering[tuple(keys)] = started
        return tok

    for i in reversed(range(depth)):
        mixer, j = i % 3, i // 3
        st = acts[i]
        k_in, k_out = [(k, j) for k in mixer_weights[mixer]]
        w_in, w_out, w_up, w_down = weights([k_in, k_out, ("ffn_w_up", i), ("ffn_w_down", i)], None)
        dgated = mm_down_da(dx, w_down, 0, "ffn_down_da", dep=token)
        g_down = mm_down_dw(st["gated"], dx, "ffn_down_dw")
        dy_a, dy_g, dp_a, dp_g = ffn_gate_bwd(st["h3"], dgated, p3[i])
        d_p3[i] = jnp.concatenate([dp_a, dp_g], axis=0)
        dh3 = ffn_conv_t(dy_a, dy_g, p3[i])
        dxn2 = mm_cs_da(dh3, w_up, 0, t, "ffn_up_da")
        g_up = mm_cs_dw(st["xn2"], dh3, nc, "ffn_up_dw")
        dx1, d_norm[i][1] = rms_bwd(st["x1"], st["g1"], dxn2, dx, "rms_bwd")
        token = scatter({("ffn_w_down", i): g_down, ("ffn_w_up", i): g_up})
        if mixer == 0:
            do = mm_rs_da(dx1, w_out, 0, BF16, "attn_out_da", dep=token)
            g_out = mm_rs_dw(st["o"], dx1, "attn_out_dw")
            d3 = sb_attn_bwd(st["qkv"], st["ltot"], do)
            dxn = mm_qkv_da(d3, w_in, 0, "qkv_da")
            g_in = mm_qkv_dw(st["xn"], d3, w_in.shape[3], "qkv_dw")
        elif mixer == 1:
            dp = mm_rs_da(dx1, w_out, 0, BF16, "sg_out_da", dep=token)
            g_out = mm_rs_dw(st["p"], dx1, "sg_out_dw")
            dhin, d_ws, d_bfull, d_gain = sgu_bwd(st["hin"], dp, sg_gain, sg_ws, sg_bfull)
            rep.update(sg_w_s=d_ws[None], sg_b=d_bfull[None, :, :, 0], sg_norm_g=d_gain)
            dxn = mm_cs_da(dhin, w_in, 0, t, "sg_in_da")
            g_in = mm_cs_dw(st["xn"], dhin, w_in.shape[3], "sg_in_dw")
        else:
            dhg = glu_bwd(st["hg"], dx1, token)
            dyg = mm_cs_da(dhg, w_out, 0, t, "ssm_glu_da")
            g_out = mm_cs_dw(st["yg"], dhg, w_out.shape[3], "ssm_glu_dw")
            dyc, du_skip, d_ssm_d = s5_post_bwd(st["yc"], st["u"], ssm_d_full, dyg)
            dx2 = mm_s5("dx", dyc, cd, t, "s5_dx")
            dcd = mm_s5("dcd", st["x2"], dyc, t, "s5_dcd")
            g2, da2 = s5_scan_bwd(dx2, st["x2"], a2)
            du = mm_s5("du", g2, bd, t, "s5_du", res=du_skip)
            dbd = mm_s5("dbd", st["u"], g2, t, "s5_dbd")
            from_bd = lambda blk: jnp.swapaxes(_block_diag_t(blk, h_, p_), 1, 2).reshape(g_ * p_, h_)

            def disc2_bwd(c1, c2, b1, b2, t1, t2):
                return jax.vjp(_disc2, c1, c2, b1, b2)[1]((t1, t2))

            d_cre, d_cim, d_b_re, d_b_im = _single(
                disc2_bwd, [cre_c, cim_c, b_re, b_im, from_bd(dbd[0]), from_bd(dbd[1])],
                [(g_ * p_, 1)] * 2 + [(g_ * p_, h_)] * 2, "s5_disc2_bwd")

            def disc1_bwd(l1, l2, ld, t1, t2, t3, t4):
                return jax.vjp(_disc1, l1, l2, ld)[1]((t1, t2, t3, t4))

            d_lam_re, d_lam_im, d_log_dt = _single(
                disc1_bwd, [lam_re, lam_im, log_dt, da2[0].reshape(g_, p_), da2[1].reshape(g_, p_),
                            d_cre.reshape(g_, p_), d_cim.reshape(g_, p_)],
                [(g_, p_), (g_, p_), (g_, 1)], "s5_disc1_bwd")
            from_cd = lambda blk: jnp.swapaxes(_block_diag_t(blk, p_, h_), 1, 2)
            rep.update(ssm_lam_re=d_lam_re[None], ssm_lam_im=d_lam_im[None], ssm_log_dt=d_log_dt.reshape(1, g_),
                       ssm_b_re=d_b_re.reshape(1, g_, p_, h_), ssm_b_im=d_b_im.reshape(1, g_, p_, h_),
                       ssm_c_re=from_cd(dcd[0])[None], ssm_c_im=-from_cd(dcd[1])[None])
            dxn = mm_rs_da(du, w_in, 0, F32, "ssm_in_da")
            g_in = mm_rs_dw(st["xn"], du, "ssm_in_dw")
        dx, d_norm[i][0] = rms_bwd(st["x"], st["g0"], dxn, dx1, "rms_bwd")
        token = scatter({k_in: g_in, k_out: g_out})

    rep["final_norm_g"] = d_final_g.reshape(d)
    rep["ffn_conv_b"] = jnp.stack([d_p3[l][:, CONV_K, :].reshape(NDEV * nc) for l in range(depth)])

    d_norm_full = jnp.stack([jnp.concatenate(pair, axis=0) for pair in d_norm])
    d_norm_pieces = jnp.transpose(d_norm_full.reshape(depth, 2, NDEV, d // NDEV), (2, 0, 1, 3))
    d_ssm_d_pieces = jnp.transpose(d_ssm_d.reshape(1, NDEV, d // NDEV), (1, 0, 2))
    d_conv_w_pieces = jnp.stack([d_p3[l][:, :CONV_K, :] for l in range(depth)], axis=1)
    small_pieces = _pack([d_norm_pieces, d_ssm_d_pieces, d_conv_w_pieces], lead=1)
    small_received, = all_to_all([[small_pieces]], name="scatter_small_grads")
    rep_parts, = all_gather([_pack([rep[k] for k in REPLICATED])], name="gather_small_grads")
    own, landed = {}, {}
    for keys, started in scattering.items():
        for key, (src, land) in zip(keys, scatter_wait(started, dx, "scatter_grads_wait")):
            own[key], landed[key] = src, land

    grads, deltas, new_m, new_v = {}, {}, {}, {}
    for k in BIG:
        layers = range(w[k].shape[0])
        grads[k], deltas[k], new_m[k], new_v[k] = adamw_layers(
            w[k], m[k], v[k], [landed[(k, l)] for l in layers], [own[(k, l)] for l in layers], me, "adamw")
    for names, parts in ((SMALL_SHARDED, small_received[:, 0]), (REPLICATED, rep_parts)):
        res = adamw(_pack([w[k] for k in names]), _pack([m[k] for k in names]), _pack([v[k] for k in names]), parts,
                    "adamw_small")
        for tree, buf in zip((grads, deltas, new_m, new_v), res):
            for k, val in zip(names, _unpack(buf, [w[k].shape for k in names])):
                tree[k] = val
    grad_x = dx.reshape(x.shape)
    return (loss, grad_x, *[grads[k] for k in WEIGHTS], *[deltas[k] for k in WEIGHTS],
            *[new_m[k] for k in WEIGHTS], *[new_v[k] for k in WEIGHTS])


def kernel(x, norm_g, final_norm_g, sb_w_qkv, sb_w_o, sg_w_in, sg_norm_g, sg_w_s, sg_b, sg_w_o, ssm_w_in, ssm_lam_re, ssm_lam_im, ssm_log_dt, ssm_b_re, ssm_b_im, ssm_c_re, ssm_c_im, ssm_d, ssm_w_glu, ffn_w_up, ffn_conv_w, ffn_conv_b, ffn_w_down, loss_target, m_norm_g, m_final_norm_g, m_sb_w_qkv, m_sb_w_o, m_sg_w_in, m_sg_norm_g, m_sg_w_s, m_sg_b, m_sg_w_o, m_ssm_w_in, m_ssm_lam_re, m_ssm_lam_im, m_ssm_log_dt, m_ssm_b_re, m_ssm_b_im, m_ssm_c_re, m_ssm_c_im, m_ssm_d, m_ssm_w_glu, m_ffn_w_up, m_ffn_conv_w, m_ffn_conv_b, m_ffn_w_down, v_norm_g, v_final_norm_g, v_sb_w_qkv, v_sb_w_o, v_sg_w_in, v_sg_norm_g, v_sg_w_s, v_sg_b, v_sg_w_o, v_ssm_w_in, v_ssm_lam_re, v_ssm_lam_im, v_ssm_log_dt, v_ssm_b_re, v_ssm_b_im, v_ssm_c_re, v_ssm_c_im, v_ssm_d, v_ssm_w_glu, v_ffn_w_up, v_ffn_conv_w, v_ffn_conv_b, v_ffn_w_down):
    w = dict(zip(WEIGHTS, (norm_g, final_norm_g, sb_w_qkv, sb_w_o, sg_w_in, sg_norm_g, sg_w_s, sg_b, sg_w_o, ssm_w_in,
                           ssm_lam_re, ssm_lam_im, ssm_log_dt, ssm_b_re, ssm_b_im, ssm_c_re, ssm_c_im, ssm_d, ssm_w_glu,
                           ffn_w_up, ffn_conv_w, ffn_conv_b, ffn_w_down)))
    m = dict(zip(WEIGHTS, (m_norm_g, m_final_norm_g, m_sb_w_qkv, m_sb_w_o, m_sg_w_in, m_sg_norm_g, m_sg_w_s, m_sg_b,
                           m_sg_w_o, m_ssm_w_in, m_ssm_lam_re, m_ssm_lam_im, m_ssm_log_dt, m_ssm_b_re, m_ssm_b_im,
                           m_ssm_c_re, m_ssm_c_im, m_ssm_d, m_ssm_w_glu, m_ffn_w_up, m_ffn_conv_w, m_ffn_conv_b,
                           m_ffn_w_down)))
    v = dict(zip(WEIGHTS, (v_norm_g, v_final_norm_g, v_sb_w_qkv, v_sb_w_o, v_sg_w_in, v_sg_norm_g, v_sg_w_s, v_sg_b,
                           v_sg_w_o, v_ssm_w_in, v_ssm_lam_re, v_ssm_lam_im, v_ssm_log_dt, v_ssm_b_re, v_ssm_b_im,
                           v_ssm_c_re, v_ssm_c_im, v_ssm_d, v_ssm_w_glu, v_ffn_w_up, v_ffn_conv_w, v_ffn_conv_b,
                           v_ffn_w_down)))
    return _step(x, loss_target, w, m, v)
```

```python
import functools

import jax
import jax.numpy as jnp
from jax import lax
from jax.experimental import pallas as pl
from jax.experimental.pallas import tpu as pltpu

F32, BF16 = jnp.float32, jnp.bfloat16
MESH_AXES = ("x", "y", "c")
NDEV = 8
EPS = 1e-6
HEAD_DIM = 64
LANES = 128
ATT_BQ, ATT_BK = 512, 256
CHUNK = 128
SG_GROUPS = 8
SSM_GROUPS, SSM_STATE, SSM_GROUP = 64, 64, 16
SSM_PACK = 8
S5_PASSES = 1
CONV_K = 3
HALO = 16
SCAN_COLS = 256
ADAM_LR, ADAM_B1, ADAM_B2, ADAM_EPS, ADAM_WD, ADAM_STEP = 0.001, 0.9, 0.999, 1e-08, 0.01, 10
VMEM_LIMIT = 56 * 1024 * 1024

_MM = (((1,), (0,)), ((), ()))
_MM_TB = (((1,), (1,)), ((), ()))
_MM_TA = (((0,), (0,)), ((), ()))


def _params(sem):
    return pltpu.CompilerParams(dimension_semantics=sem, vmem_limit_bytes=VMEM_LIMIT)


def _rows(total, cap, mult=16):
    best = None
    for d in range(mult, min(total, cap) + 1, mult):
        if total % d == 0:
            best = d
    return best if best is not None else total


def _dot(a, b, dims, passes):
    if passes == 1:
        return lax.dot_general(a.astype(BF16), b.astype(BF16), dims, preferred_element_type=F32)
    a = a.astype(F32)
    b = b.astype(F32)
    ah = a.astype(BF16)
    bh = b.astype(BF16)
    al = (a - ah.astype(F32)).astype(BF16)
    bl = (b - bh.astype(F32)).astype(BF16)
    out = lax.dot_general(ah, bh, dims, preferred_element_type=F32)
    out = out + lax.dot_general(al, bh, dims, preferred_element_type=F32)
    return out + lax.dot_general(ah, bl, dims, preferred_element_type=F32)


def _mm(a, b, *, grid, a_blk, a_map, b_blk, b_map, o_blk, o_map, out_shape, out_dtype, name,
        dims=_MM, passes=1, res=None, res_blk=None, res_map=None, b_2d=None, acc_2d=None, dep=None):
    nk = grid[2]
    has_res = res is not None
    a_maps = list(a_map) if isinstance(a_map, (list, tuple)) else [a_map]
    b_maps = list(b_map) if isinstance(b_map, (list, tuple)) else [b_map]
    na, nb = len(a_maps), len(b_maps)
    n_in = na + nb + has_res + (dep is not None)

    def body(*refs):
        o_ref = refs[n_in]
        r_ref = refs[na + nb] if has_res else None
        av = refs[0][...] if na == 1 else jnp.concatenate([r[...] for r in refs[:na]], axis=-1)
        bv = refs[na][...] if nb == 1 else jnp.concatenate([r[...] for r in refs[na:na + nb]], axis=-1)
        if b_2d is not None:
            bv = bv.reshape(b_2d)
        part = _dot(av, bv, dims, passes)

        def finish(total):
            if has_res:
                total = total + r_ref[...].astype(F32)
            o_ref[...] = total.reshape(o_ref.shape).astype(o_ref.dtype)

        if nk == 1:
            finish(part)
        else:
            acc_ref = refs[-1]
            k = pl.program_id(2)

            @pl.when(k == 0)
            def _():
                acc_ref[...] = part

            @pl.when(k > 0)
            def _():
                acc_ref[...] += part

            @pl.when(k == nk - 1)
            def _():
                finish(acc_ref[...])

    in_specs = [pl.BlockSpec(a_blk, f) for f in a_maps] + [pl.BlockSpec(b_blk, f) for f in b_maps]
    args = [a] * na + [b] * nb
    if has_res:
        in_specs.append(pl.BlockSpec(res_blk, res_map))
        args.append(res)
    if dep is not None:
        in_specs.append(pl.BlockSpec(memory_space=pl.ANY))
        args.append(dep)
    scratch = [pltpu.VMEM(acc_2d, F32)] if nk > 1 else []
    return pl.pallas_call(
        body, grid=grid, in_specs=in_specs, out_specs=pl.BlockSpec(o_blk, o_map),
        out_shape=jax.ShapeDtypeStruct(out_shape, out_dtype), scratch_shapes=scratch, name=name,
        compiler_params=_params(("parallel", "parallel", "arbitrary")),
    )(*args)


def _cs_act_spec(ns, tm, row_of, col_of):
    if ns % LANES == 0:
        return (tm, ns), lambda *g: (row_of(*g), col_of(*g))
    return (None, tm, ns), lambda *g: (col_of(*g), row_of(*g), 0)


def mm_cs_fwd(a, w4, l, out_dtype, name, tm=2048):
    m, k = a.shape
    tm = min(tm, m)
    ns = w4.shape[3]
    o_blk, o_map = _cs_act_spec(ns, tm, lambda j, i, kk: i, lambda j, i, kk: j)
    out_shape = (m, NDEV * ns) if ns % LANES == 0 else (NDEV, m, ns)
    return _mm(a, w4, grid=(NDEV, m // tm, 1), a_blk=(tm, k), a_map=lambda j, i, kk: (i, 0),
               b_blk=(None, None, k, ns), b_map=lambda j, i, kk: (j, l, 0, 0),
               o_blk=o_blk, o_map=o_map, out_shape=out_shape, out_dtype=out_dtype, name=name)


def mm_cs_da(dc, w4, l, m, name, tm=1024):
    k, ns = w4.shape[2], w4.shape[3]
    tm = min(tm, m)
    a_blk, a_map = _cs_act_spec(ns, tm, lambda i, _, j: i, lambda i, _, j: j)
    return _mm(dc, w4, grid=(m // tm, 1, NDEV), a_blk=a_blk, a_map=a_map,
               b_blk=(None, None, k, ns), b_map=lambda i, _, j: (j, l, 0, 0),
               o_blk=(tm, k), o_map=lambda i, _, j: (i, 0), out_shape=(m, k), out_dtype=F32,
               dims=_MM_TB, acc_2d=(tm, k), name=name)


def mm_cs_dw(a, dc, ns, name, tk=2048):
    m, k = a.shape
    tk = min(tk, m)
    b_blk, b_map = _cs_act_spec(ns, tk, lambda j, _, kk: kk, lambda j, _, kk: j)
    return _mm(a, dc, grid=(NDEV, 1, m // tk), a_blk=(tk, k), a_map=lambda j, _, kk: (kk, 0),
               b_blk=b_blk, b_map=b_map, o_blk=(None, k, ns), o_map=lambda j, _, kk: (j, 0, 0),
               out_shape=(NDEV, k, ns), out_dtype=BF16, dims=_MM_TA, acc_2d=(k, ns), name=name)


def mm_rs_fwd(a, w4, l, res, out_dtype, name, tm=1024):
    m, k = a.shape
    tm = min(tm, m)
    ks, n = w4.shape[2], w4.shape[3]
    return _mm(a, w4, grid=(m // tm, 1, 1), a_blk=(tm, k), a_map=lambda i, _, kk: (i, 0),
               b_blk=(NDEV, None, ks, n), b_map=lambda i, _, kk: (0, l, 0, 0), b_2d=(k, n),
               o_blk=(tm, n), o_map=lambda i, _, kk: (i, 0), out_shape=(m, n), out_dtype=out_dtype,
               res=res, res_blk=(tm, n), res_map=lambda i, _, kk: (i, 0), name=name)


def mm_rs_da(dc, w4, l, out_dtype, name, tm=1024, dep=None):
    m, n = dc.shape
    tm = min(tm, m)
    ks = w4.shape[2]
    k = NDEV * ks
    return _mm(dc, w4, grid=(m // tm, 1, 1), a_blk=(tm, n), a_map=lambda i, _, kk: (i, 0),
               b_blk=(NDEV, None, ks, n), b_map=lambda i, _, kk: (0, l, 0, 0), b_2d=(k, n),
               o_blk=(tm, k), o_map=lambda i, _, kk: (i, 0), out_shape=(m, k), out_dtype=out_dtype,
               dims=_MM_TB, name=name, dep=dep)


def mm_rs_dw(a, dc, name, tk=1024):
    m, k = a.shape
    tk = min(tk, m)
    n = dc.shape[1]
    ks = k // NDEV
    return _mm(a, dc, grid=(1, 1, m // tk), a_blk=(tk, k), a_map=lambda _, __, kk: (kk, 0),
               b_blk=(tk, n), b_map=lambda _, __, kk: (kk, 0),
               o_blk=(NDEV, ks, n), o_map=lambda _, __, kk: (0, 0, 0), out_shape=(NDEV, ks, n),
               out_dtype=BF16, dims=_MM_TA, acc_2d=(k, n), name=name)


def mm_down_fwd(a3, w4, l, res, name, tm=1024):
    nj, m, kc = a3.shape
    tm = min(tm, m)
    ks, n = w4.shape[2], w4.shape[3]
    return _mm(a3, w4, grid=(m // tm, 1, nj), a_blk=(None, tm, kc), a_map=lambda i, _, j: (j, i, 0),
               b_blk=(2, None, ks, n), b_map=lambda i, _, j: (j, l, 0, 0), b_2d=(kc, n),
               o_blk=(tm, n), o_map=lambda i, _, j: (i, 0), out_shape=(m, n), out_dtype=F32,
               res=res, res_blk=(tm, n), res_map=lambda i, _, j: (i, 0), acc_2d=(tm, n), name=name)


def mm_down_da(dc, w4, l, name, tm=2048, dep=None):
    m, n = dc.shape
    tm = min(tm, m)
    ks = w4.shape[2]
    kc = 2 * ks
    nj = NDEV // 2
    return _mm(dc, w4, grid=(nj, m // tm, 1), a_blk=(tm, n), a_map=lambda j, i, _: (i, 0),
               b_blk=(2, None, ks, n), b_map=lambda j, i, _: (j, l, 0, 0), b_2d=(kc, n),
               o_blk=(None, tm, kc), o_map=lambda j, i, _: (j, i, 0), out_shape=(nj, m, kc),
               out_dtype=BF16, dims=_MM_TB, name=name, dep=dep)


def mm_down_dw(a3, dc, name, tk=2048):
    nj, m, kc = a3.shape
    tk = min(tk, m)
    n = dc.shape[1]
    return _mm(a3, dc, grid=(nj, 1, m // tk), a_blk=(None, tk, kc), a_map=lambda j, _, kk: (j, kk, 0),
               b_blk=(tk, n), b_map=lambda j, _, kk: (kk, 0),
               o_blk=(2, kc // 2, n), o_map=lambda j, _, kk: (j, 0, 0), out_shape=(NDEV, kc // 2, n),
               out_dtype=BF16, dims=_MM_TA, acc_2d=(kc, n), name=name)


def _qkv_group_maps(d, ns, row_of, piece_of):
    per_arr, per_piece = d // LANES, ns // LANES

    def group_map(q):
        def f(*g):
            grp = piece_of(*g) * per_piece + q
            return grp // per_arr, row_of(*g), grp % per_arr
        return f

    return [group_map(q) for q in range(per_piece)]


def mm_qkv_da(d3, w4, l, name, tm=1024, dep=None):
    _, m, d = d3.shape
    tm = min(tm, m)
    k, ns = w4.shape[2], w4.shape[3]
    return _mm(d3, w4, grid=(m // tm, 1, NDEV),
               a_blk=(None, tm, LANES), a_map=_qkv_group_maps(d, ns, lambda i, _, j: i, lambda i, _, j: j),
               b_blk=(None, None, k, ns), b_map=lambda i, _, j: (j, l, 0, 0),
               o_blk=(tm, k), o_map=lambda i, _, j: (i, 0), out_shape=(m, k), out_dtype=F32,
               dims=_MM_TB, acc_2d=(tm, k), name=name, dep=dep)


def mm_qkv_dw(a, d3, ns, name, tk=2048):
    m, k = a.shape
    tk = min(tk, m)
    d = d3.shape[2]
    return _mm(a, d3, grid=(NDEV, 1, m // tk), a_blk=(tk, k), a_map=lambda j, _, kk: (kk, 0),
               b_blk=(None, tk, LANES), b_map=_qkv_group_maps(d, ns, lambda j, _, kk: kk, lambda j, _, kk: j),
               o_blk=(None, k, ns), o_map=lambda j, _, kk: (j, 0, 0),
               out_shape=(NDEV, k, ns), out_dtype=BF16, dims=_MM_TA, acc_2d=(k, ns), name=name)


def _rowwise(fn, ins, outs, *, tr, name, acc_outs=()):
    rows = next(a.shape[0] if kind == "row" else a.shape[1] for a, kind in ins if kind in ("row", "row3"))
    n_in, n_out = len(ins), len(outs)
    n_read = sum(kind != "dep" for _, kind in ins)

    def body(*refs):
        vals = fn(*[r[...] for r in refs[:n_read]])
        if not isinstance(vals, (tuple, list)):
            vals = (vals,)
        for ref, val in zip(refs[n_in:n_in + n_out], vals[:n_out]):
            ref[...] = val.astype(ref.dtype)
        i = pl.program_id(0)
        for ref, val in zip(refs[n_in + n_out:], vals[n_out:]):
            val = val.astype(ref.dtype)

            @pl.when(i == 0)
            def _(ref=ref, val=val):
                ref[...] = val

            @pl.when(i > 0)
            def _(ref=ref, val=val):
                ref[...] += val

    in_specs = []
    for a, kind in ins:
        if kind == "row":
            in_specs.append(pl.BlockSpec((tr, a.shape[1]), lambda i: (i, 0)))
        elif kind == "row3":
            in_specs.append(pl.BlockSpec((a.shape[0], tr, a.shape[2]), lambda i: (0, i, 0)))
        elif kind == "dep":
            in_specs.append(pl.BlockSpec(memory_space=pl.ANY))
        else:
            in_specs.append(pl.BlockSpec(a.shape, lambda i, nd=a.ndim: (0,) * nd))
    out_specs = [pl.BlockSpec((tr, c), lambda i: (i, 0)) for c, _ in outs]
    out_specs += [pl.BlockSpec(s, lambda i, nd=len(s): (0,) * nd) for s, _ in acc_outs]
    out_shape = [jax.ShapeDtypeStruct((rows, c), dt) for c, dt in outs]
    out_shape += [jax.ShapeDtypeStruct(s, dt) for s, dt in acc_outs]
    res = pl.pallas_call(
        body, grid=(rows // tr,), in_specs=in_specs, out_specs=out_specs, out_shape=out_shape, name=name,
        compiler_params=_params(("arbitrary",) if acc_outs else ("parallel",)),
    )(*[a for a, _ in ins])
    return res


def _rms(x, g):
    return x * lax.rsqrt(jnp.mean(x * x, axis=-1, keepdims=True) + EPS) * g


def cast_into_slot(w, l, me):
    _, r, c = w.shape
    tr = _rows(r, 512)

    def body(me_ref, w_ref, o_ref):
        o_ref[...] = w_ref[...].astype(o_ref.dtype)

    return pl.pallas_call(
        body,
        grid_spec=pltpu.PrefetchScalarGridSpec(
            num_scalar_prefetch=1, grid=(r // tr,),
            in_specs=[pl.BlockSpec((None, tr, c), lambda i, me_ref: (l, i, 0))],
            out_specs=pl.BlockSpec((None, tr, c), lambda i, me_ref: (me_ref[0], i, 0))),
        out_shape=jax.ShapeDtypeStruct((NDEV, r, c), BF16), name="cast_into_slot",
        compiler_params=_params(("parallel",)),
    )(me, w)


def rms_fwd(x, g, name):
    out, = _rowwise(_rms, [(x, "row"), (g, "full")], [(x.shape[1], BF16)], tr=256, name=name)
    return out


def rms_bwd(x, g, dy, dres, name):
    def fn(xv, gv, dyv, drv):
        _, vjp = jax.vjp(_rms, xv, gv)
        dx, dg = vjp(dyv.astype(F32))
        return drv + dx, dg

    d = x.shape[1]
    return _rowwise(fn, [(x, "row"), (g, "full"), (dy, "row"), (dres, "row")], [(d, F32)], tr=256, name=name,
                    acc_outs=[((1, d), F32)])


def loss_head(x, g, tgt):
    def f(xv, gv, tv):
        err = jnp.square(_rms(xv, gv) - tv)
        return 0.5 * jnp.sum(jnp.mean(err, axis=-1))

    def fn(xv, gv, tv):
        val, (dx, dg) = jax.value_and_grad(f, argnums=(0, 1))(xv, gv, tv)
        return dx, jnp.full((1, LANES), val, F32), dg

    d = x.shape[1]
    return _rowwise(fn, [(x, "row"), (g, "full"), (tgt, "row")], [(d, F32)], tr=256, name="loss_head",
                    acc_outs=[((1, LANES), F32), ((1, d), F32)])


def _glu(hg, x):
    half = hg.shape[1] // 2
    return x + hg[:, :half] * jax.nn.sigmoid(hg[:, half:])


def glu_fwd(hg, x):
    out, = _rowwise(lambda h, xv: _glu(h.astype(F32), xv), [(hg, "row"), (x, "row")], [(x.shape[1], F32)],
                    tr=256, name="glu_fwd")
    return out


def glu_bwd(hg, dx1, dep):
    def fn(h, d):
        _, vjp = jax.vjp(lambda hv: _glu(hv, jnp.zeros_like(d)), h.astype(F32))
        return vjp(d)[0]

    out, = _rowwise(fn, [(hg, "row"), (dx1, "row"), (dep, "dep")], [(hg.shape[1], BF16)], tr=256, name="glu_bwd")
    return out


def _s5_post(yc, u, d):
    return jax.nn.gelu(yc + d * u)


def s5_post_fwd(yc, u, d):
    out, = _rowwise(_s5_post, [(yc, "row"), (u, "row"), (d, "full")], [(yc.shape[1], BF16)], tr=256,
                    name="s5_post_fwd")
    return out


def s5_post_bwd(yc, u, d, dyg):
    def fn(ycv, uv, dv, g):
        _, vjp = jax.vjp(_s5_post, ycv, uv, dv)
        return vjp(g.astype(F32))

    dm = yc.shape[1]
    return _rowwise(fn, [(yc, "row"), (u, "row"), (d, "full"), (dyg, "row")], [(dm, F32), (dm, F32)], tr=256,
                    name="s5_post_bwd", acc_outs=[((1, dm), F32)])


def _adam_update(wv, mv, vv, g):
    m2 = ADAM_B1 * mv + (1.0 - ADAM_B1) * g
    v2 = ADAM_B2 * vv + (1.0 - ADAM_B2) * jnp.square(g)
    m_hat = m2 / (1.0 - ADAM_B1 ** ADAM_STEP)
    v_hat = v2 / (1.0 - ADAM_B2 ** ADAM_STEP)
    delta = -ADAM_LR * (m_hat / (jnp.sqrt(v_hat) + ADAM_EPS) + ADAM_WD * wv)
    return g, delta, m2, v2


def adamw(w, m, v, g_parts, name):
    def fn(wv, mv, vv, gp):
        g = gp[0].astype(F32)
        for p in range(1, gp.shape[0]):
            g = g + gp[p].astype(F32)
        return _adam_update(wv, mv, vv, g)

    c = w.shape[1]
    return _rowwise(fn, [(w, "row"), (m, "row"), (v, "row"), (g_parts, "row3")], [(c, F32)] * 4,
                    tr=_rows(w.shape[0], 256), name=name)


def adamw_layers(w, m, v, lands, owns, me, name):
    nl, r, c = w.shape
    tr = _rows(r, 256)

    def body(me_ref, w_ref, m_ref, v_ref, *rest):
        land_refs, own_refs, out_refs = rest[:nl], rest[nl:2 * nl], rest[2 * nl:]
        for l in range(nl):
            @pl.when(pl.program_id(0) == l)
            def _(l=l):
                g = own_refs[l][...].astype(F32)
                for p in range(NDEV - 1):
                    g = g + land_refs[l][p].astype(F32)
                for ref, val in zip(out_refs, _adam_update(w_ref[...], m_ref[...], v_ref[...], g)):
                    ref[...] = val

    def rows_of(l):
        return lambda li, i, me_ref: jnp.where(li == l, i, 0)

    wspec = pl.BlockSpec((None, tr, c), lambda li, i, me_ref: (li, i, 0))
    in_specs = [wspec] * 3
    in_specs += [pl.BlockSpec((NDEV - 1, tr, c), lambda li, i, me_ref, f=rows_of(l): (0, f(li, i, me_ref), 0))
                 for l in range(nl)]
    in_specs += [pl.BlockSpec((None, tr, c), lambda li, i, me_ref, f=rows_of(l): (me_ref[0], f(li, i, me_ref), 0))
                 for l in range(nl)]
    return pl.pallas_call(
        body,
        grid_spec=pltpu.PrefetchScalarGridSpec(
            num_scalar_prefetch=1, grid=(nl, r // tr), in_specs=in_specs, out_specs=[wspec] * 4),
        out_shape=[jax.ShapeDtypeStruct(w.shape, F32)] * 4, name=name, compiler_params=_params(("parallel", "parallel")),
    )(me, w, m, v, *lands, *owns)


def _conv_rows(cur, halo, p, first):
    r = cur.shape[0]
    ext = jnp.concatenate([jnp.where(first, 0.0, halo), cur], axis=0)
    s1 = pltpu.roll(ext, 1, 0)[HALO:]
    s2 = pltpu.roll(ext, 2, 0)[HALO:]
    return p[0:1] * s2 + p[1:2] * s1 + p[2:3] * cur + p[3:4], s1, s2


def ffn_gate_fwd(h3, p3, tr=256):
    _, t, c = h3.shape
    half = NDEV // 2

    def body(a_ref, ah_ref, g_ref, gh_ref, pa_ref, pg_ref, o_ref):
        first = pl.program_id(1) == 0
        ya, _, _ = _conv_rows(a_ref[...].astype(F32), ah_ref[...].astype(F32), pa_ref[...], first)
        yg, _, _ = _conv_rows(g_ref[...].astype(F32), gh_ref[...].astype(F32), pg_ref[...], first)
        o_ref[...] = (jax.nn.silu(yg) * ya).astype(o_ref.dtype)

    main = lambda off: pl.BlockSpec((None, tr, c), lambda j, i: (j + off, i, 0))
    halo = lambda off: pl.BlockSpec((None, HALO, c), lambda j, i: (j + off, jnp.maximum(i * (tr // HALO) - 1, 0), 0))
    par = lambda off: pl.BlockSpec((None, 8, c), lambda j, i: (j + off, 0, 0))
    return pl.pallas_call(
        body, grid=(half, t // tr),
        in_specs=[main(0), halo(0), main(half), halo(half), par(0), par(half)],
        out_specs=pl.BlockSpec((None, tr, c), lambda j, i: (j, i, 0)),
        out_shape=jax.ShapeDtypeStruct((half, t, c), BF16), name="ffn_gate_fwd",
        compiler_params=_params(("parallel", "parallel")),
    )(h3, h3, h3, h3, p3, p3)


def ffn_gate_bwd(h3, dgated3, p3, tr=256):
    _, t, c = h3.shape
    half = NDEV // 2

    def body(a_ref, ah_ref, g_ref, gh_ref, dg_ref, pa_ref, pg_ref, dya_ref, dyg_ref, dpa_ref, dpg_ref):
        i = pl.program_id(1)
        first = i == 0
        a = a_ref[...].astype(F32)
        g = g_ref[...].astype(F32)
        ya, a1, a2 = _conv_rows(a, ah_ref[...].astype(F32), pa_ref[...], first)
        yg, g1, g2 = _conv_rows(g, gh_ref[...].astype(F32), pg_ref[...], first)
        d = dg_ref[...].astype(F32)
        sig = jax.nn.sigmoid(yg)
        d_ya = (d * (yg * sig)).astype(dya_ref.dtype)
        d_yg = (d * ya * (sig * (1.0 + yg * (1.0 - sig)))).astype(dyg_ref.dtype)
        dya_ref[...] = d_ya
        dyg_ref[...] = d_yg
        for dy, cur, s1, s2, dp_ref in ((d_ya.astype(F32), a, a1, a2, dpa_ref), (d_yg.astype(F32), g, g1, g2, dpg_ref)):
            rows = [jnp.sum(dy * s2, axis=0, keepdims=True), jnp.sum(dy * s1, axis=0, keepdims=True),
                    jnp.sum(dy * cur, axis=0, keepdims=True), jnp.sum(dy, axis=0, keepdims=True)]
            dp = jnp.concatenate(rows + [jnp.zeros((4, c), F32)], axis=0)

            @pl.when(first)
            def _(dp_ref=dp_ref, dp=dp):
                dp_ref[...] = dp

            @pl.when(i > 0)
            def _(dp_ref=dp_ref, dp=dp):
                dp_ref[...] += dp

    main = lambda off: pl.BlockSpec((None, tr, c), lambda j, i: (j + off, i, 0))
    halo = lambda off: pl.BlockSpec((None, HALO, c), lambda j, i: (j + off, jnp.maximum(i * (tr // HALO) - 1, 0), 0))
    par = lambda off: pl.BlockSpec((None, 8, c), lambda j, i: (j + off, 0, 0))
    return pl.pallas_call(
        body, grid=(half, t // tr),
        in_specs=[main(0), halo(0), main(half), halo(half), main(0), par(0), par(half)],
        out_specs=[main(0), main(0), par(0), par(0)],
        out_shape=[jax.ShapeDtypeStruct((half, t, c), BF16)] * 2 + [jax.ShapeDtypeStruct((half, 8, c), F32)] * 2,
        name="ffn_gate_bwd", compiler_params=_params(("parallel", "arbitrary")),
    )(h3, h3, h3, h3, dgated3, p3, p3)


def ffn_conv_t(dy_a, dy_g, p3, tr=256):
    half, t, c = dy_a.shape
    nblk = t // tr

    def body(a_ref, ah_ref, g_ref, gh_ref, p_ref, o_ref):
        is_a = pl.program_id(0) < half
        last = pl.program_id(1) == nblk - 1
        cur = jnp.where(is_a, a_ref[...], g_ref[...]).astype(F32)
        nxt = jnp.where(is_a, ah_ref[...], gh_ref[...]).astype(F32)
        ext = jnp.concatenate([cur, jnp.where(last, 0.0, nxt)], axis=0)
        n = tr + HALO
        s1 = pltpu.roll(ext, n - 1, 0)[:tr]
        s2 = pltpu.roll(ext, n - 2, 0)[:tr]
        p = p_ref[...]
        o_ref[...] = (p[2:3] * cur + p[1:2] * s1 + p[0:1] * s2).astype(o_ref.dtype)

    main = pl.BlockSpec((None, tr, c), lambda j, i: (j % half, i, 0))
    halo = pl.BlockSpec((None, HALO, c), lambda j, i: (j % half, jnp.minimum((i + 1) * (tr // HALO), t // HALO - 1), 0))
    return pl.pallas_call(
        body, grid=(NDEV, nblk),
        in_specs=[main, halo, main, halo, pl.BlockSpec((None, 8, c), lambda j, i: (j, 0, 0))],
        out_specs=pl.BlockSpec((None, tr, c), lambda j, i: (j, i, 0)),
        out_shape=jax.ShapeDtypeStruct((NDEV, t, c), BF16), name="ffn_conv_t",
        compiler_params=_params(("parallel", "parallel")),
    )(dy_a, dy_a, dy_g, dy_g, p3)


def _att_consts(bq, bk):
    lane = lax.broadcasted_iota(jnp.int32, (1, LANES), 1)
    heads = (lane < HEAD_DIM, lane >= HEAD_DIM)
    rr = lax.broadcasted_iota(jnp.int32, (bq, bk), 0)
    cc = lax.broadcasted_iota(jnp.int32, (bq, bk), 1)
    kr = lax.broadcasted_iota(jnp.int32, (bk, bk), 0)
    kc = lax.broadcasted_iota(jnp.int32, (bk, bk), 1)
    return heads, rr, cc, kr, kc


def _split_dot(x, tri, parts):
    out = None
    for _ in range(parts):
        piece = x.astype(BF16)
        x = x - piece.astype(F32)
        term = jnp.dot(piece, tri, preferred_element_type=F32)
        out = term if out is None else out + term
    return out


def _att_logits(qh, k):
    z = lax.dot_general(qh, k, _MM_TB, preferred_element_type=F32)
    lsp = jnp.minimum(z, 0.0) - jnp.log(1.0 + jnp.exp(-jnp.abs(z)))
    return lsp, lsp - z


def _per_head(heads, a, b):
    return jnp.where(heads[0], a, b)


def sb_attn_fwd(qkv):
    t, d3 = qkv.shape
    d = d3 // 3
    npair = d // LANES
    bq, bk = min(ATT_BQ, t), min(ATT_BK, t)
    kpq = bq // bk

    def body(q_ref, k_ref, v_ref, o_ref, lt_ref, acc_ref):
        heads, rr, cc, kr, kc = _att_consts(bq, bk)
        suffix = (kr > kc).astype(BF16)

        def trip(qh, k0, valid, runs):
            k = k_ref[pl.ds(k0, bk), :]
            v = v_ref[pl.ds(k0, bk), :]
            new_runs = []
            for h in range(2):
                lsp, lraw = _att_logits(qh[h], k)
                lm = lraw if valid is None else jnp.where(valid, lraw, 0.0)
                w = jnp.exp(lsp + _split_dot(lm, suffix, 2) + runs[h])
                if valid is not None:
                    w = jnp.where(valid, w, 0.0)
                acc_ref[h] += jnp.dot(w.astype(BF16), v, preferred_element_type=F32)
                new_runs.append(runs[h] + jnp.sum(lm, axis=1, keepdims=True))
            return tuple(new_runs)

        def q_loop(qb, _):
            q0 = pl.multiple_of(qb * bq, bq)
            q = q_ref[pl.ds(q0, bq), :] * 0.125
            qh = [jnp.where(hm, q, 0.0).astype(BF16) for hm in heads]
            acc_ref[...] = jnp.zeros_like(acc_ref)
            runs = (jnp.zeros((bq, 1), F32),) * 2
            for dblk in reversed(range(kpq)):
                runs = trip(qh, pl.multiple_of(q0 + dblk * bk, bk), dblk * bk + cc < rr, runs)
            nleft = qb * kpq
            runs = lax.fori_loop(
                0, nleft, lambda i, r: trip(qh, pl.multiple_of((nleft - 1 - i) * bk, bk), None, r), runs)
            o_ref[pl.ds(q0, bq), :] = _per_head(heads, acc_ref[0], acc_ref[1])
            lt_ref[pl.ds(q0, bq), :] = _per_head(heads, runs[0], runs[1])
            return 0

        lax.fori_loop(0, t // bq, q_loop, 0)

    col = lambda off: pl.BlockSpec((t, LANES), lambda p: (0, p + off))
    return pl.pallas_call(
        body, grid=(npair,), in_specs=[col(0), col(npair), col(2 * npair)], out_specs=[col(0), col(0)],
        out_shape=[jax.ShapeDtypeStruct((t, d), F32)] * 2, scratch_shapes=[pltpu.VMEM((2, bq, LANES), F32)],
        name="sb_attn_fwd", compiler_params=_params(("parallel",)),
    )(qkv, qkv, qkv)


def sb_attn_bwd(qkv, ltot, do):
    t, d3 = qkv.shape
    d = d3 // 3
    npair = d // LANES
    bq, bk = min(ATT_BQ, t), min(ATT_BK, t)
    kpq = bq // bk

    def body(q_ref, k_ref, v_ref, lt_ref, do_ref, d_ref, dk_acc, dv_acc, dq_acc):
        heads, rr, cc, kr, kc = _att_consts(bq, bk)
        prefix_incl = (kr <= kc).astype(BF16)
        prefix_excl = (kr < kc).astype(BF16)
        dk_acc[...] = jnp.zeros_like(dk_acc)
        dv_acc[...] = jnp.zeros_like(dv_acc)

        def trip(qh, doh, lt, k0, valid, carry):
            lruns, gruns = carry
            k = k_ref[pl.ds(k0, bk), :]
            v = v_ref[pl.ds(k0, bk), :]
            new_lruns, new_gruns = [], []
            dk_blk = jnp.zeros((bk, LANES), F32)
            dv_blk = jnp.zeros((bk, LANES), F32)
            for h in range(2):
                lsp, lraw = _att_logits(qh[h], k)
                lm = lraw if valid is None else jnp.where(valid, lraw, 0.0)
                right = lt[h] - (lruns[h] + _split_dot(lm, prefix_incl, 2))
                w = jnp.exp(lsp + right)
                if valid is not None:
                    w = jnp.where(valid, w, 0.0)
                g = lax.dot_general(doh[h], v, _MM_TB, preferred_element_type=F32) * w
                left = gruns[h] + _split_dot(g, prefix_excl, 2)
                dz = g * jnp.exp(lraw) - jnp.exp(lsp) * left
                if valid is not None:
                    dz = jnp.where(valid, dz, 0.0)
                dz = dz.astype(BF16)
                kh = jnp.where(heads[h], k, 0.0).astype(BF16)
                dq_acc[h] += jnp.dot(dz, kh, preferred_element_type=F32)
                dk_blk = dk_blk + lax.dot_general(dz, qh[h], _MM_TA, preferred_element_type=F32)
                dv_blk = dv_blk + lax.dot_general(w.astype(BF16), doh[h], _MM_TA, preferred_element_type=F32)
                new_lruns.append(lruns[h] + jnp.sum(lm, axis=1, keepdims=True))
                new_gruns.append(gruns[h] + jnp.sum(g, axis=1, keepdims=True))
            dk_acc[pl.ds(k0, bk), :] += dk_blk
            dv_acc[pl.ds(k0, bk), :] += dv_blk
            return tuple(new_lruns), tuple(new_gruns)

        def q_loop(qb, _):
            q0 = pl.multiple_of(qb * bq, bq)
            q = q_ref[pl.ds(q0, bq), :] * 0.125
            dout = do_ref[pl.ds(q0, bq), :]
            lt2 = lt_ref[pl.ds(q0, bq), :]
            qh = [jnp.where(hm, q, 0.0).astype(BF16) for hm in heads]
            doh = [jnp.where(hm, dout, 0.0).astype(BF16) for hm in heads]
            lt = [jnp.max(jnp.where(hm, lt2, -jnp.inf), axis=1, keepdims=True) for hm in heads]
            dq_acc[...] = jnp.zeros_like(dq_acc)
            col = (jnp.zeros((bq, 1), F32),) * 2
            carry = lax.fori_loop(
                0, qb * kpq, lambda kb, c: trip(qh, doh, lt, pl.multiple_of(kb * bk, bk), None, c), (col, col))
            for dblk in range(kpq):
                carry = trip(qh, doh, lt, pl.multiple_of(q0 + dblk * bk, bk), dblk * bk + cc < rr, carry)
            d_ref[0, pl.ds(q0, bq), :] = ((dq_acc[0] + dq_acc[1]) * 0.125).astype(d_ref.dtype)
            return 0

        lax.fori_loop(0, t // bq, q_loop, 0)
        d_ref[1] = dk_acc[...].astype(d_ref.dtype)
        d_ref[2] = dv_acc[...].astype(d_ref.dtype)

    col = lambda off: pl.BlockSpec((t, LANES), lambda p: (0, p + off))
    return pl.pallas_call(
        body, grid=(npair,), in_specs=[col(0), col(npair), col(2 * npair), col(0), col(0)],
        out_specs=pl.BlockSpec((3, t, LANES), lambda p: (0, 0, p)),
        out_shape=jax.ShapeDtypeStruct((3, t, d), BF16),
        scratch_shapes=[pltpu.VMEM((t, LANES), F32), pltpu.VMEM((t, LANES), F32), pltpu.VMEM((2, bq, LANES), F32)],
        name="sb_attn_bwd", compiler_params=_params(("parallel",)),
    )(qkv, qkv, qkv, ltot, do)


def _sgu_parts(hin, g, ws_ref, bf_ref):
    width = hin.shape[1] // 2
    h = jax.nn.gelu(hin)
    u, v = h[:, :width], h[:, width:]
    r = lax.rsqrt(jnp.mean(v * v, axis=-1, keepdims=True) + EPS)
    vn = v * r * g
    rr = lax.broadcasted_iota(jnp.int32, (CHUNK, CHUNK), 0)
    cc = lax.broadcasted_iota(jnp.int32, (CHUNK, CHUNK), 1)
    causal = cc <= rr
    wcs = [jnp.where(causal, ws_ref[gi], 0.0).astype(BF16) for gi in range(SG_GROUPS)]
    sv = jnp.concatenate(
        [jnp.dot(wcs[gi], vn[:, gi * CHUNK:(gi + 1) * CHUNK].astype(BF16), preferred_element_type=F32) + bf_ref[gi]
         for gi in range(SG_GROUPS)], axis=1)
    return u, v, r, vn, wcs, sv, causal


def sgu_fwd(hin, g, ws, bfull):
    t, w2 = hin.shape
    width = w2 // 2

    def body(h_ref, g_ref, ws_ref, bf_ref, o_ref):
        u, _, _, _, _, sv, _ = _sgu_parts(h_ref[...].astype(F32), g_ref[...], ws_ref, bf_ref)
        o_ref[...] = (u * sv).astype(o_ref.dtype)

    full = lambda a: pl.BlockSpec(a.shape, lambda i, nd=a.ndim: (0,) * nd)
    return pl.pallas_call(
        body, grid=(t // CHUNK,), in_specs=[pl.BlockSpec((CHUNK, w2), lambda i: (i, 0)), full(g), full(ws), full(bfull)],
        out_specs=pl.BlockSpec((CHUNK, width), lambda i: (i, 0)), out_shape=jax.ShapeDtypeStruct((t, width), BF16),
        name="sgu_fwd", compiler_params=_params(("parallel",)),
    )(hin, g, ws, bfull)


def sgu_bwd(hin, dp, g, ws, bfull):
    t, w2 = hin.shape
    width = w2 // 2

    def body(h_ref, dp_ref, g_ref, ws_ref, bf_ref, dh_ref, dws_ref, dbf_ref, dg_ref):
        i = pl.program_id(0)
        hin_v = h_ref[...].astype(F32)
        gv = g_ref[...]
        u, v, r, vn, wcs, sv, causal = _sgu_parts(hin_v, gv, ws_ref, bf_ref)
        dpv = dp_ref[...].astype(F32)
        du = dpv * sv
        dsv = dpv * u
        dvn_parts, dws_parts, dbf_parts = [], [], []
        for gi in range(SG_GROUPS):
            dsv_g = dsv[:, gi * CHUNK:(gi + 1) * CHUNK]
            dsv_b = dsv_g.astype(BF16)
            dvn_parts.append(lax.dot_general(wcs[gi], dsv_b, _MM_TA, preferred_element_type=F32))
            vn_b = vn[:, gi * CHUNK:(gi + 1) * CHUNK].astype(BF16)
            dws_parts.append(jnp.where(causal, lax.dot_general(dsv_b, vn_b, _MM_TB, preferred_element_type=F32), 0.0))
            dbf_parts.append(jnp.broadcast_to(jnp.sum(dsv_g, axis=1, keepdims=True), (CHUNK, CHUNK)))
        dvn = jnp.concatenate(dvn_parts, axis=1)
        dgain = jnp.sum(dvn * v * r, axis=0, keepdims=True)
        gvv = dvn * gv
        dv = r * gvv - v * (r * r * r) * jnp.mean(v * gvv, axis=-1, keepdims=True)
        _, vjp = jax.vjp(jax.nn.gelu, hin_v)
        dh_ref[...] = vjp(jnp.concatenate([du, dv], axis=1))[0].astype(dh_ref.dtype)

        @pl.when(i == 0)
        def _():
            for gi in range(SG_GROUPS):
                dws_ref[gi] = dws_parts[gi]
                dbf_ref[gi] = dbf_parts[gi]
            dg_ref[...] = dgain

        @pl.when(i > 0)
        def _():
            for gi in range(SG_GROUPS):
                dws_ref[gi] += dws_parts[gi]
                dbf_ref[gi] += dbf_parts[gi]
            dg_ref[...] += dgain

    full = lambda a: pl.BlockSpec(a.shape, lambda i, nd=a.ndim: (0,) * nd)
    sq = (SG_GROUPS, CHUNK, CHUNK)
    return pl.pallas_call(
        body, grid=(t // CHUNK,),
        in_specs=[pl.BlockSpec((CHUNK, w2), lambda i: (i, 0)), pl.BlockSpec((CHUNK, width), lambda i: (i, 0)),
                  full(g), full(ws), full(bfull)],
        out_specs=[pl.BlockSpec((CHUNK, w2), lambda i: (i, 0)), pl.BlockSpec(sq, lambda i: (0, 0, 0)),
                   pl.BlockSpec(sq, lambda i: (0, 0, 0)), pl.BlockSpec((1, width), lambda i: (0, 0))],
        out_shape=[jax.ShapeDtypeStruct((t, w2), BF16), jax.ShapeDtypeStruct(sq, F32), jax.ShapeDtypeStruct(sq, F32),
                   jax.ShapeDtypeStruct((1, width), F32)],
        name="sgu_bwd", compiler_params=_params(("arbitrary",)),
    )(hin, dp, g, ws, bfull)


def _disc1(lam_re, lam_im, log_dt):
    lr = jnp.minimum(lam_re, -1e-4)
    li = lam_im
    dt = jnp.exp(log_dt)
    mag = jnp.exp(dt * lr)
    ar = mag * jnp.cos(dt * li)
    ai = mag * jnp.sin(dt * li)
    den = lr * lr + li * li
    return ar, ai, ((ar - 1.0) * lr + ai * li) / den, (ai * lr - (ar - 1.0) * li) / den


def _disc2(cre, cim, b_re, b_im):
    return cre * b_re - cim * b_im, cre * b_im + cim * b_re


def _single(fn, ins, out_shapes, name):
    n = len(ins)

    def body(*refs):
        vals = fn(*[r[...] for r in refs[:n]])
        for ref, val in zip(refs[n:], vals):
            ref[...] = val

    return pl.pallas_call(body, out_shape=[jax.ShapeDtypeStruct(s, F32) for s in out_shapes], name=name)(*ins)


SCAN_SEGMENTS = 8


def s5_reorder(x, to_steps):
    t, d = x.shape
    ns = SCAN_SEGMENTS
    seg = t // ns
    by_segment, by_step = (ns, seg, d), (seg, ns, d)

    def body(x_ref, o_ref, sems):
        copies = []
        for s in range(ns):
            src, dst = (x_ref.at[s], o_ref.at[:, s, :]) if to_steps else (x_ref.at[:, s, :], o_ref.at[s])
            copies.append(pltpu.make_async_copy(src, dst, sems.at[s]))
        for cp in copies:
            cp.start()
        for cp in copies:
            cp.wait()

    any_spec = pl.BlockSpec(memory_space=pl.ANY)
    out = pl.pallas_call(
        body, in_specs=[any_spec], out_specs=any_spec, scratch_shapes=[pltpu.SemaphoreType.DMA((ns,))],
        out_shape=jax.ShapeDtypeStruct(by_step if to_steps else by_segment, x.dtype), name="s5_reorder",
    )(x.reshape(by_segment if to_steps else by_step))
    return out.reshape(t, d)


def _cpow(ar, ai, n):
    rr, ri = None, None
    while n:
        if n & 1:
            rr, ri = (ar, ai) if rr is None else (rr * ar - ri * ai, rr * ai + ri * ar)
        ar, ai = ar * ar - ai * ai, 2.0 * ar * ai
        n >>= 1
    return rr, ri


def _edge_states(er, ei, pr, pi, reverse):
    ns = SCAN_SEGMENTS
    zero = jnp.zeros_like(er[0:1])
    rows_r, rows_i = [None] * ns, [None] * ns
    order = range(ns - 1, -1, -1) if reverse else range(ns)
    prev = None
    for s in order:
        if prev is None:
            rows_r[s], rows_i[s] = zero, zero
        else:
            cr, ci = rows_r[prev], rows_i[prev]
            rows_r[s] = er[prev:prev + 1] + pr * cr - pi * ci
            rows_i[s] = ei[prev:prev + 1] + pr * ci + pi * cr
        prev = s
    return jnp.concatenate(rows_r, axis=0), jnp.concatenate(rows_i, axis=0)


def s5_scan_fwd(bu2, a2):
    _, t, n = bu2.shape
    cb, ns = SCAN_COLS, SCAN_SEGMENTS
    seg = t // ns

    def body(bu_ref, a_ref, x_ref):
        ar, ai = a_ref[0:1, :], a_ref[1:2, :]

        def local(i, carry):
            xr, xi = carry
            xr, xi = ar * xr - ai * xi + bu_ref[0, i], ar * xi + ai * xr + bu_ref[1, i]
            x_ref[0, i] = xr
            x_ref[1, i] = xi
            return xr, xi

        zero = jnp.zeros((ns, cb), F32)
        er, ei = lax.fori_loop(0, seg, local, (zero, zero))
        cr, ci = _edge_states(er, ei, *_cpow(ar, ai, seg), reverse=False)

        def fix(i, carry):
            wr, wi = carry
            wr, wi = wr * ar - wi * ai, wr * ai + wi * ar
            x_ref[0, i] += wr * cr - wi * ci
            x_ref[1, i] += wr * ci + wi * cr
            return wr, wi

        lax.fori_loop(0, seg, fix, (jnp.ones((1, cb), F32), jnp.zeros((1, cb), F32)))

    blk = pl.BlockSpec((2, seg, ns, cb), lambda j: (0, 0, 0, j))
    out = pl.pallas_call(
        body, grid=(n // cb,), in_specs=[blk, pl.BlockSpec((2, cb), lambda j: (0, j))], out_specs=blk,
        out_shape=jax.ShapeDtypeStruct((2, seg, ns, n), F32), name="s5_scan_fwd", compiler_params=_params(("parallel",)),
    )(bu2.reshape(2, seg, ns, n), a2)
    return out.reshape(2, t, n)


def s5_scan_bwd(dx2, x2, a2):
    _, t, n = dx2.shape
    cb, ns = SCAN_COLS, SCAN_SEGMENTS
    seg = t // ns

    def body(dx_ref, x_ref, a_ref, g_ref, da_ref):
        ar, ai = a_ref[0:1, :], a_ref[1:2, :]

        def local(s, carry):
            gr, gi = carry
            i = seg - 1 - s
            gr, gi = dx_ref[0, i] + ar * gr + ai * gi, dx_ref[1, i] - ai * gr + ar * gi
            g_ref[0, i] = gr
            g_ref[1, i] = gi
            return gr, gi

        zero = jnp.zeros((ns, cb), F32)
        er, ei = lax.fori_loop(0, seg, local, (zero, zero))
        cr, ci = _edge_states(er, ei, *_cpow(ar, -ai, seg), reverse=True)
        row = lax.broadcasted_iota(jnp.int32, (ns, cb), 0)
        before_r = jnp.where(row == 0, 0.0, pltpu.roll(x_ref[0, seg - 1], 1, 0))
        before_i = jnp.where(row == 0, 0.0, pltpu.roll(x_ref[1, seg - 1], 1, 0))

        def fix(s, carry):
            wr, wi, dar, dai = carry
            i = seg - 1 - s
            wr, wi = wr * ar + wi * ai, wi * ar - wr * ai
            gr = g_ref[0, i] + wr * cr - wi * ci
            gi = g_ref[1, i] + wr * ci + wi * cr
            g_ref[0, i] = gr
            g_ref[1, i] = gi
            ip = jnp.maximum(i - 1, 0)
            xpr = jnp.where(i == 0, before_r, x_ref[0, ip])
            xpi = jnp.where(i == 0, before_i, x_ref[1, ip])
            return wr, wi, dar + gr * xpr + gi * xpi, dai + gi * xpr - gr * xpi

        one, z1 = jnp.ones((1, cb), F32), jnp.zeros((1, cb), F32)
        _, _, dar, dai = lax.fori_loop(0, seg, fix, (one, z1, zero, zero))
        da_ref[0:1, :] = jnp.sum(dar, axis=0, keepdims=True)
        da_ref[1:2, :] = jnp.sum(dai, axis=0, keepdims=True)

    blk = pl.BlockSpec((2, seg, ns, cb), lambda j: (0, 0, 0, j))
    vec = pl.BlockSpec((2, cb), lambda j: (0, j))
    g4, da = pl.pallas_call(
        body, grid=(n // cb,), in_specs=[blk, blk, vec], out_specs=[blk, vec],
        out_shape=[jax.ShapeDtypeStruct((2, seg, ns, n), F32), jax.ShapeDtypeStruct((2, n), F32)],
        name="s5_scan_bwd", compiler_params=_params(("parallel",)),
    )(dx2.reshape(2, seg, ns, n), x2.reshape(2, seg, ns, n), a2)
    return g4.reshape(2, t, n), da


_SP_U = SSM_PACK * SSM_GROUP
_SP_X = SSM_PACK * SSM_STATE
_NKB = SSM_GROUPS // SSM_PACK


def mm_s5(kind, a, b, m, name, res=None, tm=2048):
    tm = min(tm, m)
    kw = dict(passes=S5_PASSES, name=name)
    xblk = lambda row, sel, col: ((None, tm, _SP_X), lambda *g: (sel(*g), row(*g), col(*g)))
    if kind == "bu":
        o_blk, o_map = xblk(lambda g, i, k: i, lambda g, i, k: g // _NKB, lambda g, i, k: g % _NKB)
        return _mm(a, b, grid=(2 * _NKB, m // tm, 1), a_blk=(tm, _SP_U), a_map=lambda g, i, k: (i, g % _NKB),
                   b_blk=(None, None, _SP_U, _SP_X), b_map=lambda g, i, k: (g // _NKB, g % _NKB, 0, 0),
                   o_blk=o_blk, o_map=o_map, out_shape=(2, m, _NKB * _SP_X), out_dtype=F32, **kw)
    if kind == "yc":
        a_blk, a_map = xblk(lambda j, i, k: i, lambda j, i, k: k, lambda j, i, k: j)
        return _mm(a, b, grid=(_NKB, m // tm, 2), a_blk=a_blk, a_map=a_map,
                   b_blk=(None, None, _SP_X, _SP_U), b_map=lambda j, i, k: (k, j, 0, 0),
                   o_blk=(tm, _SP_U), o_map=lambda j, i, k: (i, j), out_shape=(m, _NKB * _SP_U), out_dtype=F32,
                   acc_2d=(tm, _SP_U), **kw)
    if kind == "dx":
        o_blk, o_map = xblk(lambda g, i, k: i, lambda g, i, k: g // _NKB, lambda g, i, k: g % _NKB)
        return _mm(a, b, grid=(2 * _NKB, m // tm, 1), a_blk=(tm, _SP_U), a_map=lambda g, i, k: (i, g % _NKB),
                   b_blk=(None, None, _SP_X, _SP_U), b_map=lambda g, i, k: (g // _NKB, g % _NKB, 0, 0),
                   o_blk=o_blk, o_map=o_map, out_shape=(2, m, _NKB * _SP_X), out_dtype=F32, dims=_MM_TB, **kw)
    if kind == "dcd":
        a_blk, a_map = xblk(lambda g, _, k: k, lambda g, _, k: g // _NKB, lambda g, _, k: g % _NKB)
        return _mm(a, b, grid=(2 * _NKB, 1, m // tm), a_blk=a_blk, a_map=a_map,
                   b_blk=(tm, _SP_U), b_map=lambda g, _, k: (k, g % _NKB),
                   o_blk=(None, None, _SP_X, _SP_U), o_map=lambda g, _, k: (g // _NKB, g % _NKB, 0, 0),
                   out_shape=(2, _NKB, _SP_X, _SP_U), out_dtype=F32, dims=_MM_TA, acc_2d=(_SP_X, _SP_U), **kw)
    if kind == "du":
        a_blk, a_map = xblk(lambda j, i, k: i, lambda j, i, k: k, lambda j, i, k: j)
        return _mm(a, b, grid=(_NKB, m // tm, 2), a_blk=a_blk, a_map=a_map,
                   b_blk=(None, None, _SP_U, _SP_X), b_map=lambda j, i, k: (k, j, 0, 0),
                   o_blk=(tm, _SP_U), o_map=lambda j, i, k: (i, j), out_shape=(m, _NKB * _SP_U), out_dtype=F32,
                   dims=_MM_TB, acc_2d=(tm, _SP_U), res=res, res_blk=(tm, _SP_U), res_map=lambda j, i, k: (i, j), **kw)
    assert kind == "dbd"
    b_blk, b_map = xblk(lambda g, _, k: k, lambda g, _, k: g // _NKB, lambda g, _, k: g % _NKB)
    return _mm(a, b, grid=(2 * _NKB, 1, m // tm), a_blk=(tm, _SP_U), a_map=lambda g, _, k: (k, g % _NKB),
               b_blk=b_blk, b_map=b_map,
               o_blk=(None, None, _SP_U, _SP_X), o_map=lambda g, _, k: (g // _NKB, g % _NKB, 0, 0),
               out_shape=(2, _NKB, _SP_U, _SP_X), out_dtype=F32, dims=_MM_TA, acc_2d=(_SP_U, _SP_X), **kw)


def _block_diag(w):
    g, a, b = w.shape
    eye = jnp.eye(SSM_PACK, dtype=w.dtype)
    wp = w.reshape(g // SSM_PACK, SSM_PACK, a, b)
    return jnp.einsum("kgab,gh->kgahb", wp, eye).reshape(g // SSM_PACK, SSM_PACK * a, SSM_PACK * b)


def _block_diag_t(d, a, b):
    k = d.shape[0]
    eye = jnp.eye(SSM_PACK, dtype=d.dtype)
    dp = d.reshape(k, SSM_PACK, a, SSM_PACK, b)
    return jnp.einsum("kgahb,gh->kgab", dp, eye).reshape(k * SSM_PACK, a, b)


def _coords():
    return lax.axis_index("x"), lax.axis_index("y"), lax.axis_index("c")


def all_gather(tensors, name):
    n = len(tensors)
    any_spec = pl.BlockSpec(memory_space=pl.ANY)

    def body(*refs):
        ins, outs = refs[:n], refs[n:2 * n]
        send, recv, local = refs[2 * n:]
        x, y, c = _coords()
        me, sibling = (x, y, c), (x, y, 1 - c)
        chips = [(1 - x, y), (x, 1 - y), (1 - x, 1 - y)]

        def slot(p):
            return 4 * p[0] + 2 * p[1] + p[2]

        def copy(t, k, block, to, src=None):
            dst = outs[t].at[slot(block)]
            return pltpu.make_async_remote_copy(
                src_ref=dst if src is None else src, dst_ref=dst, send_sem=send.at[7 * t + k],
                recv_sem=recv.at[7 * t + k], device_id=to, device_id_type=pl.DeviceIdType.MESH)

        own, sent = [], []
        for t in range(n):
            mine = pltpu.make_async_copy(ins[t], outs[t].at[slot(me)], local.at[t])
            mine.start()
            own.append(mine)
            first = [copy(t, 0, me, sibling, src=ins[t])]
            first += [copy(t, 1 + j, me, (*chip, c), src=ins[t]) for j, chip in enumerate(chips)]
            for cp in first:
                cp.start()
            sent += first
        for t in range(n):
            for j, chip in enumerate(chips):
                copy(t, 1 + j, (*chip, c), me).wait_recv()
                passed = copy(t, 4 + j, (*chip, c), sibling)
                passed.start()
                sent.append(passed)
        for t in range(n):
            copy(t, 0, sibling, me).wait_recv()
            for j, chip in enumerate(chips):
                copy(t, 4 + j, (*chip, 1 - c), me).wait_recv()
        for cp in sent:
            cp.wait_send()
        for cp in own:
            cp.wait()

    return pl.pallas_call(
        body, in_specs=[any_spec] * n, out_specs=[any_spec] * n,
        out_shape=[jax.ShapeDtypeStruct((NDEV,) + a.shape, a.dtype) for a in tensors],
        scratch_shapes=[pltpu.SemaphoreType.DMA((7 * n,)), pltpu.SemaphoreType.DMA((7 * n,)),
                        pltpu.SemaphoreType.DMA((n,))],
        name=name,
    )(*tensors)


_HBM_SPEC = pl.BlockSpec(memory_space=pltpu.HBM)
_SEM_SPEC = pl.BlockSpec(memory_space=pltpu.SEMAPHORE)
_NPEER = NDEV - 1


def _split_copy_params():
    return pltpu.CompilerParams(has_side_effects=pltpu.SideEffectType.DATAFLOW_SIDE_EFFECTING)


def _me_and_peers():
    x, y, c = _coords()
    peers = []
    for rel in range(1, NDEV):
        p = (1 - x if rel & 4 else x, 1 - y if rel & 2 else y, 1 - c if rel & 1 else c)
        peers.append((p, 4 * p[0] + 2 * p[1] + p[2]))
    return 4 * x + 2 * y + c, peers


def _hbm(a):
    return pltpu.with_memory_space_constraint(a, pltpu.HBM)


def gather_start(bufs, name):
    n = len(bufs)

    def body(*refs):
        ins, outs = refs[:n], refs[n:]
        me, peers = _me_and_peers()
        for t in range(n):
            for k, (dev, _) in enumerate(peers):
                pltpu.make_async_remote_copy(
                    src_ref=ins[t].at[me], dst_ref=ins[t].at[me], send_sem=outs[3 * t].at[k],
                    recv_sem=outs[3 * t + 1].at[k], device_id=dev, device_id_type=pl.DeviceIdType.MESH).start()

    out_shape, out_specs = [], []
    for b in bufs:
        out_shape += [pltpu.SemaphoreType.DMA((_NPEER,)), pltpu.SemaphoreType.DMA((_NPEER,)), pltpu.HBM(b.shape, b.dtype)]
        out_specs += [_SEM_SPEC, _SEM_SPEC, _HBM_SPEC]
    res = pl.pallas_call(
        body, name=name, out_shape=tuple(out_shape), in_specs=[_HBM_SPEC] * n, out_specs=tuple(out_specs),
        input_output_aliases={t: 3 * t + 2 for t in range(n)}, compiler_params=_split_copy_params(),
    )(*[_hbm(b) for b in bufs])
    return [tuple(res[3 * t:3 * t + 3]) for t in range(n)]


def gather_wait(started, after, name):
    n = len(started)

    def body(*refs):
        bufs, sems = refs[:n], refs[n:3 * n]
        me, peers = _me_and_peers()
        for t in range(n):
            for k, (dev, slot) in enumerate(peers):
                cp = pltpu.make_async_remote_copy(
                    src_ref=bufs[t].at[me], dst_ref=bufs[t].at[slot], send_sem=sems[2 * t].at[k],
                    recv_sem=sems[2 * t + 1].at[k], device_id=dev, device_id_type=pl.DeviceIdType.MESH)
                cp.wait_recv()
                cp.wait_send()

    args = [s[2] for s in started] + [sem for s in started for sem in s[:2]] + [after]
    res = pl.pallas_call(
        body, name=name, out_shape=tuple(pltpu.HBM(s[2].shape, s[2].dtype) for s in started),
        in_specs=[_HBM_SPEC] * n + [_SEM_SPEC] * (2 * n) + [pl.BlockSpec(memory_space=pl.ANY)],
        out_specs=tuple([_HBM_SPEC] * n), input_output_aliases={t: t for t in range(n)},
        compiler_params=_split_copy_params(),
    )(*args)
    return list(res)


def scatter_start(srcs, name):
    n = len(srcs)
    lands = [lax.empty((_NPEER,) + s.shape[1:], s.dtype) for s in srcs]

    def body(*refs):
        ins, land_refs, outs = refs[:n], refs[n:2 * n], refs[2 * n:]
        _, peers = _me_and_peers()
        for t in range(n):
            for k, (dev, slot) in enumerate(peers):
                pltpu.make_async_remote_copy(
                    src_ref=ins[t].at[slot], dst_ref=land_refs[t].at[k], send_sem=outs[4 * t].at[k],
                    recv_sem=outs[4 * t + 1].at[k], device_id=dev, device_id_type=pl.DeviceIdType.MESH).start()
        outs[4 * n][...] = jnp.zeros_like(outs[4 * n])

    out_shape, out_specs = [], []
    for s, land in zip(srcs, lands):
        out_shape += [pltpu.SemaphoreType.DMA((_NPEER,)), pltpu.SemaphoreType.DMA((_NPEER,)),
                      pltpu.HBM(s.shape, s.dtype), pltpu.HBM(land.shape, land.dtype)]
        out_specs += [_SEM_SPEC, _SEM_SPEC, _HBM_SPEC, _HBM_SPEC]
    out_shape.append(jax.ShapeDtypeStruct((8, LANES), F32))
    out_specs.append(pl.BlockSpec(memory_space=pltpu.VMEM))
    aliases = {t: 4 * t + 2 for t in range(n)}
    aliases.update({n + t: 4 * t + 3 for t in range(n)})
    res = pl.pallas_call(
        body, name=name, out_shape=tuple(out_shape), in_specs=[_HBM_SPEC] * (2 * n), out_specs=tuple(out_specs),
        input_output_aliases=aliases, compiler_params=_split_copy_params(),
    )(*[_hbm(s) for s in srcs], *[_hbm(land) for land in lands])
    return [tuple(res[4 * t:4 * t + 4]) for t in range(n)], res[4 * n]


def scatter_wait(started, after, name):
    n = len(started)

    def body(*refs):
        srcs, land_refs, sems = refs[:n], refs[n:2 * n], refs[2 * n:4 * n]
        _, peers = _me_and_peers()
        for t in range(n):
            for k, (dev, slot) in enumerate(peers):
                cp = pltpu.make_async_remote_copy(
                    src_ref=srcs[t].at[slot], dst_ref=land_refs[t].at[k], send_sem=sems[2 * t].at[k],
                    recv_sem=sems[2 * t + 1].at[k], device_id=dev, device_id_type=pl.DeviceIdType.MESH)
                cp.wait_recv()
                cp.wait_send()

    args = [s[2] for s in started] + [s[3] for s in started] + [sem for s in started for sem in s[:2]] + [after]
    res = pl.pallas_call(
        body, name=name,
        out_shape=tuple([pltpu.HBM(s[2].shape, s[2].dtype) for s in started]
                        + [pltpu.HBM(s[3].shape, s[3].dtype) for s in started]),
        in_specs=[_HBM_SPEC] * (2 * n) + [_SEM_SPEC] * (2 * n) + [pl.BlockSpec(memory_space=pl.ANY)],
        out_specs=tuple([_HBM_SPEC] * (2 * n)), input_output_aliases={t: t for t in range(2 * n)},
        compiler_params=_split_copy_params(),
    )(*args)
    return [(res[t], res[n + t]) for t in range(n)]


def all_to_all(groups, name):
    flat = [a for grp in groups for a in grp]
    n = len(flat)
    where = [(gi, li) for gi, grp in enumerate(groups) for li in range(len(grp))]
    any_spec = pl.BlockSpec(memory_space=pl.ANY)

    def body(*refs):
        ins, outs = refs[:n], refs[n:n + len(groups)]
        send, recv, local = refs[n + len(groups):]
        x, y, c = _coords()
        me = 4 * x + 2 * y + c
        waits = []
        for e in range(n):
            gi, li = where[e]
            mine = pltpu.make_async_copy(ins[e].at[me], outs[gi].at[me, li], local.at[e])
            mine.start()
            waits.append(mine)
        sent, landing = [], []
        for rel in range(1, NDEV):
            px = 1 - x if rel & 4 else x
            py = 1 - y if rel & 2 else y
            pc = 1 - c if rel & 1 else c
            peer = 4 * px + 2 * py + pc
            for e in range(n):
                gi, li = where[e]

                def copy(dst_slot, e=e, gi=gi, li=li, rel=rel, peer=peer, to=(px, py, pc)):
                    return pltpu.make_async_remote_copy(
                        src_ref=ins[e].at[peer], dst_ref=outs[gi].at[dst_slot, li], send_sem=send.at[7 * e + rel - 1],
                        recv_sem=recv.at[7 * e + rel - 1], device_id=to, device_id_type=pl.DeviceIdType.MESH)

                cp = copy(me)
                cp.start()
                sent.append(cp)
                landing.append(copy(peer))
        for cp in landing:
            cp.wait_recv()
        for cp in sent:
            cp.wait_send()
        for cp in waits:
            cp.wait()

    return pl.pallas_call(
        body, in_specs=[any_spec] * n, out_specs=[any_spec] * len(groups),
        out_shape=[jax.ShapeDtypeStruct((NDEV, len(grp)) + grp[0].shape[1:], grp[0].dtype) for grp in groups],
        scratch_shapes=[pltpu.SemaphoreType.DMA((7 * n,)), pltpu.SemaphoreType.DMA((7 * n,)),
                        pltpu.SemaphoreType.DMA((n,))],
        name=name,
    )(*flat)


_PACK_QUANTUM = 8 * LANES


def _pack(parts, lead=0):
    out = []
    for p in parts:
        head = p.shape[:lead]
        f = p.astype(F32).reshape(head + (-1,))
        pad = (-f.shape[-1]) % _PACK_QUANTUM
        if pad:
            f = jnp.concatenate([f, jnp.zeros(head + (pad,), F32)], axis=-1)
        out.append(f.reshape(head + (-1, LANES)))
    return jnp.concatenate(out, axis=lead)


def _unpack(buf, shapes):
    head = buf.shape[:-2]
    out, r = [], 0
    for s in shapes:
        n = 1
        for v in s:
            n *= v
        nr = -(-n // _PACK_QUANTUM) * 8
        flat = buf[..., r:r + nr, :].reshape(head + (nr * LANES,))[..., :n]
        out.append(flat.reshape(head + tuple(s)))
        r += nr
    return out


BIG = ("sb_w_qkv", "sb_w_o", "sg_w_in", "sg_w_o", "ssm_w_in", "ssm_w_glu", "ffn_w_up", "ffn_w_down")
SMALL_SHARDED = ("norm_g", "ssm_d", "ffn_conv_w")
REPLICATED = ("final_norm_g", "sg_norm_g", "sg_w_s", "sg_b", "ssm_lam_re", "ssm_lam_im", "ssm_log_dt",
              "ssm_b_re", "ssm_b_im", "ssm_c_re", "ssm_c_im", "ffn_conv_b")
WEIGHTS = ("norm_g", "final_norm_g", "sb_w_qkv", "sb_w_o", "sg_w_in", "sg_norm_g", "sg_w_s", "sg_b", "sg_w_o",
           "ssm_w_in", "ssm_lam_re", "ssm_lam_im", "ssm_log_dt", "ssm_b_re", "ssm_b_im", "ssm_c_re", "ssm_c_im",
           "ssm_d", "ssm_w_glu", "ffn_w_up", "ffn_conv_w", "ffn_conv_b", "ffn_w_down")


def _step(x, loss_target, w, m, v):
    t, d = x.shape[1], x.shape[2]
    depth = w["norm_g"].shape[0]
    x0 = x.reshape(t, d)
    tgt = loss_target.reshape(t, d)

    mx, my, mc = _coords()
    me = (4 * mx + 2 * my + mc).astype(jnp.int32).reshape(1)
    shard_pack = _pack([w[k] for k in SMALL_SHARDED])
    gathered_small, = all_gather([shard_pack], name="gather_small_weights")
    mixer_weights = (("sb_w_qkv", "sb_w_o"), ("sg_w_in", "sg_w_o"), ("ssm_w_in", "ssm_w_glu"))
    order = []
    for i in range(depth):
        order += [(k, i // 3) for k in mixer_weights[i % 3]] + [("ffn_w_up", i), ("ffn_w_down", i)]
    pending = dict(zip(order, gather_start([cast_into_slot(w[k], l, me) for k, l in order], "gather_weights_start")))
    wg = {}

    def weights(keys, after):
        missing = [key for key in keys if key not in wg]
        if missing:
            for key, buf in zip(missing, gather_wait([pending[key] for key in missing], after, "gather_weights_wait")):
                wg[key] = buf[:, None]
        return [wg[key] for key in keys]

    ng, sd, cw = _unpack(gathered_small, [w[k].shape for k in SMALL_SHARDED])
    norm_full = jnp.transpose(ng, (1, 2, 0, 3)).reshape(depth, 2, d)
    ssm_d_full = jnp.transpose(sd, (1, 0, 2)).reshape(1, d)
    nc = cw.shape[-1]
    conv_b3 = w["ffn_conv_b"].reshape(depth, NDEV, nc)
    p3 = [jnp.concatenate([cw[:, l], conv_b3[l][:, None, :], jnp.zeros((NDEV, 8 - CONV_K - 1, nc), F32)], axis=1)
          for l in range(depth)]

    g_, p_, h_ = SSM_GROUPS, SSM_STATE, SSM_GROUP
    lam_re, lam_im = w["ssm_lam_re"][0], w["ssm_lam_im"][0]
    log_dt = w["ssm_log_dt"][0].reshape(g_, 1)
    b_re, b_im = w["ssm_b_re"][0].reshape(g_ * p_, h_), w["ssm_b_im"][0].reshape(g_ * p_, h_)
    ar, ai, cre, cim = _single(_disc1, [lam_re, lam_im, log_dt], [(g_, p_)] * 4, "s5_disc1")
    cre_c, cim_c = cre.reshape(g_ * p_, 1), cim.reshape(g_ * p_, 1)
    bbr, bbi = _single(_disc2, [cre_c, cim_c, b_re, b_im], [(g_ * p_, h_)] * 2, "s5_disc2")
    per_group_t = lambda a, r, c: jnp.swapaxes(a.reshape(g_, r, c), 1, 2)
    bd = jnp.stack([_block_diag(per_group_t(bbr, p_, h_)), _block_diag(per_group_t(bbi, p_, h_))])
    cd = jnp.stack([_block_diag(per_group_t(w["ssm_c_re"][0], h_, p_)),
                    -_block_diag(per_group_t(w["ssm_c_im"][0], h_, p_))])
    a2 = jnp.stack([ar.reshape(g_ * p_), ai.reshape(g_ * p_)])

    sg_gain = w["sg_norm_g"]
    sg_ws = w["sg_w_s"][0]
    sg_bfull = jnp.broadcast_to(w["sg_b"][0][:, :, None], sg_ws.shape)

    acts = []
    xc = x0
    for i in range(depth):
        mixer, j = i % 3, i // 3
        st = {"x": xc}
        g0 = norm_full[i, 0][None]
        xn = rms_fwd(xc, g0, "rms_fwd")
        st["xn"] = xn
        k_in, k_out = [(k, j) for k in mixer_weights[mixer]]
        w_in, = weights([k_in], xn)
        if mixer == 0:
            qkv = mm_cs_fwd(xn, w_in, 0, BF16, "qkv_fwd")
            o, ltot = sb_attn_fwd(qkv)
            w_out, = weights([k_out], o)
            x1 = mm_rs_fwd(o, w_out, 0, xc, F32, "attn_out_fwd")
            st.update(qkv=qkv, o=o, ltot=ltot)
        elif mixer == 1:
            hin = mm_cs_fwd(xn, w_in, 0, BF16, "sg_in_fwd")
            p = sgu_fwd(hin, sg_gain, sg_ws, sg_bfull)
            w_out, = weights([k_out], p)
            x1 = mm_rs_fwd(p, w_out, 0, xc, F32, "sg_out_fwd")
            st.update(hin=hin, p=p)
        else:
            u = mm_rs_fwd(xn, w_in, 0, None, F32, "ssm_in_fwd")
            u_s = s5_reorder(u, True)
            x2 = s5_scan_fwd(mm_s5("bu", u_s, bd, t, "s5_bu"), a2)
            yc_s = mm_s5("yc", x2, cd, t, "s5_yc")
            yg = s5_post_fwd(s5_reorder(yc_s, False), u, ssm_d_full)
            w_out, = weights([k_out], yg)
            hg = mm_cs_fwd(yg, w_out, 0, BF16, "ssm_glu_fwd")
            x1 = glu_fwd(hg, xc)
            st.update(u_s=u_s, x2=x2, yc_s=yc_s, yg=yg, hg=hg)
        g1 = norm_full[i, 1][None]
        xn2 = rms_fwd(x1, g1, "rms_fwd")
        w_up, w_down = weights([("ffn_w_up", i), ("ffn_w_down", i)], xn2)
        h3 = mm_cs_fwd(xn2, w_up, 0, BF16, "ffn_up_fwd")
        gated = ffn_gate_fwd(h3, p3[i])
        xc = mm_down_fwd(gated, w_down, 0, x1, "ffn_down_fwd")
        st.update(x1=x1, xn2=xn2, h3=h3, gated=gated, g0=g0, g1=g1)
        acts.append(st)

    dx, loss_lanes, d_final_g = loss_head(xc, w["final_norm_g"][None], tgt)
    loss = lax.psum(loss_lanes[0, 0], MESH_AXES)

    scattering = {}
    d_norm = [[None, None] for _ in range(depth)]
    d_p3 = [None] * depth
    rep = {}
    d_ssm_d = None
    token = None

    def scatter(grads_by_key):
        keys = list(grads_by_key)
        started, tok = scatter_start([grads_by_key[key] for key in keys], "scatter_grads_start")
        scattering[tuple(keys)] = started
        return tok

    for i in reversed(range(depth)):
        mixer, j = i % 3, i // 3
        st = acts[i]
        k_in, k_out = [(k, j) for k in mixer_weights[mixer]]
        w_in, w_out, w_up, w_down = weights([k_in, k_out, ("ffn_w_up", i), ("ffn_w_down", i)], None)
        dgated = mm_down_da(dx, w_down, 0, "ffn_down_da", dep=token)
        g_down = mm_down_dw(st["gated"], dx, "ffn_down_dw")
        dy_a, dy_g, dp_a, dp_g = ffn_gate_bwd(st["h3"], dgated, p3[i])
        d_p3[i] = jnp.concatenate([dp_a, dp_g], axis=0)
        dh3 = ffn_conv_t(dy_a, dy_g, p3[i])
        dxn2 = mm_cs_da(dh3, w_up, 0, t, "ffn_up_da")
        g_up = mm_cs_dw(st["xn2"], dh3, nc, "ffn_up_dw")
        dx1, d_norm[i][1] = rms_bwd(st["x1"], st["g1"], dxn2, dx, "rms_bwd")
        token = scatter({("ffn_w_down", i): g_down, ("ffn_w_up", i): g_up})
        if mixer == 0:
            do = mm_rs_da(dx1, w_out, 0, BF16, "attn_out_da", dep=token)
            g_out = mm_rs_dw(st["o"], dx1, "attn_out_dw")
            d3 = sb_attn_bwd(st["qkv"], st["ltot"], do)
            g_in = mm_qkv_dw(st["xn"], d3, w_in.shape[3], "qkv_dw")
            token = scatter({k_in: g_in, k_out: g_out})
            dxn = mm_qkv_da(d3, w_in, 0, "qkv_da", dep=token)
        elif mixer == 1:
            dp = mm_rs_da(dx1, w_out, 0, BF16, "sg_out_da", dep=token)
            g_out = mm_rs_dw(st["p"], dx1, "sg_out_dw")
            dhin, d_ws, d_bfull, d_gain = sgu_bwd(st["hin"], dp, sg_gain, sg_ws, sg_bfull)
            rep.update(sg_w_s=d_ws[None], sg_b=d_bfull[None, :, :, 0], sg_norm_g=d_gain)
            dxn = mm_cs_da(dhin, w_in, 0, t, "sg_in_da")
            g_in = mm_cs_dw(st["xn"], dhin, w_in.shape[3], "sg_in_dw")
        else:
            dhg = glu_bwd(st["hg"], dx1, token)
            dyg = mm_cs_da(dhg, w_out, 0, t, "ssm_glu_da")
            g_out = mm_cs_dw(st["yg"], dhg, w_out.shape[3], "ssm_glu_dw")
            dyc_s, du_skip_s, d_ssm_d = s5_post_bwd(st["yc_s"], st["u_s"], ssm_d_full, s5_reorder(dyg, True))
            dx2 = mm_s5("dx", dyc_s, cd, t, "s5_dx")
            dcd = mm_s5("dcd", st["x2"], dyc_s, t, "s5_dcd")
            g2, da2 = s5_scan_bwd(dx2, st["x2"], a2)
            du = s5_reorder(mm_s5("du", g2, bd, t, "s5_du", res=du_skip_s), False)
            dbd = mm_s5("dbd", st["u_s"], g2, t, "s5_dbd")
            from_bd = lambda blk: jnp.swapaxes(_block_diag_t(blk, h_, p_), 1, 2).reshape(g_ * p_, h_)

            def disc2_bwd(c1, c2, b1, b2, t1, t2):
                return jax.vjp(_disc2, c1, c2, b1, b2)[1]((t1, t2))

            d_cre, d_cim, d_b_re, d_b_im = _single(
                disc2_bwd, [cre_c, cim_c, b_re, b_im, from_bd(dbd[0]), from_bd(dbd[1])],
                [(g_ * p_, 1)] * 2 + [(g_ * p_, h_)] * 2, "s5_disc2_bwd")

            def disc1_bwd(l1, l2, ld, t1, t2, t3, t4):
                return jax.vjp(_disc1, l1, l2, ld)[1]((t1, t2, t3, t4))

            d_lam_re, d_lam_im, d_log_dt = _single(
                disc1_bwd, [lam_re, lam_im, log_dt, da2[0].reshape(g_, p_), da2[1].reshape(g_, p_),
                            d_cre.reshape(g_, p_), d_cim.reshape(g_, p_)],
                [(g_, p_), (g_, p_), (g_, 1)], "s5_disc1_bwd")
            from_cd = lambda blk: jnp.swapaxes(_block_diag_t(blk, p_, h_), 1, 2)
            rep.update(ssm_lam_re=d_lam_re[None], ssm_lam_im=d_lam_im[None], ssm_log_dt=d_log_dt.reshape(1, g_),
                       ssm_b_re=d_b_re.reshape(1, g_, p_, h_), ssm_b_im=d_b_im.reshape(1, g_, p_, h_),
                       ssm_c_re=from_cd(dcd[0])[None], ssm_c_im=-from_cd(dcd[1])[None])
            dxn = mm_rs_da(du, w_in, 0, F32, "ssm_in_da")
            g_in = mm_rs_dw(st["xn"], du, "ssm_in_dw")
        dx, d_norm[i][0] = rms_bwd(st["x"], st["g0"], dxn, dx1, "rms_bwd")
        if mixer != 0:
            token = scatter({k_in: g_in, k_out: g_out})

    rep["final_norm_g"] = d_final_g.reshape(d)
    rep["ffn_conv_b"] = jnp.stack([d_p3[l][:, CONV_K, :].reshape(NDEV * nc) for l in range(depth)])

    d_norm_full = jnp.stack([jnp.concatenate(pair, axis=0) for pair in d_norm])
    d_norm_pieces = jnp.transpose(d_norm_full.reshape(depth, 2, NDEV, d // NDEV), (2, 0, 1, 3))
    d_ssm_d_pieces = jnp.transpose(d_ssm_d.reshape(1, NDEV, d // NDEV), (1, 0, 2))
    d_conv_w_pieces = jnp.stack([d_p3[l][:, :CONV_K, :] for l in range(depth)], axis=1)
    small_pieces = _pack([d_norm_pieces, d_ssm_d_pieces, d_conv_w_pieces], lead=1)
    small_received, = all_to_all([[small_pieces]], name="scatter_small_grads")
    rep_parts, = all_gather([_pack([rep[k] for k in REPLICATED])], name="gather_small_grads")
    own, landed = {}, {}
    for keys, started in scattering.items():
        for key, (src, land) in zip(keys, scatter_wait(started, dx, "scatter_grads_wait")):
            own[key], landed[key] = src, land

    grads, deltas, new_m, new_v = {}, {}, {}, {}
    for k in BIG:
        layers = range(w[k].shape[0])
        grads[k], deltas[k], new_m[k], new_v[k] = adamw_layers(
            w[k], m[k], v[k], [landed[(k, l)] for l in layers], [own[(k, l)] for l in layers], me, "adamw")
    for names, parts in ((SMALL_SHARDED, small_received[:, 0]), (REPLICATED, rep_parts)):
        res = adamw(_pack([w[k] for k in names]), _pack([m[k] for k in names]), _pack([v[k] for k in names]), parts,
                    "adamw_small")
        for tree, buf in zip((grads, deltas, new_m, new_v), res):
            for k, val in zip(names, _unpack(buf, [w[k].shape for k in names])):
                tree[k] = val
    grad_x = dx.reshape(x.shape)
    return (loss, grad_x, *[grads[k] for k in WEIGHTS], *[deltas[k] for k in WEIGHTS],
            *[new_m[k] for k in WEIGHTS], *[new_v[k] for k in WEIGHTS])


def kernel(x, norm_g, final_norm_g, sb_w_qkv, sb_w_o, sg_w_in, sg_norm_g, sg_w_s, sg_b, sg_w_o, ssm_w_in, ssm_lam_re, ssm_lam_im, ssm_log_dt, ssm_b_re, ssm_b_im, ssm_c_re, ssm_c_im, ssm_d, ssm_w_glu, ffn_w_up, ffn_conv_w, ffn_conv_b, ffn_w_down, loss_target, m_norm_g, m_final_norm_g, m_sb_w_qkv, m_sb_w_o, m_sg_w_in, m_sg_norm_g, m_sg_w_s, m_sg_b, m_sg_w_o, m_ssm_w_in, m_ssm_lam_re, m_ssm_lam_im, m_ssm_log_dt, m_ssm_b_re, m_ssm_b_im, m_ssm_c_re, m_ssm_c_im, m_ssm_d, m_ssm_w_glu, m_ffn_w_up, m_ffn_conv_w, m_ffn_conv_b, m_ffn_w_down, v_norm_g, v_final_norm_g, v_sb_w_qkv, v_sb_w_o, v_sg_w_in, v_sg_norm_g, v_sg_w_s, v_sg_b, v_sg_w_o, v_ssm_w_in, v_ssm_lam_re, v_ssm_lam_im, v_ssm_log_dt, v_ssm_b_re, v_ssm_b_im, v_ssm_c_re, v_ssm_c_im, v_ssm_d, v_ssm_w_glu, v_ffn_w_up, v_ffn_conv_w, v_ffn_conv_b, v_ffn_w_down):
    w = dict(zip(WEIGHTS, (norm_g, final_norm_g, sb_w_qkv, sb_w_o, sg_w_in, sg_norm_g, sg_w_s, sg_b, sg_w_o, ssm_w_in,
                           ssm_lam_re, ssm_lam_im, ssm_log_dt, ssm_b_re, ssm_b_im, ssm_c_re, ssm_c_im, ssm_d, ssm_w_glu,
                           ffn_w_up, ffn_conv_w, ffn_conv_b, ffn_w_down)))
    m = dict(zip(WEIGHTS, (m_norm_g, m_final_norm_g, m_sb_w_qkv, m_sb_w_o, m_sg_w_in, m_sg_norm_g, m_sg_w_s, m_sg_b,
                           m_sg_w_o, m_ssm_w_in, m_ssm_lam_re, m_ssm_lam_im, m_ssm_log_dt, m_ssm_b_re, m_ssm_b_im,
                           m_ssm_c_re, m_ssm_c_im, m_ssm_d, m_ssm_w_glu, m_ffn_w_up, m_ffn_conv_w, m_ffn_conv_b,
                           m_ffn_w_down)))
    v = dict(zip(WEIGHTS, (v_norm_g, v_final_norm_g, v_sb_w_qkv, v_sb_w_o, v_sg_w_in, v_sg_norm_g, v_sg_w_s, v_sg_b,
                           v_sg_w_o, v_ssm_w_in, v_ssm_lam_re, v_ssm_lam_im, v_ssm_log_dt, v_ssm_b_re, v_ssm_b_im,
                           v_ssm_c_re, v_ssm_c_im, v_ssm_d, v_ssm_w_glu, v_ffn_w_up, v_ffn_conv_w, v_ffn_conv_b,
                           v_ffn_w_down)))
    return _step(x, loss_target, w, m, v)
```

```python
import functools

import jax
import jax.numpy as jnp
from jax import lax
from jax.experimental import pallas as pl
from jax.experimental.pallas import tpu as pltpu

F32, BF16 = jnp.float32, jnp.bfloat16
MESH_AXES = ("x", "y", "c")
NDEV = 8
EPS = 1e-6
HEAD_DIM = 64
LANES = 128
ATT_BQ, ATT_BK = 512, 256
CHUNK = 128
SG_GROUPS = 8
SSM_GROUPS, SSM_STATE, SSM_GROUP = 64, 64, 16
SSM_PACK = 8
S5_PASSES = 1
CONV_K = 3
HALO = 16
ROW_BLOCK = 512
SCAN_COLS = 256
ADAM_LR, ADAM_B1, ADAM_B2, ADAM_EPS, ADAM_WD, ADAM_STEP = 0.001, 0.9, 0.999, 1e-08, 0.01, 10
VMEM_LIMIT = 56 * 1024 * 1024

_MM = (((1,), (0,)), ((), ()))
_MM_TB = (((1,), (1,)), ((), ()))
_MM_TA = (((0,), (0,)), ((), ()))


def _params(sem):
    return pltpu.CompilerParams(dimension_semantics=sem, vmem_limit_bytes=VMEM_LIMIT)


def _rows(total, cap, mult=16):
    best = None
    for d in range(mult, min(total, cap) + 1, mult):
        if total % d == 0:
            best = d
    return best if best is not None else total


def _dot(a, b, dims, passes):
    if passes == 1:
        return lax.dot_general(a.astype(BF16), b.astype(BF16), dims, preferred_element_type=F32)
    a = a.astype(F32)
    b = b.astype(F32)
    ah = a.astype(BF16)
    bh = b.astype(BF16)
    al = (a - ah.astype(F32)).astype(BF16)
    bl = (b - bh.astype(F32)).astype(BF16)
    out = lax.dot_general(ah, bh, dims, preferred_element_type=F32)
    out = out + lax.dot_general(al, bh, dims, preferred_element_type=F32)
    return out + lax.dot_general(ah, bl, dims, preferred_element_type=F32)


def _mm(a, b, *, grid, a_blk, a_map, b_blk, b_map, o_blk, o_map, out_shape, out_dtype, name,
        dims=_MM, passes=1, res=None, res_blk=None, res_map=None, b_2d=None, acc_2d=None, dep=None):
    nk = grid[2]
    has_res = res is not None
    a_maps = list(a_map) if isinstance(a_map, (list, tuple)) else [a_map]
    b_maps = list(b_map) if isinstance(b_map, (list, tuple)) else [b_map]
    na, nb = len(a_maps), len(b_maps)
    n_in = na + nb + has_res + (dep is not None)

    def body(*refs):
        o_ref = refs[n_in]
        r_ref = refs[na + nb] if has_res else None
        av = refs[0][...] if na == 1 else jnp.concatenate([r[...] for r in refs[:na]], axis=-1)
        bv = refs[na][...] if nb == 1 else jnp.concatenate([r[...] for r in refs[na:na + nb]], axis=-1)
        if b_2d is not None:
            bv = bv.reshape(b_2d)
        part = _dot(av, bv, dims, passes)

        def finish(total):
            if has_res:
                total = total + r_ref[...].astype(F32)
            o_ref[...] = total.reshape(o_ref.shape).astype(o_ref.dtype)

        if nk == 1:
            finish(part)
        else:
            acc_ref = refs[-1]
            k = pl.program_id(2)

            @pl.when(k == 0)
            def _():
                acc_ref[...] = part

            @pl.when(k > 0)
            def _():
                acc_ref[...] += part

            @pl.when(k == nk - 1)
            def _():
                finish(acc_ref[...])

    in_specs = [pl.BlockSpec(a_blk, f) for f in a_maps] + [pl.BlockSpec(b_blk, f) for f in b_maps]
    args = [a] * na + [b] * nb
    if has_res:
        in_specs.append(pl.BlockSpec(res_blk, res_map))
        args.append(res)
    if dep is not None:
        in_specs.append(pl.BlockSpec(memory_space=pl.ANY))
        args.append(dep)
    scratch = [pltpu.VMEM(acc_2d, F32)] if nk > 1 else []
    return pl.pallas_call(
        body, grid=grid, in_specs=in_specs, out_specs=pl.BlockSpec(o_blk, o_map),
        out_shape=jax.ShapeDtypeStruct(out_shape, out_dtype), scratch_shapes=scratch, name=name,
        compiler_params=_params(("parallel", "parallel", "arbitrary")),
    )(*args)


def _cs_act_spec(ns, tm, row_of, col_of):
    if ns % LANES == 0:
        return (tm, ns), lambda *g: (row_of(*g), col_of(*g))
    return (None, tm, ns), lambda *g: (col_of(*g), row_of(*g), 0)


def mm_cs_fwd(a, w4, l, out_dtype, name, tm=2048):
    m, k = a.shape
    tm = min(tm, m)
    ns = w4.shape[3]
    o_blk, o_map = _cs_act_spec(ns, tm, lambda j, i, kk: i, lambda j, i, kk: j)
    out_shape = (m, NDEV * ns) if ns % LANES == 0 else (NDEV, m, ns)
    return _mm(a, w4, grid=(NDEV, m // tm, 1), a_blk=(tm, k), a_map=lambda j, i, kk: (i, 0),
               b_blk=(None, None, k, ns), b_map=lambda j, i, kk: (j, l, 0, 0),
               o_blk=o_blk, o_map=o_map, out_shape=out_shape, out_dtype=out_dtype, name=name)


def mm_cs_da(dc, w4, l, m, name, tm=1024):
    k, ns = w4.shape[2], w4.shape[3]
    tm = min(tm, m)
    a_blk, a_map = _cs_act_spec(ns, tm, lambda i, _, j: i, lambda i, _, j: j)
    return _mm(dc, w4, grid=(m // tm, 1, NDEV), a_blk=a_blk, a_map=a_map,
               b_blk=(None, None, k, ns), b_map=lambda i, _, j: (j, l, 0, 0),
               o_blk=(tm, k), o_map=lambda i, _, j: (i, 0), out_shape=(m, k), out_dtype=F32,
               dims=_MM_TB, acc_2d=(tm, k), name=name)


def mm_cs_dw(a, dc, ns, name, tk=2048):
    m, k = a.shape
    tk = min(tk, m)
    b_blk, b_map = _cs_act_spec(ns, tk, lambda j, _, kk: kk, lambda j, _, kk: j)
    return _mm(a, dc, grid=(NDEV, 1, m // tk), a_blk=(tk, k), a_map=lambda j, _, kk: (kk, 0),
               b_blk=b_blk, b_map=b_map, o_blk=(None, k, ns), o_map=lambda j, _, kk: (j, 0, 0),
               out_shape=(NDEV, k, ns), out_dtype=BF16, dims=_MM_TA, acc_2d=(k, ns), name=name)


def mm_rs_fwd(a, w4, l, res, out_dtype, name, tm=1024):
    m, k = a.shape
    tm = min(tm, m)
    ks, n = w4.shape[2], w4.shape[3]
    return _mm(a, w4, grid=(m // tm, 1, 1), a_blk=(tm, k), a_map=lambda i, _, kk: (i, 0),
               b_blk=(NDEV, None, ks, n), b_map=lambda i, _, kk: (0, l, 0, 0), b_2d=(k, n),
               o_blk=(tm, n), o_map=lambda i, _, kk: (i, 0), out_shape=(m, n), out_dtype=out_dtype,
               res=res, res_blk=(tm, n), res_map=lambda i, _, kk: (i, 0), name=name)


def mm_rs_da(dc, w4, l, out_dtype, name, tm=1024, dep=None):
    m, n = dc.shape
    tm = min(tm, m)
    ks = w4.shape[2]
    k = NDEV * ks
    return _mm(dc, w4, grid=(m // tm, 1, 1), a_blk=(tm, n), a_map=lambda i, _, kk: (i, 0),
               b_blk=(NDEV, None, ks, n), b_map=lambda i, _, kk: (0, l, 0, 0), b_2d=(k, n),
               o_blk=(tm, k), o_map=lambda i, _, kk: (i, 0), out_shape=(m, k), out_dtype=out_dtype,
               dims=_MM_TB, name=name, dep=dep)


def mm_rs_dw(a, dc, name, tk=1024):
    m, k = a.shape
    tk = min(tk, m)
    n = dc.shape[1]
    ks = k // NDEV
    return _mm(a, dc, grid=(1, 1, m // tk), a_blk=(tk, k), a_map=lambda _, __, kk: (kk, 0),
               b_blk=(tk, n), b_map=lambda _, __, kk: (kk, 0),
               o_blk=(NDEV, ks, n), o_map=lambda _, __, kk: (0, 0, 0), out_shape=(NDEV, ks, n),
               out_dtype=BF16, dims=_MM_TA, acc_2d=(k, n), name=name)


def mm_down_fwd(a3, w4, l, res, name, tm=1024):
    nj, m, kc = a3.shape
    tm = min(tm, m)
    ks, n = w4.shape[2], w4.shape[3]
    return _mm(a3, w4, grid=(m // tm, 1, nj), a_blk=(None, tm, kc), a_map=lambda i, _, j: (j, i, 0),
               b_blk=(2, None, ks, n), b_map=lambda i, _, j: (j, l, 0, 0), b_2d=(kc, n),
               o_blk=(tm, n), o_map=lambda i, _, j: (i, 0), out_shape=(m, n), out_dtype=F32,
               res=res, res_blk=(tm, n), res_map=lambda i, _, j: (i, 0), acc_2d=(tm, n), name=name)


def mm_down_da(dc, w4, l, name, tm=2048, dep=None):
    m, n = dc.shape
    tm = min(tm, m)
    ks = w4.shape[2]
    kc = 2 * ks
    nj = NDEV // 2
    return _mm(dc, w4, grid=(nj, m // tm, 1), a_blk=(tm, n), a_map=lambda j, i, _: (i, 0),
               b_blk=(2, None, ks, n), b_map=lambda j, i, _: (j, l, 0, 0), b_2d=(kc, n),
               o_blk=(None, tm, kc), o_map=lambda j, i, _: (j, i, 0), out_shape=(nj, m, kc),
               out_dtype=BF16, dims=_MM_TB, name=name, dep=dep)


def mm_down_dw(a3, dc, name, tk=2048):
    nj, m, kc = a3.shape
    tk = min(tk, m)
    n = dc.shape[1]
    return _mm(a3, dc, grid=(nj, 1, m // tk), a_blk=(None, tk, kc), a_map=lambda j, _, kk: (j, kk, 0),
               b_blk=(tk, n), b_map=lambda j, _, kk: (kk, 0),
               o_blk=(2, kc // 2, n), o_map=lambda j, _, kk: (j, 0, 0), out_shape=(NDEV, kc // 2, n),
               out_dtype=BF16, dims=_MM_TA, acc_2d=(kc, n), name=name)


def _qkv_group_maps(d, ns, row_of, piece_of):
    per_arr, per_piece = d // LANES, ns // LANES

    def group_map(q):
        def f(*g):
            grp = piece_of(*g) * per_piece + q
            return grp // per_arr, row_of(*g), grp % per_arr
        return f

    return [group_map(q) for q in range(per_piece)]


def mm_qkv_da(d3, w4, l, name, tm=1024, dep=None):
    _, m, d = d3.shape
    tm = min(tm, m)
    k, ns = w4.shape[2], w4.shape[3]
    return _mm(d3, w4, grid=(m // tm, 1, NDEV),
               a_blk=(None, tm, LANES), a_map=_qkv_group_maps(d, ns, lambda i, _, j: i, lambda i, _, j: j),
               b_blk=(None, None, k, ns), b_map=lambda i, _, j: (j, l, 0, 0),
               o_blk=(tm, k), o_map=lambda i, _, j: (i, 0), out_shape=(m, k), out_dtype=F32,
               dims=_MM_TB, acc_2d=(tm, k), name=name, dep=dep)


def mm_qkv_dw(a, d3, ns, name, tk=2048):
    m, k = a.shape
    tk = min(tk, m)
    d = d3.shape[2]
    return _mm(a, d3, grid=(NDEV, 1, m // tk), a_blk=(tk, k), a_map=lambda j, _, kk: (kk, 0),
               b_blk=(None, tk, LANES), b_map=_qkv_group_maps(d, ns, lambda j, _, kk: kk, lambda j, _, kk: j),
               o_blk=(None, k, ns), o_map=lambda j, _, kk: (j, 0, 0),
               out_shape=(NDEV, k, ns), out_dtype=BF16, dims=_MM_TA, acc_2d=(k, ns), name=name)


def _rowwise(fn, ins, outs, *, tr, name, acc_outs=()):
    rows = next(a.shape[0] if kind == "row" else a.shape[1] for a, kind in ins if kind in ("row", "row3"))
    n_in, n_out = len(ins), len(outs)
    n_read = sum(kind != "dep" for _, kind in ins)

    def body(*refs):
        vals = fn(*[r[...] for r in refs[:n_read]])
        if not isinstance(vals, (tuple, list)):
            vals = (vals,)
        for ref, val in zip(refs[n_in:n_in + n_out], vals[:n_out]):
            ref[...] = val.astype(ref.dtype)
        i = pl.program_id(0)
        for ref, val in zip(refs[n_in + n_out:], vals[n_out:]):
            val = val.astype(ref.dtype)

            @pl.when(i == 0)
            def _(ref=ref, val=val):
                ref[...] = val

            @pl.when(i > 0)
            def _(ref=ref, val=val):
                ref[...] += val

    in_specs = []
    for a, kind in ins:
        if kind == "row":
            in_specs.append(pl.BlockSpec((tr, a.shape[1]), lambda i: (i, 0)))
        elif kind == "row3":
            in_specs.append(pl.BlockSpec((a.shape[0], tr, a.shape[2]), lambda i: (0, i, 0)))
        elif kind == "dep":
            in_specs.append(pl.BlockSpec(memory_space=pl.ANY))
        else:
            in_specs.append(pl.BlockSpec(a.shape, lambda i, nd=a.ndim: (0,) * nd))
    out_specs = [pl.BlockSpec((tr, c), lambda i: (i, 0)) for c, _ in outs]
    out_specs += [pl.BlockSpec(s, lambda i, nd=len(s): (0,) * nd) for s, _ in acc_outs]
    out_shape = [jax.ShapeDtypeStruct((rows, c), dt) for c, dt in outs]
    out_shape += [jax.ShapeDtypeStruct(s, dt) for s, dt in acc_outs]
    res = pl.pallas_call(
        body, grid=(rows // tr,), in_specs=in_specs, out_specs=out_specs, out_shape=out_shape, name=name,
        compiler_params=_params(("arbitrary",) if acc_outs else ("parallel",)),
    )(*[a for a, _ in ins])
    return res


def _rms(x, g):
    return x * lax.rsqrt(jnp.mean(x * x, axis=-1, keepdims=True) + EPS) * g


def cast_into_slot(w, l, me):
    _, r, c = w.shape
    tr = _rows(r, 512)

    def body(me_ref, w_ref, o_ref):
        o_ref[...] = w_ref[...].astype(o_ref.dtype)

    return pl.pallas_call(
        body,
        grid_spec=pltpu.PrefetchScalarGridSpec(
            num_scalar_prefetch=1, grid=(r // tr,),
            in_specs=[pl.BlockSpec((None, tr, c), lambda i, me_ref: (l, i, 0))],
            out_specs=pl.BlockSpec((None, tr, c), lambda i, me_ref: (me_ref[0], i, 0))),
        out_shape=jax.ShapeDtypeStruct((NDEV, r, c), BF16), name="cast_into_slot",
        compiler_params=_params(("parallel",)),
    )(me, w)


def rms_fwd(x, g, name):
    out, = _rowwise(_rms, [(x, "row"), (g, "full")], [(x.shape[1], BF16)], tr=ROW_BLOCK, name=name)
    return out


def rms_bwd(x, g, dy, dres, name):
    def fn(xv, gv, dyv, drv):
        _, vjp = jax.vjp(_rms, xv, gv)
        dx, dg = vjp(dyv.astype(F32))
        return drv + dx, dg

    d = x.shape[1]
    return _rowwise(fn, [(x, "row"), (g, "full"), (dy, "row"), (dres, "row")], [(d, F32)], tr=ROW_BLOCK, name=name,
                    acc_outs=[((1, d), F32)])


def loss_head(x, g, tgt):
    def f(xv, gv, tv):
        err = jnp.square(_rms(xv, gv) - tv)
        return 0.5 * jnp.sum(jnp.mean(err, axis=-1))

    def fn(xv, gv, tv):
        val, (dx, dg) = jax.value_and_grad(f, argnums=(0, 1))(xv, gv, tv)
        return dx, jnp.full((1, LANES), val, F32), dg

    d = x.shape[1]
    return _rowwise(fn, [(x, "row"), (g, "full"), (tgt, "row")], [(d, F32)], tr=ROW_BLOCK, name="loss_head",
                    acc_outs=[((1, LANES), F32), ((1, d), F32)])


def _glu(hg, x):
    half = hg.shape[1] // 2
    return x + hg[:, :half] * jax.nn.sigmoid(hg[:, half:])


def glu_fwd(hg, x):
    out, = _rowwise(lambda h, xv: _glu(h.astype(F32), xv), [(hg, "row"), (x, "row")], [(x.shape[1], F32)],
                    tr=ROW_BLOCK, name="glu_fwd")
    return out


def glu_bwd(hg, dx1, dep):
    def fn(h, d):
        _, vjp = jax.vjp(lambda hv: _glu(hv, jnp.zeros_like(d)), h.astype(F32))
        return vjp(d)[0]

    out, = _rowwise(fn, [(hg, "row"), (dx1, "row"), (dep, "dep")], [(hg.shape[1], BF16)], tr=ROW_BLOCK, name="glu_bwd")
    return out


def _s5_post(yc, u, d):
    return jax.nn.gelu(yc + d * u)


def s5_post_fwd(yc, u, d):
    out, = _rowwise(_s5_post, [(yc, "row"), (u, "row"), (d, "full")], [(yc.shape[1], BF16)], tr=ROW_BLOCK,
                    name="s5_post_fwd")
    return out


def s5_post_bwd(yc, u, d, dyg):
    def fn(ycv, uv, dv, g):
        _, vjp = jax.vjp(_s5_post, ycv, uv, dv)
        return vjp(g.astype(F32))

    dm = yc.shape[1]
    return _rowwise(fn, [(yc, "row"), (u, "row"), (d, "full"), (dyg, "row")], [(dm, F32), (dm, F32)], tr=ROW_BLOCK,
                    name="s5_post_bwd", acc_outs=[((1, dm), F32)])


def _adam_update(wv, mv, vv, g):
    m2 = ADAM_B1 * mv + (1.0 - ADAM_B1) * g
    v2 = ADAM_B2 * vv + (1.0 - ADAM_B2) * jnp.square(g)
    m_hat = m2 / (1.0 - ADAM_B1 ** ADAM_STEP)
    v_hat = v2 / (1.0 - ADAM_B2 ** ADAM_STEP)
    delta = -ADAM_LR * (m_hat / (jnp.sqrt(v_hat) + ADAM_EPS) + ADAM_WD * wv)
    return g, delta, m2, v2


def adamw(w, m, v, g_parts, name):
    def fn(wv, mv, vv, gp):
        g = gp[0].astype(F32)
        for p in range(1, gp.shape[0]):
            g = g + gp[p].astype(F32)
        return _adam_update(wv, mv, vv, g)

    c = w.shape[1]
    return _rowwise(fn, [(w, "row"), (m, "row"), (v, "row"), (g_parts, "row3")], [(c, F32)] * 4,
                    tr=_rows(w.shape[0], 256), name=name)


def adamw_layers(w, m, v, lands, owns, me, name):
    nl, r, c = w.shape
    tr = _rows(r, 256)

    def body(me_ref, w_ref, m_ref, v_ref, *rest):
        land_refs, own_refs, out_refs = rest[:nl], rest[nl:2 * nl], rest[2 * nl:]
        for l in range(nl):
            @pl.when(pl.program_id(0) == l)
            def _(l=l):
                g = own_refs[l][...].astype(F32)
                for p in range(NDEV - 1):
                    g = g + land_refs[l][p].astype(F32)
                for ref, val in zip(out_refs, _adam_update(w_ref[...], m_ref[...], v_ref[...], g)):
                    ref[...] = val

    def rows_of(l):
        return lambda li, i, me_ref: jnp.where(li == l, i, 0)

    wspec = pl.BlockSpec((None, tr, c), lambda li, i, me_ref: (li, i, 0))
    in_specs = [wspec] * 3
    in_specs += [pl.BlockSpec((NDEV - 1, tr, c), lambda li, i, me_ref, f=rows_of(l): (0, f(li, i, me_ref), 0))
                 for l in range(nl)]
    in_specs += [pl.BlockSpec((None, tr, c), lambda li, i, me_ref, f=rows_of(l): (me_ref[0], f(li, i, me_ref), 0))
                 for l in range(nl)]
    return pl.pallas_call(
        body,
        grid_spec=pltpu.PrefetchScalarGridSpec(
            num_scalar_prefetch=1, grid=(nl, r // tr), in_specs=in_specs, out_specs=[wspec] * 4),
        out_shape=[jax.ShapeDtypeStruct(w.shape, F32)] * 4, name=name, compiler_params=_params(("parallel", "parallel")),
    )(me, w, m, v, *lands, *owns)


def _conv_rows(cur, halo, p, first):
    r = cur.shape[0]
    ext = jnp.concatenate([jnp.where(first, 0.0, halo), cur], axis=0)
    s1 = pltpu.roll(ext, 1, 0)[HALO:]
    s2 = pltpu.roll(ext, 2, 0)[HALO:]
    return p[0:1] * s2 + p[1:2] * s1 + p[2:3] * cur + p[3:4], s1, s2


def ffn_gate_fwd(h3, p3, tr=ROW_BLOCK):
    _, t, c = h3.shape
    half = NDEV // 2

    def body(a_ref, ah_ref, g_ref, gh_ref, pa_ref, pg_ref, o_ref):
        first = pl.program_id(1) == 0
        ya, _, _ = _conv_rows(a_ref[...].astype(F32), ah_ref[...].astype(F32), pa_ref[...], first)
        yg, _, _ = _conv_rows(g_ref[...].astype(F32), gh_ref[...].astype(F32), pg_ref[...], first)
        o_ref[...] = (jax.nn.silu(yg) * ya).astype(o_ref.dtype)

    main = lambda off: pl.BlockSpec((None, tr, c), lambda j, i: (j + off, i, 0))
    halo = lambda off: pl.BlockSpec((None, HALO, c), lambda j, i: (j + off, jnp.maximum(i * (tr // HALO) - 1, 0), 0))
    par = lambda off: pl.BlockSpec((None, 8, c), lambda j, i: (j + off, 0, 0))
    return pl.pallas_call(
        body, grid=(half, t // tr),
        in_specs=[main(0), halo(0), main(half), halo(half), par(0), par(half)],
        out_specs=pl.BlockSpec((None, tr, c), lambda j, i: (j, i, 0)),
        out_shape=jax.ShapeDtypeStruct((half, t, c), BF16), name="ffn_gate_fwd",
        compiler_params=_params(("parallel", "parallel")),
    )(h3, h3, h3, h3, p3, p3)


def ffn_gate_bwd(h3, dgated3, p3, tr=ROW_BLOCK):
    _, t, c = h3.shape
    half = NDEV // 2

    def body(a_ref, ah_ref, g_ref, gh_ref, dg_ref, pa_ref, pg_ref, dya_ref, dyg_ref, dpa_ref, dpg_ref):
        i = pl.program_id(1)
        first = i == 0
        a = a_ref[...].astype(F32)
        g = g_ref[...].astype(F32)
        ya, a1, a2 = _conv_rows(a, ah_ref[...].astype(F32), pa_ref[...], first)
        yg, g1, g2 = _conv_rows(g, gh_ref[...].astype(F32), pg_ref[...], first)
        d = dg_ref[...].astype(F32)
        sig = jax.nn.sigmoid(yg)
        d_ya = (d * (yg * sig)).astype(dya_ref.dtype)
        d_yg = (d * ya * (sig * (1.0 + yg * (1.0 - sig)))).astype(dyg_ref.dtype)
        dya_ref[...] = d_ya
        dyg_ref[...] = d_yg
        for dy, cur, s1, s2, dp_ref in ((d_ya.astype(F32), a, a1, a2, dpa_ref), (d_yg.astype(F32), g, g1, g2, dpg_ref)):
            rows = [jnp.sum(dy * s2, axis=0, keepdims=True), jnp.sum(dy * s1, axis=0, keepdims=True),
                    jnp.sum(dy * cur, axis=0, keepdims=True), jnp.sum(dy, axis=0, keepdims=True)]
            dp = jnp.concatenate(rows + [jnp.zeros((4, c), F32)], axis=0)

            @pl.when(first)
            def _(dp_ref=dp_ref, dp=dp):
                dp_ref[...] = dp

            @pl.when(i > 0)
            def _(dp_ref=dp_ref, dp=dp):
                dp_ref[...] += dp

    main = lambda off: pl.BlockSpec((None, tr, c), lambda j, i: (j + off, i, 0))
    halo = lambda off: pl.BlockSpec((None, HALO, c), lambda j, i: (j + off, jnp.maximum(i * (tr // HALO) - 1, 0), 0))
    par = lambda off: pl.BlockSpec((None, 8, c), lambda j, i: (j + off, 0, 0))
    return pl.pallas_call(
        body, grid=(half, t // tr),
        in_specs=[main(0), halo(0), main(half), halo(half), main(0), par(0), par(half)],
        out_specs=[main(0), main(0), par(0), par(0)],
        out_shape=[jax.ShapeDtypeStruct((half, t, c), BF16)] * 2 + [jax.ShapeDtypeStruct((half, 8, c), F32)] * 2,
        name="ffn_gate_bwd", compiler_params=_params(("parallel", "arbitrary")),
    )(h3, h3, h3, h3, dgated3, p3, p3)


def ffn_conv_t(dy_a, dy_g, p3, tr=2 * ROW_BLOCK):
    half, t, c = dy_a.shape
    tr = min(tr, t)
    nblk = t // tr

    def body(a_ref, ah_ref, g_ref, gh_ref, p_ref, o_ref):
        is_a = pl.program_id(0) < half
        last = pl.program_id(1) == nblk - 1
        cur = jnp.where(is_a, a_ref[...], g_ref[...]).astype(F32)
        nxt = jnp.where(is_a, ah_ref[...], gh_ref[...]).astype(F32)
        ext = jnp.concatenate([cur, jnp.where(last, 0.0, nxt)], axis=0)
        n = tr + HALO
        s1 = pltpu.roll(ext, n - 1, 0)[:tr]
        s2 = pltpu.roll(ext, n - 2, 0)[:tr]
        p = p_ref[...]
        o_ref[...] = (p[2:3] * cur + p[1:2] * s1 + p[0:1] * s2).astype(o_ref.dtype)

    main = pl.BlockSpec((None, tr, c), lambda j, i: (j % half, i, 0))
    halo = pl.BlockSpec((None, HALO, c), lambda j, i: (j % half, jnp.minimum((i + 1) * (tr // HALO), t // HALO - 1), 0))
    return pl.pallas_call(
        body, grid=(NDEV, nblk),
        in_specs=[main, halo, main, halo, pl.BlockSpec((None, 8, c), lambda j, i: (j, 0, 0))],
        out_specs=pl.BlockSpec((None, tr, c), lambda j, i: (j, i, 0)),
        out_shape=jax.ShapeDtypeStruct((NDEV, t, c), BF16), name="ffn_conv_t",
        compiler_params=_params(("parallel", "parallel")),
    )(dy_a, dy_a, dy_g, dy_g, p3)


def _att_consts(bq, bk):
    lane = lax.broadcasted_iota(jnp.int32, (1, LANES), 1)
    heads = (lane < HEAD_DIM, lane >= HEAD_DIM)
    rr = lax.broadcasted_iota(jnp.int32, (bq, bk), 0)
    cc = lax.broadcasted_iota(jnp.int32, (bq, bk), 1)
    kr = lax.broadcasted_iota(jnp.int32, (bk, bk), 0)
    kc = lax.broadcasted_iota(jnp.int32, (bk, bk), 1)
    return heads, rr, cc, kr, kc


def _split_dot(x, tri, parts):
    out = None
    for _ in range(parts):
        piece = x.astype(BF16)
        x = x - piece.astype(F32)
        term = jnp.dot(piece, tri, preferred_element_type=F32)
        out = term if out is None else out + term
    return out


def _att_logits(qh, k):
    z = lax.dot_general(qh, k, _MM_TB, preferred_element_type=F32)
    lsp = jnp.minimum(z, 0.0) - jnp.log(1.0 + jnp.exp(-jnp.abs(z)))
    return lsp, lsp - z


def _per_head(heads, a, b):
    return jnp.where(heads[0], a, b)


def sb_attn_fwd(qkv):
    t, d3 = qkv.shape
    d = d3 // 3
    npair = d // LANES
    bq, bk = min(ATT_BQ, t), min(ATT_BK, t)
    kpq = bq // bk

    def body(q_ref, k_ref, v_ref, o_ref, lt_ref, acc_ref):
        heads, rr, cc, kr, kc = _att_consts(bq, bk)
        suffix = (kr > kc).astype(BF16)

        def trip(qh, k0, valid, runs):
            k = k_ref[pl.ds(k0, bk), :]
            v = v_ref[pl.ds(k0, bk), :]
            new_runs = []
            for h in range(2):
                lsp, lraw = _att_logits(qh[h], k)
                lm = lraw if valid is None else jnp.where(valid, lraw, 0.0)
                w = jnp.exp(lsp + _split_dot(lm, suffix, 2) + runs[h])
                if valid is not None:
                    w = jnp.where(valid, w, 0.0)
                acc_ref[h] += jnp.dot(w.astype(BF16), v, preferred_element_type=F32)
                new_runs.append(runs[h] + jnp.sum(lm, axis=1, keepdims=True))
            return tuple(new_runs)

        def q_loop(qb, _):
            q0 = pl.multiple_of(qb * bq, bq)
            q = q_ref[pl.ds(q0, bq), :] * 0.125
            qh = [jnp.where(hm, q, 0.0).astype(BF16) for hm in heads]
            acc_ref[...] = jnp.zeros_like(acc_ref)
            runs = (jnp.zeros((bq, 1), F32),) * 2
            for dblk in reversed(range(kpq)):
                runs = trip(qh, pl.multiple_of(q0 + dblk * bk, bk), dblk * bk + cc < rr, runs)
            nleft = qb * kpq
            runs = lax.fori_loop(
                0, nleft, lambda i, r: trip(qh, pl.multiple_of((nleft - 1 - i) * bk, bk), None, r), runs)
            o_ref[pl.ds(q0, bq), :] = _per_head(heads, acc_ref[0], acc_ref[1])
            lt_ref[pl.ds(q0, bq), :] = _per_head(heads, runs[0], runs[1])
            return 0

        lax.fori_loop(0, t // bq, q_loop, 0)

    col = lambda off: pl.BlockSpec((t, LANES), lambda p: (0, p + off))
    return pl.pallas_call(
        body, grid=(npair,), in_specs=[col(0), col(npair), col(2 * npair)], out_specs=[col(0), col(0)],
        out_shape=[jax.ShapeDtypeStruct((t, d), F32)] * 2, scratch_shapes=[pltpu.VMEM((2, bq, LANES), F32)],
        name="sb_attn_fwd", compiler_params=_params(("parallel",)),
    )(qkv, qkv, qkv)


def sb_attn_bwd(qkv, ltot, do):
    t, d3 = qkv.shape
    d = d3 // 3
    npair = d // LANES
    bq, bk = min(ATT_BQ, t), min(ATT_BK, t)
    kpq = bq // bk

    def body(q_ref, k_ref, v_ref, lt_ref, do_ref, d_ref, dk_acc, dv_acc, dq_acc):
        heads, rr, cc, kr, kc = _att_consts(bq, bk)
        prefix_incl = (kr <= kc).astype(BF16)
        prefix_excl = (kr < kc).astype(BF16)
        dk_acc[...] = jnp.zeros_like(dk_acc)
        dv_acc[...] = jnp.zeros_like(dv_acc)

        def trip(qh, doh, lt, k0, valid, carry):
            lruns, gruns = carry
            k = k_ref[pl.ds(k0, bk), :]
            v = v_ref[pl.ds(k0, bk), :]
            new_lruns, new_gruns = [], []
            dk_blk = jnp.zeros((bk, LANES), F32)
            dv_blk = jnp.zeros((bk, LANES), F32)
            for h in range(2):
                lsp, lraw = _att_logits(qh[h], k)
                lm = lraw if valid is None else jnp.where(valid, lraw, 0.0)
                right = lt[h] - (lruns[h] + _split_dot(lm, prefix_incl, 2))
                w = jnp.exp(lsp + right)
                if valid is not None:
                    w = jnp.where(valid, w, 0.0)
                g = lax.dot_general(doh[h], v, _MM_TB, preferred_element_type=F32) * w
                left = gruns[h] + _split_dot(g, prefix_excl, 2)
                dz = g * jnp.exp(lraw) - jnp.exp(lsp) * left
                if valid is not None:
                    dz = jnp.where(valid, dz, 0.0)
                dz = dz.astype(BF16)
                kh = jnp.where(heads[h], k, 0.0).astype(BF16)
                dq_acc[h] += jnp.dot(dz, kh, preferred_element_type=F32)
                dk_blk = dk_blk + lax.dot_general(dz, qh[h], _MM_TA, preferred_element_type=F32)
                dv_blk = dv_blk + lax.dot_general(w.astype(BF16), doh[h], _MM_TA, preferred_element_type=F32)
                new_lruns.append(lruns[h] + jnp.sum(lm, axis=1, keepdims=True))
                new_gruns.append(gruns[h] + jnp.sum(g, axis=1, keepdims=True))
            dk_acc[pl.ds(k0, bk), :] += dk_blk
            dv_acc[pl.ds(k0, bk), :] += dv_blk
            return tuple(new_lruns), tuple(new_gruns)

        def q_loop(qb, _):
            q0 = pl.multiple_of(qb * bq, bq)
            q = q_ref[pl.ds(q0, bq), :] * 0.125
            dout = do_ref[pl.ds(q0, bq), :]
            lt2 = lt_ref[pl.ds(q0, bq), :]
            qh = [jnp.where(hm, q, 0.0).astype(BF16) for hm in heads]
            doh = [jnp.where(hm, dout, 0.0).astype(BF16) for hm in heads]
            lt = [jnp.max(jnp.where(hm, lt2, -jnp.inf), axis=1, keepdims=True) for hm in heads]
            dq_acc[...] = jnp.zeros_like(dq_acc)
            col = (jnp.zeros((bq, 1), F32),) * 2
            carry = lax.fori_loop(
                0, qb * kpq, lambda kb, c: trip(qh, doh, lt, pl.multiple_of(kb * bk, bk), None, c), (col, col))
            for dblk in range(kpq):
                carry = trip(qh, doh, lt, pl.multiple_of(q0 + dblk * bk, bk), dblk * bk + cc < rr, carry)
            d_ref[0, pl.ds(q0, bq), :] = ((dq_acc[0] + dq_acc[1]) * 0.125).astype(d_ref.dtype)
            return 0

        lax.fori_loop(0, t // bq, q_loop, 0)
        d_ref[1] = dk_acc[...].astype(d_ref.dtype)
        d_ref[2] = dv_acc[...].astype(d_ref.dtype)

    col = lambda off: pl.BlockSpec((t, LANES), lambda p: (0, p + off))
    return pl.pallas_call(
        body, grid=(npair,), in_specs=[col(0), col(npair), col(2 * npair), col(0), col(0)],
        out_specs=pl.BlockSpec((3, t, LANES), lambda p: (0, 0, p)),
        out_shape=jax.ShapeDtypeStruct((3, t, d), BF16),
        scratch_shapes=[pltpu.VMEM((t, LANES), F32), pltpu.VMEM((t, LANES), F32), pltpu.VMEM((2, bq, LANES), F32)],
        name="sb_attn_bwd", compiler_params=_params(("parallel",)),
    )(qkv, qkv, qkv, ltot, do)


def _sgu_parts(hin, g, ws_ref, bf_ref):
    width = hin.shape[1] // 2
    h = jax.nn.gelu(hin)
    u, v = h[:, :width], h[:, width:]
    r = lax.rsqrt(jnp.mean(v * v, axis=-1, keepdims=True) + EPS)
    vn = v * r * g
    rr = lax.broadcasted_iota(jnp.int32, (CHUNK, CHUNK), 0)
    cc = lax.broadcasted_iota(jnp.int32, (CHUNK, CHUNK), 1)
    causal = cc <= rr
    wcs = [jnp.where(causal, ws_ref[gi], 0.0).astype(BF16) for gi in range(SG_GROUPS)]
    sv = jnp.concatenate(
        [jnp.dot(wcs[gi], vn[:, gi * CHUNK:(gi + 1) * CHUNK].astype(BF16), preferred_element_type=F32) + bf_ref[gi]
         for gi in range(SG_GROUPS)], axis=1)
    return u, v, r, vn, wcs, sv, causal


def sgu_fwd(hin, g, ws, bfull):
    t, w2 = hin.shape
    width = w2 // 2

    def body(h_ref, g_ref, ws_ref, bf_ref, o_ref):
        u, _, _, _, _, sv, _ = _sgu_parts(h_ref[...].astype(F32), g_ref[...], ws_ref, bf_ref)
        o_ref[...] = (u * sv).astype(o_ref.dtype)

    full = lambda a: pl.BlockSpec(a.shape, lambda i, nd=a.ndim: (0,) * nd)
    return pl.pallas_call(
        body, grid=(t // CHUNK,), in_specs=[pl.BlockSpec((CHUNK, w2), lambda i: (i, 0)), full(g), full(ws), full(bfull)],
        out_specs=pl.BlockSpec((CHUNK, width), lambda i: (i, 0)), out_shape=jax.ShapeDtypeStruct((t, width), BF16),
        name="sgu_fwd", compiler_params=_params(("parallel",)),
    )(hin, g, ws, bfull)


def sgu_bwd(hin, dp, g, ws, bfull):
    t, w2 = hin.shape
    width = w2 // 2

    def body(h_ref, dp_ref, g_ref, ws_ref, bf_ref, dh_ref, dws_ref, dbf_ref, dg_ref):
        i = pl.program_id(0)
        hin_v = h_ref[...].astype(F32)
        gv = g_ref[...]
        u, v, r, vn, wcs, sv, causal = _sgu_parts(hin_v, gv, ws_ref, bf_ref)
        dpv = dp_ref[...].astype(F32)
        du = dpv * sv
        dsv = dpv * u
        dvn_parts, dws_parts, dbf_parts = [], [], []
        for gi in range(SG_GROUPS):
            dsv_g = dsv[:, gi * CHUNK:(gi + 1) * CHUNK]
            dsv_b = dsv_g.astype(BF16)
            dvn_parts.append(lax.dot_general(wcs[gi], dsv_b, _MM_TA, preferred_element_type=F32))
            vn_b = vn[:, gi * CHUNK:(gi + 1) * CHUNK].astype(BF16)
            dws_parts.append(jnp.where(causal, lax.dot_general(dsv_b, vn_b, _MM_TB, preferred_element_type=F32), 0.0))
            dbf_parts.append(jnp.broadcast_to(jnp.sum(dsv_g, axis=1, keepdims=True), (CHUNK, CHUNK)))
        dvn = jnp.concatenate(dvn_parts, axis=1)
        dgain = jnp.sum(dvn * v * r, axis=0, keepdims=True)
        gvv = dvn * gv
        dv = r * gvv - v * (r * r * r) * jnp.mean(v * gvv, axis=-1, keepdims=True)
        _, vjp = jax.vjp(jax.nn.gelu, hin_v)
        dh_ref[...] = vjp(jnp.concatenate([du, dv], axis=1))[0].astype(dh_ref.dtype)

        @pl.when(i == 0)
        def _():
            for gi in range(SG_GROUPS):
                dws_ref[gi] = dws_parts[gi]
                dbf_ref[gi] = dbf_parts[gi]
            dg_ref[...] = dgain

        @pl.when(i > 0)
        def _():
            for gi in range(SG_GROUPS):
                dws_ref[gi] += dws_parts[gi]
                dbf_ref[gi] += dbf_parts[gi]
            dg_ref[...] += dgain

    full = lambda a: pl.BlockSpec(a.shape, lambda i, nd=a.ndim: (0,) * nd)
    sq = (SG_GROUPS, CHUNK, CHUNK)
    return pl.pallas_call(
        body, grid=(t // CHUNK,),
        in_specs=[pl.BlockSpec((CHUNK, w2), lambda i: (i, 0)), pl.BlockSpec((CHUNK, width), lambda i: (i, 0)),
                  full(g), full(ws), full(bfull)],
        out_specs=[pl.BlockSpec((CHUNK, w2), lambda i: (i, 0)), pl.BlockSpec(sq, lambda i: (0, 0, 0)),
                   pl.BlockSpec(sq, lambda i: (0, 0, 0)), pl.BlockSpec((1, width), lambda i: (0, 0))],
        out_shape=[jax.ShapeDtypeStruct((t, w2), BF16), jax.ShapeDtypeStruct(sq, F32), jax.ShapeDtypeStruct(sq, F32),
                   jax.ShapeDtypeStruct((1, width), F32)],
        name="sgu_bwd", compiler_params=_params(("arbitrary",)),
    )(hin, dp, g, ws, bfull)


def _disc1(lam_re, lam_im, log_dt):
    lr = jnp.minimum(lam_re, -1e-4)
    li = lam_im
    dt = jnp.exp(log_dt)
    mag = jnp.exp(dt * lr)
    ar = mag * jnp.cos(dt * li)
    ai = mag * jnp.sin(dt * li)
    den = lr * lr + li * li
    return ar, ai, ((ar - 1.0) * lr + ai * li) / den, (ai * lr - (ar - 1.0) * li) / den


def _disc2(cre, cim, b_re, b_im):
    return cre * b_re - cim * b_im, cre * b_im + cim * b_re


def _single(fn, ins, out_shapes, name):
    n = len(ins)

    def body(*refs):
        vals = fn(*[r[...] for r in refs[:n]])
        for ref, val in zip(refs[n:], vals):
            ref[...] = val

    return pl.pallas_call(body, out_shape=[jax.ShapeDtypeStruct(s, F32) for s in out_shapes], name=name)(*ins)


SCAN_SEGMENTS = 8
REORDER_STEPS = 64


def s5_reorder(x, to_steps):
    t, d = x.shape
    ns = SCAN_SEGMENTS
    seg = t // ns
    ts = min(REORDER_STEPS, seg)
    by_segment = ((ns, seg, d), pl.BlockSpec((ns, ts, d), lambda i: (0, i, 0)))
    by_step = ((seg, ns, d), pl.BlockSpec((ts, ns, d), lambda i: (i, 0, 0)))
    (in_shape, in_spec), (out_shape, out_spec) = (by_segment, by_step) if to_steps else (by_step, by_segment)

    def body(x_ref, o_ref):
        o_ref[...] = jnp.swapaxes(x_ref[...], 0, 1)

    out = pl.pallas_call(
        body, grid=(seg // ts,), in_specs=[in_spec], out_specs=out_spec,
        out_shape=jax.ShapeDtypeStruct(out_shape, x.dtype), name="s5_reorder", compiler_params=_params(("parallel",)),
    )(x.reshape(in_shape))
    return out.reshape(t, d)


def _cpow(ar, ai, n):
    rr, ri = None, None
    while n:
        if n & 1:
            rr, ri = (ar, ai) if rr is None else (rr * ar - ri * ai, rr * ai + ri * ar)
        ar, ai = ar * ar - ai * ai, 2.0 * ar * ai
        n >>= 1
    return rr, ri


def _edge_states(er, ei, pr, pi, reverse):
    ns = SCAN_SEGMENTS
    zero = jnp.zeros_like(er[0:1])
    rows_r, rows_i = [None] * ns, [None] * ns
    order = range(ns - 1, -1, -1) if reverse else range(ns)
    prev = None
    for s in order:
        if prev is None:
            rows_r[s], rows_i[s] = zero, zero
        else:
            cr, ci = rows_r[prev], rows_i[prev]
            rows_r[s] = er[prev:prev + 1] + pr * cr - pi * ci
            rows_i[s] = ei[prev:prev + 1] + pr * ci + pi * cr
        prev = s
    return jnp.concatenate(rows_r, axis=0), jnp.concatenate(rows_i, axis=0)


def s5_scan_fwd(bu2, a2):
    _, t, n = bu2.shape
    cb, ns = SCAN_COLS, SCAN_SEGMENTS
    seg = t // ns

    def body(bu_ref, a_ref, x_ref):
        ar, ai = a_ref[0:1, :], a_ref[1:2, :]

        def local(i, carry):
            xr, xi = carry
            xr, xi = ar * xr - ai * xi + bu_ref[0, i], ar * xi + ai * xr + bu_ref[1, i]
            x_ref[0, i] = xr
            x_ref[1, i] = xi
            return xr, xi

        zero = jnp.zeros((ns, cb), F32)
        er, ei = lax.fori_loop(0, seg, local, (zero, zero))
        cr, ci = _edge_states(er, ei, *_cpow(ar, ai, seg), reverse=False)

        def fix(i, carry):
            wr, wi = carry
            wr, wi = wr * ar - wi * ai, wr * ai + wi * ar
            x_ref[0, i] += wr * cr - wi * ci
            x_ref[1, i] += wr * ci + wi * cr
            return wr, wi

        lax.fori_loop(0, seg, fix, (jnp.ones((1, cb), F32), jnp.zeros((1, cb), F32)))

    blk = pl.BlockSpec((2, seg, ns, cb), lambda j: (0, 0, 0, j))
    out = pl.pallas_call(
        body, grid=(n // cb,), in_specs=[blk, pl.BlockSpec((2, cb), lambda j: (0, j))], out_specs=blk,
        out_shape=jax.ShapeDtypeStruct((2, seg, ns, n), F32), name="s5_scan_fwd", compiler_params=_params(("parallel",)),
    )(bu2.reshape(2, seg, ns, n), a2)
    return out.reshape(2, t, n)


def s5_scan_bwd(dx2, x2, a2):
    _, t, n = dx2.shape
    cb, ns = SCAN_COLS, SCAN_SEGMENTS
    seg = t // ns

    def body(dx_ref, x_ref, a_ref, g_ref, da_ref):
        ar, ai = a_ref[0:1, :], a_ref[1:2, :]

        def local(s, carry):
            gr, gi = carry
            i = seg - 1 - s
            gr, gi = dx_ref[0, i] + ar * gr + ai * gi, dx_ref[1, i] - ai * gr + ar * gi
            g_ref[0, i] = gr
            g_ref[1, i] = gi
            return gr, gi

        zero = jnp.zeros((ns, cb), F32)
        er, ei = lax.fori_loop(0, seg, local, (zero, zero))
        cr, ci = _edge_states(er, ei, *_cpow(ar, -ai, seg), reverse=True)
        row = lax.broadcasted_iota(jnp.int32, (ns, cb), 0)
        before_r = jnp.where(row == 0, 0.0, pltpu.roll(x_ref[0, seg - 1], 1, 0))
        before_i = jnp.where(row == 0, 0.0, pltpu.roll(x_ref[1, seg - 1], 1, 0))

        def fix(s, carry):
            wr, wi, dar, dai = carry
            i = seg - 1 - s
            wr, wi = wr * ar + wi * ai, wi * ar - wr * ai
            gr = g_ref[0, i] + wr * cr - wi * ci
            gi = g_ref[1, i] + wr * ci + wi * cr
            g_ref[0, i] = gr
            g_ref[1, i] = gi
            ip = jnp.maximum(i - 1, 0)
            xpr = jnp.where(i == 0, before_r, x_ref[0, ip])
            xpi = jnp.where(i == 0, before_i, x_ref[1, ip])
            return wr, wi, dar + gr * xpr + gi * xpi, dai + gi * xpr - gr * xpi

        one, z1 = jnp.ones((1, cb), F32), jnp.zeros((1, cb), F32)
        _, _, dar, dai = lax.fori_loop(0, seg, fix, (one, z1, zero, zero))
        da_ref[0:1, :] = jnp.sum(dar, axis=0, keepdims=True)
        da_ref[1:2, :] = jnp.sum(dai, axis=0, keepdims=True)

    blk = pl.BlockSpec((2, seg, ns, cb), lambda j: (0, 0, 0, j))
    vec = pl.BlockSpec((2, cb), lambda j: (0, j))
    g4, da = pl.pallas_call(
        body, grid=(n // cb,), in_specs=[blk, blk, vec], out_specs=[blk, vec],
        out_shape=[jax.ShapeDtypeStruct((2, seg, ns, n), F32), jax.ShapeDtypeStruct((2, n), F32)],
        name="s5_scan_bwd", compiler_params=_params(("parallel",)),
    )(dx2.reshape(2, seg, ns, n), x2.reshape(2, seg, ns, n), a2)
    return g4.reshape(2, t, n), da


_SP_U = SSM_PACK * SSM_GROUP
_SP_X = SSM_PACK * SSM_STATE
_NKB = SSM_GROUPS // SSM_PACK


def mm_s5(kind, a, b, m, name, res=None, tm=2048):
    tm = min(tm, m)
    kw = dict(passes=S5_PASSES, name=name)
    xblk = lambda row, sel, col: ((None, tm, _SP_X), lambda *g: (sel(*g), row(*g), col(*g)))
    if kind == "bu":
        o_blk, o_map = xblk(lambda g, i, k: i, lambda g, i, k: g // _NKB, lambda g, i, k: g % _NKB)
        return _mm(a, b, grid=(2 * _NKB, m // tm, 1), a_blk=(tm, _SP_U), a_map=lambda g, i, k: (i, g % _NKB),
                   b_blk=(None, None, _SP_U, _SP_X), b_map=lambda g, i, k: (g // _NKB, g % _NKB, 0, 0),
                   o_blk=o_blk, o_map=o_map, out_shape=(2, m, _NKB * _SP_X), out_dtype=F32, **kw)
    if kind == "yc":
        a_blk, a_map = xblk(lambda j, i, k: i, lambda j, i, k: k, lambda j, i, k: j)
        return _mm(a, b, grid=(_NKB, m // tm, 2), a_blk=a_blk, a_map=a_map,
                   b_blk=(None, None, _SP_X, _SP_U), b_map=lambda j, i, k: (k, j, 0, 0),
                   o_blk=(tm, _SP_U), o_map=lambda j, i, k: (i, j), out_shape=(m, _NKB * _SP_U), out_dtype=F32,
                   acc_2d=(tm, _SP_U), **kw)
    if kind == "dx":
        o_blk, o_map = xblk(lambda g, i, k: i, lambda g, i, k: g // _NKB, lambda g, i, k: g % _NKB)
        return _mm(a, b, grid=(2 * _NKB, m // tm, 1), a_blk=(tm, _SP_U), a_map=lambda g, i, k: (i, g % _NKB),
                   b_blk=(None, None, _SP_X, _SP_U), b_map=lambda g, i, k: (g // _NKB, g % _NKB, 0, 0),
                   o_blk=o_blk, o_map=o_map, out_shape=(2, m, _NKB * _SP_X), out_dtype=F32, dims=_MM_TB, **kw)
    if kind == "dcd":
        a_blk, a_map = xblk(lambda g, _, k: k, lambda g, _, k: g // _NKB, lambda g, _, k: g % _NKB)
        return _mm(a, b, grid=(2 * _NKB, 1, m // tm), a_blk=a_blk, a_map=a_map,
                   b_blk=(tm, _SP_U), b_map=lambda g, _, k: (k, g % _NKB),
                   o_blk=(None, None, _SP_X, _SP_U), o_map=lambda g, _, k: (g // _NKB, g % _NKB, 0, 0),
                   out_shape=(2, _NKB, _SP_X, _SP_U), out_dtype=F32, dims=_MM_TA, acc_2d=(_SP_X, _SP_U), **kw)
    if kind == "du":
        a_blk, a_map = xblk(lambda j, i, k: i, lambda j, i, k: k, lambda j, i, k: j)
        return _mm(a, b, grid=(_NKB, m // tm, 2), a_blk=a_blk, a_map=a_map,
                   b_blk=(None, None, _SP_U, _SP_X), b_map=lambda j, i, k: (k, j, 0, 0),
                   o_blk=(tm, _SP_U), o_map=lambda j, i, k: (i, j), out_shape=(m, _NKB * _SP_U), out_dtype=F32,
                   dims=_MM_TB, acc_2d=(tm, _SP_U), res=res, res_blk=(tm, _SP_U), res_map=lambda j, i, k: (i, j), **kw)
    assert kind == "dbd"
    b_blk, b_map = xblk(lambda g, _, k: k, lambda g, _, k: g // _NKB, lambda g, _, k: g % _NKB)
    return _mm(a, b, grid=(2 * _NKB, 1, m // tm), a_blk=(tm, _SP_U), a_map=lambda g, _, k: (k, g % _NKB),
               b_blk=b_blk, b_map=b_map,
               o_blk=(None, None, _SP_U, _SP_X), o_map=lambda g, _, k: (g // _NKB, g % _NKB, 0, 0),
               out_shape=(2, _NKB, _SP_U, _SP_X), out_dtype=F32, dims=_MM_TA, acc_2d=(_SP_U, _SP_X), **kw)


def _block_diag(w):
    g, a, b = w.shape
    eye = jnp.eye(SSM_PACK, dtype=w.dtype)
    wp = w.reshape(g // SSM_PACK, SSM_PACK, a, b)
    return jnp.einsum("kgab,gh->kgahb", wp, eye).reshape(g // SSM_PACK, SSM_PACK * a, SSM_PACK * b)


def _block_diag_t(d, a, b):
    k = d.shape[0]
    eye = jnp.eye(SSM_PACK, dtype=d.dtype)
    dp = d.reshape(k, SSM_PACK, a, SSM_PACK, b)
    return jnp.einsum("kgahb,gh->kgab", dp, eye).reshape(k * SSM_PACK, a, b)


def _coords():
    return lax.axis_index("x"), lax.axis_index("y"), lax.axis_index("c")


def all_gather(tensors, name):
    n = len(tensors)
    any_spec = pl.BlockSpec(memory_space=pl.ANY)

    def body(*refs):
        ins, outs = refs[:n], refs[n:2 * n]
        send, recv, local = refs[2 * n:]
        x, y, c = _coords()
        me, sibling = (x, y, c), (x, y, 1 - c)
        chips = [(1 - x, y), (x, 1 - y), (1 - x, 1 - y)]

        def slot(p):
            return 4 * p[0] + 2 * p[1] + p[2]

        def copy(t, k, block, to, src=None):
            dst = outs[t].at[slot(block)]
            return pltpu.make_async_remote_copy(
                src_ref=dst if src is None else src, dst_ref=dst, send_sem=send.at[7 * t + k],
                recv_sem=recv.at[7 * t + k], device_id=to, device_id_type=pl.DeviceIdType.MESH)

        own, sent = [], []
        for t in range(n):
            mine = pltpu.make_async_copy(ins[t], outs[t].at[slot(me)], local.at[t])
            mine.start()
            own.append(mine)
            first = [copy(t, 0, me, sibling, src=ins[t])]
            first += [copy(t, 1 + j, me, (*chip, c), src=ins[t]) for j, chip in enumerate(chips)]
            for cp in first:
                cp.start()
            sent += first
        for t in range(n):
            for j, chip in enumerate(chips):
                copy(t, 1 + j, (*chip, c), me).wait_recv()
                passed = copy(t, 4 + j, (*chip, c), sibling)
                passed.start()
                sent.append(passed)
        for t in range(n):
            copy(t, 0, sibling, me).wait_recv()
            for j, chip in enumerate(chips):
                copy(t, 4 + j, (*chip, 1 - c), me).wait_recv()
        for cp in sent:
            cp.wait_send()
        for cp in own:
            cp.wait()

    return pl.pallas_call(
        body, in_specs=[any_spec] * n, out_specs=[any_spec] * n,
        out_shape=[jax.ShapeDtypeStruct((NDEV,) + a.shape, a.dtype) for a in tensors],
        scratch_shapes=[pltpu.SemaphoreType.DMA((7 * n,)), pltpu.SemaphoreType.DMA((7 * n,)),
                        pltpu.SemaphoreType.DMA((n,))],
        name=name,
    )(*tensors)


_HBM_SPEC = pl.BlockSpec(memory_space=pltpu.HBM)
_SEM_SPEC = pl.BlockSpec(memory_space=pltpu.SEMAPHORE)
_NPEER = NDEV - 1


def _split_copy_params():
    return pltpu.CompilerParams(has_side_effects=pltpu.SideEffectType.DATAFLOW_SIDE_EFFECTING)


def _me_and_peers():
    x, y, c = _coords()
    peers = []
    for rel in range(1, NDEV):
        p = (1 - x if rel & 4 else x, 1 - y if rel & 2 else y, 1 - c if rel & 1 else c)
        peers.append((p, 4 * p[0] + 2 * p[1] + p[2]))
    return 4 * x + 2 * y + c, peers


def _hbm(a):
    return pltpu.with_memory_space_constraint(a, pltpu.HBM)


def gather_start(bufs, name):
    n = len(bufs)

    def body(*refs):
        ins, outs = refs[:n], refs[n:]
        me, peers = _me_and_peers()
        for t in range(n):
            for k, (dev, _) in enumerate(peers):
                pltpu.make_async_remote_copy(
                    src_ref=ins[t].at[me], dst_ref=ins[t].at[me], send_sem=outs[3 * t].at[k],
                    recv_sem=outs[3 * t + 1].at[k], device_id=dev, device_id_type=pl.DeviceIdType.MESH).start()

    out_shape, out_specs = [], []
    for b in bufs:
        out_shape += [pltpu.SemaphoreType.DMA((_NPEER,)), pltpu.SemaphoreType.DMA((_NPEER,)), pltpu.HBM(b.shape, b.dtype)]
        out_specs += [_SEM_SPEC, _SEM_SPEC, _HBM_SPEC]
    res = pl.pallas_call(
        body, name=name, out_shape=tuple(out_shape), in_specs=[_HBM_SPEC] * n, out_specs=tuple(out_specs),
        input_output_aliases={t: 3 * t + 2 for t in range(n)}, compiler_params=_split_copy_params(),
    )(*[_hbm(b) for b in bufs])
    return [tuple(res[3 * t:3 * t + 3]) for t in range(n)]


def gather_wait(started, after, name):
    n = len(started)

    def body(*refs):
        bufs, sems = refs[:n], refs[n:3 * n]
        me, peers = _me_and_peers()
        for t in range(n):
            for k, (dev, slot) in enumerate(peers):
                cp = pltpu.make_async_remote_copy(
                    src_ref=bufs[t].at[me], dst_ref=bufs[t].at[slot], send_sem=sems[2 * t].at[k],
                    recv_sem=sems[2 * t + 1].at[k], device_id=dev, device_id_type=pl.DeviceIdType.MESH)
                cp.wait_recv()
                cp.wait_send()

    args = [s[2] for s in started] + [sem for s in started for sem in s[:2]] + [after]
    res = pl.pallas_call(
        body, name=name, out_shape=tuple(pltpu.HBM(s[2].shape, s[2].dtype) for s in started),
        in_specs=[_HBM_SPEC] * n + [_SEM_SPEC] * (2 * n) + [pl.BlockSpec(memory_space=pl.ANY)],
        out_specs=tuple([_HBM_SPEC] * n), input_output_aliases={t: t for t in range(n)},
        compiler_params=_split_copy_params(),
    )(*args)
    return list(res)


def scatter_start(srcs, name):
    n = len(srcs)
    lands = [lax.empty((_NPEER,) + s.shape[1:], s.dtype) for s in srcs]

    def body(*refs):
        ins, land_refs, outs = refs[:n], refs[n:2 * n], refs[2 * n:]
        _, peers = _me_and_peers()
        for t in range(n):
            for k, (dev, slot) in enumerate(peers):
                pltpu.make_async_remote_copy(
                    src_ref=ins[t].at[slot], dst_ref=land_refs[t].at[k], send_sem=outs[4 * t].at[k],
                    recv_sem=outs[4 * t + 1].at[k], device_id=dev, device_id_type=pl.DeviceIdType.MESH).start()
        outs[4 * n][...] = jnp.zeros_like(outs[4 * n])

    out_shape, out_specs = [], []
    for s, land in zip(srcs, lands):
        out_shape += [pltpu.SemaphoreType.DMA((_NPEER,)), pltpu.SemaphoreType.DMA((_NPEER,)),
                      pltpu.HBM(s.shape, s.dtype), pltpu.HBM(land.shape, land.dtype)]
        out_specs += [_SEM_SPEC, _SEM_SPEC, _HBM_SPEC, _HBM_SPEC]
    out_shape.append(jax.ShapeDtypeStruct((8, LANES), F32))
    out_specs.append(pl.BlockSpec(memory_space=pltpu.VMEM))
    aliases = {t: 4 * t + 2 for t in range(n)}
    aliases.update({n + t: 4 * t + 3 for t in range(n)})
    res = pl.pallas_call(
        body, name=name, out_shape=tuple(out_shape), in_specs=[_HBM_SPEC] * (2 * n), out_specs=tuple(out_specs),
        input_output_aliases=aliases, compiler_params=_split_copy_params(),
    )(*[_hbm(s) for s in srcs], *[_hbm(land) for land in lands])
    return [tuple(res[4 * t:4 * t + 4]) for t in range(n)], res[4 * n]


def scatter_wait(started, after, name):
    n = len(started)

    def body(*refs):
        srcs, land_refs, sems = refs[:n], refs[n:2 * n], refs[2 * n:4 * n]
        _, peers = _me_and_peers()
        for t in range(n):
            for k, (dev, slot) in enumerate(peers):
                cp = pltpu.make_async_remote_copy(
                    src_ref=srcs[t].at[slot], dst_ref=land_refs[t].at[k], send_sem=sems[2 * t].at[k],
                    recv_sem=sems[2 * t + 1].at[k], device_id=dev, device_id_type=pl.DeviceIdType.MESH)
                cp.wait_recv()
                cp.wait_send()

    args = [s[2] for s in started] + [s[3] for s in started] + [sem for s in started for sem in s[:2]] + [after]
    res = pl.pallas_call(
        body, name=name,
        out_shape=tuple([pltpu.HBM(s[2].shape, s[2].dtype) for s in started]
                        + [pltpu.HBM(s[3].shape, s[3].dtype) for s in started]),
        in_specs=[_HBM_SPEC] * (2 * n) + [_SEM_SPEC] * (2 * n) + [pl.BlockSpec(memory_space=pl.ANY)],
        out_specs=tuple([_HBM_SPEC] * (2 * n)), input_output_aliases={t: t for t in range(2 * n)},
        compiler_params=_split_copy_params(),
    )(*args)
    return [(res[t], res[n + t]) for t in range(n)]


def all_to_all(groups, name):
    flat = [a for grp in groups for a in grp]
    n = len(flat)
    where = [(gi, li) for gi, grp in enumerate(groups) for li in range(len(grp))]
    any_spec = pl.BlockSpec(memory_space=pl.ANY)

    def body(*refs):
        ins, outs = refs[:n], refs[n:n + len(groups)]
        send, recv, local = refs[n + len(groups):]
        x, y, c = _coords()
        me = 4 * x + 2 * y + c
        waits = []
        for e in range(n):
            gi, li = where[e]
            mine = pltpu.make_async_copy(ins[e].at[me], outs[gi].at[me, li], local.at[e])
            mine.start()
            waits.append(mine)
        sent, landing = [], []
        for rel in range(1, NDEV):
            px = 1 - x if rel & 4 else x
            py = 1 - y if rel & 2 else y
            pc = 1 - c if rel & 1 else c
            peer = 4 * px + 2 * py + pc
            for e in range(n):
                gi, li = where[e]

                def copy(dst_slot, e=e, gi=gi, li=li, rel=rel, peer=peer, to=(px, py, pc)):
                    return pltpu.make_async_remote_copy(
                        src_ref=ins[e].at[peer], dst_ref=outs[gi].at[dst_slot, li], send_sem=send.at[7 * e + rel - 1],
                        recv_sem=recv.at[7 * e + rel - 1], device_id=to, device_id_type=pl.DeviceIdType.MESH)

                cp = copy(me)
                cp.start()
                sent.append(cp)
                landing.append(copy(peer))
        for cp in landing:
            cp.wait_recv()
        for cp in sent:
            cp.wait_send()
        for cp in waits:
            cp.wait()

    return pl.pallas_call(
        body, in_specs=[any_spec] * n, out_specs=[any_spec] * len(groups),
        out_shape=[jax.ShapeDtypeStruct((NDEV, len(grp)) + grp[0].shape[1:], grp[0].dtype) for grp in groups],
        scratch_shapes=[pltpu.SemaphoreType.DMA((7 * n,)), pltpu.SemaphoreType.DMA((7 * n,)),
                        pltpu.SemaphoreType.DMA((n,))],
        name=name,
    )(*flat)


_PACK_QUANTUM = 8 * LANES


def _pack(parts, lead=0):
    out = []
    for p in parts:
        head = p.shape[:lead]
        f = p.astype(F32).reshape(head + (-1,))
        pad = (-f.shape[-1]) % _PACK_QUANTUM
        if pad:
            f = jnp.concatenate([f, jnp.zeros(head + (pad,), F32)], axis=-1)
        out.append(f.reshape(head + (-1, LANES)))
    return jnp.concatenate(out, axis=lead)


def _unpack(buf, shapes):
    head = buf.shape[:-2]
    out, r = [], 0
    for s in shapes:
        n = 1
        for v in s:
            n *= v
        nr = -(-n // _PACK_QUANTUM) * 8
        flat = buf[..., r:r + nr, :].reshape(head + (nr * LANES,))[..., :n]
        out.append(flat.reshape(head + tuple(s)))
        r += nr
    return out


BIG = ("sb_w_qkv", "sb_w_o", "sg_w_in", "sg_w_o", "ssm_w_in", "ssm_w_glu", "ffn_w_up", "ffn_w_down")
SMALL_SHARDED = ("norm_g", "ssm_d", "ffn_conv_w")
REPLICATED = ("final_norm_g", "sg_norm_g", "sg_w_s", "sg_b", "ssm_lam_re", "ssm_lam_im", "ssm_log_dt",
              "ssm_b_re", "ssm_b_im", "ssm_c_re", "ssm_c_im", "ffn_conv_b")
WEIGHTS = ("norm_g", "final_norm_g", "sb_w_qkv", "sb_w_o", "sg_w_in", "sg_norm_g", "sg_w_s", "sg_b", "sg_w_o",
           "ssm_w_in", "ssm_lam_re", "ssm_lam_im", "ssm_log_dt", "ssm_b_re", "ssm_b_im", "ssm_c_re", "ssm_c_im",
           "ssm_d", "ssm_w_glu", "ffn_w_up", "ffn_conv_w", "ffn_conv_b", "ffn_w_down")


def _step(x, loss_target, w, m, v):
    t, d = x.shape[1], x.shape[2]
    depth = w["norm_g"].shape[0]
    x0 = x.reshape(t, d)
    tgt = loss_target.reshape(t, d)

    mx, my, mc = _coords()
    me = (4 * mx + 2 * my + mc).astype(jnp.int32).reshape(1)
    shard_pack = _pack([w[k] for k in SMALL_SHARDED])
    gathered_small, = all_gather([shard_pack], name="gather_small_weights")
    mixer_weights = (("sb_w_qkv", "sb_w_o"), ("sg_w_in", "sg_w_o"), ("ssm_w_in", "ssm_w_glu"))
    order = []
    for i in range(depth):
        order += [(k, i // 3) for k in mixer_weights[i % 3]] + [("ffn_w_up", i), ("ffn_w_down", i)]
    pending = dict(zip(order, gather_start([cast_into_slot(w[k], l, me) for k, l in order], "gather_weights_start")))
    wg = {}

    def weights(keys, after):
        missing = [key for key in keys if key not in wg]
        if missing:
            for key, buf in zip(missing, gather_wait([pending[key] for key in missing], after, "gather_weights_wait")):
                wg[key] = buf[:, None]
        return [wg[key] for key in keys]

    ng, sd, cw = _unpack(gathered_small, [w[k].shape for k in SMALL_SHARDED])
    norm_full = jnp.transpose(ng, (1, 2, 0, 3)).reshape(depth, 2, d)
    ssm_d_full = jnp.transpose(sd, (1, 0, 2)).reshape(1, d)
    nc = cw.shape[-1]
    conv_b3 = w["ffn_conv_b"].reshape(depth, NDEV, nc)
    p3 = [jnp.concatenate([cw[:, l], conv_b3[l][:, None, :], jnp.zeros((NDEV, 8 - CONV_K - 1, nc), F32)], axis=1)
          for l in range(depth)]

    g_, p_, h_ = SSM_GROUPS, SSM_STATE, SSM_GROUP
    lam_re, lam_im = w["ssm_lam_re"][0], w["ssm_lam_im"][0]
    log_dt = w["ssm_log_dt"][0].reshape(g_, 1)
    b_re, b_im = w["ssm_b_re"][0].reshape(g_ * p_, h_), w["ssm_b_im"][0].reshape(g_ * p_, h_)
    ar, ai, cre, cim = _single(_disc1, [lam_re, lam_im, log_dt], [(g_, p_)] * 4, "s5_disc1")
    cre_c, cim_c = cre.reshape(g_ * p_, 1), cim.reshape(g_ * p_, 1)
    bbr, bbi = _single(_disc2, [cre_c, cim_c, b_re, b_im], [(g_ * p_, h_)] * 2, "s5_disc2")
    per_group_t = lambda a, r, c: jnp.swapaxes(a.reshape(g_, r, c), 1, 2)
    bd = jnp.stack([_block_diag(per_group_t(bbr, p_, h_)), _block_diag(per_group_t(bbi, p_, h_))])
    cd = jnp.stack([_block_diag(per_group_t(w["ssm_c_re"][0], h_, p_)),
                    -_block_diag(per_group_t(w["ssm_c_im"][0], h_, p_))])
    a2 = jnp.stack([ar.reshape(g_ * p_), ai.reshape(g_ * p_)])

    sg_gain = w["sg_norm_g"]
    sg_ws = w["sg_w_s"][0]
    sg_bfull = jnp.broadcast_to(w["sg_b"][0][:, :, None], sg_ws.shape)

    acts = []
    xc = x0
    for i in range(depth):
        mixer, j = i % 3, i // 3
        st = {"x": xc}
        g0 = norm_full[i, 0][None]
        xn = rms_fwd(xc, g0, "rms_fwd")
        st["xn"] = xn
        k_in, k_out = [(k, j) for k in mixer_weights[mixer]]
        w_in, = weights([k_in], xn)
        if mixer == 0:
            qkv = mm_cs_fwd(xn, w_in, 0, BF16, "qkv_fwd")
            o, ltot = sb_attn_fwd(qkv)
            w_out, = weights([k_out], o)
            x1 = mm_rs_fwd(o, w_out, 0, xc, F32, "attn_out_fwd")
            st.update(qkv=qkv, o=o, ltot=ltot)
        elif mixer == 1:
            hin = mm_cs_fwd(xn, w_in, 0, BF16, "sg_in_fwd")
            p = sgu_fwd(hin, sg_gain, sg_ws, sg_bfull)
            w_out, = weights([k_out], p)
            x1 = mm_rs_fwd(p, w_out, 0, xc, F32, "sg_out_fwd")
            st.update(hin=hin, p=p)
        else:
            u = mm_rs_fwd(xn, w_in, 0, None, F32, "ssm_in_fwd")
            u_s = s5_reorder(u, True)
            x2 = s5_scan_fwd(mm_s5("bu", u_s, bd, t, "s5_bu"), a2)
            yc_s = mm_s5("yc", x2, cd, t, "s5_yc")
            yg = s5_post_fwd(s5_reorder(yc_s, False), u, ssm_d_full)
            w_out, = weights([k_out], yg)
            hg = mm_cs_fwd(yg, w_out, 0, BF16, "ssm_glu_fwd")
            x1 = glu_fwd(hg, xc)
            st.update(u_s=u_s, x2=x2, yc_s=yc_s, yg=yg, hg=hg)
        g1 = norm_full[i, 1][None]
        xn2 = rms_fwd(x1, g1, "rms_fwd")
        w_up, w_down = weights([("ffn_w_up", i), ("ffn_w_down", i)], xn2)
        h3 = mm_cs_fwd(xn2, w_up, 0, BF16, "ffn_up_fwd")
        gated = ffn_gate_fwd(h3, p3[i])
        xc = mm_down_fwd(gated, w_down, 0, x1, "ffn_down_fwd")
        st.update(x1=x1, xn2=xn2, h3=h3, gated=gated, g0=g0, g1=g1)
        acts.append(st)

    dx, loss_lanes, d_final_g = loss_head(xc, w["final_norm_g"][None], tgt)
    loss = lax.psum(loss_lanes[0, 0], MESH_AXES)

    scattering = {}
    d_norm = [[None, None] for _ in range(depth)]
    d_p3 = [None] * depth
    rep = {}
    d_ssm_d = None
    token = None

    def scatter(grads_by_key):
        keys = list(grads_by_key)
        started, tok = scatter_start([grads_by_key[key] for key in keys], "scatter_grads_start")
        scattering[tuple(keys)] = started
        return tok

    for i in reversed(range(depth)):
        mixer, j = i % 3, i // 3
        st = acts[i]
        k_in, k_out = [(k, j) for k in mixer_weights[mixer]]
        w_in, w_out, w_up, w_down = weights([k_in, k_out, ("ffn_w_up", i), ("ffn_w_down", i)], None)
        dgated = mm_down_da(dx, w_down, 0, "ffn_down_da", dep=token)
        g_down = mm_down_dw(st["gated"], dx, "ffn_down_dw")
        dy_a, dy_g, dp_a, dp_g = ffn_gate_bwd(st["h3"], dgated, p3[i])
        d_p3[i] = jnp.concatenate([dp_a, dp_g], axis=0)
        dh3 = ffn_conv_t(dy_a, dy_g, p3[i])
        dxn2 = mm_cs_da(dh3, w_up, 0, t, "ffn_up_da")
        g_up = mm_cs_dw(st["xn2"], dh3, nc, "ffn_up_dw")
        dx1, d_norm[i][1] = rms_bwd(st["x1"], st["g1"], dxn2, dx, "rms_bwd")
        token = scatter({("ffn_w_down", i): g_down, ("ffn_w_up", i): g_up})
        if mixer == 0:
            do = mm_rs_da(dx1, w_out, 0, BF16, "attn_out_da", dep=token)
            g_out = mm_rs_dw(st["o"], dx1, "attn_out_dw")
            d3 = sb_attn_bwd(st["qkv"], st["ltot"], do)
            g_in = mm_qkv_dw(st["xn"], d3, w_in.shape[3], "qkv_dw")
            token = scatter({k_in: g_in, k_out: g_out})
            dxn = mm_qkv_da(d3, w_in, 0, "qkv_da", dep=token)
        elif mixer == 1:
            dp = mm_rs_da(dx1, w_out, 0, BF16, "sg_out_da", dep=token)
            g_out = mm_rs_dw(st["p"], dx1, "sg_out_dw")
            dhin, d_ws, d_bfull, d_gain = sgu_bwd(st["hin"], dp, sg_gain, sg_ws, sg_bfull)
            rep.update(sg_w_s=d_ws[None], sg_b=d_bfull[None, :, :, 0], sg_norm_g=d_gain)
            dxn = mm_cs_da(dhin, w_in, 0, t, "sg_in_da")
            g_in = mm_cs_dw(st["xn"], dhin, w_in.shape[3], "sg_in_dw")
        else:
            dhg = glu_bwd(st["hg"], dx1, token)
            dyg = mm_cs_da(dhg, w_out, 0, t, "ssm_glu_da")
            g_out = mm_cs_dw(st["yg"], dhg, w_out.shape[3], "ssm_glu_dw")
            dyc_s, du_skip_s, d_ssm_d = s5_post_bwd(st["yc_s"], st["u_s"], ssm_d_full, s5_reorder(dyg, True))
            dx2 = mm_s5("dx", dyc_s, cd, t, "s5_dx")
            dcd = mm_s5("dcd", st["x2"], dyc_s, t, "s5_dcd")
            g2, da2 = s5_scan_bwd(dx2, st["x2"], a2)
            du = s5_reorder(mm_s5("du", g2, bd, t, "s5_du", res=du_skip_s), False)
            dbd = mm_s5("dbd", st["u_s"], g2, t, "s5_dbd")
            from_bd = lambda blk: jnp.swapaxes(_block_diag_t(blk, h_, p_), 1, 2).reshape(g_ * p_, h_)

            def disc2_bwd(c1, c2, b1, b2, t1, t2):
                return jax.vjp(_disc2, c1, c2, b1, b2)[1]((t1, t2))

            d_cre, d_cim, d_b_re, d_b_im = _single(
                disc2_bwd, [cre_c, cim_c, b_re, b_im, from_bd(dbd[0]), from_bd(dbd[1])],
                [(g_ * p_, 1)] * 2 + [(g_ * p_, h_)] * 2, "s5_disc2_bwd")

            def disc1_bwd(l1, l2, ld, t1, t2, t3, t4):
                return jax.vjp(_disc1, l1, l2, ld)[1]((t1, t2, t3, t4))

            d_lam_re, d_lam_im, d_log_dt = _single(
                disc1_bwd, [lam_re, lam_im, log_dt, da2[0].reshape(g_, p_), da2[1].reshape(g_, p_),
                            d_cre.reshape(g_, p_), d_cim.reshape(g_, p_)],
                [(g_, p_), (g_, p_), (g_, 1)], "s5_disc1_bwd")
            from_cd = lambda blk: jnp.swapaxes(_block_diag_t(blk, p_, h_), 1, 2)
            rep.update(ssm_lam_re=d_lam_re[None], ssm_lam_im=d_lam_im[None], ssm_log_dt=d_log_dt.reshape(1, g_),
                       ssm_b_re=d_b_re.reshape(1, g_, p_, h_), ssm_b_im=d_b_im.reshape(1, g_, p_, h_),
                       ssm_c_re=from_cd(dcd[0])[None], ssm_c_im=-from_cd(dcd[1])[None])
            dxn = mm_rs_da(du, w_in, 0, F32, "ssm_in_da")
            g_in = mm_rs_dw(st["xn"], du, "ssm_in_dw")
        dx, d_norm[i][0] = rms_bwd(st["x"], st["g0"], dxn, dx1, "rms_bwd")
        if mixer != 0:
            token = scatter({k_in: g_in, k_out: g_out})

    rep["final_norm_g"] = d_final_g.reshape(d)
    rep["ffn_conv_b"] = jnp.stack([d_p3[l][:, CONV_K, :].reshape(NDEV * nc) for l in range(depth)])

    d_norm_full = jnp.stack([jnp.concatenate(pair, axis=0) for pair in d_norm])
    d_norm_pieces = jnp.transpose(d_norm_full.reshape(depth, 2, NDEV, d // NDEV), (2, 0, 1, 3))
    d_ssm_d_pieces = jnp.transpose(d_ssm_d.reshape(1, NDEV, d // NDEV), (1, 0, 2))
    d_conv_w_pieces = jnp.stack([d_p3[l][:, :CONV_K, :] for l in range(depth)], axis=1)
    small_pieces = _pack([d_norm_pieces, d_ssm_d_pieces, d_conv_w_pieces], lead=1)
    small_received, = all_to_all([[small_pieces]], name="scatter_small_grads")
    rep_parts, = all_gather([_pack([rep[k] for k in REPLICATED])], name="gather_small_grads")
    own, landed = {}, {}
    for keys, started in scattering.items():
        for key, (src, land) in zip(keys, scatter_wait(started, dx, "scatter_grads_wait")):
            own[key], landed[key] = src, land

    grads, deltas, new_m, new_v = {}, {}, {}, {}
    for k in BIG:
        layers = range(w[k].shape[0])
        grads[k], deltas[k], new_m[k], new_v[k] = adamw_layers(
            w[k], m[k], v[k], [landed[(k, l)] for l in layers], [own[(k, l)] for l in layers], me, "adamw")
    for names, parts in ((SMALL_SHARDED, small_received[:, 0]), (REPLICATED, rep_parts)):
        res = adamw(_pack([w[k] for k in names]), _pack([m[k] for k in names]), _pack([v[k] for k in names]), parts,
                    "adamw_small")
        for tree, buf in zip((grads, deltas, new_m, new_v), res):
            for k, val in zip(names, _unpack(buf, [w[k].shape for k in names])):
                tree[k] = val
    grad_x = dx.reshape(x.shape)
    return (loss, grad_x, *[grads[k] for k in WEIGHTS], *[deltas[k] for k in WEIGHTS],
            *[new_m[k] for k in WEIGHTS], *[new_v[k] for k in WEIGHTS])


def kernel(x, norm_g, final_norm_g, sb_w_qkv, sb_w_o, sg_w_in, sg_norm_g, sg_w_s, sg_b, sg_w_o, ssm_w_in, ssm_lam_re, ssm_lam_im, ssm_log_dt, ssm_b_re, ssm_b_im, ssm_c_re, ssm_c_im, ssm_d, ssm_w_glu, ffn_w_up, ffn_conv_w, ffn_conv_b, ffn_w_down, loss_target, m_norm_g, m_final_norm_g, m_sb_w_qkv, m_sb_w_o, m_sg_w_in, m_sg_norm_g, m_sg_w_s, m_sg_b, m_sg_w_o, m_ssm_w_in, m_ssm_lam_re, m_ssm_lam_im, m_ssm_log_dt, m_ssm_b_re, m_ssm_b_im, m_ssm_c_re, m_ssm_c_im, m_ssm_d, m_ssm_w_glu, m_ffn_w_up, m_ffn_conv_w, m_ffn_conv_b, m_ffn_w_down, v_norm_g, v_final_norm_g, v_sb_w_qkv, v_sb_w_o, v_sg_w_in, v_sg_norm_g, v_sg_w_s, v_sg_b, v_sg_w_o, v_ssm_w_in, v_ssm_lam_re, v_ssm_lam_im, v_ssm_log_dt, v_ssm_b_re, v_ssm_b_im, v_ssm_c_re, v_ssm_c_im, v_ssm_d, v_ssm_w_glu, v_ffn_w_up, v_ffn_conv_w, v_ffn_conv_b, v_ffn_w_down):
    w = dict(zip(WEIGHTS, (norm_g, final_norm_g, sb_w_qkv, sb_w_o, sg_w_in, sg_norm_g, sg_w_s, sg_b, sg_w_o, ssm_w_in,
                           ssm_lam_re, ssm_lam_im, ssm_log_dt, ssm_b_re, ssm_b_im, ssm_c_re, ssm_c_im, ssm_d, ssm_w_glu,
                           ffn_w_up, ffn_conv_w, ffn_conv_b, ffn_w_down)))
    m = dict(zip(WEIGHTS, (m_norm_g, m_final_norm_g, m_sb_w_qkv, m_sb_w_o, m_sg_w_in, m_sg_norm_g, m_sg_w_s, m_sg_b,
                           m_sg_w_o, m_ssm_w_in, m_ssm_lam_re, m_ssm_lam_im, m_ssm_log_dt, m_ssm_b_re, m_ssm_b_im,
                           m_ssm_c_re, m_ssm_c_im, m_ssm_d, m_ssm_w_glu, m_ffn_w_up, m_ffn_conv_w, m_ffn_conv_b,
                           m_ffn_w_down)))
    v = dict(zip(WEIGHTS, (v_norm_g, v_final_norm_g, v_sb_w_qkv, v_sb_w_o, v_sg_w_in, v_sg_norm_g, v_sg_w_s, v_sg_b,
                           v_sg_w_o, v_ssm_w_in, v_ssm_lam_re, v_ssm_lam_im, v_ssm_log_dt, v_ssm_b_re, v_ssm_b_im,
                           v_ssm_c_re, v_ssm_c_im, v_ssm_d, v_ssm_w_glu, v_ffn_w_up, v_ffn_conv_w, v_ffn_conv_b,
                           v_ffn_w_down)))
    return _step(x, loss_target, w, m, v)
```

```python
import functools

import jax
import jax.numpy as jnp
from jax import lax
from jax.experimental import pallas as pl
from jax.experimental.pallas import tpu as pltpu

F32, BF16 = jnp.float32, jnp.bfloat16
MESH_AXES = ("x", "y", "c")
NDEV = 8
EPS = 1e-6
HEAD_DIM = 64
LANES = 128
ATT_BQ, ATT_BK = 512, 256
CHUNK = 128
SG_GROUPS = 8
SSM_GROUPS, SSM_STATE, SSM_GROUP = 64, 64, 16
SSM_PACK = 8
S5_PASSES = 1
CONV_K = 3
HALO = 16
ROW_BLOCK = 512
SCAN_COLS = 256
ADAM_LR, ADAM_B1, ADAM_B2, ADAM_EPS, ADAM_WD, ADAM_STEP = 0.001, 0.9, 0.999, 1e-08, 0.01, 10
VMEM_LIMIT = 56 * 1024 * 1024

_MM = (((1,), (0,)), ((), ()))
_MM_TB = (((1,), (1,)), ((), ()))
_MM_TA = (((0,), (0,)), ((), ()))


def _params(sem):
    return pltpu.CompilerParams(dimension_semantics=sem, vmem_limit_bytes=VMEM_LIMIT)


def _rows(total, cap, mult=16):
    best = None
    for d in range(mult, min(total, cap) + 1, mult):
        if total % d == 0:
            best = d
    return best if best is not None else total


def _dot(a, b, dims, passes):
    if passes == 1:
        return lax.dot_general(a.astype(BF16), b.astype(BF16), dims, preferred_element_type=F32)
    a = a.astype(F32)
    b = b.astype(F32)
    ah = a.astype(BF16)
    bh = b.astype(BF16)
    al = (a - ah.astype(F32)).astype(BF16)
    bl = (b - bh.astype(F32)).astype(BF16)
    out = lax.dot_general(ah, bh, dims, preferred_element_type=F32)
    out = out + lax.dot_general(al, bh, dims, preferred_element_type=F32)
    return out + lax.dot_general(ah, bl, dims, preferred_element_type=F32)


def _mm(a, b, *, grid, a_blk, a_map, b_blk, b_map, o_blk, o_map, out_shape, out_dtype, name,
        dims=_MM, passes=1, res=None, res_blk=None, res_map=None, b_2d=None, acc_2d=None, dep=None, norm=None):
    nk = grid[2]
    has_res, has_norm = res is not None, norm is not None
    a_maps = list(a_map) if isinstance(a_map, (list, tuple)) else [a_map]
    b_maps = list(b_map) if isinstance(b_map, (list, tuple)) else [b_map]
    na, nb = len(a_maps), len(b_maps)
    n_in = na + nb + has_res + has_norm + (dep is not None)

    def body(*refs):
        o_ref = refs[n_in]
        r_ref = refs[na + nb] if has_res else None
        av = refs[0][...] if na == 1 else jnp.concatenate([r[...] for r in refs[:na]], axis=-1)
        bv = refs[na][...] if nb == 1 else jnp.concatenate([r[...] for r in refs[na:na + nb]], axis=-1)
        if b_2d is not None:
            bv = bv.reshape(b_2d)
        part = _dot(av, bv, dims, passes)

        def finish(total):
            if has_res:
                total = total + r_ref[...].astype(F32)
            o_ref[...] = total.reshape(o_ref.shape).astype(o_ref.dtype)
            if has_norm:
                refs[n_in + 1][...] = _rms(total, refs[na + nb + has_res][...]).astype(BF16)

        if nk == 1:
            finish(part)
        else:
            acc_ref = refs[-1]
            k = pl.program_id(2)

            @pl.when(k == 0)
            def _():
                acc_ref[...] = part

            @pl.when(k > 0)
            def _():
                acc_ref[...] += part

            @pl.when(k == nk - 1)
            def _():
                finish(acc_ref[...])

    in_specs = [pl.BlockSpec(a_blk, f) for f in a_maps] + [pl.BlockSpec(b_blk, f) for f in b_maps]
    args = [a] * na + [b] * nb
    if has_res:
        in_specs.append(pl.BlockSpec(res_blk, res_map))
        args.append(res)
    if has_norm:
        in_specs.append(pl.BlockSpec(norm.shape, lambda *_: (0, 0)))
        args.append(norm)
    if dep is not None:
        in_specs.append(pl.BlockSpec(memory_space=pl.ANY))
        args.append(dep)
    scratch = [pltpu.VMEM(acc_2d, F32)] if nk > 1 else []
    out_specs, out_shapes = pl.BlockSpec(o_blk, o_map), jax.ShapeDtypeStruct(out_shape, out_dtype)
    if has_norm:
        out_specs, out_shapes = [out_specs] * 2, [out_shapes, jax.ShapeDtypeStruct(out_shape, BF16)]
    return pl.pallas_call(
        body, grid=grid, in_specs=in_specs, out_specs=out_specs, out_shape=out_shapes, scratch_shapes=scratch,
        name=name, compiler_params=_params(("parallel", "parallel", "arbitrary")),
    )(*args)


def _cs_act_spec(ns, tm, row_of, col_of):
    if ns % LANES == 0:
        return (tm, ns), lambda *g: (row_of(*g), col_of(*g))
    return (None, tm, ns), lambda *g: (col_of(*g), row_of(*g), 0)


def mm_cs_fwd(a, w4, l, out_dtype, name, tm=2048):
    m, k = a.shape
    tm = min(tm, m)
    ns = w4.shape[3]
    o_blk, o_map = _cs_act_spec(ns, tm, lambda j, i, kk: i, lambda j, i, kk: j)
    out_shape = (m, NDEV * ns) if ns % LANES == 0 else (NDEV, m, ns)
    return _mm(a, w4, grid=(NDEV, m // tm, 1), a_blk=(tm, k), a_map=lambda j, i, kk: (i, 0),
               b_blk=(None, None, k, ns), b_map=lambda j, i, kk: (j, l, 0, 0),
               o_blk=o_blk, o_map=o_map, out_shape=out_shape, out_dtype=out_dtype, name=name)


def mm_cs_da(dc, w4, l, m, name, tm=1024):
    k, ns = w4.shape[2], w4.shape[3]
    tm = min(tm, m)
    a_blk, a_map = _cs_act_spec(ns, tm, lambda i, _, j: i, lambda i, _, j: j)
    return _mm(dc, w4, grid=(m // tm, 1, NDEV), a_blk=a_blk, a_map=a_map,
               b_blk=(None, None, k, ns), b_map=lambda i, _, j: (j, l, 0, 0),
               o_blk=(tm, k), o_map=lambda i, _, j: (i, 0), out_shape=(m, k), out_dtype=F32,
               dims=_MM_TB, acc_2d=(tm, k), name=name)


def mm_cs_dw(a, dc, ns, name, tk=2048):
    m, k = a.shape
    tk = min(tk, m)
    b_blk, b_map = _cs_act_spec(ns, tk, lambda j, _, kk: kk, lambda j, _, kk: j)
    return _mm(a, dc, grid=(NDEV, 1, m // tk), a_blk=(tk, k), a_map=lambda j, _, kk: (kk, 0),
               b_blk=b_blk, b_map=b_map, o_blk=(None, k, ns), o_map=lambda j, _, kk: (j, 0, 0),
               out_shape=(NDEV, k, ns), out_dtype=BF16, dims=_MM_TA, acc_2d=(k, ns), name=name)


def mm_rs_fwd(a, w4, l, res, out_dtype, name, tm=1024, norm=None):
    m, k = a.shape
    tm = min(tm, m)
    ks, n = w4.shape[2], w4.shape[3]
    return _mm(a, w4, grid=(m // tm, 1, 1), a_blk=(tm, k), a_map=lambda i, _, kk: (i, 0),
               b_blk=(NDEV, None, ks, n), b_map=lambda i, _, kk: (0, l, 0, 0), b_2d=(k, n),
               o_blk=(tm, n), o_map=lambda i, _, kk: (i, 0), out_shape=(m, n), out_dtype=out_dtype,
               res=res, res_blk=(tm, n), res_map=lambda i, _, kk: (i, 0), name=name, norm=norm)


def mm_rs_da(dc, w4, l, out_dtype, name, tm=1024, dep=None):
    m, n = dc.shape
    tm = min(tm, m)
    ks = w4.shape[2]
    k = NDEV * ks
    return _mm(dc, w4, grid=(m // tm, 1, 1), a_blk=(tm, n), a_map=lambda i, _, kk: (i, 0),
               b_blk=(NDEV, None, ks, n), b_map=lambda i, _, kk: (0, l, 0, 0), b_2d=(k, n),
               o_blk=(tm, k), o_map=lambda i, _, kk: (i, 0), out_shape=(m, k), out_dtype=out_dtype,
               dims=_MM_TB, name=name, dep=dep)


def mm_rs_dw(a, dc, name, tk=1024):
    m, k = a.shape
    tk = min(tk, m)
    n = dc.shape[1]
    ks = k // NDEV
    return _mm(a, dc, grid=(1, 1, m // tk), a_blk=(tk, k), a_map=lambda _, __, kk: (kk, 0),
               b_blk=(tk, n), b_map=lambda _, __, kk: (kk, 0),
               o_blk=(NDEV, ks, n), o_map=lambda _, __, kk: (0, 0, 0), out_shape=(NDEV, ks, n),
               out_dtype=BF16, dims=_MM_TA, acc_2d=(k, n), name=name)


def mm_down_fwd(a3, w4, l, res, name, tm=1024, norm=None):
    nj, m, kc = a3.shape
    tm = min(tm, m)
    ks, n = w4.shape[2], w4.shape[3]
    return _mm(a3, w4, grid=(m // tm, 1, nj), a_blk=(None, tm, kc), a_map=lambda i, _, j: (j, i, 0),
               b_blk=(2, None, ks, n), b_map=lambda i, _, j: (j, l, 0, 0), b_2d=(kc, n),
               o_blk=(tm, n), o_map=lambda i, _, j: (i, 0), out_shape=(m, n), out_dtype=F32,
               res=res, res_blk=(tm, n), res_map=lambda i, _, j: (i, 0), acc_2d=(tm, n), name=name, norm=norm)


def mm_down_da(dc, w4, l, name, tm=2048, dep=None):
    m, n = dc.shape
    tm = min(tm, m)
    ks = w4.shape[2]
    kc = 2 * ks
    nj = NDEV // 2
    return _mm(dc, w4, grid=(nj, m // tm, 1), a_blk=(tm, n), a_map=lambda j, i, _: (i, 0),
               b_blk=(2, None, ks, n), b_map=lambda j, i, _: (j, l, 0, 0), b_2d=(kc, n),
               o_blk=(None, tm, kc), o_map=lambda j, i, _: (j, i, 0), out_shape=(nj, m, kc),
               out_dtype=BF16, dims=_MM_TB, name=name, dep=dep)


def mm_down_dw(a3, dc, name, tk=2048):
    nj, m, kc = a3.shape
    tk = min(tk, m)
    n = dc.shape[1]
    return _mm(a3, dc, grid=(nj, 1, m // tk), a_blk=(None, tk, kc), a_map=lambda j, _, kk: (j, kk, 0),
               b_blk=(tk, n), b_map=lambda j, _, kk: (kk, 0),
               o_blk=(2, kc // 2, n), o_map=lambda j, _, kk: (j, 0, 0), out_shape=(NDEV, kc // 2, n),
               out_dtype=BF16, dims=_MM_TA, acc_2d=(kc, n), name=name)


def _qkv_group_maps(d, ns, row_of, piece_of):
    per_arr, per_piece = d // LANES, ns // LANES

    def group_map(q):
        def f(*g):
            grp = piece_of(*g) * per_piece + q
            return grp // per_arr, row_of(*g), grp % per_arr
        return f

    return [group_map(q) for q in range(per_piece)]


def mm_qkv_da(d3, w4, l, name, tm=1024, dep=None):
    _, m, d = d3.shape
    tm = min(tm, m)
    k, ns = w4.shape[2], w4.shape[3]
    return _mm(d3, w4, grid=(m // tm, 1, NDEV),
               a_blk=(None, tm, LANES), a_map=_qkv_group_maps(d, ns, lambda i, _, j: i, lambda i, _, j: j),
               b_blk=(None, None, k, ns), b_map=lambda i, _, j: (j, l, 0, 0),
               o_blk=(tm, k), o_map=lambda i, _, j: (i, 0), out_shape=(m, k), out_dtype=F32,
               dims=_MM_TB, acc_2d=(tm, k), name=name, dep=dep)


def mm_qkv_dw(a, d3, ns, name, tk=2048):
    m, k = a.shape
    tk = min(tk, m)
    d = d3.shape[2]
    return _mm(a, d3, grid=(NDEV, 1, m // tk), a_blk=(tk, k), a_map=lambda j, _, kk: (kk, 0),
               b_blk=(None, tk, LANES), b_map=_qkv_group_maps(d, ns, lambda j, _, kk: kk, lambda j, _, kk: j),
               o_blk=(None, k, ns), o_map=lambda j, _, kk: (j, 0, 0),
               out_shape=(NDEV, k, ns), out_dtype=BF16, dims=_MM_TA, acc_2d=(k, ns), name=name)


def _rowwise(fn, ins, outs, *, tr, name, acc_outs=()):
    rows = next(a.shape[0] if kind == "row" else a.shape[1] for a, kind in ins if kind in ("row", "row3"))
    n_in, n_out = len(ins), len(outs)
    n_read = sum(kind != "dep" for _, kind in ins)

    def body(*refs):
        vals = fn(*[r[...] for r in refs[:n_read]])
        if not isinstance(vals, (tuple, list)):
            vals = (vals,)
        for ref, val in zip(refs[n_in:n_in + n_out], vals[:n_out]):
            ref[...] = val.astype(ref.dtype)
        i = pl.program_id(0)
        for ref, val in zip(refs[n_in + n_out:], vals[n_out:]):
            val = val.astype(ref.dtype)

            @pl.when(i == 0)
            def _(ref=ref, val=val):
                ref[...] = val

            @pl.when(i > 0)
            def _(ref=ref, val=val):
                ref[...] += val

    in_specs = []
    for a, kind in ins:
        if kind == "row":
            in_specs.append(pl.BlockSpec((tr, a.shape[1]), lambda i: (i, 0)))
        elif kind == "row3":
            in_specs.append(pl.BlockSpec((a.shape[0], tr, a.shape[2]), lambda i: (0, i, 0)))
        elif kind == "dep":
            in_specs.append(pl.BlockSpec(memory_space=pl.ANY))
        else:
            in_specs.append(pl.BlockSpec(a.shape, lambda i, nd=a.ndim: (0,) * nd))
    out_specs = [pl.BlockSpec((tr, c), lambda i: (i, 0)) for c, _ in outs]
    out_specs += [pl.BlockSpec(s, lambda i, nd=len(s): (0,) * nd) for s, _ in acc_outs]
    out_shape = [jax.ShapeDtypeStruct((rows, c), dt) for c, dt in outs]
    out_shape += [jax.ShapeDtypeStruct(s, dt) for s, dt in acc_outs]
    res = pl.pallas_call(
        body, grid=(rows // tr,), in_specs=in_specs, out_specs=out_specs, out_shape=out_shape, name=name,
        compiler_params=_params(("arbitrary",) if acc_outs else ("parallel",)),
    )(*[a for a, _ in ins])
    return res


def _rms(x, g):
    return x * lax.rsqrt(jnp.mean(x * x, axis=-1, keepdims=True) + EPS) * g


def cast_into_slot(w, l, me):
    _, r, c = w.shape
    tr = _rows(r, 512)

    def body(me_ref, w_ref, o_ref):
        o_ref[...] = w_ref[...].astype(o_ref.dtype)

    return pl.pallas_call(
        body,
        grid_spec=pltpu.PrefetchScalarGridSpec(
            num_scalar_prefetch=1, grid=(r // tr,),
            in_specs=[pl.BlockSpec((None, tr, c), lambda i, me_ref: (l, i, 0))],
            out_specs=pl.BlockSpec((None, tr, c), lambda i, me_ref: (me_ref[0], i, 0))),
        out_shape=jax.ShapeDtypeStruct((NDEV, r, c), BF16), name="cast_into_slot",
        compiler_params=_params(("parallel",)),
    )(me, w)


def rms_fwd(x, g, name):
    out, = _rowwise(_rms, [(x, "row"), (g, "full")], [(x.shape[1], BF16)], tr=ROW_BLOCK, name=name)
    return out


def rms_bwd(x, g, dy, dres, name):
    def fn(xv, gv, dyv, drv):
        _, vjp = jax.vjp(_rms, xv, gv)
        dx, dg = vjp(dyv.astype(F32))
        return drv + dx, dg

    d = x.shape[1]
    return _rowwise(fn, [(x, "row"), (g, "full"), (dy, "row"), (dres, "row")], [(d, F32)], tr=ROW_BLOCK, name=name,
                    acc_outs=[((1, d), F32)])


def loss_head(x, g, tgt):
    def f(xv, gv, tv):
        err = jnp.square(_rms(xv, gv) - tv)
        return 0.5 * jnp.sum(jnp.mean(err, axis=-1))

    def fn(xv, gv, tv):
        val, (dx, dg) = jax.value_and_grad(f, argnums=(0, 1))(xv, gv, tv)
        return dx, jnp.full((1, LANES), val, F32), dg

    d = x.shape[1]
    return _rowwise(fn, [(x, "row"), (g, "full"), (tgt, "row")], [(d, F32)], tr=ROW_BLOCK, name="loss_head",
                    acc_outs=[((1, LANES), F32), ((1, d), F32)])


def _glu(hg, x):
    half = hg.shape[1] // 2
    return x + hg[:, :half] * jax.nn.sigmoid(hg[:, half:])


def glu_fwd(hg, x, norm):
    def fn(h, xv, g):
        x1 = _glu(h.astype(F32), xv)
        return x1, _rms(x1, g)

    d = x.shape[1]
    return _rowwise(fn, [(hg, "row"), (x, "row"), (norm, "full")], [(d, F32), (d, BF16)], tr=ROW_BLOCK, name="glu_fwd")


def glu_bwd(hg, dx1, dep):
    def fn(h, d):
        _, vjp = jax.vjp(lambda hv: _glu(hv, jnp.zeros_like(d)), h.astype(F32))
        return vjp(d)[0]

    out, = _rowwise(fn, [(hg, "row"), (dx1, "row"), (dep, "dep")], [(hg.shape[1], BF16)], tr=ROW_BLOCK, name="glu_bwd")
    return out


def _s5_post(yc, u, d):
    return jax.nn.gelu(yc + d * u)


def s5_post_fwd(yc, u, d):
    out, = _rowwise(_s5_post, [(yc, "row"), (u, "row"), (d, "full")], [(yc.shape[1], BF16)], tr=ROW_BLOCK,
                    name="s5_post_fwd")
    return out


def s5_post_bwd(yc, u, d, dyg):
    def fn(ycv, uv, dv, g):
        _, vjp = jax.vjp(_s5_post, ycv, uv, dv)
        return vjp(g.astype(F32))

    dm = yc.shape[1]
    return _rowwise(fn, [(yc, "row"), (u, "row"), (d, "full"), (dyg, "row")], [(dm, F32), (dm, F32)], tr=ROW_BLOCK,
                    name="s5_post_bwd", acc_outs=[((1, dm), F32)])


def _adam_update(wv, mv, vv, g):
    m2 = ADAM_B1 * mv + (1.0 - ADAM_B1) * g
    v2 = ADAM_B2 * vv + (1.0 - ADAM_B2) * jnp.square(g)
    m_hat = m2 / (1.0 - ADAM_B1 ** ADAM_STEP)
    v_hat = v2 / (1.0 - ADAM_B2 ** ADAM_STEP)
    delta = -ADAM_LR * (m_hat / (jnp.sqrt(v_hat) + ADAM_EPS) + ADAM_WD * wv)
    return g, delta, m2, v2


def adamw(w, m, v, g_parts, name):
    def fn(wv, mv, vv, gp):
        g = gp[0].astype(F32)
        for p in range(1, gp.shape[0]):
            g = g + gp[p].astype(F32)
        return _adam_update(wv, mv, vv, g)

    c = w.shape[1]
    return _rowwise(fn, [(w, "row"), (m, "row"), (v, "row"), (g_parts, "row3")], [(c, F32)] * 4,
                    tr=_rows(w.shape[0], 256), name=name)


def adamw_layers(w, m, v, lands, owns, me, name):
    nl, r, c = w.shape
    tr = _rows(r, 256)

    def body(me_ref, w_ref, m_ref, v_ref, *rest):
        land_refs, own_refs, out_refs = rest[:nl], rest[nl:2 * nl], rest[2 * nl:]
        for l in range(nl):
            @pl.when(pl.program_id(0) == l)
            def _(l=l):
                g = own_refs[l][...].astype(F32)
                for p in range(NDEV - 1):
                    g = g + land_refs[l][p].astype(F32)
                for ref, val in zip(out_refs, _adam_update(w_ref[...], m_ref[...], v_ref[...], g)):
                    ref[...] = val

    def rows_of(l):
        return lambda li, i, me_ref: jnp.where(li == l, i, 0)

    wspec = pl.BlockSpec((None, tr, c), lambda li, i, me_ref: (li, i, 0))
    in_specs = [wspec] * 3
    in_specs += [pl.BlockSpec((NDEV - 1, tr, c), lambda li, i, me_ref, f=rows_of(l): (0, f(li, i, me_ref), 0))
                 for l in range(nl)]
    in_specs += [pl.BlockSpec((None, tr, c), lambda li, i, me_ref, f=rows_of(l): (me_ref[0], f(li, i, me_ref), 0))
                 for l in range(nl)]
    return pl.pallas_call(
        body,
        grid_spec=pltpu.PrefetchScalarGridSpec(
            num_scalar_prefetch=1, grid=(nl, r // tr), in_specs=in_specs, out_specs=[wspec] * 4),
        out_shape=[jax.ShapeDtypeStruct(w.shape, F32)] * 4, name=name, compiler_params=_params(("parallel", "parallel")),
    )(me, w, m, v, *lands, *owns)


def _conv_rows(cur, halo, p, first):
    r = cur.shape[0]
    ext = jnp.concatenate([jnp.where(first, 0.0, halo), cur], axis=0)
    s1 = pltpu.roll(ext, 1, 0)[HALO:]
    s2 = pltpu.roll(ext, 2, 0)[HALO:]
    return p[0:1] * s2 + p[1:2] * s1 + p[2:3] * cur + p[3:4], s1, s2


def ffn_gate_fwd(h3, p3, tr=ROW_BLOCK):
    _, t, c = h3.shape
    half = NDEV // 2

    def body(a_ref, ah_ref, g_ref, gh_ref, pa_ref, pg_ref, o_ref):
        first = pl.program_id(1) == 0
        ya, _, _ = _conv_rows(a_ref[...].astype(F32), ah_ref[...].astype(F32), pa_ref[...], first)
        yg, _, _ = _conv_rows(g_ref[...].astype(F32), gh_ref[...].astype(F32), pg_ref[...], first)
        o_ref[...] = (jax.nn.silu(yg) * ya).astype(o_ref.dtype)

    main = lambda off: pl.BlockSpec((None, tr, c), lambda j, i: (j + off, i, 0))
    halo = lambda off: pl.BlockSpec((None, HALO, c), lambda j, i: (j + off, jnp.maximum(i * (tr // HALO) - 1, 0), 0))
    par = lambda off: pl.BlockSpec((None, 8, c), lambda j, i: (j + off, 0, 0))
    return pl.pallas_call(
        body, grid=(half, t // tr),
        in_specs=[main(0), halo(0), main(half), halo(half), par(0), par(half)],
        out_specs=pl.BlockSpec((None, tr, c), lambda j, i: (j, i, 0)),
        out_shape=jax.ShapeDtypeStruct((half, t, c), BF16), name="ffn_gate_fwd",
        compiler_params=_params(("parallel", "parallel")),
    )(h3, h3, h3, h3, p3, p3)


def ffn_gate_bwd(h3, dgated3, p3, tr=ROW_BLOCK):
    _, t, c = h3.shape
    half = NDEV // 2

    def body(a_ref, ah_ref, g_ref, gh_ref, dg_ref, pa_ref, pg_ref, dya_ref, dyg_ref, dpa_ref, dpg_ref):
        i = pl.program_id(1)
        first = i == 0
        a = a_ref[...].astype(F32)
        g = g_ref[...].astype(F32)
        ya, a1, a2 = _conv_rows(a, ah_ref[...].astype(F32), pa_ref[...], first)
        yg, g1, g2 = _conv_rows(g, gh_ref[...].astype(F32), pg_ref[...], first)
        d = dg_ref[...].astype(F32)
        sig = jax.nn.sigmoid(yg)
        d_ya = (d * (yg * sig)).astype(dya_ref.dtype)
        d_yg = (d * ya * (sig * (1.0 + yg * (1.0 - sig)))).astype(dyg_ref.dtype)
        dya_ref[...] = d_ya
        dyg_ref[...] = d_yg
        for dy, cur, s1, s2, dp_ref in ((d_ya.astype(F32), a, a1, a2, dpa_ref), (d_yg.astype(F32), g, g1, g2, dpg_ref)):
            rows = [jnp.sum(dy * s2, axis=0, keepdims=True), jnp.sum(dy * s1, axis=0, keepdims=True),
                    jnp.sum(dy * cur, axis=0, keepdims=True), jnp.sum(dy, axis=0, keepdims=True)]
            dp = jnp.concatenate(rows + [jnp.zeros((4, c), F32)], axis=0)

            @pl.when(first)
            def _(dp_ref=dp_ref, dp=dp):
                dp_ref[...] = dp

            @pl.when(i > 0)
            def _(dp_ref=dp_ref, dp=dp):
                dp_ref[...] += dp

    main = lambda off: pl.BlockSpec((None, tr, c), lambda j, i: (j + off, i, 0))
    halo = lambda off: pl.BlockSpec((None, HALO, c), lambda j, i: (j + off, jnp.maximum(i * (tr // HALO) - 1, 0), 0))
    par = lambda off: pl.BlockSpec((None, 8, c), lambda j, i: (j + off, 0, 0))
    return pl.pallas_call(
        body, grid=(half, t // tr),
        in_specs=[main(0), halo(0), main(half), halo(half), main(0), par(0), par(half)],
        out_specs=[main(0), main(0), par(0), par(0)],
        out_shape=[jax.ShapeDtypeStruct((half, t, c), BF16)] * 2 + [jax.ShapeDtypeStruct((half, 8, c), F32)] * 2,
        name="ffn_gate_bwd", compiler_params=_params(("parallel", "arbitrary")),
    )(h3, h3, h3, h3, dgated3, p3, p3)


def ffn_conv_t(dy_a, dy_g, p3, tr=2 * ROW_BLOCK):
    half, t, c = dy_a.shape
    tr = min(tr, t)
    nblk = t // tr

    def body(a_ref, ah_ref, g_ref, gh_ref, p_ref, o_ref):
        is_a = pl.program_id(0) < half
        last = pl.program_id(1) == nblk - 1
        cur = jnp.where(is_a, a_ref[...], g_ref[...]).astype(F32)
        nxt = jnp.where(is_a, ah_ref[...], gh_ref[...]).astype(F32)
        ext = jnp.concatenate([cur, jnp.where(last, 0.0, nxt)], axis=0)
        n = tr + HALO
        s1 = pltpu.roll(ext, n - 1, 0)[:tr]
        s2 = pltpu.roll(ext, n - 2, 0)[:tr]
        p = p_ref[...]
        o_ref[...] = (p[2:3] * cur + p[1:2] * s1 + p[0:1] * s2).astype(o_ref.dtype)

    main = pl.BlockSpec((None, tr, c), lambda j, i: (j % half, i, 0))
    halo = pl.BlockSpec((None, HALO, c), lambda j, i: (j % half, jnp.minimum((i + 1) * (tr // HALO), t // HALO - 1), 0))
    return pl.pallas_call(
        body, grid=(NDEV, nblk),
        in_specs=[main, halo, main, halo, pl.BlockSpec((None, 8, c), lambda j, i: (j, 0, 0))],
        out_specs=pl.BlockSpec((None, tr, c), lambda j, i: (j, i, 0)),
        out_shape=jax.ShapeDtypeStruct((NDEV, t, c), BF16), name="ffn_conv_t",
        compiler_params=_params(("parallel", "parallel")),
    )(dy_a, dy_a, dy_g, dy_g, p3)


def _att_consts(bq, bk):
    lane = lax.broadcasted_iota(jnp.int32, (1, LANES), 1)
    heads = (lane < HEAD_DIM, lane >= HEAD_DIM)
    rr = lax.broadcasted_iota(jnp.int32, (bq, bk), 0)
    cc = lax.broadcasted_iota(jnp.int32, (bq, bk), 1)
    kr = lax.broadcasted_iota(jnp.int32, (bk, bk), 0)
    kc = lax.broadcasted_iota(jnp.int32, (bk, bk), 1)
    return heads, rr, cc, kr, kc


def _split_dot(x, tri, parts):
    out = None
    for _ in range(parts):
        piece = x.astype(BF16)
        x = x - piece.astype(F32)
        term = jnp.dot(piece, tri, preferred_element_type=F32)
        out = term if out is None else out + term
    return out


def _att_logits(qh, k):
    z = lax.dot_general(qh, k, _MM_TB, preferred_element_type=F32)
    lsp = jnp.minimum(z, 0.0) - jnp.log(1.0 + jnp.exp(-jnp.abs(z)))
    return lsp, lsp - z


def _per_head(heads, a, b):
    return jnp.where(heads[0], a, b)


def sb_attn_fwd(qkv):
    t, d3 = qkv.shape
    d = d3 // 3
    npair = d // LANES
    bq, bk = min(ATT_BQ, t), min(ATT_BK, t)
    kpq = bq // bk

    def body(q_ref, k_ref, v_ref, o_ref, lt_ref, acc_ref):
        heads, rr, cc, kr, kc = _att_consts(bq, bk)
        suffix = (kr > kc).astype(BF16)

        def trip(qh, k0, r0, runs):
            k = k_ref[pl.ds(k0, bk), :]
            v = v_ref[pl.ds(k0, bk), :]
            diag, r0 = r0 is not None, r0 or 0
            valid = cc[:bq - r0] < rr[:bq - r0]
            new_runs = []
            for h in range(2):
                lsp, lraw = _att_logits(qh[h][r0:], k)
                lm = jnp.where(valid, lraw, 0.0) if diag else lraw
                w = jnp.exp(lsp + _split_dot(lm, suffix, 2) + runs[h][r0:])
                if diag:
                    w = jnp.where(valid, w, 0.0)
                acc_ref[h, r0:, :] += jnp.dot(w.astype(BF16), v, preferred_element_type=F32)
                below = runs[h][r0:] + jnp.sum(lm, axis=1, keepdims=True)
                new_runs.append(jnp.concatenate([runs[h][:r0], below], axis=0) if r0 else below)
            return tuple(new_runs)

        def q_loop(qb, _):
            q0 = pl.multiple_of(qb * bq, bq)
            q = q_ref[pl.ds(q0, bq), :] * 0.125
            qh = [jnp.where(hm, q, 0.0).astype(BF16) for hm in heads]
            acc_ref[...] = jnp.zeros_like(acc_ref)
            runs = (jnp.zeros((bq, 1), F32),) * 2
            for dblk in reversed(range(kpq)):
                runs = trip(qh, pl.multiple_of(q0 + dblk * bk, bk), dblk * bk, runs)
            nleft = qb * kpq
            runs = lax.fori_loop(
                0, nleft, lambda i, r: trip(qh, pl.multiple_of((nleft - 1 - i) * bk, bk), None, r), runs)
            o_ref[pl.ds(q0, bq), :] = _per_head(heads, acc_ref[0], acc_ref[1])
            lt_ref[pl.ds(q0, bq), :] = _per_head(heads, runs[0], runs[1])
            return 0

        lax.fori_loop(0, t // bq, q_loop, 0)

    col = lambda off: pl.BlockSpec((t, LANES), lambda p: (0, p + off))
    return pl.pallas_call(
        body, grid=(npair,), in_specs=[col(0), col(npair), col(2 * npair)], out_specs=[col(0), col(0)],
        out_shape=[jax.ShapeDtypeStruct((t, d), F32)] * 2, scratch_shapes=[pltpu.VMEM((2, bq, LANES), F32)],
        name="sb_attn_fwd", compiler_params=_params(("parallel",)),
    )(qkv, qkv, qkv)


def sb_attn_bwd(qkv, ltot, do):
    t, d3 = qkv.shape
    d = d3 // 3
    npair = d // LANES
    bq, bk = min(ATT_BQ, t), min(ATT_BK, t)
    kpq = bq // bk

    def body(q_ref, k_ref, v_ref, lt_ref, do_ref, d_ref, dk_acc, dv_acc, dq_acc):
        heads, rr, cc, kr, kc = _att_consts(bq, bk)
        prefix_incl = (kr <= kc).astype(BF16)
        prefix_excl = (kr < kc).astype(BF16)
        dk_acc[...] = jnp.zeros_like(dk_acc)
        dv_acc[...] = jnp.zeros_like(dv_acc)

        def trip(qh, doh, lt, k0, r0, carry):
            lruns, gruns = carry
            k = k_ref[pl.ds(k0, bk), :]
            v = v_ref[pl.ds(k0, bk), :]
            diag, r0 = r0 is not None, r0 or 0
            valid = cc[:bq - r0] < rr[:bq - r0]
            new_lruns, new_gruns = [], []
            dk_blk = jnp.zeros((bk, LANES), F32)
            dv_blk = jnp.zeros((bk, LANES), F32)
            for h in range(2):
                q_rows, do_rows = qh[h][r0:], doh[h][r0:]
                lsp, lraw = _att_logits(q_rows, k)
                lm = jnp.where(valid, lraw, 0.0) if diag else lraw
                right = lt[h][r0:] - (lruns[h][r0:] + _split_dot(lm, prefix_incl, 2))
                w = jnp.exp(lsp + right)
                if diag:
                    w = jnp.where(valid, w, 0.0)
                g = lax.dot_general(do_rows, v, _MM_TB, preferred_element_type=F32) * w
                left = gruns[h][r0:] + _split_dot(g, prefix_excl, 2)
                dz = g * jnp.exp(lraw) - jnp.exp(lsp) * left
                if diag:
                    dz = jnp.where(valid, dz, 0.0)
                dz = dz.astype(BF16)
                kh = jnp.where(heads[h], k, 0.0).astype(BF16)
                dq_acc[h, r0:, :] += jnp.dot(dz, kh, preferred_element_type=F32)
                dk_blk = dk_blk + lax.dot_general(dz, q_rows, _MM_TA, preferred_element_type=F32)
                dv_blk = dv_blk + lax.dot_general(w.astype(BF16), do_rows, _MM_TA, preferred_element_type=F32)
                l_below = lruns[h][r0:] + jnp.sum(lm, axis=1, keepdims=True)
                g_below = gruns[h][r0:] + jnp.sum(g, axis=1, keepdims=True)
                new_lruns.append(jnp.concatenate([lruns[h][:r0], l_below], axis=0) if r0 else l_below)
                new_gruns.append(jnp.concatenate([gruns[h][:r0], g_below], axis=0) if r0 else g_below)
            dk_acc[pl.ds(k0, bk), :] += dk_blk
            dv_acc[pl.ds(k0, bk), :] += dv_blk
            return tuple(new_lruns), tuple(new_gruns)

        def q_loop(qb, _):
            q0 = pl.multiple_of(qb * bq, bq)
            q = q_ref[pl.ds(q0, bq), :] * 0.125
            dout = do_ref[pl.ds(q0, bq), :]
            lt2 = lt_ref[pl.ds(q0, bq), :]
            qh = [jnp.where(hm, q, 0.0).astype(BF16) for hm in heads]
            doh = [jnp.where(hm, dout, 0.0).astype(BF16) for hm in heads]
            lt = [jnp.max(jnp.where(hm, lt2, -jnp.inf), axis=1, keepdims=True) for hm in heads]
            dq_acc[...] = jnp.zeros_like(dq_acc)
            col = (jnp.zeros((bq, 1), F32),) * 2
            carry = lax.fori_loop(
                0, qb * kpq, lambda kb, c: trip(qh, doh, lt, pl.multiple_of(kb * bk, bk), None, c), (col, col))
            for dblk in range(kpq):
                carry = trip(qh, doh, lt, pl.multiple_of(q0 + dblk * bk, bk), dblk * bk, carry)
            d_ref[0, pl.ds(q0, bq), :] = ((dq_acc[0] + dq_acc[1]) * 0.125).astype(d_ref.dtype)
            return 0

        lax.fori_loop(0, t // bq, q_loop, 0)
        d_ref[1] = dk_acc[...].astype(d_ref.dtype)
        d_ref[2] = dv_acc[...].astype(d_ref.dtype)

    col = lambda off: pl.BlockSpec((t, LANES), lambda p: (0, p + off))
    return pl.pallas_call(
        body, grid=(npair,), in_specs=[col(0), col(npair), col(2 * npair), col(0), col(0)],
        out_specs=pl.BlockSpec((3, t, LANES), lambda p: (0, 0, p)),
        out_shape=jax.ShapeDtypeStruct((3, t, d), BF16),
        scratch_shapes=[pltpu.VMEM((t, LANES), F32), pltpu.VMEM((t, LANES), F32), pltpu.VMEM((2, bq, LANES), F32)],
        name="sb_attn_bwd", compiler_params=_params(("parallel",)),
    )(qkv, qkv, qkv, ltot, do)


def _sgu_parts(hin, g, ws_ref, bf_ref):
    width = hin.shape[1] // 2
    h = jax.nn.gelu(hin)
    u, v = h[:, :width], h[:, width:]
    r = lax.rsqrt(jnp.mean(v * v, axis=-1, keepdims=True) + EPS)
    vn = v * r * g
    rr = lax.broadcasted_iota(jnp.int32, (CHUNK, CHUNK), 0)
    cc = lax.broadcasted_iota(jnp.int32, (CHUNK, CHUNK), 1)
    causal = cc <= rr
    wcs = [jnp.where(causal, ws_ref[gi], 0.0).astype(BF16) for gi in range(SG_GROUPS)]
    sv = jnp.concatenate(
        [jnp.dot(wcs[gi], vn[:, gi * CHUNK:(gi + 1) * CHUNK].astype(BF16), preferred_element_type=F32) + bf_ref[gi]
         for gi in range(SG_GROUPS)], axis=1)
    return u, v, r, vn, wcs, sv, causal


def sgu_fwd(hin, g, ws, bfull):
    t, w2 = hin.shape
    width = w2 // 2

    def body(h_ref, g_ref, ws_ref, bf_ref, o_ref):
        u, _, _, _, _, sv, _ = _sgu_parts(h_ref[...].astype(F32), g_ref[...], ws_ref, bf_ref)
        o_ref[...] = (u * sv).astype(o_ref.dtype)

    full = lambda a: pl.BlockSpec(a.shape, lambda i, nd=a.ndim: (0,) * nd)
    return pl.pallas_call(
        body, grid=(t // CHUNK,), in_specs=[pl.BlockSpec((CHUNK, w2), lambda i: (i, 0)), full(g), full(ws), full(bfull)],
        out_specs=pl.BlockSpec((CHUNK, width), lambda i: (i, 0)), out_shape=jax.ShapeDtypeStruct((t, width), BF16),
        name="sgu_fwd", compiler_params=_params(("parallel",)),
    )(hin, g, ws, bfull)


def sgu_bwd(hin, dp, g, ws, bfull):
    t, w2 = hin.shape
    width = w2 // 2

    def body(h_ref, dp_ref, g_ref, ws_ref, bf_ref, dh_ref, dws_ref, dbf_ref, dg_ref):
        i = pl.program_id(0)
        hin_v = h_ref[...].astype(F32)
        gv = g_ref[...]
        u, v, r, vn, wcs, sv, causal = _sgu_parts(hin_v, gv, ws_ref, bf_ref)
        dpv = dp_ref[...].astype(F32)
        du = dpv * sv
        dsv = dpv * u
        dvn_parts, dws_parts, dbf_parts = [], [], []
        for gi in range(SG_GROUPS):
            dsv_g = dsv[:, gi * CHUNK:(gi + 1) * CHUNK]
            dsv_b = dsv_g.astype(BF16)
            dvn_parts.append(lax.dot_general(wcs[gi], dsv_b, _MM_TA, preferred_element_type=F32))
            vn_b = vn[:, gi * CHUNK:(gi + 1) * CHUNK].astype(BF16)
            dws_parts.append(jnp.where(causal, lax.dot_general(dsv_b, vn_b, _MM_TB, preferred_element_type=F32), 0.0))
            dbf_parts.append(jnp.broadcast_to(jnp.sum(dsv_g, axis=1, keepdims=True), (CHUNK, CHUNK)))
        dvn = jnp.concatenate(dvn_parts, axis=1)
        dgain = jnp.sum(dvn * v * r, axis=0, keepdims=True)
        gvv = dvn * gv
        dv = r * gvv - v * (r * r * r) * jnp.mean(v * gvv, axis=-1, keepdims=True)
        _, vjp = jax.vjp(jax.nn.gelu, hin_v)
        dh_ref[...] = vjp(jnp.concatenate([du, dv], axis=1))[0].astype(dh_ref.dtype)

        @pl.when(i == 0)
        def _():
            for gi in range(SG_GROUPS):
                dws_ref[gi] = dws_parts[gi]
                dbf_ref[gi] = dbf_parts[gi]
            dg_ref[...] = dgain

        @pl.when(i > 0)
        def _():
            for gi in range(SG_GROUPS):
                dws_ref[gi] += dws_parts[gi]
                dbf_ref[gi] += dbf_parts[gi]
            dg_ref[...] += dgain

    full = lambda a: pl.BlockSpec(a.shape, lambda i, nd=a.ndim: (0,) * nd)
    sq = (SG_GROUPS, CHUNK, CHUNK)
    return pl.pallas_call(
        body, grid=(t // CHUNK,),
        in_specs=[pl.BlockSpec((CHUNK, w2), lambda i: (i, 0)), pl.BlockSpec((CHUNK, width), lambda i: (i, 0)),
                  full(g), full(ws), full(bfull)],
        out_specs=[pl.BlockSpec((CHUNK, w2), lambda i: (i, 0)), pl.BlockSpec(sq, lambda i: (0, 0, 0)),
                   pl.BlockSpec(sq, lambda i: (0, 0, 0)), pl.BlockSpec((1, width), lambda i: (0, 0))],
        out_shape=[jax.ShapeDtypeStruct((t, w2), BF16), jax.ShapeDtypeStruct(sq, F32), jax.ShapeDtypeStruct(sq, F32),
                   jax.ShapeDtypeStruct((1, width), F32)],
        name="sgu_bwd", compiler_params=_params(("arbitrary",)),
    )(hin, dp, g, ws, bfull)


def _disc1(lam_re, lam_im, log_dt):
    lr = jnp.minimum(lam_re, -1e-4)
    li = lam_im
    dt = jnp.exp(log_dt)
    mag = jnp.exp(dt * lr)
    ar = mag * jnp.cos(dt * li)
    ai = mag * jnp.sin(dt * li)
    den = lr * lr + li * li
    return ar, ai, ((ar - 1.0) * lr + ai * li) / den, (ai * lr - (ar - 1.0) * li) / den


def _disc2(cre, cim, b_re, b_im):
    return cre * b_re - cim * b_im, cre * b_im + cim * b_re


def _single(fn, ins, out_shapes, name):
    n = len(ins)

    def body(*refs):
        vals = fn(*[r[...] for r in refs[:n]])
        for ref, val in zip(refs[n:], vals):
            ref[...] = val

    return pl.pallas_call(body, out_shape=[jax.ShapeDtypeStruct(s, F32) for s in out_shapes], name=name)(*ins)


SCAN_SEGMENTS = 8
REORDER_STEPS = 64


def s5_reorder(x, to_steps):
    t, d = x.shape
    ns = SCAN_SEGMENTS
    seg = t // ns
    ts = min(REORDER_STEPS, seg)
    by_segment = ((ns, seg, d), pl.BlockSpec((ns, ts, d), lambda i: (0, i, 0)))
    by_step = ((seg, ns, d), pl.BlockSpec((ts, ns, d), lambda i: (i, 0, 0)))
    (in_shape, in_spec), (out_shape, out_spec) = (by_segment, by_step) if to_steps else (by_step, by_segment)

    def body(x_ref, o_ref):
        o_ref[...] = jnp.swapaxes(x_ref[...], 0, 1)

    out = pl.pallas_call(
        body, grid=(seg // ts,), in_specs=[in_spec], out_specs=out_spec,
        out_shape=jax.ShapeDtypeStruct(out_shape, x.dtype), name="s5_reorder", compiler_params=_params(("parallel",)),
    )(x.reshape(in_shape))
    return out.reshape(t, d)


def _cpow(ar, ai, n):
    rr, ri = None, None
    while n:
        if n & 1:
            rr, ri = (ar, ai) if rr is None else (rr * ar - ri * ai, rr * ai + ri * ar)
        ar, ai = ar * ar - ai * ai, 2.0 * ar * ai
        n >>= 1
    return rr, ri


def _edge_states(er, ei, pr, pi, reverse):
    ns = SCAN_SEGMENTS
    zero = jnp.zeros_like(er[0:1])
    rows_r, rows_i = [None] * ns, [None] * ns
    order = range(ns - 1, -1, -1) if reverse else range(ns)
    prev = None
    for s in order:
        if prev is None:
            rows_r[s], rows_i[s] = zero, zero
        else:
            cr, ci = rows_r[prev], rows_i[prev]
            rows_r[s] = er[prev:prev + 1] + pr * cr - pi * ci
            rows_i[s] = ei[prev:prev + 1] + pr * ci + pi * cr
        prev = s
    return jnp.concatenate(rows_r, axis=0), jnp.concatenate(rows_i, axis=0)


def s5_scan_fwd(bu2, a2):
    _, t, n = bu2.shape
    cb, ns = SCAN_COLS, SCAN_SEGMENTS
    seg = t // ns

    def body(bu_ref, a_ref, x_ref):
        ar, ai = a_ref[0:1, :], a_ref[1:2, :]

        def local(i, carry):
            xr, xi = carry
            xr, xi = ar * xr - ai * xi + bu_ref[0, i], ar * xi + ai * xr + bu_ref[1, i]
            x_ref[0, i] = xr
            x_ref[1, i] = xi
            return xr, xi

        zero = jnp.zeros((ns, cb), F32)
        er, ei = lax.fori_loop(0, seg, local, (zero, zero))
        cr, ci = _edge_states(er, ei, *_cpow(ar, ai, seg), reverse=False)

        def fix(i, carry):
            wr, wi = carry
            wr, wi = wr * ar - wi * ai, wr * ai + wi * ar
            x_ref[0, i] += wr * cr - wi * ci
            x_ref[1, i] += wr * ci + wi * cr
            return wr, wi

        lax.fori_loop(0, seg, fix, (jnp.ones((1, cb), F32), jnp.zeros((1, cb), F32)))

    blk = pl.BlockSpec((2, seg, ns, cb), lambda j: (0, 0, 0, j))
    out = pl.pallas_call(
        body, grid=(n // cb,), in_specs=[blk, pl.BlockSpec((2, cb), lambda j: (0, j))], out_specs=blk,
        out_shape=jax.ShapeDtypeStruct((2, seg, ns, n), F32), name="s5_scan_fwd", compiler_params=_params(("parallel",)),
    )(bu2.reshape(2, seg, ns, n), a2)
    return out.reshape(2, t, n)


def s5_scan_bwd(dx2, x2, a2):
    _, t, n = dx2.shape
    cb, ns = SCAN_COLS, SCAN_SEGMENTS
    seg = t // ns

    def body(dx_ref, x_ref, a_ref, g_ref, da_ref):
        ar, ai = a_ref[0:1, :], a_ref[1:2, :]

        def local(s, carry):
            gr, gi = carry
            i = seg - 1 - s
            gr, gi = dx_ref[0, i] + ar * gr + ai * gi, dx_ref[1, i] - ai * gr + ar * gi
            g_ref[0, i] = gr
            g_ref[1, i] = gi
            return gr, gi

        zero = jnp.zeros((ns, cb), F32)
        er, ei = lax.fori_loop(0, seg, local, (zero, zero))
        cr, ci = _edge_states(er, ei, *_cpow(ar, -ai, seg), reverse=True)
        row = lax.broadcasted_iota(jnp.int32, (ns, cb), 0)
        before_r = jnp.where(row == 0, 0.0, pltpu.roll(x_ref[0, seg - 1], 1, 0))
        before_i = jnp.where(row == 0, 0.0, pltpu.roll(x_ref[1, seg - 1], 1, 0))

        def fix(s, carry):
            wr, wi, dar, dai = carry
            i = seg - 1 - s
            wr, wi = wr * ar + wi * ai, wi * ar - wr * ai
            gr = g_ref[0, i] + wr * cr - wi * ci
            gi = g_ref[1, i] + wr * ci + wi * cr
            g_ref[0, i] = gr
            g_ref[1, i] = gi
            ip = jnp.maximum(i - 1, 0)
            xpr = jnp.where(i == 0, before_r, x_ref[0, ip])
            xpi = jnp.where(i == 0, before_i, x_ref[1, ip])
            return wr, wi, dar + gr * xpr + gi * xpi, dai + gi * xpr - gr * xpi

        one, z1 = jnp.ones((1, cb), F32), jnp.zeros((1, cb), F32)
        _, _, dar, dai = lax.fori_loop(0, seg, fix, (one, z1, zero, zero))
        da_ref[0:1, :] = jnp.sum(dar, axis=0, keepdims=True)
        da_ref[1:2, :] = jnp.sum(dai, axis=0, keepdims=True)

    blk = pl.BlockSpec((2, seg, ns, cb), lambda j: (0, 0, 0, j))
    vec = pl.BlockSpec((2, cb), lambda j: (0, j))
    g4, da = pl.pallas_call(
        body, grid=(n // cb,), in_specs=[blk, blk, vec], out_specs=[blk, vec],
        out_shape=[jax.ShapeDtypeStruct((2, seg, ns, n), F32), jax.ShapeDtypeStruct((2, n), F32)],
        name="s5_scan_bwd", compiler_params=_params(("parallel",)),
    )(dx2.reshape(2, seg, ns, n), x2.reshape(2, seg, ns, n), a2)
    return g4.reshape(2, t, n), da


_SP_U = SSM_PACK * SSM_GROUP
_SP_X = SSM_PACK * SSM_STATE
_NKB = SSM_GROUPS // SSM_PACK


def mm_s5(kind, a, b, m, name, res=None, tm=2048):
    tm = min(tm, m)
    kw = dict(passes=S5_PASSES, name=name)
    xblk = lambda row, sel, col: ((None, tm, _SP_X), lambda *g: (sel(*g), row(*g), col(*g)))
    if kind == "bu":
        o_blk, o_map = xblk(lambda g, i, k: i, lambda g, i, k: g // _NKB, lambda g, i, k: g % _NKB)
        return _mm(a, b, grid=(2 * _NKB, m // tm, 1), a_blk=(tm, _SP_U), a_map=lambda g, i, k: (i, g % _NKB),
                   b_blk=(None, None, _SP_U, _SP_X), b_map=lambda g, i, k: (g // _NKB, g % _NKB, 0, 0),
                   o_blk=o_blk, o_map=o_map, out_shape=(2, m, _NKB * _SP_X), out_dtype=F32, **kw)
    if kind == "yc":
        a_blk, a_map = xblk(lambda j, i, k: i, lambda j, i, k: k, lambda j, i, k: j)
        return _mm(a, b, grid=(_NKB, m // tm, 2), a_blk=a_blk, a_map=a_map,
                   b_blk=(None, None, _SP_X, _SP_U), b_map=lambda j, i, k: (k, j, 0, 0),
                   o_blk=(tm, _SP_U), o_map=lambda j, i, k: (i, j), out_shape=(m, _NKB * _SP_U), out_dtype=F32,
                   acc_2d=(tm, _SP_U), **kw)
    if kind == "dx":
        o_blk, o_map = xblk(lambda g, i, k: i, lambda g, i, k: g // _NKB, lambda g, i, k: g % _NKB)
        return _mm(a, b, grid=(2 * _NKB, m // tm, 1), a_blk=(tm, _SP_U), a_map=lambda g, i, k: (i, g % _NKB),
                   b_blk=(None, None, _SP_X, _SP_U), b_map=lambda g, i, k: (g // _NKB, g % _NKB, 0, 0),
                   o_blk=o_blk, o_map=o_map, out_shape=(2, m, _NKB * _SP_X), out_dtype=F32, dims=_MM_TB, **kw)
    if kind == "dcd":
        a_blk, a_map = xblk(lambda g, _, k: k, lambda g, _, k: g // _NKB, lambda g, _, k: g % _NKB)
        return _mm(a, b, grid=(2 * _NKB, 1, m // tm), a_blk=a_blk, a_map=a_map,
                   b_blk=(tm, _SP_U), b_map=lambda g, _, k: (k, g % _NKB),
                   o_blk=(None, None, _SP_X, _SP_U), o_map=lambda g, _, k: (g // _NKB, g % _NKB, 0, 0),
                   out_shape=(2, _NKB, _SP_X, _SP_U), out_dtype=F32, dims=_MM_TA, acc_2d=(_SP_X, _SP_U), **kw)
    if kind == "du":
        a_blk, a_map = xblk(lambda j, i, k: i, lambda j, i, k: k, lambda j, i, k: j)
        return _mm(a, b, grid=(_NKB, m // tm, 2), a_blk=a_blk, a_map=a_map,
                   b_blk=(None, None, _SP_U, _SP_X), b_map=lambda j, i, k: (k, j, 0, 0),
                   o_blk=(tm, _SP_U), o_map=lambda j, i, k: (i, j), out_shape=(m, _NKB * _SP_U), out_dtype=F32,
                   dims=_MM_TB, acc_2d=(tm, _SP_U), res=res, res_blk=(tm, _SP_U), res_map=lambda j, i, k: (i, j), **kw)
    assert kind == "dbd"
    b_blk, b_map = xblk(lambda g, _, k: k, lambda g, _, k: g // _NKB, lambda g, _, k: g % _NKB)
    return _mm(a, b, grid=(2 * _NKB, 1, m // tm), a_blk=(tm, _SP_U), a_map=lambda g, _, k: (k, g % _NKB),
               b_blk=b_blk, b_map=b_map,
               o_blk=(None, None, _SP_U, _SP_X), o_map=lambda g, _, k: (g // _NKB, g % _NKB, 0, 0),
               out_shape=(2, _NKB, _SP_U, _SP_X), out_dtype=F32, dims=_MM_TA, acc_2d=(_SP_U, _SP_X), **kw)


def _block_diag(w):
    g, a, b = w.shape
    eye = jnp.eye(SSM_PACK, dtype=w.dtype)
    wp = w.reshape(g // SSM_PACK, SSM_PACK, a, b)
    return jnp.einsum("kgab,gh->kgahb", wp, eye).reshape(g // SSM_PACK, SSM_PACK * a, SSM_PACK * b)


def _block_diag_t(d, a, b):
    k = d.shape[0]
    eye = jnp.eye(SSM_PACK, dtype=d.dtype)
    dp = d.reshape(k, SSM_PACK, a, SSM_PACK, b)
    return jnp.einsum("kgahb,gh->kgab", dp, eye).reshape(k * SSM_PACK, a, b)


def _coords():
    return lax.axis_index("x"), lax.axis_index("y"), lax.axis_index("c")


def all_gather(tensors, name):
    n = len(tensors)
    any_spec = pl.BlockSpec(memory_space=pl.ANY)

    def body(*refs):
        ins, outs = refs[:n], refs[n:2 * n]
        send, recv, local = refs[2 * n:]
        x, y, c = _coords()
        me, sibling = (x, y, c), (x, y, 1 - c)
        chips = [(1 - x, y), (x, 1 - y), (1 - x, 1 - y)]

        def slot(p):
            return 4 * p[0] + 2 * p[1] + p[2]

        def copy(t, k, block, to, src=None):
            dst = outs[t].at[slot(block)]
            return pltpu.make_async_remote_copy(
                src_ref=dst if src is None else src, dst_ref=dst, send_sem=send.at[7 * t + k],
                recv_sem=recv.at[7 * t + k], device_id=to, device_id_type=pl.DeviceIdType.MESH)

        own, sent = [], []
        for t in range(n):
            mine = pltpu.make_async_copy(ins[t], outs[t].at[slot(me)], local.at[t])
            mine.start()
            own.append(mine)
            first = [copy(t, 0, me, sibling, src=ins[t])]
            first += [copy(t, 1 + j, me, (*chip, c), src=ins[t]) for j, chip in enumerate(chips)]
            for cp in first:
                cp.start()
            sent += first
        for t in range(n):
            for j, chip in enumerate(chips):
                copy(t, 1 + j, (*chip, c), me).wait_recv()
                passed = copy(t, 4 + j, (*chip, c), sibling)
                passed.start()
                sent.append(passed)
        for t in range(n):
            copy(t, 0, sibling, me).wait_recv()
            for j, chip in enumerate(chips):
                copy(t, 4 + j, (*chip, 1 - c), me).wait_recv()
        for cp in sent:
            cp.wait_send()
        for cp in own:
            cp.wait()

    return pl.pallas_call(
        body, in_specs=[any_spec] * n, out_specs=[any_spec] * n,
        out_shape=[jax.ShapeDtypeStruct((NDEV,) + a.shape, a.dtype) for a in tensors],
        scratch_shapes=[pltpu.SemaphoreType.DMA((7 * n,)), pltpu.SemaphoreType.DMA((7 * n,)),
                        pltpu.SemaphoreType.DMA((n,))],
        name=name,
    )(*tensors)


_HBM_SPEC = pl.BlockSpec(memory_space=pltpu.HBM)
_SEM_SPEC = pl.BlockSpec(memory_space=pltpu.SEMAPHORE)
_NPEER = NDEV - 1


def _split_copy_params():
    return pltpu.CompilerParams(has_side_effects=pltpu.SideEffectType.DATAFLOW_SIDE_EFFECTING)


def _me_and_peers():
    x, y, c = _coords()
    peers = []
    for rel in range(1, NDEV):
        p = (1 - x if rel & 4 else x, 1 - y if rel & 2 else y, 1 - c if rel & 1 else c)
        peers.append((p, 4 * p[0] + 2 * p[1] + p[2]))
    return 4 * x + 2 * y + c, peers


def _hbm(a):
    return pltpu.with_memory_space_constraint(a, pltpu.HBM)


def gather_start(bufs, name):
    n = len(bufs)

    def body(*refs):
        ins, outs = refs[:n], refs[n:]
        me, peers = _me_and_peers()
        for t in range(n):
            for k, (dev, _) in enumerate(peers):
                pltpu.make_async_remote_copy(
                    src_ref=ins[t].at[me], dst_ref=ins[t].at[me], send_sem=outs[3 * t].at[k],
                    recv_sem=outs[3 * t + 1].at[k], device_id=dev, device_id_type=pl.DeviceIdType.MESH).start()

    out_shape, out_specs = [], []
    for b in bufs:
        out_shape += [pltpu.SemaphoreType.DMA((_NPEER,)), pltpu.SemaphoreType.DMA((_NPEER,)), pltpu.HBM(b.shape, b.dtype)]
        out_specs += [_SEM_SPEC, _SEM_SPEC, _HBM_SPEC]
    res = pl.pallas_call(
        body, name=name, out_shape=tuple(out_shape), in_specs=[_HBM_SPEC] * n, out_specs=tuple(out_specs),
        input_output_aliases={t: 3 * t + 2 for t in range(n)}, compiler_params=_split_copy_params(),
    )(*[_hbm(b) for b in bufs])
    return [tuple(res[3 * t:3 * t + 3]) for t in range(n)]


def gather_wait(started, after, name):
    n = len(started)

    def body(*refs):
        bufs, sems = refs[:n], refs[n:3 * n]
        me, peers = _me_and_peers()
        for t in range(n):
            for k, (dev, slot) in enumerate(peers):
                cp = pltpu.make_async_remote_copy(
                    src_ref=bufs[t].at[me], dst_ref=bufs[t].at[slot], send_sem=sems[2 * t].at[k],
                    recv_sem=sems[2 * t + 1].at[k], device_id=dev, device_id_type=pl.DeviceIdType.MESH)
                cp.wait_recv()
                cp.wait_send()

    args = [s[2] for s in started] + [sem for s in started for sem in s[:2]] + [after]
    res = pl.pallas_call(
        body, name=name, out_shape=tuple(pltpu.HBM(s[2].shape, s[2].dtype) for s in started),
        in_specs=[_HBM_SPEC] * n + [_SEM_SPEC] * (2 * n) + [pl.BlockSpec(memory_space=pl.ANY)],
        out_specs=tuple([_HBM_SPEC] * n), input_output_aliases={t: t for t in range(n)},
        compiler_params=_split_copy_params(),
    )(*args)
    return list(res)


def scatter_start(srcs, name):
    n = len(srcs)
    lands = [lax.empty((_NPEER,) + s.shape[1:], s.dtype) for s in srcs]

    def body(*refs):
        ins, land_refs, outs = refs[:n], refs[n:2 * n], refs[2 * n:]
        _, peers = _me_and_peers()
        for t in range(n):
            for k, (dev, slot) in enumerate(peers):
                pltpu.make_async_remote_copy(
                    src_ref=ins[t].at[slot], dst_ref=land_refs[t].at[k], send_sem=outs[4 * t].at[k],
                    recv_sem=outs[4 * t + 1].at[k], device_id=dev, device_id_type=pl.DeviceIdType.MESH).start()
        outs[4 * n][...] = jnp.zeros_like(outs[4 * n])

    out_shape, out_specs = [], []
    for s, land in zip(srcs, lands):
        out_shape += [pltpu.SemaphoreType.DMA((_NPEER,)), pltpu.SemaphoreType.DMA((_NPEER,)),
                      pltpu.HBM(s.shape, s.dtype), pltpu.HBM(land.shape, land.dtype)]
        out_specs += [_SEM_SPEC, _SEM_SPEC, _HBM_SPEC, _HBM_SPEC]
    out_shape.append(jax.ShapeDtypeStruct((8, LANES), F32))
    out_specs.append(pl.BlockSpec(memory_space=pltpu.VMEM))
    aliases = {t: 4 * t + 2 for t in range(n)}
    aliases.update({n + t: 4 * t + 3 for t in range(n)})
    res = pl.pallas_call(
        body, name=name, out_shape=tuple(out_shape), in_specs=[_HBM_SPEC] * (2 * n), out_specs=tuple(out_specs),
        input_output_aliases=aliases, compiler_params=_split_copy_params(),
    )(*[_hbm(s) for s in srcs], *[_hbm(land) for land in lands])
    return [tuple(res[4 * t:4 * t + 4]) for t in range(n)], res[4 * n]


def scatter_wait(started, after, name):
    n = len(started)

    def body(*refs):
        srcs, land_refs, sems = refs[:n], refs[n:2 * n], refs[2 * n:4 * n]
        _, peers = _me_and_peers()
        for t in range(n):
            for k, (dev, slot) in enumerate(peers):
                cp = pltpu.make_async_remote_copy(
                    src_ref=srcs[t].at[slot], dst_ref=land_refs[t].at[k], send_sem=sems[2 * t].at[k],
                    recv_sem=sems[2 * t + 1].at[k], device_id=dev, device_id_type=pl.DeviceIdType.MESH)
                cp.wait_recv()
                cp.wait_send()

    args = [s[2] for s in started] + [s[3] for s in started] + [sem for s in started for sem in s[:2]] + [after]
    res = pl.pallas_call(
        body, name=name,
        out_shape=tuple([pltpu.HBM(s[2].shape, s[2].dtype) for s in started]
                        + [pltpu.HBM(s[3].shape, s[3].dtype) for s in started]),
        in_specs=[_HBM_SPEC] * (2 * n) + [_SEM_SPEC] * (2 * n) + [pl.BlockSpec(memory_space=pl.ANY)],
        out_specs=tuple([_HBM_SPEC] * (2 * n)), input_output_aliases={t: t for t in range(2 * n)},
        compiler_params=_split_copy_params(),
    )(*args)
    return [(res[t], res[n + t]) for t in range(n)]


def all_to_all(groups, name):
    flat = [a for grp in groups for a in grp]
    n = len(flat)
    where = [(gi, li) for gi, grp in enumerate(groups) for li in range(len(grp))]
    any_spec = pl.BlockSpec(memory_space=pl.ANY)

    def body(*refs):
        ins, outs = refs[:n], refs[n:n + len(groups)]
        send, recv, local = refs[n + len(groups):]
        x, y, c = _coords()
        me = 4 * x + 2 * y + c
        waits = []
        for e in range(n):
            gi, li = where[e]
            mine = pltpu.make_async_copy(ins[e].at[me], outs[gi].at[me, li], local.at[e])
            mine.start()
            waits.append(mine)
        sent, landing = [], []
        for rel in range(1, NDEV):
            px = 1 - x if rel & 4 else x
            py = 1 - y if rel & 2 else y
            pc = 1 - c if rel & 1 else c
            peer = 4 * px + 2 * py + pc
            for e in range(n):
                gi, li = where[e]

                def copy(dst_slot, e=e, gi=gi, li=li, rel=rel, peer=peer, to=(px, py, pc)):
                    return pltpu.make_async_remote_copy(
                        src_ref=ins[e].at[peer], dst_ref=outs[gi].at[dst_slot, li], send_sem=send.at[7 * e + rel - 1],
                        recv_sem=recv.at[7 * e + rel - 1], device_id=to, device_id_type=pl.DeviceIdType.MESH)

                cp = copy(me)
                cp.start()
                sent.append(cp)
                landing.append(copy(peer))
        for cp in landing:
            cp.wait_recv()
        for cp in sent:
            cp.wait_send()
        for cp in waits:
            cp.wait()

    return pl.pallas_call(
        body, in_specs=[any_spec] * n, out_specs=[any_spec] * len(groups),
        out_shape=[jax.ShapeDtypeStruct((NDEV, len(grp)) + grp[0].shape[1:], grp[0].dtype) for grp in groups],
        scratch_shapes=[pltpu.SemaphoreType.DMA((7 * n,)), pltpu.SemaphoreType.DMA((7 * n,)),
                        pltpu.SemaphoreType.DMA((n,))],
        name=name,
    )(*flat)


_PACK_QUANTUM = 8 * LANES


def _pack(parts, lead=0):
    out = []
    for p in parts:
        head = p.shape[:lead]
        f = p.astype(F32).reshape(head + (-1,))
        pad = (-f.shape[-1]) % _PACK_QUANTUM
        if pad:
            f = jnp.concatenate([f, jnp.zeros(head + (pad,), F32)], axis=-1)
        out.append(f.reshape(head + (-1, LANES)))
    return jnp.concatenate(out, axis=lead)


def _unpack(buf, shapes):
    head = buf.shape[:-2]
    out, r = [], 0
    for s in shapes:
        n = 1
        for v in s:
            n *= v
        nr = -(-n // _PACK_QUANTUM) * 8
        flat = buf[..., r:r + nr, :].reshape(head + (nr * LANES,))[..., :n]
        out.append(flat.reshape(head + tuple(s)))
        r += nr
    return out


BIG = ("sb_w_qkv", "sb_w_o", "sg_w_in", "sg_w_o", "ssm_w_in", "ssm_w_glu", "ffn_w_up", "ffn_w_down")
SMALL_SHARDED = ("norm_g", "ssm_d", "ffn_conv_w")
REPLICATED = ("final_norm_g", "sg_norm_g", "sg_w_s", "sg_b", "ssm_lam_re", "ssm_lam_im", "ssm_log_dt",
              "ssm_b_re", "ssm_b_im", "ssm_c_re", "ssm_c_im", "ffn_conv_b")
WEIGHTS = ("norm_g", "final_norm_g", "sb_w_qkv", "sb_w_o", "sg_w_in", "sg_norm_g", "sg_w_s", "sg_b", "sg_w_o",
           "ssm_w_in", "ssm_lam_re", "ssm_lam_im", "ssm_log_dt", "ssm_b_re", "ssm_b_im", "ssm_c_re", "ssm_c_im",
           "ssm_d", "ssm_w_glu", "ffn_w_up", "ffn_conv_w", "ffn_conv_b", "ffn_w_down")


def _step(x, loss_target, w, m, v):
    t, d = x.shape[1], x.shape[2]
    depth = w["norm_g"].shape[0]
    x0 = x.reshape(t, d)
    tgt = loss_target.reshape(t, d)

    mx, my, mc = _coords()
    me = (4 * mx + 2 * my + mc).astype(jnp.int32).reshape(1)
    shard_pack = _pack([w[k] for k in SMALL_SHARDED])
    gathered_small, = all_gather([shard_pack], name="gather_small_weights")
    mixer_weights = (("sb_w_qkv", "sb_w_o"), ("sg_w_in", "sg_w_o"), ("ssm_w_in", "ssm_w_glu"))
    order = []
    for i in range(depth):
        order += [(k, i // 3) for k in mixer_weights[i % 3]] + [("ffn_w_up", i), ("ffn_w_down", i)]
    pending = dict(zip(order, gather_start([cast_into_slot(w[k], l, me) for k, l in order], "gather_weights_start")))
    wg = {}

    def weights(keys, after):
        missing = [key for key in keys if key not in wg]
        if missing:
            for key, buf in zip(missing, gather_wait([pending[key] for key in missing], after, "gather_weights_wait")):
                wg[key] = buf[:, None]
        return [wg[key] for key in keys]

    ng, sd, cw = _unpack(gathered_small, [w[k].shape for k in SMALL_SHARDED])
    norm_full = jnp.transpose(ng, (1, 2, 0, 3)).reshape(depth, 2, d)
    ssm_d_full = jnp.transpose(sd, (1, 0, 2)).reshape(1, d)
    nc = cw.shape[-1]
    conv_b3 = w["ffn_conv_b"].reshape(depth, NDEV, nc)
    p3 = [jnp.concatenate([cw[:, l], conv_b3[l][:, None, :], jnp.zeros((NDEV, 8 - CONV_K - 1, nc), F32)], axis=1)
          for l in range(depth)]

    g_, p_, h_ = SSM_GROUPS, SSM_STATE, SSM_GROUP
    lam_re, lam_im = w["ssm_lam_re"][0], w["ssm_lam_im"][0]
    log_dt = w["ssm_log_dt"][0].reshape(g_, 1)
    b_re, b_im = w["ssm_b_re"][0].reshape(g_ * p_, h_), w["ssm_b_im"][0].reshape(g_ * p_, h_)
    ar, ai, cre, cim = _single(_disc1, [lam_re, lam_im, log_dt], [(g_, p_)] * 4, "s5_disc1")
    cre_c, cim_c = cre.reshape(g_ * p_, 1), cim.reshape(g_ * p_, 1)
    bbr, bbi = _single(_disc2, [cre_c, cim_c, b_re, b_im], [(g_ * p_, h_)] * 2, "s5_disc2")
    per_group_t = lambda a, r, c: jnp.swapaxes(a.reshape(g_, r, c), 1, 2)
    bd = jnp.stack([_block_diag(per_group_t(bbr, p_, h_)), _block_diag(per_group_t(bbi, p_, h_))])
    cd = jnp.stack([_block_diag(per_group_t(w["ssm_c_re"][0], h_, p_)),
                    -_block_diag(per_group_t(w["ssm_c_im"][0], h_, p_))])
    a2 = jnp.stack([ar.reshape(g_ * p_), ai.reshape(g_ * p_)])

    sg_gain = w["sg_norm_g"]
    sg_ws = w["sg_w_s"][0]
    sg_bfull = jnp.broadcast_to(w["sg_b"][0][:, :, None], sg_ws.shape)

    acts = []
    xc = x0
    xn = rms_fwd(xc, norm_full[0, 0][None], "rms_fwd")
    for i in range(depth):
        mixer, j = i % 3, i // 3
        st = {"x": xc, "xn": xn}
        g0, g1 = norm_full[i, 0][None], norm_full[i, 1][None]
        g_next = norm_full[i + 1, 0][None] if i + 1 < depth else None
        k_in, k_out = [(k, j) for k in mixer_weights[mixer]]
        w_in, = weights([k_in], xn)
        if mixer == 0:
            qkv = mm_cs_fwd(xn, w_in, 0, BF16, "qkv_fwd")
            o, ltot = sb_attn_fwd(qkv)
            w_out, = weights([k_out], o)
            x1, xn2 = mm_rs_fwd(o, w_out, 0, xc, F32, "attn_out_fwd", norm=g1)
            st.update(qkv=qkv, o=o, ltot=ltot)
        elif mixer == 1:
            hin = mm_cs_fwd(xn, w_in, 0, BF16, "sg_in_fwd")
            p = sgu_fwd(hin, sg_gain, sg_ws, sg_bfull)
            w_out, = weights([k_out], p)
            x1, xn2 = mm_rs_fwd(p, w_out, 0, xc, F32, "sg_out_fwd", norm=g1)
            st.update(hin=hin, p=p)
        else:
            u = mm_rs_fwd(xn, w_in, 0, None, F32, "ssm_in_fwd")
            u_s = s5_reorder(u, True)
            x2 = s5_scan_fwd(mm_s5("bu", u_s, bd, t, "s5_bu"), a2)
            yc_s = mm_s5("yc", x2, cd, t, "s5_yc")
            yg = s5_post_fwd(s5_reorder(yc_s, False), u, ssm_d_full)
            w_out, = weights([k_out], yg)
            hg = mm_cs_fwd(yg, w_out, 0, BF16, "ssm_glu_fwd")
            x1, xn2 = glu_fwd(hg, xc, g1)
            st.update(u_s=u_s, x2=x2, yc_s=yc_s, yg=yg, hg=hg)
        w_up, w_down = weights([("ffn_w_up", i), ("ffn_w_down", i)], xn2)
        h3 = mm_cs_fwd(xn2, w_up, 0, BF16, "ffn_up_fwd")
        gated = ffn_gate_fwd(h3, p3[i])
        if g_next is None:
            xc = mm_down_fwd(gated, w_down, 0, x1, "ffn_down_fwd")
        else:
            xc, xn = mm_down_fwd(gated, w_down, 0, x1, "ffn_down_fwd", norm=g_next)
        st.update(x1=x1, xn2=xn2, h3=h3, gated=gated, g0=g0, g1=g1)
        acts.append(st)

    dx, loss_lanes, d_final_g = loss_head(xc, w["final_norm_g"][None], tgt)
    loss = lax.psum(loss_lanes[0, 0], MESH_AXES)

    scattering = {}
    d_norm = [[None, None] for _ in range(depth)]
    d_p3 = [None] * depth
    rep = {}
    d_ssm_d = None
    token = None

    def scatter(grads_by_key):
        keys = list(grads_by_key)
        started, tok = scatter_start([grads_by_key[key] for key in keys], "scatter_grads_start")
        scattering[tuple(keys)] = started
        return tok

    for i in reversed(range(depth)):
        mixer, j = i % 3, i // 3
        st = acts[i]
        k_in, k_out = [(k, j) for k in mixer_weights[mixer]]
        w_in, w_out, w_up, w_down = weights([k_in, k_out, ("ffn_w_up", i), ("ffn_w_down", i)], None)
        dgated = mm_down_da(dx, w_down, 0, "ffn_down_da", dep=token)
        g_down = mm_down_dw(st["gated"], dx, "ffn_down_dw")
        dy_a, dy_g, dp_a, dp_g = ffn_gate_bwd(st["h3"], dgated, p3[i])
        d_p3[i] = jnp.concatenate([dp_a, dp_g], axis=0)
        dh3 = ffn_conv_t(dy_a, dy_g, p3[i])
        dxn2 = mm_cs_da(dh3, w_up, 0, t, "ffn_up_da")
        g_up = mm_cs_dw(st["xn2"], dh3, nc, "ffn_up_dw")
        dx1, d_norm[i][1] = rms_bwd(st["x1"], st["g1"], dxn2, dx, "rms_bwd")
        token = scatter({("ffn_w_down", i): g_down, ("ffn_w_up", i): g_up})
        if mixer == 0:
            do = mm_rs_da(dx1, w_out, 0, BF16, "attn_out_da", dep=token)
            g_out = mm_rs_dw(st["o"], dx1, "attn_out_dw")
            d3 = sb_attn_bwd(st["qkv"], st["ltot"], do)
            g_in = mm_qkv_dw(st["xn"], d3, w_in.shape[3], "qkv_dw")
            token = scatter({k_in: g_in, k_out: g_out})
            dxn = mm_qkv_da(d3, w_in, 0, "qkv_da", dep=token)
        elif mixer == 1:
            dp = mm_rs_da(dx1, w_out, 0, BF16, "sg_out_da", dep=token)
            g_out = mm_rs_dw(st["p"], dx1, "sg_out_dw")
            dhin, d_ws, d_bfull, d_gain = sgu_bwd(st["hin"], dp, sg_gain, sg_ws, sg_bfull)
            rep.update(sg_w_s=d_ws[None], sg_b=d_bfull[None, :, :, 0], sg_norm_g=d_gain)
            dxn = mm_cs_da(dhin, w_in, 0, t, "sg_in_da")
            g_in = mm_cs_dw(st["xn"], dhin, w_in.shape[3], "sg_in_dw")
        else:
            dhg = glu_bwd(st["hg"], dx1, token)
            dyg = mm_cs_da(dhg, w_out, 0, t, "ssm_glu_da")
            g_out = mm_cs_dw(st["yg"], dhg, w_out.shape[3], "ssm_glu_dw")
            dyc_s, du_skip_s, d_ssm_d = s5_post_bwd(st["yc_s"], st["u_s"], ssm_d_full, s5_reorder(dyg, True))
            dx2 = mm_s5("dx", dyc_s, cd, t, "s5_dx")
            dcd = mm_s5("dcd", st["x2"], dyc_s, t, "s5_dcd")
            g2, da2 = s5_scan_bwd(dx2, st["x2"], a2)
            du = s5_reorder(mm_s5("du", g2, bd, t, "s5_du", res=du_skip_s), False)
            dbd = mm_s5("dbd", st["u_s"], g2, t, "s5_dbd")
            from_bd = lambda blk: jnp.swapaxes(_block_diag_t(blk, h_, p_), 1, 2).reshape(g_ * p_, h_)

            def disc2_bwd(c1, c2, b1, b2, t1, t2):
                return jax.vjp(_disc2, c1, c2, b1, b2)[1]((t1, t2))

            d_cre, d_cim, d_b_re, d_b_im = _single(
                disc2_bwd, [cre_c, cim_c, b_re, b_im, from_bd(dbd[0]), from_bd(dbd[1])],
                [(g_ * p_, 1)] * 2 + [(g_ * p_, h_)] * 2, "s5_disc2_bwd")

            def disc1_bwd(l1, l2, ld, t1, t2, t3, t4):
                return jax.vjp(_disc1, l1, l2, ld)[1]((t1, t2, t3, t4))

            d_lam_re, d_lam_im, d_log_dt = _single(
                disc1_bwd, [lam_re, lam_im, log_dt, da2[0].reshape(g_, p_), da2[1].reshape(g_, p_),
                            d_cre.reshape(g_, p_), d_cim.reshape(g_, p_)],
                [(g_, p_), (g_, p_), (g_, 1)], "s5_disc1_bwd")
            from_cd = lambda blk: jnp.swapaxes(_block_diag_t(blk, p_, h_), 1, 2)
            rep.update(ssm_lam_re=d_lam_re[None], ssm_lam_im=d_lam_im[None], ssm_log_dt=d_log_dt.reshape(1, g_),
                       ssm_b_re=d_b_re.reshape(1, g_, p_, h_), ssm_b_im=d_b_im.reshape(1, g_, p_, h_),
                       ssm_c_re=from_cd(dcd[0])[None], ssm_c_im=-from_cd(dcd[1])[None])
            dxn = mm_rs_da(du, w_in, 0, F32, "ssm_in_da")
            g_in = mm_rs_dw(st["xn"], du, "ssm_in_dw")
        dx, d_norm[i][0] = rms_bwd(st["x"], st["g0"], dxn, dx1, "rms_bwd")
        if mixer != 0:
            token = scatter({k_in: g_in, k_out: g_out})

    rep["final_norm_g"] = d_final_g.reshape(d)
    rep["ffn_conv_b"] = jnp.stack([d_p3[l][:, CONV_K, :].reshape(NDEV * nc) for l in range(depth)])

    d_norm_full = jnp.stack([jnp.concatenate(pair, axis=0) for pair in d_norm])
    d_norm_pieces = jnp.transpose(d_norm_full.reshape(depth, 2, NDEV, d // NDEV), (2, 0, 1, 3))
    d_ssm_d_pieces = jnp.transpose(d_ssm_d.reshape(1, NDEV, d // NDEV), (1, 0, 2))
    d_conv_w_pieces = jnp.stack([d_p3[l][:, :CONV_K, :] for l in range(depth)], axis=1)
    small_pieces = _pack([d_norm_pieces, d_ssm_d_pieces, d_conv_w_pieces], lead=1)
    small_received, = all_to_all([[small_pieces]], name="scatter_small_grads")
    rep_parts, = all_gather([_pack([rep[k] for k in REPLICATED]).astype(BF16)], name="gather_small_grads")
    own, landed = {}, {}
    for keys, started in scattering.items():
        for key, (src, land) in zip(keys, scatter_wait(started, dx, "scatter_grads_wait")):
            own[key], landed[key] = src, land

    grads, deltas, new_m, new_v = {}, {}, {}, {}
    for k in BIG:
        layers = range(w[k].shape[0])
        grads[k], deltas[k], new_m[k], new_v[k] = adamw_layers(
            w[k], m[k], v[k], [landed[(k, l)] for l in layers], [own[(k, l)] for l in layers], me, "adamw")
    for names, parts in ((SMALL_SHARDED, small_received[:, 0]), (REPLICATED, rep_parts)):
        res = adamw(_pack([w[k] for k in names]), _pack([m[k] for k in names]), _pack([v[k] for k in names]), parts,
                    "adamw_small")
        for tree, buf in zip((grads, deltas, new_m, new_v), res):
            for k, val in zip(names, _unpack(buf, [w[k].shape for k in names])):
                tree[k] = val
    grad_x = dx.reshape(x.shape)
    return (loss, grad_x, *[grads[k] for k in WEIGHTS], *[deltas[k] for k in WEIGHTS],
            *[new_m[k] for k in WEIGHTS], *[new_v[k] for k in WEIGHTS])


def kernel(x, norm_g, final_norm_g, sb_w_qkv, sb_w_o, sg_w_in, sg_norm_g, sg_w_s, sg_b, sg_w_o, ssm_w_in, ssm_lam_re, ssm_lam_im, ssm_log_dt, ssm_b_re, ssm_b_im, ssm_c_re, ssm_c_im, ssm_d, ssm_w_glu, ffn_w_up, ffn_conv_w, ffn_conv_b, ffn_w_down, loss_target, m_norm_g, m_final_norm_g, m_sb_w_qkv, m_sb_w_o, m_sg_w_in, m_sg_norm_g, m_sg_w_s, m_sg_b, m_sg_w_o, m_ssm_w_in, m_ssm_lam_re, m_ssm_lam_im, m_ssm_log_dt, m_ssm_b_re, m_ssm_b_im, m_ssm_c_re, m_ssm_c_im, m_ssm_d, m_ssm_w_glu, m_ffn_w_up, m_ffn_conv_w, m_ffn_conv_b, m_ffn_w_down, v_norm_g, v_final_norm_g, v_sb_w_qkv, v_sb_w_o, v_sg_w_in, v_sg_norm_g, v_sg_w_s, v_sg_b, v_sg_w_o, v_ssm_w_in, v_ssm_lam_re, v_ssm_lam_im, v_ssm_log_dt, v_ssm_b_re, v_ssm_b_im, v_ssm_c_re, v_ssm_c_im, v_ssm_d, v_ssm_w_glu, v_ffn_w_up, v_ffn_conv_w, v_ffn_conv_b, v_ffn_w_down):
    w = dict(zip(WEIGHTS, (norm_g, final_norm_g, sb_w_qkv, sb_w_o, sg_w_in, sg_norm_g, sg_w_s, sg_b, sg_w_o, ssm_w_in,
                           ssm_lam_re, ssm_lam_im, ssm_log_dt, ssm_b_re, ssm_b_im, ssm_c_re, ssm_c_im, ssm_d, ssm_w_glu,
                           ffn_w_up, ffn_conv_w, ffn_conv_b, ffn_w_down)))
    m = dict(zip(WEIGHTS, (m_norm_g, m_final_norm_g, m_sb_w_qkv, m_sb_w_o, m_sg_w_in, m_sg_norm_g, m_sg_w_s, m_sg_b,
                           m_sg_w_o, m_ssm_w_in, m_ssm_lam_re, m_ssm_lam_im, m_ssm_log_dt, m_ssm_b_re, m_ssm_b_im,
                           m_ssm_c_re, m_ssm_c_im, m_ssm_d, m_ssm_w_glu, m_ffn_w_up, m_ffn_conv_w, m_ffn_conv_b,
                           m_ffn_w_down)))
    v = dict(zip(WEIGHTS, (v_norm_g, v_final_norm_g, v_sb_w_qkv, v_sb_w_o, v_sg_w_in, v_sg_norm_g, v_sg_w_s, v_sg_b,
                           v_sg_w_o, v_ssm_w_in, v_ssm_lam_re, v_ssm_lam_im, v_ssm_log_dt, v_ssm_b_re, v_ssm_b_im,
                           v_ssm_c_re, v_ssm_c_im, v_ssm_d, v_ssm_w_glu, v_ffn_w_up, v_ffn_conv_w, v_ffn_conv_b,
                           v_ffn_w_down)))
    return _step(x, loss_target, w, m, v)
```

```python
import functools

import jax
import jax.numpy as jnp
from jax import lax
from jax.experimental import pallas as pl
from jax.experimental.pallas import tpu as pltpu

F32, BF16 = jnp.float32, jnp.bfloat16
MESH_AXES = ("x", "y", "c")
NDEV = 8
EPS = 1e-6
HEAD_DIM = 64
LANES = 128
ATT_BQ, ATT_BK = 2048, 256
CHUNK = 128
SG_GROUPS = 8
SSM_GROUPS, SSM_STATE, SSM_GROUP = 64, 64, 16
SSM_PACK = 8
S5_PASSES = 1
CONV_K = 3
HALO = 16
ROW_BLOCK = 512
SCAN_COLS = 256
ADAM_LR, ADAM_B1, ADAM_B2, ADAM_EPS, ADAM_WD, ADAM_STEP = 0.001, 0.9, 0.999, 1e-08, 0.01, 10
VMEM_LIMIT = 56 * 1024 * 1024

_MM = (((1,), (0,)), ((), ()))
_MM_TB = (((1,), (1,)), ((), ()))
_MM_TA = (((0,), (0,)), ((), ()))


def _params(sem):
    return pltpu.CompilerParams(dimension_semantics=sem, vmem_limit_bytes=VMEM_LIMIT)


def _rows(total, cap, mult=16):
    best = None
    for d in range(mult, min(total, cap) + 1, mult):
        if total % d == 0:
            best = d
    return best if best is not None else total


def _dot(a, b, dims, passes):
    if passes == 1:
        return lax.dot_general(a.astype(BF16), b.astype(BF16), dims, preferred_element_type=F32)
    a = a.astype(F32)
    b = b.astype(F32)
    ah = a.astype(BF16)
    bh = b.astype(BF16)
    al = (a - ah.astype(F32)).astype(BF16)
    bl = (b - bh.astype(F32)).astype(BF16)
    out = lax.dot_general(ah, bh, dims, preferred_element_type=F32)
    out = out + lax.dot_general(al, bh, dims, preferred_element_type=F32)
    return out + lax.dot_general(ah, bl, dims, preferred_element_type=F32)


def _mm(a, b, *, grid, a_blk, a_map, b_blk, b_map, o_blk, o_map, out_shape, out_dtype, name,
        dims=_MM, passes=1, res=None, res_blk=None, res_map=None, b_2d=None, acc_2d=None, dep=None, norm=None):
    nk = grid[2]
    has_res, has_norm = res is not None, norm is not None
    a_maps = list(a_map) if isinstance(a_map, (list, tuple)) else [a_map]
    b_maps = list(b_map) if isinstance(b_map, (list, tuple)) else [b_map]
    na, nb = len(a_maps), len(b_maps)
    n_in = na + nb + has_res + has_norm + (dep is not None)

    def body(*refs):
        o_ref = refs[n_in]
        r_ref = refs[na + nb] if has_res else None
        av = refs[0][...] if na == 1 else jnp.concatenate([r[...] for r in refs[:na]], axis=-1)
        bv = refs[na][...] if nb == 1 else jnp.concatenate([r[...] for r in refs[na:na + nb]], axis=-1)
        if b_2d is not None:
            bv = bv.reshape(b_2d)
        part = _dot(av, bv, dims, passes)

        def finish(total):
            if has_res:
                total = total + r_ref[...].astype(F32)
            o_ref[...] = total.reshape(o_ref.shape).astype(o_ref.dtype)
            if has_norm:
                refs[n_in + 1][...] = _rms(total, refs[na + nb + has_res][...]).astype(BF16)

        if nk == 1:
            finish(part)
        else:
            acc_ref = refs[-1]
            k = pl.program_id(2)

            @pl.when(k == 0)
            def _():
                acc_ref[...] = part

            @pl.when(k > 0)
            def _():
                acc_ref[...] += part

            @pl.when(k == nk - 1)
            def _():
                finish(acc_ref[...])

    in_specs = [pl.BlockSpec(a_blk, f) for f in a_maps] + [pl.BlockSpec(b_blk, f) for f in b_maps]
    args = [a] * na + [b] * nb
    if has_res:
        in_specs.append(pl.BlockSpec(res_blk, res_map))
        args.append(res)
    if has_norm:
        in_specs.append(pl.BlockSpec(norm.shape, lambda *_: (0, 0)))
        args.append(norm)
    if dep is not None:
        in_specs.append(pl.BlockSpec(memory_space=pl.ANY))
        args.append(dep)
    scratch = [pltpu.VMEM(acc_2d, F32)] if nk > 1 else []
    out_specs, out_shapes = pl.BlockSpec(o_blk, o_map), jax.ShapeDtypeStruct(out_shape, out_dtype)
    if has_norm:
        out_specs, out_shapes = [out_specs] * 2, [out_shapes, jax.ShapeDtypeStruct(out_shape, BF16)]
    return pl.pallas_call(
        body, grid=grid, in_specs=in_specs, out_specs=out_specs, out_shape=out_shapes, scratch_shapes=scratch,
        name=name, compiler_params=_params(("parallel", "parallel", "arbitrary")),
    )(*args)


def _cs_act_spec(ns, tm, row_of, col_of):
    if ns % LANES == 0:
        return (tm, ns), lambda *g: (row_of(*g), col_of(*g))
    return (None, tm, ns), lambda *g: (col_of(*g), row_of(*g), 0)


def mm_cs_fwd(a, w4, l, out_dtype, name, tm=2048):
    m, k = a.shape
    tm = min(tm, m)
    ns = w4.shape[3]
    o_blk, o_map = _cs_act_spec(ns, tm, lambda j, i, kk: i, lambda j, i, kk: j)
    out_shape = (m, NDEV * ns) if ns % LANES == 0 else (NDEV, m, ns)
    return _mm(a, w4, grid=(NDEV, m // tm, 1), a_blk=(tm, k), a_map=lambda j, i, kk: (i, 0),
               b_blk=(None, None, k, ns), b_map=lambda j, i, kk: (j, l, 0, 0),
               o_blk=o_blk, o_map=o_map, out_shape=out_shape, out_dtype=out_dtype, name=name)


def mm_cs_da(dc, w4, l, m, name, tm=1024):
    k, ns = w4.shape[2], w4.shape[3]
    tm = min(tm, m)
    a_blk, a_map = _cs_act_spec(ns, tm, lambda i, _, j: i, lambda i, _, j: j)
    return _mm(dc, w4, grid=(m // tm, 1, NDEV), a_blk=a_blk, a_map=a_map,
               b_blk=(None, None, k, ns), b_map=lambda i, _, j: (j, l, 0, 0),
               o_blk=(tm, k), o_map=lambda i, _, j: (i, 0), out_shape=(m, k), out_dtype=F32,
               dims=_MM_TB, acc_2d=(tm, k), name=name)


def mm_cs_dw(a, dc, ns, name, tk=2048):
    m, k = a.shape
    tk = min(tk, m)
    b_blk, b_map = _cs_act_spec(ns, tk, lambda j, _, kk: kk, lambda j, _, kk: j)
    return _mm(a, dc, grid=(NDEV, 1, m // tk), a_blk=(tk, k), a_map=lambda j, _, kk: (kk, 0),
               b_blk=b_blk, b_map=b_map, o_blk=(None, k, ns), o_map=lambda j, _, kk: (j, 0, 0),
               out_shape=(NDEV, k, ns), out_dtype=BF16, dims=_MM_TA, acc_2d=(k, ns), name=name)


def mm_up_fwd(a, wt4, l, out_dtype, name, tm=2048):
    m, k = a.shape
    tm = min(tm, m)
    ns = wt4.shape[2]
    return _mm(a, wt4, grid=(NDEV, m // tm, 1), a_blk=(tm, k), a_map=lambda j, i, kk: (i, 0),
               b_blk=(None, None, ns, k), b_map=lambda j, i, kk: (j, l, 0, 0), dims=_MM_TB,
               o_blk=(None, tm, ns), o_map=lambda j, i, kk: (j, i, 0), out_shape=(NDEV, m, ns), out_dtype=out_dtype,
               name=name)


def mm_up_da(dc3, wt4, l, name, tm=1024):
    _, m, ns = dc3.shape
    tm = min(tm, m)
    k = wt4.shape[3]
    return _mm(dc3, wt4, grid=(m // tm, 1, NDEV), a_blk=(None, tm, ns), a_map=lambda i, _, j: (j, i, 0),
               b_blk=(None, None, ns, k), b_map=lambda i, _, j: (j, l, 0, 0),
               o_blk=(tm, k), o_map=lambda i, _, j: (i, 0), out_shape=(m, k), out_dtype=F32, acc_2d=(tm, k), name=name)


def mm_up_dw(a, dc3, name, tk=2048):
    m, k = a.shape
    tk = min(tk, m)
    ns = dc3.shape[2]
    return _mm(dc3, a, grid=(NDEV, 1, m // tk), a_blk=(None, tk, ns), a_map=lambda j, _, kk: (j, kk, 0),
               b_blk=(tk, k), b_map=lambda j, _, kk: (kk, 0), dims=_MM_TA,
               o_blk=(None, ns, k), o_map=lambda j, _, kk: (j, 0, 0), out_shape=(NDEV, ns, k), out_dtype=BF16,
               acc_2d=(ns, k), name=name)


def mm_rs_fwd(a, w4, l, res, out_dtype, name, tm=1024, norm=None):
    m, k = a.shape
    tm = min(tm, m)
    ks, n = w4.shape[2], w4.shape[3]
    return _mm(a, w4, grid=(m // tm, 1, 1), a_blk=(tm, k), a_map=lambda i, _, kk: (i, 0),
               b_blk=(NDEV, None, ks, n), b_map=lambda i, _, kk: (0, l, 0, 0), b_2d=(k, n),
               o_blk=(tm, n), o_map=lambda i, _, kk: (i, 0), out_shape=(m, n), out_dtype=out_dtype,
               res=res, res_blk=(tm, n), res_map=lambda i, _, kk: (i, 0), name=name, norm=norm)


def mm_rs_da(dc, w4, l, out_dtype, name, tm=1024, dep=None):
    m, n = dc.shape
    tm = min(tm, m)
    ks = w4.shape[2]
    k = NDEV * ks
    return _mm(dc, w4, grid=(m // tm, 1, 1), a_blk=(tm, n), a_map=lambda i, _, kk: (i, 0),
               b_blk=(NDEV, None, ks, n), b_map=lambda i, _, kk: (0, l, 0, 0), b_2d=(k, n),
               o_blk=(tm, k), o_map=lambda i, _, kk: (i, 0), out_shape=(m, k), out_dtype=out_dtype,
               dims=_MM_TB, name=name, dep=dep)


def mm_rs_dw(a, dc, name, tk=1024):
    m, k = a.shape
    tk = min(tk, m)
    n = dc.shape[1]
    ks = k // NDEV
    return _mm(a, dc, grid=(1, 1, m // tk), a_blk=(tk, k), a_map=lambda _, __, kk: (kk, 0),
               b_blk=(tk, n), b_map=lambda _, __, kk: (kk, 0),
               o_blk=(NDEV, ks, n), o_map=lambda _, __, kk: (0, 0, 0), out_shape=(NDEV, ks, n),
               out_dtype=BF16, dims=_MM_TA, acc_2d=(k, n), name=name)


def mm_down_fwd(a3, w4, l, res, name, tm=1024, norm=None):
    nj, m, kc = a3.shape
    tm = min(tm, m)
    ks, n = w4.shape[2], w4.shape[3]
    return _mm(a3, w4, grid=(m // tm, 1, nj), a_blk=(None, tm, kc), a_map=lambda i, _, j: (j, i, 0),
               b_blk=(2, None, ks, n), b_map=lambda i, _, j: (j, l, 0, 0), b_2d=(kc, n),
               o_blk=(tm, n), o_map=lambda i, _, j: (i, 0), out_shape=(m, n), out_dtype=F32,
               res=res, res_blk=(tm, n), res_map=lambda i, _, j: (i, 0), acc_2d=(tm, n), name=name, norm=norm)


def mm_down_da(dc, w4, l, name, tm=2048, dep=None):
    m, n = dc.shape
    tm = min(tm, m)
    ks = w4.shape[2]
    kc = 2 * ks
    nj = NDEV // 2
    return _mm(dc, w4, grid=(nj, m // tm, 1), a_blk=(tm, n), a_map=lambda j, i, _: (i, 0),
               b_blk=(2, None, ks, n), b_map=lambda j, i, _: (j, l, 0, 0), b_2d=(kc, n),
               o_blk=(None, tm, kc), o_map=lambda j, i, _: (j, i, 0), out_shape=(nj, m, kc),
               out_dtype=BF16, dims=_MM_TB, name=name, dep=dep)


def mm_down_dw(a3, dc, name, tk=2048):
    nj, m, kc = a3.shape
    tk = min(tk, m)
    n = dc.shape[1]
    return _mm(a3, dc, grid=(nj, 1, m // tk), a_blk=(None, tk, kc), a_map=lambda j, _, kk: (j, kk, 0),
               b_blk=(tk, n), b_map=lambda j, _, kk: (kk, 0),
               o_blk=(2, kc // 2, n), o_map=lambda j, _, kk: (j, 0, 0), out_shape=(NDEV, kc // 2, n),
               out_dtype=BF16, dims=_MM_TA, acc_2d=(kc, n), name=name)


def _qkv_group_maps(d, ns, row_of, piece_of):
    per_arr, per_piece = d // LANES, ns // LANES

    def group_map(q):
        def f(*g):
            grp = piece_of(*g) * per_piece + q
            return grp // per_arr, row_of(*g), grp % per_arr
        return f

    return [group_map(q) for q in range(per_piece)]


def mm_qkv_da(d3, w4, l, name, tm=1024, dep=None):
    _, m, d = d3.shape
    tm = min(tm, m)
    k, ns = w4.shape[2], w4.shape[3]
    return _mm(d3, w4, grid=(m // tm, 1, NDEV),
               a_blk=(None, tm, LANES), a_map=_qkv_group_maps(d, ns, lambda i, _, j: i, lambda i, _, j: j),
               b_blk=(None, None, k, ns), b_map=lambda i, _, j: (j, l, 0, 0),
               o_blk=(tm, k), o_map=lambda i, _, j: (i, 0), out_shape=(m, k), out_dtype=F32,
               dims=_MM_TB, acc_2d=(tm, k), name=name, dep=dep)


def mm_qkv_dw(a, d3, ns, name, tk=2048):
    m, k = a.shape
    tk = min(tk, m)
    d = d3.shape[2]
    return _mm(a, d3, grid=(NDEV, 1, m // tk), a_blk=(tk, k), a_map=lambda j, _, kk: (kk, 0),
               b_blk=(None, tk, LANES), b_map=_qkv_group_maps(d, ns, lambda j, _, kk: kk, lambda j, _, kk: j),
               o_blk=(None, k, ns), o_map=lambda j, _, kk: (j, 0, 0),
               out_shape=(NDEV, k, ns), out_dtype=BF16, dims=_MM_TA, acc_2d=(k, ns), name=name)


def _rowwise(fn, ins, outs, *, tr, name, acc_outs=()):
    rows = next(a.shape[0] if kind == "row" else a.shape[1] for a, kind in ins if kind in ("row", "row3"))
    n_in, n_out = len(ins), len(outs)
    n_read = sum(kind != "dep" for _, kind in ins)

    def body(*refs):
        vals = fn(*[r[...] for r in refs[:n_read]])
        if not isinstance(vals, (tuple, list)):
            vals = (vals,)
        for ref, val in zip(refs[n_in:n_in + n_out], vals[:n_out]):
            ref[...] = val.astype(ref.dtype)
        i = pl.program_id(0)
        for ref, val in zip(refs[n_in + n_out:], vals[n_out:]):
            val = val.astype(ref.dtype)

            @pl.when(i == 0)
            def _(ref=ref, val=val):
                ref[...] = val

            @pl.when(i > 0)
            def _(ref=ref, val=val):
                ref[...] += val

    in_specs = []
    for a, kind in ins:
        if kind == "row":
            in_specs.append(pl.BlockSpec((tr, a.shape[1]), lambda i: (i, 0)))
        elif kind == "row3":
            in_specs.append(pl.BlockSpec((a.shape[0], tr, a.shape[2]), lambda i: (0, i, 0)))
        elif kind == "dep":
            in_specs.append(pl.BlockSpec(memory_space=pl.ANY))
        else:
            in_specs.append(pl.BlockSpec(a.shape, lambda i, nd=a.ndim: (0,) * nd))
    out_specs = [pl.BlockSpec((tr, c), lambda i: (i, 0)) for c, _ in outs]
    out_specs += [pl.BlockSpec(s, lambda i, nd=len(s): (0,) * nd) for s, _ in acc_outs]
    out_shape = [jax.ShapeDtypeStruct((rows, c), dt) for c, dt in outs]
    out_shape += [jax.ShapeDtypeStruct(s, dt) for s, dt in acc_outs]
    res = pl.pallas_call(
        body, grid=(rows // tr,), in_specs=in_specs, out_specs=out_specs, out_shape=out_shape, name=name,
        compiler_params=_params(("arbitrary",) if acc_outs else ("parallel",)),
    )(*[a for a, _ in ins])
    return res


def _rms(x, g):
    return x * lax.rsqrt(jnp.mean(x * x, axis=-1, keepdims=True) + EPS) * g


def cast_into_slot(w, l, me):
    _, r, c = w.shape
    tr = _rows(r, 512)

    def body(me_ref, w_ref, o_ref):
        o_ref[...] = w_ref[...].astype(o_ref.dtype)

    return pl.pallas_call(
        body,
        grid_spec=pltpu.PrefetchScalarGridSpec(
            num_scalar_prefetch=1, grid=(r // tr,),
            in_specs=[pl.BlockSpec((None, tr, c), lambda i, me_ref: (l, i, 0))],
            out_specs=pl.BlockSpec((None, tr, c), lambda i, me_ref: (me_ref[0], i, 0))),
        out_shape=jax.ShapeDtypeStruct((NDEV, r, c), BF16), name="cast_into_slot",
        compiler_params=_params(("parallel",)),
    )(me, w)


def rms_fwd(x, g, name):
    out, = _rowwise(_rms, [(x, "row"), (g, "full")], [(x.shape[1], BF16)], tr=ROW_BLOCK, name=name)
    return out


def rms_bwd(x, g, dy, dres, name):
    def fn(xv, gv, dyv, drv):
        _, vjp = jax.vjp(_rms, xv, gv)
        dx, dg = vjp(dyv.astype(F32))
        return drv + dx, dg

    d = x.shape[1]
    return _rowwise(fn, [(x, "row"), (g, "full"), (dy, "row"), (dres, "row")], [(d, F32)], tr=ROW_BLOCK, name=name,
                    acc_outs=[((1, d), F32)])


def loss_head(x, g, tgt):
    def f(xv, gv, tv):
        err = jnp.square(_rms(xv, gv) - tv)
        return 0.5 * jnp.sum(jnp.mean(err, axis=-1))

    def fn(xv, gv, tv):
        val, (dx, dg) = jax.value_and_grad(f, argnums=(0, 1))(xv, gv, tv)
        return dx, jnp.full((1, LANES), val, F32), dg

    d = x.shape[1]
    return _rowwise(fn, [(x, "row"), (g, "full"), (tgt, "row")], [(d, F32)], tr=ROW_BLOCK, name="loss_head",
                    acc_outs=[((1, LANES), F32), ((1, d), F32)])


def _glu(hg, x):
    half = hg.shape[1] // 2
    return x + hg[:, :half] * jax.nn.sigmoid(hg[:, half:])


def glu_fwd(hg, x, norm):
    def fn(h, xv, g):
        x1 = _glu(h.astype(F32), xv)
        return x1, _rms(x1, g)

    d = x.shape[1]
    return _rowwise(fn, [(hg, "row"), (x, "row"), (norm, "full")], [(d, F32), (d, BF16)], tr=ROW_BLOCK, name="glu_fwd")


def glu_bwd(hg, dx1, dep):
    def fn(h, d):
        _, vjp = jax.vjp(lambda hv: _glu(hv, jnp.zeros_like(d)), h.astype(F32))
        return vjp(d)[0]

    out, = _rowwise(fn, [(hg, "row"), (dx1, "row"), (dep, "dep")], [(hg.shape[1], BF16)], tr=ROW_BLOCK, name="glu_bwd")
    return out


def _s5_post(yc, u, d):
    return jax.nn.gelu(yc + d * u)


def s5_post_fwd(yc, u, d):
    out, = _rowwise(_s5_post, [(yc, "row"), (u, "row"), (d, "full")], [(yc.shape[1], BF16)], tr=ROW_BLOCK,
                    name="s5_post_fwd")
    return out


def s5_post_bwd(yc, u, d, dyg):
    def fn(ycv, uv, dv, g):
        _, vjp = jax.vjp(_s5_post, ycv, uv, dv)
        return vjp(g.astype(F32))

    dm = yc.shape[1]
    return _rowwise(fn, [(yc, "row"), (u, "row"), (d, "full"), (dyg, "row")], [(dm, F32), (dm, F32)], tr=ROW_BLOCK,
                    name="s5_post_bwd", acc_outs=[((1, dm), F32)])


def _adam_update(wv, mv, vv, g):
    m2 = ADAM_B1 * mv + (1.0 - ADAM_B1) * g
    v2 = ADAM_B2 * vv + (1.0 - ADAM_B2) * jnp.square(g)
    m_hat = m2 / (1.0 - ADAM_B1 ** ADAM_STEP)
    v_hat = v2 / (1.0 - ADAM_B2 ** ADAM_STEP)
    delta = -ADAM_LR * (m_hat / (jnp.sqrt(v_hat) + ADAM_EPS) + ADAM_WD * wv)
    return g, delta, m2, v2


def adamw(w, m, v, g_parts, name):
    def fn(wv, mv, vv, gp):
        g = gp[0].astype(F32)
        for p in range(1, gp.shape[0]):
            g = g + gp[p].astype(F32)
        return _adam_update(wv, mv, vv, g)

    c = w.shape[1]
    return _rowwise(fn, [(w, "row"), (m, "row"), (v, "row"), (g_parts, "row3")], [(c, F32)] * 4,
                    tr=_rows(w.shape[0], 256), name=name)


def adamw_layers(w, m, v, lands, owns, me, name):
    nl, r, c = w.shape
    tr = _rows(r, 256)

    def body(me_ref, w_ref, m_ref, v_ref, *rest):
        land_refs, own_refs, out_refs = rest[:nl], rest[nl:2 * nl], rest[2 * nl:]
        for l in range(nl):
            @pl.when(pl.program_id(0) == l)
            def _(l=l):
                g = own_refs[l][...].astype(F32)
                for p in range(NDEV - 1):
                    g = g + land_refs[l][p].astype(F32)
                for ref, val in zip(out_refs, _adam_update(w_ref[...], m_ref[...], v_ref[...], g)):
                    ref[...] = val

    def rows_of(l):
        return lambda li, i, me_ref: jnp.where(li == l, i, 0)

    wspec = pl.BlockSpec((None, tr, c), lambda li, i, me_ref: (li, i, 0))
    in_specs = [wspec] * 3
    in_specs += [pl.BlockSpec((NDEV - 1, tr, c), lambda li, i, me_ref, f=rows_of(l): (0, f(li, i, me_ref), 0))
                 for l in range(nl)]
    in_specs += [pl.BlockSpec((None, tr, c), lambda li, i, me_ref, f=rows_of(l): (me_ref[0], f(li, i, me_ref), 0))
                 for l in range(nl)]
    return pl.pallas_call(
        body,
        grid_spec=pltpu.PrefetchScalarGridSpec(
            num_scalar_prefetch=1, grid=(nl, r // tr), in_specs=in_specs, out_specs=[wspec] * 4),
        out_shape=[jax.ShapeDtypeStruct(w.shape, F32)] * 4, name=name, compiler_params=_params(("parallel", "parallel")),
    )(me, w, m, v, *lands, *owns)


def _conv_rows(cur, halo, p, first):
    r = cur.shape[0]
    ext = jnp.concatenate([jnp.where(first, 0.0, halo), cur], axis=0)
    s1 = pltpu.roll(ext, 1, 0)[HALO:]
    s2 = pltpu.roll(ext, 2, 0)[HALO:]
    return p[0:1] * s2 + p[1:2] * s1 + p[2:3] * cur + p[3:4], s1, s2


def ffn_gate_fwd(h3, p3, tr=ROW_BLOCK):
    _, t, c = h3.shape
    half = NDEV // 2

    def body(a_ref, ah_ref, g_ref, gh_ref, pa_ref, pg_ref, o_ref):
        first = pl.program_id(1) == 0
        ya, _, _ = _conv_rows(a_ref[...].astype(F32), ah_ref[...].astype(F32), pa_ref[...], first)
        yg, _, _ = _conv_rows(g_ref[...].astype(F32), gh_ref[...].astype(F32), pg_ref[...], first)
        o_ref[...] = (jax.nn.silu(yg) * ya).astype(o_ref.dtype)

    main = lambda off: pl.BlockSpec((None, tr, c), lambda j, i: (j + off, i, 0))
    halo = lambda off: pl.BlockSpec((None, HALO, c), lambda j, i: (j + off, jnp.maximum(i * (tr // HALO) - 1, 0), 0))
    par = lambda off: pl.BlockSpec((None, 8, c), lambda j, i: (j + off, 0, 0))
    return pl.pallas_call(
        body, grid=(half, t // tr),
        in_specs=[main(0), halo(0), main(half), halo(half), par(0), par(half)],
        out_specs=pl.BlockSpec((None, tr, c), lambda j, i: (j, i, 0)),
        out_shape=jax.ShapeDtypeStruct((half, t, c), BF16), name="ffn_gate_fwd",
        compiler_params=_params(("parallel", "parallel")),
    )(h3, h3, h3, h3, p3, p3)


def ffn_gate_bwd(h3, dgated3, p3, tr=ROW_BLOCK):
    _, t, c = h3.shape
    half = NDEV // 2

    def body(a_ref, ah_ref, g_ref, gh_ref, dg_ref, pa_ref, pg_ref, dya_ref, dyg_ref, dpa_ref, dpg_ref):
        i = pl.program_id(1)
        first = i == 0
        a = a_ref[...].astype(F32)
        g = g_ref[...].astype(F32)
        ya, a1, a2 = _conv_rows(a, ah_ref[...].astype(F32), pa_ref[...], first)
        yg, g1, g2 = _conv_rows(g, gh_ref[...].astype(F32), pg_ref[...], first)
        d = dg_ref[...].astype(F32)
        sig = jax.nn.sigmoid(yg)
        d_ya = (d * (yg * sig)).astype(dya_ref.dtype)
        d_yg = (d * ya * (sig * (1.0 + yg * (1.0 - sig)))).astype(dyg_ref.dtype)
        dya_ref[...] = d_ya
        dyg_ref[...] = d_yg
        for dy, cur, s1, s2, dp_ref in ((d_ya.astype(F32), a, a1, a2, dpa_ref), (d_yg.astype(F32), g, g1, g2, dpg_ref)):
            rows = [jnp.sum(dy * s2, axis=0, keepdims=True), jnp.sum(dy * s1, axis=0, keepdims=True),
                    jnp.sum(dy * cur, axis=0, keepdims=True), jnp.sum(dy, axis=0, keepdims=True)]
            dp = jnp.concatenate(rows + [jnp.zeros((4, c), F32)], axis=0)

            @pl.when(first)
            def _(dp_ref=dp_ref, dp=dp):
                dp_ref[...] = dp

            @pl.when(i > 0)
            def _(dp_ref=dp_ref, dp=dp):
                dp_ref[...] += dp

    main = lambda off: pl.BlockSpec((None, tr, c), lambda j, i: (j + off, i, 0))
    halo = lambda off: pl.BlockSpec((None, HALO, c), lambda j, i: (j + off, jnp.maximum(i * (tr // HALO) - 1, 0), 0))
    par = lambda off: pl.BlockSpec((None, 8, c), lambda j, i: (j + off, 0, 0))
    return pl.pallas_call(
        body, grid=(half, t // tr),
        in_specs=[main(0), halo(0), main(half), halo(half), main(0), par(0), par(half)],
        out_specs=[main(0), main(0), par(0), par(0)],
        out_shape=[jax.ShapeDtypeStruct((half, t, c), BF16)] * 2 + [jax.ShapeDtypeStruct((half, 8, c), F32)] * 2,
        name="ffn_gate_bwd", compiler_params=_params(("parallel", "arbitrary")),
    )(h3, h3, h3, h3, dgated3, p3, p3)


def ffn_conv_t(dy_a, dy_g, p3, tr=2 * ROW_BLOCK):
    half, t, c = dy_a.shape
    tr = min(tr, t)
    nblk = t // tr

    def body(a_ref, ah_ref, g_ref, gh_ref, p_ref, o_ref):
        is_a = pl.program_id(0) < half
        last = pl.program_id(1) == nblk - 1
        cur = jnp.where(is_a, a_ref[...], g_ref[...]).astype(F32)
        nxt = jnp.where(is_a, ah_ref[...], gh_ref[...]).astype(F32)
        ext = jnp.concatenate([cur, jnp.where(last, 0.0, nxt)], axis=0)
        n = tr + HALO
        s1 = pltpu.roll(ext, n - 1, 0)[:tr]
        s2 = pltpu.roll(ext, n - 2, 0)[:tr]
        p = p_ref[...]
        o_ref[...] = (p[2:3] * cur + p[1:2] * s1 + p[0:1] * s2).astype(o_ref.dtype)

    main = pl.BlockSpec((None, tr, c), lambda j, i: (j % half, i, 0))
    halo = pl.BlockSpec((None, HALO, c), lambda j, i: (j % half, jnp.minimum((i + 1) * (tr // HALO), t // HALO - 1), 0))
    return pl.pallas_call(
        body, grid=(NDEV, nblk),
        in_specs=[main, halo, main, halo, pl.BlockSpec((None, 8, c), lambda j, i: (j, 0, 0))],
        out_specs=pl.BlockSpec((None, tr, c), lambda j, i: (j, i, 0)),
        out_shape=jax.ShapeDtypeStruct((NDEV, t, c), BF16), name="ffn_conv_t",
        compiler_params=_params(("parallel", "parallel")),
    )(dy_a, dy_a, dy_g, dy_g, p3)


def _att_consts(bq, bk):
    lane = lax.broadcasted_iota(jnp.int32, (1, LANES), 1)
    heads = (lane < HEAD_DIM, lane >= HEAD_DIM)
    rr = lax.broadcasted_iota(jnp.int32, (bq, bk), 0)
    cc = lax.broadcasted_iota(jnp.int32, (bq, bk), 1)
    kr = lax.broadcasted_iota(jnp.int32, (bk, bk), 0)
    kc = lax.broadcasted_iota(jnp.int32, (bk, bk), 1)
    return heads, rr, cc, kr, kc


def _split_dot(x, tri, parts):
    out = None
    for _ in range(parts):
        piece = x.astype(BF16)
        x = x - piece.astype(F32)
        term = jnp.dot(piece, tri, preferred_element_type=F32)
        out = term if out is None else out + term
    return out


def _att_logits(qh, k):
    z = lax.dot_general(qh, k, _MM_TB, preferred_element_type=F32)
    lsp = jnp.minimum(z, 0.0) - jnp.log(1.0 + jnp.exp(-jnp.abs(z)))
    return lsp, lsp - z


def _per_head(heads, a, b):
    return jnp.where(heads[0], a, b)


def sb_attn_fwd(qkv):
    t, d3 = qkv.shape
    d = d3 // 3
    npair = d // LANES
    bq, bk = min(ATT_BQ, t), min(ATT_BK, t)
    kpq = bq // bk

    def body(q_ref, k_ref, v_ref, o_ref, lt_ref, acc_ref):
        heads, rr, cc, kr, kc = _att_consts(bq, bk)
        suffix = (kr > kc).astype(BF16)

        def trip(qh, k0, r0, runs):
            k = k_ref[pl.ds(k0, bk), :]
            v = v_ref[pl.ds(k0, bk), :]
            diag, r0 = r0 is not None, r0 or 0
            valid = cc[:bq - r0] < rr[:bq - r0]
            new_runs = []
            for h in range(2):
                lsp, lraw = _att_logits(qh[h][r0:], k)
                lm = jnp.where(valid, lraw, 0.0) if diag else lraw
                w = jnp.exp(lsp + _split_dot(lm, suffix, 2) + runs[h][r0:])
                if diag:
                    w = jnp.where(valid, w, 0.0)
                acc_ref[h, r0:, :] += jnp.dot(w.astype(BF16), v, preferred_element_type=F32)
                below = runs[h][r0:] + jnp.sum(lm, axis=1, keepdims=True)
                new_runs.append(jnp.concatenate([runs[h][:r0], below], axis=0) if r0 else below)
            return tuple(new_runs)

        def q_loop(qb, _):
            q0 = pl.multiple_of(qb * bq, bq)
            q = q_ref[pl.ds(q0, bq), :] * 0.125
            qh = [jnp.where(hm, q, 0.0).astype(BF16) for hm in heads]
            acc_ref[...] = jnp.zeros_like(acc_ref)
            runs = (jnp.zeros((bq, 1), F32),) * 2
            for dblk in reversed(range(kpq)):
                runs = trip(qh, pl.multiple_of(q0 + dblk * bk, bk), dblk * bk, runs)
            nleft = qb * kpq
            runs = lax.fori_loop(
                0, nleft, lambda i, r: trip(qh, pl.multiple_of((nleft - 1 - i) * bk, bk), None, r), runs)
            o_ref[pl.ds(q0, bq), :] = _per_head(heads, acc_ref[0], acc_ref[1])
            lt_ref[pl.ds(q0, bq), :] = _per_head(heads, runs[0], runs[1])
            return 0

        lax.fori_loop(0, t // bq, q_loop, 0)

    col = lambda off: pl.BlockSpec((t, LANES), lambda p: (0, p + off))
    return pl.pallas_call(
        body, grid=(npair,), in_specs=[col(0), col(npair), col(2 * npair)], out_specs=[col(0), col(0)],
        out_shape=[jax.ShapeDtypeStruct((t, d), F32)] * 2, scratch_shapes=[pltpu.VMEM((2, bq, LANES), F32)],
        name="sb_attn_fwd", compiler_params=_params(("parallel",)),
    )(qkv, qkv, qkv)


def sb_attn_bwd(qkv, ltot, do):
    t, d3 = qkv.shape
    d = d3 // 3
    npair = d // LANES
    bq, bk = min(ATT_BQ, t), min(ATT_BK, t)
    kpq = bq // bk

    def body(q_ref, k_ref, v_ref, lt_ref, do_ref, d_ref, dk_acc, dv_acc, dq_acc):
        heads, rr, cc, kr, kc = _att_consts(bq, bk)
        prefix_incl = (kr <= kc).astype(BF16)
        prefix_excl = (kr < kc).astype(BF16)
        dk_acc[...] = jnp.zeros_like(dk_acc)
        dv_acc[...] = jnp.zeros_like(dv_acc)

        def trip(qh, doh, lt, k0, r0, carry):
            lruns, gruns = carry
            k = k_ref[pl.ds(k0, bk), :]
            v = v_ref[pl.ds(k0, bk), :]
            diag, r0 = r0 is not None, r0 or 0
            valid = cc[:bq - r0] < rr[:bq - r0]
            new_lruns, new_gruns = [], []
            dk_blk = jnp.zeros((bk, LANES), F32)
            dv_blk = jnp.zeros((bk, LANES), F32)
            for h in range(2):
                q_rows, do_rows = qh[h][r0:], doh[h][r0:]
                lsp, lraw = _att_logits(q_rows, k)
                lm = jnp.where(valid, lraw, 0.0) if diag else lraw
                right = lt[h][r0:] - (lruns[h][r0:] + _split_dot(lm, prefix_incl, 2))
                w = jnp.exp(lsp + right)
                if diag:
                    w = jnp.where(valid, w, 0.0)
                g = lax.dot_general(do_rows, v, _MM_TB, preferred_element_type=F32) * w
                left = gruns[h][r0:] + _split_dot(g, prefix_excl, 2)
                dz = g * jnp.exp(lraw) - jnp.exp(lsp) * left
                if diag:
                    dz = jnp.where(valid, dz, 0.0)
                dz = dz.astype(BF16)
                kh = jnp.where(heads[h], k, 0.0).astype(BF16)
                dq_acc[h, r0:, :] += jnp.dot(dz, kh, preferred_element_type=F32)
                dk_blk = dk_blk + lax.dot_general(dz, q_rows, _MM_TA, preferred_element_type=F32)
                dv_blk = dv_blk + lax.dot_general(w.astype(BF16), do_rows, _MM_TA, preferred_element_type=F32)
                l_below = lruns[h][r0:] + jnp.sum(lm, axis=1, keepdims=True)
                g_below = gruns[h][r0:] + jnp.sum(g, axis=1, keepdims=True)
                new_lruns.append(jnp.concatenate([lruns[h][:r0], l_below], axis=0) if r0 else l_below)
                new_gruns.append(jnp.concatenate([gruns[h][:r0], g_below], axis=0) if r0 else g_below)
            dk_acc[pl.ds(k0, bk), :] += dk_blk
            dv_acc[pl.ds(k0, bk), :] += dv_blk
            return tuple(new_lruns), tuple(new_gruns)

        def q_loop(qb, _):
            q0 = pl.multiple_of(qb * bq, bq)
            q = q_ref[pl.ds(q0, bq), :] * 0.125
            dout = do_ref[pl.ds(q0, bq), :]
            lt2 = lt_ref[pl.ds(q0, bq), :]
            qh = [jnp.where(hm, q, 0.0).astype(BF16) for hm in heads]
            doh = [jnp.where(hm, dout, 0.0).astype(BF16) for hm in heads]
            lt = [jnp.max(jnp.where(hm, lt2, -jnp.inf), axis=1, keepdims=True) for hm in heads]
            dq_acc[...] = jnp.zeros_like(dq_acc)
            col = (jnp.zeros((bq, 1), F32),) * 2
            carry = lax.fori_loop(
                0, qb * kpq, lambda kb, c: trip(qh, doh, lt, pl.multiple_of(kb * bk, bk), None, c), (col, col))
            for dblk in range(kpq):
                carry = trip(qh, doh, lt, pl.multiple_of(q0 + dblk * bk, bk), dblk * bk, carry)
            d_ref[0, pl.ds(q0, bq), :] = ((dq_acc[0] + dq_acc[1]) * 0.125).astype(d_ref.dtype)
            return 0

        lax.fori_loop(0, t // bq, q_loop, 0)
        d_ref[1] = dk_acc[...].astype(d_ref.dtype)
        d_ref[2] = dv_acc[...].astype(d_ref.dtype)

    col = lambda off: pl.BlockSpec((t, LANES), lambda p: (0, p + off))
    return pl.pallas_call(
        body, grid=(npair,), in_specs=[col(0), col(npair), col(2 * npair), col(0), col(0)],
        out_specs=pl.BlockSpec((3, t, LANES), lambda p: (0, 0, p)),
        out_shape=jax.ShapeDtypeStruct((3, t, d), BF16),
        scratch_shapes=[pltpu.VMEM((t, LANES), F32), pltpu.VMEM((t, LANES), F32), pltpu.VMEM((2, bq, LANES), F32)],
        name="sb_attn_bwd", compiler_params=_params(("parallel",)),
    )(qkv, qkv, qkv, ltot, do)


def _sgu_parts(hin, g, ws_ref, bf_ref):
    width = hin.shape[1] // 2
    h = jax.nn.gelu(hin)
    u, v = h[:, :width], h[:, width:]
    r = lax.rsqrt(jnp.mean(v * v, axis=-1, keepdims=True) + EPS)
    vn = v * r * g
    rr = lax.broadcasted_iota(jnp.int32, (CHUNK, CHUNK), 0)
    cc = lax.broadcasted_iota(jnp.int32, (CHUNK, CHUNK), 1)
    causal = cc <= rr
    wcs = [jnp.where(causal, ws_ref[gi], 0.0).astype(BF16) for gi in range(SG_GROUPS)]
    sv = jnp.concatenate(
        [jnp.dot(wcs[gi], vn[:, gi * CHUNK:(gi + 1) * CHUNK].astype(BF16), preferred_element_type=F32) + bf_ref[gi]
         for gi in range(SG_GROUPS)], axis=1)
    return u, v, r, vn, wcs, sv, causal


def sgu_fwd(hin, g, ws, bfull):
    t, w2 = hin.shape
    width = w2 // 2

    def body(h_ref, g_ref, ws_ref, bf_ref, o_ref):
        u, _, _, _, _, sv, _ = _sgu_parts(h_ref[...].astype(F32), g_ref[...], ws_ref, bf_ref)
        o_ref[...] = (u * sv).astype(o_ref.dtype)

    full = lambda a: pl.BlockSpec(a.shape, lambda i, nd=a.ndim: (0,) * nd)
    return pl.pallas_call(
        body, grid=(t // CHUNK,), in_specs=[pl.BlockSpec((CHUNK, w2), lambda i: (i, 0)), full(g), full(ws), full(bfull)],
        out_specs=pl.BlockSpec((CHUNK, width), lambda i: (i, 0)), out_shape=jax.ShapeDtypeStruct((t, width), BF16),
        name="sgu_fwd", compiler_params=_params(("parallel",)),
    )(hin, g, ws, bfull)


def sgu_bwd(hin, dp, g, ws, bfull):
    t, w2 = hin.shape
    width = w2 // 2

    def body(h_ref, dp_ref, g_ref, ws_ref, bf_ref, dh_ref, dws_ref, dbf_ref, dg_ref):
        i = pl.program_id(0)
        hin_v = h_ref[...].astype(F32)
        gv = g_ref[...]
        u, v, r, vn, wcs, sv, causal = _sgu_parts(hin_v, gv, ws_ref, bf_ref)
        dpv = dp_ref[...].astype(F32)
        du = dpv * sv
        dsv = dpv * u
        dvn_parts, dws_parts, dbf_parts = [], [], []
        for gi in range(SG_GROUPS):
            dsv_g = dsv[:, gi * CHUNK:(gi + 1) * CHUNK]
            dsv_b = dsv_g.astype(BF16)
            dvn_parts.append(lax.dot_general(wcs[gi], dsv_b, _MM_TA, preferred_element_type=F32))
            vn_b = vn[:, gi * CHUNK:(gi + 1) * CHUNK].astype(BF16)
            dws_parts.append(jnp.where(causal, lax.dot_general(dsv_b, vn_b, _MM_TB, preferred_element_type=F32), 0.0))
            dbf_parts.append(jnp.broadcast_to(jnp.sum(dsv_g, axis=1, keepdims=True), (CHUNK, CHUNK)))
        dvn = jnp.concatenate(dvn_parts, axis=1)
        dgain = jnp.sum(dvn * v * r, axis=0, keepdims=True)
        gvv = dvn * gv
        dv = r * gvv - v * (r * r * r) * jnp.mean(v * gvv, axis=-1, keepdims=True)
        _, vjp = jax.vjp(jax.nn.gelu, hin_v)
        dh_ref[...] = vjp(jnp.concatenate([du, dv], axis=1))[0].astype(dh_ref.dtype)

        @pl.when(i == 0)
        def _():
            for gi in range(SG_GROUPS):
                dws_ref[gi] = dws_parts[gi]
                dbf_ref[gi] = dbf_parts[gi]
            dg_ref[...] = dgain

        @pl.when(i > 0)
        def _():
            for gi in range(SG_GROUPS):
                dws_ref[gi] += dws_parts[gi]
                dbf_ref[gi] += dbf_parts[gi]
            dg_ref[...] += dgain

    full = lambda a: pl.BlockSpec(a.shape, lambda i, nd=a.ndim: (0,) * nd)
    sq = (SG_GROUPS, CHUNK, CHUNK)
    return pl.pallas_call(
        body, grid=(t // CHUNK,),
        in_specs=[pl.BlockSpec((CHUNK, w2), lambda i: (i, 0)), pl.BlockSpec((CHUNK, width), lambda i: (i, 0)),
                  full(g), full(ws), full(bfull)],
        out_specs=[pl.BlockSpec((CHUNK, w2), lambda i: (i, 0)), pl.BlockSpec(sq, lambda i: (0, 0, 0)),
                   pl.BlockSpec(sq, lambda i: (0, 0, 0)), pl.BlockSpec((1, width), lambda i: (0, 0))],
        out_shape=[jax.ShapeDtypeStruct((t, w2), BF16), jax.ShapeDtypeStruct(sq, F32), jax.ShapeDtypeStruct(sq, F32),
                   jax.ShapeDtypeStruct((1, width), F32)],
        name="sgu_bwd", compiler_params=_params(("arbitrary",)),
    )(hin, dp, g, ws, bfull)


def _disc1(lam_re, lam_im, log_dt):
    lr = jnp.minimum(lam_re, -1e-4)
    li = lam_im
    dt = jnp.exp(log_dt)
    mag = jnp.exp(dt * lr)
    ar = mag * jnp.cos(dt * li)
    ai = mag * jnp.sin(dt * li)
    den = lr * lr + li * li
    return ar, ai, ((ar - 1.0) * lr + ai * li) / den, (ai * lr - (ar - 1.0) * li) / den


def _disc2(cre, cim, b_re, b_im):
    return cre * b_re - cim * b_im, cre * b_im + cim * b_re


def _single(fn, ins, out_shapes, name):
    n = len(ins)

    def body(*refs):
        vals = fn(*[r[...] for r in refs[:n]])
        for ref, val in zip(refs[n:], vals):
            ref[...] = val

    return pl.pallas_call(body, out_shape=[jax.ShapeDtypeStruct(s, F32) for s in out_shapes], name=name)(*ins)


SCAN_SEGMENTS = 8
REORDER_STEPS = 64


def s5_reorder(x, to_steps):
    t, d = x.shape
    ns = SCAN_SEGMENTS
    seg = t // ns
    ts = min(REORDER_STEPS, seg)
    by_segment = ((ns, seg, d), pl.BlockSpec((ns, ts, d), lambda i: (0, i, 0)))
    by_step = ((seg, ns, d), pl.BlockSpec((ts, ns, d), lambda i: (i, 0, 0)))
    (in_shape, in_spec), (out_shape, out_spec) = (by_segment, by_step) if to_steps else (by_step, by_segment)

    def body(x_ref, o_ref):
        o_ref[...] = jnp.swapaxes(x_ref[...], 0, 1)

    out = pl.pallas_call(
        body, grid=(seg // ts,), in_specs=[in_spec], out_specs=out_spec,
        out_shape=jax.ShapeDtypeStruct(out_shape, x.dtype), name="s5_reorder", compiler_params=_params(("parallel",)),
    )(x.reshape(in_shape))
    return out.reshape(t, d)


def _cpow(ar, ai, n):
    rr, ri = None, None
    while n:
        if n & 1:
            rr, ri = (ar, ai) if rr is None else (rr * ar - ri * ai, rr * ai + ri * ar)
        ar, ai = ar * ar - ai * ai, 2.0 * ar * ai
        n >>= 1
    return rr, ri


def _edge_states(er, ei, pr, pi, reverse):
    ns = SCAN_SEGMENTS
    zero = jnp.zeros_like(er[0:1])
    rows_r, rows_i = [None] * ns, [None] * ns
    order = range(ns - 1, -1, -1) if reverse else range(ns)
    prev = None
    for s in order:
        if prev is None:
            rows_r[s], rows_i[s] = zero, zero
        else:
            cr, ci = rows_r[prev], rows_i[prev]
            rows_r[s] = er[prev:prev + 1] + pr * cr - pi * ci
            rows_i[s] = ei[prev:prev + 1] + pr * ci + pi * cr
        prev = s
    return jnp.concatenate(rows_r, axis=0), jnp.concatenate(rows_i, axis=0)


def s5_scan_fwd(bu2, a2):
    _, t, n = bu2.shape
    cb, ns = SCAN_COLS, SCAN_SEGMENTS
    seg = t // ns

    def body(bu_ref, a_ref, x_ref):
        ar, ai = a_ref[0:1, :], a_ref[1:2, :]

        def local(i, carry):
            xr, xi = carry
            xr, xi = ar * xr - ai * xi + bu_ref[0, i], ar * xi + ai * xr + bu_ref[1, i]
            x_ref[0, i] = xr
            x_ref[1, i] = xi
            return xr, xi

        zero = jnp.zeros((ns, cb), F32)
        er, ei = lax.fori_loop(0, seg, local, (zero, zero))
        cr, ci = _edge_states(er, ei, *_cpow(ar, ai, seg), reverse=False)

        def fix(i, carry):
            wr, wi = carry
            wr, wi = wr * ar - wi * ai, wr * ai + wi * ar
            x_ref[0, i] += wr * cr - wi * ci
            x_ref[1, i] += wr * ci + wi * cr
            return wr, wi

        lax.fori_loop(0, seg, fix, (jnp.ones((1, cb), F32), jnp.zeros((1, cb), F32)))

    blk = pl.BlockSpec((2, seg, ns, cb), lambda j: (0, 0, 0, j))
    out = pl.pallas_call(
        body, grid=(n // cb,), in_specs=[blk, pl.BlockSpec((2, cb), lambda j: (0, j))], out_specs=blk,
        out_shape=jax.ShapeDtypeStruct((2, seg, ns, n), F32), name="s5_scan_fwd", compiler_params=_params(("parallel",)),
    )(bu2.reshape(2, seg, ns, n), a2)
    return out.reshape(2, t, n)


def s5_scan_bwd(dx2, x2, a2):
    _, t, n = dx2.shape
    cb, ns = SCAN_COLS, SCAN_SEGMENTS
    seg = t // ns

    def body(dx_ref, x_ref, a_ref, g_ref, da_ref):
        ar, ai = a_ref[0:1, :], a_ref[1:2, :]

        def local(s, carry):
            gr, gi = carry
            i = seg - 1 - s
            gr, gi = dx_ref[0, i] + ar * gr + ai * gi, dx_ref[1, i] - ai * gr + ar * gi
            g_ref[0, i] = gr
            g_ref[1, i] = gi
            return gr, gi

        zero = jnp.zeros((ns, cb), F32)
        er, ei = lax.fori_loop(0, seg, local, (zero, zero))
        cr, ci = _edge_states(er, ei, *_cpow(ar, -ai, seg), reverse=True)
        row = lax.broadcasted_iota(jnp.int32, (ns, cb), 0)
        before_r = jnp.where(row == 0, 0.0, pltpu.roll(x_ref[0, seg - 1], 1, 0))
        before_i = jnp.where(row == 0, 0.0, pltpu.roll(x_ref[1, seg - 1], 1, 0))

        def fix(s, carry):
            wr, wi, dar, dai = carry
            i = seg - 1 - s
            wr, wi = wr * ar + wi * ai, wi * ar - wr * ai
            gr = g_ref[0, i] + wr * cr - wi * ci
            gi = g_ref[1, i] + wr * ci + wi * cr
            g_ref[0, i] = gr
            g_ref[1, i] = gi
            ip = jnp.maximum(i - 1, 0)
            xpr = jnp.where(i == 0, before_r, x_ref[0, ip])
            xpi = jnp.where(i == 0, before_i, x_ref[1, ip])
            return wr, wi, dar + gr * xpr + gi * xpi, dai + gi * xpr - gr * xpi

        one, z1 = jnp.ones((1, cb), F32), jnp.zeros((1, cb), F32)
        _, _, dar, dai = lax.fori_loop(0, seg, fix, (one, z1, zero, zero))
        da_ref[0:1, :] = jnp.sum(dar, axis=0, keepdims=True)
        da_ref[1:2, :] = jnp.sum(dai, axis=0, keepdims=True)

    blk = pl.BlockSpec((2, seg, ns, cb), lambda j: (0, 0, 0, j))
    vec = pl.BlockSpec((2, cb), lambda j: (0, j))
    g4, da = pl.pallas_call(
        body, grid=(n // cb,), in_specs=[blk, blk, vec], out_specs=[blk, vec],
        out_shape=[jax.ShapeDtypeStruct((2, seg, ns, n), F32), jax.ShapeDtypeStruct((2, n), F32)],
        name="s5_scan_bwd", compiler_params=_params(("parallel",)),
    )(dx2.reshape(2, seg, ns, n), x2.reshape(2, seg, ns, n), a2)
    return g4.reshape(2, t, n), da


_SP_U = SSM_PACK * SSM_GROUP
_SP_X = SSM_PACK * SSM_STATE
_NKB = SSM_GROUPS // SSM_PACK


def mm_s5(kind, a, b, m, name, res=None, tm=2048):
    tm = min(tm, m)
    kw = dict(passes=S5_PASSES, name=name)
    xblk = lambda row, sel, col: ((None, tm, _SP_X), lambda *g: (sel(*g), row(*g), col(*g)))
    if kind == "bu":
        o_blk, o_map = xblk(lambda g, i, k: i, lambda g, i, k: g // _NKB, lambda g, i, k: g % _NKB)
        return _mm(a, b, grid=(2 * _NKB, m // tm, 1), a_blk=(tm, _SP_U), a_map=lambda g, i, k: (i, g % _NKB),
                   b_blk=(None, None, _SP_U, _SP_X), b_map=lambda g, i, k: (g // _NKB, g % _NKB, 0, 0),
                   o_blk=o_blk, o_map=o_map, out_shape=(2, m, _NKB * _SP_X), out_dtype=F32, **kw)
    if kind == "yc":
        a_blk, a_map = xblk(lambda j, i, k: i, lambda j, i, k: k, lambda j, i, k: j)
        return _mm(a, b, grid=(_NKB, m // tm, 2), a_blk=a_blk, a_map=a_map,
                   b_blk=(None, None, _SP_X, _SP_U), b_map=lambda j, i, k: (k, j, 0, 0),
                   o_blk=(tm, _SP_U), o_map=lambda j, i, k: (i, j), out_shape=(m, _NKB * _SP_U), out_dtype=F32,
                   acc_2d=(tm, _SP_U), **kw)
    if kind == "dx":
        o_blk, o_map = xblk(lambda g, i, k: i, lambda g, i, k: g // _NKB, lambda g, i, k: g % _NKB)
        return _mm(a, b, grid=(2 * _NKB, m // tm, 1), a_blk=(tm, _SP_U), a_map=lambda g, i, k: (i, g % _NKB),
                   b_blk=(None, None, _SP_X, _SP_U), b_map=lambda g, i, k: (g // _NKB, g % _NKB, 0, 0),
                   o_blk=o_blk, o_map=o_map, out_shape=(2, m, _NKB * _SP_X), out_dtype=F32, dims=_MM_TB, **kw)
    if kind == "dcd":
        a_blk, a_map = xblk(lambda g, _, k: k, lambda g, _, k: g // _NKB, lambda g, _, k: g % _NKB)
        return _mm(a, b, grid=(2 * _NKB, 1, m // tm), a_blk=a_blk, a_map=a_map,
                   b_blk=(tm, _SP_U), b_map=lambda g, _, k: (k, g % _NKB),
                   o_blk=(None, None, _SP_X, _SP_U), o_map=lambda g, _, k: (g // _NKB, g % _NKB, 0, 0),
                   out_shape=(2, _NKB, _SP_X, _SP_U), out_dtype=F32, dims=_MM_TA, acc_2d=(_SP_X, _SP_U), **kw)
    if kind == "du":
        a_blk, a_map = xblk(lambda j, i, k: i, lambda j, i, k: k, lambda j, i, k: j)
        return _mm(a, b, grid=(_NKB, m // tm, 2), a_blk=a_blk, a_map=a_map,
                   b_blk=(None, None, _SP_U, _SP_X), b_map=lambda j, i, k: (k, j, 0, 0),
                   o_blk=(tm, _SP_U), o_map=lambda j, i, k: (i, j), out_shape=(m, _NKB * _SP_U), out_dtype=F32,
                   dims=_MM_TB, acc_2d=(tm, _SP_U), res=res, res_blk=(tm, _SP_U), res_map=lambda j, i, k: (i, j), **kw)
    assert kind == "dbd"
    b_blk, b_map = xblk(lambda g, _, k: k, lambda g, _, k: g // _NKB, lambda g, _, k: g % _NKB)
    return _mm(a, b, grid=(2 * _NKB, 1, m // tm), a_blk=(tm, _SP_U), a_map=lambda g, _, k: (k, g % _NKB),
               b_blk=b_blk, b_map=b_map,
               o_blk=(None, None, _SP_U, _SP_X), o_map=lambda g, _, k: (g // _NKB, g % _NKB, 0, 0),
               out_shape=(2, _NKB, _SP_U, _SP_X), out_dtype=F32, dims=_MM_TA, acc_2d=(_SP_U, _SP_X), **kw)


def _block_diag(w):
    g, a, b = w.shape
    eye = jnp.eye(SSM_PACK, dtype=w.dtype)
    wp = w.reshape(g // SSM_PACK, SSM_PACK, a, b)
    return jnp.einsum("kgab,gh->kgahb", wp, eye).reshape(g // SSM_PACK, SSM_PACK * a, SSM_PACK * b)


def _block_diag_t(d, a, b):
    k = d.shape[0]
    eye = jnp.eye(SSM_PACK, dtype=d.dtype)
    dp = d.reshape(k, SSM_PACK, a, SSM_PACK, b)
    return jnp.einsum("kgahb,gh->kgab", dp, eye).reshape(k * SSM_PACK, a, b)


def _coords():
    return lax.axis_index("x"), lax.axis_index("y"), lax.axis_index("c")


def all_gather(tensors, name):
    n = len(tensors)
    any_spec = pl.BlockSpec(memory_space=pl.ANY)

    def body(*refs):
        ins, outs = refs[:n], refs[n:2 * n]
        send, recv, local = refs[2 * n:]
        x, y, c = _coords()
        me, sibling = (x, y, c), (x, y, 1 - c)
        chips = [(1 - x, y), (x, 1 - y), (1 - x, 1 - y)]

        def slot(p):
            return 4 * p[0] + 2 * p[1] + p[2]

        def copy(t, k, block, to, src=None):
            dst = outs[t].at[slot(block)]
            return pltpu.make_async_remote_copy(
                src_ref=dst if src is None else src, dst_ref=dst, send_sem=send.at[7 * t + k],
                recv_sem=recv.at[7 * t + k], device_id=to, device_id_type=pl.DeviceIdType.MESH)

        own, sent = [], []
        for t in range(n):
            mine = pltpu.make_async_copy(ins[t], outs[t].at[slot(me)], local.at[t])
            mine.start()
            own.append(mine)
            first = [copy(t, 0, me, sibling, src=ins[t])]
            first += [copy(t, 1 + j, me, (*chip, c), src=ins[t]) for j, chip in enumerate(chips)]
            for cp in first:
                cp.start()
            sent += first
        for t in range(n):
            for j, chip in enumerate(chips):
                copy(t, 1 + j, (*chip, c), me).wait_recv()
                passed = copy(t, 4 + j, (*chip, c), sibling)
                passed.start()
                sent.append(passed)
        for t in range(n):
            copy(t, 0, sibling, me).wait_recv()
            for j, chip in enumerate(chips):
                copy(t, 4 + j, (*chip, 1 - c), me).wait_recv()
        for cp in sent:
            cp.wait_send()
        for cp in own:
            cp.wait()

    return pl.pallas_call(
        body, in_specs=[any_spec] * n, out_specs=[any_spec] * n,
        out_shape=[jax.ShapeDtypeStruct((NDEV,) + a.shape, a.dtype) for a in tensors],
        scratch_shapes=[pltpu.SemaphoreType.DMA((7 * n,)), pltpu.SemaphoreType.DMA((7 * n,)),
                        pltpu.SemaphoreType.DMA((n,))],
        name=name,
    )(*tensors)


_HBM_SPEC = pl.BlockSpec(memory_space=pltpu.HBM)
_SEM_SPEC = pl.BlockSpec(memory_space=pltpu.SEMAPHORE)
_NPEER = NDEV - 1


def _split_copy_params():
    return pltpu.CompilerParams(has_side_effects=pltpu.SideEffectType.DATAFLOW_SIDE_EFFECTING)


def _me_and_peers():
    x, y, c = _coords()
    peers = []
    for rel in range(1, NDEV):
        p = (1 - x if rel & 4 else x, 1 - y if rel & 2 else y, 1 - c if rel & 1 else c)
        peers.append((p, 4 * p[0] + 2 * p[1] + p[2]))
    return 4 * x + 2 * y + c, peers


def _hbm(a):
    return pltpu.with_memory_space_constraint(a, pltpu.HBM)


def gather_start(bufs, name):
    n = len(bufs)

    def body(*refs):
        ins, outs = refs[:n], refs[n:]
        me, peers = _me_and_peers()
        for t in range(n):
            for k, (dev, _) in enumerate(peers):
                pltpu.make_async_remote_copy(
                    src_ref=ins[t].at[me], dst_ref=ins[t].at[me], send_sem=outs[3 * t].at[k],
                    recv_sem=outs[3 * t + 1].at[k], device_id=dev, device_id_type=pl.DeviceIdType.MESH).start()

    out_shape, out_specs = [], []
    for b in bufs:
        out_shape += [pltpu.SemaphoreType.DMA((_NPEER,)), pltpu.SemaphoreType.DMA((_NPEER,)), pltpu.HBM(b.shape, b.dtype)]
        out_specs += [_SEM_SPEC, _SEM_SPEC, _HBM_SPEC]
    res = pl.pallas_call(
        body, name=name, out_shape=tuple(out_shape), in_specs=[_HBM_SPEC] * n, out_specs=tuple(out_specs),
        input_output_aliases={t: 3 * t + 2 for t in range(n)}, compiler_params=_split_copy_params(),
    )(*[_hbm(b) for b in bufs])
    return [tuple(res[3 * t:3 * t + 3]) for t in range(n)]


def gather_wait(started, after, name):
    n = len(started)

    def body(*refs):
        bufs, sems = refs[:n], refs[n:3 * n]
        me, peers = _me_and_peers()
        for t in range(n):
            for k, (dev, slot) in enumerate(peers):
                cp = pltpu.make_async_remote_copy(
                    src_ref=bufs[t].at[me], dst_ref=bufs[t].at[slot], send_sem=sems[2 * t].at[k],
                    recv_sem=sems[2 * t + 1].at[k], device_id=dev, device_id_type=pl.DeviceIdType.MESH)
                cp.wait_recv()
                cp.wait_send()

    args = [s[2] for s in started] + [sem for s in started for sem in s[:2]] + [after]
    res = pl.pallas_call(
        body, name=name, out_shape=tuple(pltpu.HBM(s[2].shape, s[2].dtype) for s in started),
        in_specs=[_HBM_SPEC] * n + [_SEM_SPEC] * (2 * n) + [pl.BlockSpec(memory_space=pl.ANY)],
        out_specs=tuple([_HBM_SPEC] * n), input_output_aliases={t: t for t in range(n)},
        compiler_params=_split_copy_params(),
    )(*args)
    return list(res)


def scatter_start(srcs, name):
    n = len(srcs)
    lands = [lax.empty((_NPEER,) + s.shape[1:], s.dtype) for s in srcs]

    def body(*refs):
        ins, land_refs, outs = refs[:n], refs[n:2 * n], refs[2 * n:]
        _, peers = _me_and_peers()
        for t in range(n):
            for k, (dev, slot) in enumerate(peers):
                pltpu.make_async_remote_copy(
                    src_ref=ins[t].at[slot], dst_ref=land_refs[t].at[k], send_sem=outs[4 * t].at[k],
                    recv_sem=outs[4 * t + 1].at[k], device_id=dev, device_id_type=pl.DeviceIdType.MESH).start()
        outs[4 * n][...] = jnp.zeros_like(outs[4 * n])

    out_shape, out_specs = [], []
    for s, land in zip(srcs, lands):
        out_shape += [pltpu.SemaphoreType.DMA((_NPEER,)), pltpu.SemaphoreType.DMA((_NPEER,)),
                      pltpu.HBM(s.shape, s.dtype), pltpu.HBM(land.shape, land.dtype)]
        out_specs += [_SEM_SPEC, _SEM_SPEC, _HBM_SPEC, _HBM_SPEC]
    out_shape.append(jax.ShapeDtypeStruct((8, LANES), F32))
    out_specs.append(pl.BlockSpec(memory_space=pltpu.VMEM))
    aliases = {t: 4 * t + 2 for t in range(n)}
    aliases.update({n + t: 4 * t + 3 for t in range(n)})
    res = pl.pallas_call(
        body, name=name, out_shape=tuple(out_shape), in_specs=[_HBM_SPEC] * (2 * n), out_specs=tuple(out_specs),
        input_output_aliases=aliases, compiler_params=_split_copy_params(),
    )(*[_hbm(s) for s in srcs], *[_hbm(land) for land in lands])
    return [tuple(res[4 * t:4 * t + 4]) for t in range(n)], res[4 * n]


def scatter_wait(started, after, name):
    n = len(started)

    def body(*refs):
        srcs, land_refs, sems = refs[:n], refs[n:2 * n], refs[2 * n:4 * n]
        _, peers = _me_and_peers()
        for t in range(n):
            for k, (dev, slot) in enumerate(peers):
                cp = pltpu.make_async_remote_copy(
                    src_ref=srcs[t].at[slot], dst_ref=land_refs[t].at[k], send_sem=sems[2 * t].at[k],
                    recv_sem=sems[2 * t + 1].at[k], device_id=dev, device_id_type=pl.DeviceIdType.MESH)
                cp.wait_recv()
                cp.wait_send()

    args = [s[2] for s in started] + [s[3] for s in started] + [sem for s in started for sem in s[:2]] + [after]
    res = pl.pallas_call(
        body, name=name,
        out_shape=tuple([pltpu.HBM(s[2].shape, s[2].dtype) for s in started]
                        + [pltpu.HBM(s[3].shape, s[3].dtype) for s in started]),
        in_specs=[_HBM_SPEC] * (2 * n) + [_SEM_SPEC] * (2 * n) + [pl.BlockSpec(memory_space=pl.ANY)],
        out_specs=tuple([_HBM_SPEC] * (2 * n)), input_output_aliases={t: t for t in range(2 * n)},
        compiler_params=_split_copy_params(),
    )(*args)
    return [(res[t], res[n + t]) for t in range(n)]


def all_to_all(groups, name):
    flat = [a for grp in groups for a in grp]
    n = len(flat)
    where = [(gi, li) for gi, grp in enumerate(groups) for li in range(len(grp))]
    any_spec = pl.BlockSpec(memory_space=pl.ANY)

    def body(*refs):
        ins, outs = refs[:n], refs[n:n + len(groups)]
        send, recv, local = refs[n + len(groups):]
        x, y, c = _coords()
        me = 4 * x + 2 * y + c
        waits = []
        for e in range(n):
            gi, li = where[e]
            mine = pltpu.make_async_copy(ins[e].at[me], outs[gi].at[me, li], local.at[e])
            mine.start()
            waits.append(mine)
        sent, landing = [], []
        for rel in range(1, NDEV):
            px = 1 - x if rel & 4 else x
            py = 1 - y if rel & 2 else y
            pc = 1 - c if rel & 1 else c
            peer = 4 * px + 2 * py + pc
            for e in range(n):
                gi, li = where[e]

                def copy(dst_slot, e=e, gi=gi, li=li, rel=rel, peer=peer, to=(px, py, pc)):
                    return pltpu.make_async_remote_copy(
                        src_ref=ins[e].at[peer], dst_ref=outs[gi].at[dst_slot, li], send_sem=send.at[7 * e + rel - 1],
                        recv_sem=recv.at[7 * e + rel - 1], device_id=to, device_id_type=pl.DeviceIdType.MESH)

                cp = copy(me)
                cp.start()
                sent.append(cp)
                landing.append(copy(peer))
        for cp in landing:
            cp.wait_recv()
        for cp in sent:
            cp.wait_send()
        for cp in waits:
            cp.wait()

    return pl.pallas_call(
        body, in_specs=[any_spec] * n, out_specs=[any_spec] * len(groups),
        out_shape=[jax.ShapeDtypeStruct((NDEV, len(grp)) + grp[0].shape[1:], grp[0].dtype) for grp in groups],
        scratch_shapes=[pltpu.SemaphoreType.DMA((7 * n,)), pltpu.SemaphoreType.DMA((7 * n,)),
                        pltpu.SemaphoreType.DMA((n,))],
        name=name,
    )(*flat)


_PACK_QUANTUM = 8 * LANES


def _pack(parts, lead=0):
    out = []
    for p in parts:
        head = p.shape[:lead]
        f = p.astype(F32).reshape(head + (-1,))
        pad = (-f.shape[-1]) % _PACK_QUANTUM
        if pad:
            f = jnp.concatenate([f, jnp.zeros(head + (pad,), F32)], axis=-1)
        out.append(f.reshape(head + (-1, LANES)))
    return jnp.concatenate(out, axis=lead)


def _unpack(buf, shapes):
    head = buf.shape[:-2]
    out, r = [], 0
    for s in shapes:
        n = 1
        for v in s:
            n *= v
        nr = -(-n // _PACK_QUANTUM) * 8
        flat = buf[..., r:r + nr, :].reshape(head + (nr * LANES,))[..., :n]
        out.append(flat.reshape(head + tuple(s)))
        r += nr
    return out


BIG = ("sb_w_qkv", "sb_w_o", "sg_w_in", "sg_w_o", "ssm_w_in", "ssm_w_glu", "ffn_w_up", "ffn_w_down")
SMALL_SHARDED = ("norm_g", "ssm_d", "ffn_conv_w")
REPLICATED = ("final_norm_g", "sg_norm_g", "sg_w_s", "sg_b", "ssm_lam_re", "ssm_lam_im", "ssm_log_dt",
              "ssm_b_re", "ssm_b_im", "ssm_c_re", "ssm_c_im", "ffn_conv_b")
WEIGHTS = ("norm_g", "final_norm_g", "sb_w_qkv", "sb_w_o", "sg_w_in", "sg_norm_g", "sg_w_s", "sg_b", "sg_w_o",
           "ssm_w_in", "ssm_lam_re", "ssm_lam_im", "ssm_log_dt", "ssm_b_re", "ssm_b_im", "ssm_c_re", "ssm_c_im",
           "ssm_d", "ssm_w_glu", "ffn_w_up", "ffn_conv_w", "ffn_conv_b", "ffn_w_down")


def _step(x, loss_target, w, m, v):
    t, d = x.shape[1], x.shape[2]
    depth = w["norm_g"].shape[0]
    x0 = x.reshape(t, d)
    tgt = loss_target.reshape(t, d)

    mx, my, mc = _coords()
    me = (4 * mx + 2 * my + mc).astype(jnp.int32).reshape(1)
    shard_pack = _pack([w[k] for k in SMALL_SHARDED])
    gathered_small, = all_gather([shard_pack], name="gather_small_weights")
    mixer_weights = (("sb_w_qkv", "sb_w_o"), ("sg_w_in", "sg_w_o"), ("ssm_w_in", "ssm_w_glu"))
    order = []
    for i in range(depth):
        order += [(k, i // 3) for k in mixer_weights[i % 3]] + [("ffn_w_up", i), ("ffn_w_down", i)]
    as_kept = lambda tree, k: jnp.swapaxes(tree[k], 1, 2) if k == "ffn_w_up" else tree[k]
    pending = dict(zip(order, gather_start([cast_into_slot(as_kept(w, k), l, me) for k, l in order],
                                           "gather_weights_start")))
    wg = {}

    def weights(keys, after):
        missing = [key for key in keys if key not in wg]
        if missing:
            for key, buf in zip(missing, gather_wait([pending[key] for key in missing], after, "gather_weights_wait")):
                wg[key] = buf[:, None]
        return [wg[key] for key in keys]

    ng, sd, cw = _unpack(gathered_small, [w[k].shape for k in SMALL_SHARDED])
    norm_full = jnp.transpose(ng, (1, 2, 0, 3)).reshape(depth, 2, d)
    ssm_d_full = jnp.transpose(sd, (1, 0, 2)).reshape(1, d)
    nc = cw.shape[-1]
    conv_b3 = w["ffn_conv_b"].reshape(depth, NDEV, nc)
    p3 = [jnp.concatenate([cw[:, l], conv_b3[l][:, None, :], jnp.zeros((NDEV, 8 - CONV_K - 1, nc), F32)], axis=1)
          for l in range(depth)]

    g_, p_, h_ = SSM_GROUPS, SSM_STATE, SSM_GROUP
    lam_re, lam_im = w["ssm_lam_re"][0], w["ssm_lam_im"][0]
    log_dt = w["ssm_log_dt"][0].reshape(g_, 1)
    b_re, b_im = w["ssm_b_re"][0].reshape(g_ * p_, h_), w["ssm_b_im"][0].reshape(g_ * p_, h_)
    ar, ai, cre, cim = _single(_disc1, [lam_re, lam_im, log_dt], [(g_, p_)] * 4, "s5_disc1")
    cre_c, cim_c = cre.reshape(g_ * p_, 1), cim.reshape(g_ * p_, 1)
    bbr, bbi = _single(_disc2, [cre_c, cim_c, b_re, b_im], [(g_ * p_, h_)] * 2, "s5_disc2")
    per_group_t = lambda a, r, c: jnp.swapaxes(a.reshape(g_, r, c), 1, 2)
    bd = jnp.stack([_block_diag(per_group_t(bbr, p_, h_)), _block_diag(per_group_t(bbi, p_, h_))])
    cd = jnp.stack([_block_diag(per_group_t(w["ssm_c_re"][0], h_, p_)),
                    -_block_diag(per_group_t(w["ssm_c_im"][0], h_, p_))])
    a2 = jnp.stack([ar.reshape(g_ * p_), ai.reshape(g_ * p_)])

    sg_gain = w["sg_norm_g"]
    sg_ws = w["sg_w_s"][0]
    sg_bfull = jnp.broadcast_to(w["sg_b"][0][:, :, None], sg_ws.shape)

    acts = []
    xc = x0
    xn = rms_fwd(xc, norm_full[0, 0][None], "rms_fwd")
    for i in range(depth):
        mixer, j = i % 3, i // 3
        st = {"x": xc, "xn": xn}
        g0, g1 = norm_full[i, 0][None], norm_full[i, 1][None]
        g_next = norm_full[i + 1, 0][None] if i + 1 < depth else None
        k_in, k_out = [(k, j) for k in mixer_weights[mixer]]
        w_in, = weights([k_in], xn)
        if mixer == 0:
            qkv = mm_cs_fwd(xn, w_in, 0, BF16, "qkv_fwd")
            o, ltot = sb_attn_fwd(qkv)
            w_out, = weights([k_out], o)
            x1, xn2 = mm_rs_fwd(o, w_out, 0, xc, F32, "attn_out_fwd", norm=g1)
            st.update(qkv=qkv, o=o, ltot=ltot)
        elif mixer == 1:
            hin = mm_cs_fwd(xn, w_in, 0, BF16, "sg_in_fwd")
            p = sgu_fwd(hin, sg_gain, sg_ws, sg_bfull)
            w_out, = weights([k_out], p)
            x1, xn2 = mm_rs_fwd(p, w_out, 0, xc, F32, "sg_out_fwd", norm=g1)
            st.update(hin=hin, p=p)
        else:
            u = mm_rs_fwd(xn, w_in, 0, None, F32, "ssm_in_fwd")
            u_s = s5_reorder(u, True)
            x2 = s5_scan_fwd(mm_s5("bu", u_s, bd, t, "s5_bu"), a2)
            yc_s = mm_s5("yc", x2, cd, t, "s5_yc")
            yg = s5_post_fwd(s5_reorder(yc_s, False), u, ssm_d_full)
            w_out, = weights([k_out], yg)
            hg = mm_cs_fwd(yg, w_out, 0, BF16, "ssm_glu_fwd")
            x1, xn2 = glu_fwd(hg, xc, g1)
            st.update(u_s=u_s, x2=x2, yc_s=yc_s, yg=yg, hg=hg)
        w_up, w_down = weights([("ffn_w_up", i), ("ffn_w_down", i)], xn2)
        h3 = mm_up_fwd(xn2, w_up, 0, BF16, "ffn_up_fwd")
        gated = ffn_gate_fwd(h3, p3[i])
        if g_next is None:
            xc = mm_down_fwd(gated, w_down, 0, x1, "ffn_down_fwd")
        else:
            xc, xn = mm_down_fwd(gated, w_down, 0, x1, "ffn_down_fwd", norm=g_next)
        st.update(x1=x1, xn2=xn2, h3=h3, gated=gated, g0=g0, g1=g1)
        acts.append(st)

    dx, loss_lanes, d_final_g = loss_head(xc, w["final_norm_g"][None], tgt)
    loss = lax.psum(loss_lanes[0, 0], MESH_AXES)

    scattering = {}
    d_norm = [[None, None] for _ in range(depth)]
    d_p3 = [None] * depth
    rep = {}
    d_ssm_d = None
    token = None

    def scatter(grads_by_key):
        keys = list(grads_by_key)
        started, tok = scatter_start([grads_by_key[key] for key in keys], "scatter_grads_start")
        scattering[tuple(keys)] = started
        return tok

    for i in reversed(range(depth)):
        mixer, j = i % 3, i // 3
        st = acts[i]
        k_in, k_out = [(k, j) for k in mixer_weights[mixer]]
        w_in, w_out, w_up, w_down = weights([k_in, k_out, ("ffn_w_up", i), ("ffn_w_down", i)], None)
        dgated = mm_down_da(dx, w_down, 0, "ffn_down_da", dep=token)
        g_down = mm_down_dw(st["gated"], dx, "ffn_down_dw")
        dy_a, dy_g, dp_a, dp_g = ffn_gate_bwd(st["h3"], dgated, p3[i])
        d_p3[i] = jnp.concatenate([dp_a, dp_g], axis=0)
        dh3 = ffn_conv_t(dy_a, dy_g, p3[i])
        dxn2 = mm_up_da(dh3, w_up, 0, "ffn_up_da")
        g_up = mm_up_dw(st["xn2"], dh3, "ffn_up_dw")
        dx1, d_norm[i][1] = rms_bwd(st["x1"], st["g1"], dxn2, dx, "rms_bwd")
        token = scatter({("ffn_w_down", i): g_down, ("ffn_w_up", i): g_up})
        if mixer == 0:
            do = mm_rs_da(dx1, w_out, 0, BF16, "attn_out_da", dep=token)
            g_out = mm_rs_dw(st["o"], dx1, "attn_out_dw")
            d3 = sb_attn_bwd(st["qkv"], st["ltot"], do)
            g_in = mm_qkv_dw(st["xn"], d3, w_in.shape[3], "qkv_dw")
            token = scatter({k_in: g_in, k_out: g_out})
            dxn = mm_qkv_da(d3, w_in, 0, "qkv_da", dep=token)
        elif mixer == 1:
            dp = mm_rs_da(dx1, w_out, 0, BF16, "sg_out_da", dep=token)
            g_out = mm_rs_dw(st["p"], dx1, "sg_out_dw")
            dhin, d_ws, d_bfull, d_gain = sgu_bwd(st["hin"], dp, sg_gain, sg_ws, sg_bfull)
            rep.update(sg_w_s=d_ws[None], sg_b=d_bfull[None, :, :, 0], sg_norm_g=d_gain)
            dxn = mm_cs_da(dhin, w_in, 0, t, "sg_in_da")
            g_in = mm_cs_dw(st["xn"], dhin, w_in.shape[3], "sg_in_dw")
        else:
            dhg = glu_bwd(st["hg"], dx1, token)
            dyg = mm_cs_da(dhg, w_out, 0, t, "ssm_glu_da")
            g_out = mm_cs_dw(st["yg"], dhg, w_out.shape[3], "ssm_glu_dw")
            dyc_s, du_skip_s, d_ssm_d = s5_post_bwd(st["yc_s"], st["u_s"], ssm_d_full, s5_reorder(dyg, True))
            dx2 = mm_s5("dx", dyc_s, cd, t, "s5_dx")
            dcd = mm_s5("dcd", st["x2"], dyc_s, t, "s5_dcd")
            g2, da2 = s5_scan_bwd(dx2, st["x2"], a2)
            du = s5_reorder(mm_s5("du", g2, bd, t, "s5_du", res=du_skip_s), False)
            dbd = mm_s5("dbd", st["u_s"], g2, t, "s5_dbd")
            from_bd = lambda blk: jnp.swapaxes(_block_diag_t(blk, h_, p_), 1, 2).reshape(g_ * p_, h_)

            def disc2_bwd(c1, c2, b1, b2, t1, t2):
                return jax.vjp(_disc2, c1, c2, b1, b2)[1]((t1, t2))

            d_cre, d_cim, d_b_re, d_b_im = _single(
                disc2_bwd, [cre_c, cim_c, b_re, b_im, from_bd(dbd[0]), from_bd(dbd[1])],
                [(g_ * p_, 1)] * 2 + [(g_ * p_, h_)] * 2, "s5_disc2_bwd")

            def disc1_bwd(l1, l2, ld, t1, t2, t3, t4):
                return jax.vjp(_disc1, l1, l2, ld)[1]((t1, t2, t3, t4))

            d_lam_re, d_lam_im, d_log_dt = _single(
                disc1_bwd, [lam_re, lam_im, log_dt, da2[0].reshape(g_, p_), da2[1].reshape(g_, p_),
                            d_cre.reshape(g_, p_), d_cim.reshape(g_, p_)],
                [(g_, p_), (g_, p_), (g_, 1)], "s5_disc1_bwd")
            from_cd = lambda blk: jnp.swapaxes(_block_diag_t(blk, p_, h_), 1, 2)
            rep.update(ssm_lam_re=d_lam_re[None], ssm_lam_im=d_lam_im[None], ssm_log_dt=d_log_dt.reshape(1, g_),
                       ssm_b_re=d_b_re.reshape(1, g_, p_, h_), ssm_b_im=d_b_im.reshape(1, g_, p_, h_),
                       ssm_c_re=from_cd(dcd[0])[None], ssm_c_im=-from_cd(dcd[1])[None])
            dxn = mm_rs_da(du, w_in, 0, F32, "ssm_in_da")
            g_in = mm_rs_dw(st["xn"], du, "ssm_in_dw")
        dx, d_norm[i][0] = rms_bwd(st["x"], st["g0"], dxn, dx1, "rms_bwd")
        if mixer != 0:
            token = scatter({k_in: g_in, k_out: g_out})

    rep["final_norm_g"] = d_final_g.reshape(d)
    rep["ffn_conv_b"] = jnp.stack([d_p3[l][:, CONV_K, :].reshape(NDEV * nc) for l in range(depth)])

    d_norm_full = jnp.stack([jnp.concatenate(pair, axis=0) for pair in d_norm])
    d_norm_pieces = jnp.transpose(d_norm_full.reshape(depth, 2, NDEV, d // NDEV), (2, 0, 1, 3))
    d_ssm_d_pieces = jnp.transpose(d_ssm_d.reshape(1, NDEV, d // NDEV), (1, 0, 2))
    d_conv_w_pieces = jnp.stack([d_p3[l][:, :CONV_K, :] for l in range(depth)], axis=1)
    small_pieces = _pack([d_norm_pieces, d_ssm_d_pieces, d_conv_w_pieces], lead=1)
    small_received, = all_to_all([[small_pieces]], name="scatter_small_grads")
    rep_parts, = all_gather([_pack([rep[k] for k in REPLICATED]).astype(BF16)], name="gather_small_grads")
    own, landed = {}, {}
    for keys, started in scattering.items():
        for key, (src, land) in zip(keys, scatter_wait(started, dx, "scatter_grads_wait")):
            own[key], landed[key] = src, land

    grads, deltas, new_m, new_v = {}, {}, {}, {}
    for k in BIG:
        layers = range(w[k].shape[0])
        res = adamw_layers(as_kept(w, k), as_kept(m, k), as_kept(v, k), [landed[(k, l)] for l in layers],
                           [own[(k, l)] for l in layers], me, "adamw")
        grads[k], deltas[k], new_m[k], new_v[k] = [jnp.swapaxes(r, 1, 2) if k == "ffn_w_up" else r for r in res]
    for names, parts in ((SMALL_SHARDED, small_received[:, 0]), (REPLICATED, rep_parts)):
        res = adamw(_pack([w[k] for k in names]), _pack([m[k] for k in names]), _pack([v[k] for k in names]), parts,
                    "adamw_small")
        for tree, buf in zip((grads, deltas, new_m, new_v), res):
            for k, val in zip(names, _unpack(buf, [w[k].shape for k in names])):
                tree[k] = val
    grad_x = dx.reshape(x.shape)
    return (loss, grad_x, *[grads[k] for k in WEIGHTS], *[deltas[k] for k in WEIGHTS],
            *[new_m[k] for k in WEIGHTS], *[new_v[k] for k in WEIGHTS])


def kernel(x, norm_g, final_norm_g, sb_w_qkv, sb_w_o, sg_w_in, sg_norm_g, sg_w_s, sg_b, sg_w_o, ssm_w_in, ssm_lam_re, ssm_lam_im, ssm_log_dt, ssm_b_re, ssm_b_im, ssm_c_re, ssm_c_im, ssm_d, ssm_w_glu, ffn_w_up, ffn_conv_w, ffn_conv_b, ffn_w_down, loss_target, m_norm_g, m_final_norm_g, m_sb_w_qkv, m_sb_w_o, m_sg_w_in, m_sg_norm_g, m_sg_w_s, m_sg_b, m_sg_w_o, m_ssm_w_in, m_ssm_lam_re, m_ssm_lam_im, m_ssm_log_dt, m_ssm_b_re, m_ssm_b_im, m_ssm_c_re, m_ssm_c_im, m_ssm_d, m_ssm_w_glu, m_ffn_w_up, m_ffn_conv_w, m_ffn_conv_b, m_ffn_w_down, v_norm_g, v_final_norm_g, v_sb_w_qkv, v_sb_w_o, v_sg_w_in, v_sg_norm_g, v_sg_w_s, v_sg_b, v_sg_w_o, v_ssm_w_in, v_ssm_lam_re, v_ssm_lam_im, v_ssm_log_dt, v_ssm_b_re, v_ssm_b_im, v_ssm_c_re, v_ssm_c_im, v_ssm_d, v_ssm_w_glu, v_ffn_w_up, v_ffn_conv_w, v_ffn_conv_b, v_ffn_w_down):
    w = dict(zip(WEIGHTS, (norm_g, final_norm_g, sb_w_qkv, sb_w_o, sg_w_in, sg_norm_g, sg_w_s, sg_b, sg_w_o, ssm_w_in,
                           ssm_lam_re, ssm_lam_im, ssm_log_dt, ssm_b_re, ssm_b_im, ssm_c_re, ssm_c_im, ssm_d, ssm_w_glu,
                           ffn_w_up, ffn_conv_w, ffn_conv_b, ffn_w_down)))
    m = dict(zip(WEIGHTS, (m_norm_g, m_final_norm_g, m_sb_w_qkv, m_sb_w_o, m_sg_w_in, m_sg_norm_g, m_sg_w_s, m_sg_b,
                           m_sg_w_o, m_ssm_w_in, m_ssm_lam_re, m_ssm_lam_im, m_ssm_log_dt, m_ssm_b_re, m_ssm_b_im,
                           m_ssm_c_re, m_ssm_c_im, m_ssm_d, m_ssm_w_glu, m_ffn_w_up, m_ffn_conv_w, m_ffn_conv_b,
                           m_ffn_w_down)))
    v = dict(zip(WEIGHTS, (v_norm_g, v_final_norm_g, v_sb_w_qkv, v_sb_w_o, v_sg_w_in, v_sg_norm_g, v_sg_w_s, v_sg_b,
                           v_sg_w_o, v_ssm_w_in, v_ssm_lam_re, v_ssm_lam_im, v_ssm_log_dt, v_ssm_b_re, v_ssm_b_im,
                           v_ssm_c_re, v_ssm_c_im, v_ssm_d, v_ssm_w_glu, v_ffn_w_up, v_ffn_conv_w, v_ffn_conv_b,
                           v_ffn_w_down)))
    return _step(x, loss_target, w, m, v)
```

```python
import functools

import jax
import jax.numpy as jnp
from jax import lax
from jax.experimental import pallas as pl
from jax.experimental.pallas import tpu as pltpu

F32, BF16 = jnp.float32, jnp.bfloat16
MESH_AXES = ("x", "y", "c")
NDEV = 8
EPS = 1e-6
HEAD_DIM = 64
LANES = 128
ATT_BQ, ATT_BK = 2048, 256
CHUNK = 128
SG_GROUPS = 8
SSM_GROUPS, SSM_STATE, SSM_GROUP = 64, 64, 16
SSM_PACK = 8
S5_PASSES = 1
CONV_K = 3
HALO = 16
ROW_BLOCK = 512
SCAN_COLS = 256
ADAM_LR, ADAM_B1, ADAM_B2, ADAM_EPS, ADAM_WD, ADAM_STEP = 0.001, 0.9, 0.999, 1e-08, 0.01, 10
VMEM_LIMIT = 56 * 1024 * 1024

_MM = (((1,), (0,)), ((), ()))
_MM_TB = (((1,), (1,)), ((), ()))
_MM_TA = (((0,), (0,)), ((), ()))


def _params(sem):
    return pltpu.CompilerParams(dimension_semantics=sem, vmem_limit_bytes=VMEM_LIMIT)


def _rows(total, cap, mult=16):
    best = None
    for d in range(mult, min(total, cap) + 1, mult):
        if total % d == 0:
            best = d
    return best if best is not None else total


def _dot(a, b, dims, passes):
    if passes == 1:
        return lax.dot_general(a.astype(BF16), b.astype(BF16), dims, preferred_element_type=F32)
    a = a.astype(F32)
    b = b.astype(F32)
    ah = a.astype(BF16)
    bh = b.astype(BF16)
    al = (a - ah.astype(F32)).astype(BF16)
    bl = (b - bh.astype(F32)).astype(BF16)
    out = lax.dot_general(ah, bh, dims, preferred_element_type=F32)
    out = out + lax.dot_general(al, bh, dims, preferred_element_type=F32)
    return out + lax.dot_general(ah, bl, dims, preferred_element_type=F32)


def _mm(a, b, *, grid, a_blk, a_map, b_blk, b_map, o_blk, o_map, out_shape, out_dtype, name,
        dims=_MM, passes=1, res=None, res_blk=None, res_map=None, b_2d=None, acc_2d=None, dep=None, norm=None):
    nk = grid[2]
    has_res, has_norm = res is not None, norm is not None
    a_maps = list(a_map) if isinstance(a_map, (list, tuple)) else [a_map]
    b_maps = list(b_map) if isinstance(b_map, (list, tuple)) else [b_map]
    na, nb = len(a_maps), len(b_maps)
    n_in = na + nb + has_res + has_norm + (dep is not None)

    def body(*refs):
        o_ref = refs[n_in]
        r_ref = refs[na + nb] if has_res else None
        av = refs[0][...] if na == 1 else jnp.concatenate([r[...] for r in refs[:na]], axis=-1)
        bv = refs[na][...] if nb == 1 else jnp.concatenate([r[...] for r in refs[na:na + nb]], axis=-1)
        if b_2d is not None:
            bv = bv.reshape(b_2d)
        part = _dot(av, bv, dims, passes)

        def finish(total):
            if has_res:
                total = total + r_ref[...].astype(F32)
            o_ref[...] = total.reshape(o_ref.shape).astype(o_ref.dtype)
            if has_norm:
                refs[n_in + 1][...] = _rms(total, refs[na + nb + has_res][...]).astype(BF16)

        if nk == 1:
            finish(part)
        else:
            acc_ref = refs[-1]
            k = pl.program_id(2)

            @pl.when(k == 0)
            def _():
                acc_ref[...] = part

            @pl.when(k > 0)
            def _():
                acc_ref[...] += part

            @pl.when(k == nk - 1)
            def _():
                finish(acc_ref[...])

    in_specs = [pl.BlockSpec(a_blk, f) for f in a_maps] + [pl.BlockSpec(b_blk, f) for f in b_maps]
    args = [a] * na + [b] * nb
    if has_res:
        in_specs.append(pl.BlockSpec(res_blk, res_map))
        args.append(res)
    if has_norm:
        in_specs.append(pl.BlockSpec(norm.shape, lambda *_: (0, 0)))
        args.append(norm)
    if dep is not None:
        in_specs.append(pl.BlockSpec(memory_space=pl.ANY))
        args.append(dep)
    scratch = [pltpu.VMEM(acc_2d, F32)] if nk > 1 else []
    out_specs, out_shapes = pl.BlockSpec(o_blk, o_map), jax.ShapeDtypeStruct(out_shape, out_dtype)
    if has_norm:
        out_specs, out_shapes = [out_specs] * 2, [out_shapes, jax.ShapeDtypeStruct(out_shape, BF16)]
    return pl.pallas_call(
        body, grid=grid, in_specs=in_specs, out_specs=out_specs, out_shape=out_shapes, scratch_shapes=scratch,
        name=name, compiler_params=_params(("parallel", "parallel", "arbitrary")),
    )(*args)


def _cs_act_spec(ns, tm, row_of, col_of):
    if ns % LANES == 0:
        return (tm, ns), lambda *g: (row_of(*g), col_of(*g))
    return (None, tm, ns), lambda *g: (col_of(*g), row_of(*g), 0)


def mm_cs_fwd(a, w4, l, out_dtype, name, tm=2048):
    m, k = a.shape
    tm = min(tm, m)
    ns = w4.shape[3]
    o_blk, o_map = _cs_act_spec(ns, tm, lambda j, i, kk: i, lambda j, i, kk: j)
    out_shape = (m, NDEV * ns) if ns % LANES == 0 else (NDEV, m, ns)
    return _mm(a, w4, grid=(NDEV, m // tm, 1), a_blk=(tm, k), a_map=lambda j, i, kk: (i, 0),
               b_blk=(None, None, k, ns), b_map=lambda j, i, kk: (j, l, 0, 0),
               o_blk=o_blk, o_map=o_map, out_shape=out_shape, out_dtype=out_dtype, name=name)


def mm_cs_da(dc, w4, l, m, name, tm=1024):
    k, ns = w4.shape[2], w4.shape[3]
    tm = min(tm, m)
    a_blk, a_map = _cs_act_spec(ns, tm, lambda i, _, j: i, lambda i, _, j: j)
    return _mm(dc, w4, grid=(m // tm, 1, NDEV), a_blk=a_blk, a_map=a_map,
               b_blk=(None, None, k, ns), b_map=lambda i, _, j: (j, l, 0, 0),
               o_blk=(tm, k), o_map=lambda i, _, j: (i, 0), out_shape=(m, k), out_dtype=F32,
               dims=_MM_TB, acc_2d=(tm, k), name=name)


def mm_cs_dw(a, dc, ns, name, tk=2048):
    m, k = a.shape
    tk = min(tk, m)
    b_blk, b_map = _cs_act_spec(ns, tk, lambda j, _, kk: kk, lambda j, _, kk: j)
    return _mm(a, dc, grid=(NDEV, 1, m // tk), a_blk=(tk, k), a_map=lambda j, _, kk: (kk, 0),
               b_blk=b_blk, b_map=b_map, o_blk=(None, k, ns), o_map=lambda j, _, kk: (j, 0, 0),
               out_shape=(NDEV, k, ns), out_dtype=BF16, dims=_MM_TA, acc_2d=(k, ns), name=name)


def mm_up_fwd(a, wt4, l, out_dtype, name, tm=2048):
    m, k = a.shape
    tm = min(tm, m)
    ns = wt4.shape[2]
    return _mm(a, wt4, grid=(NDEV, m // tm, 1), a_blk=(tm, k), a_map=lambda j, i, kk: (i, 0),
               b_blk=(None, None, ns, k), b_map=lambda j, i, kk: (j, l, 0, 0), dims=_MM_TB,
               o_blk=(None, tm, ns), o_map=lambda j, i, kk: (j, i, 0), out_shape=(NDEV, m, ns), out_dtype=out_dtype,
               name=name)


def mm_up_da(dc3, wt4, l, name, tm=1024):
    _, m, ns = dc3.shape
    tm = min(tm, m)
    k = wt4.shape[3]
    return _mm(dc3, wt4, grid=(m // tm, 1, NDEV), a_blk=(None, tm, ns), a_map=lambda i, _, j: (j, i, 0),
               b_blk=(None, None, ns, k), b_map=lambda i, _, j: (j, l, 0, 0),
               o_blk=(tm, k), o_map=lambda i, _, j: (i, 0), out_shape=(m, k), out_dtype=F32, acc_2d=(tm, k), name=name)


def mm_up_dw(a, dc3, name, tk=2048):
    m, k = a.shape
    tk = min(tk, m)
    ns = dc3.shape[2]
    return _mm(dc3, a, grid=(NDEV, 1, m // tk), a_blk=(None, tk, ns), a_map=lambda j, _, kk: (j, kk, 0),
               b_blk=(tk, k), b_map=lambda j, _, kk: (kk, 0), dims=_MM_TA,
               o_blk=(None, ns, k), o_map=lambda j, _, kk: (j, 0, 0), out_shape=(NDEV, ns, k), out_dtype=BF16,
               acc_2d=(ns, k), name=name)


def mm_rs_fwd(a, w4, l, res, out_dtype, name, tm=1024, norm=None):
    m, k = a.shape
    tm = min(tm, m)
    ks, n = w4.shape[2], w4.shape[3]
    return _mm(a, w4, grid=(m // tm, 1, 1), a_blk=(tm, k), a_map=lambda i, _, kk: (i, 0),
               b_blk=(NDEV, None, ks, n), b_map=lambda i, _, kk: (0, l, 0, 0), b_2d=(k, n),
               o_blk=(tm, n), o_map=lambda i, _, kk: (i, 0), out_shape=(m, n), out_dtype=out_dtype,
               res=res, res_blk=(tm, n), res_map=lambda i, _, kk: (i, 0), name=name, norm=norm)


def mm_rs_da(dc, w4, l, out_dtype, name, tm=1024, dep=None):
    m, n = dc.shape
    tm = min(tm, m)
    ks = w4.shape[2]
    k = NDEV * ks
    return _mm(dc, w4, grid=(m // tm, 1, 1), a_blk=(tm, n), a_map=lambda i, _, kk: (i, 0),
               b_blk=(NDEV, None, ks, n), b_map=lambda i, _, kk: (0, l, 0, 0), b_2d=(k, n),
               o_blk=(tm, k), o_map=lambda i, _, kk: (i, 0), out_shape=(m, k), out_dtype=out_dtype,
               dims=_MM_TB, name=name, dep=dep)


def mm_rs_dw(a, dc, name, tk=1024):
    m, k = a.shape
    tk = min(tk, m)
    n = dc.shape[1]
    ks = k // NDEV
    return _mm(a, dc, grid=(1, 1, m // tk), a_blk=(tk, k), a_map=lambda _, __, kk: (kk, 0),
               b_blk=(tk, n), b_map=lambda _, __, kk: (kk, 0),
               o_blk=(NDEV, ks, n), o_map=lambda _, __, kk: (0, 0, 0), out_shape=(NDEV, ks, n),
               out_dtype=BF16, dims=_MM_TA, acc_2d=(k, n), name=name)


def mm_down_fwd(a3, w4, l, res, name, tm=1024, norm=None):
    nj, m, kc = a3.shape
    tm = min(tm, m)
    ks, n = w4.shape[2], w4.shape[3]
    return _mm(a3, w4, grid=(m // tm, 1, nj), a_blk=(None, tm, kc), a_map=lambda i, _, j: (j, i, 0),
               b_blk=(2, None, ks, n), b_map=lambda i, _, j: (j, l, 0, 0), b_2d=(kc, n),
               o_blk=(tm, n), o_map=lambda i, _, j: (i, 0), out_shape=(m, n), out_dtype=F32,
               res=res, res_blk=(tm, n), res_map=lambda i, _, j: (i, 0), acc_2d=(tm, n), name=name, norm=norm)


def mm_down_da(dc, w4, l, name, tm=2048, dep=None):
    m, n = dc.shape
    tm = min(tm, m)
    ks = w4.shape[2]
    kc = 2 * ks
    nj = NDEV // 2
    return _mm(dc, w4, grid=(nj, m // tm, 1), a_blk=(tm, n), a_map=lambda j, i, _: (i, 0),
               b_blk=(2, None, ks, n), b_map=lambda j, i, _: (j, l, 0, 0), b_2d=(kc, n),
               o_blk=(None, tm, kc), o_map=lambda j, i, _: (j, i, 0), out_shape=(nj, m, kc),
               out_dtype=BF16, dims=_MM_TB, name=name, dep=dep)


def mm_down_dw(a3, dc, name, tk=2048):
    nj, m, kc = a3.shape
    tk = min(tk, m)
    n = dc.shape[1]
    return _mm(a3, dc, grid=(nj, 1, m // tk), a_blk=(None, tk, kc), a_map=lambda j, _, kk: (j, kk, 0),
               b_blk=(tk, n), b_map=lambda j, _, kk: (kk, 0),
               o_blk=(2, kc // 2, n), o_map=lambda j, _, kk: (j, 0, 0), out_shape=(NDEV, kc // 2, n),
               out_dtype=BF16, dims=_MM_TA, acc_2d=(kc, n), name=name)


def _qkv_group_maps(d, ns, row_of, piece_of):
    per_arr, per_piece = d // LANES, ns // LANES

    def group_map(q):
        def f(*g):
            grp = piece_of(*g) * per_piece + q
            return grp // per_arr, row_of(*g), grp % per_arr
        return f

    return [group_map(q) for q in range(per_piece)]


def mm_qkv_da(d3, w4, l, name, tm=1024, dep=None):
    _, m, d = d3.shape
    tm = min(tm, m)
    k, ns = w4.shape[2], w4.shape[3]
    return _mm(d3, w4, grid=(m // tm, 1, NDEV),
               a_blk=(None, tm, LANES), a_map=_qkv_group_maps(d, ns, lambda i, _, j: i, lambda i, _, j: j),
               b_blk=(None, None, k, ns), b_map=lambda i, _, j: (j, l, 0, 0),
               o_blk=(tm, k), o_map=lambda i, _, j: (i, 0), out_shape=(m, k), out_dtype=F32,
               dims=_MM_TB, acc_2d=(tm, k), name=name, dep=dep)


def mm_qkv_dw(a, d3, ns, name, tk=2048):
    m, k = a.shape
    tk = min(tk, m)
    d = d3.shape[2]
    return _mm(a, d3, grid=(NDEV, 1, m // tk), a_blk=(tk, k), a_map=lambda j, _, kk: (kk, 0),
               b_blk=(None, tk, LANES), b_map=_qkv_group_maps(d, ns, lambda j, _, kk: kk, lambda j, _, kk: j),
               o_blk=(None, k, ns), o_map=lambda j, _, kk: (j, 0, 0),
               out_shape=(NDEV, k, ns), out_dtype=BF16, dims=_MM_TA, acc_2d=(k, ns), name=name)


def _rowwise(fn, ins, outs, *, tr, name, acc_outs=()):
    rows = next(a.shape[0] if kind == "row" else a.shape[1] for a, kind in ins if kind in ("row", "row3"))
    n_in, n_out = len(ins), len(outs)
    n_read = sum(kind != "dep" for _, kind in ins)

    def body(*refs):
        vals = fn(*[r[...] for r in refs[:n_read]])
        if not isinstance(vals, (tuple, list)):
            vals = (vals,)
        for ref, val in zip(refs[n_in:n_in + n_out], vals[:n_out]):
            ref[...] = val.astype(ref.dtype)
        i = pl.program_id(0)
        for ref, val in zip(refs[n_in + n_out:], vals[n_out:]):
            val = val.astype(ref.dtype)

            @pl.when(i == 0)
            def _(ref=ref, val=val):
                ref[...] = val

            @pl.when(i > 0)
            def _(ref=ref, val=val):
                ref[...] += val

    in_specs = []
    for a, kind in ins:
        if kind == "row":
            in_specs.append(pl.BlockSpec((tr, a.shape[1]), lambda i: (i, 0)))
        elif kind == "row3":
            in_specs.append(pl.BlockSpec((a.shape[0], tr, a.shape[2]), lambda i: (0, i, 0)))
        elif kind == "dep":
            in_specs.append(pl.BlockSpec(memory_space=pl.ANY))
        else:
            in_specs.append(pl.BlockSpec(a.shape, lambda i, nd=a.ndim: (0,) * nd))
    out_specs = [pl.BlockSpec((tr, c), lambda i: (i, 0)) for c, _ in outs]
    out_specs += [pl.BlockSpec(s, lambda i, nd=len(s): (0,) * nd) for s, _ in acc_outs]
    out_shape = [jax.ShapeDtypeStruct((rows, c), dt) for c, dt in outs]
    out_shape += [jax.ShapeDtypeStruct(s, dt) for s, dt in acc_outs]
    res = pl.pallas_call(
        body, grid=(rows // tr,), in_specs=in_specs, out_specs=out_specs, out_shape=out_shape, name=name,
        compiler_params=_params(("arbitrary",) if acc_outs else ("parallel",)),
    )(*[a for a, _ in ins])
    return res


def _rms(x, g):
    return x * lax.rsqrt(jnp.mean(x * x, axis=-1, keepdims=True) + EPS) * g


def cast_into_slot(w, l, me):
    _, r, c = w.shape
    tr = _rows(r, 512)

    def body(me_ref, w_ref, o_ref):
        o_ref[...] = w_ref[...].astype(o_ref.dtype)

    return pl.pallas_call(
        body,
        grid_spec=pltpu.PrefetchScalarGridSpec(
            num_scalar_prefetch=1, grid=(r // tr,),
            in_specs=[pl.BlockSpec((None, tr, c), lambda i, me_ref: (l, i, 0))],
            out_specs=pl.BlockSpec((None, tr, c), lambda i, me_ref: (me_ref[0], i, 0))),
        out_shape=jax.ShapeDtypeStruct((NDEV, r, c), BF16), name="cast_into_slot",
        compiler_params=_params(("parallel",)),
    )(me, w)


def rms_fwd(x, g, name):
    out, = _rowwise(_rms, [(x, "row"), (g, "full")], [(x.shape[1], BF16)], tr=ROW_BLOCK, name=name)
    return out


def rms_bwd(x, g, dy, dres, name):
    def fn(xv, gv, dyv, drv):
        _, vjp = jax.vjp(_rms, xv, gv)
        dx, dg = vjp(dyv.astype(F32))
        return drv + dx, dg

    d = x.shape[1]
    return _rowwise(fn, [(x, "row"), (g, "full"), (dy, "row"), (dres, "row")], [(d, F32)], tr=ROW_BLOCK, name=name,
                    acc_outs=[((1, d), F32)])


def loss_head(x, g, tgt):
    def f(xv, gv, tv):
        err = jnp.square(_rms(xv, gv) - tv)
        return 0.5 * jnp.sum(jnp.mean(err, axis=-1))

    def fn(xv, gv, tv):
        val, (dx, dg) = jax.value_and_grad(f, argnums=(0, 1))(xv, gv, tv)
        return dx, jnp.full((1, LANES), val, F32), dg

    d = x.shape[1]
    return _rowwise(fn, [(x, "row"), (g, "full"), (tgt, "row")], [(d, F32)], tr=ROW_BLOCK, name="loss_head",
                    acc_outs=[((1, LANES), F32), ((1, d), F32)])


def _glu(hg, x):
    half = hg.shape[1] // 2
    return x + hg[:, :half] * jax.nn.sigmoid(hg[:, half:])


def glu_fwd(hg, x, norm):
    def fn(h, xv, g):
        x1 = _glu(h.astype(F32), xv)
        return x1, _rms(x1, g)

    d = x.shape[1]
    return _rowwise(fn, [(hg, "row"), (x, "row"), (norm, "full")], [(d, F32), (d, BF16)], tr=ROW_BLOCK, name="glu_fwd")


def glu_bwd(hg, dx1, dep):
    def fn(h, d):
        _, vjp = jax.vjp(lambda hv: _glu(hv, jnp.zeros_like(d)), h.astype(F32))
        return vjp(d)[0]

    out, = _rowwise(fn, [(hg, "row"), (dx1, "row"), (dep, "dep")], [(hg.shape[1], BF16)], tr=ROW_BLOCK, name="glu_bwd")
    return out


def _s5_post(yc, u, d):
    return jax.nn.gelu(yc + d * u)


def s5_post_fwd(yc, u, d):
    out, = _rowwise(_s5_post, [(yc, "row"), (u, "row"), (d, "full")], [(yc.shape[1], BF16)], tr=ROW_BLOCK,
                    name="s5_post_fwd")
    return out


def s5_post_bwd(yc, u, d, dyg):
    def fn(ycv, uv, dv, g):
        _, vjp = jax.vjp(_s5_post, ycv, uv, dv)
        return vjp(g.astype(F32))

    dm = yc.shape[1]
    return _rowwise(fn, [(yc, "row"), (u, "row"), (d, "full"), (dyg, "row")], [(dm, F32), (dm, F32)], tr=ROW_BLOCK,
                    name="s5_post_bwd", acc_outs=[((1, dm), F32)])


def _adam_update(wv, mv, vv, g):
    m2 = ADAM_B1 * mv + (1.0 - ADAM_B1) * g
    v2 = ADAM_B2 * vv + (1.0 - ADAM_B2) * jnp.square(g)
    m_hat = m2 / (1.0 - ADAM_B1 ** ADAM_STEP)
    v_hat = v2 / (1.0 - ADAM_B2 ** ADAM_STEP)
    delta = -ADAM_LR * (m_hat / (jnp.sqrt(v_hat) + ADAM_EPS) + ADAM_WD * wv)
    return g, delta, m2, v2


def adamw(w, m, v, g_parts, name):
    def fn(wv, mv, vv, gp):
        g = gp[0].astype(F32)
        for p in range(1, gp.shape[0]):
            g = g + gp[p].astype(F32)
        return _adam_update(wv, mv, vv, g)

    c = w.shape[1]
    return _rowwise(fn, [(w, "row"), (m, "row"), (v, "row"), (g_parts, "row3")], [(c, F32)] * 4,
                    tr=_rows(w.shape[0], 256), name=name)


def adamw_layers(w, m, v, lands, owns, me, name):
    nl, r, c = w.shape
    tr = _rows(r, 256)

    def body(me_ref, w_ref, m_ref, v_ref, *rest):
        land_refs, own_refs, out_refs = rest[:nl], rest[nl:2 * nl], rest[2 * nl:]
        for l in range(nl):
            @pl.when(pl.program_id(0) == l)
            def _(l=l):
                g = own_refs[l][...].astype(F32)
                for p in range(NDEV - 1):
                    g = g + land_refs[l][p].astype(F32)
                for ref, val in zip(out_refs, _adam_update(w_ref[...], m_ref[...], v_ref[...], g)):
                    ref[...] = val

    def rows_of(l):
        return lambda li, i, me_ref: jnp.where(li == l, i, 0)

    wspec = pl.BlockSpec((None, tr, c), lambda li, i, me_ref: (li, i, 0))
    in_specs = [wspec] * 3
    in_specs += [pl.BlockSpec((NDEV - 1, tr, c), lambda li, i, me_ref, f=rows_of(l): (0, f(li, i, me_ref), 0))
                 for l in range(nl)]
    in_specs += [pl.BlockSpec((None, tr, c), lambda li, i, me_ref, f=rows_of(l): (me_ref[0], f(li, i, me_ref), 0))
                 for l in range(nl)]
    return pl.pallas_call(
        body,
        grid_spec=pltpu.PrefetchScalarGridSpec(
            num_scalar_prefetch=1, grid=(nl, r // tr), in_specs=in_specs, out_specs=[wspec] * 4),
        out_shape=[jax.ShapeDtypeStruct(w.shape, F32)] * 4, name=name, compiler_params=_params(("parallel", "parallel")),
    )(me, w, m, v, *lands, *owns)


def _conv_rows(cur, halo, p, first):
    r = cur.shape[0]
    ext = jnp.concatenate([jnp.where(first, 0.0, halo), cur], axis=0)
    s1 = pltpu.roll(ext, 1, 0)[HALO:]
    s2 = pltpu.roll(ext, 2, 0)[HALO:]
    return p[0:1] * s2 + p[1:2] * s1 + p[2:3] * cur + p[3:4], s1, s2


def ffn_gate_fwd(h3, p3, tr=ROW_BLOCK):
    _, t, c = h3.shape
    half = NDEV // 2

    def body(a_ref, ah_ref, g_ref, gh_ref, pa_ref, pg_ref, o_ref):
        first = pl.program_id(1) == 0
        ya, _, _ = _conv_rows(a_ref[...].astype(F32), ah_ref[...].astype(F32), pa_ref[...], first)
        yg, _, _ = _conv_rows(g_ref[...].astype(F32), gh_ref[...].astype(F32), pg_ref[...], first)
        o_ref[...] = (jax.nn.silu(yg) * ya).astype(o_ref.dtype)

    main = lambda off: pl.BlockSpec((None, tr, c), lambda j, i: (j + off, i, 0))
    halo = lambda off: pl.BlockSpec((None, HALO, c), lambda j, i: (j + off, jnp.maximum(i * (tr // HALO) - 1, 0), 0))
    par = lambda off: pl.BlockSpec((None, 8, c), lambda j, i: (j + off, 0, 0))
    return pl.pallas_call(
        body, grid=(half, t // tr),
        in_specs=[main(0), halo(0), main(half), halo(half), par(0), par(half)],
        out_specs=pl.BlockSpec((None, tr, c), lambda j, i: (j, i, 0)),
        out_shape=jax.ShapeDtypeStruct((half, t, c), BF16), name="ffn_gate_fwd",
        compiler_params=_params(("parallel", "parallel")),
    )(h3, h3, h3, h3, p3, p3)


def ffn_gate_bwd(h3, dgated3, p3, tr=ROW_BLOCK):
    _, t, c = h3.shape
    half = NDEV // 2

    def body(a_ref, ah_ref, g_ref, gh_ref, dg_ref, pa_ref, pg_ref, dya_ref, dyg_ref, dpa_ref, dpg_ref):
        i = pl.program_id(1)
        first = i == 0
        a = a_ref[...].astype(F32)
        g = g_ref[...].astype(F32)
        ya, a1, a2 = _conv_rows(a, ah_ref[...].astype(F32), pa_ref[...], first)
        yg, g1, g2 = _conv_rows(g, gh_ref[...].astype(F32), pg_ref[...], first)
        d = dg_ref[...].astype(F32)
        sig = jax.nn.sigmoid(yg)
        d_ya = (d * (yg * sig)).astype(dya_ref.dtype)
        d_yg = (d * ya * (sig * (1.0 + yg * (1.0 - sig)))).astype(dyg_ref.dtype)
        dya_ref[...] = d_ya
        dyg_ref[...] = d_yg
        for dy, cur, s1, s2, dp_ref in ((d_ya.astype(F32), a, a1, a2, dpa_ref), (d_yg.astype(F32), g, g1, g2, dpg_ref)):
            rows = [jnp.sum(dy * s2, axis=0, keepdims=True), jnp.sum(dy * s1, axis=0, keepdims=True),
                    jnp.sum(dy * cur, axis=0, keepdims=True), jnp.sum(dy, axis=0, keepdims=True)]
            dp = jnp.concatenate(rows + [jnp.zeros((4, c), F32)], axis=0)

            @pl.when(first)
            def _(dp_ref=dp_ref, dp=dp):
                dp_ref[...] = dp

            @pl.when(i > 0)
            def _(dp_ref=dp_ref, dp=dp):
                dp_ref[...] += dp

    main = lambda off: pl.BlockSpec((None, tr, c), lambda j, i: (j + off, i, 0))
    halo = lambda off: pl.BlockSpec((None, HALO, c), lambda j, i: (j + off, jnp.maximum(i * (tr // HALO) - 1, 0), 0))
    par = lambda off: pl.BlockSpec((None, 8, c), lambda j, i: (j + off, 0, 0))
    return pl.pallas_call(
        body, grid=(half, t // tr),
        in_specs=[main(0), halo(0), main(half), halo(half), main(0), par(0), par(half)],
        out_specs=[main(0), main(0), par(0), par(0)],
        out_shape=[jax.ShapeDtypeStruct((half, t, c), BF16)] * 2 + [jax.ShapeDtypeStruct((half, 8, c), F32)] * 2,
        name="ffn_gate_bwd", compiler_params=_params(("parallel", "arbitrary")),
    )(h3, h3, h3, h3, dgated3, p3, p3)


def ffn_conv_t(dy_a, dy_g, p3, tr=2 * ROW_BLOCK):
    half, t, c = dy_a.shape
    tr = min(tr, t)
    nblk = t // tr

    def body(a_ref, ah_ref, g_ref, gh_ref, p_ref, o_ref):
        is_a = pl.program_id(0) < half
        last = pl.program_id(1) == nblk - 1
        cur = jnp.where(is_a, a_ref[...], g_ref[...]).astype(F32)
        nxt = jnp.where(is_a, ah_ref[...], gh_ref[...]).astype(F32)
        ext = jnp.concatenate([cur, jnp.where(last, 0.0, nxt)], axis=0)
        n = tr + HALO
        s1 = pltpu.roll(ext, n - 1, 0)[:tr]
        s2 = pltpu.roll(ext, n - 2, 0)[:tr]
        p = p_ref[...]
        o_ref[...] = (p[2:3] * cur + p[1:2] * s1 + p[0:1] * s2).astype(o_ref.dtype)

    main = pl.BlockSpec((None, tr, c), lambda j, i: (j % half, i, 0))
    halo = pl.BlockSpec((None, HALO, c), lambda j, i: (j % half, jnp.minimum((i + 1) * (tr // HALO), t // HALO - 1), 0))
    return pl.pallas_call(
        body, grid=(NDEV, nblk),
        in_specs=[main, halo, main, halo, pl.BlockSpec((None, 8, c), lambda j, i: (j, 0, 0))],
        out_specs=pl.BlockSpec((None, tr, c), lambda j, i: (j, i, 0)),
        out_shape=jax.ShapeDtypeStruct((NDEV, t, c), BF16), name="ffn_conv_t",
        compiler_params=_params(("parallel", "parallel")),
    )(dy_a, dy_a, dy_g, dy_g, p3)


def _att_consts(bq, bk):
    lane = lax.broadcasted_iota(jnp.int32, (1, LANES), 1)
    heads = (lane < HEAD_DIM, lane >= HEAD_DIM)
    rr = lax.broadcasted_iota(jnp.int32, (bq, bk), 0)
    cc = lax.broadcasted_iota(jnp.int32, (bq, bk), 1)
    kr = lax.broadcasted_iota(jnp.int32, (bk, bk), 0)
    kc = lax.broadcasted_iota(jnp.int32, (bk, bk), 1)
    return heads, rr, cc, kr, kc


def _split_dot(x, tri, parts):
    out = None
    for _ in range(parts):
        piece = x.astype(BF16)
        x = x - piece.astype(F32)
        term = jnp.dot(piece, tri, preferred_element_type=F32)
        out = term if out is None else out + term
    return out


def _att_logits(qh, k):
    z = lax.dot_general(qh, k, _MM_TB, preferred_element_type=F32)
    lsp = jnp.minimum(z, 0.0) - jnp.log(1.0 + jnp.exp(-jnp.abs(z)))
    return lsp, lsp - z


def _per_head(heads, a, b):
    return jnp.where(heads[0], a, b)


def sb_attn_fwd(qkv):
    t, d3 = qkv.shape
    d = d3 // 3
    npair = d // LANES
    bq, bk = min(ATT_BQ, t), min(ATT_BK, t)
    kpq = bq // bk

    def body(q_ref, k_ref, v_ref, o_ref, lt_ref, acc_ref):
        heads, rr, cc, kr, kc = _att_consts(bq, bk)
        suffix = (kr > kc).astype(BF16)

        def trip(qh, k0, r0, runs):
            k = k_ref[pl.ds(k0, bk), :]
            v = v_ref[pl.ds(k0, bk), :]
            diag, r0 = r0 is not None, r0 or 0
            valid = cc[:bq - r0] < rr[:bq - r0]
            new_runs = []
            for h in range(2):
                lsp, lraw = _att_logits(qh[h][r0:], k)
                lm = jnp.where(valid, lraw, 0.0) if diag else lraw
                w = jnp.exp(lsp + _split_dot(lm, suffix, 1) + runs[h][r0:])
                if diag:
                    w = jnp.where(valid, w, 0.0)
                acc_ref[h, r0:, :] += jnp.dot(w.astype(BF16), v, preferred_element_type=F32)
                below = runs[h][r0:] + jnp.sum(lm, axis=1, keepdims=True)
                new_runs.append(jnp.concatenate([runs[h][:r0], below], axis=0) if r0 else below)
            return tuple(new_runs)

        def q_loop(qb, _):
            q0 = pl.multiple_of(qb * bq, bq)
            q = q_ref[pl.ds(q0, bq), :] * 0.125
            qh = [jnp.where(hm, q, 0.0).astype(BF16) for hm in heads]
            acc_ref[...] = jnp.zeros_like(acc_ref)
            runs = (jnp.zeros((bq, 1), F32),) * 2
            for dblk in reversed(range(kpq)):
                runs = trip(qh, pl.multiple_of(q0 + dblk * bk, bk), dblk * bk, runs)
            nleft = qb * kpq
            runs = lax.fori_loop(
                0, nleft, lambda i, r: trip(qh, pl.multiple_of((nleft - 1 - i) * bk, bk), None, r), runs)
            o_ref[pl.ds(q0, bq), :] = _per_head(heads, acc_ref[0], acc_ref[1])
            lt_ref[pl.ds(q0, bq), :] = _per_head(heads, runs[0], runs[1])
            return 0

        lax.fori_loop(0, t // bq, q_loop, 0)

    col = lambda off: pl.BlockSpec((t, LANES), lambda p: (0, p + off))
    return pl.pallas_call(
        body, grid=(npair,), in_specs=[col(0), col(npair), col(2 * npair)], out_specs=[col(0), col(0)],
        out_shape=[jax.ShapeDtypeStruct((t, d), F32)] * 2, scratch_shapes=[pltpu.VMEM((2, bq, LANES), F32)],
        name="sb_attn_fwd", compiler_params=_params(("parallel",)),
    )(qkv, qkv, qkv)


def sb_attn_bwd(qkv, ltot, do):
    t, d3 = qkv.shape
    d = d3 // 3
    npair = d // LANES
    bq, bk = min(ATT_BQ, t), min(ATT_BK, t)
    kpq = bq // bk

    def body(q_ref, k_ref, v_ref, lt_ref, do_ref, d_ref, dk_acc, dv_acc, dq_acc):
        heads, rr, cc, kr, kc = _att_consts(bq, bk)
        prefix_incl = (kr <= kc).astype(BF16)
        prefix_excl = (kr < kc).astype(BF16)
        dk_acc[...] = jnp.zeros_like(dk_acc)
        dv_acc[...] = jnp.zeros_like(dv_acc)

        def trip(qh, doh, lt, k0, r0, carry):
            lruns, gruns = carry
            k = k_ref[pl.ds(k0, bk), :]
            v = v_ref[pl.ds(k0, bk), :]
            diag, r0 = r0 is not None, r0 or 0
            valid = cc[:bq - r0] < rr[:bq - r0]
            new_lruns, new_gruns = [], []
            dk_blk = jnp.zeros((bk, LANES), F32)
            dv_blk = jnp.zeros((bk, LANES), F32)
            for h in range(2):
                q_rows, do_rows = qh[h][r0:], doh[h][r0:]
                lsp, lraw = _att_logits(q_rows, k)
                lm = jnp.where(valid, lraw, 0.0) if diag else lraw
                right = lt[h][r0:] - (lruns[h][r0:] + _split_dot(lm, prefix_incl, 2))
                w = jnp.exp(lsp + right)
                if diag:
                    w = jnp.where(valid, w, 0.0)
                g = lax.dot_general(do_rows, v, _MM_TB, preferred_element_type=F32) * w
                left = gruns[h][r0:] + _split_dot(g, prefix_excl, 1)
                dz = g * jnp.exp(lraw) - jnp.exp(lsp) * left
                if diag:
                    dz = jnp.where(valid, dz, 0.0)
                dz = dz.astype(BF16)
                kh = jnp.where(heads[h], k, 0.0).astype(BF16)
                dq_acc[h, r0:, :] += jnp.dot(dz, kh, preferred_element_type=F32)
                dk_blk = dk_blk + lax.dot_general(dz, q_rows, _MM_TA, preferred_element_type=F32)
                dv_blk = dv_blk + lax.dot_general(w.astype(BF16), do_rows, _MM_TA, preferred_element_type=F32)
                l_below = lruns[h][r0:] + jnp.sum(lm, axis=1, keepdims=True)
                g_below = gruns[h][r0:] + jnp.sum(g, axis=1, keepdims=True)
                new_lruns.append(jnp.concatenate([lruns[h][:r0], l_below], axis=0) if r0 else l_below)
                new_gruns.append(jnp.concatenate([gruns[h][:r0], g_below], axis=0) if r0 else g_below)
            dk_acc[pl.ds(k0, bk), :] += dk_blk
            dv_acc[pl.ds(k0, bk), :] += dv_blk
            return tuple(new_lruns), tuple(new_gruns)

        def q_loop(qb, _):
            q0 = pl.multiple_of(qb * bq, bq)
            q = q_ref[pl.ds(q0, bq), :] * 0.125
            dout = do_ref[pl.ds(q0, bq), :]
            lt2 = lt_ref[pl.ds(q0, bq), :]
            qh = [jnp.where(hm, q, 0.0).astype(BF16) for hm in heads]
            doh = [jnp.where(hm, dout, 0.0).astype(BF16) for hm in heads]
            lt = [jnp.max(jnp.where(hm, lt2, -jnp.inf), axis=1, keepdims=True) for hm in heads]
            dq_acc[...] = jnp.zeros_like(dq_acc)
            col = (jnp.zeros((bq, 1), F32),) * 2
            carry = lax.fori_loop(
                0, qb * kpq, lambda kb, c: trip(qh, doh, lt, pl.multiple_of(kb * bk, bk), None, c), (col, col))
            for dblk in range(kpq):
                carry = trip(qh, doh, lt, pl.multiple_of(q0 + dblk * bk, bk), dblk * bk, carry)
            d_ref[0, pl.ds(q0, bq), :] = ((dq_acc[0] + dq_acc[1]) * 0.125).astype(d_ref.dtype)
            return 0

        lax.fori_loop(0, t // bq, q_loop, 0)
        d_ref[1] = dk_acc[...].astype(d_ref.dtype)
        d_ref[2] = dv_acc[...].astype(d_ref.dtype)

    col = lambda off: pl.BlockSpec((t, LANES), lambda p: (0, p + off))
    return pl.pallas_call(
        body, grid=(npair,), in_specs=[col(0), col(npair), col(2 * npair), col(0), col(0)],
        out_specs=pl.BlockSpec((3, t, LANES), lambda p: (0, 0, p)),
        out_shape=jax.ShapeDtypeStruct((3, t, d), BF16),
        scratch_shapes=[pltpu.VMEM((t, LANES), F32), pltpu.VMEM((t, LANES), F32), pltpu.VMEM((2, bq, LANES), F32)],
        name="sb_attn_bwd", compiler_params=_params(("parallel",)),
    )(qkv, qkv, qkv, ltot, do)


def _sgu_parts(hin, g, ws_ref, bf_ref):
    width = hin.shape[1] // 2
    h = jax.nn.gelu(hin)
    u, v = h[:, :width], h[:, width:]
    r = lax.rsqrt(jnp.mean(v * v, axis=-1, keepdims=True) + EPS)
    vn = v * r * g
    rr = lax.broadcasted_iota(jnp.int32, (CHUNK, CHUNK), 0)
    cc = lax.broadcasted_iota(jnp.int32, (CHUNK, CHUNK), 1)
    causal = cc <= rr
    wcs = [jnp.where(causal, ws_ref[gi], 0.0).astype(BF16) for gi in range(SG_GROUPS)]
    sv = jnp.concatenate(
        [jnp.dot(wcs[gi], vn[:, gi * CHUNK:(gi + 1) * CHUNK].astype(BF16), preferred_element_type=F32) + bf_ref[gi]
         for gi in range(SG_GROUPS)], axis=1)
    return u, v, r, vn, wcs, sv, causal


def sgu_fwd(hin, g, ws, bfull):
    t, w2 = hin.shape
    width = w2 // 2

    def body(h_ref, g_ref, ws_ref, bf_ref, o_ref):
        u, _, _, _, _, sv, _ = _sgu_parts(h_ref[...].astype(F32), g_ref[...], ws_ref, bf_ref)
        o_ref[...] = (u * sv).astype(o_ref.dtype)

    full = lambda a: pl.BlockSpec(a.shape, lambda i, nd=a.ndim: (0,) * nd)
    return pl.pallas_call(
        body, grid=(t // CHUNK,), in_specs=[pl.BlockSpec((CHUNK, w2), lambda i: (i, 0)), full(g), full(ws), full(bfull)],
        out_specs=pl.BlockSpec((CHUNK, width), lambda i: (i, 0)), out_shape=jax.ShapeDtypeStruct((t, width), BF16),
        name="sgu_fwd", compiler_params=_params(("parallel",)),
    )(hin, g, ws, bfull)


def sgu_bwd(hin, dp, g, ws, bfull):
    t, w2 = hin.shape
    width = w2 // 2

    def body(h_ref, dp_ref, g_ref, ws_ref, bf_ref, dh_ref, dws_ref, dbf_ref, dg_ref):
        i = pl.program_id(0)
        hin_v = h_ref[...].astype(F32)
        gv = g_ref[...]
        u, v, r, vn, wcs, sv, causal = _sgu_parts(hin_v, gv, ws_ref, bf_ref)
        dpv = dp_ref[...].astype(F32)
        du = dpv * sv
        dsv = dpv * u
        dvn_parts, dws_parts, dbf_parts = [], [], []
        for gi in range(SG_GROUPS):
            dsv_g = dsv[:, gi * CHUNK:(gi + 1) * CHUNK]
            dsv_b = dsv_g.astype(BF16)
            dvn_parts.append(lax.dot_general(wcs[gi], dsv_b, _MM_TA, preferred_element_type=F32))
            vn_b = vn[:, gi * CHUNK:(gi + 1) * CHUNK].astype(BF16)
            dws_parts.append(jnp.where(causal, lax.dot_general(dsv_b, vn_b, _MM_TB, preferred_element_type=F32), 0.0))
            dbf_parts.append(jnp.broadcast_to(jnp.sum(dsv_g, axis=1, keepdims=True), (CHUNK, CHUNK)))
        dvn = jnp.concatenate(dvn_parts, axis=1)
        dgain = jnp.sum(dvn * v * r, axis=0, keepdims=True)
        gvv = dvn * gv
        dv = r * gvv - v * (r * r * r) * jnp.mean(v * gvv, axis=-1, keepdims=True)
        _, vjp = jax.vjp(jax.nn.gelu, hin_v)
        dh_ref[...] = vjp(jnp.concatenate([du, dv], axis=1))[0].astype(dh_ref.dtype)

        @pl.when(i == 0)
        def _():
            for gi in range(SG_GROUPS):
                dws_ref[gi] = dws_parts[gi]
                dbf_ref[gi] = dbf_parts[gi]
            dg_ref[...] = dgain

        @pl.when(i > 0)
        def _():
            for gi in range(SG_GROUPS):
                dws_ref[gi] += dws_parts[gi]
                dbf_ref[gi] += dbf_parts[gi]
            dg_ref[...] += dgain

    full = lambda a: pl.BlockSpec(a.shape, lambda i, nd=a.ndim: (0,) * nd)
    sq = (SG_GROUPS, CHUNK, CHUNK)
    return pl.pallas_call(
        body, grid=(t // CHUNK,),
        in_specs=[pl.BlockSpec((CHUNK, w2), lambda i: (i, 0)), pl.BlockSpec((CHUNK, width), lambda i: (i, 0)),
                  full(g), full(ws), full(bfull)],
        out_specs=[pl.BlockSpec((CHUNK, w2), lambda i: (i, 0)), pl.BlockSpec(sq, lambda i: (0, 0, 0)),
                   pl.BlockSpec(sq, lambda i: (0, 0, 0)), pl.BlockSpec((1, width), lambda i: (0, 0))],
        out_shape=[jax.ShapeDtypeStruct((t, w2), BF16), jax.ShapeDtypeStruct(sq, F32), jax.ShapeDtypeStruct(sq, F32),
                   jax.ShapeDtypeStruct((1, width), F32)],
        name="sgu_bwd", compiler_params=_params(("arbitrary",)),
    )(hin, dp, g, ws, bfull)


def _disc1(lam_re, lam_im, log_dt):
    lr = jnp.minimum(lam_re, -1e-4)
    li = lam_im
    dt = jnp.exp(log_dt)
    mag = jnp.exp(dt * lr)
    ar = mag * jnp.cos(dt * li)
    ai = mag * jnp.sin(dt * li)
    den = lr * lr + li * li
    return ar, ai, ((ar - 1.0) * lr + ai * li) / den, (ai * lr - (ar - 1.0) * li) / den


def _disc2(cre, cim, b_re, b_im):
    return cre * b_re - cim * b_im, cre * b_im + cim * b_re


def _single(fn, ins, out_shapes, name):
    n = len(ins)

    def body(*refs):
        vals = fn(*[r[...] for r in refs[:n]])
        for ref, val in zip(refs[n:], vals):
            ref[...] = val

    return pl.pallas_call(body, out_shape=[jax.ShapeDtypeStruct(s, F32) for s in out_shapes], name=name)(*ins)


SCAN_SEGMENTS = 8
REORDER_STEPS = 64


def s5_reorder(x, to_steps):
    t, d = x.shape
    ns = SCAN_SEGMENTS
    seg = t // ns
    ts = min(REORDER_STEPS, seg)
    by_segment = ((ns, seg, d), pl.BlockSpec((ns, ts, d), lambda i: (0, i, 0)))
    by_step = ((seg, ns, d), pl.BlockSpec((ts, ns, d), lambda i: (i, 0, 0)))
    (in_shape, in_spec), (out_shape, out_spec) = (by_segment, by_step) if to_steps else (by_step, by_segment)

    def body(x_ref, o_ref):
        o_ref[...] = jnp.swapaxes(x_ref[...], 0, 1)

    out = pl.pallas_call(
        body, grid=(seg // ts,), in_specs=[in_spec], out_specs=out_spec,
        out_shape=jax.ShapeDtypeStruct(out_shape, x.dtype), name="s5_reorder", compiler_params=_params(("parallel",)),
    )(x.reshape(in_shape))
    return out.reshape(t, d)


def _cpow(ar, ai, n):
    rr, ri = None, None
    while n:
        if n & 1:
            rr, ri = (ar, ai) if rr is None else (rr * ar - ri * ai, rr * ai + ri * ar)
        ar, ai = ar * ar - ai * ai, 2.0 * ar * ai
        n >>= 1
    return rr, ri


def _edge_states(er, ei, pr, pi, reverse):
    ns = SCAN_SEGMENTS
    zero = jnp.zeros_like(er[0:1])
    rows_r, rows_i = [None] * ns, [None] * ns
    order = range(ns - 1, -1, -1) if reverse else range(ns)
    prev = None
    for s in order:
        if prev is None:
            rows_r[s], rows_i[s] = zero, zero
        else:
            cr, ci = rows_r[prev], rows_i[prev]
            rows_r[s] = er[prev:prev + 1] + pr * cr - pi * ci
            rows_i[s] = ei[prev:prev + 1] + pr * ci + pi * cr
        prev = s
    return jnp.concatenate(rows_r, axis=0), jnp.concatenate(rows_i, axis=0)


def s5_scan_fwd(bu2, a2):
    _, t, n = bu2.shape
    cb, ns = SCAN_COLS, SCAN_SEGMENTS
    seg = t // ns

    def body(bu_ref, a_ref, x_ref):
        ar, ai = a_ref[0:1, :], a_ref[1:2, :]

        def local(i, carry):
            xr, xi = carry
            xr, xi = ar * xr - ai * xi + bu_ref[0, i], ar * xi + ai * xr + bu_ref[1, i]
            x_ref[0, i] = xr
            x_ref[1, i] = xi
            return xr, xi

        zero = jnp.zeros((ns, cb), F32)
        er, ei = lax.fori_loop(0, seg, local, (zero, zero))
        cr, ci = _edge_states(er, ei, *_cpow(ar, ai, seg), reverse=False)

        def fix(i, carry):
            wr, wi = carry
            wr, wi = wr * ar - wi * ai, wr * ai + wi * ar
            x_ref[0, i] += wr * cr - wi * ci
            x_ref[1, i] += wr * ci + wi * cr
            return wr, wi

        lax.fori_loop(0, seg, fix, (jnp.ones((1, cb), F32), jnp.zeros((1, cb), F32)))

    blk = pl.BlockSpec((2, seg, ns, cb), lambda j: (0, 0, 0, j))
    out = pl.pallas_call(
        body, grid=(n // cb,), in_specs=[blk, pl.BlockSpec((2, cb), lambda j: (0, j))], out_specs=blk,
        out_shape=jax.ShapeDtypeStruct((2, seg, ns, n), F32), name="s5_scan_fwd", compiler_params=_params(("parallel",)),
    )(bu2.reshape(2, seg, ns, n), a2)
    return out.reshape(2, t, n)


def s5_scan_bwd(dx2, x2, a2):
    _, t, n = dx2.shape
    cb, ns = SCAN_COLS, SCAN_SEGMENTS
    seg = t // ns

    def body(dx_ref, x_ref, a_ref, g_ref, da_ref):
        ar, ai = a_ref[0:1, :], a_ref[1:2, :]

        def local(s, carry):
            gr, gi = carry
            i = seg - 1 - s
            gr, gi = dx_ref[0, i] + ar * gr + ai * gi, dx_ref[1, i] - ai * gr + ar * gi
            g_ref[0, i] = gr
            g_ref[1, i] = gi
            return gr, gi

        zero = jnp.zeros((ns, cb), F32)
        er, ei = lax.fori_loop(0, seg, local, (zero, zero))
        cr, ci = _edge_states(er, ei, *_cpow(ar, -ai, seg), reverse=True)
        row = lax.broadcasted_iota(jnp.int32, (ns, cb), 0)
        before_r = jnp.where(row == 0, 0.0, pltpu.roll(x_ref[0, seg - 1], 1, 0))
        before_i = jnp.where(row == 0, 0.0, pltpu.roll(x_ref[1, seg - 1], 1, 0))

        def fix(s, carry):
            wr, wi, dar, dai = carry
            i = seg - 1 - s
            wr, wi = wr * ar + wi * ai, wi * ar - wr * ai
            gr = g_ref[0, i] + wr * cr - wi * ci
            gi = g_ref[1, i] + wr * ci + wi * cr
            g_ref[0, i] = gr
            g_ref[1, i] = gi
            ip = jnp.maximum(i - 1, 0)
            xpr = jnp.where(i == 0, before_r, x_ref[0, ip])
            xpi = jnp.where(i == 0, before_i, x_ref[1, ip])
            return wr, wi, dar + gr * xpr + gi * xpi, dai + gi * xpr - gr * xpi

        one, z1 = jnp.ones((1, cb), F32), jnp.zeros((1, cb), F32)
        _, _, dar, dai = lax.fori_loop(0, seg, fix, (one, z1, zero, zero))
        da_ref[0:1, :] = jnp.sum(dar, axis=0, keepdims=True)
        da_ref[1:2, :] = jnp.sum(dai, axis=0, keepdims=True)

    blk = pl.BlockSpec((2, seg, ns, cb), lambda j: (0, 0, 0, j))
    vec = pl.BlockSpec((2, cb), lambda j: (0, j))
    g4, da = pl.pallas_call(
        body, grid=(n // cb,), in_specs=[blk, blk, vec], out_specs=[blk, vec],
        out_shape=[jax.ShapeDtypeStruct((2, seg, ns, n), F32), jax.ShapeDtypeStruct((2, n), F32)],
        name="s5_scan_bwd", compiler_params=_params(("parallel",)),
    )(dx2.reshape(2, seg, ns, n), x2.reshape(2, seg, ns, n), a2)
    return g4.reshape(2, t, n), da


_SP_U = SSM_PACK * SSM_GROUP
_SP_X = SSM_PACK * SSM_STATE
_NKB = SSM_GROUPS // SSM_PACK


def mm_s5(kind, a, b, m, name, res=None, tm=2048):
    tm = min(tm, m)
    kw = dict(passes=S5_PASSES, name=name)
    xblk = lambda row, sel, col: ((None, tm, _SP_X), lambda *g: (sel(*g), row(*g), col(*g)))
    if kind == "bu":
        o_blk, o_map = xblk(lambda g, i, k: i, lambda g, i, k: g // _NKB, lambda g, i, k: g % _NKB)
        return _mm(a, b, grid=(2 * _NKB, m // tm, 1), a_blk=(tm, _SP_U), a_map=lambda g, i, k: (i, g % _NKB),
                   b_blk=(None, None, _SP_U, _SP_X), b_map=lambda g, i, k: (g // _NKB, g % _NKB, 0, 0),
                   o_blk=o_blk, o_map=o_map, out_shape=(2, m, _NKB * _SP_X), out_dtype=F32, **kw)
    if kind == "yc":
        a_blk, a_map = xblk(lambda j, i, k: i, lambda j, i, k: k, lambda j, i, k: j)
        return _mm(a, b, grid=(_NKB, m // tm, 2), a_blk=a_blk, a_map=a_map,
                   b_blk=(None, None, _SP_X, _SP_U), b_map=lambda j, i, k: (k, j, 0, 0),
                   o_blk=(tm, _SP_U), o_map=lambda j, i, k: (i, j), out_shape=(m, _NKB * _SP_U), out_dtype=F32,
                   acc_2d=(tm, _SP_U), **kw)
    if kind == "dx":
        o_blk, o_map = xblk(lambda g, i, k: i, lambda g, i, k: g // _NKB, lambda g, i, k: g % _NKB)
        return _mm(a, b, grid=(2 * _NKB, m // tm, 1), a_blk=(tm, _SP_U), a_map=lambda g, i, k: (i, g % _NKB),
                   b_blk=(None, None, _SP_X, _SP_U), b_map=lambda g, i, k: (g // _NKB, g % _NKB, 0, 0),
                   o_blk=o_blk, o_map=o_map, out_shape=(2, m, _NKB * _SP_X), out_dtype=F32, dims=_MM_TB, **kw)
    if kind == "dcd":
        a_blk, a_map = xblk(lambda g, _, k: k, lambda g, _, k: g // _NKB, lambda g, _, k: g % _NKB)
        return _mm(a, b, grid=(2 * _NKB, 1, m // tm), a_blk=a_blk, a_map=a_map,
                   b_blk=(tm, _SP_U), b_map=lambda g, _, k: (k, g % _NKB),
                   o_blk=(None, None, _SP_X, _SP_U), o_map=lambda g, _, k: (g // _NKB, g % _NKB, 0, 0),
                   out_shape=(2, _NKB, _SP_X, _SP_U), out_dtype=F32, dims=_MM_TA, acc_2d=(_SP_X, _SP_U), **kw)
    if kind == "du":
        a_blk, a_map = xblk(lambda j, i, k: i, lambda j, i, k: k, lambda j, i, k: j)
        return _mm(a, b, grid=(_NKB, m // tm, 2), a_blk=a_blk, a_map=a_map,
                   b_blk=(None, None, _SP_U, _SP_X), b_map=lambda j, i, k: (k, j, 0, 0),
                   o_blk=(tm, _SP_U), o_map=lambda j, i, k: (i, j), out_shape=(m, _NKB * _SP_U), out_dtype=F32,
                   dims=_MM_TB, acc_2d=(tm, _SP_U), res=res, res_blk=(tm, _SP_U), res_map=lambda j, i, k: (i, j), **kw)
    assert kind == "dbd"
    b_blk, b_map = xblk(lambda g, _, k: k, lambda g, _, k: g // _NKB, lambda g, _, k: g % _NKB)
    return _mm(a, b, grid=(2 * _NKB, 1, m // tm), a_blk=(tm, _SP_U), a_map=lambda g, _, k: (k, g % _NKB),
               b_blk=b_blk, b_map=b_map,
               o_blk=(None, None, _SP_U, _SP_X), o_map=lambda g, _, k: (g // _NKB, g % _NKB, 0, 0),
               out_shape=(2, _NKB, _SP_U, _SP_X), out_dtype=F32, dims=_MM_TA, acc_2d=(_SP_U, _SP_X), **kw)


def _block_diag(w):
    g, a, b = w.shape
    eye = jnp.eye(SSM_PACK, dtype=w.dtype)
    wp = w.reshape(g // SSM_PACK, SSM_PACK, a, b)
    return jnp.einsum("kgab,gh->kgahb", wp, eye).reshape(g // SSM_PACK, SSM_PACK * a, SSM_PACK * b)


def _block_diag_t(d, a, b):
    k = d.shape[0]
    eye = jnp.eye(SSM_PACK, dtype=d.dtype)
    dp = d.reshape(k, SSM_PACK, a, SSM_PACK, b)
    return jnp.einsum("kgahb,gh->kgab", dp, eye).reshape(k * SSM_PACK, a, b)


def _coords():
    return lax.axis_index("x"), lax.axis_index("y"), lax.axis_index("c")


def all_gather(tensors, name, scatter=None):
    n = len(tensors)
    ns = 0 if scatter is None else 1
    any_spec = pl.BlockSpec(memory_space=pl.ANY)

    def body(*refs):
        ins, outs = refs[:n], refs[n + ns:2 * n + ns]
        send, recv, local = refs[2 * (n + ns):2 * (n + ns) + 3]
        x, y, c = _coords()
        me, sibling = (x, y, c), (x, y, 1 - c)
        chips = [(1 - x, y), (x, 1 - y), (1 - x, 1 - y)]

        def slot(p):
            return 4 * p[0] + 2 * p[1] + p[2]

        def copy(t, k, block, to, src=None):
            dst = outs[t].at[slot(block)]
            return pltpu.make_async_remote_copy(
                src_ref=dst if src is None else src, dst_ref=dst, send_sem=send.at[7 * t + k],
                recv_sem=recv.at[7 * t + k], device_id=to, device_id_type=pl.DeviceIdType.MESH)

        own, sent, landing = [], [], []
        for t in range(n):
            mine = pltpu.make_async_copy(ins[t], outs[t].at[slot(me)], local.at[t])
            mine.start()
            own.append(mine)
            first = [copy(t, 0, me, sibling, src=ins[t])]
            first += [copy(t, 1 + j, me, (*chip, c), src=ins[t]) for j, chip in enumerate(chips)]
            for cp in first:
                cp.start()
            sent += first
        if ns:
            src, dst = refs[n], refs[2 * n + ns]
            s_send, s_recv = refs[2 * (n + ns) + 3:]
            my_slot, peers = _me_and_peers()
            mine = pltpu.make_async_copy(src.at[my_slot], dst.at[my_slot], local.at[n])
            mine.start()
            own.append(mine)
            for k, (dev, peer_slot) in enumerate(peers):
                def piece(dst_slot, k=k, dev=dev, peer_slot=peer_slot):
                    return pltpu.make_async_remote_copy(
                        src_ref=src.at[peer_slot], dst_ref=dst.at[dst_slot], send_sem=s_send.at[k],
                        recv_sem=s_recv.at[k], device_id=dev, device_id_type=pl.DeviceIdType.MESH)

                cp = piece(my_slot)
                cp.start()
                sent.append(cp)
                landing.append(piece(peer_slot))
        for t in range(n):
            for j, chip in enumerate(chips):
                copy(t, 1 + j, (*chip, c), me).wait_recv()
                passed = copy(t, 4 + j, (*chip, c), sibling)
                passed.start()
                sent.append(passed)
        for t in range(n):
            copy(t, 0, sibling, me).wait_recv()
            for j, chip in enumerate(chips):
                copy(t, 4 + j, (*chip, 1 - c), me).wait_recv()
        for cp in landing:
            cp.wait_recv()
        for cp in sent:
            cp.wait_send()
        for cp in own:
            cp.wait()

    scratch = [pltpu.SemaphoreType.DMA((7 * n,)), pltpu.SemaphoreType.DMA((7 * n,)), pltpu.SemaphoreType.DMA((n + ns,))]
    out_shape = [jax.ShapeDtypeStruct((NDEV,) + a.shape, a.dtype) for a in tensors]
    args = list(tensors)
    if ns:
        scratch += [pltpu.SemaphoreType.DMA((NDEV - 1,)), pltpu.SemaphoreType.DMA((NDEV - 1,))]
        out_shape.append(jax.ShapeDtypeStruct(scatter.shape, scatter.dtype))
        args.append(scatter)
    return pl.pallas_call(
        body, in_specs=[any_spec] * (n + ns), out_specs=[any_spec] * (n + ns), out_shape=out_shape,
        scratch_shapes=scratch, name=name,
    )(*args)


_HBM_SPEC = pl.BlockSpec(memory_space=pltpu.HBM)
_SEM_SPEC = pl.BlockSpec(memory_space=pltpu.SEMAPHORE)
_NPEER = NDEV - 1


def _split_copy_params():
    return pltpu.CompilerParams(has_side_effects=pltpu.SideEffectType.DATAFLOW_SIDE_EFFECTING)


def _me_and_peers():
    x, y, c = _coords()
    peers = []
    for rel in range(1, NDEV):
        p = (1 - x if rel & 4 else x, 1 - y if rel & 2 else y, 1 - c if rel & 1 else c)
        peers.append((p, 4 * p[0] + 2 * p[1] + p[2]))
    return 4 * x + 2 * y + c, peers


def _hbm(a):
    return pltpu.with_memory_space_constraint(a, pltpu.HBM)


def gather_start(bufs, name):
    n = len(bufs)

    def body(*refs):
        ins, outs = refs[:n], refs[n:]
        me, peers = _me_and_peers()
        for t in range(n):
            for k, (dev, _) in enumerate(peers):
                pltpu.make_async_remote_copy(
                    src_ref=ins[t].at[me], dst_ref=ins[t].at[me], send_sem=outs[3 * t].at[k],
                    recv_sem=outs[3 * t + 1].at[k], device_id=dev, device_id_type=pl.DeviceIdType.MESH).start()
        outs[3 * n][...] = jnp.zeros_like(outs[3 * n])

    out_shape, out_specs = [], []
    for b in bufs:
        out_shape += [pltpu.SemaphoreType.DMA((_NPEER,)), pltpu.SemaphoreType.DMA((_NPEER,)), pltpu.HBM(b.shape, b.dtype)]
        out_specs += [_SEM_SPEC, _SEM_SPEC, _HBM_SPEC]
    out_shape.append(jax.ShapeDtypeStruct((8, LANES), F32))
    out_specs.append(pl.BlockSpec(memory_space=pltpu.VMEM))
    res = pl.pallas_call(
        body, name=name, out_shape=tuple(out_shape), in_specs=[_HBM_SPEC] * n, out_specs=tuple(out_specs),
        input_output_aliases={t: 3 * t + 2 for t in range(n)}, compiler_params=_split_copy_params(),
    )(*[_hbm(b) for b in bufs])
    return [tuple(res[3 * t:3 * t + 3]) for t in range(n)], res[3 * n]


def gather_wait(started, after, name):
    n = len(started)

    def body(*refs):
        bufs, sems = refs[:n], refs[n:3 * n]
        me, peers = _me_and_peers()
        for t in range(n):
            for k, (dev, slot) in enumerate(peers):
                cp = pltpu.make_async_remote_copy(
                    src_ref=bufs[t].at[me], dst_ref=bufs[t].at[slot], send_sem=sems[2 * t].at[k],
                    recv_sem=sems[2 * t + 1].at[k], device_id=dev, device_id_type=pl.DeviceIdType.MESH)
                cp.wait_recv()
                cp.wait_send()

    args = [s[2] for s in started] + [sem for s in started for sem in s[:2]] + [after]
    res = pl.pallas_call(
        body, name=name, out_shape=tuple(pltpu.HBM(s[2].shape, s[2].dtype) for s in started),
        in_specs=[_HBM_SPEC] * n + [_SEM_SPEC] * (2 * n) + [pl.BlockSpec(memory_space=pl.ANY)],
        out_specs=tuple([_HBM_SPEC] * n), input_output_aliases={t: t for t in range(n)},
        compiler_params=_split_copy_params(),
    )(*args)
    return list(res)


def scatter_start(srcs, name):
    n = len(srcs)
    lands = [lax.empty((_NPEER,) + s.shape[1:], s.dtype) for s in srcs]

    def body(*refs):
        ins, land_refs, outs = refs[:n], refs[n:2 * n], refs[2 * n:]
        _, peers = _me_and_peers()
        for t in range(n):
            for k, (dev, slot) in enumerate(peers):
                pltpu.make_async_remote_copy(
                    src_ref=ins[t].at[slot], dst_ref=land_refs[t].at[k], send_sem=outs[4 * t].at[k],
                    recv_sem=outs[4 * t + 1].at[k], device_id=dev, device_id_type=pl.DeviceIdType.MESH).start()
        outs[4 * n][...] = jnp.zeros_like(outs[4 * n])

    out_shape, out_specs = [], []
    for s, land in zip(srcs, lands):
        out_shape += [pltpu.SemaphoreType.DMA((_NPEER,)), pltpu.SemaphoreType.DMA((_NPEER,)),
                      pltpu.HBM(s.shape, s.dtype), pltpu.HBM(land.shape, land.dtype)]
        out_specs += [_SEM_SPEC, _SEM_SPEC, _HBM_SPEC, _HBM_SPEC]
    out_shape.append(jax.ShapeDtypeStruct((8, LANES), F32))
    out_specs.append(pl.BlockSpec(memory_space=pltpu.VMEM))
    aliases = {t: 4 * t + 2 for t in range(n)}
    aliases.update({n + t: 4 * t + 3 for t in range(n)})
    res = pl.pallas_call(
        body, name=name, out_shape=tuple(out_shape), in_specs=[_HBM_SPEC] * (2 * n), out_specs=tuple(out_specs),
        input_output_aliases=aliases, compiler_params=_split_copy_params(),
    )(*[_hbm(s) for s in srcs], *[_hbm(land) for land in lands])
    return [tuple(res[4 * t:4 * t + 4]) for t in range(n)], res[4 * n]


def scatter_wait(started, after, name):
    n = len(started)

    def body(*refs):
        srcs, land_refs, sems = refs[:n], refs[n:2 * n], refs[2 * n:4 * n]
        _, peers = _me_and_peers()
        for t in range(n):
            for k, (dev, slot) in enumerate(peers):
                cp = pltpu.make_async_remote_copy(
                    src_ref=srcs[t].at[slot], dst_ref=land_refs[t].at[k], send_sem=sems[2 * t].at[k],
                    recv_sem=sems[2 * t + 1].at[k], device_id=dev, device_id_type=pl.DeviceIdType.MESH)
                cp.wait_recv()
                cp.wait_send()

    args = [s[2] for s in started] + [s[3] for s in started] + [sem for s in started for sem in s[:2]] + [after]
    res = pl.pallas_call(
        body, name=name,
        out_shape=tuple([pltpu.HBM(s[2].shape, s[2].dtype) for s in started]
                        + [pltpu.HBM(s[3].shape, s[3].dtype) for s in started]),
        in_specs=[_HBM_SPEC] * (2 * n) + [_SEM_SPEC] * (2 * n) + [pl.BlockSpec(memory_space=pl.ANY)],
        out_specs=tuple([_HBM_SPEC] * (2 * n)), input_output_aliases={t: t for t in range(2 * n)},
        compiler_params=_split_copy_params(),
    )(*args)
    return [(res[t], res[n + t]) for t in range(n)]


_PACK_QUANTUM = 8 * LANES


def _pack(parts, lead=0):
    out = []
    for p in parts:
        head = p.shape[:lead]
        f = p.astype(F32).reshape(head + (-1,))
        pad = (-f.shape[-1]) % _PACK_QUANTUM
        if pad:
            f = jnp.concatenate([f, jnp.zeros(head + (pad,), F32)], axis=-1)
        out.append(f.reshape(head + (-1, LANES)))
    return jnp.concatenate(out, axis=lead)


def _unpack(buf, shapes):
    head = buf.shape[:-2]
    out, r = [], 0
    for s in shapes:
        n = 1
        for v in s:
            n *= v
        nr = -(-n // _PACK_QUANTUM) * 8
        flat = buf[..., r:r + nr, :].reshape(head + (nr * LANES,))[..., :n]
        out.append(flat.reshape(head + tuple(s)))
        r += nr
    return out


BIG = ("sb_w_qkv", "sb_w_o", "sg_w_in", "sg_w_o", "ssm_w_in", "ssm_w_glu", "ffn_w_up", "ffn_w_down")
SMALL_SHARDED = ("norm_g", "ssm_d", "ffn_conv_w")
REPLICATED = ("final_norm_g", "sg_norm_g", "sg_w_s", "sg_b", "ssm_lam_re", "ssm_lam_im", "ssm_log_dt",
              "ssm_b_re", "ssm_b_im", "ssm_c_re", "ssm_c_im", "ffn_conv_b")
WEIGHTS = ("norm_g", "final_norm_g", "sb_w_qkv", "sb_w_o", "sg_w_in", "sg_norm_g", "sg_w_s", "sg_b", "sg_w_o",
           "ssm_w_in", "ssm_lam_re", "ssm_lam_im", "ssm_log_dt", "ssm_b_re", "ssm_b_im", "ssm_c_re", "ssm_c_im",
           "ssm_d", "ssm_w_glu", "ffn_w_up", "ffn_conv_w", "ffn_conv_b", "ffn_w_down")


def _step(x, loss_target, w, m, v):
    t, d = x.shape[1], x.shape[2]
    depth = w["norm_g"].shape[0]
    x0 = x.reshape(t, d)
    tgt = loss_target.reshape(t, d)

    mx, my, mc = _coords()
    me = (4 * mx + 2 * my + mc).astype(jnp.int32).reshape(1)
    shard_pack = _pack([w[k] for k in SMALL_SHARDED])
    gathered_small, = all_gather([shard_pack], name="gather_small_weights")
    mixer_weights = (("sb_w_qkv", "sb_w_o"), ("sg_w_in", "sg_w_o"), ("ssm_w_in", "ssm_w_glu"))
    order = []
    for i in range(depth):
        order += [(k, i // 3) for k in mixer_weights[i % 3]] + [("ffn_w_up", i), ("ffn_w_down", i)]
    as_kept = lambda tree, k: jnp.swapaxes(tree[k], 1, 2) if k == "ffn_w_up" else tree[k]
    pending, token = {}, None
    for group in (order[:4], order[4:]):
        started, token = gather_start([cast_into_slot(as_kept(w, k), l, me) for k, l in group], "gather_weights_start")
        pending.update(zip(group, started))
    wg = {}

    def weights(keys, after):
        missing = [key for key in keys if key not in wg]
        if missing:
            for key, buf in zip(missing, gather_wait([pending[key] for key in missing], after, "gather_weights_wait")):
                wg[key] = buf[:, None]
        return [wg[key] for key in keys]

    ng, sd, cw = _unpack(gathered_small, [w[k].shape for k in SMALL_SHARDED])
    norm_full = jnp.transpose(ng, (1, 2, 0, 3)).reshape(depth, 2, d)
    ssm_d_full = jnp.transpose(sd, (1, 0, 2)).reshape(1, d)
    nc = cw.shape[-1]
    conv_b3 = w["ffn_conv_b"].reshape(depth, NDEV, nc)
    p3 = [jnp.concatenate([cw[:, l], conv_b3[l][:, None, :], jnp.zeros((NDEV, 8 - CONV_K - 1, nc), F32)], axis=1)
          for l in range(depth)]

    g_, p_, h_ = SSM_GROUPS, SSM_STATE, SSM_GROUP
    lam_re, lam_im = w["ssm_lam_re"][0], w["ssm_lam_im"][0]
    log_dt = w["ssm_log_dt"][0].reshape(g_, 1)
    b_re, b_im = w["ssm_b_re"][0].reshape(g_ * p_, h_), w["ssm_b_im"][0].reshape(g_ * p_, h_)
    ar, ai, cre, cim = _single(_disc1, [lam_re, lam_im, log_dt], [(g_, p_)] * 4, "s5_disc1")
    cre_c, cim_c = cre.reshape(g_ * p_, 1), cim.reshape(g_ * p_, 1)
    bbr, bbi = _single(_disc2, [cre_c, cim_c, b_re, b_im], [(g_ * p_, h_)] * 2, "s5_disc2")
    per_group_t = lambda a, r, c: jnp.swapaxes(a.reshape(g_, r, c), 1, 2)
    bd = jnp.stack([_block_diag(per_group_t(bbr, p_, h_)), _block_diag(per_group_t(bbi, p_, h_))])
    cd = jnp.stack([_block_diag(per_group_t(w["ssm_c_re"][0], h_, p_)),
                    -_block_diag(per_group_t(w["ssm_c_im"][0], h_, p_))])
    a2 = jnp.stack([ar.reshape(g_ * p_), ai.reshape(g_ * p_)])

    sg_gain = w["sg_norm_g"]
    sg_ws = w["sg_w_s"][0]
    sg_bfull = jnp.broadcast_to(w["sg_b"][0][:, :, None], sg_ws.shape)

    acts = []
    xc = x0
    xn = rms_fwd(xc, norm_full[0, 0][None], "rms_fwd")
    for i in range(depth):
        mixer, j = i % 3, i // 3
        st = {"x": xc, "xn": xn}
        g0, g1 = norm_full[i, 0][None], norm_full[i, 1][None]
        g_next = norm_full[i + 1, 0][None] if i + 1 < depth else None
        k_in, k_out = [(k, j) for k in mixer_weights[mixer]]
        w_in, = weights([k_in], token if i == 0 else xn)
        if mixer == 0:
            qkv = mm_cs_fwd(xn, w_in, 0, BF16, "qkv_fwd")
            o, ltot = sb_attn_fwd(qkv)
            w_out, = weights([k_out], o)
            x1, xn2 = mm_rs_fwd(o, w_out, 0, xc, F32, "attn_out_fwd", norm=g1)
            st.update(qkv=qkv, o=o, ltot=ltot)
        elif mixer == 1:
            hin = mm_cs_fwd(xn, w_in, 0, BF16, "sg_in_fwd")
            p = sgu_fwd(hin, sg_gain, sg_ws, sg_bfull)
            w_out, = weights([k_out], p)
            x1, xn2 = mm_rs_fwd(p, w_out, 0, xc, F32, "sg_out_fwd", norm=g1)
            st.update(hin=hin, p=p)
        else:
            u = mm_rs_fwd(xn, w_in, 0, None, F32, "ssm_in_fwd")
            u_s = s5_reorder(u, True)
            x2 = s5_scan_fwd(mm_s5("bu", u_s, bd, t, "s5_bu"), a2)
            yc_s = mm_s5("yc", x2, cd, t, "s5_yc")
            yg = s5_post_fwd(s5_reorder(yc_s, False), u, ssm_d_full)
            w_out, = weights([k_out], yg)
            hg = mm_cs_fwd(yg, w_out, 0, BF16, "ssm_glu_fwd")
            x1, xn2 = glu_fwd(hg, xc, g1)
            st.update(u_s=u_s, x2=x2, yc_s=yc_s, yg=yg, hg=hg)
        w_up, w_down = weights([("ffn_w_up", i), ("ffn_w_down", i)], xn2)
        h3 = mm_up_fwd(xn2, w_up, 0, BF16, "ffn_up_fwd")
        gated = ffn_gate_fwd(h3, p3[i])
        if g_next is None:
            xc = mm_down_fwd(gated, w_down, 0, x1, "ffn_down_fwd")
        else:
            xc, xn = mm_down_fwd(gated, w_down, 0, x1, "ffn_down_fwd", norm=g_next)
        st.update(x1=x1, xn2=xn2, h3=h3, gated=gated, g0=g0, g1=g1)
        acts.append(st)

    dx, loss_lanes, d_final_g = loss_head(xc, w["final_norm_g"][None], tgt)
    loss = lax.psum(loss_lanes[0, 0], MESH_AXES)

    scattering = {}
    d_norm = [[None, None] for _ in range(depth)]
    d_p3 = [None] * depth
    rep = {}
    d_ssm_d = None
    token = None

    def scatter(grads_by_key):
        keys = list(grads_by_key)
        started, tok = scatter_start([grads_by_key[key] for key in keys], "scatter_grads_start")
        scattering[tuple(keys)] = started
        return tok

    for i in reversed(range(depth)):
        mixer, j = i % 3, i // 3
        st = acts[i]
        k_in, k_out = [(k, j) for k in mixer_weights[mixer]]
        w_in, w_out, w_up, w_down = weights([k_in, k_out, ("ffn_w_up", i), ("ffn_w_down", i)], None)
        dgated = mm_down_da(dx, w_down, 0, "ffn_down_da", dep=token)
        g_down = mm_down_dw(st["gated"], dx, "ffn_down_dw")
        dy_a, dy_g, dp_a, dp_g = ffn_gate_bwd(st["h3"], dgated, p3[i])
        d_p3[i] = jnp.concatenate([dp_a, dp_g], axis=0)
        dh3 = ffn_conv_t(dy_a, dy_g, p3[i])
        dxn2 = mm_up_da(dh3, w_up, 0, "ffn_up_da")
        g_up = mm_up_dw(st["xn2"], dh3, "ffn_up_dw")
        dx1, d_norm[i][1] = rms_bwd(st["x1"], st["g1"], dxn2, dx, "rms_bwd")
        token = scatter({("ffn_w_down", i): g_down, ("ffn_w_up", i): g_up})
        if mixer == 0:
            do = mm_rs_da(dx1, w_out, 0, BF16, "attn_out_da", dep=token)
            g_out = mm_rs_dw(st["o"], dx1, "attn_out_dw")
            d3 = sb_attn_bwd(st["qkv"], st["ltot"], do)
            g_in = mm_qkv_dw(st["xn"], d3, w_in.shape[3], "qkv_dw")
            token = scatter({k_in: g_in, k_out: g_out})
            dxn = mm_qkv_da(d3, w_in, 0, "qkv_da", dep=token)
        elif mixer == 1:
            dp = mm_rs_da(dx1, w_out, 0, BF16, "sg_out_da", dep=token)
            g_out = mm_rs_dw(st["p"], dx1, "sg_out_dw")
            dhin, d_ws, d_bfull, d_gain = sgu_bwd(st["hin"], dp, sg_gain, sg_ws, sg_bfull)
            rep.update(sg_w_s=d_ws[None], sg_b=d_bfull[None, :, :, 0], sg_norm_g=d_gain)
            dxn = mm_cs_da(dhin, w_in, 0, t, "sg_in_da")
            g_in = mm_cs_dw(st["xn"], dhin, w_in.shape[3], "sg_in_dw")
        else:
            dhg = glu_bwd(st["hg"], dx1, token)
            dyg = mm_cs_da(dhg, w_out, 0, t, "ssm_glu_da")
            g_out = mm_cs_dw(st["yg"], dhg, w_out.shape[3], "ssm_glu_dw")
            dyc_s, du_skip_s, d_ssm_d = s5_post_bwd(st["yc_s"], st["u_s"], ssm_d_full, s5_reorder(dyg, True))
            dx2 = mm_s5("dx", dyc_s, cd, t, "s5_dx")
            dcd = mm_s5("dcd", st["x2"], dyc_s, t, "s5_dcd")
            g2, da2 = s5_scan_bwd(dx2, st["x2"], a2)
            du = s5_reorder(mm_s5("du", g2, bd, t, "s5_du", res=du_skip_s), False)
            dbd = mm_s5("dbd", st["u_s"], g2, t, "s5_dbd")
            from_bd = lambda blk: jnp.swapaxes(_block_diag_t(blk, h_, p_), 1, 2).reshape(g_ * p_, h_)

            def disc2_bwd(c1, c2, b1, b2, t1, t2):
                return jax.vjp(_disc2, c1, c2, b1, b2)[1]((t1, t2))

            d_cre, d_cim, d_b_re, d_b_im = _single(
                disc2_bwd, [cre_c, cim_c, b_re, b_im, from_bd(dbd[0]), from_bd(dbd[1])],
                [(g_ * p_, 1)] * 2 + [(g_ * p_, h_)] * 2, "s5_disc2_bwd")

            def disc1_bwd(l1, l2, ld, t1, t2, t3, t4):
                return jax.vjp(_disc1, l1, l2, ld)[1]((t1, t2, t3, t4))

            d_lam_re, d_lam_im, d_log_dt = _single(
                disc1_bwd, [lam_re, lam_im, log_dt, da2[0].reshape(g_, p_), da2[1].reshape(g_, p_),
                            d_cre.reshape(g_, p_), d_cim.reshape(g_, p_)],
                [(g_, p_), (g_, p_), (g_, 1)], "s5_disc1_bwd")
            from_cd = lambda blk: jnp.swapaxes(_block_diag_t(blk, p_, h_), 1, 2)
            rep.update(ssm_lam_re=d_lam_re[None], ssm_lam_im=d_lam_im[None], ssm_log_dt=d_log_dt.reshape(1, g_),
                       ssm_b_re=d_b_re.reshape(1, g_, p_, h_), ssm_b_im=d_b_im.reshape(1, g_, p_, h_),
                       ssm_c_re=from_cd(dcd[0])[None], ssm_c_im=-from_cd(dcd[1])[None])
            dxn = mm_rs_da(du, w_in, 0, F32, "ssm_in_da")
            g_in = mm_rs_dw(st["xn"], du, "ssm_in_dw")
        dx, d_norm[i][0] = rms_bwd(st["x"], st["g0"], dxn, dx1, "rms_bwd")
        if mixer != 0:
            token = scatter({k_in: g_in, k_out: g_out})

    rep["final_norm_g"] = d_final_g.reshape(d)
    rep["ffn_conv_b"] = jnp.stack([d_p3[l][:, CONV_K, :].reshape(NDEV * nc) for l in range(depth)])

    d_norm_full = jnp.stack([jnp.concatenate(pair, axis=0) for pair in d_norm])
    d_norm_pieces = jnp.transpose(d_norm_full.reshape(depth, 2, NDEV, d // NDEV), (2, 0, 1, 3))
    d_ssm_d_pieces = jnp.transpose(d_ssm_d.reshape(1, NDEV, d // NDEV), (1, 0, 2))
    d_conv_w_pieces = jnp.stack([d_p3[l][:, :CONV_K, :] for l in range(depth)], axis=1)
    small_pieces = _pack([d_norm_pieces, d_ssm_d_pieces, d_conv_w_pieces], lead=1)
    rep_parts, small_received = all_gather([_pack([rep[k] for k in REPLICATED]).astype(BF16)],
                                           name="exchange_small_grads", scatter=small_pieces)
    own, landed = {}, {}
    for keys, started in scattering.items():
        for key, (src, land) in zip(keys, scatter_wait(started, dx, "scatter_grads_wait")):
            own[key], landed[key] = src, land

    grads, deltas, new_m, new_v = {}, {}, {}, {}
    for k in BIG:
        layers = range(w[k].shape[0])
        res = adamw_layers(as_kept(w, k), as_kept(m, k), as_kept(v, k), [landed[(k, l)] for l in layers],
                           [own[(k, l)] for l in layers], me, "adamw")
        grads[k], deltas[k], new_m[k], new_v[k] = [jnp.swapaxes(r, 1, 2) if k == "ffn_w_up" else r for r in res]
    for names, parts in ((SMALL_SHARDED, small_received), (REPLICATED, rep_parts)):
        res = adamw(_pack([w[k] for k in names]), _pack([m[k] for k in names]), _pack([v[k] for k in names]), parts,
                    "adamw_small")
        for tree, buf in zip((grads, deltas, new_m, new_v), res):
            for k, val in zip(names, _unpack(buf, [w[k].shape for k in names])):
                tree[k] = val
    grad_x = dx.reshape(x.shape)
    return (loss, grad_x, *[grads[k] for k in WEIGHTS], *[deltas[k] for k in WEIGHTS],
            *[new_m[k] for k in WEIGHTS], *[new_v[k] for k in WEIGHTS])


def kernel(x, norm_g, final_norm_g, sb_w_qkv, sb_w_o, sg_w_in, sg_norm_g, sg_w_s, sg_b, sg_w_o, ssm_w_in, ssm_lam_re, ssm_lam_im, ssm_log_dt, ssm_b_re, ssm_b_im, ssm_c_re, ssm_c_im, ssm_d, ssm_w_glu, ffn_w_up, ffn_conv_w, ffn_conv_b, ffn_w_down, loss_target, m_norm_g, m_final_norm_g, m_sb_w_qkv, m_sb_w_o, m_sg_w_in, m_sg_norm_g, m_sg_w_s, m_sg_b, m_sg_w_o, m_ssm_w_in, m_ssm_lam_re, m_ssm_lam_im, m_ssm_log_dt, m_ssm_b_re, m_ssm_b_im, m_ssm_c_re, m_ssm_c_im, m_ssm_d, m_ssm_w_glu, m_ffn_w_up, m_ffn_conv_w, m_ffn_conv_b, m_ffn_w_down, v_norm_g, v_final_norm_g, v_sb_w_qkv, v_sb_w_o, v_sg_w_in, v_sg_norm_g, v_sg_w_s, v_sg_b, v_sg_w_o, v_ssm_w_in, v_ssm_lam_re, v_ssm_lam_im, v_ssm_log_dt, v_ssm_b_re, v_ssm_b_im, v_ssm_c_re, v_ssm_c_im, v_ssm_d, v_ssm_w_glu, v_ffn_w_up, v_ffn_conv_w, v_ffn_conv_b, v_ffn_w_down):
    w = dict(zip(WEIGHTS, (norm_g, final_norm_g, sb_w_qkv, sb_w_o, sg_w_in, sg_norm_g, sg_w_s, sg_b, sg_w_o, ssm_w_in,
                           ssm_lam_re, ssm_lam_im, ssm_log_dt, ssm_b_re, ssm_b_im, ssm_c_re, ssm_c_im, ssm_d, ssm_w_glu,
                           ffn_w_up, ffn_conv_w, ffn_conv_b, ffn_w_down)))
    m = dict(zip(WEIGHTS, (m_norm_g, m_final_norm_g, m_sb_w_qkv, m_sb_w_o, m_sg_w_in, m_sg_norm_g, m_sg_w_s, m_sg_b,
                           m_sg_w_o, m_ssm_w_in, m_ssm_lam_re, m_ssm_lam_im, m_ssm_log_dt, m_ssm_b_re, m_ssm_b_im,
                           m_ssm_c_re, m_ssm_c_im, m_ssm_d, m_ssm_w_glu, m_ffn_w_up, m_ffn_conv_w, m_ffn_conv_b,
                           m_ffn_w_down)))
    v = dict(zip(WEIGHTS, (v_norm_g, v_final_norm_g, v_sb_w_qkv, v_sb_w_o, v_sg_w_in, v_sg_norm_g, v_sg_w_s, v_sg_b,
                           v_sg_w_o, v_ssm_w_in, v_ssm_lam_re, v_ssm_lam_im, v_ssm_log_dt, v_ssm_b_re, v_ssm_b_im,
                           v_ssm_c_re, v_ssm_c_im, v_ssm_d, v_ssm_w_glu, v_ffn_w_up, v_ffn_conv_w, v_ffn_conv_b,
                           v_ffn_w_down)))
    return _step(x, loss_target, w, m, v)
```

```python
import functools

import jax
import jax.numpy as jnp
from jax import lax
from jax.experimental import pallas as pl
from jax.experimental.pallas import tpu as pltpu

F32, BF16 = jnp.float32, jnp.bfloat16
MESH_AXES = ("x", "y", "c")
NDEV = 8
EPS = 1e-6
HEAD_DIM = 64
LANES = 128
ATT_BQ, ATT_BK = 2048, 256
CHUNK = 128
SG_GROUPS = 8
SSM_GROUPS, SSM_STATE, SSM_GROUP = 64, 64, 16
SSM_PACK = 8
S5_PASSES = 1
CONV_K = 3
HALO = 16
ROW_BLOCK = 512
SCAN_COLS = 256
ADAM_LR, ADAM_B1, ADAM_B2, ADAM_EPS, ADAM_WD, ADAM_STEP = 0.001, 0.9, 0.999, 1e-08, 0.01, 10
VMEM_LIMIT = 56 * 1024 * 1024

_MM = (((1,), (0,)), ((), ()))
_MM_TB = (((1,), (1,)), ((), ()))
_MM_TA = (((0,), (0,)), ((), ()))


def _params(sem):
    return pltpu.CompilerParams(dimension_semantics=sem, vmem_limit_bytes=VMEM_LIMIT)


def _rows(total, cap, mult=16):
    best = None
    for d in range(mult, min(total, cap) + 1, mult):
        if total % d == 0:
            best = d
    return best if best is not None else total


def _dot(a, b, dims, passes):
    if passes == 1:
        return lax.dot_general(a.astype(BF16), b.astype(BF16), dims, preferred_element_type=F32)
    a = a.astype(F32)
    b = b.astype(F32)
    ah = a.astype(BF16)
    bh = b.astype(BF16)
    al = (a - ah.astype(F32)).astype(BF16)
    bl = (b - bh.astype(F32)).astype(BF16)
    out = lax.dot_general(ah, bh, dims, preferred_element_type=F32)
    out = out + lax.dot_general(al, bh, dims, preferred_element_type=F32)
    return out + lax.dot_general(ah, bl, dims, preferred_element_type=F32)


def _mm(a, b, *, grid, a_blk, a_map, b_blk, b_map, o_blk, o_map, out_shape, out_dtype, name,
        dims=_MM, passes=1, res=None, res_blk=None, res_map=None, b_2d=None, acc_2d=None, dep=None, norm=None):
    nk = grid[2]
    has_res, has_norm = res is not None, norm is not None
    a_maps = list(a_map) if isinstance(a_map, (list, tuple)) else [a_map]
    b_maps = list(b_map) if isinstance(b_map, (list, tuple)) else [b_map]
    na, nb = len(a_maps), len(b_maps)
    n_in = na + nb + has_res + has_norm + (dep is not None)

    def body(*refs):
        o_ref = refs[n_in]
        r_ref = refs[na + nb] if has_res else None
        av = refs[0][...] if na == 1 else jnp.concatenate([r[...] for r in refs[:na]], axis=-1)
        bv = refs[na][...] if nb == 1 else jnp.concatenate([r[...] for r in refs[na:na + nb]], axis=-1)
        if b_2d is not None:
            bv = bv.reshape(b_2d)
        part = _dot(av, bv, dims, passes)

        def finish(total):
            if has_res:
                total = total + r_ref[...].astype(F32)
            o_ref[...] = total.reshape(o_ref.shape).astype(o_ref.dtype)
            if has_norm:
                refs[n_in + 1][...] = _rms(total, refs[na + nb + has_res][...]).astype(BF16)

        if nk == 1:
            finish(part)
        else:
            acc_ref = refs[-1]
            k = pl.program_id(2)

            @pl.when(k == 0)
            def _():
                acc_ref[...] = part

            @pl.when(k > 0)
            def _():
                acc_ref[...] += part

            @pl.when(k == nk - 1)
            def _():
                finish(acc_ref[...])

    in_specs = [pl.BlockSpec(a_blk, f) for f in a_maps] + [pl.BlockSpec(b_blk, f) for f in b_maps]
    args = [a] * na + [b] * nb
    if has_res:
        in_specs.append(pl.BlockSpec(res_blk, res_map))
        args.append(res)
    if has_norm:
        in_specs.append(pl.BlockSpec(norm.shape, lambda *_: (0, 0)))
        args.append(norm)
    if dep is not None:
        in_specs.append(pl.BlockSpec(memory_space=pl.ANY))
        args.append(dep)
    scratch = [pltpu.VMEM(acc_2d, F32)] if nk > 1 else []
    out_specs, out_shapes = pl.BlockSpec(o_blk, o_map), jax.ShapeDtypeStruct(out_shape, out_dtype)
    if has_norm:
        out_specs, out_shapes = [out_specs] * 2, [out_shapes, jax.ShapeDtypeStruct(out_shape, BF16)]
    return pl.pallas_call(
        body, grid=grid, in_specs=in_specs, out_specs=out_specs, out_shape=out_shapes, scratch_shapes=scratch,
        name=name, compiler_params=_params(("parallel", "parallel", "arbitrary")),
    )(*args)


def _cs_act_spec(ns, tm, row_of, col_of):
    if ns % LANES == 0:
        return (tm, ns), lambda *g: (row_of(*g), col_of(*g))
    return (None, tm, ns), lambda *g: (col_of(*g), row_of(*g), 0)


def mm_cs_fwd(a, w4, l, out_dtype, name, tm=2048):
    m, k = a.shape
    tm = min(tm, m)
    ns = w4.shape[3]
    o_blk, o_map = _cs_act_spec(ns, tm, lambda j, i, kk: i, lambda j, i, kk: j)
    out_shape = (m, NDEV * ns) if ns % LANES == 0 else (NDEV, m, ns)
    return _mm(a, w4, grid=(NDEV, m // tm, 1), a_blk=(tm, k), a_map=lambda j, i, kk: (i, 0),
               b_blk=(None, None, k, ns), b_map=lambda j, i, kk: (j, l, 0, 0),
               o_blk=o_blk, o_map=o_map, out_shape=out_shape, out_dtype=out_dtype, name=name)


def mm_cs_da(dc, w4, l, m, name, tm=1024):
    k, ns = w4.shape[2], w4.shape[3]
    tm = min(tm, m)
    a_blk, a_map = _cs_act_spec(ns, tm, lambda i, _, j: i, lambda i, _, j: j)
    return _mm(dc, w4, grid=(m // tm, 1, NDEV), a_blk=a_blk, a_map=a_map,
               b_blk=(None, None, k, ns), b_map=lambda i, _, j: (j, l, 0, 0),
               o_blk=(tm, k), o_map=lambda i, _, j: (i, 0), out_shape=(m, k), out_dtype=F32,
               dims=_MM_TB, acc_2d=(tm, k), name=name)


def mm_cs_dw(a, dc, ns, name, tk=2048):
    m, k = a.shape
    tk = min(tk, m)
    b_blk, b_map = _cs_act_spec(ns, tk, lambda j, _, kk: kk, lambda j, _, kk: j)
    return _mm(a, dc, grid=(NDEV, 1, m // tk), a_blk=(tk, k), a_map=lambda j, _, kk: (kk, 0),
               b_blk=b_blk, b_map=b_map, o_blk=(None, k, ns), o_map=lambda j, _, kk: (j, 0, 0),
               out_shape=(NDEV, k, ns), out_dtype=BF16, dims=_MM_TA, acc_2d=(k, ns), name=name)


def mm_up_fwd(a, wt4, l, out_dtype, name, tm=2048):
    m, k = a.shape
    tm = min(tm, m)
    ns = wt4.shape[2]
    return _mm(a, wt4, grid=(NDEV, m // tm, 1), a_blk=(tm, k), a_map=lambda j, i, kk: (i, 0),
               b_blk=(None, None, ns, k), b_map=lambda j, i, kk: (j, l, 0, 0), dims=_MM_TB,
               o_blk=(None, tm, ns), o_map=lambda j, i, kk: (j, i, 0), out_shape=(NDEV, m, ns), out_dtype=out_dtype,
               name=name)


def mm_up_da(dc3, wt4, l, name, tm=1024):
    _, m, ns = dc3.shape
    tm = min(tm, m)
    k = wt4.shape[3]
    return _mm(dc3, wt4, grid=(m // tm, 1, NDEV), a_blk=(None, tm, ns), a_map=lambda i, _, j: (j, i, 0),
               b_blk=(None, None, ns, k), b_map=lambda i, _, j: (j, l, 0, 0),
               o_blk=(tm, k), o_map=lambda i, _, j: (i, 0), out_shape=(m, k), out_dtype=F32, acc_2d=(tm, k), name=name)


def mm_up_dw(a, dc3, name, tk=2048):
    m, k = a.shape
    tk = min(tk, m)
    ns = dc3.shape[2]
    return _mm(dc3, a, grid=(NDEV, 1, m // tk), a_blk=(None, tk, ns), a_map=lambda j, _, kk: (j, kk, 0),
               b_blk=(tk, k), b_map=lambda j, _, kk: (kk, 0), dims=_MM_TA,
               o_blk=(None, ns, k), o_map=lambda j, _, kk: (j, 0, 0), out_shape=(NDEV, ns, k), out_dtype=BF16,
               acc_2d=(ns, k), name=name)


def mm_rs_fwd(a, w4, l, res, out_dtype, name, tm=1024, norm=None):
    m, k = a.shape
    tm = min(tm, m)
    ks, n = w4.shape[2], w4.shape[3]
    return _mm(a, w4, grid=(m // tm, 1, 1), a_blk=(tm, k), a_map=lambda i, _, kk: (i, 0),
               b_blk=(NDEV, None, ks, n), b_map=lambda i, _, kk: (0, l, 0, 0), b_2d=(k, n),
               o_blk=(tm, n), o_map=lambda i, _, kk: (i, 0), out_shape=(m, n), out_dtype=out_dtype,
               res=res, res_blk=(tm, n), res_map=lambda i, _, kk: (i, 0), name=name, norm=norm)


def mm_rs_da(dc, w4, l, out_dtype, name, tm=1024, dep=None):
    m, n = dc.shape
    tm = min(tm, m)
    ks = w4.shape[2]
    k = NDEV * ks
    return _mm(dc, w4, grid=(m // tm, 1, 1), a_blk=(tm, n), a_map=lambda i, _, kk: (i, 0),
               b_blk=(NDEV, None, ks, n), b_map=lambda i, _, kk: (0, l, 0, 0), b_2d=(k, n),
               o_blk=(tm, k), o_map=lambda i, _, kk: (i, 0), out_shape=(m, k), out_dtype=out_dtype,
               dims=_MM_TB, name=name, dep=dep)


def mm_rs_dw(a, dc, name, tk=1024):
    m, k = a.shape
    tk = min(tk, m)
    n = dc.shape[1]
    ks = k // NDEV
    return _mm(a, dc, grid=(1, 1, m // tk), a_blk=(tk, k), a_map=lambda _, __, kk: (kk, 0),
               b_blk=(tk, n), b_map=lambda _, __, kk: (kk, 0),
               o_blk=(NDEV, ks, n), o_map=lambda _, __, kk: (0, 0, 0), out_shape=(NDEV, ks, n),
               out_dtype=BF16, dims=_MM_TA, acc_2d=(k, n), name=name)


def mm_down_fwd(a3, w4, l, res, name, tm=1024, norm=None):
    nj, m, kc = a3.shape
    tm = min(tm, m)
    ks, n = w4.shape[2], w4.shape[3]
    return _mm(a3, w4, grid=(m // tm, 1, nj), a_blk=(None, tm, kc), a_map=lambda i, _, j: (j, i, 0),
               b_blk=(2, None, ks, n), b_map=lambda i, _, j: (j, l, 0, 0), b_2d=(kc, n),
               o_blk=(tm, n), o_map=lambda i, _, j: (i, 0), out_shape=(m, n), out_dtype=F32,
               res=res, res_blk=(tm, n), res_map=lambda i, _, j: (i, 0), acc_2d=(tm, n), name=name, norm=norm)


def mm_down_da(dc, w4, l, name, tm=2048, dep=None):
    m, n = dc.shape
    tm = min(tm, m)
    ks = w4.shape[2]
    kc = 2 * ks
    nj = NDEV // 2
    return _mm(dc, w4, grid=(nj, m // tm, 1), a_blk=(tm, n), a_map=lambda j, i, _: (i, 0),
               b_blk=(2, None, ks, n), b_map=lambda j, i, _: (j, l, 0, 0), b_2d=(kc, n),
               o_blk=(None, tm, kc), o_map=lambda j, i, _: (j, i, 0), out_shape=(nj, m, kc),
               out_dtype=BF16, dims=_MM_TB, name=name, dep=dep)


def mm_down_dw(a3, dc, name, tk=2048):
    nj, m, kc = a3.shape
    tk = min(tk, m)
    n = dc.shape[1]
    return _mm(a3, dc, grid=(nj, 1, m // tk), a_blk=(None, tk, kc), a_map=lambda j, _, kk: (j, kk, 0),
               b_blk=(tk, n), b_map=lambda j, _, kk: (kk, 0),
               o_blk=(2, kc // 2, n), o_map=lambda j, _, kk: (j, 0, 0), out_shape=(NDEV, kc // 2, n),
               out_dtype=BF16, dims=_MM_TA, acc_2d=(kc, n), name=name)


def _qkv_group_maps(d, ns, row_of, piece_of):
    per_arr, per_piece = d // LANES, ns // LANES

    def group_map(q):
        def f(*g):
            grp = piece_of(*g) * per_piece + q
            return grp // per_arr, row_of(*g), grp % per_arr
        return f

    return [group_map(q) for q in range(per_piece)]


def mm_qkv_da(d3, w4, l, name, tm=1024, dep=None):
    _, m, d = d3.shape
    tm = min(tm, m)
    k, ns = w4.shape[2], w4.shape[3]
    return _mm(d3, w4, grid=(m // tm, 1, NDEV),
               a_blk=(None, tm, LANES), a_map=_qkv_group_maps(d, ns, lambda i, _, j: i, lambda i, _, j: j),
               b_blk=(None, None, k, ns), b_map=lambda i, _, j: (j, l, 0, 0),
               o_blk=(tm, k), o_map=lambda i, _, j: (i, 0), out_shape=(m, k), out_dtype=F32,
               dims=_MM_TB, acc_2d=(tm, k), name=name, dep=dep)


def mm_qkv_dw(a, d3, ns, name, tk=2048):
    m, k = a.shape
    tk = min(tk, m)
    d = d3.shape[2]
    return _mm(a, d3, grid=(NDEV, 1, m // tk), a_blk=(tk, k), a_map=lambda j, _, kk: (kk, 0),
               b_blk=(None, tk, LANES), b_map=_qkv_group_maps(d, ns, lambda j, _, kk: kk, lambda j, _, kk: j),
               o_blk=(None, k, ns), o_map=lambda j, _, kk: (j, 0, 0),
               out_shape=(NDEV, k, ns), out_dtype=BF16, dims=_MM_TA, acc_2d=(k, ns), name=name)


def _rowwise(fn, ins, outs, *, tr, name, acc_outs=()):
    rows = next(a.shape[0] if kind == "row" else a.shape[1] for a, kind in ins if kind in ("row", "row3"))
    n_in, n_out = len(ins), len(outs)
    n_read = sum(kind != "dep" for _, kind in ins)

    def body(*refs):
        vals = fn(*[r[...] for r in refs[:n_read]])
        if not isinstance(vals, (tuple, list)):
            vals = (vals,)
        for ref, val in zip(refs[n_in:n_in + n_out], vals[:n_out]):
            ref[...] = val.astype(ref.dtype)
        i = pl.program_id(0)
        for ref, val in zip(refs[n_in + n_out:], vals[n_out:]):
            val = val.astype(ref.dtype)

            @pl.when(i == 0)
            def _(ref=ref, val=val):
                ref[...] = val

            @pl.when(i > 0)
            def _(ref=ref, val=val):
                ref[...] += val

    in_specs = []
    for a, kind in ins:
        if kind == "row":
            in_specs.append(pl.BlockSpec((tr, a.shape[1]), lambda i: (i, 0)))
        elif kind == "row3":
            in_specs.append(pl.BlockSpec((a.shape[0], tr, a.shape[2]), lambda i: (0, i, 0)))
        elif kind == "dep":
            in_specs.append(pl.BlockSpec(memory_space=pl.ANY))
        else:
            in_specs.append(pl.BlockSpec(a.shape, lambda i, nd=a.ndim: (0,) * nd))
    out_specs = [pl.BlockSpec((tr, c), lambda i: (i, 0)) for c, _ in outs]
    out_specs += [pl.BlockSpec(s, lambda i, nd=len(s): (0,) * nd) for s, _ in acc_outs]
    out_shape = [jax.ShapeDtypeStruct((rows, c), dt) for c, dt in outs]
    out_shape += [jax.ShapeDtypeStruct(s, dt) for s, dt in acc_outs]
    res = pl.pallas_call(
        body, grid=(rows // tr,), in_specs=in_specs, out_specs=out_specs, out_shape=out_shape, name=name,
        compiler_params=_params(("arbitrary",) if acc_outs else ("parallel",)),
    )(*[a for a, _ in ins])
    return res


def _rms(x, g):
    return x * lax.rsqrt(jnp.mean(x * x, axis=-1, keepdims=True) + EPS) * g


def cast_into_slot(w, l, me):
    _, r, c = w.shape
    tr = _rows(r, 512)

    def body(me_ref, w_ref, o_ref):
        o_ref[...] = w_ref[...].astype(o_ref.dtype)

    return pl.pallas_call(
        body,
        grid_spec=pltpu.PrefetchScalarGridSpec(
            num_scalar_prefetch=1, grid=(r // tr,),
            in_specs=[pl.BlockSpec((None, tr, c), lambda i, me_ref: (l, i, 0))],
            out_specs=pl.BlockSpec((None, tr, c), lambda i, me_ref: (me_ref[0], i, 0))),
        out_shape=jax.ShapeDtypeStruct((NDEV, r, c), BF16), name="cast_into_slot",
        compiler_params=_params(("parallel",)),
    )(me, w)


def rms_fwd(x, g, name):
    out, = _rowwise(_rms, [(x, "row"), (g, "full")], [(x.shape[1], BF16)], tr=ROW_BLOCK, name=name)
    return out


def rms_bwd(x, g, dy, dres, name):
    def fn(xv, gv, dyv, drv):
        _, vjp = jax.vjp(_rms, xv, gv)
        dx, dg = vjp(dyv.astype(F32))
        return drv + dx, dg

    d = x.shape[1]
    return _rowwise(fn, [(x, "row"), (g, "full"), (dy, "row"), (dres, "row")], [(d, F32)], tr=ROW_BLOCK, name=name,
                    acc_outs=[((1, d), F32)])


def loss_head(x, g, tgt):
    def f(xv, gv, tv):
        err = jnp.square(_rms(xv, gv) - tv)
        return 0.5 * jnp.sum(jnp.mean(err, axis=-1))

    def fn(xv, gv, tv):
        val, (dx, dg) = jax.value_and_grad(f, argnums=(0, 1))(xv, gv, tv)
        return dx, jnp.full((1, LANES), val, F32), dg

    d = x.shape[1]
    return _rowwise(fn, [(x, "row"), (g, "full"), (tgt, "row")], [(d, F32)], tr=ROW_BLOCK, name="loss_head",
                    acc_outs=[((1, LANES), F32), ((1, d), F32)])


def _glu(hg, x):
    half = hg.shape[1] // 2
    return x + hg[:, :half] * jax.nn.sigmoid(hg[:, half:])


def glu_fwd(hg, x, norm):
    def fn(h, xv, g):
        x1 = _glu(h.astype(F32), xv)
        return x1, _rms(x1, g)

    d = x.shape[1]
    return _rowwise(fn, [(hg, "row"), (x, "row"), (norm, "full")], [(d, F32), (d, BF16)], tr=ROW_BLOCK, name="glu_fwd")


def glu_bwd(hg, dx1, dep):
    def fn(h, d):
        _, vjp = jax.vjp(lambda hv: _glu(hv, jnp.zeros_like(d)), h.astype(F32))
        return vjp(d)[0]

    out, = _rowwise(fn, [(hg, "row"), (dx1, "row"), (dep, "dep")], [(hg.shape[1], BF16)], tr=ROW_BLOCK, name="glu_bwd")
    return out


def _s5_post(yc, u, d):
    return jax.nn.gelu(yc + d * u)


def s5_post_fwd(yc, u, d):
    out, = _rowwise(_s5_post, [(yc, "row"), (u, "row"), (d, "full")], [(yc.shape[1], BF16)], tr=ROW_BLOCK,
                    name="s5_post_fwd")
    return out


def s5_post_bwd(yc, u, d, dyg):
    def fn(ycv, uv, dv, g):
        _, vjp = jax.vjp(_s5_post, ycv, uv, dv)
        return vjp(g.astype(F32))

    dm = yc.shape[1]
    return _rowwise(fn, [(yc, "row"), (u, "row"), (d, "full"), (dyg, "row")], [(dm, F32), (dm, F32)], tr=ROW_BLOCK,
                    name="s5_post_bwd", acc_outs=[((1, dm), F32)])


def _adam_update(wv, mv, vv, g):
    m2 = ADAM_B1 * mv + (1.0 - ADAM_B1) * g
    v2 = ADAM_B2 * vv + (1.0 - ADAM_B2) * jnp.square(g)
    m_hat = m2 / (1.0 - ADAM_B1 ** ADAM_STEP)
    v_hat = v2 / (1.0 - ADAM_B2 ** ADAM_STEP)
    delta = -ADAM_LR * (m_hat / (jnp.sqrt(v_hat) + ADAM_EPS) + ADAM_WD * wv)
    return g, delta, m2, v2


def adamw(w, m, v, g_parts, name):
    def fn(wv, mv, vv, gp):
        g = gp[0].astype(F32)
        for p in range(1, gp.shape[0]):
            g = g + gp[p].astype(F32)
        return _adam_update(wv, mv, vv, g)

    c = w.shape[1]
    return _rowwise(fn, [(w, "row"), (m, "row"), (v, "row"), (g_parts, "row3")], [(c, F32)] * 4,
                    tr=_rows(w.shape[0], 256), name=name)


def adamw_layers(w, m, v, lands, owns, me, name):
    nl, r, c = w.shape
    tr = _rows(r, 256)

    def body(me_ref, w_ref, m_ref, v_ref, *rest):
        land_refs, own_refs, out_refs = rest[:nl], rest[nl:2 * nl], rest[2 * nl:]
        for l in range(nl):
            @pl.when(pl.program_id(0) == l)
            def _(l=l):
                g = own_refs[l][...].astype(F32)
                for p in range(NDEV - 1):
                    g = g + land_refs[l][p].astype(F32)
                for ref, val in zip(out_refs, _adam_update(w_ref[...], m_ref[...], v_ref[...], g)):
                    ref[...] = val

    def rows_of(l):
        return lambda li, i, me_ref: jnp.where(li == l, i, 0)

    wspec = pl.BlockSpec((None, tr, c), lambda li, i, me_ref: (li, i, 0))
    in_specs = [wspec] * 3
    in_specs += [pl.BlockSpec((NDEV - 1, tr, c), lambda li, i, me_ref, f=rows_of(l): (0, f(li, i, me_ref), 0))
                 for l in range(nl)]
    in_specs += [pl.BlockSpec((None, tr, c), lambda li, i, me_ref, f=rows_of(l): (me_ref[0], f(li, i, me_ref), 0))
                 for l in range(nl)]
    return pl.pallas_call(
        body,
        grid_spec=pltpu.PrefetchScalarGridSpec(
            num_scalar_prefetch=1, grid=(nl, r // tr), in_specs=in_specs, out_specs=[wspec] * 4),
        out_shape=[jax.ShapeDtypeStruct(w.shape, F32)] * 4, name=name, compiler_params=_params(("parallel", "parallel")),
    )(me, w, m, v, *lands, *owns)


def _conv_rows(cur, halo, p, first):
    r = cur.shape[0]
    ext = jnp.concatenate([jnp.where(first, 0.0, halo), cur], axis=0)
    s1 = pltpu.roll(ext, 1, 0)[HALO:]
    s2 = pltpu.roll(ext, 2, 0)[HALO:]
    return p[0:1] * s2 + p[1:2] * s1 + p[2:3] * cur + p[3:4], s1, s2


def ffn_gate_fwd(h3, p3, tr=ROW_BLOCK):
    _, t, c = h3.shape
    half = NDEV // 2

    def body(a_ref, ah_ref, g_ref, gh_ref, pa_ref, pg_ref, o_ref):
        first = pl.program_id(1) == 0
        ya, _, _ = _conv_rows(a_ref[...].astype(F32), ah_ref[...].astype(F32), pa_ref[...], first)
        yg, _, _ = _conv_rows(g_ref[...].astype(F32), gh_ref[...].astype(F32), pg_ref[...], first)
        o_ref[...] = (jax.nn.silu(yg) * ya).astype(o_ref.dtype)

    main = lambda off: pl.BlockSpec((None, tr, c), lambda j, i: (j + off, i, 0))
    halo = lambda off: pl.BlockSpec((None, HALO, c), lambda j, i: (j + off, jnp.maximum(i * (tr // HALO) - 1, 0), 0))
    par = lambda off: pl.BlockSpec((None, 8, c), lambda j, i: (j + off, 0, 0))
    return pl.pallas_call(
        body, grid=(half, t // tr),
        in_specs=[main(0), halo(0), main(half), halo(half), par(0), par(half)],
        out_specs=pl.BlockSpec((None, tr, c), lambda j, i: (j, i, 0)),
        out_shape=jax.ShapeDtypeStruct((half, t, c), BF16), name="ffn_gate_fwd",
        compiler_params=_params(("parallel", "parallel")),
    )(h3, h3, h3, h3, p3, p3)


def ffn_gate_bwd(h3, dgated3, p3, tr=ROW_BLOCK):
    _, t, c = h3.shape
    half = NDEV // 2

    def body(a_ref, ah_ref, g_ref, gh_ref, dg_ref, pa_ref, pg_ref, dya_ref, dyg_ref, dpa_ref, dpg_ref):
        i = pl.program_id(1)
        first = i == 0
        a = a_ref[...].astype(F32)
        g = g_ref[...].astype(F32)
        ya, a1, a2 = _conv_rows(a, ah_ref[...].astype(F32), pa_ref[...], first)
        yg, g1, g2 = _conv_rows(g, gh_ref[...].astype(F32), pg_ref[...], first)
        d = dg_ref[...].astype(F32)
        sig = jax.nn.sigmoid(yg)
        d_ya = (d * (yg * sig)).astype(dya_ref.dtype)
        d_yg = (d * ya * (sig * (1.0 + yg * (1.0 - sig)))).astype(dyg_ref.dtype)
        dya_ref[...] = d_ya
        dyg_ref[...] = d_yg
        for dy, cur, s1, s2, dp_ref in ((d_ya.astype(F32), a, a1, a2, dpa_ref), (d_yg.astype(F32), g, g1, g2, dpg_ref)):
            rows = [jnp.sum(dy * s2, axis=0, keepdims=True), jnp.sum(dy * s1, axis=0, keepdims=True),
                    jnp.sum(dy * cur, axis=0, keepdims=True), jnp.sum(dy, axis=0, keepdims=True)]
            dp = jnp.concatenate(rows + [jnp.zeros((4, c), F32)], axis=0)

            @pl.when(first)
            def _(dp_ref=dp_ref, dp=dp):
                dp_ref[...] = dp

            @pl.when(i > 0)
            def _(dp_ref=dp_ref, dp=dp):
                dp_ref[...] += dp

    main = lambda off: pl.BlockSpec((None, tr, c), lambda j, i: (j + off, i, 0))
    halo = lambda off: pl.BlockSpec((None, HALO, c), lambda j, i: (j + off, jnp.maximum(i * (tr // HALO) - 1, 0), 0))
    par = lambda off: pl.BlockSpec((None, 8, c), lambda j, i: (j + off, 0, 0))
    return pl.pallas_call(
        body, grid=(half, t // tr),
        in_specs=[main(0), halo(0), main(half), halo(half), main(0), par(0), par(half)],
        out_specs=[main(0), main(0), par(0), par(0)],
        out_shape=[jax.ShapeDtypeStruct((half, t, c), BF16)] * 2 + [jax.ShapeDtypeStruct((half, 8, c), F32)] * 2,
        name="ffn_gate_bwd", compiler_params=_params(("parallel", "arbitrary")),
    )(h3, h3, h3, h3, dgated3, p3, p3)


def ffn_conv_t(dy_a, dy_g, p3, tr=2 * ROW_BLOCK):
    half, t, c = dy_a.shape
    tr = min(tr, t)
    nblk = t // tr

    def body(a_ref, ah_ref, g_ref, gh_ref, p_ref, o_ref):
        is_a = pl.program_id(0) < half
        last = pl.program_id(1) == nblk - 1
        cur = jnp.where(is_a, a_ref[...], g_ref[...]).astype(F32)
        nxt = jnp.where(is_a, ah_ref[...], gh_ref[...]).astype(F32)
        ext = jnp.concatenate([cur, jnp.where(last, 0.0, nxt)], axis=0)
        n = tr + HALO
        s1 = pltpu.roll(ext, n - 1, 0)[:tr]
        s2 = pltpu.roll(ext, n - 2, 0)[:tr]
        p = p_ref[...]
        o_ref[...] = (p[2:3] * cur + p[1:2] * s1 + p[0:1] * s2).astype(o_ref.dtype)

    main = pl.BlockSpec((None, tr, c), lambda j, i: (j % half, i, 0))
    halo = pl.BlockSpec((None, HALO, c), lambda j, i: (j % half, jnp.minimum((i + 1) * (tr // HALO), t // HALO - 1), 0))
    return pl.pallas_call(
        body, grid=(NDEV, nblk),
        in_specs=[main, halo, main, halo, pl.BlockSpec((None, 8, c), lambda j, i: (j, 0, 0))],
        out_specs=pl.BlockSpec((None, tr, c), lambda j, i: (j, i, 0)),
        out_shape=jax.ShapeDtypeStruct((NDEV, t, c), BF16), name="ffn_conv_t",
        compiler_params=_params(("parallel", "parallel")),
    )(dy_a, dy_a, dy_g, dy_g, p3)


def _att_consts(bq, bk):
    lane = lax.broadcasted_iota(jnp.int32, (1, LANES), 1)
    heads = (lane < HEAD_DIM, lane >= HEAD_DIM)
    rr = lax.broadcasted_iota(jnp.int32, (bq, bk), 0)
    cc = lax.broadcasted_iota(jnp.int32, (bq, bk), 1)
    kr = lax.broadcasted_iota(jnp.int32, (bk, bk), 0)
    kc = lax.broadcasted_iota(jnp.int32, (bk, bk), 1)
    return heads, rr, cc, kr, kc


def _split_dot(x, tri, parts):
    out = None
    for _ in range(parts):
        piece = x.astype(BF16)
        x = x - piece.astype(F32)
        term = jnp.dot(piece, tri, preferred_element_type=F32)
        out = term if out is None else out + term
    return out


def _att_logits(qh, k):
    z = lax.dot_general(qh, k, _MM_TB, preferred_element_type=F32)
    lsp = jnp.minimum(z, 0.0) - jnp.log(1.0 + jnp.exp(-jnp.abs(z)))
    return lsp, lsp - z


def _per_head(heads, a, b):
    return jnp.where(heads[0], a, b)


def sb_attn_fwd(qkv):
    t, d3 = qkv.shape
    d = d3 // 3
    npair = d // LANES
    bq, bk = min(ATT_BQ, t), min(ATT_BK, t)
    kpq = bq // bk

    def body(q_ref, k_ref, v_ref, o_ref, lt_ref, acc_ref):
        heads, rr, cc, kr, kc = _att_consts(bq, bk)
        suffix = (kr > kc).astype(BF16)

        def trip(qh, k0, r0, runs):
            k = k_ref[pl.ds(k0, bk), :]
            v = v_ref[pl.ds(k0, bk), :]
            diag, r0 = r0 is not None, r0 or 0
            valid = cc[:bq - r0] < rr[:bq - r0]
            new_runs = []
            for h in range(2):
                lsp, lraw = _att_logits(qh[h][r0:], k)
                lm = jnp.where(valid, lraw, 0.0) if diag else lraw
                w = jnp.exp(lsp + _split_dot(lm, suffix, 2) + runs[h][r0:])
                if diag:
                    w = jnp.where(valid, w, 0.0)
                acc_ref[h, r0:, :] += jnp.dot(w.astype(BF16), v, preferred_element_type=F32)
                below = runs[h][r0:] + jnp.sum(lm, axis=1, keepdims=True)
                new_runs.append(jnp.concatenate([runs[h][:r0], below], axis=0) if r0 else below)
            return tuple(new_runs)

        def q_loop(qb, _):
            q0 = pl.multiple_of(qb * bq, bq)
            q = q_ref[pl.ds(q0, bq), :] * 0.125
            qh = [jnp.where(hm, q, 0.0).astype(BF16) for hm in heads]
            acc_ref[...] = jnp.zeros_like(acc_ref)
            runs = (jnp.zeros((bq, 1), F32),) * 2
            for dblk in reversed(range(kpq)):
                runs = trip(qh, pl.multiple_of(q0 + dblk * bk, bk), dblk * bk, runs)
            nleft = qb * kpq
            runs = lax.fori_loop(
                0, nleft, lambda i, r: trip(qh, pl.multiple_of((nleft - 1 - i) * bk, bk), None, r), runs)
            o_ref[pl.ds(q0, bq), :] = _per_head(heads, acc_ref[0], acc_ref[1])
            lt_ref[pl.ds(q0, bq), :] = _per_head(heads, runs[0], runs[1])
            return 0

        lax.fori_loop(0, t // bq, q_loop, 0)

    col = lambda off: pl.BlockSpec((t, LANES), lambda p: (0, p + off))
    return pl.pallas_call(
        body, grid=(npair,), in_specs=[col(0), col(npair), col(2 * npair)], out_specs=[col(0), col(0)],
        out_shape=[jax.ShapeDtypeStruct((t, d), F32)] * 2, scratch_shapes=[pltpu.VMEM((2, bq, LANES), F32)],
        name="sb_attn_fwd", compiler_params=_params(("parallel",)),
    )(qkv, qkv, qkv)


def sb_attn_bwd(qkv, ltot, do):
    t, d3 = qkv.shape
    d = d3 // 3
    npair = d // LANES
    bq, bk = min(ATT_BQ, t), min(ATT_BK, t)
    kpq = bq // bk

    def body(q_ref, k_ref, v_ref, lt_ref, do_ref, d_ref, dk_acc, dv_acc, dq_acc):
        heads, rr, cc, kr, kc = _att_consts(bq, bk)
        prefix_incl = (kr <= kc).astype(BF16)
        prefix_excl = (kr < kc).astype(BF16)
        dk_acc[...] = jnp.zeros_like(dk_acc)
        dv_acc[...] = jnp.zeros_like(dv_acc)

        def trip(qh, doh, lt, k0, r0, carry):
            lruns, gruns = carry
            k = k_ref[pl.ds(k0, bk), :]
            v = v_ref[pl.ds(k0, bk), :]
            diag, r0 = r0 is not None, r0 or 0
            valid = cc[:bq - r0] < rr[:bq - r0]
            new_lruns, new_gruns = [], []
            dk_blk = jnp.zeros((bk, LANES), F32)
            dv_blk = jnp.zeros((bk, LANES), F32)
            for h in range(2):
                q_rows, do_rows = qh[h][r0:], doh[h][r0:]
                lsp, lraw = _att_logits(q_rows, k)
                lm = jnp.where(valid, lraw, 0.0) if diag else lraw
                right = lt[h][r0:] - (lruns[h][r0:] + _split_dot(lm, prefix_incl, 2))
                w = jnp.exp(lsp + right)
                if diag:
                    w = jnp.where(valid, w, 0.0)
                g = lax.dot_general(do_rows, v, _MM_TB, preferred_element_type=F32) * w
                left = gruns[h][r0:] + _split_dot(g, prefix_excl, 1)
                dz = g * jnp.exp(lraw) - jnp.exp(lsp) * left
                if diag:
                    dz = jnp.where(valid, dz, 0.0)
                dz = dz.astype(BF16)
                kh = jnp.where(heads[h], k, 0.0).astype(BF16)
                dq_acc[h, r0:, :] += jnp.dot(dz, kh, preferred_element_type=F32)
                dk_blk = dk_blk + lax.dot_general(dz, q_rows, _MM_TA, preferred_element_type=F32)
                dv_blk = dv_blk + lax.dot_general(w.astype(BF16), do_rows, _MM_TA, preferred_element_type=F32)
                l_below = lruns[h][r0:] + jnp.sum(lm, axis=1, keepdims=True)
                g_below = gruns[h][r0:] + jnp.sum(g, axis=1, keepdims=True)
                new_lruns.append(jnp.concatenate([lruns[h][:r0], l_below], axis=0) if r0 else l_below)
                new_gruns.append(jnp.concatenate([gruns[h][:r0], g_below], axis=0) if r0 else g_below)
            dk_acc[pl.ds(k0, bk), :] += dk_blk
            dv_acc[pl.ds(k0, bk), :] += dv_blk
            return tuple(new_lruns), tuple(new_gruns)

        def q_loop(qb, _):
            q0 = pl.multiple_of(qb * bq, bq)
            q = q_ref[pl.ds(q0, bq), :] * 0.125
            dout = do_ref[pl.ds(q0, bq), :]
            lt2 = lt_ref[pl.ds(q0, bq), :]
            qh = [jnp.where(hm, q, 0.0).astype(BF16) for hm in heads]
            doh = [jnp.where(hm, dout, 0.0).astype(BF16) for hm in heads]
            lt = [jnp.max(jnp.where(hm, lt2, -jnp.inf), axis=1, keepdims=True) for hm in heads]
            dq_acc[...] = jnp.zeros_like(dq_acc)
            col = (jnp.zeros((bq, 1), F32),) * 2
            carry = lax.fori_loop(
                0, qb * kpq, lambda kb, c: trip(qh, doh, lt, pl.multiple_of(kb * bk, bk), None, c), (col, col))
            for dblk in range(kpq):
                carry = trip(qh, doh, lt, pl.multiple_of(q0 + dblk * bk, bk), dblk * bk, carry)
            d_ref[0, pl.ds(q0, bq), :] = ((dq_acc[0] + dq_acc[1]) * 0.125).astype(d_ref.dtype)
            return 0

        lax.fori_loop(0, t // bq, q_loop, 0)
        d_ref[1] = dk_acc[...].astype(d_ref.dtype)
        d_ref[2] = dv_acc[...].astype(d_ref.dtype)

    col = lambda off: pl.BlockSpec((t, LANES), lambda p: (0, p + off))
    return pl.pallas_call(
        body, grid=(npair,), in_specs=[col(0), col(npair), col(2 * npair), col(0), col(0)],
        out_specs=pl.BlockSpec((3, t, LANES), lambda p: (0, 0, p)),
        out_shape=jax.ShapeDtypeStruct((3, t, d), BF16),
        scratch_shapes=[pltpu.VMEM((t, LANES), F32), pltpu.VMEM((t, LANES), F32), pltpu.VMEM((2, bq, LANES), F32)],
        name="sb_attn_bwd", compiler_params=_params(("parallel",)),
    )(qkv, qkv, qkv, ltot, do)


def _sgu_parts(hin, g, ws_ref, bf_ref):
    width = hin.shape[1] // 2
    h = jax.nn.gelu(hin)
    u, v = h[:, :width], h[:, width:]
    r = lax.rsqrt(jnp.mean(v * v, axis=-1, keepdims=True) + EPS)
    vn = v * r * g
    rr = lax.broadcasted_iota(jnp.int32, (CHUNK, CHUNK), 0)
    cc = lax.broadcasted_iota(jnp.int32, (CHUNK, CHUNK), 1)
    causal = cc <= rr
    wcs = [jnp.where(causal, ws_ref[gi], 0.0).astype(BF16) for gi in range(SG_GROUPS)]
    sv = jnp.concatenate(
        [jnp.dot(wcs[gi], vn[:, gi * CHUNK:(gi + 1) * CHUNK].astype(BF16), preferred_element_type=F32) + bf_ref[gi]
         for gi in range(SG_GROUPS)], axis=1)
    return u, v, r, vn, wcs, sv, causal


def sgu_fwd(hin, g, ws, bfull):
    t, w2 = hin.shape
    width = w2 // 2

    def body(h_ref, g_ref, ws_ref, bf_ref, o_ref):
        u, _, _, _, _, sv, _ = _sgu_parts(h_ref[...].astype(F32), g_ref[...], ws_ref, bf_ref)
        o_ref[...] = (u * sv).astype(o_ref.dtype)

    full = lambda a: pl.BlockSpec(a.shape, lambda i, nd=a.ndim: (0,) * nd)
    return pl.pallas_call(
        body, grid=(t // CHUNK,), in_specs=[pl.BlockSpec((CHUNK, w2), lambda i: (i, 0)), full(g), full(ws), full(bfull)],
        out_specs=pl.BlockSpec((CHUNK, width), lambda i: (i, 0)), out_shape=jax.ShapeDtypeStruct((t, width), BF16),
        name="sgu_fwd", compiler_params=_params(("parallel",)),
    )(hin, g, ws, bfull)


def sgu_bwd(hin, dp, g, ws, bfull):
    t, w2 = hin.shape
    width = w2 // 2

    def body(h_ref, dp_ref, g_ref, ws_ref, bf_ref, dh_ref, dws_ref, dbf_ref, dg_ref):
        i = pl.program_id(0)
        hin_v = h_ref[...].astype(F32)
        gv = g_ref[...]
        u, v, r, vn, wcs, sv, causal = _sgu_parts(hin_v, gv, ws_ref, bf_ref)
        dpv = dp_ref[...].astype(F32)
        du = dpv * sv
        dsv = dpv * u
        dvn_parts, dws_parts, dbf_parts = [], [], []
        for gi in range(SG_GROUPS):
            dsv_g = dsv[:, gi * CHUNK:(gi + 1) * CHUNK]
            dsv_b = dsv_g.astype(BF16)
            dvn_parts.append(lax.dot_general(wcs[gi], dsv_b, _MM_TA, preferred_element_type=F32))
            vn_b = vn[:, gi * CHUNK:(gi + 1) * CHUNK].astype(BF16)
            dws_parts.append(jnp.where(causal, lax.dot_general(dsv_b, vn_b, _MM_TB, preferred_element_type=F32), 0.0))
            dbf_parts.append(jnp.broadcast_to(jnp.sum(dsv_g, axis=1, keepdims=True), (CHUNK, CHUNK)))
        dvn = jnp.concatenate(dvn_parts, axis=1)
        dgain = jnp.sum(dvn * v * r, axis=0, keepdims=True)
        gvv = dvn * gv
        dv = r * gvv - v * (r * r * r) * jnp.mean(v * gvv, axis=-1, keepdims=True)
        _, vjp = jax.vjp(jax.nn.gelu, hin_v)
        dh_ref[...] = vjp(jnp.concatenate([du, dv], axis=1))[0].astype(dh_ref.dtype)

        @pl.when(i == 0)
        def _():
            for gi in range(SG_GROUPS):
                dws_ref[gi] = dws_parts[gi]
                dbf_ref[gi] = dbf_parts[gi]
            dg_ref[...] = dgain

        @pl.when(i > 0)
        def _():
            for gi in range(SG_GROUPS):
                dws_ref[gi] += dws_parts[gi]
                dbf_ref[gi] += dbf_parts[gi]
            dg_ref[...] += dgain

    full = lambda a: pl.BlockSpec(a.shape, lambda i, nd=a.ndim: (0,) * nd)
    sq = (SG_GROUPS, CHUNK, CHUNK)
    return pl.pallas_call(
        body, grid=(t // CHUNK,),
        in_specs=[pl.BlockSpec((CHUNK, w2), lambda i: (i, 0)), pl.BlockSpec((CHUNK, width), lambda i: (i, 0)),
                  full(g), full(ws), full(bfull)],
        out_specs=[pl.BlockSpec((CHUNK, w2), lambda i: (i, 0)), pl.BlockSpec(sq, lambda i: (0, 0, 0)),
                   pl.BlockSpec(sq, lambda i: (0, 0, 0)), pl.BlockSpec((1, width), lambda i: (0, 0))],
        out_shape=[jax.ShapeDtypeStruct((t, w2), BF16), jax.ShapeDtypeStruct(sq, F32), jax.ShapeDtypeStruct(sq, F32),
                   jax.ShapeDtypeStruct((1, width), F32)],
        name="sgu_bwd", compiler_params=_params(("arbitrary",)),
    )(hin, dp, g, ws, bfull)


def _disc1(lam_re, lam_im, log_dt):
    lr = jnp.minimum(lam_re, -1e-4)
    li = lam_im
    dt = jnp.exp(log_dt)
    mag = jnp.exp(dt * lr)
    ar = mag * jnp.cos(dt * li)
    ai = mag * jnp.sin(dt * li)
    den = lr * lr + li * li
    return ar, ai, ((ar - 1.0) * lr + ai * li) / den, (ai * lr - (ar - 1.0) * li) / den


def _disc2(cre, cim, b_re, b_im):
    return cre * b_re - cim * b_im, cre * b_im + cim * b_re


def _single(fn, ins, out_shapes, name):
    n = len(ins)

    def body(*refs):
        vals = fn(*[r[...] for r in refs[:n]])
        for ref, val in zip(refs[n:], vals):
            ref[...] = val

    return pl.pallas_call(body, out_shape=[jax.ShapeDtypeStruct(s, F32) for s in out_shapes], name=name)(*ins)


SCAN_SEGMENTS = 8
REORDER_STEPS = 64


def s5_reorder(x, to_steps):
    t, d = x.shape
    ns = SCAN_SEGMENTS
    seg = t // ns
    ts = min(REORDER_STEPS, seg)
    by_segment = ((ns, seg, d), pl.BlockSpec((ns, ts, d), lambda i: (0, i, 0)))
    by_step = ((seg, ns, d), pl.BlockSpec((ts, ns, d), lambda i: (i, 0, 0)))
    (in_shape, in_spec), (out_shape, out_spec) = (by_segment, by_step) if to_steps else (by_step, by_segment)

    def body(x_ref, o_ref):
        o_ref[...] = jnp.swapaxes(x_ref[...], 0, 1)

    out = pl.pallas_call(
        body, grid=(seg // ts,), in_specs=[in_spec], out_specs=out_spec,
        out_shape=jax.ShapeDtypeStruct(out_shape, x.dtype), name="s5_reorder", compiler_params=_params(("parallel",)),
    )(x.reshape(in_shape))
    return out.reshape(t, d)


def _cpow(ar, ai, n):
    rr, ri = None, None
    while n:
        if n & 1:
            rr, ri = (ar, ai) if rr is None else (rr * ar - ri * ai, rr * ai + ri * ar)
        ar, ai = ar * ar - ai * ai, 2.0 * ar * ai
        n >>= 1
    return rr, ri


def _edge_states(er, ei, pr, pi, reverse):
    ns = SCAN_SEGMENTS
    zero = jnp.zeros_like(er[0:1])
    rows_r, rows_i = [None] * ns, [None] * ns
    order = range(ns - 1, -1, -1) if reverse else range(ns)
    prev = None
    for s in order:
        if prev is None:
            rows_r[s], rows_i[s] = zero, zero
        else:
            cr, ci = rows_r[prev], rows_i[prev]
            rows_r[s] = er[prev:prev + 1] + pr * cr - pi * ci
            rows_i[s] = ei[prev:prev + 1] + pr * ci + pi * cr
        prev = s
    return jnp.concatenate(rows_r, axis=0), jnp.concatenate(rows_i, axis=0)


def s5_scan_fwd(bu2, a2):
    _, t, n = bu2.shape
    cb, ns = SCAN_COLS, SCAN_SEGMENTS
    seg = t // ns

    def body(bu_ref, a_ref, x_ref):
        ar, ai = a_ref[0:1, :], a_ref[1:2, :]

        def local(i, carry):
            xr, xi = carry
            xr, xi = ar * xr - ai * xi + bu_ref[0, i], ar * xi + ai * xr + bu_ref[1, i]
            x_ref[0, i] = xr
            x_ref[1, i] = xi
            return xr, xi

        zero = jnp.zeros((ns, cb), F32)
        er, ei = lax.fori_loop(0, seg, local, (zero, zero))
        cr, ci = _edge_states(er, ei, *_cpow(ar, ai, seg), reverse=False)

        def fix(i, carry):
            wr, wi = carry
            wr, wi = wr * ar - wi * ai, wr * ai + wi * ar
            x_ref[0, i] += wr * cr - wi * ci
            x_ref[1, i] += wr * ci + wi * cr
            return wr, wi

        lax.fori_loop(0, seg, fix, (jnp.ones((1, cb), F32), jnp.zeros((1, cb), F32)))

    blk = pl.BlockSpec((2, seg, ns, cb), lambda j: (0, 0, 0, j))
    out = pl.pallas_call(
        body, grid=(n // cb,), in_specs=[blk, pl.BlockSpec((2, cb), lambda j: (0, j))], out_specs=blk,
        out_shape=jax.ShapeDtypeStruct((2, seg, ns, n), F32), name="s5_scan_fwd", compiler_params=_params(("parallel",)),
    )(bu2.reshape(2, seg, ns, n), a2)
    return out.reshape(2, t, n)


def s5_scan_bwd(dx2, x2, a2):
    _, t, n = dx2.shape
    cb, ns = SCAN_COLS, SCAN_SEGMENTS
    seg = t // ns

    def body(dx_ref, x_ref, a_ref, g_ref, da_ref):
        ar, ai = a_ref[0:1, :], a_ref[1:2, :]

        def local(s, carry):
            gr, gi = carry
            i = seg - 1 - s
            gr, gi = dx_ref[0, i] + ar * gr + ai * gi, dx_ref[1, i] - ai * gr + ar * gi
            g_ref[0, i] = gr
            g_ref[1, i] = gi
            return gr, gi

        zero = jnp.zeros((ns, cb), F32)
        er, ei = lax.fori_loop(0, seg, local, (zero, zero))
        cr, ci = _edge_states(er, ei, *_cpow(ar, -ai, seg), reverse=True)
        row = lax.broadcasted_iota(jnp.int32, (ns, cb), 0)
        before_r = jnp.where(row == 0, 0.0, pltpu.roll(x_ref[0, seg - 1], 1, 0))
        before_i = jnp.where(row == 0, 0.0, pltpu.roll(x_ref[1, seg - 1], 1, 0))

        def fix(s, carry):
            wr, wi, dar, dai = carry
            i = seg - 1 - s
            wr, wi = wr * ar + wi * ai, wi * ar - wr * ai
            gr = g_ref[0, i] + wr * cr - wi * ci
            gi = g_ref[1, i] + wr * ci + wi * cr
            g_ref[0, i] = gr
            g_ref[1, i] = gi
            ip = jnp.maximum(i - 1, 0)
            xpr = jnp.where(i == 0, before_r, x_ref[0, ip])
            xpi = jnp.where(i == 0, before_i, x_ref[1, ip])
            return wr, wi, dar + gr * xpr + gi * xpi, dai + gi * xpr - gr * xpi

        one, z1 = jnp.ones((1, cb), F32), jnp.zeros((1, cb), F32)
        _, _, dar, dai = lax.fori_loop(0, seg, fix, (one, z1, zero, zero))
        da_ref[0:1, :] = jnp.sum(dar, axis=0, keepdims=True)
        da_ref[1:2, :] = jnp.sum(dai, axis=0, keepdims=True)

    blk = pl.BlockSpec((2, seg, ns, cb), lambda j: (0, 0, 0, j))
    vec = pl.BlockSpec((2, cb), lambda j: (0, j))
    g4, da = pl.pallas_call(
        body, grid=(n // cb,), in_specs=[blk, blk, vec], out_specs=[blk, vec],
        out_shape=[jax.ShapeDtypeStruct((2, seg, ns, n), F32), jax.ShapeDtypeStruct((2, n), F32)],
        name="s5_scan_bwd", compiler_params=_params(("parallel",)),
    )(dx2.reshape(2, seg, ns, n), x2.reshape(2, seg, ns, n), a2)
    return g4.reshape(2, t, n), da


_SP_U = SSM_PACK * SSM_GROUP
_SP_X = SSM_PACK * SSM_STATE
_NKB = SSM_GROUPS // SSM_PACK


def mm_s5(kind, a, b, m, name, res=None, tm=2048):
    tm = min(tm, m)
    kw = dict(passes=S5_PASSES, name=name)
    xblk = lambda row, sel, col: ((None, tm, _SP_X), lambda *g: (sel(*g), row(*g), col(*g)))
    if kind == "bu":
        o_blk, o_map = xblk(lambda g, i, k: i, lambda g, i, k: g // _NKB, lambda g, i, k: g % _NKB)
        return _mm(a, b, grid=(2 * _NKB, m // tm, 1), a_blk=(tm, _SP_U), a_map=lambda g, i, k: (i, g % _NKB),
                   b_blk=(None, None, _SP_U, _SP_X), b_map=lambda g, i, k: (g // _NKB, g % _NKB, 0, 0),
                   o_blk=o_blk, o_map=o_map, out_shape=(2, m, _NKB * _SP_X), out_dtype=F32, **kw)
    if kind == "yc":
        a_blk, a_map = xblk(lambda j, i, k: i, lambda j, i, k: k, lambda j, i, k: j)
        return _mm(a, b, grid=(_NKB, m // tm, 2), a_blk=a_blk, a_map=a_map,
                   b_blk=(None, None, _SP_X, _SP_U), b_map=lambda j, i, k: (k, j, 0, 0),
                   o_blk=(tm, _SP_U), o_map=lambda j, i, k: (i, j), out_shape=(m, _NKB * _SP_U), out_dtype=F32,
                   acc_2d=(tm, _SP_U), **kw)
    if kind == "dx":
        o_blk, o_map = xblk(lambda g, i, k: i, lambda g, i, k: g // _NKB, lambda g, i, k: g % _NKB)
        return _mm(a, b, grid=(2 * _NKB, m // tm, 1), a_blk=(tm, _SP_U), a_map=lambda g, i, k: (i, g % _NKB),
                   b_blk=(None, None, _SP_X, _SP_U), b_map=lambda g, i, k: (g // _NKB, g % _NKB, 0, 0),
                   o_blk=o_blk, o_map=o_map, out_shape=(2, m, _NKB * _SP_X), out_dtype=F32, dims=_MM_TB, **kw)
    if kind == "dcd":
        a_blk, a_map = xblk(lambda g, _, k: k, lambda g, _, k: g // _NKB, lambda g, _, k: g % _NKB)
        return _mm(a, b, grid=(2 * _NKB, 1, m // tm), a_blk=a_blk, a_map=a_map,
                   b_blk=(tm, _SP_U), b_map=lambda g, _, k: (k, g % _NKB),
                   o_blk=(None, None, _SP_X, _SP_U), o_map=lambda g, _, k: (g // _NKB, g % _NKB, 0, 0),
                   out_shape=(2, _NKB, _SP_X, _SP_U), out_dtype=F32, dims=_MM_TA, acc_2d=(_SP_X, _SP_U), **kw)
    if kind == "du":
        a_blk, a_map = xblk(lambda j, i, k: i, lambda j, i, k: k, lambda j, i, k: j)
        return _mm(a, b, grid=(_NKB, m // tm, 2), a_blk=a_blk, a_map=a_map,
                   b_blk=(None, None, _SP_U, _SP_X), b_map=lambda j, i, k: (k, j, 0, 0),
                   o_blk=(tm, _SP_U), o_map=lambda j, i, k: (i, j), out_shape=(m, _NKB * _SP_U), out_dtype=F32,
                   dims=_MM_TB, acc_2d=(tm, _SP_U), res=res, res_blk=(tm, _SP_U), res_map=lambda j, i, k: (i, j), **kw)
    assert kind == "dbd"
    b_blk, b_map = xblk(lambda g, _, k: k, lambda g, _, k: g // _NKB, lambda g, _, k: g % _NKB)
    return _mm(a, b, grid=(2 * _NKB, 1, m // tm), a_blk=(tm, _SP_U), a_map=lambda g, _, k: (k, g % _NKB),
               b_blk=b_blk, b_map=b_map,
               o_blk=(None, None, _SP_U, _SP_X), o_map=lambda g, _, k: (g // _NKB, g % _NKB, 0, 0),
               out_shape=(2, _NKB, _SP_U, _SP_X), out_dtype=F32, dims=_MM_TA, acc_2d=(_SP_U, _SP_X), **kw)


def _block_diag(w):
    g, a, b = w.shape
    eye = jnp.eye(SSM_PACK, dtype=w.dtype)
    wp = w.reshape(g // SSM_PACK, SSM_PACK, a, b)
    return jnp.einsum("kgab,gh->kgahb", wp, eye).reshape(g // SSM_PACK, SSM_PACK * a, SSM_PACK * b)


def _block_diag_t(d, a, b):
    k = d.shape[0]
    eye = jnp.eye(SSM_PACK, dtype=d.dtype)
    dp = d.reshape(k, SSM_PACK, a, SSM_PACK, b)
    return jnp.einsum("kgahb,gh->kgab", dp, eye).reshape(k * SSM_PACK, a, b)


def _coords():
    return lax.axis_index("x"), lax.axis_index("y"), lax.axis_index("c")


def all_gather(tensors, name, scatter=None):
    n = len(tensors)
    ns = 0 if scatter is None else 1
    any_spec = pl.BlockSpec(memory_space=pl.ANY)

    def body(*refs):
        ins, outs = refs[:n], refs[n + ns:2 * n + ns]
        send, recv, local = refs[2 * (n + ns):2 * (n + ns) + 3]
        x, y, c = _coords()
        me, sibling = (x, y, c), (x, y, 1 - c)
        chips = [(1 - x, y), (x, 1 - y), (1 - x, 1 - y)]

        def slot(p):
            return 4 * p[0] + 2 * p[1] + p[2]

        def copy(t, k, block, to, src=None):
            dst = outs[t].at[slot(block)]
            return pltpu.make_async_remote_copy(
                src_ref=dst if src is None else src, dst_ref=dst, send_sem=send.at[7 * t + k],
                recv_sem=recv.at[7 * t + k], device_id=to, device_id_type=pl.DeviceIdType.MESH)

        own, sent, landing = [], [], []
        for t in range(n):
            mine = pltpu.make_async_copy(ins[t], outs[t].at[slot(me)], local.at[t])
            mine.start()
            own.append(mine)
            first = [copy(t, 0, me, sibling, src=ins[t])]
            first += [copy(t, 1 + j, me, (*chip, c), src=ins[t]) for j, chip in enumerate(chips)]
            for cp in first:
                cp.start()
            sent += first
        if ns:
            src, dst = refs[n], refs[2 * n + ns]
            s_send, s_recv = refs[2 * (n + ns) + 3:]
            my_slot, peers = _me_and_peers()
            mine = pltpu.make_async_copy(src.at[my_slot], dst.at[my_slot], local.at[n])
            mine.start()
            own.append(mine)
            for k, (dev, peer_slot) in enumerate(peers):
                def piece(dst_slot, k=k, dev=dev, peer_slot=peer_slot):
                    return pltpu.make_async_remote_copy(
                        src_ref=src.at[peer_slot], dst_ref=dst.at[dst_slot], send_sem=s_send.at[k],
                        recv_sem=s_recv.at[k], device_id=dev, device_id_type=pl.DeviceIdType.MESH)

                cp = piece(my_slot)
                cp.start()
                sent.append(cp)
                landing.append(piece(peer_slot))
        for t in range(n):
            for j, chip in enumerate(chips):
                copy(t, 1 + j, (*chip, c), me).wait_recv()
                passed = copy(t, 4 + j, (*chip, c), sibling)
                passed.start()
                sent.append(passed)
        for t in range(n):
            copy(t, 0, sibling, me).wait_recv()
            for j, chip in enumerate(chips):
                copy(t, 4 + j, (*chip, 1 - c), me).wait_recv()
        for cp in landing:
            cp.wait_recv()
        for cp in sent:
            cp.wait_send()
        for cp in own:
            cp.wait()

    scratch = [pltpu.SemaphoreType.DMA((7 * n,)), pltpu.SemaphoreType.DMA((7 * n,)), pltpu.SemaphoreType.DMA((n + ns,))]
    out_shape = [jax.ShapeDtypeStruct((NDEV,) + a.shape, a.dtype) for a in tensors]
    args = list(tensors)
    if ns:
        scratch += [pltpu.SemaphoreType.DMA((NDEV - 1,)), pltpu.SemaphoreType.DMA((NDEV - 1,))]
        out_shape.append(jax.ShapeDtypeStruct(scatter.shape, scatter.dtype))
        args.append(scatter)
    return pl.pallas_call(
        body, in_specs=[any_spec] * (n + ns), out_specs=[any_spec] * (n + ns), out_shape=out_shape,
        scratch_shapes=scratch, name=name,
    )(*args)


_HBM_SPEC = pl.BlockSpec(memory_space=pltpu.HBM)
_SEM_SPEC = pl.BlockSpec(memory_space=pltpu.SEMAPHORE)
_NPEER = NDEV - 1


def _split_copy_params():
    return pltpu.CompilerParams(has_side_effects=pltpu.SideEffectType.DATAFLOW_SIDE_EFFECTING)


def _me_and_peers():
    x, y, c = _coords()
    peers = []
    for rel in range(1, NDEV):
        p = (1 - x if rel & 4 else x, 1 - y if rel & 2 else y, 1 - c if rel & 1 else c)
        peers.append((p, 4 * p[0] + 2 * p[1] + p[2]))
    return 4 * x + 2 * y + c, peers


def _hbm(a):
    return pltpu.with_memory_space_constraint(a, pltpu.HBM)


def gather_start(bufs, name):
    n = len(bufs)

    def body(*refs):
        ins, outs = refs[:n], refs[n:]
        me, peers = _me_and_peers()
        for t in range(n):
            for k, (dev, _) in enumerate(peers):
                pltpu.make_async_remote_copy(
                    src_ref=ins[t].at[me], dst_ref=ins[t].at[me], send_sem=outs[3 * t].at[k],
                    recv_sem=outs[3 * t + 1].at[k], device_id=dev, device_id_type=pl.DeviceIdType.MESH).start()
        outs[3 * n][...] = jnp.zeros_like(outs[3 * n])

    out_shape, out_specs = [], []
    for b in bufs:
        out_shape += [pltpu.SemaphoreType.DMA((_NPEER,)), pltpu.SemaphoreType.DMA((_NPEER,)), pltpu.HBM(b.shape, b.dtype)]
        out_specs += [_SEM_SPEC, _SEM_SPEC, _HBM_SPEC]
    out_shape.append(jax.ShapeDtypeStruct((8, LANES), F32))
    out_specs.append(pl.BlockSpec(memory_space=pltpu.VMEM))
    res = pl.pallas_call(
        body, name=name, out_shape=tuple(out_shape), in_specs=[_HBM_SPEC] * n, out_specs=tuple(out_specs),
        input_output_aliases={t: 3 * t + 2 for t in range(n)}, compiler_params=_split_copy_params(),
    )(*[_hbm(b) for b in bufs])
    return [tuple(res[3 * t:3 * t + 3]) for t in range(n)], res[3 * n]


def gather_wait(started, after, name):
    n = len(started)

    def body(*refs):
        bufs, sems = refs[:n], refs[n:3 * n]
        me, peers = _me_and_peers()
        for t in range(n):
            for k, (dev, slot) in enumerate(peers):
                cp = pltpu.make_async_remote_copy(
                    src_ref=bufs[t].at[me], dst_ref=bufs[t].at[slot], send_sem=sems[2 * t].at[k],
                    recv_sem=sems[2 * t + 1].at[k], device_id=dev, device_id_type=pl.DeviceIdType.MESH)
                cp.wait_recv()
                cp.wait_send()

    args = [s[2] for s in started] + [sem for s in started for sem in s[:2]] + [after]
    res = pl.pallas_call(
        body, name=name, out_shape=tuple(pltpu.HBM(s[2].shape, s[2].dtype) for s in started),
        in_specs=[_HBM_SPEC] * n + [_SEM_SPEC] * (2 * n) + [pl.BlockSpec(memory_space=pl.ANY)],
        out_specs=tuple([_HBM_SPEC] * n), input_output_aliases={t: t for t in range(n)},
        compiler_params=_split_copy_params(),
    )(*args)
    return list(res)


def scatter_start(srcs, name):
    n = len(srcs)
    lands = [lax.empty((_NPEER,) + s.shape[1:], s.dtype) for s in srcs]

    def body(*refs):
        ins, land_refs, outs = refs[:n], refs[n:2 * n], refs[2 * n:]
        _, peers = _me_and_peers()
        for t in range(n):
            for k, (dev, slot) in enumerate(peers):
                pltpu.make_async_remote_copy(
                    src_ref=ins[t].at[slot], dst_ref=land_refs[t].at[k], send_sem=outs[4 * t].at[k],
                    recv_sem=outs[4 * t + 1].at[k], device_id=dev, device_id_type=pl.DeviceIdType.MESH).start()
        outs[4 * n][...] = jnp.zeros_like(outs[4 * n])

    out_shape, out_specs = [], []
    for s, land in zip(srcs, lands):
        out_shape += [pltpu.SemaphoreType.DMA((_NPEER,)), pltpu.SemaphoreType.DMA((_NPEER,)),
                      pltpu.HBM(s.shape, s.dtype), pltpu.HBM(land.shape, land.dtype)]
        out_specs += [_SEM_SPEC, _SEM_SPEC, _HBM_SPEC, _HBM_SPEC]
    out_shape.append(jax.ShapeDtypeStruct((8, LANES), F32))
    out_specs.append(pl.BlockSpec(memory_space=pltpu.VMEM))
    aliases = {t: 4 * t + 2 for t in range(n)}
    aliases.update({n + t: 4 * t + 3 for t in range(n)})
    res = pl.pallas_call(
        body, name=name, out_shape=tuple(out_shape), in_specs=[_HBM_SPEC] * (2 * n), out_specs=tuple(out_specs),
        input_output_aliases=aliases, compiler_params=_split_copy_params(),
    )(*[_hbm(s) for s in srcs], *[_hbm(land) for land in lands])
    return [tuple(res[4 * t:4 * t + 4]) for t in range(n)], res[4 * n]


def scatter_wait(started, after, name):
    n = len(started)

    def body(*refs):
        srcs, land_refs, sems = refs[:n], refs[n:2 * n], refs[2 * n:4 * n]
        _, peers = _me_and_peers()
        for t in range(n):
            for k, (dev, slot) in enumerate(peers):
                cp = pltpu.make_async_remote_copy(
                    src_ref=srcs[t].at[slot], dst_ref=land_refs[t].at[k], send_sem=sems[2 * t].at[k],
                    recv_sem=sems[2 * t + 1].at[k], device_id=dev, device_id_type=pl.DeviceIdType.MESH)
                cp.wait_recv()
                cp.wait_send()

    args = [s[2] for s in started] + [s[3] for s in started] + [sem for s in started for sem in s[:2]] + [after]
    res = pl.pallas_call(
        body, name=name,
        out_shape=tuple([pltpu.HBM(s[2].shape, s[2].dtype) for s in started]
                        + [pltpu.HBM(s[3].shape, s[3].dtype) for s in started]),
        in_specs=[_HBM_SPEC] * (2 * n) + [_SEM_SPEC] * (2 * n) + [pl.BlockSpec(memory_space=pl.ANY)],
        out_specs=tuple([_HBM_SPEC] * (2 * n)), input_output_aliases={t: t for t in range(2 * n)},
        compiler_params=_split_copy_params(),
    )(*args)
    return [(res[t], res[n + t]) for t in range(n)]


_PACK_QUANTUM = 8 * LANES


def _pack(parts, lead=0):
    out = []
    for p in parts:
        head = p.shape[:lead]
        f = p.astype(F32).reshape(head + (-1,))
        pad = (-f.shape[-1]) % _PACK_QUANTUM
        if pad:
            f = jnp.concatenate([f, jnp.zeros(head + (pad,), F32)], axis=-1)
        out.append(f.reshape(head + (-1, LANES)))
    return jnp.concatenate(out, axis=lead)


def _unpack(buf, shapes):
    head = buf.shape[:-2]
    out, r = [], 0
    for s in shapes:
        n = 1
        for v in s:
            n *= v
        nr = -(-n // _PACK_QUANTUM) * 8
        flat = buf[..., r:r + nr, :].reshape(head + (nr * LANES,))[..., :n]
        out.append(flat.reshape(head + tuple(s)))
        r += nr
    return out


BIG = ("sb_w_qkv", "sb_w_o", "sg_w_in", "sg_w_o", "ssm_w_in", "ssm_w_glu", "ffn_w_up", "ffn_w_down")
SMALL_SHARDED = ("norm_g", "ssm_d", "ffn_conv_w")
REPLICATED = ("final_norm_g", "sg_norm_g", "sg_w_s", "sg_b", "ssm_lam_re", "ssm_lam_im", "ssm_log_dt",
              "ssm_b_re", "ssm_b_im", "ssm_c_re", "ssm_c_im", "ffn_conv_b")
WEIGHTS = ("norm_g", "final_norm_g", "sb_w_qkv", "sb_w_o", "sg_w_in", "sg_norm_g", "sg_w_s", "sg_b", "sg_w_o",
           "ssm_w_in", "ssm_lam_re", "ssm_lam_im", "ssm_log_dt", "ssm_b_re", "ssm_b_im", "ssm_c_re", "ssm_c_im",
           "ssm_d", "ssm_w_glu", "ffn_w_up", "ffn_conv_w", "ffn_conv_b", "ffn_w_down")


def _step(x, loss_target, w, m, v):
    t, d = x.shape[1], x.shape[2]
    depth = w["norm_g"].shape[0]
    x0 = x.reshape(t, d)
    tgt = loss_target.reshape(t, d)

    mx, my, mc = _coords()
    me = (4 * mx + 2 * my + mc).astype(jnp.int32).reshape(1)
    shard_pack = _pack([w[k] for k in SMALL_SHARDED])
    gathered_small, = all_gather([shard_pack], name="gather_small_weights")
    mixer_weights = (("sb_w_qkv", "sb_w_o"), ("sg_w_in", "sg_w_o"), ("ssm_w_in", "ssm_w_glu"))
    order = []
    for i in range(depth):
        order += [(k, i // 3) for k in mixer_weights[i % 3]] + [("ffn_w_up", i), ("ffn_w_down", i)]
    as_kept = lambda tree, k: jnp.swapaxes(tree[k], 1, 2) if k == "ffn_w_up" else tree[k]
    pending, token = {}, None
    for group in (order[:4], order[4:]):
        started, token = gather_start([cast_into_slot(as_kept(w, k), l, me) for k, l in group], "gather_weights_start")
        pending.update(zip(group, started))
    wg = {}

    def weights(keys, after):
        missing = [key for key in keys if key not in wg]
        if missing:
            for key, buf in zip(missing, gather_wait([pending[key] for key in missing], after, "gather_weights_wait")):
                wg[key] = buf[:, None]
        return [wg[key] for key in keys]

    ng, sd, cw = _unpack(gathered_small, [w[k].shape for k in SMALL_SHARDED])
    norm_full = jnp.transpose(ng, (1, 2, 0, 3)).reshape(depth, 2, d)
    ssm_d_full = jnp.transpose(sd, (1, 0, 2)).reshape(1, d)
    nc = cw.shape[-1]
    conv_b3 = w["ffn_conv_b"].reshape(depth, NDEV, nc)
    p3 = [jnp.concatenate([cw[:, l], conv_b3[l][:, None, :], jnp.zeros((NDEV, 8 - CONV_K - 1, nc), F32)], axis=1)
          for l in range(depth)]

    g_, p_, h_ = SSM_GROUPS, SSM_STATE, SSM_GROUP
    lam_re, lam_im = w["ssm_lam_re"][0], w["ssm_lam_im"][0]
    log_dt = w["ssm_log_dt"][0].reshape(g_, 1)
    b_re, b_im = w["ssm_b_re"][0].reshape(g_ * p_, h_), w["ssm_b_im"][0].reshape(g_ * p_, h_)
    ar, ai, cre, cim = _single(_disc1, [lam_re, lam_im, log_dt], [(g_, p_)] * 4, "s5_disc1")
    cre_c, cim_c = cre.reshape(g_ * p_, 1), cim.reshape(g_ * p_, 1)
    bbr, bbi = _single(_disc2, [cre_c, cim_c, b_re, b_im], [(g_ * p_, h_)] * 2, "s5_disc2")
    per_group_t = lambda a, r, c: jnp.swapaxes(a.reshape(g_, r, c), 1, 2)
    bd = jnp.stack([_block_diag(per_group_t(bbr, p_, h_)), _block_diag(per_group_t(bbi, p_, h_))])
    cd = jnp.stack([_block_diag(per_group_t(w["ssm_c_re"][0], h_, p_)),
                    -_block_diag(per_group_t(w["ssm_c_im"][0], h_, p_))])
    a2 = jnp.stack([ar.reshape(g_ * p_), ai.reshape(g_ * p_)])

    sg_gain = w["sg_norm_g"]
    sg_ws = w["sg_w_s"][0]
    sg_bfull = jnp.broadcast_to(w["sg_b"][0][:, :, None], sg_ws.shape)

    acts = []
    xc = x0
    xn = rms_fwd(xc, norm_full[0, 0][None], "rms_fwd")
    for i in range(depth):
        mixer, j = i % 3, i // 3
        st = {"x": xc, "xn": xn}
        g0, g1 = norm_full[i, 0][None], norm_full[i, 1][None]
        g_next = norm_full[i + 1, 0][None] if i + 1 < depth else None
        k_in, k_out = [(k, j) for k in mixer_weights[mixer]]
        w_in, = weights([k_in], token if i == 0 else xn)
        if mixer == 0:
            qkv = mm_cs_fwd(xn, w_in, 0, BF16, "qkv_fwd")
            o, ltot = sb_attn_fwd(qkv)
            w_out, = weights([k_out], o)
            x1, xn2 = mm_rs_fwd(o, w_out, 0, xc, F32, "attn_out_fwd", norm=g1)
            st.update(qkv=qkv, o=o, ltot=ltot)
        elif mixer == 1:
            hin = mm_cs_fwd(xn, w_in, 0, BF16, "sg_in_fwd")
            p = sgu_fwd(hin, sg_gain, sg_ws, sg_bfull)
            w_out, = weights([k_out], p)
            x1, xn2 = mm_rs_fwd(p, w_out, 0, xc, F32, "sg_out_fwd", norm=g1)
            st.update(hin=hin, p=p)
        else:
            u = mm_rs_fwd(xn, w_in, 0, None, F32, "ssm_in_fwd")
            u_s = s5_reorder(u, True)
            x2 = s5_scan_fwd(mm_s5("bu", u_s, bd, t, "s5_bu"), a2)
            yc_s = mm_s5("yc", x2, cd, t, "s5_yc")
            yg = s5_post_fwd(s5_reorder(yc_s, False), u, ssm_d_full)
            w_out, = weights([k_out], yg)
            hg = mm_cs_fwd(yg, w_out, 0, BF16, "ssm_glu_fwd")
            x1, xn2 = glu_fwd(hg, xc, g1)
            st.update(u_s=u_s, x2=x2, yc_s=yc_s, yg=yg, hg=hg)
        w_up, w_down = weights([("ffn_w_up", i), ("ffn_w_down", i)], xn2)
        h3 = mm_up_fwd(xn2, w_up, 0, BF16, "ffn_up_fwd")
        gated = ffn_gate_fwd(h3, p3[i])
        if g_next is None:
            xc = mm_down_fwd(gated, w_down, 0, x1, "ffn_down_fwd")
        else:
            xc, xn = mm_down_fwd(gated, w_down, 0, x1, "ffn_down_fwd", norm=g_next)
        st.update(x1=x1, xn2=xn2, h3=h3, gated=gated, g0=g0, g1=g1)
        acts.append(st)

    dx, loss_lanes, d_final_g = loss_head(xc, w["final_norm_g"][None], tgt)
    loss = lax.psum(loss_lanes[0, 0], MESH_AXES)

    scattering = {}
    d_norm = [[None, None] for _ in range(depth)]
    d_p3 = [None] * depth
    rep = {}
    d_ssm_d = None
    token = None

    def scatter(grads_by_key):
        keys = list(grads_by_key)
        started, tok = scatter_start([grads_by_key[key] for key in keys], "scatter_grads_start")
        scattering[tuple(keys)] = started
        return tok

    for i in reversed(range(depth)):
        mixer, j = i % 3, i // 3
        st = acts[i]
        k_in, k_out = [(k, j) for k in mixer_weights[mixer]]
        w_in, w_out, w_up, w_down = weights([k_in, k_out, ("ffn_w_up", i), ("ffn_w_down", i)], None)
        dgated = mm_down_da(dx, w_down, 0, "ffn_down_da", dep=token)
        g_down = mm_down_dw(st["gated"], dx, "ffn_down_dw")
        dy_a, dy_g, dp_a, dp_g = ffn_gate_bwd(st["h3"], dgated, p3[i])
        d_p3[i] = jnp.concatenate([dp_a, dp_g], axis=0)
        dh3 = ffn_conv_t(dy_a, dy_g, p3[i])
        dxn2 = mm_up_da(dh3, w_up, 0, "ffn_up_da")
        g_up = mm_up_dw(st["xn2"], dh3, "ffn_up_dw")
        dx1, d_norm[i][1] = rms_bwd(st["x1"], st["g1"], dxn2, dx, "rms_bwd")
        token = scatter({("ffn_w_down", i): g_down, ("ffn_w_up", i): g_up})
        if mixer == 0:
            do = mm_rs_da(dx1, w_out, 0, BF16, "attn_out_da", dep=token)
            g_out = mm_rs_dw(st["o"], dx1, "attn_out_dw")
            d3 = sb_attn_bwd(st["qkv"], st["ltot"], do)
            g_in = mm_qkv_dw(st["xn"], d3, w_in.shape[3], "qkv_dw")
            token = scatter({k_in: g_in, k_out: g_out})
            dxn = mm_qkv_da(d3, w_in, 0, "qkv_da", dep=token)
        elif mixer == 1:
            dp = mm_rs_da(dx1, w_out, 0, BF16, "sg_out_da", dep=token)
            g_out = mm_rs_dw(st["p"], dx1, "sg_out_dw")
            dhin, d_ws, d_bfull, d_gain = sgu_bwd(st["hin"], dp, sg_gain, sg_ws, sg_bfull)
            rep.update(sg_w_s=d_ws[None], sg_b=d_bfull[None, :, :, 0], sg_norm_g=d_gain)
            dxn = mm_cs_da(dhin, w_in, 0, t, "sg_in_da")
            g_in = mm_cs_dw(st["xn"], dhin, w_in.shape[3], "sg_in_dw")
        else:
            dhg = glu_bwd(st["hg"], dx1, token)
            dyg = mm_cs_da(dhg, w_out, 0, t, "ssm_glu_da")
            g_out = mm_cs_dw(st["yg"], dhg, w_out.shape[3], "ssm_glu_dw")
            dyc_s, du_skip_s, d_ssm_d = s5_post_bwd(st["yc_s"], st["u_s"], ssm_d_full, s5_reorder(dyg, True))
            dx2 = mm_s5("dx", dyc_s, cd, t, "s5_dx")
            dcd = mm_s5("dcd", st["x2"], dyc_s, t, "s5_dcd")
            g2, da2 = s5_scan_bwd(dx2, st["x2"], a2)
            du = s5_reorder(mm_s5("du", g2, bd, t, "s5_du", res=du_skip_s), False)
            dbd = mm_s5("dbd", st["u_s"], g2, t, "s5_dbd")
            from_bd = lambda blk: jnp.swapaxes(_block_diag_t(blk, h_, p_), 1, 2).reshape(g_ * p_, h_)

            def disc2_bwd(c1, c2, b1, b2, t1, t2):
                return jax.vjp(_disc2, c1, c2, b1, b2)[1]((t1, t2))

            d_cre, d_cim, d_b_re, d_b_im = _single(
                disc2_bwd, [cre_c, cim_c, b_re, b_im, from_bd(dbd[0]), from_bd(dbd[1])],
                [(g_ * p_, 1)] * 2 + [(g_ * p_, h_)] * 2, "s5_disc2_bwd")

            def disc1_bwd(l1, l2, ld, t1, t2, t3, t4):
                return jax.vjp(_disc1, l1, l2, ld)[1]((t1, t2, t3, t4))

            d_lam_re, d_lam_im, d_log_dt = _single(
                disc1_bwd, [lam_re, lam_im, log_dt, da2[0].reshape(g_, p_), da2[1].reshape(g_, p_),
                            d_cre.reshape(g_, p_), d_cim.reshape(g_, p_)],
                [(g_, p_), (g_, p_), (g_, 1)], "s5_disc1_bwd")
            from_cd = lambda blk: jnp.swapaxes(_block_diag_t(blk, p_, h_), 1, 2)
            rep.update(ssm_lam_re=d_lam_re[None], ssm_lam_im=d_lam_im[None], ssm_log_dt=d_log_dt.reshape(1, g_),
                       ssm_b_re=d_b_re.reshape(1, g_, p_, h_), ssm_b_im=d_b_im.reshape(1, g_, p_, h_),
                       ssm_c_re=from_cd(dcd[0])[None], ssm_c_im=-from_cd(dcd[1])[None])
            dxn = mm_rs_da(du, w_in, 0, F32, "ssm_in_da")
            g_in = mm_rs_dw(st["xn"], du, "ssm_in_dw")
        dx, d_norm[i][0] = rms_bwd(st["x"], st["g0"], dxn, dx1, "rms_bwd")
        if mixer != 0:
            token = scatter({k_in: g_in, k_out: g_out})

    rep["final_norm_g"] = d_final_g.reshape(d)
    rep["ffn_conv_b"] = jnp.stack([d_p3[l][:, CONV_K, :].reshape(NDEV * nc) for l in range(depth)])

    d_norm_full = jnp.stack([jnp.concatenate(pair, axis=0) for pair in d_norm])
    d_norm_pieces = jnp.transpose(d_norm_full.reshape(depth, 2, NDEV, d // NDEV), (2, 0, 1, 3))
    d_ssm_d_pieces = jnp.transpose(d_ssm_d.reshape(1, NDEV, d // NDEV), (1, 0, 2))
    d_conv_w_pieces = jnp.stack([d_p3[l][:, :CONV_K, :] for l in range(depth)], axis=1)
    small_pieces = _pack([d_norm_pieces, d_ssm_d_pieces, d_conv_w_pieces], lead=1)
    rep_parts, small_received = all_gather([_pack([rep[k] for k in REPLICATED]).astype(BF16)],
                                           name="exchange_small_grads", scatter=small_pieces)
    own, landed = {}, {}
    for keys, started in scattering.items():
        for key, (src, land) in zip(keys, scatter_wait(started, dx, "scatter_grads_wait")):
            own[key], landed[key] = src, land

    grads, deltas, new_m, new_v = {}, {}, {}, {}
    for k in BIG:
        layers = range(w[k].shape[0])
        res = adamw_layers(as_kept(w, k), as_kept(m, k), as_kept(v, k), [landed[(k, l)] for l in layers],
                           [own[(k, l)] for l in layers], me, "adamw")
        grads[k], deltas[k], new_m[k], new_v[k] = [jnp.swapaxes(r, 1, 2) if k == "ffn_w_up" else r for r in res]
    for names, parts in ((SMALL_SHARDED, small_received), (REPLICATED, rep_parts)):
        res = adamw(_pack([w[k] for k in names]), _pack([m[k] for k in names]), _pack([v[k] for k in names]), parts,
                    "adamw_small")
        for tree, buf in zip((grads, deltas, new_m, new_v), res):
            for k, val in zip(names, _unpack(buf, [w[k].shape for k in names])):
                tree[k] = val
    grad_x = dx.reshape(x.shape)
    return (loss, grad_x, *[grads[k] for k in WEIGHTS], *[deltas[k] for k in WEIGHTS],
            *[new_m[k] for k in WEIGHTS], *[new_v[k] for k in WEIGHTS])


def kernel(x, norm_g, final_norm_g, sb_w_qkv, sb_w_o, sg_w_in, sg_norm_g, sg_w_s, sg_b, sg_w_o, ssm_w_in, ssm_lam_re, ssm_lam_im, ssm_log_dt, ssm_b_re, ssm_b_im, ssm_c_re, ssm_c_im, ssm_d, ssm_w_glu, ffn_w_up, ffn_conv_w, ffn_conv_b, ffn_w_down, loss_target, m_norm_g, m_final_norm_g, m_sb_w_qkv, m_sb_w_o, m_sg_w_in, m_sg_norm_g, m_sg_w_s, m_sg_b, m_sg_w_o, m_ssm_w_in, m_ssm_lam_re, m_ssm_lam_im, m_ssm_log_dt, m_ssm_b_re, m_ssm_b_im, m_ssm_c_re, m_ssm_c_im, m_ssm_d, m_ssm_w_glu, m_ffn_w_up, m_ffn_conv_w, m_ffn_conv_b, m_ffn_w_down, v_norm_g, v_final_norm_g, v_sb_w_qkv, v_sb_w_o, v_sg_w_in, v_sg_norm_g, v_sg_w_s, v_sg_b, v_sg_w_o, v_ssm_w_in, v_ssm_lam_re, v_ssm_lam_im, v_ssm_log_dt, v_ssm_b_re, v_ssm_b_im, v_ssm_c_re, v_ssm_c_im, v_ssm_d, v_ssm_w_glu, v_ffn_w_up, v_ffn_conv_w, v_ffn_conv_b, v_ffn_w_down):
    w = dict(zip(WEIGHTS, (norm_g, final_norm_g, sb_w_qkv, sb_w_o, sg_w_in, sg_norm_g, sg_w_s, sg_b, sg_w_o, ssm_w_in,
                           ssm_lam_re, ssm_lam_im, ssm_log_dt, ssm_b_re, ssm_b_im, ssm_c_re, ssm_c_im, ssm_d, ssm_w_glu,
                           ffn_w_up, ffn_conv_w, ffn_conv_b, ffn_w_down)))
    m = dict(zip(WEIGHTS, (m_norm_g, m_final_norm_g, m_sb_w_qkv, m_sb_w_o, m_sg_w_in, m_sg_norm_g, m_sg_w_s, m_sg_b,
                           m_sg_w_o, m_ssm_w_in, m_ssm_lam_re, m_ssm_lam_im, m_ssm_log_dt, m_ssm_b_re, m_ssm_b_im,
                           m_ssm_c_re, m_ssm_c_im, m_ssm_d, m_ssm_w_glu, m_ffn_w_up, m_ffn_conv_w, m_ffn_conv_b,
                           m_ffn_w_down)))
    v = dict(zip(WEIGHTS, (v_norm_g, v_final_norm_g, v_sb_w_qkv, v_sb_w_o, v_sg_w_in, v_sg_norm_g, v_sg_w_s, v_sg_b,
                           v_sg_w_o, v_ssm_w_in, v_ssm_lam_re, v_ssm_lam_im, v_ssm_log_dt, v_ssm_b_re, v_ssm_b_im,
                           v_ssm_c_re, v_ssm_c_im, v_ssm_d, v_ssm_w_glu, v_ffn_w_up, v_ffn_conv_w, v_ffn_conv_b,
                           v_ffn_w_down)))
    return _step(x, loss_target, w, m, v)
```

```python
import functools

import jax
import jax.numpy as jnp
from jax import lax
from jax.experimental import pallas as pl
from jax.experimental.pallas import tpu as pltpu

F32, BF16 = jnp.float32, jnp.bfloat16
MESH_AXES = ("x", "y", "c")
NDEV = 8
EPS = 1e-6
HEAD_DIM = 64
LANES = 128
ATT_BQ, ATT_BK = 2048, 256
CHUNK = 128
SG_GROUPS = 8
SSM_GROUPS, SSM_STATE, SSM_GROUP = 64, 64, 16
SSM_PACK = 8
S5_PASSES = 1
CONV_K = 3
HALO = 16
ROW_BLOCK = 512
SCAN_COLS = 256
ADAM_LR, ADAM_B1, ADAM_B2, ADAM_EPS, ADAM_WD, ADAM_STEP = 0.001, 0.9, 0.999, 1e-08, 0.01, 10
VMEM_LIMIT = 56 * 1024 * 1024

_MM = (((1,), (0,)), ((), ()))
_MM_TB = (((1,), (1,)), ((), ()))
_MM_TA = (((0,), (0,)), ((), ()))


def _params(sem):
    return pltpu.CompilerParams(dimension_semantics=sem, vmem_limit_bytes=VMEM_LIMIT)


def _rows(total, cap, mult=16):
    best = None
    for d in range(mult, min(total, cap) + 1, mult):
        if total % d == 0:
            best = d
    return best if best is not None else total


def _dot(a, b, dims, passes):
    if passes == 1:
        return lax.dot_general(a.astype(BF16), b.astype(BF16), dims, preferred_element_type=F32)
    a = a.astype(F32)
    b = b.astype(F32)
    ah = a.astype(BF16)
    bh = b.astype(BF16)
    al = (a - ah.astype(F32)).astype(BF16)
    bl = (b - bh.astype(F32)).astype(BF16)
    out = lax.dot_general(ah, bh, dims, preferred_element_type=F32)
    out = out + lax.dot_general(al, bh, dims, preferred_element_type=F32)
    return out + lax.dot_general(ah, bl, dims, preferred_element_type=F32)


def _mm(a, b, *, grid, a_blk, a_map, b_blk, b_map, o_blk, o_map, out_shape, out_dtype, name,
        dims=_MM, passes=1, res=None, res_blk=None, res_map=None, b_2d=None, acc_2d=None, dep=None, norm=None):
    nk = grid[2]
    has_res, has_norm = res is not None, norm is not None
    a_maps = list(a_map) if isinstance(a_map, (list, tuple)) else [a_map]
    b_maps = list(b_map) if isinstance(b_map, (list, tuple)) else [b_map]
    na, nb = len(a_maps), len(b_maps)
    n_in = na + nb + has_res + has_norm + (dep is not None)

    def body(*refs):
        o_ref = refs[n_in]
        r_ref = refs[na + nb] if has_res else None
        av = refs[0][...] if na == 1 else jnp.concatenate([r[...] for r in refs[:na]], axis=-1)
        bv = refs[na][...] if nb == 1 else jnp.concatenate([r[...] for r in refs[na:na + nb]], axis=-1)
        if b_2d is not None:
            bv = bv.reshape(b_2d)
        part = _dot(av, bv, dims, passes)

        def finish(total):
            if has_res:
                total = total + r_ref[...].astype(F32)
            o_ref[...] = total.reshape(o_ref.shape).astype(o_ref.dtype)
            if has_norm:
                refs[n_in + 1][...] = _rms(total, refs[na + nb + has_res][...]).astype(BF16)

        if nk == 1:
            finish(part)
        else:
            acc_ref = refs[-1]
            k = pl.program_id(2)

            @pl.when(k == 0)
            def _():
                acc_ref[...] = part

            @pl.when(k > 0)
            def _():
                acc_ref[...] += part

            @pl.when(k == nk - 1)
            def _():
                finish(acc_ref[...])

    in_specs = [pl.BlockSpec(a_blk, f) for f in a_maps] + [pl.BlockSpec(b_blk, f) for f in b_maps]
    args = [a] * na + [b] * nb
    if has_res:
        in_specs.append(pl.BlockSpec(res_blk, res_map))
        args.append(res)
    if has_norm:
        in_specs.append(pl.BlockSpec(norm.shape, lambda *_: (0, 0)))
        args.append(norm)
    if dep is not None:
        in_specs.append(pl.BlockSpec(memory_space=pl.ANY))
        args.append(dep)
    scratch = [pltpu.VMEM(acc_2d, F32)] if nk > 1 else []
    out_specs, out_shapes = pl.BlockSpec(o_blk, o_map), jax.ShapeDtypeStruct(out_shape, out_dtype)
    if has_norm:
        out_specs, out_shapes = [out_specs] * 2, [out_shapes, jax.ShapeDtypeStruct(out_shape, BF16)]
    return pl.pallas_call(
        body, grid=grid, in_specs=in_specs, out_specs=out_specs, out_shape=out_shapes, scratch_shapes=scratch,
        name=name, compiler_params=_params(("parallel", "parallel", "arbitrary")),
    )(*args)


def _cs_act_spec(ns, tm, row_of, col_of):
    if ns % LANES == 0:
        return (tm, ns), lambda *g: (row_of(*g), col_of(*g))
    return (None, tm, ns), lambda *g: (col_of(*g), row_of(*g), 0)


def mm_cs_fwd(a, w4, l, out_dtype, name, tm=2048):
    m, k = a.shape
    tm = min(tm, m)
    ns = w4.shape[3]
    o_blk, o_map = _cs_act_spec(ns, tm, lambda j, i, kk: i, lambda j, i, kk: j)
    out_shape = (m, NDEV * ns) if ns % LANES == 0 else (NDEV, m, ns)
    return _mm(a, w4, grid=(NDEV, m // tm, 1), a_blk=(tm, k), a_map=lambda j, i, kk: (i, 0),
               b_blk=(None, None, k, ns), b_map=lambda j, i, kk: (j, l, 0, 0),
               o_blk=o_blk, o_map=o_map, out_shape=out_shape, out_dtype=out_dtype, name=name)


def mm_cs_da(dc, w4, l, m, name, tm=1024):
    k, ns = w4.shape[2], w4.shape[3]
    tm = min(tm, m)
    a_blk, a_map = _cs_act_spec(ns, tm, lambda i, _, j: i, lambda i, _, j: j)
    return _mm(dc, w4, grid=(m // tm, 1, NDEV), a_blk=a_blk, a_map=a_map,
               b_blk=(None, None, k, ns), b_map=lambda i, _, j: (j, l, 0, 0),
               o_blk=(tm, k), o_map=lambda i, _, j: (i, 0), out_shape=(m, k), out_dtype=F32,
               dims=_MM_TB, acc_2d=(tm, k), name=name)


def mm_cs_dw(a, dc, ns, name, tk=2048):
    m, k = a.shape
    tk = min(tk, m)
    b_blk, b_map = _cs_act_spec(ns, tk, lambda j, _, kk: kk, lambda j, _, kk: j)
    return _mm(a, dc, grid=(NDEV, 1, m // tk), a_blk=(tk, k), a_map=lambda j, _, kk: (kk, 0),
               b_blk=b_blk, b_map=b_map, o_blk=(None, k, ns), o_map=lambda j, _, kk: (j, 0, 0),
               out_shape=(NDEV, k, ns), out_dtype=BF16, dims=_MM_TA, acc_2d=(k, ns), name=name)


def mm_up_fwd(a, wt4, l, out_dtype, name, tm=2048):
    m, k = a.shape
    tm = min(tm, m)
    ns = wt4.shape[2]
    return _mm(a, wt4, grid=(NDEV, m // tm, 1), a_blk=(tm, k), a_map=lambda j, i, kk: (i, 0),
               b_blk=(None, None, ns, k), b_map=lambda j, i, kk: (j, l, 0, 0), dims=_MM_TB,
               o_blk=(None, tm, ns), o_map=lambda j, i, kk: (j, i, 0), out_shape=(NDEV, m, ns), out_dtype=out_dtype,
               name=name)


def mm_up_da(dc3, wt4, l, name, tm=1024):
    _, m, ns = dc3.shape
    tm = min(tm, m)
    k = wt4.shape[3]
    return _mm(dc3, wt4, grid=(m // tm, 1, NDEV), a_blk=(None, tm, ns), a_map=lambda i, _, j: (j, i, 0),
               b_blk=(None, None, ns, k), b_map=lambda i, _, j: (j, l, 0, 0),
               o_blk=(tm, k), o_map=lambda i, _, j: (i, 0), out_shape=(m, k), out_dtype=F32, acc_2d=(tm, k), name=name)


def mm_up_dw(a, dc3, name, tk=2048):
    m, k = a.shape
    tk = min(tk, m)
    ns = dc3.shape[2]
    return _mm(dc3, a, grid=(NDEV, 1, m // tk), a_blk=(None, tk, ns), a_map=lambda j, _, kk: (j, kk, 0),
               b_blk=(tk, k), b_map=lambda j, _, kk: (kk, 0), dims=_MM_TA,
               o_blk=(None, ns, k), o_map=lambda j, _, kk: (j, 0, 0), out_shape=(NDEV, ns, k), out_dtype=BF16,
               acc_2d=(ns, k), name=name)


def mm_rs_fwd(a, w4, l, res, out_dtype, name, tm=1024, norm=None):
    m, k = a.shape
    tm = min(tm, m)
    ks, n = w4.shape[2], w4.shape[3]
    return _mm(a, w4, grid=(m // tm, 1, 1), a_blk=(tm, k), a_map=lambda i, _, kk: (i, 0),
               b_blk=(NDEV, None, ks, n), b_map=lambda i, _, kk: (0, l, 0, 0), b_2d=(k, n),
               o_blk=(tm, n), o_map=lambda i, _, kk: (i, 0), out_shape=(m, n), out_dtype=out_dtype,
               res=res, res_blk=(tm, n), res_map=lambda i, _, kk: (i, 0), name=name, norm=norm)


def mm_rs_da(dc, w4, l, out_dtype, name, tm=1024, dep=None):
    m, n = dc.shape
    tm = min(tm, m)
    ks = w4.shape[2]
    k = NDEV * ks
    return _mm(dc, w4, grid=(m // tm, 1, 1), a_blk=(tm, n), a_map=lambda i, _, kk: (i, 0),
               b_blk=(NDEV, None, ks, n), b_map=lambda i, _, kk: (0, l, 0, 0), b_2d=(k, n),
               o_blk=(tm, k), o_map=lambda i, _, kk: (i, 0), out_shape=(m, k), out_dtype=out_dtype,
               dims=_MM_TB, name=name, dep=dep)


def mm_rs_dw(a, dc, name, tk=1024):
    m, k = a.shape
    tk = min(tk, m)
    n = dc.shape[1]
    ks = k // NDEV
    return _mm(a, dc, grid=(1, 1, m // tk), a_blk=(tk, k), a_map=lambda _, __, kk: (kk, 0),
               b_blk=(tk, n), b_map=lambda _, __, kk: (kk, 0),
               o_blk=(NDEV, ks, n), o_map=lambda _, __, kk: (0, 0, 0), out_shape=(NDEV, ks, n),
               out_dtype=BF16, dims=_MM_TA, acc_2d=(k, n), name=name)


def mm_down_fwd(a3, w4, l, res, name, tm=1024, norm=None):
    nj, m, kc = a3.shape
    tm = min(tm, m)
    ks, n = w4.shape[2], w4.shape[3]
    return _mm(a3, w4, grid=(m // tm, 1, nj), a_blk=(None, tm, kc), a_map=lambda i, _, j: (j, i, 0),
               b_blk=(2, None, ks, n), b_map=lambda i, _, j: (j, l, 0, 0), b_2d=(kc, n),
               o_blk=(tm, n), o_map=lambda i, _, j: (i, 0), out_shape=(m, n), out_dtype=F32,
               res=res, res_blk=(tm, n), res_map=lambda i, _, j: (i, 0), acc_2d=(tm, n), name=name, norm=norm)


def mm_down_da(dc, w4, l, name, tm=2048, dep=None):
    m, n = dc.shape
    tm = min(tm, m)
    ks = w4.shape[2]
    kc = 2 * ks
    nj = NDEV // 2
    return _mm(dc, w4, grid=(nj, m // tm, 1), a_blk=(tm, n), a_map=lambda j, i, _: (i, 0),
               b_blk=(2, None, ks, n), b_map=lambda j, i, _: (j, l, 0, 0), b_2d=(kc, n),
               o_blk=(None, tm, kc), o_map=lambda j, i, _: (j, i, 0), out_shape=(nj, m, kc),
               out_dtype=BF16, dims=_MM_TB, name=name, dep=dep)


def mm_down_dw(a3, dc, name, tk=2048):
    nj, m, kc = a3.shape
    tk = min(tk, m)
    n = dc.shape[1]
    return _mm(a3, dc, grid=(nj, 1, m // tk), a_blk=(None, tk, kc), a_map=lambda j, _, kk: (j, kk, 0),
               b_blk=(tk, n), b_map=lambda j, _, kk: (kk, 0),
               o_blk=(2, kc // 2, n), o_map=lambda j, _, kk: (j, 0, 0), out_shape=(NDEV, kc // 2, n),
               out_dtype=BF16, dims=_MM_TA, acc_2d=(kc, n), name=name)


def _qkv_group_maps(d, ns, row_of, piece_of):
    per_arr, per_piece = d // LANES, ns // LANES

    def group_map(q):
        def f(*g):
            grp = piece_of(*g) * per_piece + q
            return grp // per_arr, row_of(*g), grp % per_arr
        return f

    return [group_map(q) for q in range(per_piece)]


def mm_qkv_da(d3, w4, l, name, tm=1024, dep=None):
    _, m, d = d3.shape
    tm = min(tm, m)
    k, ns = w4.shape[2], w4.shape[3]
    return _mm(d3, w4, grid=(m // tm, 1, NDEV),
               a_blk=(None, tm, LANES), a_map=_qkv_group_maps(d, ns, lambda i, _, j: i, lambda i, _, j: j),
               b_blk=(None, None, k, ns), b_map=lambda i, _, j: (j, l, 0, 0),
               o_blk=(tm, k), o_map=lambda i, _, j: (i, 0), out_shape=(m, k), out_dtype=F32,
               dims=_MM_TB, acc_2d=(tm, k), name=name, dep=dep)


def mm_qkv_dw(a, d3, ns, name, tk=2048):
    m, k = a.shape
    tk = min(tk, m)
    d = d3.shape[2]
    return _mm(a, d3, grid=(NDEV, 1, m // tk), a_blk=(tk, k), a_map=lambda j, _, kk: (kk, 0),
               b_blk=(None, tk, LANES), b_map=_qkv_group_maps(d, ns, lambda j, _, kk: kk, lambda j, _, kk: j),
               o_blk=(None, k, ns), o_map=lambda j, _, kk: (j, 0, 0),
               out_shape=(NDEV, k, ns), out_dtype=BF16, dims=_MM_TA, acc_2d=(k, ns), name=name)


def _rowwise(fn, ins, outs, *, tr, name, acc_outs=()):
    rows = next(a.shape[0] if kind == "row" else a.shape[1] for a, kind in ins if kind in ("row", "row3"))
    n_in, n_out = len(ins), len(outs)
    n_read = sum(kind != "dep" for _, kind in ins)

    def body(*refs):
        vals = fn(*[r[...] for r in refs[:n_read]])
        if not isinstance(vals, (tuple, list)):
            vals = (vals,)
        for ref, val in zip(refs[n_in:n_in + n_out], vals[:n_out]):
            ref[...] = val.astype(ref.dtype)
        i = pl.program_id(0)
        for ref, val in zip(refs[n_in + n_out:], vals[n_out:]):
            val = val.astype(ref.dtype)

            @pl.when(i == 0)
            def _(ref=ref, val=val):
                ref[...] = val

            @pl.when(i > 0)
            def _(ref=ref, val=val):
                ref[...] += val

    in_specs = []
    for a, kind in ins:
        if kind == "row":
            in_specs.append(pl.BlockSpec((tr, a.shape[1]), lambda i: (i, 0)))
        elif kind == "row3":
            in_specs.append(pl.BlockSpec((a.shape[0], tr, a.shape[2]), lambda i: (0, i, 0)))
        elif kind == "dep":
            in_specs.append(pl.BlockSpec(memory_space=pl.ANY))
        else:
            in_specs.append(pl.BlockSpec(a.shape, lambda i, nd=a.ndim: (0,) * nd))
    out_specs = [pl.BlockSpec((tr, c), lambda i: (i, 0)) for c, _ in outs]
    out_specs += [pl.BlockSpec(s, lambda i, nd=len(s): (0,) * nd) for s, _ in acc_outs]
    out_shape = [jax.ShapeDtypeStruct((rows, c), dt) for c, dt in outs]
    out_shape += [jax.ShapeDtypeStruct(s, dt) for s, dt in acc_outs]
    res = pl.pallas_call(
        body, grid=(rows // tr,), in_specs=in_specs, out_specs=out_specs, out_shape=out_shape, name=name,
        compiler_params=_params(("arbitrary",) if acc_outs else ("parallel",)),
    )(*[a for a, _ in ins])
    return res


def _rms(x, g):
    return x * lax.rsqrt(jnp.mean(x * x, axis=-1, keepdims=True) + EPS) * g


def cast_into_slot(w, l, me):
    _, r, c = w.shape
    tr = _rows(r, 512)

    def body(me_ref, w_ref, o_ref):
        o_ref[...] = w_ref[...].astype(o_ref.dtype)

    return pl.pallas_call(
        body,
        grid_spec=pltpu.PrefetchScalarGridSpec(
            num_scalar_prefetch=1, grid=(r // tr,),
            in_specs=[pl.BlockSpec((None, tr, c), lambda i, me_ref: (l, i, 0))],
            out_specs=pl.BlockSpec((None, tr, c), lambda i, me_ref: (me_ref[0], i, 0))),
        out_shape=jax.ShapeDtypeStruct((NDEV, r, c), BF16), name="cast_into_slot",
        compiler_params=_params(("parallel",)),
    )(me, w)


def rms_fwd(x, g, name):
    out, = _rowwise(_rms, [(x, "row"), (g, "full")], [(x.shape[1], BF16)], tr=ROW_BLOCK, name=name)
    return out


def rms_bwd(x, g, dy, dres, name):
    def fn(xv, gv, dyv, drv):
        _, vjp = jax.vjp(_rms, xv, gv)
        dx, dg = vjp(dyv.astype(F32))
        return drv + dx, dg

    d = x.shape[1]
    return _rowwise(fn, [(x, "row"), (g, "full"), (dy, "row"), (dres, "row")], [(d, F32)], tr=ROW_BLOCK, name=name,
                    acc_outs=[((1, d), F32)])


def loss_head(x, g, tgt):
    def f(xv, gv, tv):
        err = jnp.square(_rms(xv, gv) - tv)
        return 0.5 * jnp.sum(jnp.mean(err, axis=-1))

    def fn(xv, gv, tv):
        val, (dx, dg) = jax.value_and_grad(f, argnums=(0, 1))(xv, gv, tv)
        return dx, jnp.full((1, LANES), val, F32), dg

    d = x.shape[1]
    return _rowwise(fn, [(x, "row"), (g, "full"), (tgt, "row")], [(d, F32)], tr=ROW_BLOCK, name="loss_head",
                    acc_outs=[((1, LANES), F32), ((1, d), F32)])


def _glu(hg, x):
    half = hg.shape[1] // 2
    return x + hg[:, :half] * jax.nn.sigmoid(hg[:, half:])


def glu_fwd(hg, x, norm):
    def fn(h, xv, g):
        x1 = _glu(h.astype(F32), xv)
        return x1, _rms(x1, g)

    d = x.shape[1]
    return _rowwise(fn, [(hg, "row"), (x, "row"), (norm, "full")], [(d, F32), (d, BF16)], tr=ROW_BLOCK, name="glu_fwd")


def glu_bwd(hg, dx1, dep):
    def fn(h, d):
        _, vjp = jax.vjp(lambda hv: _glu(hv, jnp.zeros_like(d)), h.astype(F32))
        return vjp(d)[0]

    out, = _rowwise(fn, [(hg, "row"), (dx1, "row"), (dep, "dep")], [(hg.shape[1], BF16)], tr=ROW_BLOCK, name="glu_bwd")
    return out


def _s5_post(yc, u, d):
    return jax.nn.gelu(yc + d * u)


def s5_post_fwd(yc, u, d):
    out, = _rowwise(_s5_post, [(yc, "row"), (u, "row"), (d, "full")], [(yc.shape[1], BF16)], tr=ROW_BLOCK,
                    name="s5_post_fwd")
    return out


def s5_post_bwd(yc, u, d, dyg):
    def fn(ycv, uv, dv, g):
        _, vjp = jax.vjp(_s5_post, ycv, uv, dv)
        return vjp(g.astype(F32))

    dm = yc.shape[1]
    return _rowwise(fn, [(yc, "row"), (u, "row"), (d, "full"), (dyg, "row")], [(dm, F32), (dm, F32)], tr=ROW_BLOCK,
                    name="s5_post_bwd", acc_outs=[((1, dm), F32)])


def _adam_update(wv, mv, vv, g):
    m2 = ADAM_B1 * mv + (1.0 - ADAM_B1) * g
    v2 = ADAM_B2 * vv + (1.0 - ADAM_B2) * jnp.square(g)
    m_hat = m2 / (1.0 - ADAM_B1 ** ADAM_STEP)
    v_hat = v2 / (1.0 - ADAM_B2 ** ADAM_STEP)
    delta = -ADAM_LR * (m_hat / (jnp.sqrt(v_hat) + ADAM_EPS) + ADAM_WD * wv)
    return g, delta, m2, v2


def adamw(w, m, v, g_parts, name):
    def fn(wv, mv, vv, gp):
        g = gp[0].astype(F32)
        for p in range(1, gp.shape[0]):
            g = g + gp[p].astype(F32)
        return _adam_update(wv, mv, vv, g)

    c = w.shape[1]
    return _rowwise(fn, [(w, "row"), (m, "row"), (v, "row"), (g_parts, "row3")], [(c, F32)] * 4,
                    tr=_rows(w.shape[0], 256), name=name)


def adamw_layers(w, m, v, lands, owns, me, name):
    nl, r, c = w.shape
    tr = _rows(r, 256)

    def body(me_ref, w_ref, m_ref, v_ref, *rest):
        land_refs, own_refs, out_refs = rest[:nl], rest[nl:2 * nl], rest[2 * nl:]
        for l in range(nl):
            @pl.when(pl.program_id(0) == l)
            def _(l=l):
                g = own_refs[l][...].astype(F32)
                for p in range(NDEV - 1):
                    g = g + land_refs[l][p].astype(F32)
                for ref, val in zip(out_refs, _adam_update(w_ref[...], m_ref[...], v_ref[...], g)):
                    ref[...] = val

    def rows_of(l):
        return lambda li, i, me_ref: jnp.where(li == l, i, 0)

    wspec = pl.BlockSpec((None, tr, c), lambda li, i, me_ref: (li, i, 0))
    in_specs = [wspec] * 3
    in_specs += [pl.BlockSpec((NDEV - 1, tr, c), lambda li, i, me_ref, f=rows_of(l): (0, f(li, i, me_ref), 0))
                 for l in range(nl)]
    in_specs += [pl.BlockSpec((None, tr, c), lambda li, i, me_ref, f=rows_of(l): (me_ref[0], f(li, i, me_ref), 0))
                 for l in range(nl)]
    return pl.pallas_call(
        body,
        grid_spec=pltpu.PrefetchScalarGridSpec(
            num_scalar_prefetch=1, grid=(nl, r // tr), in_specs=in_specs, out_specs=[wspec] * 4),
        out_shape=[jax.ShapeDtypeStruct(w.shape, F32)] * 4, name=name, compiler_params=_params(("parallel", "parallel")),
    )(me, w, m, v, *lands, *owns)


def _conv_rows(cur, halo, p, first):
    r = cur.shape[0]
    ext = jnp.concatenate([jnp.where(first, 0.0, halo), cur], axis=0)
    s1 = pltpu.roll(ext, 1, 0)[HALO:]
    s2 = pltpu.roll(ext, 2, 0)[HALO:]
    return p[0:1] * s2 + p[1:2] * s1 + p[2:3] * cur + p[3:4], s1, s2


def ffn_gate_fwd(h3, p3, tr=ROW_BLOCK):
    _, t, c = h3.shape
    half = NDEV // 2

    def body(a_ref, ah_ref, g_ref, gh_ref, pa_ref, pg_ref, o_ref):
        first = pl.program_id(1) == 0
        ya, _, _ = _conv_rows(a_ref[...].astype(F32), ah_ref[...].astype(F32), pa_ref[...], first)
        yg, _, _ = _conv_rows(g_ref[...].astype(F32), gh_ref[...].astype(F32), pg_ref[...], first)
        o_ref[...] = (jax.nn.silu(yg) * ya).astype(o_ref.dtype)

    main = lambda off: pl.BlockSpec((None, tr, c), lambda j, i: (j + off, i, 0))
    halo = lambda off: pl.BlockSpec((None, HALO, c), lambda j, i: (j + off, jnp.maximum(i * (tr // HALO) - 1, 0), 0))
    par = lambda off: pl.BlockSpec((None, 8, c), lambda j, i: (j + off, 0, 0))
    return pl.pallas_call(
        body, grid=(half, t // tr),
        in_specs=[main(0), halo(0), main(half), halo(half), par(0), par(half)],
        out_specs=pl.BlockSpec((None, tr, c), lambda j, i: (j, i, 0)),
        out_shape=jax.ShapeDtypeStruct((half, t, c), BF16), name="ffn_gate_fwd",
        compiler_params=_params(("parallel", "parallel")),
    )(h3, h3, h3, h3, p3, p3)


def ffn_gate_bwd(h3, dgated3, p3, tr=ROW_BLOCK):
    _, t, c = h3.shape
    half = NDEV // 2

    def body(a_ref, ah_ref, g_ref, gh_ref, dg_ref, pa_ref, pg_ref, dya_ref, dyg_ref, dpa_ref, dpg_ref):
        i = pl.program_id(1)
        first = i == 0
        a = a_ref[...].astype(F32)
        g = g_ref[...].astype(F32)
        ya, a1, a2 = _conv_rows(a, ah_ref[...].astype(F32), pa_ref[...], first)
        yg, g1, g2 = _conv_rows(g, gh_ref[...].astype(F32), pg_ref[...], first)
        d = dg_ref[...].astype(F32)
        sig = jax.nn.sigmoid(yg)
        d_ya = (d * (yg * sig)).astype(dya_ref.dtype)
        d_yg = (d * ya * (sig * (1.0 + yg * (1.0 - sig)))).astype(dyg_ref.dtype)
        dya_ref[...] = d_ya
        dyg_ref[...] = d_yg
        for dy, cur, s1, s2, dp_ref in ((d_ya.astype(F32), a, a1, a2, dpa_ref), (d_yg.astype(F32), g, g1, g2, dpg_ref)):
            rows = [jnp.sum(dy * s2, axis=0, keepdims=True), jnp.sum(dy * s1, axis=0, keepdims=True),
                    jnp.sum(dy * cur, axis=0, keepdims=True), jnp.sum(dy, axis=0, keepdims=True)]
            dp = jnp.concatenate(rows + [jnp.zeros((4, c), F32)], axis=0)

            @pl.when(first)
            def _(dp_ref=dp_ref, dp=dp):
                dp_ref[...] = dp

            @pl.when(i > 0)
            def _(dp_ref=dp_ref, dp=dp):
                dp_ref[...] += dp

    main = lambda off: pl.BlockSpec((None, tr, c), lambda j, i: (j + off, i, 0))
    halo = lambda off: pl.BlockSpec((None, HALO, c), lambda j, i: (j + off, jnp.maximum(i * (tr // HALO) - 1, 0), 0))
    par = lambda off: pl.BlockSpec((None, 8, c), lambda j, i: (j + off, 0, 0))
    return pl.pallas_call(
        body, grid=(half, t // tr),
        in_specs=[main(0), halo(0), main(half), halo(half), main(0), par(0), par(half)],
        out_specs=[main(0), main(0), par(0), par(0)],
        out_shape=[jax.ShapeDtypeStruct((half, t, c), BF16)] * 2 + [jax.ShapeDtypeStruct((half, 8, c), F32)] * 2,
        name="ffn_gate_bwd", compiler_params=_params(("parallel", "arbitrary")),
    )(h3, h3, h3, h3, dgated3, p3, p3)


def ffn_conv_t(dy_a, dy_g, p3, tr=2 * ROW_BLOCK):
    half, t, c = dy_a.shape
    tr = min(tr, t)
    nblk = t // tr

    def body(a_ref, ah_ref, g_ref, gh_ref, p_ref, o_ref):
        is_a = pl.program_id(0) < half
        last = pl.program_id(1) == nblk - 1
        cur = jnp.where(is_a, a_ref[...], g_ref[...]).astype(F32)
        nxt = jnp.where(is_a, ah_ref[...], gh_ref[...]).astype(F32)
        ext = jnp.concatenate([cur, jnp.where(last, 0.0, nxt)], axis=0)
        n = tr + HALO
        s1 = pltpu.roll(ext, n - 1, 0)[:tr]
        s2 = pltpu.roll(ext, n - 2, 0)[:tr]
        p = p_ref[...]
        o_ref[...] = (p[2:3] * cur + p[1:2] * s1 + p[0:1] * s2).astype(o_ref.dtype)

    main = pl.BlockSpec((None, tr, c), lambda j, i: (j % half, i, 0))
    halo = pl.BlockSpec((None, HALO, c), lambda j, i: (j % half, jnp.minimum((i + 1) * (tr // HALO), t // HALO - 1), 0))
    return pl.pallas_call(
        body, grid=(NDEV, nblk),
        in_specs=[main, halo, main, halo, pl.BlockSpec((None, 8, c), lambda j, i: (j, 0, 0))],
        out_specs=pl.BlockSpec((None, tr, c), lambda j, i: (j, i, 0)),
        out_shape=jax.ShapeDtypeStruct((NDEV, t, c), BF16), name="ffn_conv_t",
        compiler_params=_params(("parallel", "parallel")),
    )(dy_a, dy_a, dy_g, dy_g, p3)


def _att_consts(bq, bk):
    lane = lax.broadcasted_iota(jnp.int32, (1, LANES), 1)
    heads = (lane < HEAD_DIM, lane >= HEAD_DIM)
    rr = lax.broadcasted_iota(jnp.int32, (bq, bk), 0)
    cc = lax.broadcasted_iota(jnp.int32, (bq, bk), 1)
    kr = lax.broadcasted_iota(jnp.int32, (bk, bk), 0)
    kc = lax.broadcasted_iota(jnp.int32, (bk, bk), 1)
    return heads, rr, cc, kr, kc


def _split_dot(x, tri, parts):
    out = None
    for _ in range(parts):
        piece = x.astype(BF16)
        x = x - piece.astype(F32)
        term = jnp.dot(piece, tri, preferred_element_type=F32)
        out = term if out is None else out + term
    return out


def _att_logits(qh, k):
    z = lax.dot_general(qh, k, _MM_TB, preferred_element_type=F32)
    lsp = jnp.minimum(z, 0.0) - jnp.log(1.0 + jnp.exp(-jnp.abs(z)))
    return lsp, lsp - z


def _per_head(heads, a, b):
    return jnp.where(heads[0], a, b)


def sb_attn_fwd(qkv):
    t, d3 = qkv.shape
    d = d3 // 3
    npair = d // LANES
    bq, bk = min(ATT_BQ, t), min(ATT_BK, t)
    kpq = bq // bk

    def body(q_ref, k_ref, v_ref, o_ref, lt_ref, acc_ref):
        heads, rr, cc, kr, kc = _att_consts(bq, bk)
        suffix = (kr > kc).astype(BF16)

        def trip(qh, k0, r0, runs):
            k = k_ref[pl.ds(k0, bk), :]
            v = v_ref[pl.ds(k0, bk), :]
            diag, r0 = r0 is not None, r0 or 0
            valid = cc[:bq - r0] < rr[:bq - r0]
            new_runs = []
            for h in range(2):
                lsp, lraw = _att_logits(qh[h][r0:], k)
                lm = jnp.where(valid, lraw, 0.0) if diag else lraw
                w = jnp.exp(lsp + _split_dot(lm, suffix, 2) + runs[h][r0:])
                if diag:
                    w = jnp.where(valid, w, 0.0)
                acc_ref[h, r0:, :] += jnp.dot(w.astype(BF16), v, preferred_element_type=F32)
                below = runs[h][r0:] + jnp.sum(lm, axis=1, keepdims=True)
                new_runs.append(jnp.concatenate([runs[h][:r0], below], axis=0) if r0 else below)
            return tuple(new_runs)

        def q_loop(qb, _):
            q0 = pl.multiple_of(qb * bq, bq)
            q = q_ref[pl.ds(q0, bq), :] * 0.125
            qh = [jnp.where(hm, q, 0.0).astype(BF16) for hm in heads]
            acc_ref[...] = jnp.zeros_like(acc_ref)
            runs = (jnp.zeros((bq, 1), F32),) * 2
            for dblk in reversed(range(kpq)):
                runs = trip(qh, pl.multiple_of(q0 + dblk * bk, bk), dblk * bk, runs)
            nleft = qb * kpq
            runs = lax.fori_loop(
                0, nleft, lambda i, r: trip(qh, pl.multiple_of((nleft - 1 - i) * bk, bk), None, r), runs)
            o_ref[pl.ds(q0, bq), :] = _per_head(heads, acc_ref[0], acc_ref[1])
            lt_ref[pl.ds(q0, bq), :] = _per_head(heads, runs[0], runs[1])
            return 0

        lax.fori_loop(0, t // bq, q_loop, 0)

    col = lambda off: pl.BlockSpec((t, LANES), lambda p: (0, p + off))
    return pl.pallas_call(
        body, grid=(npair,), in_specs=[col(0), col(npair), col(2 * npair)], out_specs=[col(0), col(0)],
        out_shape=[jax.ShapeDtypeStruct((t, d), F32)] * 2, scratch_shapes=[pltpu.VMEM((2, bq, LANES), F32)],
        name="sb_attn_fwd", compiler_params=_params(("parallel",)),
    )(qkv, qkv, qkv)


def sb_attn_bwd(qkv, ltot, do):
    t, d3 = qkv.shape
    d = d3 // 3
    npair = d // LANES
    bq, bk = min(ATT_BQ, t), min(ATT_BK, t)
    kpq = bq // bk

    def body(q_ref, k_ref, v_ref, lt_ref, do_ref, d_ref, dk_acc, dv_acc, dq_acc):
        heads, rr, cc, kr, kc = _att_consts(bq, bk)
        prefix_incl = (kr <= kc).astype(BF16)
        prefix_excl = (kr < kc).astype(BF16)
        dk_acc[...] = jnp.zeros_like(dk_acc)
        dv_acc[...] = jnp.zeros_like(dv_acc)

        def trip(qh, doh, lt, k0, r0, carry):
            lruns, gruns = carry
            k = k_ref[pl.ds(k0, bk), :]
            v = v_ref[pl.ds(k0, bk), :]
            diag, r0 = r0 is not None, r0 or 0
            valid = cc[:bq - r0] < rr[:bq - r0]
            new_lruns, new_gruns = [], []
            dk_blk = jnp.zeros((bk, LANES), F32)
            dv_blk = jnp.zeros((bk, LANES), F32)
            for h in range(2):
                q_rows, do_rows = qh[h][r0:], doh[h][r0:]
                lsp, lraw = _att_logits(q_rows, k)
                lm = jnp.where(valid, lraw, 0.0) if diag else lraw
                right = lt[h][r0:] - (lruns[h][r0:] + _split_dot(lm, prefix_incl, 2))
                w = jnp.exp(lsp + right)
                if diag:
                    w = jnp.where(valid, w, 0.0)
                g = lax.dot_general(do_rows, v, _MM_TB, preferred_element_type=F32) * w
                left = gruns[h][r0:] + _split_dot(g, prefix_excl, 1)
                dz = g * jnp.exp(lraw) - jnp.exp(lsp) * left
                if diag:
                    dz = jnp.where(valid, dz, 0.0)
                dz = dz.astype(BF16)
                kh = jnp.where(heads[h], k, 0.0).astype(BF16)
                dq_acc[h, r0:, :] += jnp.dot(dz, kh, preferred_element_type=F32)
                dk_blk = dk_blk + lax.dot_general(dz, q_rows, _MM_TA, preferred_element_type=F32)
                dv_blk = dv_blk + lax.dot_general(w.astype(BF16), do_rows, _MM_TA, preferred_element_type=F32)
                l_below = lruns[h][r0:] + jnp.sum(lm, axis=1, keepdims=True)
                g_below = gruns[h][r0:] + jnp.sum(g, axis=1, keepdims=True)
                new_lruns.append(jnp.concatenate([lruns[h][:r0], l_below], axis=0) if r0 else l_below)
                new_gruns.append(jnp.concatenate([gruns[h][:r0], g_below], axis=0) if r0 else g_below)
            dk_acc[pl.ds(k0, bk), :] += dk_blk
            dv_acc[pl.ds(k0, bk), :] += dv_blk
            return tuple(new_lruns), tuple(new_gruns)

        def q_loop(qb, _):
            q0 = pl.multiple_of(qb * bq, bq)
            q = q_ref[pl.ds(q0, bq), :] * 0.125
            dout = do_ref[pl.ds(q0, bq), :]
            lt2 = lt_ref[pl.ds(q0, bq), :]
            qh = [jnp.where(hm, q, 0.0).astype(BF16) for hm in heads]
            doh = [jnp.where(hm, dout, 0.0).astype(BF16) for hm in heads]
            lt = [jnp.max(jnp.where(hm, lt2, -jnp.inf), axis=1, keepdims=True) for hm in heads]
            dq_acc[...] = jnp.zeros_like(dq_acc)
            col = (jnp.zeros((bq, 1), F32),) * 2
            carry = lax.fori_loop(
                0, qb * kpq, lambda kb, c: trip(qh, doh, lt, pl.multiple_of(kb * bk, bk), None, c), (col, col))
            for dblk in range(kpq):
                carry = trip(qh, doh, lt, pl.multiple_of(q0 + dblk * bk, bk), dblk * bk, carry)
            d_ref[0, pl.ds(q0, bq), :] = ((dq_acc[0] + dq_acc[1]) * 0.125).astype(d_ref.dtype)
            return 0

        lax.fori_loop(0, t // bq, q_loop, 0)
        d_ref[1] = dk_acc[...].astype(d_ref.dtype)
        d_ref[2] = dv_acc[...].astype(d_ref.dtype)

    col = lambda off: pl.BlockSpec((t, LANES), lambda p: (0, p + off))
    return pl.pallas_call(
        body, grid=(npair,), in_specs=[col(0), col(npair), col(2 * npair), col(0), col(0)],
        out_specs=pl.BlockSpec((3, t, LANES), lambda p: (0, 0, p)),
        out_shape=jax.ShapeDtypeStruct((3, t, d), BF16),
        scratch_shapes=[pltpu.VMEM((t, LANES), F32), pltpu.VMEM((t, LANES), F32), pltpu.VMEM((2, bq, LANES), F32)],
        name="sb_attn_bwd", compiler_params=_params(("parallel",)),
    )(qkv, qkv, qkv, ltot, do)


def _sgu_parts(hin, g, ws_ref, bf_ref):
    width = hin.shape[1] // 2
    h = jax.nn.gelu(hin)
    u, v = h[:, :width], h[:, width:]
    r = lax.rsqrt(jnp.mean(v * v, axis=-1, keepdims=True) + EPS)
    vn = v * r * g
    rr = lax.broadcasted_iota(jnp.int32, (CHUNK, CHUNK), 0)
    cc = lax.broadcasted_iota(jnp.int32, (CHUNK, CHUNK), 1)
    causal = cc <= rr
    wcs = [jnp.where(causal, ws_ref[gi], 0.0).astype(BF16) for gi in range(SG_GROUPS)]
    sv = jnp.concatenate(
        [jnp.dot(wcs[gi], vn[:, gi * CHUNK:(gi + 1) * CHUNK].astype(BF16), preferred_element_type=F32) + bf_ref[gi]
         for gi in range(SG_GROUPS)], axis=1)
    return u, v, r, vn, wcs, sv, causal


def sgu_fwd(hin, g, ws, bfull):
    t, w2 = hin.shape
    width = w2 // 2

    def body(h_ref, g_ref, ws_ref, bf_ref, o_ref):
        u, _, _, _, _, sv, _ = _sgu_parts(h_ref[...].astype(F32), g_ref[...], ws_ref, bf_ref)
        o_ref[...] = (u * sv).astype(o_ref.dtype)

    full = lambda a: pl.BlockSpec(a.shape, lambda i, nd=a.ndim: (0,) * nd)
    return pl.pallas_call(
        body, grid=(t // CHUNK,), in_specs=[pl.BlockSpec((CHUNK, w2), lambda i: (i, 0)), full(g), full(ws), full(bfull)],
        out_specs=pl.BlockSpec((CHUNK, width), lambda i: (i, 0)), out_shape=jax.ShapeDtypeStruct((t, width), BF16),
        name="sgu_fwd", compiler_params=_params(("parallel",)),
    )(hin, g, ws, bfull)


def sgu_bwd(hin, dp, g, ws, bfull):
    t, w2 = hin.shape
    width = w2 // 2

    def body(h_ref, dp_ref, g_ref, ws_ref, bf_ref, dh_ref, dws_ref, dbf_ref, dg_ref):
        i = pl.program_id(0)
        hin_v = h_ref[...].astype(F32)
        gv = g_ref[...]
        u, v, r, vn, wcs, sv, causal = _sgu_parts(hin_v, gv, ws_ref, bf_ref)
        dpv = dp_ref[...].astype(F32)
        du = dpv * sv
        dsv = dpv * u
        dvn_parts, dws_parts, dbf_parts = [], [], []
        for gi in range(SG_GROUPS):
            dsv_g = dsv[:, gi * CHUNK:(gi + 1) * CHUNK]
            dsv_b = dsv_g.astype(BF16)
            dvn_parts.append(lax.dot_general(wcs[gi], dsv_b, _MM_TA, preferred_element_type=F32))
            vn_b = vn[:, gi * CHUNK:(gi + 1) * CHUNK].astype(BF16)
            dws_parts.append(jnp.where(causal, lax.dot_general(dsv_b, vn_b, _MM_TB, preferred_element_type=F32), 0.0))
            dbf_parts.append(jnp.broadcast_to(jnp.sum(dsv_g, axis=1, keepdims=True), (CHUNK, CHUNK)))
        dvn = jnp.concatenate(dvn_parts, axis=1)
        dgain = jnp.sum(dvn * v * r, axis=0, keepdims=True)
        gvv = dvn * gv
        dv = r * gvv - v * (r * r * r) * jnp.mean(v * gvv, axis=-1, keepdims=True)
        _, vjp = jax.vjp(jax.nn.gelu, hin_v)
        dh_ref[...] = vjp(jnp.concatenate([du, dv], axis=1))[0].astype(dh_ref.dtype)

        @pl.when(i == 0)
        def _():
            for gi in range(SG_GROUPS):
                dws_ref[gi] = dws_parts[gi]
                dbf_ref[gi] = dbf_parts[gi]
            dg_ref[...] = dgain

        @pl.when(i > 0)
        def _():
            for gi in range(SG_GROUPS):
                dws_ref[gi] += dws_parts[gi]
                dbf_ref[gi] += dbf_parts[gi]
            dg_ref[...] += dgain

    full = lambda a: pl.BlockSpec(a.shape, lambda i, nd=a.ndim: (0,) * nd)
    sq = (SG_GROUPS, CHUNK, CHUNK)
    return pl.pallas_call(
        body, grid=(t // CHUNK,),
        in_specs=[pl.BlockSpec((CHUNK, w2), lambda i: (i, 0)), pl.BlockSpec((CHUNK, width), lambda i: (i, 0)),
                  full(g), full(ws), full(bfull)],
        out_specs=[pl.BlockSpec((CHUNK, w2), lambda i: (i, 0)), pl.BlockSpec(sq, lambda i: (0, 0, 0)),
                   pl.BlockSpec(sq, lambda i: (0, 0, 0)), pl.BlockSpec((1, width), lambda i: (0, 0))],
        out_shape=[jax.ShapeDtypeStruct((t, w2), BF16), jax.ShapeDtypeStruct(sq, F32), jax.ShapeDtypeStruct(sq, F32),
                   jax.ShapeDtypeStruct((1, width), F32)],
        name="sgu_bwd", compiler_params=_params(("arbitrary",)),
    )(hin, dp, g, ws, bfull)


def _disc1(lam_re, lam_im, log_dt):
    lr = jnp.minimum(lam_re, -1e-4)
    li = lam_im
    dt = jnp.exp(log_dt)
    mag = jnp.exp(dt * lr)
    ar = mag * jnp.cos(dt * li)
    ai = mag * jnp.sin(dt * li)
    den = lr * lr + li * li
    return ar, ai, ((ar - 1.0) * lr + ai * li) / den, (ai * lr - (ar - 1.0) * li) / den


def _disc2(cre, cim, b_re, b_im):
    return cre * b_re - cim * b_im, cre * b_im + cim * b_re


def _single(fn, ins, out_shapes, name):
    n = len(ins)

    def body(*refs):
        vals = fn(*[r[...] for r in refs[:n]])
        for ref, val in zip(refs[n:], vals):
            ref[...] = val

    return pl.pallas_call(body, out_shape=[jax.ShapeDtypeStruct(s, F32) for s in out_shapes], name=name)(*ins)


SCAN_SEGMENTS = 8
REORDER_STEPS = 64


def s5_reorder(x, to_steps):
    t, d = x.shape
    ns = SCAN_SEGMENTS
    seg = t // ns
    ts = min(REORDER_STEPS, seg)
    by_segment = ((ns, seg, d), pl.BlockSpec((ns, ts, d), lambda i: (0, i, 0)))
    by_step = ((seg, ns, d), pl.BlockSpec((ts, ns, d), lambda i: (i, 0, 0)))
    (in_shape, in_spec), (out_shape, out_spec) = (by_segment, by_step) if to_steps else (by_step, by_segment)

    def body(x_ref, o_ref):
        o_ref[...] = jnp.swapaxes(x_ref[...], 0, 1)

    out = pl.pallas_call(
        body, grid=(seg // ts,), in_specs=[in_spec], out_specs=out_spec,
        out_shape=jax.ShapeDtypeStruct(out_shape, x.dtype), name="s5_reorder", compiler_params=_params(("parallel",)),
    )(x.reshape(in_shape))
    return out.reshape(t, d)


def _cpow(ar, ai, n):
    rr, ri = None, None
    while n:
        if n & 1:
            rr, ri = (ar, ai) if rr is None else (rr * ar - ri * ai, rr * ai + ri * ar)
        ar, ai = ar * ar - ai * ai, 2.0 * ar * ai
        n >>= 1
    return rr, ri


def _edge_states(er, ei, pr, pi, reverse):
    ns = SCAN_SEGMENTS
    zero = jnp.zeros_like(er[0:1])
    rows_r, rows_i = [None] * ns, [None] * ns
    order = range(ns - 1, -1, -1) if reverse else range(ns)
    prev = None
    for s in order:
        if prev is None:
            rows_r[s], rows_i[s] = zero, zero
        else:
            cr, ci = rows_r[prev], rows_i[prev]
            rows_r[s] = er[prev:prev + 1] + pr * cr - pi * ci
            rows_i[s] = ei[prev:prev + 1] + pr * ci + pi * cr
        prev = s
    return jnp.concatenate(rows_r, axis=0), jnp.concatenate(rows_i, axis=0)


def s5_scan_fwd(bu2, a2):
    _, t, n = bu2.shape
    cb, ns = SCAN_COLS, SCAN_SEGMENTS
    seg = t // ns

    def body(bu_ref, a_ref, x_ref):
        ar, ai = a_ref[0:1, :], a_ref[1:2, :]

        def local(i, carry):
            xr, xi = carry
            xr, xi = ar * xr - ai * xi + bu_ref[0, i], ar * xi + ai * xr + bu_ref[1, i]
            x_ref[0, i] = xr
            x_ref[1, i] = xi
            return xr, xi

        zero = jnp.zeros((ns, cb), F32)
        er, ei = lax.fori_loop(0, seg, local, (zero, zero))
        cr, ci = _edge_states(er, ei, *_cpow(ar, ai, seg), reverse=False)

        def fix(i, carry):
            wr, wi = carry
            wr, wi = wr * ar - wi * ai, wr * ai + wi * ar
            x_ref[0, i] += wr * cr - wi * ci
            x_ref[1, i] += wr * ci + wi * cr
            return wr, wi

        lax.fori_loop(0, seg, fix, (jnp.ones((1, cb), F32), jnp.zeros((1, cb), F32)))

    blk = pl.BlockSpec((2, seg, ns, cb), lambda j: (0, 0, 0, j))
    out = pl.pallas_call(
        body, grid=(n // cb,), in_specs=[blk, pl.BlockSpec((2, cb), lambda j: (0, j))], out_specs=blk,
        out_shape=jax.ShapeDtypeStruct((2, seg, ns, n), F32), name="s5_scan_fwd", compiler_params=_params(("parallel",)),
    )(bu2.reshape(2, seg, ns, n), a2)
    return out.reshape(2, t, n)


def s5_scan_bwd(dx2, x2, a2):
    _, t, n = dx2.shape
    cb, ns = SCAN_COLS, SCAN_SEGMENTS
    seg = t // ns

    def body(dx_ref, x_ref, a_ref, g_ref, da_ref):
        ar, ai = a_ref[0:1, :], a_ref[1:2, :]

        def local(s, carry):
            gr, gi = carry
            i = seg - 1 - s
            gr, gi = dx_ref[0, i] + ar * gr + ai * gi, dx_ref[1, i] - ai * gr + ar * gi
            g_ref[0, i] = gr
            g_ref[1, i] = gi
            return gr, gi

        zero = jnp.zeros((ns, cb), F32)
        er, ei = lax.fori_loop(0, seg, local, (zero, zero))
        cr, ci = _edge_states(er, ei, *_cpow(ar, -ai, seg), reverse=True)
        row = lax.broadcasted_iota(jnp.int32, (ns, cb), 0)
        before_r = jnp.where(row == 0, 0.0, pltpu.roll(x_ref[0, seg - 1], 1, 0))
        before_i = jnp.where(row == 0, 0.0, pltpu.roll(x_ref[1, seg - 1], 1, 0))

        def fix(s, carry):
            wr, wi, dar, dai = carry
            i = seg - 1 - s
            wr, wi = wr * ar + wi * ai, wi * ar - wr * ai
            gr = g_ref[0, i] + wr * cr - wi * ci
            gi = g_ref[1, i] + wr * ci + wi * cr
            g_ref[0, i] = gr
            g_ref[1, i] = gi
            ip = jnp.maximum(i - 1, 0)
            xpr = jnp.where(i == 0, before_r, x_ref[0, ip])
            xpi = jnp.where(i == 0, before_i, x_ref[1, ip])
            return wr, wi, dar + gr * xpr + gi * xpi, dai + gi * xpr - gr * xpi

        one, z1 = jnp.ones((1, cb), F32), jnp.zeros((1, cb), F32)
        _, _, dar, dai = lax.fori_loop(0, seg, fix, (one, z1, zero, zero))
        da_ref[0:1, :] = jnp.sum(dar, axis=0, keepdims=True)
        da_ref[1:2, :] = jnp.sum(dai, axis=0, keepdims=True)

    blk = pl.BlockSpec((2, seg, ns, cb), lambda j: (0, 0, 0, j))
    vec = pl.BlockSpec((2, cb), lambda j: (0, j))
    g4, da = pl.pallas_call(
        body, grid=(n // cb,), in_specs=[blk, blk, vec], out_specs=[blk, vec],
        out_shape=[jax.ShapeDtypeStruct((2, seg, ns, n), F32), jax.ShapeDtypeStruct((2, n), F32)],
        name="s5_scan_bwd", compiler_params=_params(("parallel",)),
    )(dx2.reshape(2, seg, ns, n), x2.reshape(2, seg, ns, n), a2)
    return g4.reshape(2, t, n), da


_SP_U = SSM_PACK * SSM_GROUP
_SP_X = SSM_PACK * SSM_STATE
_NKB = SSM_GROUPS // SSM_PACK


def mm_s5(kind, a, b, m, name, res=None, tm=2048):
    tm = min(tm, m)
    kw = dict(passes=S5_PASSES, name=name)
    xblk = lambda row, sel, col: ((None, tm, _SP_X), lambda *g: (sel(*g), row(*g), col(*g)))
    if kind == "bu":
        o_blk, o_map = xblk(lambda g, i, k: i, lambda g, i, k: g // _NKB, lambda g, i, k: g % _NKB)
        return _mm(a, b, grid=(2 * _NKB, m // tm, 1), a_blk=(tm, _SP_U), a_map=lambda g, i, k: (i, g % _NKB),
                   b_blk=(None, None, _SP_U, _SP_X), b_map=lambda g, i, k: (g // _NKB, g % _NKB, 0, 0),
                   o_blk=o_blk, o_map=o_map, out_shape=(2, m, _NKB * _SP_X), out_dtype=F32, **kw)
    if kind == "yc":
        a_blk, a_map = xblk(lambda j, i, k: i, lambda j, i, k: k, lambda j, i, k: j)
        return _mm(a, b, grid=(_NKB, m // tm, 2), a_blk=a_blk, a_map=a_map,
                   b_blk=(None, None, _SP_X, _SP_U), b_map=lambda j, i, k: (k, j, 0, 0),
                   o_blk=(tm, _SP_U), o_map=lambda j, i, k: (i, j), out_shape=(m, _NKB * _SP_U), out_dtype=F32,
                   acc_2d=(tm, _SP_U), **kw)
    if kind == "dx":
        o_blk, o_map = xblk(lambda g, i, k: i, lambda g, i, k: g // _NKB, lambda g, i, k: g % _NKB)
        return _mm(a, b, grid=(2 * _NKB, m // tm, 1), a_blk=(tm, _SP_U), a_map=lambda g, i, k: (i, g % _NKB),
                   b_blk=(None, None, _SP_X, _SP_U), b_map=lambda g, i, k: (g // _NKB, g % _NKB, 0, 0),
                   o_blk=o_blk, o_map=o_map, out_shape=(2, m, _NKB * _SP_X), out_dtype=F32, dims=_MM_TB, **kw)
    if kind == "dcd":
        a_blk, a_map = xblk(lambda g, _, k: k, lambda g, _, k: g // _NKB, lambda g, _, k: g % _NKB)
        return _mm(a, b, grid=(2 * _NKB, 1, m // tm), a_blk=a_blk, a_map=a_map,
                   b_blk=(tm, _SP_U), b_map=lambda g, _, k: (k, g % _NKB),
                   o_blk=(None, None, _SP_X, _SP_U), o_map=lambda g, _, k: (g // _NKB, g % _NKB, 0, 0),
                   out_shape=(2, _NKB, _SP_X, _SP_U), out_dtype=F32, dims=_MM_TA, acc_2d=(_SP_X, _SP_U), **kw)
    if kind == "du":
        a_blk, a_map = xblk(lambda j, i, k: i, lambda j, i, k: k, lambda j, i, k: j)
        return _mm(a, b, grid=(_NKB, m // tm, 2), a_blk=a_blk, a_map=a_map,
                   b_blk=(None, None, _SP_U, _SP_X), b_map=lambda j, i, k: (k, j, 0, 0),
                   o_blk=(tm, _SP_U), o_map=lambda j, i, k: (i, j), out_shape=(m, _NKB * _SP_U), out_dtype=F32,
                   dims=_MM_TB, acc_2d=(tm, _SP_U), res=res, res_blk=(tm, _SP_U), res_map=lambda j, i, k: (i, j), **kw)
    assert kind == "dbd"
    b_blk, b_map = xblk(lambda g, _, k: k, lambda g, _, k: g // _NKB, lambda g, _, k: g % _NKB)
    return _mm(a, b, grid=(2 * _NKB, 1, m // tm), a_blk=(tm, _SP_U), a_map=lambda g, _, k: (k, g % _NKB),
               b_blk=b_blk, b_map=b_map,
               o_blk=(None, None, _SP_U, _SP_X), o_map=lambda g, _, k: (g // _NKB, g % _NKB, 0, 0),
               out_shape=(2, _NKB, _SP_U, _SP_X), out_dtype=F32, dims=_MM_TA, acc_2d=(_SP_U, _SP_X), **kw)


def _block_diag(w):
    g, a, b = w.shape
    eye = jnp.eye(SSM_PACK, dtype=w.dtype)
    wp = w.reshape(g // SSM_PACK, SSM_PACK, a, b)
    return jnp.einsum("kgab,gh->kgahb", wp, eye).reshape(g // SSM_PACK, SSM_PACK * a, SSM_PACK * b)


def _block_diag_t(d, a, b):
    k = d.shape[0]
    eye = jnp.eye(SSM_PACK, dtype=d.dtype)
    dp = d.reshape(k, SSM_PACK, a, SSM_PACK, b)
    return jnp.einsum("kgahb,gh->kgab", dp, eye).reshape(k * SSM_PACK, a, b)


def _coords():
    return lax.axis_index("x"), lax.axis_index("y"), lax.axis_index("c")


def all_gather(tensors, name):
    n = len(tensors)
    any_spec = pl.BlockSpec(memory_space=pl.ANY)

    def body(*refs):
        ins, outs = refs[:n], refs[n:2 * n]
        send, recv, local = refs[2 * n:]
        x, y, c = _coords()
        me, sibling = (x, y, c), (x, y, 1 - c)
        chips = [(1 - x, y), (x, 1 - y), (1 - x, 1 - y)]

        def slot(p):
            return 4 * p[0] + 2 * p[1] + p[2]

        def copy(t, k, block, to, src=None):
            dst = outs[t].at[slot(block)]
            return pltpu.make_async_remote_copy(
                src_ref=dst if src is None else src, dst_ref=dst, send_sem=send.at[7 * t + k],
                recv_sem=recv.at[7 * t + k], device_id=to, device_id_type=pl.DeviceIdType.MESH)

        own, sent = [], []
        for t in range(n):
            mine = pltpu.make_async_copy(ins[t], outs[t].at[slot(me)], local.at[t])
            mine.start()
            own.append(mine)
            first = [copy(t, 0, me, sibling, src=ins[t])]
            first += [copy(t, 1 + j, me, (*chip, c), src=ins[t]) for j, chip in enumerate(chips)]
            for cp in first:
                cp.start()
            sent += first
        for t in range(n):
            for j, chip in enumerate(chips):
                copy(t, 1 + j, (*chip, c), me).wait_recv()
                passed = copy(t, 4 + j, (*chip, c), sibling)
                passed.start()
                sent.append(passed)
        for t in range(n):
            copy(t, 0, sibling, me).wait_recv()
            for j, chip in enumerate(chips):
                copy(t, 4 + j, (*chip, 1 - c), me).wait_recv()
        for cp in sent:
            cp.wait_send()
        for cp in own:
            cp.wait()

    return pl.pallas_call(
        body, in_specs=[any_spec] * n, out_specs=[any_spec] * n,
        out_shape=[jax.ShapeDtypeStruct((NDEV,) + a.shape, a.dtype) for a in tensors],
        scratch_shapes=[pltpu.SemaphoreType.DMA((7 * n,)), pltpu.SemaphoreType.DMA((7 * n,)),
                        pltpu.SemaphoreType.DMA((n,))],
        name=name,
    )(*tensors)


_HBM_SPEC = pl.BlockSpec(memory_space=pltpu.HBM)
_SEM_SPEC = pl.BlockSpec(memory_space=pltpu.SEMAPHORE)
_NPEER = NDEV - 1


def _split_copy_params():
    return pltpu.CompilerParams(has_side_effects=pltpu.SideEffectType.DATAFLOW_SIDE_EFFECTING)


def _me_and_peers():
    x, y, c = _coords()
    peers = []
    for rel in range(1, NDEV):
        p = (1 - x if rel & 4 else x, 1 - y if rel & 2 else y, 1 - c if rel & 1 else c)
        peers.append((p, 4 * p[0] + 2 * p[1] + p[2]))
    return 4 * x + 2 * y + c, peers


def _hbm(a):
    return pltpu.with_memory_space_constraint(a, pltpu.HBM)


def gather_start(bufs, name):
    n = len(bufs)

    def body(*refs):
        ins, outs = refs[:n], refs[n:]
        me, peers = _me_and_peers()
        for t in range(n):
            for k, (dev, _) in enumerate(peers):
                pltpu.make_async_remote_copy(
                    src_ref=ins[t].at[me], dst_ref=ins[t].at[me], send_sem=outs[3 * t].at[k],
                    recv_sem=outs[3 * t + 1].at[k], device_id=dev, device_id_type=pl.DeviceIdType.MESH).start()
        outs[3 * n][...] = jnp.zeros_like(outs[3 * n])

    out_shape, out_specs = [], []
    for b in bufs:
        out_shape += [pltpu.SemaphoreType.DMA((_NPEER,)), pltpu.SemaphoreType.DMA((_NPEER,)), pltpu.HBM(b.shape, b.dtype)]
        out_specs += [_SEM_SPEC, _SEM_SPEC, _HBM_SPEC]
    out_shape.append(jax.ShapeDtypeStruct((8, LANES), F32))
    out_specs.append(pl.BlockSpec(memory_space=pltpu.VMEM))
    res = pl.pallas_call(
        body, name=name, out_shape=tuple(out_shape), in_specs=[_HBM_SPEC] * n, out_specs=tuple(out_specs),
        input_output_aliases={t: 3 * t + 2 for t in range(n)}, compiler_params=_split_copy_params(),
    )(*[_hbm(b) for b in bufs])
    return [tuple(res[3 * t:3 * t + 3]) for t in range(n)], res[3 * n]


def gather_wait(started, after, name):
    n = len(started)

    def body(*refs):
        bufs, sems = refs[:n], refs[n:3 * n]
        me, peers = _me_and_peers()
        for t in range(n):
            for k, (dev, slot) in enumerate(peers):
                cp = pltpu.make_async_remote_copy(
                    src_ref=bufs[t].at[me], dst_ref=bufs[t].at[slot], send_sem=sems[2 * t].at[k],
                    recv_sem=sems[2 * t + 1].at[k], device_id=dev, device_id_type=pl.DeviceIdType.MESH)
                cp.wait_recv()
                cp.wait_send()

    args = [s[2] for s in started] + [sem for s in started for sem in s[:2]] + [after]
    res = pl.pallas_call(
        body, name=name, out_shape=tuple(pltpu.HBM(s[2].shape, s[2].dtype) for s in started),
        in_specs=[_HBM_SPEC] * n + [_SEM_SPEC] * (2 * n) + [pl.BlockSpec(memory_space=pl.ANY)],
        out_specs=tuple([_HBM_SPEC] * n), input_output_aliases={t: t for t in range(n)},
        compiler_params=_split_copy_params(),
    )(*args)
    return list(res)


def scatter_start(srcs, name):
    n = len(srcs)
    lands = [lax.empty((_NPEER,) + s.shape[1:], s.dtype) for s in srcs]

    def body(*refs):
        ins, land_refs, outs = refs[:n], refs[n:2 * n], refs[2 * n:]
        _, peers = _me_and_peers()
        for t in range(n):
            for k, (dev, slot) in enumerate(peers):
                pltpu.make_async_remote_copy(
                    src_ref=ins[t].at[slot], dst_ref=land_refs[t].at[k], send_sem=outs[4 * t].at[k],
                    recv_sem=outs[4 * t + 1].at[k], device_id=dev, device_id_type=pl.DeviceIdType.MESH).start()
        outs[4 * n][...] = jnp.zeros_like(outs[4 * n])

    out_shape, out_specs = [], []
    for s, land in zip(srcs, lands):
        out_shape += [pltpu.SemaphoreType.DMA((_NPEER,)), pltpu.SemaphoreType.DMA((_NPEER,)),
                      pltpu.HBM(s.shape, s.dtype), pltpu.HBM(land.shape, land.dtype)]
        out_specs += [_SEM_SPEC, _SEM_SPEC, _HBM_SPEC, _HBM_SPEC]
    out_shape.append(jax.ShapeDtypeStruct((8, LANES), F32))
    out_specs.append(pl.BlockSpec(memory_space=pltpu.VMEM))
    aliases = {t: 4 * t + 2 for t in range(n)}
    aliases.update({n + t: 4 * t + 3 for t in range(n)})
    res = pl.pallas_call(
        body, name=name, out_shape=tuple(out_shape), in_specs=[_HBM_SPEC] * (2 * n), out_specs=tuple(out_specs),
        input_output_aliases=aliases, compiler_params=_split_copy_params(),
    )(*[_hbm(s) for s in srcs], *[_hbm(land) for land in lands])
    return [tuple(res[4 * t:4 * t + 4]) for t in range(n)], res[4 * n]


def scatter_wait(started, after, name):
    n = len(started)

    def body(*refs):
        srcs, land_refs, sems = refs[:n], refs[n:2 * n], refs[2 * n:4 * n]
        _, peers = _me_and_peers()
        for t in range(n):
            for k, (dev, slot) in enumerate(peers):
                cp = pltpu.make_async_remote_copy(
                    src_ref=srcs[t].at[slot], dst_ref=land_refs[t].at[k], send_sem=sems[2 * t].at[k],
                    recv_sem=sems[2 * t + 1].at[k], device_id=dev, device_id_type=pl.DeviceIdType.MESH)
                cp.wait_recv()
                cp.wait_send()

    args = [s[2] for s in started] + [s[3] for s in started] + [sem for s in started for sem in s[:2]] + [after]
    res = pl.pallas_call(
        body, name=name,
        out_shape=tuple([pltpu.HBM(s[2].shape, s[2].dtype) for s in started]
                        + [pltpu.HBM(s[3].shape, s[3].dtype) for s in started]),
        in_specs=[_HBM_SPEC] * (2 * n) + [_SEM_SPEC] * (2 * n) + [pl.BlockSpec(memory_space=pl.ANY)],
        out_specs=tuple([_HBM_SPEC] * (2 * n)), input_output_aliases={t: t for t in range(2 * n)},
        compiler_params=_split_copy_params(),
    )(*args)
    return [(res[t], res[n + t]) for t in range(n)]


_PACK_QUANTUM = 8 * LANES


def _pack(parts, lead=0):
    out = []
    for p in parts:
        head = p.shape[:lead]
        f = p.astype(F32).reshape(head + (-1,))
        pad = (-f.shape[-1]) % _PACK_QUANTUM
        if pad:
            f = jnp.concatenate([f, jnp.zeros(head + (pad,), F32)], axis=-1)
        out.append(f.reshape(head + (-1, LANES)))
    return jnp.concatenate(out, axis=lead)


def _unpack(buf, shapes):
    head = buf.shape[:-2]
    out, r = [], 0
    for s in shapes:
        n = 1
        for v in s:
            n *= v
        nr = -(-n // _PACK_QUANTUM) * 8
        flat = buf[..., r:r + nr, :].reshape(head + (nr * LANES,))[..., :n]
        out.append(flat.reshape(head + tuple(s)))
        r += nr
    return out


BIG = ("sb_w_qkv", "sb_w_o", "sg_w_in", "sg_w_o", "ssm_w_in", "ssm_w_glu", "ffn_w_up", "ffn_w_down")
SMALL_SHARDED = ("norm_g", "ssm_d", "ffn_conv_w")
REPLICATED = ("final_norm_g", "sg_norm_g", "sg_w_s", "sg_b", "ssm_lam_re", "ssm_lam_im", "ssm_log_dt",
              "ssm_b_re", "ssm_b_im", "ssm_c_re", "ssm_c_im", "ffn_conv_b")
WEIGHTS = ("norm_g", "final_norm_g", "sb_w_qkv", "sb_w_o", "sg_w_in", "sg_norm_g", "sg_w_s", "sg_b", "sg_w_o",
           "ssm_w_in", "ssm_lam_re", "ssm_lam_im", "ssm_log_dt", "ssm_b_re", "ssm_b_im", "ssm_c_re", "ssm_c_im",
           "ssm_d", "ssm_w_glu", "ffn_w_up", "ffn_conv_w", "ffn_conv_b", "ffn_w_down")


def _step(x, loss_target, w, m, v):
    t, d = x.shape[1], x.shape[2]
    depth = w["norm_g"].shape[0]
    x0 = x.reshape(t, d)
    tgt = loss_target.reshape(t, d)

    mx, my, mc = _coords()
    me = (4 * mx + 2 * my + mc).astype(jnp.int32).reshape(1)
    shard_pack = _pack([w[k] for k in SMALL_SHARDED])
    gathered_small, = all_gather([shard_pack], name="gather_small_weights")
    mixer_weights = (("sb_w_qkv", "sb_w_o"), ("sg_w_in", "sg_w_o"), ("ssm_w_in", "ssm_w_glu"))
    order = []
    for i in range(depth):
        order += [(k, i // 3) for k in mixer_weights[i % 3]] + [("ffn_w_up", i), ("ffn_w_down", i)]
    as_kept = lambda tree, k: jnp.swapaxes(tree[k], 1, 2) if k == "ffn_w_up" else tree[k]
    pending, token = {}, None
    for group in (order[:4], order[4:]):
        started, token = gather_start([cast_into_slot(as_kept(w, k), l, me) for k, l in group], "gather_weights_start")
        pending.update(zip(group, started))
    wg = {}

    def weights(keys, after):
        missing = [key for key in keys if key not in wg]
        if missing:
            for key, buf in zip(missing, gather_wait([pending[key] for key in missing], after, "gather_weights_wait")):
                wg[key] = buf[:, None]
        return [wg[key] for key in keys]

    ng, sd, cw = _unpack(gathered_small, [w[k].shape for k in SMALL_SHARDED])
    norm_full = jnp.transpose(ng, (1, 2, 0, 3)).reshape(depth, 2, d)
    ssm_d_full = jnp.transpose(sd, (1, 0, 2)).reshape(1, d)
    nc = cw.shape[-1]
    conv_b3 = w["ffn_conv_b"].reshape(depth, NDEV, nc)
    p3 = [jnp.concatenate([cw[:, l], conv_b3[l][:, None, :], jnp.zeros((NDEV, 8 - CONV_K - 1, nc), F32)], axis=1)
          for l in range(depth)]

    g_, p_, h_ = SSM_GROUPS, SSM_STATE, SSM_GROUP
    lam_re, lam_im = w["ssm_lam_re"][0], w["ssm_lam_im"][0]
    log_dt = w["ssm_log_dt"][0].reshape(g_, 1)
    b_re, b_im = w["ssm_b_re"][0].reshape(g_ * p_, h_), w["ssm_b_im"][0].reshape(g_ * p_, h_)
    ar, ai, cre, cim = _single(_disc1, [lam_re, lam_im, log_dt], [(g_, p_)] * 4, "s5_disc1")
    cre_c, cim_c = cre.reshape(g_ * p_, 1), cim.reshape(g_ * p_, 1)
    bbr, bbi = _single(_disc2, [cre_c, cim_c, b_re, b_im], [(g_ * p_, h_)] * 2, "s5_disc2")
    per_group_t = lambda a, r, c: jnp.swapaxes(a.reshape(g_, r, c), 1, 2)
    bd = jnp.stack([_block_diag(per_group_t(bbr, p_, h_)), _block_diag(per_group_t(bbi, p_, h_))])
    cd = jnp.stack([_block_diag(per_group_t(w["ssm_c_re"][0], h_, p_)),
                    -_block_diag(per_group_t(w["ssm_c_im"][0], h_, p_))])
    a2 = jnp.stack([ar.reshape(g_ * p_), ai.reshape(g_ * p_)])

    sg_gain = w["sg_norm_g"]
    sg_ws = w["sg_w_s"][0]
    sg_bfull = jnp.broadcast_to(w["sg_b"][0][:, :, None], sg_ws.shape)

    acts = []
    xc = x0
    xn = rms_fwd(xc, norm_full[0, 0][None], "rms_fwd")
    for i in range(depth):
        mixer, j = i % 3, i // 3
        st = {"x": xc, "xn": xn}
        g0, g1 = norm_full[i, 0][None], norm_full[i, 1][None]
        g_next = norm_full[i + 1, 0][None] if i + 1 < depth else None
        k_in, k_out = [(k, j) for k in mixer_weights[mixer]]
        w_in, = weights([k_in], token if i == 0 else xn)
        if mixer == 0:
            qkv = mm_cs_fwd(xn, w_in, 0, BF16, "qkv_fwd")
            o, ltot = sb_attn_fwd(qkv)
            w_out, = weights([k_out], o)
            x1, xn2 = mm_rs_fwd(o, w_out, 0, xc, F32, "attn_out_fwd", norm=g1)
            st.update(qkv=qkv, o=o, ltot=ltot)
        elif mixer == 1:
            hin = mm_cs_fwd(xn, w_in, 0, BF16, "sg_in_fwd")
            p = sgu_fwd(hin, sg_gain, sg_ws, sg_bfull)
            w_out, = weights([k_out], p)
            x1, xn2 = mm_rs_fwd(p, w_out, 0, xc, F32, "sg_out_fwd", norm=g1)
            st.update(hin=hin, p=p)
        else:
            u = mm_rs_fwd(xn, w_in, 0, None, F32, "ssm_in_fwd")
            u_s = s5_reorder(u, True)
            x2 = s5_scan_fwd(mm_s5("bu", u_s, bd, t, "s5_bu"), a2)
            yc_s = mm_s5("yc", x2, cd, t, "s5_yc")
            yg = s5_post_fwd(s5_reorder(yc_s, False), u, ssm_d_full)
            w_out, = weights([k_out], yg)
            hg = mm_cs_fwd(yg, w_out, 0, BF16, "ssm_glu_fwd")
            x1, xn2 = glu_fwd(hg, xc, g1)
            st.update(u_s=u_s, x2=x2, yc_s=yc_s, yg=yg, hg=hg)
        w_up, w_down = weights([("ffn_w_up", i), ("ffn_w_down", i)], xn2)
        h3 = mm_up_fwd(xn2, w_up, 0, BF16, "ffn_up_fwd")
        gated = ffn_gate_fwd(h3, p3[i])
        if g_next is None:
            xc = mm_down_fwd(gated, w_down, 0, x1, "ffn_down_fwd")
        else:
            xc, xn = mm_down_fwd(gated, w_down, 0, x1, "ffn_down_fwd", norm=g_next)
        st.update(x1=x1, xn2=xn2, h3=h3, gated=gated, g0=g0, g1=g1)
        acts.append(st)

    dx, loss_lanes, d_final_g = loss_head(xc, w["final_norm_g"][None], tgt)
    loss = lax.psum(loss_lanes[0, 0], MESH_AXES)

    scattering = {}
    d_norm = [[None, None] for _ in range(depth)]
    d_p3 = [None] * depth
    rep = {}
    d_ssm_d = None
    token = None

    def scatter(grads_by_key):
        keys = list(grads_by_key)
        started, tok = scatter_start([grads_by_key[key] for key in keys], "scatter_grads_start")
        scattering[tuple(keys)] = started
        return tok

    for i in reversed(range(depth)):
        mixer, j = i % 3, i // 3
        st = acts[i]
        k_in, k_out = [(k, j) for k in mixer_weights[mixer]]
        w_in, w_out, w_up, w_down = weights([k_in, k_out, ("ffn_w_up", i), ("ffn_w_down", i)], None)
        dgated = mm_down_da(dx, w_down, 0, "ffn_down_da", dep=token)
        g_down = mm_down_dw(st["gated"], dx, "ffn_down_dw")
        dy_a, dy_g, dp_a, dp_g = ffn_gate_bwd(st["h3"], dgated, p3[i])
        d_p3[i] = jnp.concatenate([dp_a, dp_g], axis=0)
        dh3 = ffn_conv_t(dy_a, dy_g, p3[i])
        dxn2 = mm_up_da(dh3, w_up, 0, "ffn_up_da")
        g_up = mm_up_dw(st["xn2"], dh3, "ffn_up_dw")
        dx1, d_norm[i][1] = rms_bwd(st["x1"], st["g1"], dxn2, dx, "rms_bwd")
        token = scatter({("ffn_w_down", i): g_down, ("ffn_w_up", i): g_up})
        if mixer == 0:
            do = mm_rs_da(dx1, w_out, 0, BF16, "attn_out_da", dep=token)
            g_out = mm_rs_dw(st["o"], dx1, "attn_out_dw")
            d3 = sb_attn_bwd(st["qkv"], st["ltot"], do)
            g_in = mm_qkv_dw(st["xn"], d3, w_in.shape[3], "qkv_dw")
            token = scatter({k_in: g_in, k_out: g_out})
            dxn = mm_qkv_da(d3, w_in, 0, "qkv_da", dep=token)
        elif mixer == 1:
            dp = mm_rs_da(dx1, w_out, 0, BF16, "sg_out_da", dep=token)
            g_out = mm_rs_dw(st["p"], dx1, "sg_out_dw")
            dhin, d_ws, d_bfull, d_gain = sgu_bwd(st["hin"], dp, sg_gain, sg_ws, sg_bfull)
            rep.update(sg_w_s=d_ws[None], sg_b=d_bfull[None, :, :, 0], sg_norm_g=d_gain)
            dxn = mm_cs_da(dhin, w_in, 0, t, "sg_in_da")
            g_in = mm_cs_dw(st["xn"], dhin, w_in.shape[3], "sg_in_dw")
        else:
            dhg = glu_bwd(st["hg"], dx1, token)
            dyg = mm_cs_da(dhg, w_out, 0, t, "ssm_glu_da")
            g_out = mm_cs_dw(st["yg"], dhg, w_out.shape[3], "ssm_glu_dw")
            dyc_s, du_skip_s, d_ssm_d = s5_post_bwd(st["yc_s"], st["u_s"], ssm_d_full, s5_reorder(dyg, True))
            dx2 = mm_s5("dx", dyc_s, cd, t, "s5_dx")
            dcd = mm_s5("dcd", st["x2"], dyc_s, t, "s5_dcd")
            g2, da2 = s5_scan_bwd(dx2, st["x2"], a2)
            du = s5_reorder(mm_s5("du", g2, bd, t, "s5_du", res=du_skip_s), False)
            dbd = mm_s5("dbd", st["u_s"], g2, t, "s5_dbd")
            from_bd = lambda blk: jnp.swapaxes(_block_diag_t(blk, h_, p_), 1, 2).reshape(g_ * p_, h_)

            def disc2_bwd(c1, c2, b1, b2, t1, t2):
                return jax.vjp(_disc2, c1, c2, b1, b2)[1]((t1, t2))

            d_cre, d_cim, d_b_re, d_b_im = _single(
                disc2_bwd, [cre_c, cim_c, b_re, b_im, from_bd(dbd[0]), from_bd(dbd[1])],
                [(g_ * p_, 1)] * 2 + [(g_ * p_, h_)] * 2, "s5_disc2_bwd")

            def disc1_bwd(l1, l2, ld, t1, t2, t3, t4):
                return jax.vjp(_disc1, l1, l2, ld)[1]((t1, t2, t3, t4))

            d_lam_re, d_lam_im, d_log_dt = _single(
                disc1_bwd, [lam_re, lam_im, log_dt, da2[0].reshape(g_, p_), da2[1].reshape(g_, p_),
                            d_cre.reshape(g_, p_), d_cim.reshape(g_, p_)],
                [(g_, p_), (g_, p_), (g_, 1)], "s5_disc1_bwd")
            from_cd = lambda blk: jnp.swapaxes(_block_diag_t(blk, p_, h_), 1, 2)
            rep.update(ssm_lam_re=d_lam_re[None], ssm_lam_im=d_lam_im[None], ssm_log_dt=d_log_dt.reshape(1, g_),
                       ssm_b_re=d_b_re.reshape(1, g_, p_, h_), ssm_b_im=d_b_im.reshape(1, g_, p_, h_),
                       ssm_c_re=from_cd(dcd[0])[None], ssm_c_im=-from_cd(dcd[1])[None])
            dxn = mm_rs_da(du, w_in, 0, F32, "ssm_in_da")
            g_in = mm_rs_dw(st["xn"], du, "ssm_in_dw")
        dx, d_norm[i][0] = rms_bwd(st["x"], st["g0"], dxn, dx1, "rms_bwd")
        if mixer != 0:
            token = scatter({k_in: g_in, k_out: g_out})

    rep["final_norm_g"] = d_final_g.reshape(d)
    rep["ffn_conv_b"] = jnp.stack([d_p3[l][:, CONV_K, :].reshape(NDEV * nc) for l in range(depth)])

    d_norm_full = jnp.stack([jnp.concatenate(pair, axis=0) for pair in d_norm])
    d_norm_pieces = jnp.transpose(d_norm_full.reshape(depth, 2, NDEV, d // NDEV), (2, 0, 1, 3))
    d_ssm_d_pieces = jnp.transpose(d_ssm_d.reshape(1, NDEV, d // NDEV), (1, 0, 2))
    d_conv_w_pieces = jnp.stack([d_p3[l][:, :CONV_K, :] for l in range(depth)], axis=1)
    small_pieces = _pack([d_norm_pieces, d_ssm_d_pieces, d_conv_w_pieces], lead=1)
    rep_pack = _pack([rep[k] for k in REPLICATED])
    (rep_started,), _ = gather_start([cast_into_slot(rep_pack[None], 0, me)], "gather_small_grads_start")
    small_started, small_token = scatter_start([small_pieces], "scatter_small_grads_start")
    own, landed = {}, {}
    for keys, started in scattering.items():
        for key, (src, land) in zip(keys, scatter_wait(started, small_token, "scatter_grads_wait")):
            own[key], landed[key] = src, land

    grads, deltas, new_m, new_v = {}, {}, {}, {}
    for k in BIG:
        layers = range(w[k].shape[0])
        res = adamw_layers(as_kept(w, k), as_kept(m, k), as_kept(v, k), [landed[(k, l)] for l in layers],
                           [own[(k, l)] for l in layers], me, "adamw")
        grads[k], deltas[k], new_m[k], new_v[k] = [jnp.swapaxes(r, 1, 2) if k == "ffn_w_up" else r for r in res]
    packs = lambda names: [_pack([tree[k] for k in names]) for tree in (w, m, v)]
    rep_parts, = gather_wait([rep_started], deltas[BIG[-1]], "gather_small_grads_wait")
    (small_own, small_land), = scatter_wait(small_started, deltas[BIG[-1]], "scatter_small_grads_wait")
    res_sharded = adamw_layers(*[p[None] for p in packs(SMALL_SHARDED)], [small_land], [small_own], me, "adamw_small")
    res_replicated = adamw(*packs(REPLICATED), rep_parts, "adamw_small")
    for names, res in ((SMALL_SHARDED, [r[0] for r in res_sharded]), (REPLICATED, res_replicated)):
        for tree, buf in zip((grads, deltas, new_m, new_v), res):
            for k, val in zip(names, _unpack(buf, [w[k].shape for k in names])):
                tree[k] = val
    grad_x = dx.reshape(x.shape)
    return (loss, grad_x, *[grads[k] for k in WEIGHTS], *[deltas[k] for k in WEIGHTS],
            *[new_m[k] for k in WEIGHTS], *[new_v[k] for k in WEIGHTS])


def kernel(x, norm_g, final_norm_g, sb_w_qkv, sb_w_o, sg_w_in, sg_norm_g, sg_w_s, sg_b, sg_w_o, ssm_w_in, ssm_lam_re, ssm_lam_im, ssm_log_dt, ssm_b_re, ssm_b_im, ssm_c_re, ssm_c_im, ssm_d, ssm_w_glu, ffn_w_up, ffn_conv_w, ffn_conv_b, ffn_w_down, loss_target, m_norm_g, m_final_norm_g, m_sb_w_qkv, m_sb_w_o, m_sg_w_in, m_sg_norm_g, m_sg_w_s, m_sg_b, m_sg_w_o, m_ssm_w_in, m_ssm_lam_re, m_ssm_lam_im, m_ssm_log_dt, m_ssm_b_re, m_ssm_b_im, m_ssm_c_re, m_ssm_c_im, m_ssm_d, m_ssm_w_glu, m_ffn_w_up, m_ffn_conv_w, m_ffn_conv_b, m_ffn_w_down, v_norm_g, v_final_norm_g, v_sb_w_qkv, v_sb_w_o, v_sg_w_in, v_sg_norm_g, v_sg_w_s, v_sg_b, v_sg_w_o, v_ssm_w_in, v_ssm_lam_re, v_ssm_lam_im, v_ssm_log_dt, v_ssm_b_re, v_ssm_b_im, v_ssm_c_re, v_ssm_c_im, v_ssm_d, v_ssm_w_glu, v_ffn_w_up, v_ffn_conv_w, v_ffn_conv_b, v_ffn_w_down):
    w = dict(zip(WEIGHTS, (norm_g, final_norm_g, sb_w_qkv, sb_w_o, sg_w_in, sg_norm_g, sg_w_s, sg_b, sg_w_o, ssm_w_in,
                           ssm_lam_re, ssm_lam_im, ssm_log_dt, ssm_b_re, ssm_b_im, ssm_c_re, ssm_c_im, ssm_d, ssm_w_glu,
                           ffn_w_up, ffn_conv_w, ffn_conv_b, ffn_w_down)))
    m = dict(zip(WEIGHTS, (m_norm_g, m_final_norm_g, m_sb_w_qkv, m_sb_w_o, m_sg_w_in, m_sg_norm_g, m_sg_w_s, m_sg_b,
                           m_sg_w_o, m_ssm_w_in, m_ssm_lam_re, m_ssm_lam_im, m_ssm_log_dt, m_ssm_b_re, m_ssm_b_im,
                           m_ssm_c_re, m_ssm_c_im, m_ssm_d, m_ssm_w_glu, m_ffn_w_up, m_ffn_conv_w, m_ffn_conv_b,
                           m_ffn_w_down)))
    v = dict(zip(WEIGHTS, (v_norm_g, v_final_norm_g, v_sb_w_qkv, v_sb_w_o, v_sg_w_in, v_sg_norm_g, v_sg_w_s, v_sg_b,
                           v_sg_w_o, v_ssm_w_in, v_ssm_lam_re, v_ssm_lam_im, v_ssm_log_dt, v_ssm_b_re, v_ssm_b_im,
                           v_ssm_c_re, v_ssm_c_im, v_ssm_d, v_ssm_w_glu, v_ffn_w_up, v_ffn_conv_w, v_ffn_conv_b,
                           v_ffn_w_down)))
    return _step(x, loss_target, w, m, v)
```

```python
import functools

import jax
import jax.numpy as jnp
from jax import lax
from jax.experimental import pallas as pl
from jax.experimental.pallas import tpu as pltpu

F32, BF16 = jnp.float32, jnp.bfloat16
MESH_AXES = ("x", "y", "c")
NDEV = 8
EPS = 1e-6
HEAD_DIM = 64
LANES = 128
ATT_BQ, ATT_BK = 2048, 256
CHUNK = 128
SG_GROUPS = 8
SSM_GROUPS, SSM_STATE, SSM_GROUP = 64, 64, 16
SSM_PACK = 8
S5_PASSES = 1
CONV_K = 3
HALO = 16
ROW_BLOCK = 512
SCAN_COLS = 256
ADAM_LR, ADAM_B1, ADAM_B2, ADAM_EPS, ADAM_WD, ADAM_STEP = 0.001, 0.9, 0.999, 1e-08, 0.01, 10
VMEM_LIMIT = 56 * 1024 * 1024

_MM = (((1,), (0,)), ((), ()))
_MM_TB = (((1,), (1,)), ((), ()))
_MM_TA = (((0,), (0,)), ((), ()))


def _params(sem):
    return pltpu.CompilerParams(dimension_semantics=sem, vmem_limit_bytes=VMEM_LIMIT)


def _rows(total, cap, mult=16):
    best = None
    for d in range(mult, min(total, cap) + 1, mult):
        if total % d == 0:
            best = d
    return best if best is not None else total


def _dot(a, b, dims, passes):
    if passes == 1:
        return lax.dot_general(a.astype(BF16), b.astype(BF16), dims, preferred_element_type=F32)
    a = a.astype(F32)
    b = b.astype(F32)
    ah = a.astype(BF16)
    bh = b.astype(BF16)
    al = (a - ah.astype(F32)).astype(BF16)
    bl = (b - bh.astype(F32)).astype(BF16)
    out = lax.dot_general(ah, bh, dims, preferred_element_type=F32)
    out = out + lax.dot_general(al, bh, dims, preferred_element_type=F32)
    return out + lax.dot_general(ah, bl, dims, preferred_element_type=F32)


def _mm(a, b, *, grid, a_blk, a_map, b_blk, b_map, o_blk, o_map, out_shape, out_dtype, name,
        dims=_MM, passes=1, res=None, res_blk=None, res_map=None, b_2d=None, acc_2d=None, dep=None, norm=None):
    nk = grid[2]
    has_res, has_norm = res is not None, norm is not None
    a_maps = list(a_map) if isinstance(a_map, (list, tuple)) else [a_map]
    b_maps = list(b_map) if isinstance(b_map, (list, tuple)) else [b_map]
    na, nb = len(a_maps), len(b_maps)
    n_in = na + nb + has_res + has_norm + (dep is not None)

    def body(*refs):
        o_ref = refs[n_in]
        r_ref = refs[na + nb] if has_res else None
        av = refs[0][...] if na == 1 else jnp.concatenate([r[...] for r in refs[:na]], axis=-1)
        bv = refs[na][...] if nb == 1 else jnp.concatenate([r[...] for r in refs[na:na + nb]], axis=-1)
        if b_2d is not None:
            bv = bv.reshape(b_2d)
        part = _dot(av, bv, dims, passes)

        def finish(total):
            if has_res:
                total = total + r_ref[...].astype(F32)
            o_ref[...] = total.reshape(o_ref.shape).astype(o_ref.dtype)
            if has_norm:
                refs[n_in + 1][...] = _rms(total, refs[na + nb + has_res][...]).astype(BF16)

        if nk == 1:
            finish(part)
        else:
            acc_ref = refs[-1]
            k = pl.program_id(2)

            @pl.when(k == 0)
            def _():
                acc_ref[...] = part

            @pl.when(k > 0)
            def _():
                acc_ref[...] += part

            @pl.when(k == nk - 1)
            def _():
                finish(acc_ref[...])

    in_specs = [pl.BlockSpec(a_blk, f) for f in a_maps] + [pl.BlockSpec(b_blk, f) for f in b_maps]
    args = [a] * na + [b] * nb
    if has_res:
        in_specs.append(pl.BlockSpec(res_blk, res_map))
        args.append(res)
    if has_norm:
        in_specs.append(pl.BlockSpec(norm.shape, lambda *_: (0, 0)))
        args.append(norm)
    if dep is not None:
        in_specs.append(pl.BlockSpec(memory_space=pl.ANY))
        args.append(dep)
    scratch = [pltpu.VMEM(acc_2d, F32)] if nk > 1 else []
    out_specs, out_shapes = pl.BlockSpec(o_blk, o_map), jax.ShapeDtypeStruct(out_shape, out_dtype)
    if has_norm:
        out_specs, out_shapes = [out_specs] * 2, [out_shapes, jax.ShapeDtypeStruct(out_shape, BF16)]
    return pl.pallas_call(
        body, grid=grid, in_specs=in_specs, out_specs=out_specs, out_shape=out_shapes, scratch_shapes=scratch,
        name=name, compiler_params=_params(("parallel", "parallel", "arbitrary")),
    )(*args)


def _cs_act_spec(ns, tm, row_of, col_of):
    if ns % LANES == 0:
        return (tm, ns), lambda *g: (row_of(*g), col_of(*g))
    return (None, tm, ns), lambda *g: (col_of(*g), row_of(*g), 0)


def mm_cs_fwd(a, w4, l, out_dtype, name, tm=2048):
    m, k = a.shape
    tm = min(tm, m)
    ns = w4.shape[3]
    o_blk, o_map = _cs_act_spec(ns, tm, lambda j, i, kk: i, lambda j, i, kk: j)
    out_shape = (m, NDEV * ns) if ns % LANES == 0 else (NDEV, m, ns)
    return _mm(a, w4, grid=(NDEV, m // tm, 1), a_blk=(tm, k), a_map=lambda j, i, kk: (i, 0),
               b_blk=(None, None, k, ns), b_map=lambda j, i, kk: (j, l, 0, 0),
               o_blk=o_blk, o_map=o_map, out_shape=out_shape, out_dtype=out_dtype, name=name)


def mm_cs_da(dc, w4, l, m, name, tm=1024):
    k, ns = w4.shape[2], w4.shape[3]
    tm = min(tm, m)
    a_blk, a_map = _cs_act_spec(ns, tm, lambda i, _, j: i, lambda i, _, j: j)
    return _mm(dc, w4, grid=(m // tm, 1, NDEV), a_blk=a_blk, a_map=a_map,
               b_blk=(None, None, k, ns), b_map=lambda i, _, j: (j, l, 0, 0),
               o_blk=(tm, k), o_map=lambda i, _, j: (i, 0), out_shape=(m, k), out_dtype=F32,
               dims=_MM_TB, acc_2d=(tm, k), name=name)


def mm_cs_dw(a, dc, ns, name, tk=2048):
    m, k = a.shape
    tk = min(tk, m)
    b_blk, b_map = _cs_act_spec(ns, tk, lambda j, _, kk: kk, lambda j, _, kk: j)
    return _mm(a, dc, grid=(NDEV, 1, m // tk), a_blk=(tk, k), a_map=lambda j, _, kk: (kk, 0),
               b_blk=b_blk, b_map=b_map, o_blk=(None, k, ns), o_map=lambda j, _, kk: (j, 0, 0),
               out_shape=(NDEV, k, ns), out_dtype=BF16, dims=_MM_TA, acc_2d=(k, ns), name=name)


def mm_up_fwd(a, wt4, l, out_dtype, name, tm=2048):
    m, k = a.shape
    tm = min(tm, m)
    ns = wt4.shape[2]
    return _mm(a, wt4, grid=(NDEV, m // tm, 1), a_blk=(tm, k), a_map=lambda j, i, kk: (i, 0),
               b_blk=(None, None, ns, k), b_map=lambda j, i, kk: (j, l, 0, 0), dims=_MM_TB,
               o_blk=(None, tm, ns), o_map=lambda j, i, kk: (j, i, 0), out_shape=(NDEV, m, ns), out_dtype=out_dtype,
               name=name)


def mm_up_da(dc3, wt4, l, name, tm=1024):
    _, m, ns = dc3.shape
    tm = min(tm, m)
    k = wt4.shape[3]
    return _mm(dc3, wt4, grid=(m // tm, 1, NDEV), a_blk=(None, tm, ns), a_map=lambda i, _, j: (j, i, 0),
               b_blk=(None, None, ns, k), b_map=lambda i, _, j: (j, l, 0, 0),
               o_blk=(tm, k), o_map=lambda i, _, j: (i, 0), out_shape=(m, k), out_dtype=F32, acc_2d=(tm, k), name=name)


def mm_up_dw(a, dc3, name, tk=2048):
    m, k = a.shape
    tk = min(tk, m)
    ns = dc3.shape[2]
    return _mm(dc3, a, grid=(NDEV, 1, m // tk), a_blk=(None, tk, ns), a_map=lambda j, _, kk: (j, kk, 0),
               b_blk=(tk, k), b_map=lambda j, _, kk: (kk, 0), dims=_MM_TA,
               o_blk=(None, ns, k), o_map=lambda j, _, kk: (j, 0, 0), out_shape=(NDEV, ns, k), out_dtype=BF16,
               acc_2d=(ns, k), name=name)


def mm_rs_fwd(a, w4, l, res, out_dtype, name, tm=1024, norm=None):
    m, k = a.shape
    tm = min(tm, m)
    ks, n = w4.shape[2], w4.shape[3]
    return _mm(a, w4, grid=(m // tm, 1, 1), a_blk=(tm, k), a_map=lambda i, _, kk: (i, 0),
               b_blk=(NDEV, None, ks, n), b_map=lambda i, _, kk: (0, l, 0, 0), b_2d=(k, n),
               o_blk=(tm, n), o_map=lambda i, _, kk: (i, 0), out_shape=(m, n), out_dtype=out_dtype,
               res=res, res_blk=(tm, n), res_map=lambda i, _, kk: (i, 0), name=name, norm=norm)


def mm_rs_da(dc, w4, l, out_dtype, name, tm=1024, dep=None):
    m, n = dc.shape
    tm = min(tm, m)
    ks = w4.shape[2]
    k = NDEV * ks
    return _mm(dc, w4, grid=(m // tm, 1, 1), a_blk=(tm, n), a_map=lambda i, _, kk: (i, 0),
               b_blk=(NDEV, None, ks, n), b_map=lambda i, _, kk: (0, l, 0, 0), b_2d=(k, n),
               o_blk=(tm, k), o_map=lambda i, _, kk: (i, 0), out_shape=(m, k), out_dtype=out_dtype,
               dims=_MM_TB, name=name, dep=dep)


def mm_rs_dw(a, dc, name, tk=1024):
    m, k = a.shape
    tk = min(tk, m)
    n = dc.shape[1]
    ks = k // NDEV
    return _mm(a, dc, grid=(1, 1, m // tk), a_blk=(tk, k), a_map=lambda _, __, kk: (kk, 0),
               b_blk=(tk, n), b_map=lambda _, __, kk: (kk, 0),
               o_blk=(NDEV, ks, n), o_map=lambda _, __, kk: (0, 0, 0), out_shape=(NDEV, ks, n),
               out_dtype=BF16, dims=_MM_TA, acc_2d=(k, n), name=name)


def mm_down_fwd(a3, w4, l, res, name, tm=1024, norm=None):
    nj, m, kc = a3.shape
    tm = min(tm, m)
    ks, n = w4.shape[2], w4.shape[3]
    return _mm(a3, w4, grid=(m // tm, 1, nj), a_blk=(None, tm, kc), a_map=lambda i, _, j: (j, i, 0),
               b_blk=(2, None, ks, n), b_map=lambda i, _, j: (j, l, 0, 0), b_2d=(kc, n),
               o_blk=(tm, n), o_map=lambda i, _, j: (i, 0), out_shape=(m, n), out_dtype=F32,
               res=res, res_blk=(tm, n), res_map=lambda i, _, j: (i, 0), acc_2d=(tm, n), name=name, norm=norm)


def mm_down_da(dc, w4, l, name, tm=2048, dep=None):
    m, n = dc.shape
    tm = min(tm, m)
    ks = w4.shape[2]
    kc = 2 * ks
    nj = NDEV // 2
    return _mm(dc, w4, grid=(nj, m // tm, 1), a_blk=(tm, n), a_map=lambda j, i, _: (i, 0),
               b_blk=(2, None, ks, n), b_map=lambda j, i, _: (j, l, 0, 0), b_2d=(kc, n),
               o_blk=(None, tm, kc), o_map=lambda j, i, _: (j, i, 0), out_shape=(nj, m, kc),
               out_dtype=BF16, dims=_MM_TB, name=name, dep=dep)


def mm_down_dw(a3, dc, name, tk=2048):
    nj, m, kc = a3.shape
    tk = min(tk, m)
    n = dc.shape[1]
    return _mm(a3, dc, grid=(nj, 1, m // tk), a_blk=(None, tk, kc), a_map=lambda j, _, kk: (j, kk, 0),
               b_blk=(tk, n), b_map=lambda j, _, kk: (kk, 0),
               o_blk=(2, kc // 2, n), o_map=lambda j, _, kk: (j, 0, 0), out_shape=(NDEV, kc // 2, n),
               out_dtype=BF16, dims=_MM_TA, acc_2d=(kc, n), name=name)


def _qkv_group_maps(d, ns, row_of, piece_of):
    per_arr, per_piece = d // LANES, ns // LANES

    def group_map(q):
        def f(*g):
            grp = piece_of(*g) * per_piece + q
            return grp // per_arr, row_of(*g), grp % per_arr
        return f

    return [group_map(q) for q in range(per_piece)]


def mm_qkv_da(d3, w4, l, name, tm=1024, dep=None):
    _, m, d = d3.shape
    tm = min(tm, m)
    k, ns = w4.shape[2], w4.shape[3]
    return _mm(d3, w4, grid=(m // tm, 1, NDEV),
               a_blk=(None, tm, LANES), a_map=_qkv_group_maps(d, ns, lambda i, _, j: i, lambda i, _, j: j),
               b_blk=(None, None, k, ns), b_map=lambda i, _, j: (j, l, 0, 0),
               o_blk=(tm, k), o_map=lambda i, _, j: (i, 0), out_shape=(m, k), out_dtype=F32,
               dims=_MM_TB, acc_2d=(tm, k), name=name, dep=dep)


def mm_qkv_dw(a, d3, ns, name, tk=2048):
    m, k = a.shape
    tk = min(tk, m)
    d = d3.shape[2]
    return _mm(a, d3, grid=(NDEV, 1, m // tk), a_blk=(tk, k), a_map=lambda j, _, kk: (kk, 0),
               b_blk=(None, tk, LANES), b_map=_qkv_group_maps(d, ns, lambda j, _, kk: kk, lambda j, _, kk: j),
               o_blk=(None, k, ns), o_map=lambda j, _, kk: (j, 0, 0),
               out_shape=(NDEV, k, ns), out_dtype=BF16, dims=_MM_TA, acc_2d=(k, ns), name=name)


def _rowwise(fn, ins, outs, *, tr, name, acc_outs=()):
    rows = next(a.shape[0] if kind == "row" else a.shape[1] for a, kind in ins if kind in ("row", "row3"))
    n_in, n_out = len(ins), len(outs)
    n_read = sum(kind != "dep" for _, kind in ins)

    def body(*refs):
        vals = fn(*[r[...] for r in refs[:n_read]])
        if not isinstance(vals, (tuple, list)):
            vals = (vals,)
        for ref, val in zip(refs[n_in:n_in + n_out], vals[:n_out]):
            ref[...] = val.astype(ref.dtype)
        i = pl.program_id(0)
        for ref, val in zip(refs[n_in + n_out:], vals[n_out:]):
            val = val.astype(ref.dtype)

            @pl.when(i == 0)
            def _(ref=ref, val=val):
                ref[...] = val

            @pl.when(i > 0)
            def _(ref=ref, val=val):
                ref[...] += val

    in_specs = []
    for a, kind in ins:
        if kind == "row":
            in_specs.append(pl.BlockSpec((tr, a.shape[1]), lambda i: (i, 0)))
        elif kind == "row3":
            in_specs.append(pl.BlockSpec((a.shape[0], tr, a.shape[2]), lambda i: (0, i, 0)))
        elif kind == "dep":
            in_specs.append(pl.BlockSpec(memory_space=pl.ANY))
        else:
            in_specs.append(pl.BlockSpec(a.shape, lambda i, nd=a.ndim: (0,) * nd))
    out_specs = [pl.BlockSpec((tr, c), lambda i: (i, 0)) for c, _ in outs]
    out_specs += [pl.BlockSpec(s, lambda i, nd=len(s): (0,) * nd) for s, _ in acc_outs]
    out_shape = [jax.ShapeDtypeStruct((rows, c), dt) for c, dt in outs]
    out_shape += [jax.ShapeDtypeStruct(s, dt) for s, dt in acc_outs]
    res = pl.pallas_call(
        body, grid=(rows // tr,), in_specs=in_specs, out_specs=out_specs, out_shape=out_shape, name=name,
        compiler_params=_params(("arbitrary",) if acc_outs else ("parallel",)),
    )(*[a for a, _ in ins])
    return res


def _rms(x, g):
    return x * lax.rsqrt(jnp.mean(x * x, axis=-1, keepdims=True) + EPS) * g


def cast_into_slot(w, l, me):
    _, r, c = w.shape
    tr = _rows(r, 512)

    def body(me_ref, w_ref, o_ref):
        o_ref[...] = w_ref[...].astype(o_ref.dtype)

    return pl.pallas_call(
        body,
        grid_spec=pltpu.PrefetchScalarGridSpec(
            num_scalar_prefetch=1, grid=(r // tr,),
            in_specs=[pl.BlockSpec((None, tr, c), lambda i, me_ref: (l, i, 0))],
            out_specs=pl.BlockSpec((None, tr, c), lambda i, me_ref: (me_ref[0], i, 0))),
        out_shape=jax.ShapeDtypeStruct((NDEV, r, c), BF16), name="cast_into_slot",
        compiler_params=_params(("parallel",)),
    )(me, w)


def rms_fwd(x, g, name):
    out, = _rowwise(_rms, [(x, "row"), (g, "full")], [(x.shape[1], BF16)], tr=ROW_BLOCK, name=name)
    return out


def rms_bwd(x, g, dy, dres, name):
    def fn(xv, gv, dyv, drv):
        _, vjp = jax.vjp(_rms, xv, gv)
        dx, dg = vjp(dyv.astype(F32))
        return drv + dx, dg

    d = x.shape[1]
    return _rowwise(fn, [(x, "row"), (g, "full"), (dy, "row"), (dres, "row")], [(d, F32)], tr=ROW_BLOCK, name=name,
                    acc_outs=[((1, d), F32)])


def loss_head(x, g, tgt):
    def f(xv, gv, tv):
        err = jnp.square(_rms(xv, gv) - tv)
        return 0.5 * jnp.sum(jnp.mean(err, axis=-1))

    def fn(xv, gv, tv):
        val, (dx, dg) = jax.value_and_grad(f, argnums=(0, 1))(xv, gv, tv)
        return dx, jnp.full((1, LANES), val, F32), dg

    d = x.shape[1]
    return _rowwise(fn, [(x, "row"), (g, "full"), (tgt, "row")], [(d, F32)], tr=ROW_BLOCK, name="loss_head",
                    acc_outs=[((1, LANES), F32), ((1, d), F32)])


def _glu(hg, x):
    half = hg.shape[1] // 2
    return x + hg[:, :half] * jax.nn.sigmoid(hg[:, half:])


def glu_fwd(hg, x, norm):
    def fn(h, xv, g):
        x1 = _glu(h.astype(F32), xv)
        return x1, _rms(x1, g)

    d = x.shape[1]
    return _rowwise(fn, [(hg, "row"), (x, "row"), (norm, "full")], [(d, F32), (d, BF16)], tr=ROW_BLOCK, name="glu_fwd")


def glu_bwd(hg, dx1, dep):
    def fn(h, d):
        _, vjp = jax.vjp(lambda hv: _glu(hv, jnp.zeros_like(d)), h.astype(F32))
        return vjp(d)[0]

    out, = _rowwise(fn, [(hg, "row"), (dx1, "row"), (dep, "dep")], [(hg.shape[1], BF16)], tr=ROW_BLOCK, name="glu_bwd")
    return out


def _s5_post(yc, u, d):
    return jax.nn.gelu(yc + d * u)


def s5_post_fwd(yc, u, d):
    out, = _rowwise(_s5_post, [(yc, "row"), (u, "row"), (d, "full")], [(yc.shape[1], BF16)], tr=ROW_BLOCK,
                    name="s5_post_fwd")
    return out


def s5_post_bwd(yc, u, d, dyg):
    def fn(ycv, uv, dv, g):
        _, vjp = jax.vjp(_s5_post, ycv, uv, dv)
        return vjp(g.astype(F32))

    dm = yc.shape[1]
    return _rowwise(fn, [(yc, "row"), (u, "row"), (d, "full"), (dyg, "row")], [(dm, F32), (dm, F32)], tr=ROW_BLOCK,
                    name="s5_post_bwd", acc_outs=[((1, dm), F32)])


def _adam_update(wv, mv, vv, g):
    m2 = ADAM_B1 * mv + (1.0 - ADAM_B1) * g
    v2 = ADAM_B2 * vv + (1.0 - ADAM_B2) * jnp.square(g)
    m_hat = m2 / (1.0 - ADAM_B1 ** ADAM_STEP)
    v_hat = v2 / (1.0 - ADAM_B2 ** ADAM_STEP)
    delta = -ADAM_LR * (m_hat / (jnp.sqrt(v_hat) + ADAM_EPS) + ADAM_WD * wv)
    return g, delta, m2, v2


def adamw(w, m, v, g_parts, name):
    def fn(wv, mv, vv, gp):
        g = gp[0].astype(F32)
        for p in range(1, gp.shape[0]):
            g = g + gp[p].astype(F32)
        return _adam_update(wv, mv, vv, g)

    c = w.shape[1]
    return _rowwise(fn, [(w, "row"), (m, "row"), (v, "row"), (g_parts, "row3")], [(c, F32)] * 4,
                    tr=_rows(w.shape[0], 256), name=name)


def adamw_layers(w, m, v, lands, owns, me, name):
    nl, r, c = w.shape
    tr = _rows(r, 256)

    def body(me_ref, w_ref, m_ref, v_ref, *rest):
        land_refs, own_refs, out_refs = rest[:nl], rest[nl:2 * nl], rest[2 * nl:]
        for l in range(nl):
            @pl.when(pl.program_id(0) == l)
            def _(l=l):
                g = own_refs[l][...].astype(F32)
                for p in range(NDEV - 1):
                    g = g + land_refs[l][p].astype(F32)
                for ref, val in zip(out_refs, _adam_update(w_ref[...], m_ref[...], v_ref[...], g)):
                    ref[...] = val

    def rows_of(l):
        return lambda li, i, me_ref: jnp.where(li == l, i, 0)

    wspec = pl.BlockSpec((None, tr, c), lambda li, i, me_ref: (li, i, 0))
    in_specs = [wspec] * 3
    in_specs += [pl.BlockSpec((NDEV - 1, tr, c), lambda li, i, me_ref, f=rows_of(l): (0, f(li, i, me_ref), 0))
                 for l in range(nl)]
    in_specs += [pl.BlockSpec((None, tr, c), lambda li, i, me_ref, f=rows_of(l): (me_ref[0], f(li, i, me_ref), 0))
                 for l in range(nl)]
    return pl.pallas_call(
        body,
        grid_spec=pltpu.PrefetchScalarGridSpec(
            num_scalar_prefetch=1, grid=(nl, r // tr), in_specs=in_specs, out_specs=[wspec] * 4),
        out_shape=[jax.ShapeDtypeStruct(w.shape, F32)] * 4, name=name, compiler_params=_params(("parallel", "parallel")),
    )(me, w, m, v, *lands, *owns)


def _conv_rows(cur, halo, p, first):
    r = cur.shape[0]
    ext = jnp.concatenate([jnp.where(first, 0.0, halo), cur], axis=0)
    s1 = pltpu.roll(ext, 1, 0)[HALO:]
    s2 = pltpu.roll(ext, 2, 0)[HALO:]
    return p[0:1] * s2 + p[1:2] * s1 + p[2:3] * cur + p[3:4], s1, s2


def ffn_gate_fwd(h3, p3, tr=ROW_BLOCK):
    _, t, c = h3.shape
    half = NDEV // 2

    def body(a_ref, ah_ref, g_ref, gh_ref, pa_ref, pg_ref, o_ref):
        first = pl.program_id(1) == 0
        ya, _, _ = _conv_rows(a_ref[...].astype(F32), ah_ref[...].astype(F32), pa_ref[...], first)
        yg, _, _ = _conv_rows(g_ref[...].astype(F32), gh_ref[...].astype(F32), pg_ref[...], first)
        o_ref[...] = (jax.nn.silu(yg) * ya).astype(o_ref.dtype)

    main = lambda off: pl.BlockSpec((None, tr, c), lambda j, i: (j + off, i, 0))
    halo = lambda off: pl.BlockSpec((None, HALO, c), lambda j, i: (j + off, jnp.maximum(i * (tr // HALO) - 1, 0), 0))
    par = lambda off: pl.BlockSpec((None, 8, c), lambda j, i: (j + off, 0, 0))
    return pl.pallas_call(
        body, grid=(half, t // tr),
        in_specs=[main(0), halo(0), main(half), halo(half), par(0), par(half)],
        out_specs=pl.BlockSpec((None, tr, c), lambda j, i: (j, i, 0)),
        out_shape=jax.ShapeDtypeStruct((half, t, c), BF16), name="ffn_gate_fwd",
        compiler_params=_params(("parallel", "parallel")),
    )(h3, h3, h3, h3, p3, p3)


def ffn_gate_bwd(h3, dgated3, p3, tr=ROW_BLOCK):
    _, t, c = h3.shape
    half = NDEV // 2

    def body(a_ref, ah_ref, g_ref, gh_ref, dg_ref, pa_ref, pg_ref, dya_ref, dyg_ref, dpa_ref, dpg_ref):
        i = pl.program_id(1)
        first = i == 0
        a = a_ref[...].astype(F32)
        g = g_ref[...].astype(F32)
        ya, a1, a2 = _conv_rows(a, ah_ref[...].astype(F32), pa_ref[...], first)
        yg, g1, g2 = _conv_rows(g, gh_ref[...].astype(F32), pg_ref[...], first)
        d = dg_ref[...].astype(F32)
        sig = jax.nn.sigmoid(yg)
        d_ya = d * (yg * sig)
        d_yg = d * ya * (sig * (1.0 + yg * (1.0 - sig)))
        dya_ref[...] = d_ya.astype(dya_ref.dtype)
        dyg_ref[...] = d_yg.astype(dyg_ref.dtype)
        for dy, cur, s1, s2, dp_ref in ((d_ya, a, a1, a2, dpa_ref), (d_yg, g, g1, g2, dpg_ref)):
            rows = [jnp.sum(dy * s2, axis=0, keepdims=True), jnp.sum(dy * s1, axis=0, keepdims=True),
                    jnp.sum(dy * cur, axis=0, keepdims=True), jnp.sum(dy, axis=0, keepdims=True)]
            dp = jnp.concatenate(rows + [jnp.zeros((4, c), F32)], axis=0)

            @pl.when(first)
            def _(dp_ref=dp_ref, dp=dp):
                dp_ref[...] = dp

            @pl.when(i > 0)
            def _(dp_ref=dp_ref, dp=dp):
                dp_ref[...] += dp

    main = lambda off: pl.BlockSpec((None, tr, c), lambda j, i: (j + off, i, 0))
    halo = lambda off: pl.BlockSpec((None, HALO, c), lambda j, i: (j + off, jnp.maximum(i * (tr // HALO) - 1, 0), 0))
    par = lambda off: pl.BlockSpec((None, 8, c), lambda j, i: (j + off, 0, 0))
    return pl.pallas_call(
        body, grid=(half, t // tr),
        in_specs=[main(0), halo(0), main(half), halo(half), main(0), par(0), par(half)],
        out_specs=[main(0), main(0), par(0), par(0)],
        out_shape=[jax.ShapeDtypeStruct((half, t, c), BF16)] * 2 + [jax.ShapeDtypeStruct((half, 8, c), F32)] * 2,
        name="ffn_gate_bwd", compiler_params=_params(("parallel", "arbitrary")),
    )(h3, h3, h3, h3, dgated3, p3, p3)


def ffn_conv_t(dy_a, dy_g, p3, tr=2 * ROW_BLOCK):
    half, t, c = dy_a.shape
    tr = min(tr, t)
    nblk = t // tr

    def body(a_ref, ah_ref, g_ref, gh_ref, p_ref, o_ref):
        is_a = pl.program_id(0) < half
        last = pl.program_id(1) == nblk - 1
        cur = jnp.where(is_a, a_ref[...], g_ref[...]).astype(F32)
        nxt = jnp.where(is_a, ah_ref[...], gh_ref[...]).astype(F32)
        ext = jnp.concatenate([cur, jnp.where(last, 0.0, nxt)], axis=0)
        n = tr + HALO
        s1 = pltpu.roll(ext, n - 1, 0)[:tr]
        s2 = pltpu.roll(ext, n - 2, 0)[:tr]
        p = p_ref[...]
        o_ref[...] = (p[2:3] * cur + p[1:2] * s1 + p[0:1] * s2).astype(o_ref.dtype)

    main = pl.BlockSpec((None, tr, c), lambda j, i: (j % half, i, 0))
    halo = pl.BlockSpec((None, HALO, c), lambda j, i: (j % half, jnp.minimum((i + 1) * (tr // HALO), t // HALO - 1), 0))
    return pl.pallas_call(
        body, grid=(NDEV, nblk),
        in_specs=[main, halo, main, halo, pl.BlockSpec((None, 8, c), lambda j, i: (j, 0, 0))],
        out_specs=pl.BlockSpec((None, tr, c), lambda j, i: (j, i, 0)),
        out_shape=jax.ShapeDtypeStruct((NDEV, t, c), BF16), name="ffn_conv_t",
        compiler_params=_params(("parallel", "parallel")),
    )(dy_a, dy_a, dy_g, dy_g, p3)


def _att_consts(bq, bk):
    lane = lax.broadcasted_iota(jnp.int32, (1, LANES), 1)
    heads = (lane < HEAD_DIM, lane >= HEAD_DIM)
    rr = lax.broadcasted_iota(jnp.int32, (bq, bk), 0)
    cc = lax.broadcasted_iota(jnp.int32, (bq, bk), 1)
    kr = lax.broadcasted_iota(jnp.int32, (bk, bk), 0)
    kc = lax.broadcasted_iota(jnp.int32, (bk, bk), 1)
    return heads, rr, cc, kr, kc


def _split_dot(x, tri, parts):
    out = None
    for _ in range(parts):
        piece = x.astype(BF16)
        x = x - piece.astype(F32)
        term = jnp.dot(piece, tri, preferred_element_type=F32)
        out = term if out is None else out + term
    return out


def _att_logits(qh, k):
    z = lax.dot_general(qh, k, _MM_TB, preferred_element_type=F32)
    lsp = jnp.minimum(z, 0.0) - jnp.log(1.0 + jnp.exp(-jnp.abs(z)))
    return lsp, lsp - z


def _per_head(heads, a, b):
    return jnp.where(heads[0], a, b)


def sb_attn_fwd(qkv):
    t, d3 = qkv.shape
    d = d3 // 3
    npair = d // LANES
    bq, bk = min(ATT_BQ, t), min(ATT_BK, t)
    kpq = bq // bk

    def body(q_ref, k_ref, v_ref, o_ref, lt_ref, acc_ref):
        heads, rr, cc, kr, kc = _att_consts(bq, bk)
        suffix = (kr > kc).astype(BF16)

        def trip(qh, k0, r0, runs):
            k = k_ref[pl.ds(k0, bk), :]
            v = v_ref[pl.ds(k0, bk), :]
            diag, r0 = r0 is not None, r0 or 0
            valid = cc[:bq - r0] < rr[:bq - r0]
            new_runs = []
            for h in range(2):
                lsp, lraw = _att_logits(qh[h][r0:], k)
                lm = jnp.where(valid, lraw, 0.0) if diag else lraw
                w = jnp.exp(lsp + _split_dot(lm, suffix, 2) + runs[h][r0:])
                if diag:
                    w = jnp.where(valid, w, 0.0)
                acc_ref[h, r0:, :] += jnp.dot(w.astype(BF16), v, preferred_element_type=F32)
                below = runs[h][r0:] + jnp.sum(lm, axis=1, keepdims=True)
                new_runs.append(jnp.concatenate([runs[h][:r0], below], axis=0) if r0 else below)
            return tuple(new_runs)

        def q_loop(qb, _):
            q0 = pl.multiple_of(qb * bq, bq)
            q = q_ref[pl.ds(q0, bq), :] * 0.125
            qh = [jnp.where(hm, q, 0.0).astype(BF16) for hm in heads]
            acc_ref[...] = jnp.zeros_like(acc_ref)
            runs = (jnp.zeros((bq, 1), F32),) * 2
            for dblk in reversed(range(kpq)):
                runs = trip(qh, pl.multiple_of(q0 + dblk * bk, bk), dblk * bk, runs)
            nleft = qb * kpq
            runs = lax.fori_loop(
                0, nleft, lambda i, r: trip(qh, pl.multiple_of((nleft - 1 - i) * bk, bk), None, r), runs)
            o_ref[pl.ds(q0, bq), :] = _per_head(heads, acc_ref[0], acc_ref[1])
            lt_ref[pl.ds(q0, bq), :] = _per_head(heads, runs[0], runs[1])
            return 0

        lax.fori_loop(0, t // bq, q_loop, 0)

    col = lambda off: pl.BlockSpec((t, LANES), lambda p: (0, p + off))
    return pl.pallas_call(
        body, grid=(npair,), in_specs=[col(0), col(npair), col(2 * npair)], out_specs=[col(0), col(0)],
        out_shape=[jax.ShapeDtypeStruct((t, d), F32)] * 2, scratch_shapes=[pltpu.VMEM((2, bq, LANES), F32)],
        name="sb_attn_fwd", compiler_params=_params(("parallel",)),
    )(qkv, qkv, qkv)


def sb_attn_bwd(qkv, ltot, do):
    t, d3 = qkv.shape
    d = d3 // 3
    npair = d // LANES
    bq, bk = min(ATT_BQ, t), min(ATT_BK, t)
    kpq = bq // bk

    def body(q_ref, k_ref, v_ref, lt_ref, do_ref, d_ref, dk_acc, dv_acc, dq_acc):
        heads, rr, cc, kr, kc = _att_consts(bq, bk)
        suffix = (kr > kc).astype(BF16)
        prefix_excl = (kr < kc).astype(BF16)
        dk_acc[...] = jnp.zeros_like(dk_acc)
        dv_acc[...] = jnp.zeros_like(dv_acc)

        def trip(qh, doh, lt, k0, r0, carry):
            lruns, gruns = carry
            k = k_ref[pl.ds(k0, bk), :]
            v = v_ref[pl.ds(k0, bk), :]
            diag, r0 = r0 is not None, r0 or 0
            valid = cc[:bq - r0] < rr[:bq - r0]
            new_lruns, new_gruns = [], []
            dk_blk = jnp.zeros((bk, LANES), F32)
            dv_blk = jnp.zeros((bk, LANES), F32)
            for h in range(2):
                q_rows, do_rows = qh[h][r0:], doh[h][r0:]
                lsp, lraw = _att_logits(q_rows, k)
                lm = jnp.where(valid, lraw, 0.0) if diag else lraw
                l_below = lruns[h][r0:] + jnp.sum(lm, axis=1, keepdims=True)
                right = _split_dot(lm, suffix, 1) + (lt[h][r0:] - l_below)
                w = jnp.exp(lsp + right)
                if diag:
                    w = jnp.where(valid, w, 0.0)
                g = lax.dot_general(do_rows, v, _MM_TB, preferred_element_type=F32) * w
                left = gruns[h][r0:] + _split_dot(g, prefix_excl, 1)
                dz = g * jnp.exp(lraw) - jnp.exp(lsp) * left
                if diag:
                    dz = jnp.where(valid, dz, 0.0)
                dz = dz.astype(BF16)
                kh = jnp.where(heads[h], k, 0.0).astype(BF16)
                dq_acc[h, r0:, :] += jnp.dot(dz, kh, preferred_element_type=F32)
                dk_blk = dk_blk + lax.dot_general(dz, q_rows, _MM_TA, preferred_element_type=F32)
                dv_blk = dv_blk + lax.dot_general(w.astype(BF16), do_rows, _MM_TA, preferred_element_type=F32)
                g_below = gruns[h][r0:] + jnp.sum(g, axis=1, keepdims=True)
                new_lruns.append(jnp.concatenate([lruns[h][:r0], l_below], axis=0) if r0 else l_below)
                new_gruns.append(jnp.concatenate([gruns[h][:r0], g_below], axis=0) if r0 else g_below)
            dk_acc[pl.ds(k0, bk), :] += dk_blk
            dv_acc[pl.ds(k0, bk), :] += dv_blk
            return tuple(new_lruns), tuple(new_gruns)

        def q_loop(qb, _):
            q0 = pl.multiple_of(qb * bq, bq)
            q = q_ref[pl.ds(q0, bq), :] * 0.125
            dout = do_ref[pl.ds(q0, bq), :]
            lt2 = lt_ref[pl.ds(q0, bq), :]
            qh = [jnp.where(hm, q, 0.0).astype(BF16) for hm in heads]
            doh = [jnp.where(hm, dout, 0.0).astype(BF16) for hm in heads]
            lt = [jnp.max(jnp.where(hm, lt2, -jnp.inf), axis=1, keepdims=True) for hm in heads]
            dq_acc[...] = jnp.zeros_like(dq_acc)
            col = (jnp.zeros((bq, 1), F32),) * 2
            carry = lax.fori_loop(
                0, qb * kpq, lambda kb, c: trip(qh, doh, lt, pl.multiple_of(kb * bk, bk), None, c), (col, col))
            for dblk in range(kpq):
                carry = trip(qh, doh, lt, pl.multiple_of(q0 + dblk * bk, bk), dblk * bk, carry)
            d_ref[0, pl.ds(q0, bq), :] = ((dq_acc[0] + dq_acc[1]) * 0.125).astype(d_ref.dtype)
            return 0

        lax.fori_loop(0, t // bq, q_loop, 0)
        d_ref[1] = dk_acc[...].astype(d_ref.dtype)
        d_ref[2] = dv_acc[...].astype(d_ref.dtype)

    col = lambda off: pl.BlockSpec((t, LANES), lambda p: (0, p + off))
    return pl.pallas_call(
        body, grid=(npair,), in_specs=[col(0), col(npair), col(2 * npair), col(0), col(0)],
        out_specs=pl.BlockSpec((3, t, LANES), lambda p: (0, 0, p)),
        out_shape=jax.ShapeDtypeStruct((3, t, d), BF16),
        scratch_shapes=[pltpu.VMEM((t, LANES), F32), pltpu.VMEM((t, LANES), F32), pltpu.VMEM((2, bq, LANES), F32)],
        name="sb_attn_bwd", compiler_params=_params(("parallel",)),
    )(qkv, qkv, qkv, ltot, do)


def _sgu_parts(hin, g, ws_ref, bf_ref):
    width = hin.shape[1] // 2
    h = jax.nn.gelu(hin)
    u, v = h[:, :width], h[:, width:]
    r = lax.rsqrt(jnp.mean(v * v, axis=-1, keepdims=True) + EPS)
    vn = v * r * g
    rr = lax.broadcasted_iota(jnp.int32, (CHUNK, CHUNK), 0)
    cc = lax.broadcasted_iota(jnp.int32, (CHUNK, CHUNK), 1)
    causal = cc <= rr
    wcs = [jnp.where(causal, ws_ref[gi], 0.0).astype(BF16) for gi in range(SG_GROUPS)]
    sv = jnp.concatenate(
        [jnp.dot(wcs[gi], vn[:, gi * CHUNK:(gi + 1) * CHUNK].astype(BF16), preferred_element_type=F32) + bf_ref[gi]
         for gi in range(SG_GROUPS)], axis=1)
    return u, v, r, vn, wcs, sv, causal


def sgu_fwd(hin, g, ws, bfull):
    t, w2 = hin.shape
    width = w2 // 2

    def body(h_ref, g_ref, ws_ref, bf_ref, o_ref):
        u, _, _, _, _, sv, _ = _sgu_parts(h_ref[...].astype(F32), g_ref[...], ws_ref, bf_ref)
        o_ref[...] = (u * sv).astype(o_ref.dtype)

    full = lambda a: pl.BlockSpec(a.shape, lambda i, nd=a.ndim: (0,) * nd)
    return pl.pallas_call(
        body, grid=(t // CHUNK,), in_specs=[pl.BlockSpec((CHUNK, w2), lambda i: (i, 0)), full(g), full(ws), full(bfull)],
        out_specs=pl.BlockSpec((CHUNK, width), lambda i: (i, 0)), out_shape=jax.ShapeDtypeStruct((t, width), BF16),
        name="sgu_fwd", compiler_params=_params(("parallel",)),
    )(hin, g, ws, bfull)


def sgu_bwd(hin, dp, g, ws, bfull):
    t, w2 = hin.shape
    width = w2 // 2

    def body(h_ref, dp_ref, g_ref, ws_ref, bf_ref, dh_ref, dws_ref, dbf_ref, dg_ref):
        i = pl.program_id(0)
        hin_v = h_ref[...].astype(F32)
        gv = g_ref[...]
        u, v, r, vn, wcs, sv, causal = _sgu_parts(hin_v, gv, ws_ref, bf_ref)
        dpv = dp_ref[...].astype(F32)
        du = dpv * sv
        dsv = dpv * u
        dvn_parts, dws_parts, dbf_parts = [], [], []
        for gi in range(SG_GROUPS):
            dsv_g = dsv[:, gi * CHUNK:(gi + 1) * CHUNK]
            dsv_b = dsv_g.astype(BF16)
            dvn_parts.append(lax.dot_general(wcs[gi], dsv_b, _MM_TA, preferred_element_type=F32))
            vn_b = vn[:, gi * CHUNK:(gi + 1) * CHUNK].astype(BF16)
            dws_parts.append(jnp.where(causal, lax.dot_general(dsv_b, vn_b, _MM_TB, preferred_element_type=F32), 0.0))
            dbf_parts.append(jnp.broadcast_to(jnp.sum(dsv_g, axis=1, keepdims=True), (CHUNK, CHUNK)))
        dvn = jnp.concatenate(dvn_parts, axis=1)
        dgain = jnp.sum(dvn * v * r, axis=0, keepdims=True)
        gvv = dvn * gv
        dv = r * gvv - v * (r * r * r) * jnp.mean(v * gvv, axis=-1, keepdims=True)
        _, vjp = jax.vjp(jax.nn.gelu, hin_v)
        dh_ref[...] = vjp(jnp.concatenate([du, dv], axis=1))[0].astype(dh_ref.dtype)

        @pl.when(i == 0)
        def _():
            for gi in range(SG_GROUPS):
                dws_ref[gi] = dws_parts[gi]
                dbf_ref[gi] = dbf_parts[gi]
            dg_ref[...] = dgain

        @pl.when(i > 0)
        def _():
            for gi in range(SG_GROUPS):
                dws_ref[gi] += dws_parts[gi]
                dbf_ref[gi] += dbf_parts[gi]
            dg_ref[...] += dgain

    full = lambda a: pl.BlockSpec(a.shape, lambda i, nd=a.ndim: (0,) * nd)
    sq = (SG_GROUPS, CHUNK, CHUNK)
    return pl.pallas_call(
        body, grid=(t // CHUNK,),
        in_specs=[pl.BlockSpec((CHUNK, w2), lambda i: (i, 0)), pl.BlockSpec((CHUNK, width), lambda i: (i, 0)),
                  full(g), full(ws), full(bfull)],
        out_specs=[pl.BlockSpec((CHUNK, w2), lambda i: (i, 0)), pl.BlockSpec(sq, lambda i: (0, 0, 0)),
                   pl.BlockSpec(sq, lambda i: (0, 0, 0)), pl.BlockSpec((1, width), lambda i: (0, 0))],
        out_shape=[jax.ShapeDtypeStruct((t, w2), BF16), jax.ShapeDtypeStruct(sq, F32), jax.ShapeDtypeStruct(sq, F32),
                   jax.ShapeDtypeStruct((1, width), F32)],
        name="sgu_bwd", compiler_params=_params(("arbitrary",)),
    )(hin, dp, g, ws, bfull)


def _disc1(lam_re, lam_im, log_dt):
    lr = jnp.minimum(lam_re, -1e-4)
    li = lam_im
    dt = jnp.exp(log_dt)
    mag = jnp.exp(dt * lr)
    ar = mag * jnp.cos(dt * li)
    ai = mag * jnp.sin(dt * li)
    den = lr * lr + li * li
    return ar, ai, ((ar - 1.0) * lr + ai * li) / den, (ai * lr - (ar - 1.0) * li) / den


def _disc2(cre, cim, b_re, b_im):
    return cre * b_re - cim * b_im, cre * b_im + cim * b_re


def _single(fn, ins, out_shapes, name):
    n = len(ins)

    def body(*refs):
        vals = fn(*[r[...] for r in refs[:n]])
        for ref, val in zip(refs[n:], vals):
            ref[...] = val

    return pl.pallas_call(body, out_shape=[jax.ShapeDtypeStruct(s, F32) for s in out_shapes], name=name)(*ins)


SCAN_SEGMENTS = 8
REORDER_STEPS = 64


def s5_reorder(x, to_steps):
    t, d = x.shape
    ns = SCAN_SEGMENTS
    seg = t // ns
    ts = min(REORDER_STEPS, seg)
    by_segment = ((ns, seg, d), pl.BlockSpec((ns, ts, d), lambda i: (0, i, 0)))
    by_step = ((seg, ns, d), pl.BlockSpec((ts, ns, d), lambda i: (i, 0, 0)))
    (in_shape, in_spec), (out_shape, out_spec) = (by_segment, by_step) if to_steps else (by_step, by_segment)

    def body(x_ref, o_ref):
        o_ref[...] = jnp.swapaxes(x_ref[...], 0, 1)

    out = pl.pallas_call(
        body, grid=(seg // ts,), in_specs=[in_spec], out_specs=out_spec,
        out_shape=jax.ShapeDtypeStruct(out_shape, x.dtype), name="s5_reorder", compiler_params=_params(("parallel",)),
    )(x.reshape(in_shape))
    return out.reshape(t, d)


def _cpow(ar, ai, n):
    rr, ri = None, None
    while n:
        if n & 1:
            rr, ri = (ar, ai) if rr is None else (rr * ar - ri * ai, rr * ai + ri * ar)
        ar, ai = ar * ar - ai * ai, 2.0 * ar * ai
        n >>= 1
    return rr, ri


def _edge_states(er, ei, pr, pi, reverse):
    ns = SCAN_SEGMENTS
    zero = jnp.zeros_like(er[0:1])
    rows_r, rows_i = [None] * ns, [None] * ns
    order = range(ns - 1, -1, -1) if reverse else range(ns)
    prev = None
    for s in order:
        if prev is None:
            rows_r[s], rows_i[s] = zero, zero
        else:
            cr, ci = rows_r[prev], rows_i[prev]
            rows_r[s] = er[prev:prev + 1] + pr * cr - pi * ci
            rows_i[s] = ei[prev:prev + 1] + pr * ci + pi * cr
        prev = s
    return jnp.concatenate(rows_r, axis=0), jnp.concatenate(rows_i, axis=0)


def s5_scan_fwd(bu2, a2):
    _, t, n = bu2.shape
    cb, ns = SCAN_COLS, SCAN_SEGMENTS
    seg = t // ns

    def body(bu_ref, a_ref, x_ref):
        ar, ai = a_ref[0:1, :], a_ref[1:2, :]

        def local(i, carry):
            xr, xi = carry
            xr, xi = ar * xr - ai * xi + bu_ref[0, i], ar * xi + ai * xr + bu_ref[1, i]
            x_ref[0, i] = xr
            x_ref[1, i] = xi
            return xr, xi

        zero = jnp.zeros((ns, cb), F32)
        er, ei = lax.fori_loop(0, seg, local, (zero, zero))
        cr, ci = _edge_states(er, ei, *_cpow(ar, ai, seg), reverse=False)

        def fix(i, carry):
            wr, wi = carry
            wr, wi = wr * ar - wi * ai, wr * ai + wi * ar
            x_ref[0, i] += wr * cr - wi * ci
            x_ref[1, i] += wr * ci + wi * cr
            return wr, wi

        lax.fori_loop(0, seg, fix, (jnp.ones((1, cb), F32), jnp.zeros((1, cb), F32)))

    blk = pl.BlockSpec((2, seg, ns, cb), lambda j: (0, 0, 0, j))
    out = pl.pallas_call(
        body, grid=(n // cb,), in_specs=[blk, pl.BlockSpec((2, cb), lambda j: (0, j))], out_specs=blk,
        out_shape=jax.ShapeDtypeStruct((2, seg, ns, n), F32), name="s5_scan_fwd", compiler_params=_params(("parallel",)),
    )(bu2.reshape(2, seg, ns, n), a2)
    return out.reshape(2, t, n)


def s5_scan_bwd(dx2, x2, a2):
    _, t, n = dx2.shape
    cb, ns = SCAN_COLS, SCAN_SEGMENTS
    seg = t // ns

    def body(dx_ref, x_ref, a_ref, g_ref, da_ref):
        ar, ai = a_ref[0:1, :], a_ref[1:2, :]

        def local(s, carry):
            gr, gi = carry
            i = seg - 1 - s
            gr, gi = dx_ref[0, i] + ar * gr + ai * gi, dx_ref[1, i] - ai * gr + ar * gi
            g_ref[0, i] = gr
            g_ref[1, i] = gi
            return gr, gi

        zero = jnp.zeros((ns, cb), F32)
        er, ei = lax.fori_loop(0, seg, local, (zero, zero))
        cr, ci = _edge_states(er, ei, *_cpow(ar, -ai, seg), reverse=True)
        row = lax.broadcasted_iota(jnp.int32, (ns, cb), 0)
        before_r = jnp.where(row == 0, 0.0, pltpu.roll(x_ref[0, seg - 1], 1, 0))
        before_i = jnp.where(row == 0, 0.0, pltpu.roll(x_ref[1, seg - 1], 1, 0))

        def fix(s, carry):
            wr, wi, dar, dai = carry
            i = seg - 1 - s
            wr, wi = wr * ar + wi * ai, wi * ar - wr * ai
            gr = g_ref[0, i] + wr * cr - wi * ci
            gi = g_ref[1, i] + wr * ci + wi * cr
            g_ref[0, i] = gr
            g_ref[1, i] = gi
            ip = jnp.maximum(i - 1, 0)
            xpr = jnp.where(i == 0, before_r, x_ref[0, ip])
            xpi = jnp.where(i == 0, before_i, x_ref[1, ip])
            return wr, wi, dar + gr * xpr + gi * xpi, dai + gi * xpr - gr * xpi

        one, z1 = jnp.ones((1, cb), F32), jnp.zeros((1, cb), F32)
        _, _, dar, dai = lax.fori_loop(0, seg, fix, (one, z1, zero, zero))
        da_ref[0:1, :] = jnp.sum(dar, axis=0, keepdims=True)
        da_ref[1:2, :] = jnp.sum(dai, axis=0, keepdims=True)

    blk = pl.BlockSpec((2, seg, ns, cb), lambda j: (0, 0, 0, j))
    vec = pl.BlockSpec((2, cb), lambda j: (0, j))
    g4, da = pl.pallas_call(
        body, grid=(n // cb,), in_specs=[blk, blk, vec], out_specs=[blk, vec],
        out_shape=[jax.ShapeDtypeStruct((2, seg, ns, n), F32), jax.ShapeDtypeStruct((2, n), F32)],
        name="s5_scan_bwd", compiler_params=_params(("parallel",)),
    )(dx2.reshape(2, seg, ns, n), x2.reshape(2, seg, ns, n), a2)
    return g4.reshape(2, t, n), da


_SP_U = SSM_PACK * SSM_GROUP
_SP_X = SSM_PACK * SSM_STATE
_NKB = SSM_GROUPS // SSM_PACK


def mm_s5(kind, a, b, m, name, res=None, tm=2048):
    tm = min(tm, m)
    kw = dict(passes=S5_PASSES, name=name)
    xblk = lambda row, sel, col: ((None, tm, _SP_X), lambda *g: (sel(*g), row(*g), col(*g)))
    if kind == "bu":
        o_blk, o_map = xblk(lambda g, i, k: i, lambda g, i, k: g // _NKB, lambda g, i, k: g % _NKB)
        return _mm(a, b, grid=(2 * _NKB, m // tm, 1), a_blk=(tm, _SP_U), a_map=lambda g, i, k: (i, g % _NKB),
                   b_blk=(None, None, _SP_U, _SP_X), b_map=lambda g, i, k: (g // _NKB, g % _NKB, 0, 0),
                   o_blk=o_blk, o_map=o_map, out_shape=(2, m, _NKB * _SP_X), out_dtype=F32, **kw)
    if kind == "yc":
        a_blk, a_map = xblk(lambda j, i, k: i, lambda j, i, k: k, lambda j, i, k: j)
        return _mm(a, b, grid=(_NKB, m // tm, 2), a_blk=a_blk, a_map=a_map,
                   b_blk=(None, None, _SP_X, _SP_U), b_map=lambda j, i, k: (k, j, 0, 0),
                   o_blk=(tm, _SP_U), o_map=lambda j, i, k: (i, j), out_shape=(m, _NKB * _SP_U), out_dtype=F32,
                   acc_2d=(tm, _SP_U), **kw)
    if kind == "dx":
        o_blk, o_map = xblk(lambda g, i, k: i, lambda g, i, k: g // _NKB, lambda g, i, k: g % _NKB)
        return _mm(a, b, grid=(2 * _NKB, m // tm, 1), a_blk=(tm, _SP_U), a_map=lambda g, i, k: (i, g % _NKB),
                   b_blk=(None, None, _SP_X, _SP_U), b_map=lambda g, i, k: (g // _NKB, g % _NKB, 0, 0),
                   o_blk=o_blk, o_map=o_map, out_shape=(2, m, _NKB * _SP_X), out_dtype=F32, dims=_MM_TB, **kw)
    if kind == "dcd":
        a_blk, a_map = xblk(lambda g, _, k: k, lambda g, _, k: g // _NKB, lambda g, _, k: g % _NKB)
        return _mm(a, b, grid=(2 * _NKB, 1, m // tm), a_blk=a_blk, a_map=a_map,
                   b_blk=(tm, _SP_U), b_map=lambda g, _, k: (k, g % _NKB),
                   o_blk=(None, None, _SP_X, _SP_U), o_map=lambda g, _, k: (g // _NKB, g % _NKB, 0, 0),
                   out_shape=(2, _NKB, _SP_X, _SP_U), out_dtype=F32, dims=_MM_TA, acc_2d=(_SP_X, _SP_U), **kw)
    if kind == "du":
        a_blk, a_map = xblk(lambda j, i, k: i, lambda j, i, k: k, lambda j, i, k: j)
        return _mm(a, b, grid=(_NKB, m // tm, 2), a_blk=a_blk, a_map=a_map,
                   b_blk=(None, None, _SP_U, _SP_X), b_map=lambda j, i, k: (k, j, 0, 0),
                   o_blk=(tm, _SP_U), o_map=lambda j, i, k: (i, j), out_shape=(m, _NKB * _SP_U), out_dtype=F32,
                   dims=_MM_TB, acc_2d=(tm, _SP_U), res=res, res_blk=(tm, _SP_U), res_map=lambda j, i, k: (i, j), **kw)
    assert kind == "dbd"
    b_blk, b_map = xblk(lambda g, _, k: k, lambda g, _, k: g // _NKB, lambda g, _, k: g % _NKB)
    return _mm(a, b, grid=(2 * _NKB, 1, m // tm), a_blk=(tm, _SP_U), a_map=lambda g, _, k: (k, g % _NKB),
               b_blk=b_blk, b_map=b_map,
               o_blk=(None, None, _SP_U, _SP_X), o_map=lambda g, _, k: (g // _NKB, g % _NKB, 0, 0),
               out_shape=(2, _NKB, _SP_U, _SP_X), out_dtype=F32, dims=_MM_TA, acc_2d=(_SP_U, _SP_X), **kw)


def _block_diag(w):
    g, a, b = w.shape
    eye = jnp.eye(SSM_PACK, dtype=w.dtype)
    wp = w.reshape(g // SSM_PACK, SSM_PACK, a, b)
    return jnp.einsum("kgab,gh->kgahb", wp, eye).reshape(g // SSM_PACK, SSM_PACK * a, SSM_PACK * b)


def _block_diag_t(d, a, b):
    k = d.shape[0]
    eye = jnp.eye(SSM_PACK, dtype=d.dtype)
    dp = d.reshape(k, SSM_PACK, a, SSM_PACK, b)
    return jnp.einsum("kgahb,gh->kgab", dp, eye).reshape(k * SSM_PACK, a, b)


def _coords():
    return lax.axis_index("x"), lax.axis_index("y"), lax.axis_index("c")


def all_gather(tensors, name, scatter=None):
    n = len(tensors)
    ns = 0 if scatter is None else 1
    any_spec = pl.BlockSpec(memory_space=pl.ANY)

    def body(*refs):
        ins, outs = refs[:n], refs[n + ns:2 * n + ns]
        send, recv, local = refs[2 * (n + ns):2 * (n + ns) + 3]
        x, y, c = _coords()
        me, sibling = (x, y, c), (x, y, 1 - c)
        chips = [(1 - x, y), (x, 1 - y), (1 - x, 1 - y)]

        def slot(p):
            return 4 * p[0] + 2 * p[1] + p[2]

        def copy(t, k, block, to, src=None):
            dst = outs[t].at[slot(block)]
            return pltpu.make_async_remote_copy(
                src_ref=dst if src is None else src, dst_ref=dst, send_sem=send.at[7 * t + k],
                recv_sem=recv.at[7 * t + k], device_id=to, device_id_type=pl.DeviceIdType.MESH)

        own, sent, landing = [], [], []
        for t in range(n):
            mine = pltpu.make_async_copy(ins[t], outs[t].at[slot(me)], local.at[t])
            mine.start()
            own.append(mine)
            first = [copy(t, 0, me, sibling, src=ins[t])]
            first += [copy(t, 1 + j, me, (*chip, c), src=ins[t]) for j, chip in enumerate(chips)]
            for cp in first:
                cp.start()
            sent += first
        if ns:
            src, dst = refs[n], refs[2 * n + ns]
            s_send, s_recv = refs[2 * (n + ns) + 3:]
            my_slot, peers = _me_and_peers()
            mine = pltpu.make_async_copy(src.at[my_slot], dst.at[my_slot], local.at[n])
            mine.start()
            own.append(mine)
            for k, (dev, peer_slot) in enumerate(peers):
                def piece(dst_slot, k=k, dev=dev, peer_slot=peer_slot):
                    return pltpu.make_async_remote_copy(
                        src_ref=src.at[peer_slot], dst_ref=dst.at[dst_slot], send_sem=s_send.at[k],
                        recv_sem=s_recv.at[k], device_id=dev, device_id_type=pl.DeviceIdType.MESH)

                cp = piece(my_slot)
                cp.start()
                sent.append(cp)
                landing.append(piece(peer_slot))
        for t in range(n):
            for j, chip in enumerate(chips):
                copy(t, 1 + j, (*chip, c), me).wait_recv()
                passed = copy(t, 4 + j, (*chip, c), sibling)
                passed.start()
                sent.append(passed)
        for t in range(n):
            copy(t, 0, sibling, me).wait_recv()
            for j, chip in enumerate(chips):
                copy(t, 4 + j, (*chip, 1 - c), me).wait_recv()
        for cp in landing:
            cp.wait_recv()
        for cp in sent:
            cp.wait_send()
        for cp in own:
            cp.wait()

    scratch = [pltpu.SemaphoreType.DMA((7 * n,)), pltpu.SemaphoreType.DMA((7 * n,)), pltpu.SemaphoreType.DMA((n + ns,))]
    out_shape = [jax.ShapeDtypeStruct((NDEV,) + a.shape, a.dtype) for a in tensors]
    args = list(tensors)
    if ns:
        scratch += [pltpu.SemaphoreType.DMA((NDEV - 1,)), pltpu.SemaphoreType.DMA((NDEV - 1,))]
        out_shape.append(jax.ShapeDtypeStruct(scatter.shape, scatter.dtype))
        args.append(scatter)
    return pl.pallas_call(
        body, in_specs=[any_spec] * (n + ns), out_specs=[any_spec] * (n + ns), out_shape=out_shape,
        scratch_shapes=scratch, name=name,
    )(*args)


_HBM_SPEC = pl.BlockSpec(memory_space=pltpu.HBM)
_SEM_SPEC = pl.BlockSpec(memory_space=pltpu.SEMAPHORE)
_NPEER = NDEV - 1


def _split_copy_params():
    return pltpu.CompilerParams(has_side_effects=pltpu.SideEffectType.DATAFLOW_SIDE_EFFECTING)


def _me_and_peers():
    x, y, c = _coords()
    peers = []
    for rel in range(1, NDEV):
        p = (1 - x if rel & 4 else x, 1 - y if rel & 2 else y, 1 - c if rel & 1 else c)
        peers.append((p, 4 * p[0] + 2 * p[1] + p[2]))
    return 4 * x + 2 * y + c, peers


def _hbm(a):
    return pltpu.with_memory_space_constraint(a, pltpu.HBM)


def gather_start(bufs, name):
    n = len(bufs)

    def body(*refs):
        ins, outs = refs[:n], refs[n:]
        me, peers = _me_and_peers()
        for t in range(n):
            for k, (dev, _) in enumerate(peers):
                pltpu.make_async_remote_copy(
                    src_ref=ins[t].at[me], dst_ref=ins[t].at[me], send_sem=outs[3 * t].at[k],
                    recv_sem=outs[3 * t + 1].at[k], device_id=dev, device_id_type=pl.DeviceIdType.MESH).start()
        outs[3 * n][...] = jnp.zeros_like(outs[3 * n])

    out_shape, out_specs = [], []
    for b in bufs:
        out_shape += [pltpu.SemaphoreType.DMA((_NPEER,)), pltpu.SemaphoreType.DMA((_NPEER,)), pltpu.HBM(b.shape, b.dtype)]
        out_specs += [_SEM_SPEC, _SEM_SPEC, _HBM_SPEC]
    out_shape.append(jax.ShapeDtypeStruct((8, LANES), F32))
    out_specs.append(pl.BlockSpec(memory_space=pltpu.VMEM))
    res = pl.pallas_call(
        body, name=name, out_shape=tuple(out_shape), in_specs=[_HBM_SPEC] * n, out_specs=tuple(out_specs),
        input_output_aliases={t: 3 * t + 2 for t in range(n)}, compiler_params=_split_copy_params(),
    )(*[_hbm(b) for b in bufs])
    return [tuple(res[3 * t:3 * t + 3]) for t in range(n)], res[3 * n]


def gather_wait(started, after, name):
    n = len(started)

    def body(*refs):
        bufs, sems = refs[:n], refs[n:3 * n]
        me, peers = _me_and_peers()
        for t in range(n):
            for k, (dev, slot) in enumerate(peers):
                cp = pltpu.make_async_remote_copy(
                    src_ref=bufs[t].at[me], dst_ref=bufs[t].at[slot], send_sem=sems[2 * t].at[k],
                    recv_sem=sems[2 * t + 1].at[k], device_id=dev, device_id_type=pl.DeviceIdType.MESH)
                cp.wait_recv()
                cp.wait_send()

    args = [s[2] for s in started] + [sem for s in started for sem in s[:2]] + [after]
    res = pl.pallas_call(
        body, name=name, out_shape=tuple(pltpu.HBM(s[2].shape, s[2].dtype) for s in started),
        in_specs=[_HBM_SPEC] * n + [_SEM_SPEC] * (2 * n) + [pl.BlockSpec(memory_space=pl.ANY)],
        out_specs=tuple([_HBM_SPEC] * n), input_output_aliases={t: t for t in range(n)},
        compiler_params=_split_copy_params(),
    )(*args)
    return list(res)


def scatter_start(srcs, name):
    n = len(srcs)
    lands = [lax.empty((_NPEER,) + s.shape[1:], s.dtype) for s in srcs]

    def body(*refs):
        ins, land_refs, outs = refs[:n], refs[n:2 * n], refs[2 * n:]
        _, peers = _me_and_peers()
        for t in range(n):
            for k, (dev, slot) in enumerate(peers):
                pltpu.make_async_remote_copy(
                    src_ref=ins[t].at[slot], dst_ref=land_refs[t].at[k], send_sem=outs[4 * t].at[k],
                    recv_sem=outs[4 * t + 1].at[k], device_id=dev, device_id_type=pl.DeviceIdType.MESH).start()
        outs[4 * n][...] = jnp.zeros_like(outs[4 * n])

    out_shape, out_specs = [], []
    for s, land in zip(srcs, lands):
        out_shape += [pltpu.SemaphoreType.DMA((_NPEER,)), pltpu.SemaphoreType.DMA((_NPEER,)),
                      pltpu.HBM(s.shape, s.dtype), pltpu.HBM(land.shape, land.dtype)]
        out_specs += [_SEM_SPEC, _SEM_SPEC, _HBM_SPEC, _HBM_SPEC]
    out_shape.append(jax.ShapeDtypeStruct((8, LANES), F32))
    out_specs.append(pl.BlockSpec(memory_space=pltpu.VMEM))
    aliases = {t: 4 * t + 2 for t in range(n)}
    aliases.update({n + t: 4 * t + 3 for t in range(n)})
    res = pl.pallas_call(
        body, name=name, out_shape=tuple(out_shape), in_specs=[_HBM_SPEC] * (2 * n), out_specs=tuple(out_specs),
        input_output_aliases=aliases, compiler_params=_split_copy_params(),
    )(*[_hbm(s) for s in srcs], *[_hbm(land) for land in lands])
    return [tuple(res[4 * t:4 * t + 4]) for t in range(n)], res[4 * n]


def scatter_wait(started, after, name):
    n = len(started)

    def body(*refs):
        srcs, land_refs, sems = refs[:n], refs[n:2 * n], refs[2 * n:4 * n]
        _, peers = _me_and_peers()
        for t in range(n):
            for k, (dev, slot) in enumerate(peers):
                cp = pltpu.make_async_remote_copy(
                    src_ref=srcs[t].at[slot], dst_ref=land_refs[t].at[k], send_sem=sems[2 * t].at[k],
                    recv_sem=sems[2 * t + 1].at[k], device_id=dev, device_id_type=pl.DeviceIdType.MESH)
                cp.wait_recv()
                cp.wait_send()

    args = [s[2] for s in started] + [s[3] for s in started] + [sem for s in started for sem in s[:2]] + [after]
    res = pl.pallas_call(
        body, name=name,
        out_shape=tuple([pltpu.HBM(s[2].shape, s[2].dtype) for s in started]
                        + [pltpu.HBM(s[3].shape, s[3].dtype) for s in started]),
        in_specs=[_HBM_SPEC] * (2 * n) + [_SEM_SPEC] * (2 * n) + [pl.BlockSpec(memory_space=pl.ANY)],
        out_specs=tuple([_HBM_SPEC] * (2 * n)), input_output_aliases={t: t for t in range(2 * n)},
        compiler_params=_split_copy_params(),
    )(*args)
    return [(res[t], res[n + t]) for t in range(n)]


_PACK_QUANTUM = 8 * LANES


def _pack(parts, lead=0):
    out = []
    for p in parts:
        head = p.shape[:lead]
        f = p.astype(F32).reshape(head + (-1,))
        pad = (-f.shape[-1]) % _PACK_QUANTUM
        if pad:
            f = jnp.concatenate([f, jnp.zeros(head + (pad,), F32)], axis=-1)
        out.append(f.reshape(head + (-1, LANES)))
    return jnp.concatenate(out, axis=lead)


def _unpack(buf, shapes):
    head = buf.shape[:-2]
    out, r = [], 0
    for s in shapes:
        n = 1
        for v in s:
            n *= v
        nr = -(-n // _PACK_QUANTUM) * 8
        flat = buf[..., r:r + nr, :].reshape(head + (nr * LANES,))[..., :n]
        out.append(flat.reshape(head + tuple(s)))
        r += nr
    return out


BIG = ("sb_w_qkv", "sb_w_o", "sg_w_in", "sg_w_o", "ssm_w_in", "ssm_w_glu", "ffn_w_up", "ffn_w_down")
SMALL_SHARDED = ("norm_g", "ssm_d", "ffn_conv_w")
REPLICATED = ("final_norm_g", "sg_norm_g", "sg_w_s", "sg_b", "ssm_lam_re", "ssm_lam_im", "ssm_log_dt",
              "ssm_b_re", "ssm_b_im", "ssm_c_re", "ssm_c_im", "ffn_conv_b")
WEIGHTS = ("norm_g", "final_norm_g", "sb_w_qkv", "sb_w_o", "sg_w_in", "sg_norm_g", "sg_w_s", "sg_b", "sg_w_o",
           "ssm_w_in", "ssm_lam_re", "ssm_lam_im", "ssm_log_dt", "ssm_b_re", "ssm_b_im", "ssm_c_re", "ssm_c_im",
           "ssm_d", "ssm_w_glu", "ffn_w_up", "ffn_conv_w", "ffn_conv_b", "ffn_w_down")


def _step(x, loss_target, w, m, v):
    t, d = x.shape[1], x.shape[2]
    depth = w["norm_g"].shape[0]
    x0 = x.reshape(t, d)
    tgt = loss_target.reshape(t, d)

    mx, my, mc = _coords()
    me = (4 * mx + 2 * my + mc).astype(jnp.int32).reshape(1)
    shard_pack = _pack([w[k] for k in SMALL_SHARDED])
    gathered_small, = all_gather([shard_pack], name="gather_small_weights")
    mixer_weights = (("sb_w_qkv", "sb_w_o"), ("sg_w_in", "sg_w_o"), ("ssm_w_in", "ssm_w_glu"))
    order = []
    for i in range(depth):
        order += [(k, i // 3) for k in mixer_weights[i % 3]] + [("ffn_w_up", i), ("ffn_w_down", i)]
    as_kept = lambda tree, k: jnp.swapaxes(tree[k], 1, 2) if k == "ffn_w_up" else tree[k]
    pending, token = {}, None
    for group in (order[:4], order[4:]):
        started, token = gather_start([cast_into_slot(as_kept(w, k), l, me) for k, l in group], "gather_weights_start")
        pending.update(zip(group, started))
    wg = {}

    def weights(keys, after):
        missing = [key for key in keys if key not in wg]
        if missing:
            for key, buf in zip(missing, gather_wait([pending[key] for key in missing], after, "gather_weights_wait")):
                wg[key] = buf[:, None]
        return [wg[key] for key in keys]

    ng, sd, cw = _unpack(gathered_small, [w[k].shape for k in SMALL_SHARDED])
    norm_full = jnp.transpose(ng, (1, 2, 0, 3)).reshape(depth, 2, d)
    ssm_d_full = jnp.transpose(sd, (1, 0, 2)).reshape(1, d)
    nc = cw.shape[-1]
    conv_b3 = w["ffn_conv_b"].reshape(depth, NDEV, nc)
    p3 = [jnp.concatenate([cw[:, l], conv_b3[l][:, None, :], jnp.zeros((NDEV, 8 - CONV_K - 1, nc), F32)], axis=1)
          for l in range(depth)]

    g_, p_, h_ = SSM_GROUPS, SSM_STATE, SSM_GROUP
    lam_re, lam_im = w["ssm_lam_re"][0], w["ssm_lam_im"][0]
    log_dt = w["ssm_log_dt"][0].reshape(g_, 1)
    b_re, b_im = w["ssm_b_re"][0].reshape(g_ * p_, h_), w["ssm_b_im"][0].reshape(g_ * p_, h_)
    ar, ai, cre, cim = _single(_disc1, [lam_re, lam_im, log_dt], [(g_, p_)] * 4, "s5_disc1")
    cre_c, cim_c = cre.reshape(g_ * p_, 1), cim.reshape(g_ * p_, 1)
    bbr, bbi = _single(_disc2, [cre_c, cim_c, b_re, b_im], [(g_ * p_, h_)] * 2, "s5_disc2")
    per_group_t = lambda a, r, c: jnp.swapaxes(a.reshape(g_, r, c), 1, 2)
    bd = jnp.stack([_block_diag(per_group_t(bbr, p_, h_)), _block_diag(per_group_t(bbi, p_, h_))])
    cd = jnp.stack([_block_diag(per_group_t(w["ssm_c_re"][0], h_, p_)),
                    -_block_diag(per_group_t(w["ssm_c_im"][0], h_, p_))])
    a2 = jnp.stack([ar.reshape(g_ * p_), ai.reshape(g_ * p_)])

    sg_gain = w["sg_norm_g"]
    sg_ws = w["sg_w_s"][0]
    sg_bfull = jnp.broadcast_to(w["sg_b"][0][:, :, None], sg_ws.shape)

    acts = []
    xc = x0
    xn = rms_fwd(xc, norm_full[0, 0][None], "rms_fwd")
    for i in range(depth):
        mixer, j = i % 3, i // 3
        st = {"x": xc, "xn": xn}
        g0, g1 = norm_full[i, 0][None], norm_full[i, 1][None]
        g_next = norm_full[i + 1, 0][None] if i + 1 < depth else None
        k_in, k_out = [(k, j) for k in mixer_weights[mixer]]
        w_in, = weights([k_in], token if i == 0 else xn)
        if mixer == 0:
            qkv = mm_cs_fwd(xn, w_in, 0, BF16, "qkv_fwd")
            o, ltot = sb_attn_fwd(qkv)
            w_out, = weights([k_out], o)
            x1, xn2 = mm_rs_fwd(o, w_out, 0, xc, F32, "attn_out_fwd", norm=g1)
            st.update(qkv=qkv, o=o, ltot=ltot)
        elif mixer == 1:
            hin = mm_cs_fwd(xn, w_in, 0, BF16, "sg_in_fwd")
            p = sgu_fwd(hin, sg_gain, sg_ws, sg_bfull)
            w_out, = weights([k_out], p)
            x1, xn2 = mm_rs_fwd(p, w_out, 0, xc, F32, "sg_out_fwd", norm=g1)
            st.update(hin=hin, p=p)
        else:
            u = mm_rs_fwd(xn, w_in, 0, None, F32, "ssm_in_fwd")
            u_s = s5_reorder(u, True)
            x2 = s5_scan_fwd(mm_s5("bu", u_s, bd, t, "s5_bu"), a2)
            yc_s = mm_s5("yc", x2, cd, t, "s5_yc")
            yg = s5_post_fwd(s5_reorder(yc_s, False), u, ssm_d_full)
            w_out, = weights([k_out], yg)
            hg = mm_cs_fwd(yg, w_out, 0, BF16, "ssm_glu_fwd")
            x1, xn2 = glu_fwd(hg, xc, g1)
            st.update(u_s=u_s, x2=x2, yc_s=yc_s, yg=yg, hg=hg)
        w_up, w_down = weights([("ffn_w_up", i), ("ffn_w_down", i)], xn2)
        h3 = mm_up_fwd(xn2, w_up, 0, BF16, "ffn_up_fwd")
        gated = ffn_gate_fwd(h3, p3[i])
        if g_next is None:
            xc = mm_down_fwd(gated, w_down, 0, x1, "ffn_down_fwd")
        else:
            xc, xn = mm_down_fwd(gated, w_down, 0, x1, "ffn_down_fwd", norm=g_next)
        st.update(x1=x1, xn2=xn2, h3=h3, gated=gated, g0=g0, g1=g1)
        acts.append(st)

    dx, loss_lanes, d_final_g = loss_head(xc, w["final_norm_g"][None], tgt)
    loss = lax.psum(loss_lanes[0, 0], MESH_AXES)

    scattering = {}
    d_norm = [[None, None] for _ in range(depth)]
    d_p3 = [None] * depth
    rep = {}
    d_ssm_d = None
    token = None

    def scatter(grads_by_key):
        keys = list(grads_by_key)
        started, tok = scatter_start([grads_by_key[key] for key in keys], "scatter_grads_start")
        scattering[tuple(keys)] = started
        return tok

    for i in reversed(range(depth)):
        mixer, j = i % 3, i // 3
        st = acts[i]
        k_in, k_out = [(k, j) for k in mixer_weights[mixer]]
        w_in, w_out, w_up, w_down = weights([k_in, k_out, ("ffn_w_up", i), ("ffn_w_down", i)], None)
        dgated = mm_down_da(dx, w_down, 0, "ffn_down_da", dep=token)
        g_down = mm_down_dw(st["gated"], dx, "ffn_down_dw")
        dy_a, dy_g, dp_a, dp_g = ffn_gate_bwd(st["h3"], dgated, p3[i])
        d_p3[i] = jnp.concatenate([dp_a, dp_g], axis=0)
        dh3 = ffn_conv_t(dy_a, dy_g, p3[i])
        dxn2 = mm_up_da(dh3, w_up, 0, "ffn_up_da")
        g_up = mm_up_dw(st["xn2"], dh3, "ffn_up_dw")
        dx1, d_norm[i][1] = rms_bwd(st["x1"], st["g1"], dxn2, dx, "rms_bwd")
        token = scatter({("ffn_w_down", i): g_down, ("ffn_w_up", i): g_up})
        if mixer == 0:
            do = mm_rs_da(dx1, w_out, 0, BF16, "attn_out_da", dep=token)
            g_out = mm_rs_dw(st["o"], dx1, "attn_out_dw")
            d3 = sb_attn_bwd(st["qkv"], st["ltot"], do)
            g_in = mm_qkv_dw(st["xn"], d3, w_in.shape[3], "qkv_dw")
            token = scatter({k_in: g_in, k_out: g_out})
            dxn = mm_qkv_da(d3, w_in, 0, "qkv_da", dep=token)
        elif mixer == 1:
            dp = mm_rs_da(dx1, w_out, 0, BF16, "sg_out_da", dep=token)
            g_out = mm_rs_dw(st["p"], dx1, "sg_out_dw")
            dhin, d_ws, d_bfull, d_gain = sgu_bwd(st["hin"], dp, sg_gain, sg_ws, sg_bfull)
            rep.update(sg_w_s=d_ws[None], sg_b=d_bfull[None, :, :, 0], sg_norm_g=d_gain)
            dxn = mm_cs_da(dhin, w_in, 0, t, "sg_in_da")
            g_in = mm_cs_dw(st["xn"], dhin, w_in.shape[3], "sg_in_dw")
        else:
            dhg = glu_bwd(st["hg"], dx1, token)
            dyg = mm_cs_da(dhg, w_out, 0, t, "ssm_glu_da")
            g_out = mm_cs_dw(st["yg"], dhg, w_out.shape[3], "ssm_glu_dw")
            dyc_s, du_skip_s, d_ssm_d = s5_post_bwd(st["yc_s"], st["u_s"], ssm_d_full, s5_reorder(dyg, True))
            dx2 = mm_s5("dx", dyc_s, cd, t, "s5_dx")
            dcd = mm_s5("dcd", st["x2"], dyc_s, t, "s5_dcd")
            g2, da2 = s5_scan_bwd(dx2, st["x2"], a2)
            du = s5_reorder(mm_s5("du", g2, bd, t, "s5_du", res=du_skip_s), False)
            dbd = mm_s5("dbd", st["u_s"], g2, t, "s5_dbd")
            from_bd = lambda blk: jnp.swapaxes(_block_diag_t(blk, h_, p_), 1, 2).reshape(g_ * p_, h_)

            def disc2_bwd(c1, c2, b1, b2, t1, t2):
                return jax.vjp(_disc2, c1, c2, b1, b2)[1]((t1, t2))

            d_cre, d_cim, d_b_re, d_b_im = _single(
                disc2_bwd, [cre_c, cim_c, b_re, b_im, from_bd(dbd[0]), from_bd(dbd[1])],
                [(g_ * p_, 1)] * 2 + [(g_ * p_, h_)] * 2, "s5_disc2_bwd")

            def disc1_bwd(l1, l2, ld, t1, t2, t3, t4):
                return jax.vjp(_disc1, l1, l2, ld)[1]((t1, t2, t3, t4))

            d_lam_re, d_lam_im, d_log_dt = _single(
                disc1_bwd, [lam_re, lam_im, log_dt, da2[0].reshape(g_, p_), da2[1].reshape(g_, p_),
                            d_cre.reshape(g_, p_), d_cim.reshape(g_, p_)],
                [(g_, p_), (g_, p_), (g_, 1)], "s5_disc1_bwd")
            from_cd = lambda blk: jnp.swapaxes(_block_diag_t(blk, p_, h_), 1, 2)
            rep.update(ssm_lam_re=d_lam_re[None], ssm_lam_im=d_lam_im[None], ssm_log_dt=d_log_dt.reshape(1, g_),
                       ssm_b_re=d_b_re.reshape(1, g_, p_, h_), ssm_b_im=d_b_im.reshape(1, g_, p_, h_),
                       ssm_c_re=from_cd(dcd[0])[None], ssm_c_im=-from_cd(dcd[1])[None])
            dxn = mm_rs_da(du, w_in, 0, F32, "ssm_in_da")
            g_in = mm_rs_dw(st["xn"], du, "ssm_in_dw")
        dx, d_norm[i][0] = rms_bwd(st["x"], st["g0"], dxn, dx1, "rms_bwd")
        if mixer != 0:
            token = scatter({k_in: g_in, k_out: g_out})

    rep["final_norm_g"] = d_final_g.reshape(d)
    rep["ffn_conv_b"] = jnp.stack([d_p3[l][:, CONV_K, :].reshape(NDEV * nc) for l in range(depth)])

    d_norm_full = jnp.stack([jnp.concatenate(pair, axis=0) for pair in d_norm])
    d_norm_pieces = jnp.transpose(d_norm_full.reshape(depth, 2, NDEV, d // NDEV), (2, 0, 1, 3))
    d_ssm_d_pieces = jnp.transpose(d_ssm_d.reshape(1, NDEV, d // NDEV), (1, 0, 2))
    d_conv_w_pieces = jnp.stack([d_p3[l][:, :CONV_K, :] for l in range(depth)], axis=1)
    small_pieces = _pack([d_norm_pieces, d_ssm_d_pieces, d_conv_w_pieces], lead=1)
    rep_parts, small_received = all_gather([_pack([rep[k] for k in REPLICATED]).astype(BF16)],
                                           name="exchange_small_grads", scatter=small_pieces)
    own, landed = {}, {}
    for keys, started in scattering.items():
        for key, (src, land) in zip(keys, scatter_wait(started, dx, "scatter_grads_wait")):
            own[key], landed[key] = src, land

    grads, deltas, new_m, new_v = {}, {}, {}, {}
    for k in BIG:
        layers = range(w[k].shape[0])
        res = adamw_layers(as_kept(w, k), as_kept(m, k), as_kept(v, k), [landed[(k, l)] for l in layers],
                           [own[(k, l)] for l in layers], me, "adamw")
        grads[k], deltas[k], new_m[k], new_v[k] = [jnp.swapaxes(r, 1, 2) if k == "ffn_w_up" else r for r in res]
    for names, parts in ((SMALL_SHARDED, small_received), (REPLICATED, rep_parts)):
        res = adamw(_pack([w[k] for k in names]), _pack([m[k] for k in names]), _pack([v[k] for k in names]), parts,
                    "adamw_small")
        for tree, buf in zip((grads, deltas, new_m, new_v), res):
            for k, val in zip(names, _unpack(buf, [w[k].shape for k in names])):
                tree[k] = val
    grad_x = dx.reshape(x.shape)
    return (loss, grad_x, *[grads[k] for k in WEIGHTS], *[deltas[k] for k in WEIGHTS],
            *[new_m[k] for k in WEIGHTS], *[new_v[k] for k in WEIGHTS])


def kernel(x, norm_g, final_norm_g, sb_w_qkv, sb_w_o, sg_w_in, sg_norm_g, sg_w_s, sg_b, sg_w_o, ssm_w_in, ssm_lam_re, ssm_lam_im, ssm_log_dt, ssm_b_re, ssm_b_im, ssm_c_re, ssm_c_im, ssm_d, ssm_w_glu, ffn_w_up, ffn_conv_w, ffn_conv_b, ffn_w_down, loss_target, m_norm_g, m_final_norm_g, m_sb_w_qkv, m_sb_w_o, m_sg_w_in, m_sg_norm_g, m_sg_w_s, m_sg_b, m_sg_w_o, m_ssm_w_in, m_ssm_lam_re, m_ssm_lam_im, m_ssm_log_dt, m_ssm_b_re, m_ssm_b_im, m_ssm_c_re, m_ssm_c_im, m_ssm_d, m_ssm_w_glu, m_ffn_w_up, m_ffn_conv_w, m_ffn_conv_b, m_ffn_w_down, v_norm_g, v_final_norm_g, v_sb_w_qkv, v_sb_w_o, v_sg_w_in, v_sg_norm_g, v_sg_w_s, v_sg_b, v_sg_w_o, v_ssm_w_in, v_ssm_lam_re, v_ssm_lam_im, v_ssm_log_dt, v_ssm_b_re, v_ssm_b_im, v_ssm_c_re, v_ssm_c_im, v_ssm_d, v_ssm_w_glu, v_ffn_w_up, v_ffn_conv_w, v_ffn_conv_b, v_ffn_w_down):
    w = dict(zip(WEIGHTS, (norm_g, final_norm_g, sb_w_qkv, sb_w_o, sg_w_in, sg_norm_g, sg_w_s, sg_b, sg_w_o, ssm_w_in,
                           ssm_lam_re, ssm_lam_im, ssm_log_dt, ssm_b_re, ssm_b_im, ssm_c_re, ssm_c_im, ssm_d, ssm_w_glu,
                           ffn_w_up, ffn_conv_w, ffn_conv_b, ffn_w_down)))
    m = dict(zip(WEIGHTS, (m_norm_g, m_final_norm_g, m_sb_w_qkv, m_sb_w_o, m_sg_w_in, m_sg_norm_g, m_sg_w_s, m_sg_b,
                           m_sg_w_o, m_ssm_w_in, m_ssm_lam_re, m_ssm_lam_im, m_ssm_log_dt, m_ssm_b_re, m_ssm_b_im,
                           m_ssm_c_re, m_ssm_c_im, m_ssm_d, m_ssm_w_glu, m_ffn_w_up, m_ffn_conv_w, m_ffn_conv_b,
                           m_ffn_w_down)))
    v = dict(zip(WEIGHTS, (v_norm_g, v_final_norm_g, v_sb_w_qkv, v_sb_w_o, v_sg_w_in, v_sg_norm_g, v_sg_w_s, v_sg_b,
                           v_sg_w_o, v_ssm_w_in, v_ssm_lam_re, v_ssm_lam_im, v_ssm_log_dt, v_ssm_b_re, v_ssm_b_im,
                           v_ssm_c_re, v_ssm_c_im, v_ssm_d, v_ssm_w_glu, v_ffn_w_up, v_ffn_conv_w, v_ffn_conv_b,
                           v_ffn_w_down)))
    return _step(x, loss_target, w, m, v)
```

```python
import functools

import jax
import jax.numpy as jnp
from jax import lax
from jax.experimental import pallas as pl
from jax.experimental.pallas import tpu as pltpu

F32, BF16 = jnp.float32, jnp.bfloat16
MESH_AXES = ("x", "y", "c")
NDEV = 8
EPS = 1e-6
HEAD_DIM = 64
LANES = 128
ATT_BQ, ATT_BK = 2048, 256
CHUNK = 128
SG_GROUPS = 8
SSM_GROUPS, SSM_STATE, SSM_GROUP = 64, 64, 16
SSM_PACK = 8
S5_PASSES = 1
CONV_K = 3
HALO = 16
ROW_BLOCK = 512
SCAN_COLS = 256
ADAM_LR, ADAM_B1, ADAM_B2, ADAM_EPS, ADAM_WD, ADAM_STEP = 0.001, 0.9, 0.999, 1e-08, 0.01, 10
VMEM_LIMIT = 56 * 1024 * 1024

_MM = (((1,), (0,)), ((), ()))
_MM_TB = (((1,), (1,)), ((), ()))
_MM_TA = (((0,), (0,)), ((), ()))


def _params(sem):
    return pltpu.CompilerParams(dimension_semantics=sem, vmem_limit_bytes=VMEM_LIMIT)


def _rows(total, cap, mult=16):
    best = None
    for d in range(mult, min(total, cap) + 1, mult):
        if total % d == 0:
            best = d
    return best if best is not None else total


def _dot(a, b, dims, passes):
    if passes == 1:
        return lax.dot_general(a.astype(BF16), b.astype(BF16), dims, preferred_element_type=F32)
    a = a.astype(F32)
    b = b.astype(F32)
    ah = a.astype(BF16)
    bh = b.astype(BF16)
    al = (a - ah.astype(F32)).astype(BF16)
    bl = (b - bh.astype(F32)).astype(BF16)
    out = lax.dot_general(ah, bh, dims, preferred_element_type=F32)
    out = out + lax.dot_general(al, bh, dims, preferred_element_type=F32)
    return out + lax.dot_general(ah, bl, dims, preferred_element_type=F32)


def _mm(a, b, *, grid, a_blk, a_map, b_blk, b_map, o_blk, o_map, out_shape, out_dtype, name,
        dims=_MM, passes=1, res=None, res_blk=None, res_map=None, b_2d=None, acc_2d=None, dep=None, norm=None):
    nk = grid[2]
    has_res, has_norm = res is not None, norm is not None
    a_maps = list(a_map) if isinstance(a_map, (list, tuple)) else [a_map]
    b_maps = list(b_map) if isinstance(b_map, (list, tuple)) else [b_map]
    na, nb = len(a_maps), len(b_maps)
    n_in = na + nb + has_res + has_norm + (dep is not None)

    def body(*refs):
        o_ref = refs[n_in]
        r_ref = refs[na + nb] if has_res else None
        av = refs[0][...] if na == 1 else jnp.concatenate([r[...] for r in refs[:na]], axis=-1)
        bv = refs[na][...] if nb == 1 else jnp.concatenate([r[...] for r in refs[na:na + nb]], axis=-1)
        if b_2d is not None:
            bv = bv.reshape(b_2d)
        part = _dot(av, bv, dims, passes)

        def finish(total):
            if has_res:
                total = total + r_ref[...].astype(F32)
            o_ref[...] = total.reshape(o_ref.shape).astype(o_ref.dtype)
            if has_norm:
                refs[n_in + 1][...] = _rms(total, refs[na + nb + has_res][...]).astype(BF16)

        if nk == 1:
            finish(part)
        else:
            acc_ref = refs[-1]
            k = pl.program_id(2)

            @pl.when(k == 0)
            def _():
                acc_ref[...] = part

            @pl.when(k > 0)
            def _():
                acc_ref[...] += part

            @pl.when(k == nk - 1)
            def _():
                finish(acc_ref[...])

    in_specs = [pl.BlockSpec(a_blk, f) for f in a_maps] + [pl.BlockSpec(b_blk, f) for f in b_maps]
    args = [a] * na + [b] * nb
    if has_res:
        in_specs.append(pl.BlockSpec(res_blk, res_map))
        args.append(res)
    if has_norm:
        in_specs.append(pl.BlockSpec(norm.shape, lambda *_: (0, 0)))
        args.append(norm)
    if dep is not None:
        in_specs.append(pl.BlockSpec(memory_space=pl.ANY))
        args.append(dep)
    scratch = [pltpu.VMEM(acc_2d, F32)] if nk > 1 else []
    out_specs, out_shapes = pl.BlockSpec(o_blk, o_map), jax.ShapeDtypeStruct(out_shape, out_dtype)
    if has_norm:
        out_specs, out_shapes = [out_specs] * 2, [out_shapes, jax.ShapeDtypeStruct(out_shape, BF16)]
    return pl.pallas_call(
        body, grid=grid, in_specs=in_specs, out_specs=out_specs, out_shape=out_shapes, scratch_shapes=scratch,
        name=name, compiler_params=_params(("parallel", "parallel", "arbitrary")),
    )(*args)


def _cs_act_spec(ns, tm, row_of, col_of):
    if ns % LANES == 0:
        return (tm, ns), lambda *g: (row_of(*g), col_of(*g))
    return (None, tm, ns), lambda *g: (col_of(*g), row_of(*g), 0)


def mm_cs_fwd(a, w4, l, out_dtype, name, tm=2048):
    m, k = a.shape
    tm = min(tm, m)
    ns = w4.shape[3]
    o_blk, o_map = _cs_act_spec(ns, tm, lambda j, i, kk: i, lambda j, i, kk: j)
    out_shape = (m, NDEV * ns) if ns % LANES == 0 else (NDEV, m, ns)
    return _mm(a, w4, grid=(NDEV, m // tm, 1), a_blk=(tm, k), a_map=lambda j, i, kk: (i, 0),
               b_blk=(None, None, k, ns), b_map=lambda j, i, kk: (j, l, 0, 0),
               o_blk=o_blk, o_map=o_map, out_shape=out_shape, out_dtype=out_dtype, name=name)


def mm_cs_da(dc, w4, l, m, name, tm=1024):
    k, ns = w4.shape[2], w4.shape[3]
    tm = min(tm, m)
    a_blk, a_map = _cs_act_spec(ns, tm, lambda i, _, j: i, lambda i, _, j: j)
    return _mm(dc, w4, grid=(m // tm, 1, NDEV), a_blk=a_blk, a_map=a_map,
               b_blk=(None, None, k, ns), b_map=lambda i, _, j: (j, l, 0, 0),
               o_blk=(tm, k), o_map=lambda i, _, j: (i, 0), out_shape=(m, k), out_dtype=F32,
               dims=_MM_TB, acc_2d=(tm, k), name=name)


def mm_cs_dw(a, dc, ns, name, tk=2048):
    m, k = a.shape
    tk = min(tk, m)
    b_blk, b_map = _cs_act_spec(ns, tk, lambda j, _, kk: kk, lambda j, _, kk: j)
    return _mm(a, dc, grid=(NDEV, 1, m // tk), a_blk=(tk, k), a_map=lambda j, _, kk: (kk, 0),
               b_blk=b_blk, b_map=b_map, o_blk=(None, k, ns), o_map=lambda j, _, kk: (j, 0, 0),
               out_shape=(NDEV, k, ns), out_dtype=BF16, dims=_MM_TA, acc_2d=(k, ns), name=name)


def mm_up_fwd(a, wt4, l, out_dtype, name, tm=2048):
    m, k = a.shape
    tm = min(tm, m)
    ns = wt4.shape[2]
    return _mm(a, wt4, grid=(NDEV, m // tm, 1), a_blk=(tm, k), a_map=lambda j, i, kk: (i, 0),
               b_blk=(None, None, ns, k), b_map=lambda j, i, kk: (j, l, 0, 0), dims=_MM_TB,
               o_blk=(None, tm, ns), o_map=lambda j, i, kk: (j, i, 0), out_shape=(NDEV, m, ns), out_dtype=out_dtype,
               name=name)


def mm_up_da(dc3, wt4, l, name, tm=1024):
    _, m, ns = dc3.shape
    tm = min(tm, m)
    k = wt4.shape[3]
    return _mm(dc3, wt4, grid=(m // tm, 1, NDEV), a_blk=(None, tm, ns), a_map=lambda i, _, j: (j, i, 0),
               b_blk=(None, None, ns, k), b_map=lambda i, _, j: (j, l, 0, 0),
               o_blk=(tm, k), o_map=lambda i, _, j: (i, 0), out_shape=(m, k), out_dtype=F32, acc_2d=(tm, k), name=name)


def mm_up_dw(a, dc3, name, tk=2048):
    m, k = a.shape
    tk = min(tk, m)
    ns = dc3.shape[2]
    return _mm(dc3, a, grid=(NDEV, 1, m // tk), a_blk=(None, tk, ns), a_map=lambda j, _, kk: (j, kk, 0),
               b_blk=(tk, k), b_map=lambda j, _, kk: (kk, 0), dims=_MM_TA,
               o_blk=(None, ns, k), o_map=lambda j, _, kk: (j, 0, 0), out_shape=(NDEV, ns, k), out_dtype=BF16,
               acc_2d=(ns, k), name=name)


def mm_rs_fwd(a, w4, l, res, out_dtype, name, tm=1024, norm=None):
    m, k = a.shape
    tm = min(tm, m)
    ks, n = w4.shape[2], w4.shape[3]
    return _mm(a, w4, grid=(m // tm, 1, 1), a_blk=(tm, k), a_map=lambda i, _, kk: (i, 0),
               b_blk=(NDEV, None, ks, n), b_map=lambda i, _, kk: (0, l, 0, 0), b_2d=(k, n),
               o_blk=(tm, n), o_map=lambda i, _, kk: (i, 0), out_shape=(m, n), out_dtype=out_dtype,
               res=res, res_blk=(tm, n), res_map=lambda i, _, kk: (i, 0), name=name, norm=norm)


def mm_rs_da(dc, w4, l, out_dtype, name, tm=1024, dep=None):
    m, n = dc.shape
    tm = min(tm, m)
    ks = w4.shape[2]
    k = NDEV * ks
    return _mm(dc, w4, grid=(m // tm, 1, 1), a_blk=(tm, n), a_map=lambda i, _, kk: (i, 0),
               b_blk=(NDEV, None, ks, n), b_map=lambda i, _, kk: (0, l, 0, 0), b_2d=(k, n),
               o_blk=(tm, k), o_map=lambda i, _, kk: (i, 0), out_shape=(m, k), out_dtype=out_dtype,
               dims=_MM_TB, name=name, dep=dep)


def mm_rs_dw(a, dc, name, tk=1024):
    m, k = a.shape
    tk = min(tk, m)
    n = dc.shape[1]
    ks = k // NDEV
    return _mm(a, dc, grid=(1, 1, m // tk), a_blk=(tk, k), a_map=lambda _, __, kk: (kk, 0),
               b_blk=(tk, n), b_map=lambda _, __, kk: (kk, 0),
               o_blk=(NDEV, ks, n), o_map=lambda _, __, kk: (0, 0, 0), out_shape=(NDEV, ks, n),
               out_dtype=BF16, dims=_MM_TA, acc_2d=(k, n), name=name)


def mm_down_fwd(a3, w4, l, res, name, tm=1024, norm=None):
    nj, m, kc = a3.shape
    tm = min(tm, m)
    ks, n = w4.shape[2], w4.shape[3]
    return _mm(a3, w4, grid=(m // tm, 1, nj), a_blk=(None, tm, kc), a_map=lambda i, _, j: (j, i, 0),
               b_blk=(2, None, ks, n), b_map=lambda i, _, j: (j, l, 0, 0), b_2d=(kc, n),
               o_blk=(tm, n), o_map=lambda i, _, j: (i, 0), out_shape=(m, n), out_dtype=F32,
               res=res, res_blk=(tm, n), res_map=lambda i, _, j: (i, 0), acc_2d=(tm, n), name=name, norm=norm)


def mm_down_da(dc, w4, l, name, tm=2048, dep=None):
    m, n = dc.shape
    tm = min(tm, m)
    ks = w4.shape[2]
    kc = 2 * ks
    nj = NDEV // 2
    return _mm(dc, w4, grid=(nj, m // tm, 1), a_blk=(tm, n), a_map=lambda j, i, _: (i, 0),
               b_blk=(2, None, ks, n), b_map=lambda j, i, _: (j, l, 0, 0), b_2d=(kc, n),
               o_blk=(None, tm, kc), o_map=lambda j, i, _: (j, i, 0), out_shape=(nj, m, kc),
               out_dtype=BF16, dims=_MM_TB, name=name, dep=dep)


def mm_down_dw(a3, dc, name, tk=2048):
    nj, m, kc = a3.shape
    tk = min(tk, m)
    n = dc.shape[1]
    return _mm(a3, dc, grid=(nj, 1, m // tk), a_blk=(None, tk, kc), a_map=lambda j, _, kk: (j, kk, 0),
               b_blk=(tk, n), b_map=lambda j, _, kk: (kk, 0),
               o_blk=(2, kc // 2, n), o_map=lambda j, _, kk: (j, 0, 0), out_shape=(NDEV, kc // 2, n),
               out_dtype=BF16, dims=_MM_TA, acc_2d=(kc, n), name=name)


def _qkv_group_maps(d, ns, row_of, piece_of):
    per_arr, per_piece = d // LANES, ns // LANES

    def group_map(q):
        def f(*g):
            grp = piece_of(*g) * per_piece + q
            return grp // per_arr, row_of(*g), grp % per_arr
        return f

    return [group_map(q) for q in range(per_piece)]


def mm_qkv_da(d3, w4, l, name, tm=1024, dep=None):
    _, m, d = d3.shape
    tm = min(tm, m)
    k, ns = w4.shape[2], w4.shape[3]
    return _mm(d3, w4, grid=(m // tm, 1, NDEV),
               a_blk=(None, tm, LANES), a_map=_qkv_group_maps(d, ns, lambda i, _, j: i, lambda i, _, j: j),
               b_blk=(None, None, k, ns), b_map=lambda i, _, j: (j, l, 0, 0),
               o_blk=(tm, k), o_map=lambda i, _, j: (i, 0), out_shape=(m, k), out_dtype=F32,
               dims=_MM_TB, acc_2d=(tm, k), name=name, dep=dep)


def mm_qkv_dw(a, d3, ns, name, tk=2048):
    m, k = a.shape
    tk = min(tk, m)
    d = d3.shape[2]
    return _mm(a, d3, grid=(NDEV, 1, m // tk), a_blk=(tk, k), a_map=lambda j, _, kk: (kk, 0),
               b_blk=(None, tk, LANES), b_map=_qkv_group_maps(d, ns, lambda j, _, kk: kk, lambda j, _, kk: j),
               o_blk=(None, k, ns), o_map=lambda j, _, kk: (j, 0, 0),
               out_shape=(NDEV, k, ns), out_dtype=BF16, dims=_MM_TA, acc_2d=(k, ns), name=name)


def _rowwise(fn, ins, outs, *, tr, name, acc_outs=()):
    rows = next(a.shape[0] if kind == "row" else a.shape[1] for a, kind in ins if kind in ("row", "row3"))
    n_in, n_out = len(ins), len(outs)
    n_read = sum(kind != "dep" for _, kind in ins)

    def body(*refs):
        vals = fn(*[r[...] for r in refs[:n_read]])
        if not isinstance(vals, (tuple, list)):
            vals = (vals,)
        for ref, val in zip(refs[n_in:n_in + n_out], vals[:n_out]):
            ref[...] = val.astype(ref.dtype)
        i = pl.program_id(0)
        for ref, val in zip(refs[n_in + n_out:], vals[n_out:]):
            val = val.astype(ref.dtype)

            @pl.when(i == 0)
            def _(ref=ref, val=val):
                ref[...] = val

            @pl.when(i > 0)
            def _(ref=ref, val=val):
                ref[...] += val

    in_specs = []
    for a, kind in ins:
        if kind == "row":
            in_specs.append(pl.BlockSpec((tr, a.shape[1]), lambda i: (i, 0)))
        elif kind == "row3":
            in_specs.append(pl.BlockSpec((a.shape[0], tr, a.shape[2]), lambda i: (0, i, 0)))
        elif kind == "dep":
            in_specs.append(pl.BlockSpec(memory_space=pl.ANY))
        else:
            in_specs.append(pl.BlockSpec(a.shape, lambda i, nd=a.ndim: (0,) * nd))
    out_specs = [pl.BlockSpec((tr, c), lambda i: (i, 0)) for c, _ in outs]
    out_specs += [pl.BlockSpec(s, lambda i, nd=len(s): (0,) * nd) for s, _ in acc_outs]
    out_shape = [jax.ShapeDtypeStruct((rows, c), dt) for c, dt in outs]
    out_shape += [jax.ShapeDtypeStruct(s, dt) for s, dt in acc_outs]
    res = pl.pallas_call(
        body, grid=(rows // tr,), in_specs=in_specs, out_specs=out_specs, out_shape=out_shape, name=name,
        compiler_params=_params(("arbitrary",) if acc_outs else ("parallel",)),
    )(*[a for a, _ in ins])
    return res


def _rms(x, g):
    return x * lax.rsqrt(jnp.mean(x * x, axis=-1, keepdims=True) + EPS) * g


def cast_into_slot(w, l, me):
    _, r, c = w.shape
    tr = _rows(r, 512)

    def body(me_ref, w_ref, o_ref):
        o_ref[...] = w_ref[...].astype(o_ref.dtype)

    return pl.pallas_call(
        body,
        grid_spec=pltpu.PrefetchScalarGridSpec(
            num_scalar_prefetch=1, grid=(r // tr,),
            in_specs=[pl.BlockSpec((None, tr, c), lambda i, me_ref: (l, i, 0))],
            out_specs=pl.BlockSpec((None, tr, c), lambda i, me_ref: (me_ref[0], i, 0))),
        out_shape=jax.ShapeDtypeStruct((NDEV, r, c), BF16), name="cast_into_slot",
        compiler_params=_params(("parallel",)),
    )(me, w)


def rms_fwd(x, g, name):
    out, = _rowwise(_rms, [(x, "row"), (g, "full")], [(x.shape[1], BF16)], tr=ROW_BLOCK, name=name)
    return out


def rms_bwd(x, g, dy, dres, name):
    def fn(xv, gv, dyv, drv):
        _, vjp = jax.vjp(_rms, xv, gv)
        dx, dg = vjp(dyv.astype(F32))
        return drv + dx, dg

    d = x.shape[1]
    return _rowwise(fn, [(x, "row"), (g, "full"), (dy, "row"), (dres, "row")], [(d, F32)], tr=ROW_BLOCK, name=name,
                    acc_outs=[((1, d), F32)])


def loss_head(x, g, tgt):
    def f(xv, gv, tv):
        err = jnp.square(_rms(xv, gv) - tv)
        return 0.5 * jnp.sum(jnp.mean(err, axis=-1))

    def fn(xv, gv, tv):
        val, (dx, dg) = jax.value_and_grad(f, argnums=(0, 1))(xv, gv, tv)
        return dx, jnp.full((1, LANES), val, F32), dg

    d = x.shape[1]
    return _rowwise(fn, [(x, "row"), (g, "full"), (tgt, "row")], [(d, F32)], tr=ROW_BLOCK, name="loss_head",
                    acc_outs=[((1, LANES), F32), ((1, d), F32)])


def _glu(hg, x):
    half = hg.shape[1] // 2
    return x + hg[:, :half] * jax.nn.sigmoid(hg[:, half:])


def glu_fwd(hg, x, norm):
    def fn(h, xv, g):
        x1 = _glu(h.astype(F32), xv)
        return x1, _rms(x1, g)

    d = x.shape[1]
    return _rowwise(fn, [(hg, "row"), (x, "row"), (norm, "full")], [(d, F32), (d, BF16)], tr=ROW_BLOCK, name="glu_fwd")


def glu_bwd(hg, dx1, dep):
    def fn(h, d):
        _, vjp = jax.vjp(lambda hv: _glu(hv, jnp.zeros_like(d)), h.astype(F32))
        return vjp(d)[0]

    out, = _rowwise(fn, [(hg, "row"), (dx1, "row"), (dep, "dep")], [(hg.shape[1], BF16)], tr=ROW_BLOCK, name="glu_bwd")
    return out


def _s5_post(yc, u, d):
    return jax.nn.gelu(yc + d * u)


def s5_post_fwd(yc, u, d):
    out, = _rowwise(_s5_post, [(yc, "row"), (u, "row"), (d, "full")], [(yc.shape[1], BF16)], tr=ROW_BLOCK,
                    name="s5_post_fwd")
    return out


def s5_post_bwd(yc, u, d, dyg):
    def fn(ycv, uv, dv, g):
        _, vjp = jax.vjp(_s5_post, ycv, uv, dv)
        return vjp(g.astype(F32))

    dm = yc.shape[1]
    return _rowwise(fn, [(yc, "row"), (u, "row"), (d, "full"), (dyg, "row")], [(dm, F32), (dm, F32)], tr=ROW_BLOCK,
                    name="s5_post_bwd", acc_outs=[((1, dm), F32)])


def _adam_update(wv, mv, vv, g):
    m2 = ADAM_B1 * mv + (1.0 - ADAM_B1) * g
    v2 = ADAM_B2 * vv + (1.0 - ADAM_B2) * jnp.square(g)
    m_hat = m2 / (1.0 - ADAM_B1 ** ADAM_STEP)
    v_hat = v2 / (1.0 - ADAM_B2 ** ADAM_STEP)
    delta = -ADAM_LR * (m_hat / (jnp.sqrt(v_hat) + ADAM_EPS) + ADAM_WD * wv)
    return g, delta, m2, v2


def adamw(w, m, v, g_parts, name):
    def fn(wv, mv, vv, gp):
        g = gp[0].astype(F32)
        for p in range(1, gp.shape[0]):
            g = g + gp[p].astype(F32)
        return _adam_update(wv, mv, vv, g)

    c = w.shape[1]
    return _rowwise(fn, [(w, "row"), (m, "row"), (v, "row"), (g_parts, "row3")], [(c, F32)] * 4,
                    tr=_rows(w.shape[0], 256), name=name)


def adamw_layers(w, m, v, lands, owns, me, name):
    nl, r, c = w.shape
    tr = _rows(r, 256)

    def body(me_ref, w_ref, m_ref, v_ref, *rest):
        land_refs, own_refs, out_refs = rest[:nl], rest[nl:2 * nl], rest[2 * nl:]
        for l in range(nl):
            @pl.when(pl.program_id(0) == l)
            def _(l=l):
                g = own_refs[l][...].astype(F32)
                for p in range(NDEV - 1):
                    g = g + land_refs[l][p].astype(F32)
                for ref, val in zip(out_refs, _adam_update(w_ref[...], m_ref[...], v_ref[...], g)):
                    ref[...] = val

    def rows_of(l):
        return lambda li, i, me_ref: jnp.where(li == l, i, 0)

    wspec = pl.BlockSpec((None, tr, c), lambda li, i, me_ref: (li, i, 0))
    in_specs = [wspec] * 3
    in_specs += [pl.BlockSpec((NDEV - 1, tr, c), lambda li, i, me_ref, f=rows_of(l): (0, f(li, i, me_ref), 0))
                 for l in range(nl)]
    in_specs += [pl.BlockSpec((None, tr, c), lambda li, i, me_ref, f=rows_of(l): (me_ref[0], f(li, i, me_ref), 0))
                 for l in range(nl)]
    return pl.pallas_call(
        body,
        grid_spec=pltpu.PrefetchScalarGridSpec(
            num_scalar_prefetch=1, grid=(nl, r // tr), in_specs=in_specs, out_specs=[wspec] * 4),
        out_shape=[jax.ShapeDtypeStruct(w.shape, F32)] * 4, name=name, compiler_params=_params(("parallel", "parallel")),
    )(me, w, m, v, *lands, *owns)


def _conv_rows(cur, halo, p, first):
    r = cur.shape[0]
    ext = jnp.concatenate([jnp.where(first, 0.0, halo), cur], axis=0)
    s1 = pltpu.roll(ext, 1, 0)[HALO:]
    s2 = pltpu.roll(ext, 2, 0)[HALO:]
    return p[0:1] * s2 + p[1:2] * s1 + p[2:3] * cur + p[3:4], s1, s2


def ffn_gate_fwd(h3, p3, tr=ROW_BLOCK):
    _, t, c = h3.shape
    half = NDEV // 2

    def body(a_ref, ah_ref, g_ref, gh_ref, pa_ref, pg_ref, o_ref):
        first = pl.program_id(1) == 0
        ya, _, _ = _conv_rows(a_ref[...].astype(F32), ah_ref[...].astype(F32), pa_ref[...], first)
        yg, _, _ = _conv_rows(g_ref[...].astype(F32), gh_ref[...].astype(F32), pg_ref[...], first)
        o_ref[...] = (jax.nn.silu(yg) * ya).astype(o_ref.dtype)

    main = lambda off: pl.BlockSpec((None, tr, c), lambda j, i: (j + off, i, 0))
    halo = lambda off: pl.BlockSpec((None, HALO, c), lambda j, i: (j + off, jnp.maximum(i * (tr // HALO) - 1, 0), 0))
    par = lambda off: pl.BlockSpec((None, 8, c), lambda j, i: (j + off, 0, 0))
    return pl.pallas_call(
        body, grid=(half, t // tr),
        in_specs=[main(0), halo(0), main(half), halo(half), par(0), par(half)],
        out_specs=pl.BlockSpec((None, tr, c), lambda j, i: (j, i, 0)),
        out_shape=jax.ShapeDtypeStruct((half, t, c), BF16), name="ffn_gate_fwd",
        compiler_params=_params(("parallel", "parallel")),
    )(h3, h3, h3, h3, p3, p3)


def ffn_gate_bwd(h3, dgated3, p3, tr=ROW_BLOCK):
    _, t, c = h3.shape
    half = NDEV // 2

    def body(a_ref, ah_ref, g_ref, gh_ref, dg_ref, pa_ref, pg_ref, dya_ref, dyg_ref, dpa_ref, dpg_ref):
        i = pl.program_id(1)
        first = i == 0
        a = a_ref[...].astype(F32)
        g = g_ref[...].astype(F32)
        ya, a1, a2 = _conv_rows(a, ah_ref[...].astype(F32), pa_ref[...], first)
        yg, g1, g2 = _conv_rows(g, gh_ref[...].astype(F32), pg_ref[...], first)
        d = dg_ref[...].astype(F32)
        sig = jax.nn.sigmoid(yg)
        d_ya = (d * (yg * sig)).astype(dya_ref.dtype)
        d_yg = (d * ya * (sig * (1.0 + yg * (1.0 - sig)))).astype(dyg_ref.dtype)
        dya_ref[...] = d_ya
        dyg_ref[...] = d_yg
        for dy, cur, s1, s2, dp_ref in ((d_ya.astype(F32), a, a1, a2, dpa_ref), (d_yg.astype(F32), g, g1, g2, dpg_ref)):
            rows = [jnp.sum(dy * s2, axis=0, keepdims=True), jnp.sum(dy * s1, axis=0, keepdims=True),
                    jnp.sum(dy * cur, axis=0, keepdims=True), jnp.sum(dy, axis=0, keepdims=True)]
            dp = jnp.concatenate(rows + [jnp.zeros((4, c), F32)], axis=0)

            @pl.when(first)
            def _(dp_ref=dp_ref, dp=dp):
                dp_ref[...] = dp

            @pl.when(i > 0)
            def _(dp_ref=dp_ref, dp=dp):
                dp_ref[...] += dp

    main = lambda off: pl.BlockSpec((None, tr, c), lambda j, i: (j + off, i, 0))
    halo = lambda off: pl.BlockSpec((None, HALO, c), lambda j, i: (j + off, jnp.maximum(i * (tr // HALO) - 1, 0), 0))
    par = lambda off: pl.BlockSpec((None, 8, c), lambda j, i: (j + off, 0, 0))
    return pl.pallas_call(
        body, grid=(half, t // tr),
        in_specs=[main(0), halo(0), main(half), halo(half), main(0), par(0), par(half)],
        out_specs=[main(0), main(0), par(0), par(0)],
        out_shape=[jax.ShapeDtypeStruct((half, t, c), BF16)] * 2 + [jax.ShapeDtypeStruct((half, 8, c), F32)] * 2,
        name="ffn_gate_bwd", compiler_params=_params(("parallel", "arbitrary")),
    )(h3, h3, h3, h3, dgated3, p3, p3)


def ffn_conv_t(dy_a, dy_g, p3, tr=2 * ROW_BLOCK):
    half, t, c = dy_a.shape
    tr = min(tr, t)
    nblk = t // tr

    def body(a_ref, ah_ref, g_ref, gh_ref, p_ref, o_ref):
        is_a = pl.program_id(0) < half
        last = pl.program_id(1) == nblk - 1
        cur = jnp.where(is_a, a_ref[...], g_ref[...]).astype(F32)
        nxt = jnp.where(is_a, ah_ref[...], gh_ref[...]).astype(F32)
        ext = jnp.concatenate([cur, jnp.where(last, 0.0, nxt)], axis=0)
        n = tr + HALO
        s1 = pltpu.roll(ext, n - 1, 0)[:tr]
        s2 = pltpu.roll(ext, n - 2, 0)[:tr]
        p = p_ref[...]
        o_ref[...] = (p[2:3] * cur + p[1:2] * s1 + p[0:1] * s2).astype(o_ref.dtype)

    main = pl.BlockSpec((None, tr, c), lambda j, i: (j % half, i, 0))
    halo = pl.BlockSpec((None, HALO, c), lambda j, i: (j % half, jnp.minimum((i + 1) * (tr // HALO), t // HALO - 1), 0))
    return pl.pallas_call(
        body, grid=(NDEV, nblk),
        in_specs=[main, halo, main, halo, pl.BlockSpec((None, 8, c), lambda j, i: (j, 0, 0))],
        out_specs=pl.BlockSpec((None, tr, c), lambda j, i: (j, i, 0)),
        out_shape=jax.ShapeDtypeStruct((NDEV, t, c), BF16), name="ffn_conv_t",
        compiler_params=_params(("parallel", "parallel")),
    )(dy_a, dy_a, dy_g, dy_g, p3)


def _att_consts(bq, bk):
    lane = lax.broadcasted_iota(jnp.int32, (1, LANES), 1)
    heads = (lane < HEAD_DIM, lane >= HEAD_DIM)
    rr = lax.broadcasted_iota(jnp.int32, (bq, bk), 0)
    cc = lax.broadcasted_iota(jnp.int32, (bq, bk), 1)
    kr = lax.broadcasted_iota(jnp.int32, (bk, bk), 0)
    kc = lax.broadcasted_iota(jnp.int32, (bk, bk), 1)
    return heads, rr, cc, kr, kc


def _split_dot(x, tri, parts):
    out = None
    for _ in range(parts):
        piece = x.astype(BF16)
        x = x - piece.astype(F32)
        term = jnp.dot(piece, tri, preferred_element_type=F32)
        out = term if out is None else out + term
    return out


def _att_logits(qh, k):
    z = lax.dot_general(qh, k, _MM_TB, preferred_element_type=F32)
    lsp = jnp.minimum(z, 0.0) - jnp.log(1.0 + jnp.exp(-jnp.abs(z)))
    return lsp, lsp - z


def _per_head(heads, a, b):
    return jnp.where(heads[0], a, b)


def sb_attn_fwd(qkv):
    t, d3 = qkv.shape
    d = d3 // 3
    npair = d // LANES
    bq, bk = min(ATT_BQ, t), min(ATT_BK, t)
    kpq = bq // bk

    def body(q_ref, k_ref, v_ref, o_ref, lt_ref, acc_ref):
        heads, rr, cc, kr, kc = _att_consts(bq, bk)
        suffix = (kr > kc).astype(BF16)

        def trip(qh, k0, r0, runs):
            k = k_ref[pl.ds(k0, bk), :]
            v = v_ref[pl.ds(k0, bk), :]
            diag, r0 = r0 is not None, r0 or 0
            valid = cc[:bq - r0] < rr[:bq - r0]
            new_runs = []
            for h in range(2):
                lsp, lraw = _att_logits(qh[h][r0:], k)
                lm = jnp.where(valid, lraw, 0.0) if diag else lraw
                w = jnp.exp(lsp + _split_dot(lm, suffix, 2) + runs[h][r0:])
                if diag:
                    w = jnp.where(valid, w, 0.0)
                acc_ref[h, r0:, :] += jnp.dot(w.astype(BF16), v, preferred_element_type=F32)
                below = runs[h][r0:] + jnp.sum(lm, axis=1, keepdims=True)
                new_runs.append(jnp.concatenate([runs[h][:r0], below], axis=0) if r0 else below)
            return tuple(new_runs)

        def q_loop(qb, _):
            q0 = pl.multiple_of(qb * bq, bq)
            q = q_ref[pl.ds(q0, bq), :] * 0.125
            qh = [jnp.where(hm, q, 0.0).astype(BF16) for hm in heads]
            acc_ref[...] = jnp.zeros_like(acc_ref)
            runs = (jnp.zeros((bq, 1), F32),) * 2
            for dblk in reversed(range(kpq)):
                runs = trip(qh, pl.multiple_of(q0 + dblk * bk, bk), dblk * bk, runs)
            nleft = qb * kpq
            runs = lax.fori_loop(
                0, nleft, lambda i, r: trip(qh, pl.multiple_of((nleft - 1 - i) * bk, bk), None, r), runs)
            o_ref[pl.ds(q0, bq), :] = _per_head(heads, acc_ref[0], acc_ref[1])
            lt_ref[pl.ds(q0, bq), :] = _per_head(heads, runs[0], runs[1])
            return 0

        lax.fori_loop(0, t // bq, q_loop, 0)

    col = lambda off: pl.BlockSpec((t, LANES), lambda p: (0, p + off))
    return pl.pallas_call(
        body, grid=(npair,), in_specs=[col(0), col(npair), col(2 * npair)], out_specs=[col(0), col(0)],
        out_shape=[jax.ShapeDtypeStruct((t, d), F32)] * 2, scratch_shapes=[pltpu.VMEM((2, bq, LANES), F32)],
        name="sb_attn_fwd", compiler_params=_params(("parallel",)),
    )(qkv, qkv, qkv)


def sb_attn_bwd(qkv, ltot, do):
    t, d3 = qkv.shape
    d = d3 // 3
    npair = d // LANES
    bq, bk = min(ATT_BQ, t), min(ATT_BK, t)
    kpq = bq // bk

    def body(q_ref, k_ref, v_ref, lt_ref, do_ref, d_ref, dk_acc, dv_acc, dq_acc):
        heads, rr, cc, kr, kc = _att_consts(bq, bk)
        prefix_incl = (kr <= kc).astype(BF16)
        prefix_excl = (kr < kc).astype(BF16)
        dk_acc[...] = jnp.zeros_like(dk_acc)
        dv_acc[...] = jnp.zeros_like(dv_acc)

        def trip(qh, doh, lt, k0, r0, carry):
            lruns, gruns = carry
            k = k_ref[pl.ds(k0, bk), :]
            v = v_ref[pl.ds(k0, bk), :]
            diag, r0 = r0 is not None, r0 or 0
            valid = cc[:bq - r0] < rr[:bq - r0]
            new_lruns, new_gruns = [], []
            dk_blk = jnp.zeros((bk, LANES), F32)
            dv_blk = jnp.zeros((bk, LANES), F32)
            for h in range(2):
                q_rows, do_rows = qh[h][r0:], doh[h][r0:]
                lsp, lraw = _att_logits(q_rows, k)
                lm = jnp.where(valid, lraw, 0.0) if diag else lraw
                right = lt[h][r0:] - (lruns[h][r0:] + _split_dot(lm, prefix_incl, 2))
                w = jnp.exp(lsp + right)
                if diag:
                    w = jnp.where(valid, w, 0.0)
                g = lax.dot_general(do_rows, v, _MM_TB, preferred_element_type=F32) * w
                left = gruns[h][r0:] + _split_dot(g, prefix_excl, 1)
                dz = g * jnp.exp(lraw) - jnp.exp(lsp) * left
                if diag:
                    dz = jnp.where(valid, dz, 0.0)
                dz = dz.astype(BF16)
                kh = jnp.where(heads[h], k, 0.0).astype(BF16)
                dq_acc[h, r0:, :] += jnp.dot(dz, kh, preferred_element_type=F32)
                dk_blk = dk_blk + lax.dot_general(dz, q_rows, _MM_TA, preferred_element_type=F32)
                dv_blk = dv_blk + lax.dot_general(w.astype(BF16), do_rows, _MM_TA, preferred_element_type=F32)
                l_below = lruns[h][r0:] + jnp.sum(lm, axis=1, keepdims=True)
                g_below = gruns[h][r0:] + jnp.sum(g, axis=1, keepdims=True)
                new_lruns.append(jnp.concatenate([lruns[h][:r0], l_below], axis=0) if r0 else l_below)
                new_gruns.append(jnp.concatenate([gruns[h][:r0], g_below], axis=0) if r0 else g_below)
            dk_acc[pl.ds(k0, bk), :] += dk_blk
            dv_acc[pl.ds(k0, bk), :] += dv_blk
            return tuple(new_lruns), tuple(new_gruns)

        def q_loop(qb, _):
            q0 = pl.multiple_of(qb * bq, bq)
            q = q_ref[pl.ds(q0, bq), :] * 0.125
            dout = do_ref[pl.ds(q0, bq), :]
            lt2 = lt_ref[pl.ds(q0, bq), :]
            qh = [jnp.where(hm, q, 0.0).astype(BF16) for hm in heads]
            doh = [jnp.where(hm, dout, 0.0).astype(BF16) for hm in heads]
            lt = [jnp.max(jnp.where(hm, lt2, -jnp.inf), axis=1, keepdims=True) for hm in heads]
            dq_acc[...] = jnp.zeros_like(dq_acc)
            col = (jnp.zeros((bq, 1), F32),) * 2
            carry = lax.fori_loop(
                0, qb * kpq, lambda kb, c: trip(qh, doh, lt, pl.multiple_of(kb * bk, bk), None, c), (col, col))
            for dblk in range(kpq):
                carry = trip(qh, doh, lt, pl.multiple_of(q0 + dblk * bk, bk), dblk * bk, carry)
            d_ref[0, pl.ds(q0, bq), :] = ((dq_acc[0] + dq_acc[1]) * 0.125).astype(d_ref.dtype)
            return 0

        lax.fori_loop(0, t // bq, q_loop, 0)
        d_ref[1] = dk_acc[...].astype(d_ref.dtype)
        d_ref[2] = dv_acc[...].astype(d_ref.dtype)

    col = lambda off: pl.BlockSpec((t, LANES), lambda p: (0, p + off))
    return pl.pallas_call(
        body, grid=(npair,), in_specs=[col(0), col(npair), col(2 * npair), col(0), col(0)],
        out_specs=pl.BlockSpec((3, t, LANES), lambda p: (0, 0, p)),
        out_shape=jax.ShapeDtypeStruct((3, t, d), BF16),
        scratch_shapes=[pltpu.VMEM((t, LANES), F32), pltpu.VMEM((t, LANES), F32), pltpu.VMEM((2, bq, LANES), F32)],
        name="sb_attn_bwd", compiler_params=_params(("parallel",)),
    )(qkv, qkv, qkv, ltot, do)


def _sgu_parts(hin, g, ws_ref, bf_ref):
    width = hin.shape[1] // 2
    h = jax.nn.gelu(hin)
    u, v = h[:, :width], h[:, width:]
    r = lax.rsqrt(jnp.mean(v * v, axis=-1, keepdims=True) + EPS)
    vn = v * r * g
    rr = lax.broadcasted_iota(jnp.int32, (CHUNK, CHUNK), 0)
    cc = lax.broadcasted_iota(jnp.int32, (CHUNK, CHUNK), 1)
    causal = cc <= rr
    wcs = [jnp.where(causal, ws_ref[gi], 0.0).astype(BF16) for gi in range(SG_GROUPS)]
    sv = jnp.concatenate(
        [jnp.dot(wcs[gi], vn[:, gi * CHUNK:(gi + 1) * CHUNK].astype(BF16), preferred_element_type=F32) + bf_ref[gi]
         for gi in range(SG_GROUPS)], axis=1)
    return u, v, r, vn, wcs, sv, causal


def sgu_fwd(hin, g, ws, bfull):
    t, w2 = hin.shape
    width = w2 // 2

    def body(h_ref, g_ref, ws_ref, bf_ref, o_ref):
        u, _, _, _, _, sv, _ = _sgu_parts(h_ref[...].astype(F32), g_ref[...], ws_ref, bf_ref)
        o_ref[...] = (u * sv).astype(o_ref.dtype)

    full = lambda a: pl.BlockSpec(a.shape, lambda i, nd=a.ndim: (0,) * nd)
    return pl.pallas_call(
        body, grid=(t // CHUNK,), in_specs=[pl.BlockSpec((CHUNK, w2), lambda i: (i, 0)), full(g), full(ws), full(bfull)],
        out_specs=pl.BlockSpec((CHUNK, width), lambda i: (i, 0)), out_shape=jax.ShapeDtypeStruct((t, width), BF16),
        name="sgu_fwd", compiler_params=_params(("parallel",)),
    )(hin, g, ws, bfull)


def sgu_bwd(hin, dp, g, ws, bfull):
    t, w2 = hin.shape
    width = w2 // 2

    def body(h_ref, dp_ref, g_ref, ws_ref, bf_ref, dh_ref, dws_ref, dbf_ref, dg_ref):
        i = pl.program_id(0)
        hin_v = h_ref[...].astype(F32)
        gv = g_ref[...]
        u, v, r, vn, wcs, sv, causal = _sgu_parts(hin_v, gv, ws_ref, bf_ref)
        dpv = dp_ref[...].astype(F32)
        du = dpv * sv
        dsv = dpv * u
        dvn_parts, dws_parts, dbf_parts = [], [], []
        for gi in range(SG_GROUPS):
            dsv_g = dsv[:, gi * CHUNK:(gi + 1) * CHUNK]
            dsv_b = dsv_g.astype(BF16)
            dvn_parts.append(lax.dot_general(wcs[gi], dsv_b, _MM_TA, preferred_element_type=F32))
            vn_b = vn[:, gi * CHUNK:(gi + 1) * CHUNK].astype(BF16)
            dws_parts.append(jnp.where(causal, lax.dot_general(dsv_b, vn_b, _MM_TB, preferred_element_type=F32), 0.0))
            dbf_parts.append(jnp.broadcast_to(jnp.sum(dsv_g, axis=1, keepdims=True), (CHUNK, CHUNK)))
        dvn = jnp.concatenate(dvn_parts, axis=1)
        dgain = jnp.sum(dvn * v * r, axis=0, keepdims=True)
        gvv = dvn * gv
        dv = r * gvv - v * (r * r * r) * jnp.mean(v * gvv, axis=-1, keepdims=True)
        _, vjp = jax.vjp(jax.nn.gelu, hin_v)
        dh_ref[...] = vjp(jnp.concatenate([du, dv], axis=1))[0].astype(dh_ref.dtype)

        @pl.when(i == 0)
        def _():
            for gi in range(SG_GROUPS):
                dws_ref[gi] = dws_parts[gi]
                dbf_ref[gi] = dbf_parts[gi]
            dg_ref[...] = dgain

        @pl.when(i > 0)
        def _():
            for gi in range(SG_GROUPS):
                dws_ref[gi] += dws_parts[gi]
                dbf_ref[gi] += dbf_parts[gi]
            dg_ref[...] += dgain

    full = lambda a: pl.BlockSpec(a.shape, lambda i, nd=a.ndim: (0,) * nd)
    sq = (SG_GROUPS, CHUNK, CHUNK)
    return pl.pallas_call(
        body, grid=(t // CHUNK,),
        in_specs=[pl.BlockSpec((CHUNK, w2), lambda i: (i, 0)), pl.BlockSpec((CHUNK, width), lambda i: (i, 0)),
                  full(g), full(ws), full(bfull)],
        out_specs=[pl.BlockSpec((CHUNK, w2), lambda i: (i, 0)), pl.BlockSpec(sq, lambda i: (0, 0, 0)),
                   pl.BlockSpec(sq, lambda i: (0, 0, 0)), pl.BlockSpec((1, width), lambda i: (0, 0))],
        out_shape=[jax.ShapeDtypeStruct((t, w2), BF16), jax.ShapeDtypeStruct(sq, F32), jax.ShapeDtypeStruct(sq, F32),
                   jax.ShapeDtypeStruct((1, width), F32)],
        name="sgu_bwd", compiler_params=_params(("arbitrary",)),
    )(hin, dp, g, ws, bfull)


def _disc1(lam_re, lam_im, log_dt):
    lr = jnp.minimum(lam_re, -1e-4)
    li = lam_im
    dt = jnp.exp(log_dt)
    mag = jnp.exp(dt * lr)
    ar = mag * jnp.cos(dt * li)
    ai = mag * jnp.sin(dt * li)
    den = lr * lr + li * li
    return ar, ai, ((ar - 1.0) * lr + ai * li) / den, (ai * lr - (ar - 1.0) * li) / den


def _disc2(cre, cim, b_re, b_im):
    return cre * b_re - cim * b_im, cre * b_im + cim * b_re


def _single(fn, ins, out_shapes, name):
    n = len(ins)

    def body(*refs):
        vals = fn(*[r[...] for r in refs[:n]])
        for ref, val in zip(refs[n:], vals):
            ref[...] = val

    return pl.pallas_call(body, out_shape=[jax.ShapeDtypeStruct(s, F32) for s in out_shapes], name=name)(*ins)


SCAN_SEGMENTS = 8
REORDER_STEPS = 64


def s5_reorder(x, to_steps):
    t, d = x.shape
    ns = SCAN_SEGMENTS
    seg = t // ns
    ts = min(REORDER_STEPS, seg)
    by_segment = ((ns, seg, d), pl.BlockSpec((ns, ts, d), lambda i: (0, i, 0)))
    by_step = ((seg, ns, d), pl.BlockSpec((ts, ns, d), lambda i: (i, 0, 0)))
    (in_shape, in_spec), (out_shape, out_spec) = (by_segment, by_step) if to_steps else (by_step, by_segment)

    def body(x_ref, o_ref):
        o_ref[...] = jnp.swapaxes(x_ref[...], 0, 1)

    out = pl.pallas_call(
        body, grid=(seg // ts,), in_specs=[in_spec], out_specs=out_spec,
        out_shape=jax.ShapeDtypeStruct(out_shape, x.dtype), name="s5_reorder", compiler_params=_params(("parallel",)),
    )(x.reshape(in_shape))
    return out.reshape(t, d)


def _cpow(ar, ai, n):
    rr, ri = None, None
    while n:
        if n & 1:
            rr, ri = (ar, ai) if rr is None else (rr * ar - ri * ai, rr * ai + ri * ar)
        ar, ai = ar * ar - ai * ai, 2.0 * ar * ai
        n >>= 1
    return rr, ri


def _edge_states(er, ei, pr, pi, reverse):
    ns = SCAN_SEGMENTS
    zero = jnp.zeros_like(er[0:1])
    rows_r, rows_i = [None] * ns, [None] * ns
    order = range(ns - 1, -1, -1) if reverse else range(ns)
    prev = None
    for s in order:
        if prev is None:
            rows_r[s], rows_i[s] = zero, zero
        else:
            cr, ci = rows_r[prev], rows_i[prev]
            rows_r[s] = er[prev:prev + 1] + pr * cr - pi * ci
            rows_i[s] = ei[prev:prev + 1] + pr * ci + pi * cr
        prev = s
    return jnp.concatenate(rows_r, axis=0), jnp.concatenate(rows_i, axis=0)


def s5_scan_fwd(bu2, a2):
    _, t, n = bu2.shape
    cb, ns = SCAN_COLS, SCAN_SEGMENTS
    seg = t // ns

    def body(bu_ref, a_ref, x_ref):
        ar, ai = a_ref[0:1, :], a_ref[1:2, :]

        def local(i, carry):
            xr, xi = carry
            xr, xi = ar * xr - ai * xi + bu_ref[0, i], ar * xi + ai * xr + bu_ref[1, i]
            x_ref[0, i] = xr
            x_ref[1, i] = xi
            return xr, xi

        zero = jnp.zeros((ns, cb), F32)
        er, ei = lax.fori_loop(0, seg, local, (zero, zero))
        cr, ci = _edge_states(er, ei, *_cpow(ar, ai, seg), reverse=False)

        def fix(i, carry):
            wr, wi = carry
            wr, wi = wr * ar - wi * ai, wr * ai + wi * ar
            x_ref[0, i] += wr * cr - wi * ci
            x_ref[1, i] += wr * ci + wi * cr
            return wr, wi

        lax.fori_loop(0, seg, fix, (jnp.ones((1, cb), F32), jnp.zeros((1, cb), F32)))

    blk = pl.BlockSpec((2, seg, ns, cb), lambda j: (0, 0, 0, j))
    out = pl.pallas_call(
        body, grid=(n // cb,), in_specs=[blk, pl.BlockSpec((2, cb), lambda j: (0, j))], out_specs=blk,
        out_shape=jax.ShapeDtypeStruct((2, seg, ns, n), F32), name="s5_scan_fwd", compiler_params=_params(("parallel",)),
    )(bu2.reshape(2, seg, ns, n), a2)
    return out.reshape(2, t, n)


def s5_scan_bwd(dx2, x2, a2):
    _, t, n = dx2.shape
    cb, ns = SCAN_COLS, SCAN_SEGMENTS
    seg = t // ns

    def body(dx_ref, x_ref, a_ref, g_ref, da_ref):
        ar, ai = a_ref[0:1, :], a_ref[1:2, :]

        def local(s, carry):
            gr, gi = carry
            i = seg - 1 - s
            gr, gi = dx_ref[0, i] + ar * gr + ai * gi, dx_ref[1, i] - ai * gr + ar * gi
            g_ref[0, i] = gr
            g_ref[1, i] = gi
            return gr, gi

        zero = jnp.zeros((ns, cb), F32)
        er, ei = lax.fori_loop(0, seg, local, (zero, zero))
        cr, ci = _edge_states(er, ei, *_cpow(ar, -ai, seg), reverse=True)
        row = lax.broadcasted_iota(jnp.int32, (ns, cb), 0)
        before_r = jnp.where(row == 0, 0.0, pltpu.roll(x_ref[0, seg - 1], 1, 0))
        before_i = jnp.where(row == 0, 0.0, pltpu.roll(x_ref[1, seg - 1], 1, 0))

        def fix(s, carry):
            wr, wi, dar, dai = carry
            i = seg - 1 - s
            wr, wi = wr * ar + wi * ai, wi * ar - wr * ai
            gr = g_ref[0, i] + wr * cr - wi * ci
            gi = g_ref[1, i] + wr * ci + wi * cr
            g_ref[0, i] = gr
            g_ref[1, i] = gi
            ip = jnp.maximum(i - 1, 0)
            xpr = jnp.where(i == 0, before_r, x_ref[0, ip])
            xpi = jnp.where(i == 0, before_i, x_ref[1, ip])
            return wr, wi, dar + gr * xpr + gi * xpi, dai + gi * xpr - gr * xpi

        one, z1 = jnp.ones((1, cb), F32), jnp.zeros((1, cb), F32)
        _, _, dar, dai = lax.fori_loop(0, seg, fix, (one, z1, zero, zero))
        da_ref[0:1, :] = jnp.sum(dar, axis=0, keepdims=True)
        da_ref[1:2, :] = jnp.sum(dai, axis=0, keepdims=True)

    blk = pl.BlockSpec((2, seg, ns, cb), lambda j: (0, 0, 0, j))
    vec = pl.BlockSpec((2, cb), lambda j: (0, j))
    g4, da = pl.pallas_call(
        body, grid=(n // cb,), in_specs=[blk, blk, vec], out_specs=[blk, vec],
        out_shape=[jax.ShapeDtypeStruct((2, seg, ns, n), F32), jax.ShapeDtypeStruct((2, n), F32)],
        name="s5_scan_bwd", compiler_params=_params(("parallel",)),
    )(dx2.reshape(2, seg, ns, n), x2.reshape(2, seg, ns, n), a2)
    return g4.reshape(2, t, n), da


_SP_U = SSM_PACK * SSM_GROUP
_SP_X = SSM_PACK * SSM_STATE
_NKB = SSM_GROUPS // SSM_PACK


def mm_s5(kind, a, b, m, name, res=None, tm=2048):
    tm = min(tm, m)
    kw = dict(passes=S5_PASSES, name=name)
    xblk = lambda row, sel, col: ((None, tm, _SP_X), lambda *g: (sel(*g), row(*g), col(*g)))
    if kind == "bu":
        o_blk, o_map = xblk(lambda g, i, k: i, lambda g, i, k: g // _NKB, lambda g, i, k: g % _NKB)
        return _mm(a, b, grid=(2 * _NKB, m // tm, 1), a_blk=(tm, _SP_U), a_map=lambda g, i, k: (i, g % _NKB),
                   b_blk=(None, None, _SP_U, _SP_X), b_map=lambda g, i, k: (g // _NKB, g % _NKB, 0, 0),
                   o_blk=o_blk, o_map=o_map, out_shape=(2, m, _NKB * _SP_X), out_dtype=F32, **kw)
    if kind == "yc":
        a_blk, a_map = xblk(lambda j, i, k: i, lambda j, i, k: k, lambda j, i, k: j)
        return _mm(a, b, grid=(_NKB, m // tm, 2), a_blk=a_blk, a_map=a_map,
                   b_blk=(None, None, _SP_X, _SP_U), b_map=lambda j, i, k: (k, j, 0, 0),
                   o_blk=(tm, _SP_U), o_map=lambda j, i, k: (i, j), out_shape=(m, _NKB * _SP_U), out_dtype=F32,
                   acc_2d=(tm, _SP_U), **kw)
    if kind == "dx":
        o_blk, o_map = xblk(lambda g, i, k: i, lambda g, i, k: g // _NKB, lambda g, i, k: g % _NKB)
        return _mm(a, b, grid=(2 * _NKB, m // tm, 1), a_blk=(tm, _SP_U), a_map=lambda g, i, k: (i, g % _NKB),
                   b_blk=(None, None, _SP_X, _SP_U), b_map=lambda g, i, k: (g // _NKB, g % _NKB, 0, 0),
                   o_blk=o_blk, o_map=o_map, out_shape=(2, m, _NKB * _SP_X), out_dtype=F32, dims=_MM_TB, **kw)
    if kind == "dcd":
        a_blk, a_map = xblk(lambda g, _, k: k, lambda g, _, k: g // _NKB, lambda g, _, k: g % _NKB)
        return _mm(a, b, grid=(2 * _NKB, 1, m // tm), a_blk=a_blk, a_map=a_map,
                   b_blk=(tm, _SP_U), b_map=lambda g, _, k: (k, g % _NKB),
                   o_blk=(None, None, _SP_X, _SP_U), o_map=lambda g, _, k: (g // _NKB, g % _NKB, 0, 0),
                   out_shape=(2, _NKB, _SP_X, _SP_U), out_dtype=F32, dims=_MM_TA, acc_2d=(_SP_X, _SP_U), **kw)
    if kind == "du":
        a_blk, a_map = xblk(lambda j, i, k: i, lambda j, i, k: k, lambda j, i, k: j)
        return _mm(a, b, grid=(_NKB, m // tm, 2), a_blk=a_blk, a_map=a_map,
                   b_blk=(None, None, _SP_U, _SP_X), b_map=lambda j, i, k: (k, j, 0, 0),
                   o_blk=(tm, _SP_U), o_map=lambda j, i, k: (i, j), out_shape=(m, _NKB * _SP_U), out_dtype=F32,
                   dims=_MM_TB, acc_2d=(tm, _SP_U), res=res, res_blk=(tm, _SP_U), res_map=lambda j, i, k: (i, j), **kw)
    assert kind == "dbd"
    b_blk, b_map = xblk(lambda g, _, k: k, lambda g, _, k: g // _NKB, lambda g, _, k: g % _NKB)
    return _mm(a, b, grid=(2 * _NKB, 1, m // tm), a_blk=(tm, _SP_U), a_map=lambda g, _, k: (k, g % _NKB),
               b_blk=b_blk, b_map=b_map,
               o_blk=(None, None, _SP_U, _SP_X), o_map=lambda g, _, k: (g // _NKB, g % _NKB, 0, 0),
               out_shape=(2, _NKB, _SP_U, _SP_X), out_dtype=F32, dims=_MM_TA, acc_2d=(_SP_U, _SP_X), **kw)


def _block_diag(w):
    g, a, b = w.shape
    eye = jnp.eye(SSM_PACK, dtype=w.dtype)
    wp = w.reshape(g // SSM_PACK, SSM_PACK, a, b)
    return jnp.einsum("kgab,gh->kgahb", wp, eye).reshape(g // SSM_PACK, SSM_PACK * a, SSM_PACK * b)


def _block_diag_t(d, a, b):
    k = d.shape[0]
    eye = jnp.eye(SSM_PACK, dtype=d.dtype)
    dp = d.reshape(k, SSM_PACK, a, SSM_PACK, b)
    return jnp.einsum("kgahb,gh->kgab", dp, eye).reshape(k * SSM_PACK, a, b)


def _coords():
    return lax.axis_index("x"), lax.axis_index("y"), lax.axis_index("c")


def all_gather(tensors, name, scatter=None):
    n = len(tensors)
    ns = 0 if scatter is None else 1
    any_spec = pl.BlockSpec(memory_space=pl.ANY)

    def body(*refs):
        ins, outs = refs[:n], refs[n + ns:2 * n + ns]
        send, recv, local = refs[2 * (n + ns):2 * (n + ns) + 3]
        x, y, c = _coords()
        me, sibling = (x, y, c), (x, y, 1 - c)
        chips = [(1 - x, y), (x, 1 - y), (1 - x, 1 - y)]

        def slot(p):
            return 4 * p[0] + 2 * p[1] + p[2]

        def copy(t, k, block, to, src=None):
            dst = outs[t].at[slot(block)]
            return pltpu.make_async_remote_copy(
                src_ref=dst if src is None else src, dst_ref=dst, send_sem=send.at[7 * t + k],
                recv_sem=recv.at[7 * t + k], device_id=to, device_id_type=pl.DeviceIdType.MESH)

        own, sent, landing = [], [], []
        for t in range(n):
            mine = pltpu.make_async_copy(ins[t], outs[t].at[slot(me)], local.at[t])
            mine.start()
            own.append(mine)
            first = [copy(t, 0, me, sibling, src=ins[t])]
            first += [copy(t, 1 + j, me, (*chip, c), src=ins[t]) for j, chip in enumerate(chips)]
            for cp in first:
                cp.start()
            sent += first
        if ns:
            src, dst = refs[n], refs[2 * n + ns]
            s_send, s_recv = refs[2 * (n + ns) + 3:]
            my_slot, peers = _me_and_peers()
            mine = pltpu.make_async_copy(src.at[my_slot], dst.at[my_slot], local.at[n])
            mine.start()
            own.append(mine)
            for k, (dev, peer_slot) in enumerate(peers):
                def piece(dst_slot, k=k, dev=dev, peer_slot=peer_slot):
                    return pltpu.make_async_remote_copy(
                        src_ref=src.at[peer_slot], dst_ref=dst.at[dst_slot], send_sem=s_send.at[k],
                        recv_sem=s_recv.at[k], device_id=dev, device_id_type=pl.DeviceIdType.MESH)

                cp = piece(my_slot)
                cp.start()
                sent.append(cp)
                landing.append(piece(peer_slot))
        for t in range(n):
            for j, chip in enumerate(chips):
                copy(t, 1 + j, (*chip, c), me).wait_recv()
                passed = copy(t, 4 + j, (*chip, c), sibling)
                passed.start()
                sent.append(passed)
        for t in range(n):
            copy(t, 0, sibling, me).wait_recv()
            for j, chip in enumerate(chips):
                copy(t, 4 + j, (*chip, 1 - c), me).wait_recv()
        for cp in landing:
            cp.wait_recv()
        for cp in sent:
            cp.wait_send()
        for cp in own:
            cp.wait()

    scratch = [pltpu.SemaphoreType.DMA((7 * n,)), pltpu.SemaphoreType.DMA((7 * n,)), pltpu.SemaphoreType.DMA((n + ns,))]
    out_shape = [jax.ShapeDtypeStruct((NDEV,) + a.shape, a.dtype) for a in tensors]
    args = list(tensors)
    if ns:
        scratch += [pltpu.SemaphoreType.DMA((NDEV - 1,)), pltpu.SemaphoreType.DMA((NDEV - 1,))]
        out_shape.append(jax.ShapeDtypeStruct(scatter.shape, scatter.dtype))
        args.append(scatter)
    return pl.pallas_call(
        body, in_specs=[any_spec] * (n + ns), out_specs=[any_spec] * (n + ns), out_shape=out_shape,
        scratch_shapes=scratch, name=name,
    )(*args)


_HBM_SPEC = pl.BlockSpec(memory_space=pltpu.HBM)
_SEM_SPEC = pl.BlockSpec(memory_space=pltpu.SEMAPHORE)
_NPEER = NDEV - 1


def _split_copy_params():
    return pltpu.CompilerParams(has_side_effects=pltpu.SideEffectType.DATAFLOW_SIDE_EFFECTING)


def _me_and_peers():
    x, y, c = _coords()
    peers = []
    for rel in range(1, NDEV):
        p = (1 - x if rel & 4 else x, 1 - y if rel & 2 else y, 1 - c if rel & 1 else c)
        peers.append((p, 4 * p[0] + 2 * p[1] + p[2]))
    return 4 * x + 2 * y + c, peers


def _hbm(a):
    return pltpu.with_memory_space_constraint(a, pltpu.HBM)


def gather_start(bufs, name):
    n = len(bufs)

    def body(*refs):
        ins, outs = refs[:n], refs[n:]
        me, peers = _me_and_peers()
        for t in range(n):
            for k, (dev, _) in enumerate(peers):
                pltpu.make_async_remote_copy(
                    src_ref=ins[t].at[me], dst_ref=ins[t].at[me], send_sem=outs[3 * t].at[k],
                    recv_sem=outs[3 * t + 1].at[k], device_id=dev, device_id_type=pl.DeviceIdType.MESH).start()
        outs[3 * n][...] = jnp.zeros_like(outs[3 * n])

    out_shape, out_specs = [], []
    for b in bufs:
        out_shape += [pltpu.SemaphoreType.DMA((_NPEER,)), pltpu.SemaphoreType.DMA((_NPEER,)), pltpu.HBM(b.shape, b.dtype)]
        out_specs += [_SEM_SPEC, _SEM_SPEC, _HBM_SPEC]
    out_shape.append(jax.ShapeDtypeStruct((8, LANES), F32))
    out_specs.append(pl.BlockSpec(memory_space=pltpu.VMEM))
    res = pl.pallas_call(
        body, name=name, out_shape=tuple(out_shape), in_specs=[_HBM_SPEC] * n, out_specs=tuple(out_specs),
        input_output_aliases={t: 3 * t + 2 for t in range(n)}, compiler_params=_split_copy_params(),
    )(*[_hbm(b) for b in bufs])
    return [tuple(res[3 * t:3 * t + 3]) for t in range(n)], res[3 * n]


def gather_wait(started, after, name):
    n = len(started)

    def body(*refs):
        bufs, sems = refs[:n], refs[n:3 * n]
        me, peers = _me_and_peers()
        for t in range(n):
            for k, (dev, slot) in enumerate(peers):
                cp = pltpu.make_async_remote_copy(
                    src_ref=bufs[t].at[me], dst_ref=bufs[t].at[slot], send_sem=sems[2 * t].at[k],
                    recv_sem=sems[2 * t + 1].at[k], device_id=dev, device_id_type=pl.DeviceIdType.MESH)
                cp.wait_recv()
                cp.wait_send()

    args = [s[2] for s in started] + [sem for s in started for sem in s[:2]] + [after]
    res = pl.pallas_call(
        body, name=name, out_shape=tuple(pltpu.HBM(s[2].shape, s[2].dtype) for s in started),
        in_specs=[_HBM_SPEC] * n + [_SEM_SPEC] * (2 * n) + [pl.BlockSpec(memory_space=pl.ANY)],
        out_specs=tuple([_HBM_SPEC] * n), input_output_aliases={t: t for t in range(n)},
        compiler_params=_split_copy_params(),
    )(*args)
    return list(res)


def scatter_start(srcs, name):
    n = len(srcs)
    lands = [lax.empty((_NPEER,) + s.shape[1:], s.dtype) for s in srcs]

    def body(*refs):
        ins, land_refs, outs = refs[:n], refs[n:2 * n], refs[2 * n:]
        _, peers = _me_and_peers()
        for t in range(n):
            for k, (dev, slot) in enumerate(peers):
                pltpu.make_async_remote_copy(
                    src_ref=ins[t].at[slot], dst_ref=land_refs[t].at[k], send_sem=outs[4 * t].at[k],
                    recv_sem=outs[4 * t + 1].at[k], device_id=dev, device_id_type=pl.DeviceIdType.MESH).start()
        outs[4 * n][...] = jnp.zeros_like(outs[4 * n])

    out_shape, out_specs = [], []
    for s, land in zip(srcs, lands):
        out_shape += [pltpu.SemaphoreType.DMA((_NPEER,)), pltpu.SemaphoreType.DMA((_NPEER,)),
                      pltpu.HBM(s.shape, s.dtype), pltpu.HBM(land.shape, land.dtype)]
        out_specs += [_SEM_SPEC, _SEM_SPEC, _HBM_SPEC, _HBM_SPEC]
    out_shape.append(jax.ShapeDtypeStruct((8, LANES), F32))
    out_specs.append(pl.BlockSpec(memory_space=pltpu.VMEM))
    aliases = {t: 4 * t + 2 for t in range(n)}
    aliases.update({n + t: 4 * t + 3 for t in range(n)})
    res = pl.pallas_call(
        body, name=name, out_shape=tuple(out_shape), in_specs=[_HBM_SPEC] * (2 * n), out_specs=tuple(out_specs),
        input_output_aliases=aliases, compiler_params=_split_copy_params(),
    )(*[_hbm(s) for s in srcs], *[_hbm(land) for land in lands])
    return [tuple(res[4 * t:4 * t + 4]) for t in range(n)], res[4 * n]


def scatter_wait(started, after, name):
    n = len(started)

    def body(*refs):
        srcs, land_refs, sems = refs[:n], refs[n:2 * n], refs[2 * n:4 * n]
        _, peers = _me_and_peers()
        for t in range(n):
            for k, (dev, slot) in enumerate(peers):
                cp = pltpu.make_async_remote_copy(
                    src_ref=srcs[t].at[slot], dst_ref=land_refs[t].at[k], send_sem=sems[2 * t].at[k],
                    recv_sem=sems[2 * t + 1].at[k], device_id=dev, device_id_type=pl.DeviceIdType.MESH)
                cp.wait_recv()
                cp.wait_send()

    args = [s[2] for s in started] + [s[3] for s in started] + [sem for s in started for sem in s[:2]] + [after]
    res = pl.pallas_call(
        body, name=name,
        out_shape=tuple([pltpu.HBM(s[2].shape, s[2].dtype) for s in started]
                        + [pltpu.HBM(s[3].shape, s[3].dtype) for s in started]),
        in_specs=[_HBM_SPEC] * (2 * n) + [_SEM_SPEC] * (2 * n) + [pl.BlockSpec(memory_space=pl.ANY)],
        out_specs=tuple([_HBM_SPEC] * (2 * n)), input_output_aliases={t: t for t in range(2 * n)},
        compiler_params=_split_copy_params(),
    )(*args)
    return [(res[t], res[n + t]) for t in range(n)]


_PACK_QUANTUM = 8 * LANES


def _pack(parts, lead=0):
    out = []
    for p in parts:
        head = p.shape[:lead]
        f = p.astype(F32).reshape(head + (-1,))
        pad = (-f.shape[-1]) % _PACK_QUANTUM
        if pad:
            f = jnp.concatenate([f, jnp.zeros(head + (pad,), F32)], axis=-1)
        out.append(f.reshape(head + (-1, LANES)))
    return jnp.concatenate(out, axis=lead)


def _unpack(buf, shapes):
    head = buf.shape[:-2]
    out, r = [], 0
    for s in shapes:
        n = 1
        for v in s:
            n *= v
        nr = -(-n // _PACK_QUANTUM) * 8
        flat = buf[..., r:r + nr, :].reshape(head + (nr * LANES,))[..., :n]
        out.append(flat.reshape(head + tuple(s)))
        r += nr
    return out


BIG = ("sb_w_qkv", "sb_w_o", "sg_w_in", "sg_w_o", "ssm_w_in", "ssm_w_glu", "ffn_w_up", "ffn_w_down")
SMALL_SHARDED = ("norm_g", "ssm_d", "ffn_conv_w")
REPLICATED = ("final_norm_g", "sg_norm_g", "sg_w_s", "sg_b", "ssm_lam_re", "ssm_lam_im", "ssm_log_dt",
              "ssm_b_re", "ssm_b_im", "ssm_c_re", "ssm_c_im", "ffn_conv_b")
WEIGHTS = ("norm_g", "final_norm_g", "sb_w_qkv", "sb_w_o", "sg_w_in", "sg_norm_g", "sg_w_s", "sg_b", "sg_w_o",
           "ssm_w_in", "ssm_lam_re", "ssm_lam_im", "ssm_log_dt", "ssm_b_re", "ssm_b_im", "ssm_c_re", "ssm_c_im",
           "ssm_d", "ssm_w_glu", "ffn_w_up", "ffn_conv_w", "ffn_conv_b", "ffn_w_down")


def _step(x, loss_target, w, m, v):
    t, d = x.shape[1], x.shape[2]
    depth = w["norm_g"].shape[0]
    x0 = x.reshape(t, d)
    tgt = loss_target.reshape(t, d)

    mx, my, mc = _coords()
    me = (4 * mx + 2 * my + mc).astype(jnp.int32).reshape(1)
    shard_pack = _pack([w[k] for k in SMALL_SHARDED])
    gathered_small, = all_gather([shard_pack], name="gather_small_weights")
    mixer_weights = (("sb_w_qkv", "sb_w_o"), ("sg_w_in", "sg_w_o"), ("ssm_w_in", "ssm_w_glu"))
    order = []
    for i in range(depth):
        order += [(k, i // 3) for k in mixer_weights[i % 3]] + [("ffn_w_up", i), ("ffn_w_down", i)]
    as_kept = lambda tree, k: jnp.swapaxes(tree[k], 1, 2) if k == "ffn_w_up" else tree[k]
    pending, token = {}, None
    for group in (order[:4], order[4:]):
        started, token = gather_start([cast_into_slot(as_kept(w, k), l, me) for k, l in group], "gather_weights_start")
        pending.update(zip(group, started))
    wg = {}

    def weights(keys, after):
        missing = [key for key in keys if key not in wg]
        if missing:
            for key, buf in zip(missing, gather_wait([pending[key] for key in missing], after, "gather_weights_wait")):
                wg[key] = buf[:, None]
        return [wg[key] for key in keys]

    ng, sd, cw = _unpack(gathered_small, [w[k].shape for k in SMALL_SHARDED])
    norm_full = jnp.transpose(ng, (1, 2, 0, 3)).reshape(depth, 2, d)
    ssm_d_full = jnp.transpose(sd, (1, 0, 2)).reshape(1, d)
    nc = cw.shape[-1]
    conv_b3 = w["ffn_conv_b"].reshape(depth, NDEV, nc)
    p3 = [jnp.concatenate([cw[:, l], conv_b3[l][:, None, :], jnp.zeros((NDEV, 8 - CONV_K - 1, nc), F32)], axis=1)
          for l in range(depth)]

    g_, p_, h_ = SSM_GROUPS, SSM_STATE, SSM_GROUP
    lam_re, lam_im = w["ssm_lam_re"][0], w["ssm_lam_im"][0]
    log_dt = w["ssm_log_dt"][0].reshape(g_, 1)
    b_re, b_im = w["ssm_b_re"][0].reshape(g_ * p_, h_), w["ssm_b_im"][0].reshape(g_ * p_, h_)
    ar, ai, cre, cim = _single(_disc1, [lam_re, lam_im, log_dt], [(g_, p_)] * 4, "s5_disc1")
    cre_c, cim_c = cre.reshape(g_ * p_, 1), cim.reshape(g_ * p_, 1)
    bbr, bbi = _single(_disc2, [cre_c, cim_c, b_re, b_im], [(g_ * p_, h_)] * 2, "s5_disc2")
    per_group_t = lambda a, r, c: jnp.swapaxes(a.reshape(g_, r, c), 1, 2)
    bd = jnp.stack([_block_diag(per_group_t(bbr, p_, h_)), _block_diag(per_group_t(bbi, p_, h_))])
    cd = jnp.stack([_block_diag(per_group_t(w["ssm_c_re"][0], h_, p_)),
                    -_block_diag(per_group_t(w["ssm_c_im"][0], h_, p_))])
    a2 = jnp.stack([ar.reshape(g_ * p_), ai.reshape(g_ * p_)])

    sg_gain = w["sg_norm_g"]
    sg_ws = w["sg_w_s"][0]
    sg_bfull = jnp.broadcast_to(w["sg_b"][0][:, :, None], sg_ws.shape)

    acts = []
    xc = x0
    xn = rms_fwd(xc, norm_full[0, 0][None], "rms_fwd")
    for i in range(depth):
        mixer, j = i % 3, i // 3
        st = {"x": xc, "xn": xn}
        g0, g1 = norm_full[i, 0][None], norm_full[i, 1][None]
        g_next = norm_full[i + 1, 0][None] if i + 1 < depth else None
        k_in, k_out = [(k, j) for k in mixer_weights[mixer]]
        w_in, = weights([k_in], token if i == 0 else xn)
        if mixer == 0:
            qkv = mm_cs_fwd(xn, w_in, 0, BF16, "qkv_fwd")
            o, ltot = sb_attn_fwd(qkv)
            w_out, = weights([k_out], o)
            x1, xn2 = mm_rs_fwd(o, w_out, 0, xc, F32, "attn_out_fwd", norm=g1)
            st.update(qkv=qkv, o=o, ltot=ltot)
        elif mixer == 1:
            hin = mm_cs_fwd(xn, w_in, 0, BF16, "sg_in_fwd")
            p = sgu_fwd(hin, sg_gain, sg_ws, sg_bfull)
            w_out, = weights([k_out], p)
            x1, xn2 = mm_rs_fwd(p, w_out, 0, xc, F32, "sg_out_fwd", norm=g1)
            st.update(hin=hin, p=p)
        else:
            u = mm_rs_fwd(xn, w_in, 0, None, F32, "ssm_in_fwd")
            u_s = s5_reorder(u, True)
            x2 = s5_scan_fwd(mm_s5("bu", u_s, bd, t, "s5_bu"), a2)
            yc_s = mm_s5("yc", x2, cd, t, "s5_yc")
            yg = s5_post_fwd(s5_reorder(yc_s, False), u, ssm_d_full)
            w_out, = weights([k_out], yg)
            hg = mm_cs_fwd(yg, w_out, 0, BF16, "ssm_glu_fwd")
            x1, xn2 = glu_fwd(hg, xc, g1)
            st.update(u_s=u_s, x2=x2, yc_s=yc_s, yg=yg, hg=hg)
        w_up, = weights([("ffn_w_up", i)], xn2)
        h3 = mm_up_fwd(xn2, w_up, 0, BF16, "ffn_up_fwd")
        gated = ffn_gate_fwd(h3, p3[i])
        w_down, = weights([("ffn_w_down", i)], gated)
        if g_next is None:
            xc = mm_down_fwd(gated, w_down, 0, x1, "ffn_down_fwd")
        else:
            xc, xn = mm_down_fwd(gated, w_down, 0, x1, "ffn_down_fwd", norm=g_next)
        st.update(x1=x1, xn2=xn2, h3=h3, gated=gated, g0=g0, g1=g1)
        acts.append(st)

    dx, loss_lanes, d_final_g = loss_head(xc, w["final_norm_g"][None], tgt)
    loss = lax.psum(loss_lanes[0, 0], MESH_AXES)

    scattering = {}
    d_norm = [[None, None] for _ in range(depth)]
    d_p3 = [None] * depth
    rep = {}
    d_ssm_d = None
    token = None

    def scatter(grads_by_key):
        keys = list(grads_by_key)
        started, tok = scatter_start([grads_by_key[key] for key in keys], "scatter_grads_start")
        scattering[tuple(keys)] = started
        return tok

    for i in reversed(range(depth)):
        mixer, j = i % 3, i // 3
        st = acts[i]
        k_in, k_out = [(k, j) for k in mixer_weights[mixer]]
        w_in, w_out, w_up, w_down = weights([k_in, k_out, ("ffn_w_up", i), ("ffn_w_down", i)], None)
        dgated = mm_down_da(dx, w_down, 0, "ffn_down_da", dep=token)
        g_down = mm_down_dw(st["gated"], dx, "ffn_down_dw")
        dy_a, dy_g, dp_a, dp_g = ffn_gate_bwd(st["h3"], dgated, p3[i])
        d_p3[i] = jnp.concatenate([dp_a, dp_g], axis=0)
        dh3 = ffn_conv_t(dy_a, dy_g, p3[i])
        dxn2 = mm_up_da(dh3, w_up, 0, "ffn_up_da")
        g_up = mm_up_dw(st["xn2"], dh3, "ffn_up_dw")
        dx1, d_norm[i][1] = rms_bwd(st["x1"], st["g1"], dxn2, dx, "rms_bwd")
        token = scatter({("ffn_w_down", i): g_down, ("ffn_w_up", i): g_up})
        if mixer == 0:
            do = mm_rs_da(dx1, w_out, 0, BF16, "attn_out_da", dep=token)
            g_out = mm_rs_dw(st["o"], dx1, "attn_out_dw")
            d3 = sb_attn_bwd(st["qkv"], st["ltot"], do)
            g_in = mm_qkv_dw(st["xn"], d3, w_in.shape[3], "qkv_dw")
            token = scatter({k_in: g_in, k_out: g_out})
            dxn = mm_qkv_da(d3, w_in, 0, "qkv_da", dep=token)
        elif mixer == 1:
            dp = mm_rs_da(dx1, w_out, 0, BF16, "sg_out_da", dep=token)
            g_out = mm_rs_dw(st["p"], dx1, "sg_out_dw")
            dhin, d_ws, d_bfull, d_gain = sgu_bwd(st["hin"], dp, sg_gain, sg_ws, sg_bfull)
            rep.update(sg_w_s=d_ws[None], sg_b=d_bfull[None, :, :, 0], sg_norm_g=d_gain)
            dxn = mm_cs_da(dhin, w_in, 0, t, "sg_in_da")
            g_in = mm_cs_dw(st["xn"], dhin, w_in.shape[3], "sg_in_dw")
        else:
            dhg = glu_bwd(st["hg"], dx1, token)
            dyg = mm_cs_da(dhg, w_out, 0, t, "ssm_glu_da")
            g_out = mm_cs_dw(st["yg"], dhg, w_out.shape[3], "ssm_glu_dw")
            dyc_s, du_skip_s, d_ssm_d = s5_post_bwd(st["yc_s"], st["u_s"], ssm_d_full, s5_reorder(dyg, True))
            dx2 = mm_s5("dx", dyc_s, cd, t, "s5_dx")
            dcd = mm_s5("dcd", st["x2"], dyc_s, t, "s5_dcd")
            g2, da2 = s5_scan_bwd(dx2, st["x2"], a2)
            du = s5_reorder(mm_s5("du", g2, bd, t, "s5_du", res=du_skip_s), False)
            dbd = mm_s5("dbd", st["u_s"], g2, t, "s5_dbd")
            from_bd = lambda blk: jnp.swapaxes(_block_diag_t(blk, h_, p_), 1, 2).reshape(g_ * p_, h_)

            def disc2_bwd(c1, c2, b1, b2, t1, t2):
                return jax.vjp(_disc2, c1, c2, b1, b2)[1]((t1, t2))

            d_cre, d_cim, d_b_re, d_b_im = _single(
                disc2_bwd, [cre_c, cim_c, b_re, b_im, from_bd(dbd[0]), from_bd(dbd[1])],
                [(g_ * p_, 1)] * 2 + [(g_ * p_, h_)] * 2, "s5_disc2_bwd")

            def disc1_bwd(l1, l2, ld, t1, t2, t3, t4):
                return jax.vjp(_disc1, l1, l2, ld)[1]((t1, t2, t3, t4))

            d_lam_re, d_lam_im, d_log_dt = _single(
                disc1_bwd, [lam_re, lam_im, log_dt, da2[0].reshape(g_, p_), da2[1].reshape(g_, p_),
                            d_cre.reshape(g_, p_), d_cim.reshape(g_, p_)],
                [(g_, p_), (g_, p_), (g_, 1)], "s5_disc1_bwd")
            from_cd = lambda blk: jnp.swapaxes(_block_diag_t(blk, p_, h_), 1, 2)
            rep.update(ssm_lam_re=d_lam_re[None], ssm_lam_im=d_lam_im[None], ssm_log_dt=d_log_dt.reshape(1, g_),
                       ssm_b_re=d_b_re.reshape(1, g_, p_, h_), ssm_b_im=d_b_im.reshape(1, g_, p_, h_),
                       ssm_c_re=from_cd(dcd[0])[None], ssm_c_im=-from_cd(dcd[1])[None])
            dxn = mm_rs_da(du, w_in, 0, F32, "ssm_in_da")
            g_in = mm_rs_dw(st["xn"], du, "ssm_in_dw")
        dx, d_norm[i][0] = rms_bwd(st["x"], st["g0"], dxn, dx1, "rms_bwd")
        if mixer != 0:
            token = scatter({k_in: g_in, k_out: g_out})

    rep["final_norm_g"] = d_final_g.reshape(d)
    rep["ffn_conv_b"] = jnp.stack([d_p3[l][:, CONV_K, :].reshape(NDEV * nc) for l in range(depth)])

    d_norm_full = jnp.stack([jnp.concatenate(pair, axis=0) for pair in d_norm])
    d_norm_pieces = jnp.transpose(d_norm_full.reshape(depth, 2, NDEV, d // NDEV), (2, 0, 1, 3))
    d_ssm_d_pieces = jnp.transpose(d_ssm_d.reshape(1, NDEV, d // NDEV), (1, 0, 2))
    d_conv_w_pieces = jnp.stack([d_p3[l][:, :CONV_K, :] for l in range(depth)], axis=1)
    small_pieces = _pack([d_norm_pieces, d_ssm_d_pieces, d_conv_w_pieces], lead=1)
    rep_parts, small_received = all_gather([_pack([rep[k] for k in REPLICATED]).astype(BF16)],
                                           name="exchange_small_grads", scatter=small_pieces)
    own, landed = {}, {}
    for keys, started in scattering.items():
        for key, (src, land) in zip(keys, scatter_wait(started, dx, "scatter_grads_wait")):
            own[key], landed[key] = src, land

    grads, deltas, new_m, new_v = {}, {}, {}, {}
    for k in BIG:
        layers = range(w[k].shape[0])
        res = adamw_layers(as_kept(w, k), as_kept(m, k), as_kept(v, k), [landed[(k, l)] for l in layers],
                           [own[(k, l)] for l in layers], me, "adamw")
        grads[k], deltas[k], new_m[k], new_v[k] = [jnp.swapaxes(r, 1, 2) if k == "ffn_w_up" else r for r in res]
    for names, parts in ((SMALL_SHARDED, small_received), (REPLICATED, rep_parts)):
        res = adamw(_pack([w[k] for k in names]), _pack([m[k] for k in names]), _pack([v[k] for k in names]), parts,
                    "adamw_small")
        for tree, buf in zip((grads, deltas, new_m, new_v), res):
            for k, val in zip(names, _unpack(buf, [w[k].shape for k in names])):
                tree[k] = val
    grad_x = dx.reshape(x.shape)
    return (loss, grad_x, *[grads[k] for k in WEIGHTS], *[deltas[k] for k in WEIGHTS],
            *[new_m[k] for k in WEIGHTS], *[new_v[k] for k in WEIGHTS])


def kernel(x, norm_g, final_norm_g, sb_w_qkv, sb_w_o, sg_w_in, sg_norm_g, sg_w_s, sg_b, sg_w_o, ssm_w_in, ssm_lam_re, ssm_lam_im, ssm_log_dt, ssm_b_re, ssm_b_im, ssm_c_re, ssm_c_im, ssm_d, ssm_w_glu, ffn_w_up, ffn_conv_w, ffn_conv_b, ffn_w_down, loss_target, m_norm_g, m_final_norm_g, m_sb_w_qkv, m_sb_w_o, m_sg_w_in, m_sg_norm_g, m_sg_w_s, m_sg_b, m_sg_w_o, m_ssm_w_in, m_ssm_lam_re, m_ssm_lam_im, m_ssm_log_dt, m_ssm_b_re, m_ssm_b_im, m_ssm_c_re, m_ssm_c_im, m_ssm_d, m_ssm_w_glu, m_ffn_w_up, m_ffn_conv_w, m_ffn_conv_b, m_ffn_w_down, v_norm_g, v_final_norm_g, v_sb_w_qkv, v_sb_w_o, v_sg_w_in, v_sg_norm_g, v_sg_w_s, v_sg_b, v_sg_w_o, v_ssm_w_in, v_ssm_lam_re, v_ssm_lam_im, v_ssm_log_dt, v_ssm_b_re, v_ssm_b_im, v_ssm_c_re, v_ssm_c_im, v_ssm_d, v_ssm_w_glu, v_ffn_w_up, v_ffn_conv_w, v_ffn_conv_b, v_ffn_w_down):
    w = dict(zip(WEIGHTS, (norm_g, final_norm_g, sb_w_qkv, sb_w_o, sg_w_in, sg_norm_g, sg_w_s, sg_b, sg_w_o, ssm_w_in,
                           ssm_lam_re, ssm_lam_im, ssm_log_dt, ssm_b_re, ssm_b_im, ssm_c_re, ssm_c_im, ssm_d, ssm_w_glu,
                           ffn_w_up, ffn_conv_w, ffn_conv_b, ffn_w_down)))
    m = dict(zip(WEIGHTS, (m_norm_g, m_final_norm_g, m_sb_w_qkv, m_sb_w_o, m_sg_w_in, m_sg_norm_g, m_sg_w_s, m_sg_b,
                           m_sg_w_o, m_ssm_w_in, m_ssm_lam_re, m_ssm_lam_im, m_ssm_log_dt, m_ssm_b_re, m_ssm_b_im,
                           m_ssm_c_re, m_ssm_c_im, m_ssm_d, m_ssm_w_glu, m_ffn_w_up, m_ffn_conv_w, m_ffn_conv_b,
                           m_ffn_w_down)))
    v = dict(zip(WEIGHTS, (v_norm_g, v_final_norm_g, v_sb_w_qkv, v_sb_w_o, v_sg_w_in, v_sg_norm_g, v_sg_w_s, v_sg_b,
                           v_sg_w_o, v_ssm_w_in, v_ssm_lam_re, v_ssm_lam_im, v_ssm_log_dt, v_ssm_b_re, v_ssm_b_im,
                           v_ssm_c_re, v_ssm_c_im, v_ssm_d, v_ssm_w_glu, v_ffn_w_up, v_ffn_conv_w, v_ffn_conv_b,
                           v_ffn_w_down)))
    return _step(x, loss_target, w, m, v)
```

```python
import functools

import jax
import jax.numpy as jnp
from jax import lax
from jax.experimental import pallas as pl
from jax.experimental.pallas import tpu as pltpu

F32, BF16 = jnp.float32, jnp.bfloat16
MESH_AXES = ("x", "y", "c")
NDEV = 8
EPS = 1e-6
HEAD_DIM = 64
LANES = 128
ATT_BQ, ATT_BK = 2048, 256
CHUNK = 128
SG_GROUPS = 8
SSM_GROUPS, SSM_STATE, SSM_GROUP = 64, 64, 16
SSM_PACK = 8
S5_PASSES = 1
CONV_K = 3
HALO = 16
ROW_BLOCK = 512
SCAN_COLS = 256
ADAM_LR, ADAM_B1, ADAM_B2, ADAM_EPS, ADAM_WD, ADAM_STEP = 0.001, 0.9, 0.999, 1e-08, 0.01, 10
VMEM_LIMIT = 56 * 1024 * 1024

_MM = (((1,), (0,)), ((), ()))
_MM_TB = (((1,), (1,)), ((), ()))
_MM_TA = (((0,), (0,)), ((), ()))


def _params(sem):
    return pltpu.CompilerParams(dimension_semantics=sem, vmem_limit_bytes=VMEM_LIMIT)


def _rows(total, cap, mult=16):
    best = None
    for d in range(mult, min(total, cap) + 1, mult):
        if total % d == 0:
            best = d
    return best if best is not None else total


def _dot(a, b, dims, passes):
    if passes == 1:
        return lax.dot_general(a.astype(BF16), b.astype(BF16), dims, preferred_element_type=F32)
    a = a.astype(F32)
    b = b.astype(F32)
    ah = a.astype(BF16)
    bh = b.astype(BF16)
    al = (a - ah.astype(F32)).astype(BF16)
    bl = (b - bh.astype(F32)).astype(BF16)
    out = lax.dot_general(ah, bh, dims, preferred_element_type=F32)
    out = out + lax.dot_general(al, bh, dims, preferred_element_type=F32)
    return out + lax.dot_general(ah, bl, dims, preferred_element_type=F32)


def _mm(a, b, *, grid, a_blk, a_map, b_blk, b_map, o_blk, o_map, out_shape, out_dtype, name,
        dims=_MM, passes=1, res=None, res_blk=None, res_map=None, b_2d=None, acc_2d=None, dep=None, norm=None):
    nk = grid[2]
    has_res, has_norm = res is not None, norm is not None
    a_maps = list(a_map) if isinstance(a_map, (list, tuple)) else [a_map]
    b_maps = list(b_map) if isinstance(b_map, (list, tuple)) else [b_map]
    na, nb = len(a_maps), len(b_maps)
    n_in = na + nb + has_res + has_norm + (dep is not None)

    def body(*refs):
        o_ref = refs[n_in]
        r_ref = refs[na + nb] if has_res else None
        av = refs[0][...] if na == 1 else jnp.concatenate([r[...] for r in refs[:na]], axis=-1)
        bv = refs[na][...] if nb == 1 else jnp.concatenate([r[...] for r in refs[na:na + nb]], axis=-1)
        if b_2d is not None:
            bv = bv.reshape(b_2d)
        part = _dot(av, bv, dims, passes)

        def finish(total):
            if has_res:
                total = total + r_ref[...].astype(F32)
            o_ref[...] = total.reshape(o_ref.shape).astype(o_ref.dtype)
            if has_norm:
                refs[n_in + 1][...] = _rms(total, refs[na + nb + has_res][...]).astype(BF16)

        if nk == 1:
            finish(part)
        else:
            acc_ref = refs[-1]
            k = pl.program_id(2)

            @pl.when(k == 0)
            def _():
                acc_ref[...] = part

            @pl.when(k > 0)
            def _():
                acc_ref[...] += part

            @pl.when(k == nk - 1)
            def _():
                finish(acc_ref[...])

    in_specs = [pl.BlockSpec(a_blk, f) for f in a_maps] + [pl.BlockSpec(b_blk, f) for f in b_maps]
    args = [a] * na + [b] * nb
    if has_res:
        in_specs.append(pl.BlockSpec(res_blk, res_map))
        args.append(res)
    if has_norm:
        in_specs.append(pl.BlockSpec(norm.shape, lambda *_: (0, 0)))
        args.append(norm)
    if dep is not None:
        in_specs.append(pl.BlockSpec(memory_space=pl.ANY))
        args.append(dep)
    scratch = [pltpu.VMEM(acc_2d, F32)] if nk > 1 else []
    out_specs, out_shapes = pl.BlockSpec(o_blk, o_map), jax.ShapeDtypeStruct(out_shape, out_dtype)
    if has_norm:
        out_specs, out_shapes = [out_specs] * 2, [out_shapes, jax.ShapeDtypeStruct(out_shape, BF16)]
    return pl.pallas_call(
        body, grid=grid, in_specs=in_specs, out_specs=out_specs, out_shape=out_shapes, scratch_shapes=scratch,
        name=name, compiler_params=_params(("parallel", "parallel", "arbitrary")),
    )(*args)


def _cs_act_spec(ns, tm, row_of, col_of):
    if ns % LANES == 0:
        return (tm, ns), lambda *g: (row_of(*g), col_of(*g))
    return (None, tm, ns), lambda *g: (col_of(*g), row_of(*g), 0)


def mm_cs_fwd(a, w4, l, out_dtype, name, tm=2048):
    m, k = a.shape
    tm = min(tm, m)
    ns = w4.shape[3]
    o_blk, o_map = _cs_act_spec(ns, tm, lambda j, i, kk: i, lambda j, i, kk: j)
    out_shape = (m, NDEV * ns) if ns % LANES == 0 else (NDEV, m, ns)
    return _mm(a, w4, grid=(NDEV, m // tm, 1), a_blk=(tm, k), a_map=lambda j, i, kk: (i, 0),
               b_blk=(None, None, k, ns), b_map=lambda j, i, kk: (j, l, 0, 0),
               o_blk=o_blk, o_map=o_map, out_shape=out_shape, out_dtype=out_dtype, name=name)


def mm_cs_da(dc, w4, l, m, name, tm=1024):
    k, ns = w4.shape[2], w4.shape[3]
    tm = min(tm, m)
    a_blk, a_map = _cs_act_spec(ns, tm, lambda i, _, j: i, lambda i, _, j: j)
    return _mm(dc, w4, grid=(m // tm, 1, NDEV), a_blk=a_blk, a_map=a_map,
               b_blk=(None, None, k, ns), b_map=lambda i, _, j: (j, l, 0, 0),
               o_blk=(tm, k), o_map=lambda i, _, j: (i, 0), out_shape=(m, k), out_dtype=F32,
               dims=_MM_TB, acc_2d=(tm, k), name=name)


def mm_cs_dw(a, dc, ns, name, tk=2048):
    m, k = a.shape
    tk = min(tk, m)
    b_blk, b_map = _cs_act_spec(ns, tk, lambda j, _, kk: kk, lambda j, _, kk: j)
    return _mm(a, dc, grid=(NDEV, 1, m // tk), a_blk=(tk, k), a_map=lambda j, _, kk: (kk, 0),
               b_blk=b_blk, b_map=b_map, o_blk=(None, k, ns), o_map=lambda j, _, kk: (j, 0, 0),
               out_shape=(NDEV, k, ns), out_dtype=BF16, dims=_MM_TA, acc_2d=(k, ns), name=name)


def mm_up_fwd(a, wt4, l, out_dtype, name, tm=2048):
    m, k = a.shape
    tm = min(tm, m)
    ns = wt4.shape[2]
    return _mm(a, wt4, grid=(NDEV, m // tm, 1), a_blk=(tm, k), a_map=lambda j, i, kk: (i, 0),
               b_blk=(None, None, ns, k), b_map=lambda j, i, kk: (j, l, 0, 0), dims=_MM_TB,
               o_blk=(None, tm, ns), o_map=lambda j, i, kk: (j, i, 0), out_shape=(NDEV, m, ns), out_dtype=out_dtype,
               name=name)


def mm_up_da(dc3, wt4, l, name, tm=1024):
    _, m, ns = dc3.shape
    tm = min(tm, m)
    k = wt4.shape[3]
    return _mm(dc3, wt4, grid=(m // tm, 1, NDEV), a_blk=(None, tm, ns), a_map=lambda i, _, j: (j, i, 0),
               b_blk=(None, None, ns, k), b_map=lambda i, _, j: (j, l, 0, 0),
               o_blk=(tm, k), o_map=lambda i, _, j: (i, 0), out_shape=(m, k), out_dtype=F32, acc_2d=(tm, k), name=name)


def mm_up_dw(a, dc3, name, tk=2048):
    m, k = a.shape
    tk = min(tk, m)
    ns = dc3.shape[2]
    return _mm(dc3, a, grid=(NDEV, 1, m // tk), a_blk=(None, tk, ns), a_map=lambda j, _, kk: (j, kk, 0),
               b_blk=(tk, k), b_map=lambda j, _, kk: (kk, 0), dims=_MM_TA,
               o_blk=(None, ns, k), o_map=lambda j, _, kk: (j, 0, 0), out_shape=(NDEV, ns, k), out_dtype=BF16,
               acc_2d=(ns, k), name=name)


def mm_rs_fwd(a, w4, l, res, out_dtype, name, tm=1024, norm=None):
    m, k = a.shape
    tm = min(tm, m)
    ks, n = w4.shape[2], w4.shape[3]
    return _mm(a, w4, grid=(m // tm, 1, 1), a_blk=(tm, k), a_map=lambda i, _, kk: (i, 0),
               b_blk=(NDEV, None, ks, n), b_map=lambda i, _, kk: (0, l, 0, 0), b_2d=(k, n),
               o_blk=(tm, n), o_map=lambda i, _, kk: (i, 0), out_shape=(m, n), out_dtype=out_dtype,
               res=res, res_blk=(tm, n), res_map=lambda i, _, kk: (i, 0), name=name, norm=norm)


def mm_rs_da(dc, w4, l, out_dtype, name, tm=1024, dep=None):
    m, n = dc.shape
    tm = min(tm, m)
    ks = w4.shape[2]
    k = NDEV * ks
    return _mm(dc, w4, grid=(m // tm, 1, 1), a_blk=(tm, n), a_map=lambda i, _, kk: (i, 0),
               b_blk=(NDEV, None, ks, n), b_map=lambda i, _, kk: (0, l, 0, 0), b_2d=(k, n),
               o_blk=(tm, k), o_map=lambda i, _, kk: (i, 0), out_shape=(m, k), out_dtype=out_dtype,
               dims=_MM_TB, name=name, dep=dep)


def mm_rs_dw(a, dc, name, tk=1024):
    m, k = a.shape
    tk = min(tk, m)
    n = dc.shape[1]
    ks = k // NDEV
    return _mm(a, dc, grid=(1, 1, m // tk), a_blk=(tk, k), a_map=lambda _, __, kk: (kk, 0),
               b_blk=(tk, n), b_map=lambda _, __, kk: (kk, 0),
               o_blk=(NDEV, ks, n), o_map=lambda _, __, kk: (0, 0, 0), out_shape=(NDEV, ks, n),
               out_dtype=BF16, dims=_MM_TA, acc_2d=(k, n), name=name)


def mm_down_fwd(a3, w4, l, res, name, tm=1024, norm=None):
    nj, m, kc = a3.shape
    tm = min(tm, m)
    ks, n = w4.shape[2], w4.shape[3]
    return _mm(a3, w4, grid=(m // tm, 1, nj), a_blk=(None, tm, kc), a_map=lambda i, _, j: (j, i, 0),
               b_blk=(2, None, ks, n), b_map=lambda i, _, j: (j, l, 0, 0), b_2d=(kc, n),
               o_blk=(tm, n), o_map=lambda i, _, j: (i, 0), out_shape=(m, n), out_dtype=F32,
               res=res, res_blk=(tm, n), res_map=lambda i, _, j: (i, 0), acc_2d=(tm, n), name=name, norm=norm)


def mm_down_da(dc, w4, l, name, tm=2048, dep=None):
    m, n = dc.shape
    tm = min(tm, m)
    ks = w4.shape[2]
    kc = 2 * ks
    nj = NDEV // 2
    return _mm(dc, w4, grid=(nj, m // tm, 1), a_blk=(tm, n), a_map=lambda j, i, _: (i, 0),
               b_blk=(2, None, ks, n), b_map=lambda j, i, _: (j, l, 0, 0), b_2d=(kc, n),
               o_blk=(None, tm, kc), o_map=lambda j, i, _: (j, i, 0), out_shape=(nj, m, kc),
               out_dtype=BF16, dims=_MM_TB, name=name, dep=dep)


def mm_down_dw(a3, dc, name, tk=2048):
    nj, m, kc = a3.shape
    tk = min(tk, m)
    n = dc.shape[1]
    return _mm(a3, dc, grid=(nj, 1, m // tk), a_blk=(None, tk, kc), a_map=lambda j, _, kk: (j, kk, 0),
               b_blk=(tk, n), b_map=lambda j, _, kk: (kk, 0),
               o_blk=(2, kc // 2, n), o_map=lambda j, _, kk: (j, 0, 0), out_shape=(NDEV, kc // 2, n),
               out_dtype=BF16, dims=_MM_TA, acc_2d=(kc, n), name=name)


def _qkv_group_maps(d, ns, row_of, piece_of):
    per_arr, per_piece = d // LANES, ns // LANES

    def group_map(q):
        def f(*g):
            grp = piece_of(*g) * per_piece + q
            return grp // per_arr, row_of(*g), grp % per_arr
        return f

    return [group_map(q) for q in range(per_piece)]


def mm_qkv_da(d3, w4, l, name, tm=1024, dep=None):
    _, m, d = d3.shape
    tm = min(tm, m)
    k, ns = w4.shape[2], w4.shape[3]
    return _mm(d3, w4, grid=(m // tm, 1, NDEV),
               a_blk=(None, tm, LANES), a_map=_qkv_group_maps(d, ns, lambda i, _, j: i, lambda i, _, j: j),
               b_blk=(None, None, k, ns), b_map=lambda i, _, j: (j, l, 0, 0),
               o_blk=(tm, k), o_map=lambda i, _, j: (i, 0), out_shape=(m, k), out_dtype=F32,
               dims=_MM_TB, acc_2d=(tm, k), name=name, dep=dep)


def mm_qkv_dw(a, d3, ns, name, tk=2048):
    m, k = a.shape
    tk = min(tk, m)
    d = d3.shape[2]
    return _mm(a, d3, grid=(NDEV, 1, m // tk), a_blk=(tk, k), a_map=lambda j, _, kk: (kk, 0),
               b_blk=(None, tk, LANES), b_map=_qkv_group_maps(d, ns, lambda j, _, kk: kk, lambda j, _, kk: j),
               o_blk=(None, k, ns), o_map=lambda j, _, kk: (j, 0, 0),
               out_shape=(NDEV, k, ns), out_dtype=BF16, dims=_MM_TA, acc_2d=(k, ns), name=name)


def _rowwise(fn, ins, outs, *, tr, name, acc_outs=()):
    rows = next(a.shape[0] if kind == "row" else a.shape[1] for a, kind in ins if kind in ("row", "row3"))
    n_in, n_out = len(ins), len(outs)
    n_read = sum(kind != "dep" for _, kind in ins)

    def body(*refs):
        vals = fn(*[r[...] for r in refs[:n_read]])
        if not isinstance(vals, (tuple, list)):
            vals = (vals,)
        for ref, val in zip(refs[n_in:n_in + n_out], vals[:n_out]):
            ref[...] = val.astype(ref.dtype)
        i = pl.program_id(0)
        for ref, val in zip(refs[n_in + n_out:], vals[n_out:]):
            val = val.astype(ref.dtype)

            @pl.when(i == 0)
            def _(ref=ref, val=val):
                ref[...] = val

            @pl.when(i > 0)
            def _(ref=ref, val=val):
                ref[...] += val

    in_specs = []
    for a, kind in ins:
        if kind == "row":
            in_specs.append(pl.BlockSpec((tr, a.shape[1]), lambda i: (i, 0)))
        elif kind == "row3":
            in_specs.append(pl.BlockSpec((a.shape[0], tr, a.shape[2]), lambda i: (0, i, 0)))
        elif kind == "dep":
            in_specs.append(pl.BlockSpec(memory_space=pl.ANY))
        else:
            in_specs.append(pl.BlockSpec(a.shape, lambda i, nd=a.ndim: (0,) * nd))
    out_specs = [pl.BlockSpec((tr, c), lambda i: (i, 0)) for c, _ in outs]
    out_specs += [pl.BlockSpec(s, lambda i, nd=len(s): (0,) * nd) for s, _ in acc_outs]
    out_shape = [jax.ShapeDtypeStruct((rows, c), dt) for c, dt in outs]
    out_shape += [jax.ShapeDtypeStruct(s, dt) for s, dt in acc_outs]
    res = pl.pallas_call(
        body, grid=(rows // tr,), in_specs=in_specs, out_specs=out_specs, out_shape=out_shape, name=name,
        compiler_params=_params(("arbitrary",) if acc_outs else ("parallel",)),
    )(*[a for a, _ in ins])
    return res


def _rms(x, g):
    return x * lax.rsqrt(jnp.mean(x * x, axis=-1, keepdims=True) + EPS) * g


def cast_into_slot(w, l, me):
    _, r, c = w.shape
    tr = _rows(r, 512)

    def body(me_ref, w_ref, o_ref):
        o_ref[...] = w_ref[...].astype(o_ref.dtype)

    return pl.pallas_call(
        body,
        grid_spec=pltpu.PrefetchScalarGridSpec(
            num_scalar_prefetch=1, grid=(r // tr,),
            in_specs=[pl.BlockSpec((None, tr, c), lambda i, me_ref: (l, i, 0))],
            out_specs=pl.BlockSpec((None, tr, c), lambda i, me_ref: (me_ref[0], i, 0))),
        out_shape=jax.ShapeDtypeStruct((NDEV, r, c), BF16), name="cast_into_slot",
        compiler_params=_params(("parallel",)),
    )(me, w)


def rms_fwd(x, g, name):
    out, = _rowwise(_rms, [(x, "row"), (g, "full")], [(x.shape[1], BF16)], tr=ROW_BLOCK, name=name)
    return out


def rms_bwd(x, g, dy, dres, name):
    def fn(xv, gv, dyv, drv):
        _, vjp = jax.vjp(_rms, xv, gv)
        dx, dg = vjp(dyv.astype(F32))
        return drv + dx, dg

    d = x.shape[1]
    return _rowwise(fn, [(x, "row"), (g, "full"), (dy, "row"), (dres, "row")], [(d, F32)], tr=ROW_BLOCK, name=name,
                    acc_outs=[((1, d), F32)])


def loss_head(x, g, tgt):
    def f(xv, gv, tv):
        err = jnp.square(_rms(xv, gv) - tv)
        return 0.5 * jnp.sum(jnp.mean(err, axis=-1))

    def fn(xv, gv, tv):
        val, (dx, dg) = jax.value_and_grad(f, argnums=(0, 1))(xv, gv, tv)
        return dx, jnp.full((1, LANES), val, F32), dg

    d = x.shape[1]
    return _rowwise(fn, [(x, "row"), (g, "full"), (tgt, "row")], [(d, F32)], tr=ROW_BLOCK, name="loss_head",
                    acc_outs=[((1, LANES), F32), ((1, d), F32)])


def _glu(hg, x):
    half = hg.shape[1] // 2
    return x + hg[:, :half] * jax.nn.sigmoid(hg[:, half:])


def glu_fwd(hg, x, norm):
    def fn(h, xv, g):
        x1 = _glu(h.astype(F32), xv)
        return x1, _rms(x1, g)

    d = x.shape[1]
    return _rowwise(fn, [(hg, "row"), (x, "row"), (norm, "full")], [(d, F32), (d, BF16)], tr=ROW_BLOCK, name="glu_fwd")


def glu_bwd(hg, dx1, dep):
    def fn(h, d):
        _, vjp = jax.vjp(lambda hv: _glu(hv, jnp.zeros_like(d)), h.astype(F32))
        return vjp(d)[0]

    out, = _rowwise(fn, [(hg, "row"), (dx1, "row"), (dep, "dep")], [(hg.shape[1], BF16)], tr=ROW_BLOCK, name="glu_bwd")
    return out


def _s5_post(yc, u, d):
    return jax.nn.gelu(yc + d * u)


def s5_post_fwd(yc, u, d):
    out, = _rowwise(_s5_post, [(yc, "row"), (u, "row"), (d, "full")], [(yc.shape[1], BF16)], tr=ROW_BLOCK,
                    name="s5_post_fwd")
    return out


def s5_post_bwd(yc, u, d, dyg):
    def fn(ycv, uv, dv, g):
        _, vjp = jax.vjp(_s5_post, ycv, uv, dv)
        return vjp(g.astype(F32))

    dm = yc.shape[1]
    return _rowwise(fn, [(yc, "row"), (u, "row"), (d, "full"), (dyg, "row")], [(dm, F32), (dm, F32)], tr=ROW_BLOCK,
                    name="s5_post_bwd", acc_outs=[((1, dm), F32)])


def _adam_update(wv, mv, vv, g):
    m2 = ADAM_B1 * mv + (1.0 - ADAM_B1) * g
    v2 = ADAM_B2 * vv + (1.0 - ADAM_B2) * jnp.square(g)
    m_hat = m2 / (1.0 - ADAM_B1 ** ADAM_STEP)
    v_hat = v2 / (1.0 - ADAM_B2 ** ADAM_STEP)
    delta = -ADAM_LR * (m_hat / (jnp.sqrt(v_hat) + ADAM_EPS) + ADAM_WD * wv)
    return g, delta, m2, v2


def adamw(w, m, v, g_parts, name):
    def fn(wv, mv, vv, gp):
        g = gp[0].astype(F32)
        for p in range(1, gp.shape[0]):
            g = g + gp[p].astype(F32)
        return _adam_update(wv, mv, vv, g)

    c = w.shape[1]
    return _rowwise(fn, [(w, "row"), (m, "row"), (v, "row"), (g_parts, "row3")], [(c, F32)] * 4,
                    tr=_rows(w.shape[0], 256), name=name)


def adamw_layers(w, m, v, lands, owns, me, name):
    nl, r, c = w.shape
    tr = _rows(r, 256)

    def body(me_ref, w_ref, m_ref, v_ref, *rest):
        land_refs, own_refs, out_refs = rest[:nl], rest[nl:2 * nl], rest[2 * nl:]
        for l in range(nl):
            @pl.when(pl.program_id(0) == l)
            def _(l=l):
                g = own_refs[l][...].astype(F32)
                for p in range(NDEV - 1):
                    g = g + land_refs[l][p].astype(F32)
                for ref, val in zip(out_refs, _adam_update(w_ref[...], m_ref[...], v_ref[...], g)):
                    ref[...] = val

    def rows_of(l):
        return lambda li, i, me_ref: jnp.where(li == l, i, 0)

    wspec = pl.BlockSpec((None, tr, c), lambda li, i, me_ref: (li, i, 0))
    in_specs = [wspec] * 3
    in_specs += [pl.BlockSpec((NDEV - 1, tr, c), lambda li, i, me_ref, f=rows_of(l): (0, f(li, i, me_ref), 0))
                 for l in range(nl)]
    in_specs += [pl.BlockSpec((None, tr, c), lambda li, i, me_ref, f=rows_of(l): (me_ref[0], f(li, i, me_ref), 0))
                 for l in range(nl)]
    return pl.pallas_call(
        body,
        grid_spec=pltpu.PrefetchScalarGridSpec(
            num_scalar_prefetch=1, grid=(nl, r // tr), in_specs=in_specs, out_specs=[wspec] * 4),
        out_shape=[jax.ShapeDtypeStruct(w.shape, F32)] * 4, name=name, compiler_params=_params(("parallel", "parallel")),
    )(me, w, m, v, *lands, *owns)


def _conv_rows(cur, halo, p, first):
    r = cur.shape[0]
    ext = jnp.concatenate([jnp.where(first, 0.0, halo), cur], axis=0)
    s1 = pltpu.roll(ext, 1, 0)[HALO:]
    s2 = pltpu.roll(ext, 2, 0)[HALO:]
    return p[0:1] * s2 + p[1:2] * s1 + p[2:3] * cur + p[3:4], s1, s2


def ffn_gate_fwd(h3, p3, tr=ROW_BLOCK):
    _, t, c = h3.shape
    half = NDEV // 2

    def body(a_ref, ah_ref, g_ref, gh_ref, pa_ref, pg_ref, o_ref):
        first = pl.program_id(1) == 0
        ya, _, _ = _conv_rows(a_ref[...].astype(F32), ah_ref[...].astype(F32), pa_ref[...], first)
        yg, _, _ = _conv_rows(g_ref[...].astype(F32), gh_ref[...].astype(F32), pg_ref[...], first)
        o_ref[...] = (jax.nn.silu(yg) * ya).astype(o_ref.dtype)

    main = lambda off: pl.BlockSpec((None, tr, c), lambda j, i: (j + off, i, 0))
    halo = lambda off: pl.BlockSpec((None, HALO, c), lambda j, i: (j + off, jnp.maximum(i * (tr // HALO) - 1, 0), 0))
    par = lambda off: pl.BlockSpec((None, 8, c), lambda j, i: (j + off, 0, 0))
    return pl.pallas_call(
        body, grid=(half, t // tr),
        in_specs=[main(0), halo(0), main(half), halo(half), par(0), par(half)],
        out_specs=pl.BlockSpec((None, tr, c), lambda j, i: (j, i, 0)),
        out_shape=jax.ShapeDtypeStruct((half, t, c), BF16), name="ffn_gate_fwd",
        compiler_params=_params(("parallel", "parallel")),
    )(h3, h3, h3, h3, p3, p3)


def ffn_gate_bwd(h3, dgated3, p3, tr=ROW_BLOCK):
    _, t, c = h3.shape
    half = NDEV // 2

    def body(a_ref, ah_ref, g_ref, gh_ref, dg_ref, pa_ref, pg_ref, dya_ref, dyg_ref, dpa_ref, dpg_ref):
        i = pl.program_id(1)
        first = i == 0
        a = a_ref[...].astype(F32)
        g = g_ref[...].astype(F32)
        ya, a1, a2 = _conv_rows(a, ah_ref[...].astype(F32), pa_ref[...], first)
        yg, g1, g2 = _conv_rows(g, gh_ref[...].astype(F32), pg_ref[...], first)
        d = dg_ref[...].astype(F32)
        sig = jax.nn.sigmoid(yg)
        d_ya = d * (yg * sig)
        d_yg = d * ya * (sig * (1.0 + yg * (1.0 - sig)))
        dya_ref[...] = d_ya.astype(dya_ref.dtype)
        dyg_ref[...] = d_yg.astype(dyg_ref.dtype)
        for dy, cur, s1, s2, dp_ref in ((d_ya, a, a1, a2, dpa_ref), (d_yg, g, g1, g2, dpg_ref)):
            rows = [jnp.sum(dy * s2, axis=0, keepdims=True), jnp.sum(dy * s1, axis=0, keepdims=True),
                    jnp.sum(dy * cur, axis=0, keepdims=True), jnp.sum(dy, axis=0, keepdims=True)]
            dp = jnp.concatenate(rows + [jnp.zeros((4, c), F32)], axis=0)

            @pl.when(first)
            def _(dp_ref=dp_ref, dp=dp):
                dp_ref[...] = dp

            @pl.when(i > 0)
            def _(dp_ref=dp_ref, dp=dp):
                dp_ref[...] += dp

    main = lambda off: pl.BlockSpec((None, tr, c), lambda j, i: (j + off, i, 0))
    halo = lambda off: pl.BlockSpec((None, HALO, c), lambda j, i: (j + off, jnp.maximum(i * (tr // HALO) - 1, 0), 0))
    par = lambda off: pl.BlockSpec((None, 8, c), lambda j, i: (j + off, 0, 0))
    return pl.pallas_call(
        body, grid=(half, t // tr),
        in_specs=[main(0), halo(0), main(half), halo(half), main(0), par(0), par(half)],
        out_specs=[main(0), main(0), par(0), par(0)],
        out_shape=[jax.ShapeDtypeStruct((half, t, c), BF16)] * 2 + [jax.ShapeDtypeStruct((half, 8, c), F32)] * 2,
        name="ffn_gate_bwd", compiler_params=_params(("parallel", "arbitrary")),
    )(h3, h3, h3, h3, dgated3, p3, p3)


def ffn_conv_t(dy_a, dy_g, p3, tr=2 * ROW_BLOCK):
    half, t, c = dy_a.shape
    tr = min(tr, t)
    nblk = t // tr

    def body(a_ref, ah_ref, g_ref, gh_ref, p_ref, o_ref):
        is_a = pl.program_id(0) < half
        last = pl.program_id(1) == nblk - 1
        cur = jnp.where(is_a, a_ref[...], g_ref[...]).astype(F32)
        nxt = jnp.where(is_a, ah_ref[...], gh_ref[...]).astype(F32)
        ext = jnp.concatenate([cur, jnp.where(last, 0.0, nxt)], axis=0)
        n = tr + HALO
        s1 = pltpu.roll(ext, n - 1, 0)[:tr]
        s2 = pltpu.roll(ext, n - 2, 0)[:tr]
        p = p_ref[...]
        o_ref[...] = (p[2:3] * cur + p[1:2] * s1 + p[0:1] * s2).astype(o_ref.dtype)

    main = pl.BlockSpec((None, tr, c), lambda j, i: (j % half, i, 0))
    halo = pl.BlockSpec((None, HALO, c), lambda j, i: (j % half, jnp.minimum((i + 1) * (tr // HALO), t // HALO - 1), 0))
    return pl.pallas_call(
        body, grid=(NDEV, nblk),
        in_specs=[main, halo, main, halo, pl.BlockSpec((None, 8, c), lambda j, i: (j, 0, 0))],
        out_specs=pl.BlockSpec((None, tr, c), lambda j, i: (j, i, 0)),
        out_shape=jax.ShapeDtypeStruct((NDEV, t, c), BF16), name="ffn_conv_t",
        compiler_params=_params(("parallel", "parallel")),
    )(dy_a, dy_a, dy_g, dy_g, p3)


def _att_consts(bq, bk):
    lane = lax.broadcasted_iota(jnp.int32, (1, LANES), 1)
    heads = (lane < HEAD_DIM, lane >= HEAD_DIM)
    rr = lax.broadcasted_iota(jnp.int32, (bq, bk), 0)
    cc = lax.broadcasted_iota(jnp.int32, (bq, bk), 1)
    kr = lax.broadcasted_iota(jnp.int32, (bk, bk), 0)
    kc = lax.broadcasted_iota(jnp.int32, (bk, bk), 1)
    return heads, rr, cc, kr, kc


def _split_dot(x, tri, parts):
    out = None
    for _ in range(parts):
        piece = x.astype(BF16)
        x = x - piece.astype(F32)
        term = jnp.dot(piece, tri, preferred_element_type=F32)
        out = term if out is None else out + term
    return out


def _att_logits(qh, k):
    z = lax.dot_general(qh, k, _MM_TB, preferred_element_type=F32)
    lsp = jnp.minimum(z, 0.0) - jnp.log(1.0 + jnp.exp(-jnp.abs(z)))
    return lsp, lsp - z


def _per_head(heads, a, b):
    return jnp.where(heads[0], a, b)


def sb_attn_fwd(qkv):
    t, d3 = qkv.shape
    d = d3 // 3
    npair = d // LANES
    bq, bk = min(ATT_BQ, t), min(ATT_BK, t)
    kpq = bq // bk

    def body(q_ref, k_ref, v_ref, o_ref, lt_ref, acc_ref):
        heads, rr, cc, kr, kc = _att_consts(bq, bk)
        suffix = (kr > kc).astype(BF16)

        def trip(qh, k0, r0, runs):
            k = k_ref[pl.ds(k0, bk), :]
            v = v_ref[pl.ds(k0, bk), :]
            diag, r0 = r0 is not None, r0 or 0
            valid = cc[:bq - r0] < rr[:bq - r0]
            new_runs = []
            for h in range(2):
                lsp, lraw = _att_logits(qh[h][r0:], k)
                lm = jnp.where(valid, lraw, 0.0) if diag else lraw
                w = jnp.exp(lsp + _split_dot(lm, suffix, 2) + runs[h][r0:])
                if diag:
                    w = jnp.where(valid, w, 0.0)
                acc_ref[h, r0:, :] += jnp.dot(w.astype(BF16), v, preferred_element_type=F32)
                below = runs[h][r0:] + jnp.sum(lm, axis=1, keepdims=True)
                new_runs.append(jnp.concatenate([runs[h][:r0], below], axis=0) if r0 else below)
            return tuple(new_runs)

        def q_loop(qb, _):
            q0 = pl.multiple_of(qb * bq, bq)
            q = q_ref[pl.ds(q0, bq), :] * 0.125
            qh = [jnp.where(hm, q, 0.0).astype(BF16) for hm in heads]
            acc_ref[...] = jnp.zeros_like(acc_ref)
            runs = (jnp.zeros((bq, 1), F32),) * 2
            for dblk in reversed(range(kpq)):
                runs = trip(qh, pl.multiple_of(q0 + dblk * bk, bk), dblk * bk, runs)
            nleft = qb * kpq
            runs = lax.fori_loop(
                0, nleft, lambda i, r: trip(qh, pl.multiple_of((nleft - 1 - i) * bk, bk), None, r), runs)
            o_ref[pl.ds(q0, bq), :] = _per_head(heads, acc_ref[0], acc_ref[1])
            lt_ref[pl.ds(q0, bq), :] = _per_head(heads, runs[0], runs[1])
            return 0

        lax.fori_loop(0, t // bq, q_loop, 0)

    col = lambda off: pl.BlockSpec((t, LANES), lambda p: (0, p + off))
    return pl.pallas_call(
        body, grid=(npair,), in_specs=[col(0), col(npair), col(2 * npair)], out_specs=[col(0), col(0)],
        out_shape=[jax.ShapeDtypeStruct((t, d), F32)] * 2, scratch_shapes=[pltpu.VMEM((2, bq, LANES), F32)],
        name="sb_attn_fwd", compiler_params=_params(("parallel",)),
    )(qkv, qkv, qkv)


def sb_attn_bwd(qkv, ltot, do):
    t, d3 = qkv.shape
    d = d3 // 3
    npair = d // LANES
    bq, bk = min(ATT_BQ, t), min(ATT_BK, t)
    kpq = bq // bk

    def body(q_ref, k_ref, v_ref, lt_ref, do_ref, d_ref, dk_acc, dv_acc, dq_acc):
        heads, rr, cc, kr, kc = _att_consts(bq, bk)
        prefix_incl = (kr <= kc).astype(BF16)
        prefix_excl = (kr < kc).astype(BF16)
        dk_acc[...] = jnp.zeros_like(dk_acc)
        dv_acc[...] = jnp.zeros_like(dv_acc)

        def trip(qh, doh, lt, k0, r0, carry):
            lruns, gruns = carry
            k = k_ref[pl.ds(k0, bk), :]
            v = v_ref[pl.ds(k0, bk), :]
            diag, r0 = r0 is not None, r0 or 0
            valid = cc[:bq - r0] < rr[:bq - r0]
            new_lruns, new_gruns = [], []
            dk_blk = jnp.zeros((bk, LANES), F32)
            dv_blk = jnp.zeros((bk, LANES), F32)
            for h in range(2):
                q_rows, do_rows = qh[h][r0:], doh[h][r0:]
                lsp, lraw = _att_logits(q_rows, k)
                lm = jnp.where(valid, lraw, 0.0) if diag else lraw
                right = lt[h][r0:] - (lruns[h][r0:] + _split_dot(lm, prefix_incl, 2))
                w = jnp.exp(lsp + right)
                if diag:
                    w = jnp.where(valid, w, 0.0)
                g = lax.dot_general(do_rows, v, _MM_TB, preferred_element_type=F32) * w
                left = gruns[h][r0:] + _split_dot(g, prefix_excl, 1)
                dz = g * jnp.exp(lraw) - jnp.exp(lsp) * left
                if diag:
                    dz = jnp.where(valid, dz, 0.0)
                dz = dz.astype(BF16)
                kh = jnp.where(heads[h], k, 0.0).astype(BF16)
                dq_acc[h, r0:, :] += jnp.dot(dz, kh, preferred_element_type=F32)
                dk_blk = dk_blk + lax.dot_general(dz, q_rows, _MM_TA, preferred_element_type=F32)
                dv_blk = dv_blk + lax.dot_general(w.astype(BF16), do_rows, _MM_TA, preferred_element_type=F32)
                l_below = lruns[h][r0:] + jnp.sum(lm, axis=1, keepdims=True)
                g_below = gruns[h][r0:] + jnp.sum(g, axis=1, keepdims=True)
                new_lruns.append(jnp.concatenate([lruns[h][:r0], l_below], axis=0) if r0 else l_below)
                new_gruns.append(jnp.concatenate([gruns[h][:r0], g_below], axis=0) if r0 else g_below)
            dk_acc[pl.ds(k0, bk), :] += dk_blk
            dv_acc[pl.ds(k0, bk), :] += dv_blk
            return tuple(new_lruns), tuple(new_gruns)

        def q_loop(qb, _):
            q0 = pl.multiple_of(qb * bq, bq)
            q = q_ref[pl.ds(q0, bq), :] * 0.125
            dout = do_ref[pl.ds(q0, bq), :]
            lt2 = lt_ref[pl.ds(q0, bq), :]
            qh = [jnp.where(hm, q, 0.0).astype(BF16) for hm in heads]
            doh = [jnp.where(hm, dout, 0.0).astype(BF16) for hm in heads]
            lt = [jnp.max(jnp.where(hm, lt2, -jnp.inf), axis=1, keepdims=True) for hm in heads]
            dq_acc[...] = jnp.zeros_like(dq_acc)
            col = (jnp.zeros((bq, 1), F32),) * 2
            carry = lax.fori_loop(
                0, qb * kpq, lambda kb, c: trip(qh, doh, lt, pl.multiple_of(kb * bk, bk), None, c), (col, col))
            for dblk in range(kpq):
                carry = trip(qh, doh, lt, pl.multiple_of(q0 + dblk * bk, bk), dblk * bk, carry)
            d_ref[0, pl.ds(q0, bq), :] = ((dq_acc[0] + dq_acc[1]) * 0.125).astype(d_ref.dtype)
            return 0

        lax.fori_loop(0, t // bq, q_loop, 0)
        d_ref[1] = dk_acc[...].astype(d_ref.dtype)
        d_ref[2] = dv_acc[...].astype(d_ref.dtype)

    col = lambda off: pl.BlockSpec((t, LANES), lambda p: (0, p + off))
    return pl.pallas_call(
        body, grid=(npair,), in_specs=[col(0), col(npair), col(2 * npair), col(0), col(0)],
        out_specs=pl.BlockSpec((3, t, LANES), lambda p: (0, 0, p)),
        out_shape=jax.ShapeDtypeStruct((3, t, d), BF16),
        scratch_shapes=[pltpu.VMEM((t, LANES), F32), pltpu.VMEM((t, LANES), F32), pltpu.VMEM((2, bq, LANES), F32)],
        name="sb_attn_bwd", compiler_params=_params(("parallel",)),
    )(qkv, qkv, qkv, ltot, do)


def _sgu_parts(hin, g, ws_ref, bf_ref):
    width = hin.shape[1] // 2
    h = jax.nn.gelu(hin)
    u, v = h[:, :width], h[:, width:]
    r = lax.rsqrt(jnp.mean(v * v, axis=-1, keepdims=True) + EPS)
    vn = v * r * g
    rr = lax.broadcasted_iota(jnp.int32, (CHUNK, CHUNK), 0)
    cc = lax.broadcasted_iota(jnp.int32, (CHUNK, CHUNK), 1)
    causal = cc <= rr
    wcs = [jnp.where(causal, ws_ref[gi], 0.0).astype(BF16) for gi in range(SG_GROUPS)]
    sv = jnp.concatenate(
        [jnp.dot(wcs[gi], vn[:, gi * CHUNK:(gi + 1) * CHUNK].astype(BF16), preferred_element_type=F32) + bf_ref[gi]
         for gi in range(SG_GROUPS)], axis=1)
    return u, v, r, vn, wcs, sv, causal


def sgu_fwd(hin, g, ws, bfull):
    t, w2 = hin.shape
    width = w2 // 2

    def body(h_ref, g_ref, ws_ref, bf_ref, o_ref):
        u, _, _, _, _, sv, _ = _sgu_parts(h_ref[...].astype(F32), g_ref[...], ws_ref, bf_ref)
        o_ref[...] = (u * sv).astype(o_ref.dtype)

    full = lambda a: pl.BlockSpec(a.shape, lambda i, nd=a.ndim: (0,) * nd)
    return pl.pallas_call(
        body, grid=(t // CHUNK,), in_specs=[pl.BlockSpec((CHUNK, w2), lambda i: (i, 0)), full(g), full(ws), full(bfull)],
        out_specs=pl.BlockSpec((CHUNK, width), lambda i: (i, 0)), out_shape=jax.ShapeDtypeStruct((t, width), BF16),
        name="sgu_fwd", compiler_params=_params(("parallel",)),
    )(hin, g, ws, bfull)


def sgu_bwd(hin, dp, g, ws, bfull):
    t, w2 = hin.shape
    width = w2 // 2

    def body(h_ref, dp_ref, g_ref, ws_ref, bf_ref, dh_ref, dws_ref, dbf_ref, dg_ref):
        i = pl.program_id(0)
        hin_v = h_ref[...].astype(F32)
        gv = g_ref[...]
        u, v, r, vn, wcs, sv, causal = _sgu_parts(hin_v, gv, ws_ref, bf_ref)
        dpv = dp_ref[...].astype(F32)
        du = dpv * sv
        dsv = dpv * u
        dvn_parts, dws_parts, dbf_parts = [], [], []
        for gi in range(SG_GROUPS):
            dsv_g = dsv[:, gi * CHUNK:(gi + 1) * CHUNK]
            dsv_b = dsv_g.astype(BF16)
            dvn_parts.append(lax.dot_general(wcs[gi], dsv_b, _MM_TA, preferred_element_type=F32))
            vn_b = vn[:, gi * CHUNK:(gi + 1) * CHUNK].astype(BF16)
            dws_parts.append(jnp.where(causal, lax.dot_general(dsv_b, vn_b, _MM_TB, preferred_element_type=F32), 0.0))
            dbf_parts.append(jnp.broadcast_to(jnp.sum(dsv_g, axis=1, keepdims=True), (CHUNK, CHUNK)))
        dvn = jnp.concatenate(dvn_parts, axis=1)
        dgain = jnp.sum(dvn * v * r, axis=0, keepdims=True)
        gvv = dvn * gv
        dv = r * gvv - v * (r * r * r) * jnp.mean(v * gvv, axis=-1, keepdims=True)
        _, vjp = jax.vjp(jax.nn.gelu, hin_v)
        dh_ref[...] = vjp(jnp.concatenate([du, dv], axis=1))[0].astype(dh_ref.dtype)

        @pl.when(i == 0)
        def _():
            for gi in range(SG_GROUPS):
                dws_ref[gi] = dws_parts[gi]
                dbf_ref[gi] = dbf_parts[gi]
            dg_ref[...] = dgain

        @pl.when(i > 0)
        def _():
            for gi in range(SG_GROUPS):
                dws_ref[gi] += dws_parts[gi]
                dbf_ref[gi] += dbf_parts[gi]
            dg_ref[...] += dgain

    full = lambda a: pl.BlockSpec(a.shape, lambda i, nd=a.ndim: (0,) * nd)
    sq = (SG_GROUPS, CHUNK, CHUNK)
    return pl.pallas_call(
        body, grid=(t // CHUNK,),
        in_specs=[pl.BlockSpec((CHUNK, w2), lambda i: (i, 0)), pl.BlockSpec((CHUNK, width), lambda i: (i, 0)),
                  full(g), full(ws), full(bfull)],
        out_specs=[pl.BlockSpec((CHUNK, w2), lambda i: (i, 0)), pl.BlockSpec(sq, lambda i: (0, 0, 0)),
                   pl.BlockSpec(sq, lambda i: (0, 0, 0)), pl.BlockSpec((1, width), lambda i: (0, 0))],
        out_shape=[jax.ShapeDtypeStruct((t, w2), BF16), jax.ShapeDtypeStruct(sq, F32), jax.ShapeDtypeStruct(sq, F32),
                   jax.ShapeDtypeStruct((1, width), F32)],
        name="sgu_bwd", compiler_params=_params(("arbitrary",)),
    )(hin, dp, g, ws, bfull)


def _disc1(lam_re, lam_im, log_dt):
    lr = jnp.minimum(lam_re, -1e-4)
    li = lam_im
    dt = jnp.exp(log_dt)
    mag = jnp.exp(dt * lr)
    ar = mag * jnp.cos(dt * li)
    ai = mag * jnp.sin(dt * li)
    den = lr * lr + li * li
    return ar, ai, ((ar - 1.0) * lr + ai * li) / den, (ai * lr - (ar - 1.0) * li) / den


def _disc2(cre, cim, b_re, b_im):
    return cre * b_re - cim * b_im, cre * b_im + cim * b_re


def _single(fn, ins, out_shapes, name):
    n = len(ins)

    def body(*refs):
        vals = fn(*[r[...] for r in refs[:n]])
        for ref, val in zip(refs[n:], vals):
            ref[...] = val

    return pl.pallas_call(body, out_shape=[jax.ShapeDtypeStruct(s, F32) for s in out_shapes], name=name)(*ins)


SCAN_SEGMENTS = 8
REORDER_STEPS = 64


def s5_reorder(x, to_steps):
    t, d = x.shape
    ns = SCAN_SEGMENTS
    seg = t // ns
    ts = min(REORDER_STEPS, seg)
    by_segment = ((ns, seg, d), pl.BlockSpec((ns, ts, d), lambda i: (0, i, 0)))
    by_step = ((seg, ns, d), pl.BlockSpec((ts, ns, d), lambda i: (i, 0, 0)))
    (in_shape, in_spec), (out_shape, out_spec) = (by_segment, by_step) if to_steps else (by_step, by_segment)

    def body(x_ref, o_ref):
        o_ref[...] = jnp.swapaxes(x_ref[...], 0, 1)

    out = pl.pallas_call(
        body, grid=(seg // ts,), in_specs=[in_spec], out_specs=out_spec,
        out_shape=jax.ShapeDtypeStruct(out_shape, x.dtype), name="s5_reorder", compiler_params=_params(("parallel",)),
    )(x.reshape(in_shape))
    return out.reshape(t, d)


def _cpow(ar, ai, n):
    rr, ri = None, None
    while n:
        if n & 1:
            rr, ri = (ar, ai) if rr is None else (rr * ar - ri * ai, rr * ai + ri * ar)
        ar, ai = ar * ar - ai * ai, 2.0 * ar * ai
        n >>= 1
    return rr, ri


def _edge_states(er, ei, pr, pi, reverse):
    ns = SCAN_SEGMENTS
    zero = jnp.zeros_like(er[0:1])
    rows_r, rows_i = [None] * ns, [None] * ns
    order = range(ns - 1, -1, -1) if reverse else range(ns)
    prev = None
    for s in order:
        if prev is None:
            rows_r[s], rows_i[s] = zero, zero
        else:
            cr, ci = rows_r[prev], rows_i[prev]
            rows_r[s] = er[prev:prev + 1] + pr * cr - pi * ci
            rows_i[s] = ei[prev:prev + 1] + pr * ci + pi * cr
        prev = s
    return jnp.concatenate(rows_r, axis=0), jnp.concatenate(rows_i, axis=0)


def s5_scan_fwd(bu2, a2):
    _, t, n = bu2.shape
    cb, ns = SCAN_COLS, SCAN_SEGMENTS
    seg = t // ns

    def body(bu_ref, a_ref, x_ref):
        ar, ai = a_ref[0:1, :], a_ref[1:2, :]

        def local(i, carry):
            xr, xi = carry
            xr, xi = ar * xr - ai * xi + bu_ref[0, i], ar * xi + ai * xr + bu_ref[1, i]
            x_ref[0, i] = xr
            x_ref[1, i] = xi
            return xr, xi

        zero = jnp.zeros((ns, cb), F32)
        er, ei = lax.fori_loop(0, seg, local, (zero, zero))
        cr, ci = _edge_states(er, ei, *_cpow(ar, ai, seg), reverse=False)

        def fix(i, carry):
            wr, wi = carry
            wr, wi = wr * ar - wi * ai, wr * ai + wi * ar
            x_ref[0, i] += wr * cr - wi * ci
            x_ref[1, i] += wr * ci + wi * cr
            return wr, wi

        lax.fori_loop(0, seg, fix, (jnp.ones((1, cb), F32), jnp.zeros((1, cb), F32)))

    blk = pl.BlockSpec((2, seg, ns, cb), lambda j: (0, 0, 0, j))
    out = pl.pallas_call(
        body, grid=(n // cb,), in_specs=[blk, pl.BlockSpec((2, cb), lambda j: (0, j))], out_specs=blk,
        out_shape=jax.ShapeDtypeStruct((2, seg, ns, n), F32), name="s5_scan_fwd", compiler_params=_params(("parallel",)),
    )(bu2.reshape(2, seg, ns, n), a2)
    return out.reshape(2, t, n)


def s5_scan_bwd(dx2, x2, a2):
    _, t, n = dx2.shape
    cb, ns = SCAN_COLS, SCAN_SEGMENTS
    seg = t // ns

    def body(dx_ref, x_ref, a_ref, g_ref, da_ref):
        ar, ai = a_ref[0:1, :], a_ref[1:2, :]

        def local(s, carry):
            gr, gi = carry
            i = seg - 1 - s
            gr, gi = dx_ref[0, i] + ar * gr + ai * gi, dx_ref[1, i] - ai * gr + ar * gi
            g_ref[0, i] = gr
            g_ref[1, i] = gi
            return gr, gi

        zero = jnp.zeros((ns, cb), F32)
        er, ei = lax.fori_loop(0, seg, local, (zero, zero))
        cr, ci = _edge_states(er, ei, *_cpow(ar, -ai, seg), reverse=True)
        row = lax.broadcasted_iota(jnp.int32, (ns, cb), 0)
        before_r = jnp.where(row == 0, 0.0, pltpu.roll(x_ref[0, seg - 1], 1, 0))
        before_i = jnp.where(row == 0, 0.0, pltpu.roll(x_ref[1, seg - 1], 1, 0))

        def fix(s, carry):
            wr, wi, dar, dai = carry
            i = seg - 1 - s
            wr, wi = wr * ar + wi * ai, wi * ar - wr * ai
            gr = g_ref[0, i] + wr * cr - wi * ci
            gi = g_ref[1, i] + wr * ci + wi * cr
            g_ref[0, i] = gr
            g_ref[1, i] = gi
            ip = jnp.maximum(i - 1, 0)
            xpr = jnp.where(i == 0, before_r, x_ref[0, ip])
            xpi = jnp.where(i == 0, before_i, x_ref[1, ip])
            return wr, wi, dar + gr * xpr + gi * xpi, dai + gi * xpr - gr * xpi

        one, z1 = jnp.ones((1, cb), F32), jnp.zeros((1, cb), F32)
        _, _, dar, dai = lax.fori_loop(0, seg, fix, (one, z1, zero, zero))
        da_ref[0:1, :] = jnp.sum(dar, axis=0, keepdims=True)
        da_ref[1:2, :] = jnp.sum(dai, axis=0, keepdims=True)

    blk = pl.BlockSpec((2, seg, ns, cb), lambda j: (0, 0, 0, j))
    vec = pl.BlockSpec((2, cb), lambda j: (0, j))
    g4, da = pl.pallas_call(
        body, grid=(n // cb,), in_specs=[blk, blk, vec], out_specs=[blk, vec],
        out_shape=[jax.ShapeDtypeStruct((2, seg, ns, n), F32), jax.ShapeDtypeStruct((2, n), F32)],
        name="s5_scan_bwd", compiler_params=_params(("parallel",)),
    )(dx2.reshape(2, seg, ns, n), x2.reshape(2, seg, ns, n), a2)
    return g4.reshape(2, t, n), da


_SP_U = SSM_PACK * SSM_GROUP
_SP_X = SSM_PACK * SSM_STATE
_NKB = SSM_GROUPS // SSM_PACK


def mm_s5(kind, a, b, m, name, res=None, tm=2048):
    tm = min(tm, m)
    kw = dict(passes=S5_PASSES, name=name)
    xblk = lambda row, sel, col: ((None, tm, _SP_X), lambda *g: (sel(*g), row(*g), col(*g)))
    if kind == "bu":
        o_blk, o_map = xblk(lambda g, i, k: i, lambda g, i, k: g // _NKB, lambda g, i, k: g % _NKB)
        return _mm(a, b, grid=(2 * _NKB, m // tm, 1), a_blk=(tm, _SP_U), a_map=lambda g, i, k: (i, g % _NKB),
                   b_blk=(None, None, _SP_U, _SP_X), b_map=lambda g, i, k: (g // _NKB, g % _NKB, 0, 0),
                   o_blk=o_blk, o_map=o_map, out_shape=(2, m, _NKB * _SP_X), out_dtype=F32, **kw)
    if kind == "yc":
        a_blk, a_map = xblk(lambda j, i, k: i, lambda j, i, k: k, lambda j, i, k: j)
        return _mm(a, b, grid=(_NKB, m // tm, 2), a_blk=a_blk, a_map=a_map,
                   b_blk=(None, None, _SP_X, _SP_U), b_map=lambda j, i, k: (k, j, 0, 0),
                   o_blk=(tm, _SP_U), o_map=lambda j, i, k: (i, j), out_shape=(m, _NKB * _SP_U), out_dtype=F32,
                   acc_2d=(tm, _SP_U), **kw)
    if kind == "dx":
        o_blk, o_map = xblk(lambda g, i, k: i, lambda g, i, k: g // _NKB, lambda g, i, k: g % _NKB)
        return _mm(a, b, grid=(2 * _NKB, m // tm, 1), a_blk=(tm, _SP_U), a_map=lambda g, i, k: (i, g % _NKB),
                   b_blk=(None, None, _SP_X, _SP_U), b_map=lambda g, i, k: (g // _NKB, g % _NKB, 0, 0),
                   o_blk=o_blk, o_map=o_map, out_shape=(2, m, _NKB * _SP_X), out_dtype=F32, dims=_MM_TB, **kw)
    if kind == "dcd":
        a_blk, a_map = xblk(lambda g, _, k: k, lambda g, _, k: g // _NKB, lambda g, _, k: g % _NKB)
        return _mm(a, b, grid=(2 * _NKB, 1, m // tm), a_blk=a_blk, a_map=a_map,
                   b_blk=(tm, _SP_U), b_map=lambda g, _, k: (k, g % _NKB),
                   o_blk=(None, None, _SP_X, _SP_U), o_map=lambda g, _, k: (g // _NKB, g % _NKB, 0, 0),
                   out_shape=(2, _NKB, _SP_X, _SP_U), out_dtype=F32, dims=_MM_TA, acc_2d=(_SP_X, _SP_U), **kw)
    if kind == "du":
        a_blk, a_map = xblk(lambda j, i, k: i, lambda j, i, k: k, lambda j, i, k: j)
        return _mm(a, b, grid=(_NKB, m // tm, 2), a_blk=a_blk, a_map=a_map,
                   b_blk=(None, None, _SP_U, _SP_X), b_map=lambda j, i, k: (k, j, 0, 0),
                   o_blk=(tm, _SP_U), o_map=lambda j, i, k: (i, j), out_shape=(m, _NKB * _SP_U), out_dtype=F32,
                   dims=_MM_TB, acc_2d=(tm, _SP_U), res=res, res_blk=(tm, _SP_U), res_map=lambda j, i, k: (i, j), **kw)
    assert kind == "dbd"
    b_blk, b_map = xblk(lambda g, _, k: k, lambda g, _, k: g // _NKB, lambda g, _, k: g % _NKB)
    return _mm(a, b, grid=(2 * _NKB, 1, m // tm), a_blk=(tm, _SP_U), a_map=lambda g, _, k: (k, g % _NKB),
               b_blk=b_blk, b_map=b_map,
               o_blk=(None, None, _SP_U, _SP_X), o_map=lambda g, _, k: (g // _NKB, g % _NKB, 0, 0),
               out_shape=(2, _NKB, _SP_U, _SP_X), out_dtype=F32, dims=_MM_TA, acc_2d=(_SP_U, _SP_X), **kw)


def _block_diag(w):
    g, a, b = w.shape
    eye = jnp.eye(SSM_PACK, dtype=w.dtype)
    wp = w.reshape(g // SSM_PACK, SSM_PACK, a, b)
    return jnp.einsum("kgab,gh->kgahb", wp, eye).reshape(g // SSM_PACK, SSM_PACK * a, SSM_PACK * b)


def _block_diag_t(d, a, b):
    k = d.shape[0]
    eye = jnp.eye(SSM_PACK, dtype=d.dtype)
    dp = d.reshape(k, SSM_PACK, a, SSM_PACK, b)
    return jnp.einsum("kgahb,gh->kgab", dp, eye).reshape(k * SSM_PACK, a, b)


def _coords():
    return lax.axis_index("x"), lax.axis_index("y"), lax.axis_index("c")


def all_gather(tensors, name, scatter=None):
    n = len(tensors)
    ns = 0 if scatter is None else 1
    any_spec = pl.BlockSpec(memory_space=pl.ANY)

    def body(*refs):
        ins, outs = refs[:n], refs[n + ns:2 * n + ns]
        send, recv, local = refs[2 * (n + ns):2 * (n + ns) + 3]
        x, y, c = _coords()
        me, sibling = (x, y, c), (x, y, 1 - c)
        chips = [(1 - x, y), (x, 1 - y), (1 - x, 1 - y)]

        def slot(p):
            return 4 * p[0] + 2 * p[1] + p[2]

        def copy(t, k, block, to, src=None):
            dst = outs[t].at[slot(block)]
            return pltpu.make_async_remote_copy(
                src_ref=dst if src is None else src, dst_ref=dst, send_sem=send.at[7 * t + k],
                recv_sem=recv.at[7 * t + k], device_id=to, device_id_type=pl.DeviceIdType.MESH)

        own, sent, landing = [], [], []
        for t in range(n):
            mine = pltpu.make_async_copy(ins[t], outs[t].at[slot(me)], local.at[t])
            mine.start()
            own.append(mine)
            first = [copy(t, 0, me, sibling, src=ins[t])]
            first += [copy(t, 1 + j, me, (*chip, c), src=ins[t]) for j, chip in enumerate(chips)]
            for cp in first:
                cp.start()
            sent += first
        if ns:
            src, dst = refs[n], refs[2 * n + ns]
            s_send, s_recv = refs[2 * (n + ns) + 3:]
            my_slot, peers = _me_and_peers()
            mine = pltpu.make_async_copy(src.at[my_slot], dst.at[my_slot], local.at[n])
            mine.start()
            own.append(mine)
            for k, (dev, peer_slot) in enumerate(peers):
                def piece(dst_slot, k=k, dev=dev, peer_slot=peer_slot):
                    return pltpu.make_async_remote_copy(
                        src_ref=src.at[peer_slot], dst_ref=dst.at[dst_slot], send_sem=s_send.at[k],
                        recv_sem=s_recv.at[k], device_id=dev, device_id_type=pl.DeviceIdType.MESH)

                cp = piece(my_slot)
                cp.start()
                sent.append(cp)
                landing.append(piece(peer_slot))
        for t in range(n):
            for j, chip in enumerate(chips):
                copy(t, 1 + j, (*chip, c), me).wait_recv()
                passed = copy(t, 4 + j, (*chip, c), sibling)
                passed.start()
                sent.append(passed)
        for t in range(n):
            copy(t, 0, sibling, me).wait_recv()
            for j, chip in enumerate(chips):
                copy(t, 4 + j, (*chip, 1 - c), me).wait_recv()
        for cp in landing:
            cp.wait_recv()
        for cp in sent:
            cp.wait_send()
        for cp in own:
            cp.wait()

    scratch = [pltpu.SemaphoreType.DMA((7 * n,)), pltpu.SemaphoreType.DMA((7 * n,)), pltpu.SemaphoreType.DMA((n + ns,))]
    out_shape = [jax.ShapeDtypeStruct((NDEV,) + a.shape, a.dtype) for a in tensors]
    args = list(tensors)
    if ns:
        scratch += [pltpu.SemaphoreType.DMA((NDEV - 1,)), pltpu.SemaphoreType.DMA((NDEV - 1,))]
        out_shape.append(jax.ShapeDtypeStruct(scatter.shape, scatter.dtype))
        args.append(scatter)
    return pl.pallas_call(
        body, in_specs=[any_spec] * (n + ns), out_specs=[any_spec] * (n + ns), out_shape=out_shape,
        scratch_shapes=scratch, name=name,
    )(*args)


_HBM_SPEC = pl.BlockSpec(memory_space=pltpu.HBM)
_SEM_SPEC = pl.BlockSpec(memory_space=pltpu.SEMAPHORE)
_NPEER = NDEV - 1


def _split_copy_params():
    return pltpu.CompilerParams(has_side_effects=pltpu.SideEffectType.DATAFLOW_SIDE_EFFECTING)


def _me_and_peers():
    x, y, c = _coords()
    peers = []
    for rel in range(1, NDEV):
        p = (1 - x if rel & 4 else x, 1 - y if rel & 2 else y, 1 - c if rel & 1 else c)
        peers.append((p, 4 * p[0] + 2 * p[1] + p[2]))
    return 4 * x + 2 * y + c, peers


def _hbm(a):
    return pltpu.with_memory_space_constraint(a, pltpu.HBM)


def gather_start(bufs, name):
    n = len(bufs)

    def body(*refs):
        ins, outs = refs[:n], refs[n:]
        me, peers = _me_and_peers()
        for t in range(n):
            for k, (dev, _) in enumerate(peers):
                pltpu.make_async_remote_copy(
                    src_ref=ins[t].at[me], dst_ref=ins[t].at[me], send_sem=outs[3 * t].at[k],
                    recv_sem=outs[3 * t + 1].at[k], device_id=dev, device_id_type=pl.DeviceIdType.MESH).start()
        outs[3 * n][...] = jnp.zeros_like(outs[3 * n])

    out_shape, out_specs = [], []
    for b in bufs:
        out_shape += [pltpu.SemaphoreType.DMA((_NPEER,)), pltpu.SemaphoreType.DMA((_NPEER,)), pltpu.HBM(b.shape, b.dtype)]
        out_specs += [_SEM_SPEC, _SEM_SPEC, _HBM_SPEC]
    out_shape.append(jax.ShapeDtypeStruct((8, LANES), F32))
    out_specs.append(pl.BlockSpec(memory_space=pltpu.VMEM))
    res = pl.pallas_call(
        body, name=name, out_shape=tuple(out_shape), in_specs=[_HBM_SPEC] * n, out_specs=tuple(out_specs),
        input_output_aliases={t: 3 * t + 2 for t in range(n)}, compiler_params=_split_copy_params(),
    )(*[_hbm(b) for b in bufs])
    return [tuple(res[3 * t:3 * t + 3]) for t in range(n)], res[3 * n]


def gather_wait(started, after, name):
    n = len(started)

    def body(*refs):
        bufs, sems = refs[:n], refs[n:3 * n]
        me, peers = _me_and_peers()
        for t in range(n):
            for k, (dev, slot) in enumerate(peers):
                cp = pltpu.make_async_remote_copy(
                    src_ref=bufs[t].at[me], dst_ref=bufs[t].at[slot], send_sem=sems[2 * t].at[k],
                    recv_sem=sems[2 * t + 1].at[k], device_id=dev, device_id_type=pl.DeviceIdType.MESH)
                cp.wait_recv()
                cp.wait_send()

    args = [s[2] for s in started] + [sem for s in started for sem in s[:2]] + [after]
    res = pl.pallas_call(
        body, name=name, out_shape=tuple(pltpu.HBM(s[2].shape, s[2].dtype) for s in started),
        in_specs=[_HBM_SPEC] * n + [_SEM_SPEC] * (2 * n) + [pl.BlockSpec(memory_space=pl.ANY)],
        out_specs=tuple([_HBM_SPEC] * n), input_output_aliases={t: t for t in range(n)},
        compiler_params=_split_copy_params(),
    )(*args)
    return list(res)


def scatter_start(srcs, name):
    n = len(srcs)
    lands = [lax.empty((_NPEER,) + s.shape[1:], s.dtype) for s in srcs]

    def body(*refs):
        ins, land_refs, outs = refs[:n], refs[n:2 * n], refs[2 * n:]
        _, peers = _me_and_peers()
        for t in range(n):
            for k, (dev, slot) in enumerate(peers):
                pltpu.make_async_remote_copy(
                    src_ref=ins[t].at[slot], dst_ref=land_refs[t].at[k], send_sem=outs[4 * t].at[k],
                    recv_sem=outs[4 * t + 1].at[k], device_id=dev, device_id_type=pl.DeviceIdType.MESH).start()
        outs[4 * n][...] = jnp.zeros_like(outs[4 * n])

    out_shape, out_specs = [], []
    for s, land in zip(srcs, lands):
        out_shape += [pltpu.SemaphoreType.DMA((_NPEER,)), pltpu.SemaphoreType.DMA((_NPEER,)),
                      pltpu.HBM(s.shape, s.dtype), pltpu.HBM(land.shape, land.dtype)]
        out_specs += [_SEM_SPEC, _SEM_SPEC, _HBM_SPEC, _HBM_SPEC]
    out_shape.append(jax.ShapeDtypeStruct((8, LANES), F32))
    out_specs.append(pl.BlockSpec(memory_space=pltpu.VMEM))
    aliases = {t: 4 * t + 2 for t in range(n)}
    aliases.update({n + t: 4 * t + 3 for t in range(n)})
    res = pl.pallas_call(
        body, name=name, out_shape=tuple(out_shape), in_specs=[_HBM_SPEC] * (2 * n), out_specs=tuple(out_specs),
        input_output_aliases=aliases, compiler_params=_split_copy_params(),
    )(*[_hbm(s) for s in srcs], *[_hbm(land) for land in lands])
    return [tuple(res[4 * t:4 * t + 4]) for t in range(n)], res[4 * n]


def scatter_wait(started, after, name):
    n = len(started)

    def body(*refs):
        srcs, land_refs, sems = refs[:n], refs[n:2 * n], refs[2 * n:4 * n]
        _, peers = _me_and_peers()
        for t in range(n):
            for k, (dev, slot) in enumerate(peers):
                cp = pltpu.make_async_remote_copy(
                    src_ref=srcs[t].at[slot], dst_ref=land_refs[t].at[k], send_sem=sems[2 * t].at[k],
                    recv_sem=sems[2 * t + 1].at[k], device_id=dev, device_id_type=pl.DeviceIdType.MESH)
                cp.wait_recv()
                cp.wait_send()

    args = [s[2] for s in started] + [s[3] for s in started] + [sem for s in started for sem in s[:2]] + [after]
    res = pl.pallas_call(
        body, name=name,
        out_shape=tuple([pltpu.HBM(s[2].shape, s[2].dtype) for s in started]
                        + [pltpu.HBM(s[3].shape, s[3].dtype) for s in started]),
        in_specs=[_HBM_SPEC] * (2 * n) + [_SEM_SPEC] * (2 * n) + [pl.BlockSpec(memory_space=pl.ANY)],
        out_specs=tuple([_HBM_SPEC] * (2 * n)), input_output_aliases={t: t for t in range(2 * n)},
        compiler_params=_split_copy_params(),
    )(*args)
    return [(res[t], res[n + t]) for t in range(n)]


_PACK_QUANTUM = 8 * LANES


def _pack(parts, lead=0):
    out = []
    for p in parts:
        head = p.shape[:lead]
        f = p.astype(F32).reshape(head + (-1,))
        pad = (-f.shape[-1]) % _PACK_QUANTUM
        if pad:
            f = jnp.concatenate([f, jnp.zeros(head + (pad,), F32)], axis=-1)
        out.append(f.reshape(head + (-1, LANES)))
    return jnp.concatenate(out, axis=lead)


def _unpack(buf, shapes):
    head = buf.shape[:-2]
    out, r = [], 0
    for s in shapes:
        n = 1
        for v in s:
            n *= v
        nr = -(-n // _PACK_QUANTUM) * 8
        flat = buf[..., r:r + nr, :].reshape(head + (nr * LANES,))[..., :n]
        out.append(flat.reshape(head + tuple(s)))
        r += nr
    return out


BIG = ("sb_w_qkv", "sb_w_o", "sg_w_in", "sg_w_o", "ssm_w_in", "ssm_w_glu", "ffn_w_up", "ffn_w_down")
SMALL_SHARDED = ("norm_g", "ssm_d", "ffn_conv_w")
REPLICATED = ("final_norm_g", "sg_norm_g", "sg_w_s", "sg_b", "ssm_lam_re", "ssm_lam_im", "ssm_log_dt",
              "ssm_b_re", "ssm_b_im", "ssm_c_re", "ssm_c_im", "ffn_conv_b")
WEIGHTS = ("norm_g", "final_norm_g", "sb_w_qkv", "sb_w_o", "sg_w_in", "sg_norm_g", "sg_w_s", "sg_b", "sg_w_o",
           "ssm_w_in", "ssm_lam_re", "ssm_lam_im", "ssm_log_dt", "ssm_b_re", "ssm_b_im", "ssm_c_re", "ssm_c_im",
           "ssm_d", "ssm_w_glu", "ffn_w_up", "ffn_conv_w", "ffn_conv_b", "ffn_w_down")


def _step(x, loss_target, w, m, v):
    t, d = x.shape[1], x.shape[2]
    depth = w["norm_g"].shape[0]
    x0 = x.reshape(t, d)
    tgt = loss_target.reshape(t, d)

    mx, my, mc = _coords()
    me = (4 * mx + 2 * my + mc).astype(jnp.int32).reshape(1)
    shard_pack = _pack([w[k] for k in SMALL_SHARDED])
    gathered_small, = all_gather([shard_pack], name="gather_small_weights")
    mixer_weights = (("sb_w_qkv", "sb_w_o"), ("sg_w_in", "sg_w_o"), ("ssm_w_in", "ssm_w_glu"))
    order = []
    for i in range(depth):
        order += [(k, i // 3) for k in mixer_weights[i % 3]] + [("ffn_w_up", i), ("ffn_w_down", i)]
    as_kept = lambda tree, k: jnp.swapaxes(tree[k], 1, 2) if k == "ffn_w_up" else tree[k]
    pending, token = {}, None
    for group in (order[:4], order[4:]):
        started, token = gather_start([cast_into_slot(as_kept(w, k), l, me) for k, l in group], "gather_weights_start")
        pending.update(zip(group, started))
    wg = {}

    def weights(keys, after):
        missing = [key for key in keys if key not in wg]
        if missing:
            for key, buf in zip(missing, gather_wait([pending[key] for key in missing], after, "gather_weights_wait")):
                wg[key] = buf[:, None]
        return [wg[key] for key in keys]

    ng, sd, cw = _unpack(gathered_small, [w[k].shape for k in SMALL_SHARDED])
    norm_full = jnp.transpose(ng, (1, 2, 0, 3)).reshape(depth, 2, d)
    ssm_d_full = jnp.transpose(sd, (1, 0, 2)).reshape(1, d)
    nc = cw.shape[-1]
    conv_b3 = w["ffn_conv_b"].reshape(depth, NDEV, nc)
    p3 = [jnp.concatenate([cw[:, l], conv_b3[l][:, None, :], jnp.zeros((NDEV, 8 - CONV_K - 1, nc), F32)], axis=1)
          for l in range(depth)]

    g_, p_, h_ = SSM_GROUPS, SSM_STATE, SSM_GROUP
    lam_re, lam_im = w["ssm_lam_re"][0], w["ssm_lam_im"][0]
    log_dt = w["ssm_log_dt"][0].reshape(g_, 1)
    b_re, b_im = w["ssm_b_re"][0].reshape(g_ * p_, h_), w["ssm_b_im"][0].reshape(g_ * p_, h_)
    ar, ai, cre, cim = _single(_disc1, [lam_re, lam_im, log_dt], [(g_, p_)] * 4, "s5_disc1")
    cre_c, cim_c = cre.reshape(g_ * p_, 1), cim.reshape(g_ * p_, 1)
    bbr, bbi = _single(_disc2, [cre_c, cim_c, b_re, b_im], [(g_ * p_, h_)] * 2, "s5_disc2")
    per_group_t = lambda a, r, c: jnp.swapaxes(a.reshape(g_, r, c), 1, 2)
    bd = jnp.stack([_block_diag(per_group_t(bbr, p_, h_)), _block_diag(per_group_t(bbi, p_, h_))])
    cd = jnp.stack([_block_diag(per_group_t(w["ssm_c_re"][0], h_, p_)),
                    -_block_diag(per_group_t(w["ssm_c_im"][0], h_, p_))])
    a2 = jnp.stack([ar.reshape(g_ * p_), ai.reshape(g_ * p_)])

    sg_gain = w["sg_norm_g"]
    sg_ws = w["sg_w_s"][0]
    sg_bfull = jnp.broadcast_to(w["sg_b"][0][:, :, None], sg_ws.shape)

    acts = []
    xc = x0
    xn = rms_fwd(xc, norm_full[0, 0][None], "rms_fwd")
    for i in range(depth):
        mixer, j = i % 3, i // 3
        st = {"x": xc, "xn": xn}
        g0, g1 = norm_full[i, 0][None], norm_full[i, 1][None]
        g_next = norm_full[i + 1, 0][None] if i + 1 < depth else None
        k_in, k_out = [(k, j) for k in mixer_weights[mixer]]
        w_in, = weights([k_in], token if i == 0 else xn)
        if mixer == 0:
            qkv = mm_cs_fwd(xn, w_in, 0, BF16, "qkv_fwd")
            o, ltot = sb_attn_fwd(qkv)
            w_out, = weights([k_out], o)
            x1, xn2 = mm_rs_fwd(o, w_out, 0, xc, F32, "attn_out_fwd", norm=g1)
            st.update(qkv=qkv, o=o, ltot=ltot)
        elif mixer == 1:
            hin = mm_cs_fwd(xn, w_in, 0, BF16, "sg_in_fwd")
            p = sgu_fwd(hin, sg_gain, sg_ws, sg_bfull)
            w_out, = weights([k_out], p)
            x1, xn2 = mm_rs_fwd(p, w_out, 0, xc, F32, "sg_out_fwd", norm=g1)
            st.update(hin=hin, p=p)
        else:
            u = mm_rs_fwd(xn, w_in, 0, None, F32, "ssm_in_fwd")
            u_s = s5_reorder(u, True)
            x2 = s5_scan_fwd(mm_s5("bu", u_s, bd, t, "s5_bu"), a2)
            yc_s = mm_s5("yc", x2, cd, t, "s5_yc")
            yg = s5_post_fwd(s5_reorder(yc_s, False), u, ssm_d_full)
            w_out, = weights([k_out], yg)
            hg = mm_cs_fwd(yg, w_out, 0, BF16, "ssm_glu_fwd")
            x1, xn2 = glu_fwd(hg, xc, g1)
            st.update(u_s=u_s, x2=x2, yc_s=yc_s, yg=yg, hg=hg)
        w_up, = weights([("ffn_w_up", i)], xn2)
        h3 = mm_up_fwd(xn2, w_up, 0, BF16, "ffn_up_fwd")
        gated = ffn_gate_fwd(h3, p3[i])
        w_down, = weights([("ffn_w_down", i)], gated)
        if g_next is None:
            xc = mm_down_fwd(gated, w_down, 0, x1, "ffn_down_fwd")
        else:
            xc, xn = mm_down_fwd(gated, w_down, 0, x1, "ffn_down_fwd", norm=g_next)
        st.update(x1=x1, xn2=xn2, h3=h3, gated=gated, g0=g0, g1=g1)
        acts.append(st)

    dx, loss_lanes, d_final_g = loss_head(xc, w["final_norm_g"][None], tgt)
    loss = lax.psum(loss_lanes[0, 0], MESH_AXES)

    scattering = {}
    d_norm = [[None, None] for _ in range(depth)]
    d_p3 = [None] * depth
    rep = {}
    d_ssm_d = None
    token = None

    def scatter(grads_by_key):
        keys = list(grads_by_key)
        started, tok = scatter_start([grads_by_key[key] for key in keys], "scatter_grads_start")
        scattering[tuple(keys)] = started
        return tok

    for i in reversed(range(depth)):
        mixer, j = i % 3, i // 3
        st = acts[i]
        k_in, k_out = [(k, j) for k in mixer_weights[mixer]]
        w_in, w_out, w_up, w_down = weights([k_in, k_out, ("ffn_w_up", i), ("ffn_w_down", i)], None)
        dgated = mm_down_da(dx, w_down, 0, "ffn_down_da", dep=token)
        g_down = mm_down_dw(st["gated"], dx, "ffn_down_dw")
        dy_a, dy_g, dp_a, dp_g = ffn_gate_bwd(st["h3"], dgated, p3[i])
        d_p3[i] = jnp.concatenate([dp_a, dp_g], axis=0)
        dh3 = ffn_conv_t(dy_a, dy_g, p3[i])
        dxn2 = mm_up_da(dh3, w_up, 0, "ffn_up_da")
        g_up = mm_up_dw(st["xn2"], dh3, "ffn_up_dw")
        dx1, d_norm[i][1] = rms_bwd(st["x1"], st["g1"], dxn2, dx, "rms_bwd")
        token = scatter({("ffn_w_down", i): g_down, ("ffn_w_up", i): g_up})
        if mixer == 0:
            do = mm_rs_da(dx1, w_out, 0, BF16, "attn_out_da", dep=token)
            g_out = mm_rs_dw(st["o"], dx1, "attn_out_dw")
            d3 = sb_attn_bwd(st["qkv"], st["ltot"], do)
            g_in = mm_qkv_dw(st["xn"], d3, w_in.shape[3], "qkv_dw")
            token = scatter({k_in: g_in, k_out: g_out})
            dxn = mm_qkv_da(d3, w_in, 0, "qkv_da", dep=token)
        elif mixer == 1:
            dp = mm_rs_da(dx1, w_out, 0, BF16, "sg_out_da", dep=token)
            g_out = mm_rs_dw(st["p"], dx1, "sg_out_dw")
            dhin, d_ws, d_bfull, d_gain = sgu_bwd(st["hin"], dp, sg_gain, sg_ws, sg_bfull)
            rep.update(sg_w_s=d_ws[None], sg_b=d_bfull[None, :, :, 0], sg_norm_g=d_gain)
            dxn = mm_cs_da(dhin, w_in, 0, t, "sg_in_da")
            g_in = mm_cs_dw(st["xn"], dhin, w_in.shape[3], "sg_in_dw")
        else:
            dhg = glu_bwd(st["hg"], dx1, token)
            dyg = mm_cs_da(dhg, w_out, 0, t, "ssm_glu_da")
            g_out = mm_cs_dw(st["yg"], dhg, w_out.shape[3], "ssm_glu_dw")
            dyc_s, du_skip_s, d_ssm_d = s5_post_bwd(st["yc_s"], st["u_s"], ssm_d_full, s5_reorder(dyg, True))
            dx2 = mm_s5("dx", dyc_s, cd, t, "s5_dx")
            dcd = mm_s5("dcd", st["x2"], dyc_s, t, "s5_dcd")
            g2, da2 = s5_scan_bwd(dx2, st["x2"], a2)
            du = s5_reorder(mm_s5("du", g2, bd, t, "s5_du", res=du_skip_s), False)
            dbd = mm_s5("dbd", st["u_s"], g2, t, "s5_dbd")
            from_bd = lambda blk: jnp.swapaxes(_block_diag_t(blk, h_, p_), 1, 2).reshape(g_ * p_, h_)

            def disc2_bwd(c1, c2, b1, b2, t1, t2):
                return jax.vjp(_disc2, c1, c2, b1, b2)[1]((t1, t2))

            d_cre, d_cim, d_b_re, d_b_im = _single(
                disc2_bwd, [cre_c, cim_c, b_re, b_im, from_bd(dbd[0]), from_bd(dbd[1])],
                [(g_ * p_, 1)] * 2 + [(g_ * p_, h_)] * 2, "s5_disc2_bwd")

            def disc1_bwd(l1, l2, ld, t1, t2, t3, t4):
                return jax.vjp(_disc1, l1, l2, ld)[1]((t1, t2, t3, t4))

            d_lam_re, d_lam_im, d_log_dt = _single(
                disc1_bwd, [lam_re, lam_im, log_dt, da2[0].reshape(g_, p_), da2[1].reshape(g_, p_),
                            d_cre.reshape(g_, p_), d_cim.reshape(g_, p_)],
                [(g_, p_), (g_, p_), (g_, 1)], "s5_disc1_bwd")
            from_cd = lambda blk: jnp.swapaxes(_block_diag_t(blk, p_, h_), 1, 2)
            rep.update(ssm_lam_re=d_lam_re[None], ssm_lam_im=d_lam_im[None], ssm_log_dt=d_log_dt.reshape(1, g_),
                       ssm_b_re=d_b_re.reshape(1, g_, p_, h_), ssm_b_im=d_b_im.reshape(1, g_, p_, h_),
                       ssm_c_re=from_cd(dcd[0])[None], ssm_c_im=-from_cd(dcd[1])[None])
            dxn = mm_rs_da(du, w_in, 0, F32, "ssm_in_da")
            g_in = mm_rs_dw(st["xn"], du, "ssm_in_dw")
        dx, d_norm[i][0] = rms_bwd(st["x"], st["g0"], dxn, dx1, "rms_bwd")
        if mixer != 0:
            token = scatter({k_in: g_in, k_out: g_out})

    rep["final_norm_g"] = d_final_g.reshape(d)
    rep["ffn_conv_b"] = jnp.stack([d_p3[l][:, CONV_K, :].reshape(NDEV * nc) for l in range(depth)])

    d_norm_full = jnp.stack([jnp.concatenate(pair, axis=0) for pair in d_norm])
    d_norm_pieces = jnp.transpose(d_norm_full.reshape(depth, 2, NDEV, d // NDEV), (2, 0, 1, 3))
    d_ssm_d_pieces = jnp.transpose(d_ssm_d.reshape(1, NDEV, d // NDEV), (1, 0, 2))
    d_conv_w_pieces = jnp.stack([d_p3[l][:, :CONV_K, :] for l in range(depth)], axis=1)
    small_pieces = _pack([d_norm_pieces, d_ssm_d_pieces, d_conv_w_pieces], lead=1)
    rep_parts, small_received = all_gather([_pack([rep[k] for k in REPLICATED]).astype(BF16)],
                                           name="exchange_small_grads", scatter=small_pieces)
    own, landed = {}, {}
    for keys, started in scattering.items():
        for key, (src, land) in zip(keys, scatter_wait(started, dx, "scatter_grads_wait")):
            own[key], landed[key] = src, land

    grads, deltas, new_m, new_v = {}, {}, {}, {}
    for k in BIG:
        layers = range(w[k].shape[0])
        res = adamw_layers(as_kept(w, k), as_kept(m, k), as_kept(v, k), [landed[(k, l)] for l in layers],
                           [own[(k, l)] for l in layers], me, "adamw")
        grads[k], deltas[k], new_m[k], new_v[k] = [jnp.swapaxes(r, 1, 2) if k == "ffn_w_up" else r for r in res]
    for names, parts in ((SMALL_SHARDED, small_received), (REPLICATED, rep_parts)):
        res = adamw(_pack([w[k] for k in names]), _pack([m[k] for k in names]), _pack([v[k] for k in names]), parts,
                    "adamw_small")
        for tree, buf in zip((grads, deltas, new_m, new_v), res):
            for k, val in zip(names, _unpack(buf, [w[k].shape for k in names])):
                tree[k] = val
    grad_x = dx.reshape(x.shape)
    return (loss, grad_x, *[grads[k] for k in WEIGHTS], *[deltas[k] for k in WEIGHTS],
            *[new_m[k] for k in WEIGHTS], *[new_v[k] for k in WEIGHTS])


def kernel(x, norm_g, final_norm_g, sb_w_qkv, sb_w_o, sg_w_in, sg_norm_g, sg_w_s, sg_b, sg_w_o, ssm_w_in, ssm_lam_re, ssm_lam_im, ssm_log_dt, ssm_b_re, ssm_b_im, ssm_c_re, ssm_c_im, ssm_d, ssm_w_glu, ffn_w_up, ffn_conv_w, ffn_conv_b, ffn_w_down, loss_target, m_norm_g, m_final_norm_g, m_sb_w_qkv, m_sb_w_o, m_sg_w_in, m_sg_norm_g, m_sg_w_s, m_sg_b, m_sg_w_o, m_ssm_w_in, m_ssm_lam_re, m_ssm_lam_im, m_ssm_log_dt, m_ssm_b_re, m_ssm_b_im, m_ssm_c_re, m_ssm_c_im, m_ssm_d, m_ssm_w_glu, m_ffn_w_up, m_ffn_conv_w, m_ffn_conv_b, m_ffn_w_down, v_norm_g, v_final_norm_g, v_sb_w_qkv, v_sb_w_o, v_sg_w_in, v_sg_norm_g, v_sg_w_s, v_sg_b, v_sg_w_o, v_ssm_w_in, v_ssm_lam_re, v_ssm_lam_im, v_ssm_log_dt, v_ssm_b_re, v_ssm_b_im, v_ssm_c_re, v_ssm_c_im, v_ssm_d, v_ssm_w_glu, v_ffn_w_up, v_ffn_conv_w, v_ffn_conv_b, v_ffn_w_down):
    w = dict(zip(WEIGHTS, (norm_g, final_norm_g, sb_w_qkv, sb_w_o, sg_w_in, sg_norm_g, sg_w_s, sg_b, sg_w_o, ssm_w_in,
                           ssm_lam_re, ssm_lam_im, ssm_log_dt, ssm_b_re, ssm_b_im, ssm_c_re, ssm_c_im, ssm_d, ssm_w_glu,
                           ffn_w_up, ffn_conv_w, ffn_conv_b, ffn_w_down)))
    m = dict(zip(WEIGHTS, (m_norm_g, m_final_norm_g, m_sb_w_qkv, m_sb_w_o, m_sg_w_in, m_sg_norm_g, m_sg_w_s, m_sg_b,
                           m_sg_w_o, m_ssm_w_in, m_ssm_lam_re, m_ssm_lam_im, m_ssm_log_dt, m_ssm_b_re, m_ssm_b_im,
                           m_ssm_c_re, m_ssm_c_im, m_ssm_d, m_ssm_w_glu, m_ffn_w_up, m_ffn_conv_w, m_ffn_conv_b,
                           m_ffn_w_down)))
    v = dict(zip(WEIGHTS, (v_norm_g, v_final_norm_g, v_sb_w_qkv, v_sb_w_o, v_sg_w_in, v_sg_norm_g, v_sg_w_s, v_sg_b,
                           v_sg_w_o, v_ssm_w_in, v_ssm_lam_re, v_ssm_lam_im, v_ssm_log_dt, v_ssm_b_re, v_ssm_b_im,
                           v_ssm_c_re, v_ssm_c_im, v_ssm_d, v_ssm_w_glu, v_ffn_w_up, v_ffn_conv_w, v_ffn_conv_b,
                           v_ffn_w_down)))
    return _step(x, loss_target, w, m, v)
```

```python
import functools

import jax
import jax.numpy as jnp
from jax import lax
from jax.experimental import pallas as pl
from jax.experimental.pallas import tpu as pltpu

F32, BF16 = jnp.float32, jnp.bfloat16
MESH_AXES = ("x", "y", "c")
NDEV = 8
EPS = 1e-6
HEAD_DIM = 64
LANES = 128
ATT_BQ, ATT_BK = 2048, 256
CHUNK = 128
SG_GROUPS = 8
SSM_GROUPS, SSM_STATE, SSM_GROUP = 64, 64, 16
SSM_PACK = 8
S5_PASSES = 1
CONV_K = 3
HALO = 16
ROW_BLOCK = 512
SCAN_COLS = 256
ADAM_LR, ADAM_B1, ADAM_B2, ADAM_EPS, ADAM_WD, ADAM_STEP = 0.001, 0.9, 0.999, 1e-08, 0.01, 10
VMEM_LIMIT = 56 * 1024 * 1024

_MM = (((1,), (0,)), ((), ()))
_MM_TB = (((1,), (1,)), ((), ()))
_MM_TA = (((0,), (0,)), ((), ()))


def _params(sem):
    return pltpu.CompilerParams(dimension_semantics=sem, vmem_limit_bytes=VMEM_LIMIT)


def _rows(total, cap, mult=16):
    best = None
    for d in range(mult, min(total, cap) + 1, mult):
        if total % d == 0:
            best = d
    return best if best is not None else total


def _dot(a, b, dims, passes):
    if passes == 1:
        return lax.dot_general(a.astype(BF16), b.astype(BF16), dims, preferred_element_type=F32)
    a = a.astype(F32)
    b = b.astype(F32)
    ah = a.astype(BF16)
    bh = b.astype(BF16)
    al = (a - ah.astype(F32)).astype(BF16)
    bl = (b - bh.astype(F32)).astype(BF16)
    out = lax.dot_general(ah, bh, dims, preferred_element_type=F32)
    out = out + lax.dot_general(al, bh, dims, preferred_element_type=F32)
    return out + lax.dot_general(ah, bl, dims, preferred_element_type=F32)


def _mm(a, b, *, grid, a_blk, a_map, b_blk, b_map, o_blk, o_map, out_shape, out_dtype, name,
        dims=_MM, passes=1, res=None, res_blk=None, res_map=None, b_2d=None, acc_2d=None, dep=None, norm=None):
    nk = grid[2]
    has_res, has_norm = res is not None, norm is not None
    a_maps = list(a_map) if isinstance(a_map, (list, tuple)) else [a_map]
    b_maps = list(b_map) if isinstance(b_map, (list, tuple)) else [b_map]
    na, nb = len(a_maps), len(b_maps)
    n_in = na + nb + has_res + has_norm + (dep is not None)

    def body(*refs):
        o_ref = refs[n_in]
        r_ref = refs[na + nb] if has_res else None
        av = refs[0][...] if na == 1 else jnp.concatenate([r[...] for r in refs[:na]], axis=-1)
        bv = refs[na][...] if nb == 1 else jnp.concatenate([r[...] for r in refs[na:na + nb]], axis=-1)
        if b_2d is not None:
            bv = bv.reshape(b_2d)
        part = _dot(av, bv, dims, passes)

        def finish(total):
            if has_res:
                total = total + r_ref[...].astype(F32)
            o_ref[...] = total.reshape(o_ref.shape).astype(o_ref.dtype)
            if has_norm:
                refs[n_in + 1][...] = _rms(total, refs[na + nb + has_res][...]).astype(BF16)

        if nk == 1:
            finish(part)
        else:
            acc_ref = refs[-1]
            k = pl.program_id(2)

            @pl.when(k == 0)
            def _():
                acc_ref[...] = part

            @pl.when(k > 0)
            def _():
                acc_ref[...] += part

            @pl.when(k == nk - 1)
            def _():
                finish(acc_ref[...])

    in_specs = [pl.BlockSpec(a_blk, f) for f in a_maps] + [pl.BlockSpec(b_blk, f) for f in b_maps]
    args = [a] * na + [b] * nb
    if has_res:
        in_specs.append(pl.BlockSpec(res_blk, res_map))
        args.append(res)
    if has_norm:
        in_specs.append(pl.BlockSpec(norm.shape, lambda *_: (0, 0)))
        args.append(norm)
    if dep is not None:
        in_specs.append(pl.BlockSpec(memory_space=pl.ANY))
        args.append(dep)
    scratch = [pltpu.VMEM(acc_2d, F32)] if nk > 1 else []
    out_specs, out_shapes = pl.BlockSpec(o_blk, o_map), jax.ShapeDtypeStruct(out_shape, out_dtype)
    if has_norm:
        out_specs, out_shapes = [out_specs] * 2, [out_shapes, jax.ShapeDtypeStruct(out_shape, BF16)]
    return pl.pallas_call(
        body, grid=grid, in_specs=in_specs, out_specs=out_specs, out_shape=out_shapes, scratch_shapes=scratch,
        name=name, compiler_params=_params(("parallel", "parallel", "arbitrary")),
    )(*args)


def _cs_act_spec(ns, tm, row_of, col_of):
    if ns % LANES == 0:
        return (tm, ns), lambda *g: (row_of(*g), col_of(*g))
    return (None, tm, ns), lambda *g: (col_of(*g), row_of(*g), 0)


def mm_cs_fwd(a, w4, l, out_dtype, name, tm=2048):
    m, k = a.shape
    tm = min(tm, m)
    ns = w4.shape[3]
    o_blk, o_map = _cs_act_spec(ns, tm, lambda j, i, kk: i, lambda j, i, kk: j)
    out_shape = (m, NDEV * ns) if ns % LANES == 0 else (NDEV, m, ns)
    return _mm(a, w4, grid=(NDEV, m // tm, 1), a_blk=(tm, k), a_map=lambda j, i, kk: (i, 0),
               b_blk=(None, None, k, ns), b_map=lambda j, i, kk: (j, l, 0, 0),
               o_blk=o_blk, o_map=o_map, out_shape=out_shape, out_dtype=out_dtype, name=name)


def mm_cs_da(dc, w4, l, m, name, tm=1024):
    k, ns = w4.shape[2], w4.shape[3]
    tm = min(tm, m)
    a_blk, a_map = _cs_act_spec(ns, tm, lambda i, _, j: i, lambda i, _, j: j)
    return _mm(dc, w4, grid=(m // tm, 1, NDEV), a_blk=a_blk, a_map=a_map,
               b_blk=(None, None, k, ns), b_map=lambda i, _, j: (j, l, 0, 0),
               o_blk=(tm, k), o_map=lambda i, _, j: (i, 0), out_shape=(m, k), out_dtype=F32,
               dims=_MM_TB, acc_2d=(tm, k), name=name)


def mm_cs_dw(a, dc, ns, name, tk=2048):
    m, k = a.shape
    tk = min(tk, m)
    b_blk, b_map = _cs_act_spec(ns, tk, lambda j, _, kk: kk, lambda j, _, kk: j)
    return _mm(a, dc, grid=(NDEV, 1, m // tk), a_blk=(tk, k), a_map=lambda j, _, kk: (kk, 0),
               b_blk=b_blk, b_map=b_map, o_blk=(None, k, ns), o_map=lambda j, _, kk: (j, 0, 0),
               out_shape=(NDEV, k, ns), out_dtype=BF16, dims=_MM_TA, acc_2d=(k, ns), name=name)


def mm_up_fwd(a, wt4, l, out_dtype, name, tm=2048):
    m, k = a.shape
    tm = min(tm, m)
    ns = wt4.shape[2]
    return _mm(a, wt4, grid=(NDEV, m // tm, 1), a_blk=(tm, k), a_map=lambda j, i, kk: (i, 0),
               b_blk=(None, None, ns, k), b_map=lambda j, i, kk: (j, l, 0, 0), dims=_MM_TB,
               o_blk=(None, tm, ns), o_map=lambda j, i, kk: (j, i, 0), out_shape=(NDEV, m, ns), out_dtype=out_dtype,
               name=name)


def mm_up_da(dc3, wt4, l, name, tm=1024):
    _, m, ns = dc3.shape
    tm = min(tm, m)
    k = wt4.shape[3]
    return _mm(dc3, wt4, grid=(m // tm, 1, NDEV), a_blk=(None, tm, ns), a_map=lambda i, _, j: (j, i, 0),
               b_blk=(None, None, ns, k), b_map=lambda i, _, j: (j, l, 0, 0),
               o_blk=(tm, k), o_map=lambda i, _, j: (i, 0), out_shape=(m, k), out_dtype=F32, acc_2d=(tm, k), name=name)


def mm_up_dw(a, dc3, name, tk=2048):
    m, k = a.shape
    tk = min(tk, m)
    ns = dc3.shape[2]
    return _mm(dc3, a, grid=(NDEV, 1, m // tk), a_blk=(None, tk, ns), a_map=lambda j, _, kk: (j, kk, 0),
               b_blk=(tk, k), b_map=lambda j, _, kk: (kk, 0), dims=_MM_TA,
               o_blk=(None, ns, k), o_map=lambda j, _, kk: (j, 0, 0), out_shape=(NDEV, ns, k), out_dtype=BF16,
               acc_2d=(ns, k), name=name)


def mm_rs_fwd(a, w4, l, res, out_dtype, name, tm=1024, norm=None):
    m, k = a.shape
    tm = min(tm, m)
    ks, n = w4.shape[2], w4.shape[3]
    return _mm(a, w4, grid=(m // tm, 1, 1), a_blk=(tm, k), a_map=lambda i, _, kk: (i, 0),
               b_blk=(NDEV, None, ks, n), b_map=lambda i, _, kk: (0, l, 0, 0), b_2d=(k, n),
               o_blk=(tm, n), o_map=lambda i, _, kk: (i, 0), out_shape=(m, n), out_dtype=out_dtype,
               res=res, res_blk=(tm, n), res_map=lambda i, _, kk: (i, 0), name=name, norm=norm)


def mm_rs_da(dc, w4, l, out_dtype, name, tm=1024, dep=None):
    m, n = dc.shape
    tm = min(tm, m)
    ks = w4.shape[2]
    k = NDEV * ks
    return _mm(dc, w4, grid=(m // tm, 1, 1), a_blk=(tm, n), a_map=lambda i, _, kk: (i, 0),
               b_blk=(NDEV, None, ks, n), b_map=lambda i, _, kk: (0, l, 0, 0), b_2d=(k, n),
               o_blk=(tm, k), o_map=lambda i, _, kk: (i, 0), out_shape=(m, k), out_dtype=out_dtype,
               dims=_MM_TB, name=name, dep=dep)


def mm_rs_dw(a, dc, name, tk=1024):
    m, k = a.shape
    tk = min(tk, m)
    n = dc.shape[1]
    ks = k // NDEV
    return _mm(a, dc, grid=(1, 1, m // tk), a_blk=(tk, k), a_map=lambda _, __, kk: (kk, 0),
               b_blk=(tk, n), b_map=lambda _, __, kk: (kk, 0),
               o_blk=(NDEV, ks, n), o_map=lambda _, __, kk: (0, 0, 0), out_shape=(NDEV, ks, n),
               out_dtype=BF16, dims=_MM_TA, acc_2d=(k, n), name=name)


def mm_down_fwd(a3, w4, l, res, name, tm=1024, norm=None):
    nj, m, kc = a3.shape
    tm = min(tm, m)
    ks, n = w4.shape[2], w4.shape[3]
    return _mm(a3, w4, grid=(m // tm, 1, nj), a_blk=(None, tm, kc), a_map=lambda i, _, j: (j, i, 0),
               b_blk=(2, None, ks, n), b_map=lambda i, _, j: (j, l, 0, 0), b_2d=(kc, n),
               o_blk=(tm, n), o_map=lambda i, _, j: (i, 0), out_shape=(m, n), out_dtype=F32,
               res=res, res_blk=(tm, n), res_map=lambda i, _, j: (i, 0), acc_2d=(tm, n), name=name, norm=norm)


def mm_down_da(dc, w4, l, name, tm=2048, dep=None):
    m, n = dc.shape
    tm = min(tm, m)
    ks = w4.shape[2]
    kc = 2 * ks
    nj = NDEV // 2
    return _mm(dc, w4, grid=(nj, m // tm, 1), a_blk=(tm, n), a_map=lambda j, i, _: (i, 0),
               b_blk=(2, None, ks, n), b_map=lambda j, i, _: (j, l, 0, 0), b_2d=(kc, n),
               o_blk=(None, tm, kc), o_map=lambda j, i, _: (j, i, 0), out_shape=(nj, m, kc),
               out_dtype=BF16, dims=_MM_TB, name=name, dep=dep)


def mm_down_dw(a3, dc, name, tk=2048):
    nj, m, kc = a3.shape
    tk = min(tk, m)
    n = dc.shape[1]
    return _mm(a3, dc, grid=(nj, 1, m // tk), a_blk=(None, tk, kc), a_map=lambda j, _, kk: (j, kk, 0),
               b_blk=(tk, n), b_map=lambda j, _, kk: (kk, 0),
               o_blk=(2, kc // 2, n), o_map=lambda j, _, kk: (j, 0, 0), out_shape=(NDEV, kc // 2, n),
               out_dtype=BF16, dims=_MM_TA, acc_2d=(kc, n), name=name)


def _qkv_group_maps(d, ns, row_of, piece_of):
    per_arr, per_piece = d // LANES, ns // LANES

    def group_map(q):
        def f(*g):
            grp = piece_of(*g) * per_piece + q
            return grp // per_arr, row_of(*g), grp % per_arr
        return f

    return [group_map(q) for q in range(per_piece)]


def mm_qkv_da(d3, w4, l, name, tm=1024, dep=None):
    _, m, d = d3.shape
    tm = min(tm, m)
    k, ns = w4.shape[2], w4.shape[3]
    return _mm(d3, w4, grid=(m // tm, 1, NDEV),
               a_blk=(None, tm, LANES), a_map=_qkv_group_maps(d, ns, lambda i, _, j: i, lambda i, _, j: j),
               b_blk=(None, None, k, ns), b_map=lambda i, _, j: (j, l, 0, 0),
               o_blk=(tm, k), o_map=lambda i, _, j: (i, 0), out_shape=(m, k), out_dtype=F32,
               dims=_MM_TB, acc_2d=(tm, k), name=name, dep=dep)


def mm_qkv_dw(a, d3, ns, name, tk=2048):
    m, k = a.shape
    tk = min(tk, m)
    d = d3.shape[2]
    return _mm(a, d3, grid=(NDEV, 1, m // tk), a_blk=(tk, k), a_map=lambda j, _, kk: (kk, 0),
               b_blk=(None, tk, LANES), b_map=_qkv_group_maps(d, ns, lambda j, _, kk: kk, lambda j, _, kk: j),
               o_blk=(None, k, ns), o_map=lambda j, _, kk: (j, 0, 0),
               out_shape=(NDEV, k, ns), out_dtype=BF16, dims=_MM_TA, acc_2d=(k, ns), name=name)


def _rowwise(fn, ins, outs, *, tr, name, acc_outs=()):
    rows = next(a.shape[0] if kind == "row" else a.shape[1] for a, kind in ins if kind in ("row", "row3"))
    n_in, n_out = len(ins), len(outs)
    n_read = sum(kind != "dep" for _, kind in ins)

    def body(*refs):
        vals = fn(*[r[...] for r in refs[:n_read]])
        if not isinstance(vals, (tuple, list)):
            vals = (vals,)
        for ref, val in zip(refs[n_in:n_in + n_out], vals[:n_out]):
            ref[...] = val.astype(ref.dtype)
        i = pl.program_id(0)
        for ref, val in zip(refs[n_in + n_out:], vals[n_out:]):
            val = val.astype(ref.dtype)

            @pl.when(i == 0)
            def _(ref=ref, val=val):
                ref[...] = val

            @pl.when(i > 0)
            def _(ref=ref, val=val):
                ref[...] += val

    in_specs = []
    for a, kind in ins:
        if kind == "row":
            in_specs.append(pl.BlockSpec((tr, a.shape[1]), lambda i: (i, 0)))
        elif kind == "row3":
            in_specs.append(pl.BlockSpec((a.shape[0], tr, a.shape[2]), lambda i: (0, i, 0)))
        elif kind == "dep":
            in_specs.append(pl.BlockSpec(memory_space=pl.ANY))
        else:
            in_specs.append(pl.BlockSpec(a.shape, lambda i, nd=a.ndim: (0,) * nd))
    out_specs = [pl.BlockSpec((tr, c), lambda i: (i, 0)) for c, _ in outs]
    out_specs += [pl.BlockSpec(s, lambda i, nd=len(s): (0,) * nd) for s, _ in acc_outs]
    out_shape = [jax.ShapeDtypeStruct((rows, c), dt) for c, dt in outs]
    out_shape += [jax.ShapeDtypeStruct(s, dt) for s, dt in acc_outs]
    res = pl.pallas_call(
        body, grid=(rows // tr,), in_specs=in_specs, out_specs=out_specs, out_shape=out_shape, name=name,
        compiler_params=_params(("arbitrary",) if acc_outs else ("parallel",)),
    )(*[a for a, _ in ins])
    return res


def _rms(x, g):
    return x * lax.rsqrt(jnp.mean(x * x, axis=-1, keepdims=True) + EPS) * g


def cast_into_slot(w, l, me):
    _, r, c = w.shape
    tr = _rows(r, 512)

    def body(me_ref, w_ref, o_ref):
        o_ref[...] = w_ref[...].astype(o_ref.dtype)

    return pl.pallas_call(
        body,
        grid_spec=pltpu.PrefetchScalarGridSpec(
            num_scalar_prefetch=1, grid=(r // tr,),
            in_specs=[pl.BlockSpec((None, tr, c), lambda i, me_ref: (l, i, 0))],
            out_specs=pl.BlockSpec((None, tr, c), lambda i, me_ref: (me_ref[0], i, 0))),
        out_shape=jax.ShapeDtypeStruct((NDEV, r, c), BF16), name="cast_into_slot",
        compiler_params=_params(("parallel",)),
    )(me, w)


def rms_fwd(x, g, name):
    out, = _rowwise(_rms, [(x, "row"), (g, "full")], [(x.shape[1], BF16)], tr=ROW_BLOCK, name=name)
    return out


def rms_bwd(x, g, dy, dres, name):
    def fn(xv, gv, dyv, drv):
        _, vjp = jax.vjp(_rms, xv, gv)
        dx, dg = vjp(dyv.astype(F32))
        return drv + dx, dg

    d = x.shape[1]
    return _rowwise(fn, [(x, "row"), (g, "full"), (dy, "row"), (dres, "row")], [(d, F32)], tr=ROW_BLOCK, name=name,
                    acc_outs=[((1, d), F32)])


def loss_head(x, g, tgt):
    def f(xv, gv, tv):
        err = jnp.square(_rms(xv, gv) - tv)
        return 0.5 * jnp.sum(jnp.mean(err, axis=-1))

    def fn(xv, gv, tv):
        val, (dx, dg) = jax.value_and_grad(f, argnums=(0, 1))(xv, gv, tv)
        return dx, jnp.full((1, LANES), val, F32), dg

    d = x.shape[1]
    return _rowwise(fn, [(x, "row"), (g, "full"), (tgt, "row")], [(d, F32)], tr=ROW_BLOCK, name="loss_head",
                    acc_outs=[((1, LANES), F32), ((1, d), F32)])


def _glu(hg, x):
    half = hg.shape[1] // 2
    return x + hg[:, :half] * jax.nn.sigmoid(hg[:, half:])


def glu_fwd(hg, x, norm):
    def fn(h, xv, g):
        x1 = _glu(h.astype(F32), xv)
        return x1, _rms(x1, g)

    d = x.shape[1]
    return _rowwise(fn, [(hg, "row"), (x, "row"), (norm, "full")], [(d, F32), (d, BF16)], tr=ROW_BLOCK, name="glu_fwd")


def glu_bwd(hg, dx1, dep):
    def fn(h, d):
        _, vjp = jax.vjp(lambda hv: _glu(hv, jnp.zeros_like(d)), h.astype(F32))
        return vjp(d)[0]

    out, = _rowwise(fn, [(hg, "row"), (dx1, "row"), (dep, "dep")], [(hg.shape[1], BF16)], tr=ROW_BLOCK, name="glu_bwd")
    return out


def _s5_post(yc, u, d):
    return jax.nn.gelu(yc + d * u)


def s5_post_fwd(yc, u, d):
    out, = _rowwise(_s5_post, [(yc, "row"), (u, "row"), (d, "full")], [(yc.shape[1], BF16)], tr=ROW_BLOCK,
                    name="s5_post_fwd")
    return out


def s5_post_bwd(yc, u, d, dyg):
    def fn(ycv, uv, dv, g):
        _, vjp = jax.vjp(_s5_post, ycv, uv, dv)
        return vjp(g.astype(F32))

    dm = yc.shape[1]
    return _rowwise(fn, [(yc, "row"), (u, "row"), (d, "full"), (dyg, "row")], [(dm, F32), (dm, F32)], tr=ROW_BLOCK,
                    name="s5_post_bwd", acc_outs=[((1, dm), F32)])


def _adam_update(wv, mv, vv, g):
    m2 = ADAM_B1 * mv + (1.0 - ADAM_B1) * g
    v2 = ADAM_B2 * vv + (1.0 - ADAM_B2) * jnp.square(g)
    m_hat = m2 / (1.0 - ADAM_B1 ** ADAM_STEP)
    v_hat = v2 / (1.0 - ADAM_B2 ** ADAM_STEP)
    delta = -ADAM_LR * (m_hat / (jnp.sqrt(v_hat) + ADAM_EPS) + ADAM_WD * wv)
    return g, delta, m2, v2


def adamw(w, m, v, g_parts, name):
    def fn(wv, mv, vv, gp):
        g = gp[0].astype(F32)
        for p in range(1, gp.shape[0]):
            g = g + gp[p].astype(F32)
        return _adam_update(wv, mv, vv, g)

    c = w.shape[1]
    return _rowwise(fn, [(w, "row"), (m, "row"), (v, "row"), (g_parts, "row3")], [(c, F32)] * 4,
                    tr=_rows(w.shape[0], 256), name=name)


def adamw_layers(w, m, v, lands, owns, me, name):
    nl, r, c = w.shape
    tr = _rows(r, 256)

    def body(me_ref, w_ref, m_ref, v_ref, *rest):
        land_refs, own_refs, out_refs = rest[:nl], rest[nl:2 * nl], rest[2 * nl:]
        for l in range(nl):
            @pl.when(pl.program_id(0) == l)
            def _(l=l):
                g = own_refs[l][...].astype(F32)
                for p in range(NDEV - 1):
                    g = g + land_refs[l][p].astype(F32)
                for ref, val in zip(out_refs, _adam_update(w_ref[...], m_ref[...], v_ref[...], g)):
                    ref[...] = val

    def rows_of(l):
        return lambda li, i, me_ref: jnp.where(li == l, i, 0)

    wspec = pl.BlockSpec((None, tr, c), lambda li, i, me_ref: (li, i, 0))
    in_specs = [wspec] * 3
    in_specs += [pl.BlockSpec((NDEV - 1, tr, c), lambda li, i, me_ref, f=rows_of(l): (0, f(li, i, me_ref), 0))
                 for l in range(nl)]
    in_specs += [pl.BlockSpec((None, tr, c), lambda li, i, me_ref, f=rows_of(l): (me_ref[0], f(li, i, me_ref), 0))
                 for l in range(nl)]
    return pl.pallas_call(
        body,
        grid_spec=pltpu.PrefetchScalarGridSpec(
            num_scalar_prefetch=1, grid=(nl, r // tr), in_specs=in_specs, out_specs=[wspec] * 4),
        out_shape=[jax.ShapeDtypeStruct(w.shape, F32)] * 4, name=name, compiler_params=_params(("parallel", "parallel")),
    )(me, w, m, v, *lands, *owns)


def _conv_rows(cur, halo, p, first):
    r = cur.shape[0]
    ext = jnp.concatenate([jnp.where(first, 0.0, halo), cur], axis=0)
    s1 = pltpu.roll(ext, 1, 0)[HALO:]
    s2 = pltpu.roll(ext, 2, 0)[HALO:]
    return p[0:1] * s2 + p[1:2] * s1 + p[2:3] * cur + p[3:4], s1, s2


def ffn_gate_fwd(h3, p3, tr=ROW_BLOCK):
    _, t, c = h3.shape
    half = NDEV // 2

    def body(a_ref, ah_ref, g_ref, gh_ref, pa_ref, pg_ref, o_ref):
        first = pl.program_id(1) == 0
        ya, _, _ = _conv_rows(a_ref[...].astype(F32), ah_ref[...].astype(F32), pa_ref[...], first)
        yg, _, _ = _conv_rows(g_ref[...].astype(F32), gh_ref[...].astype(F32), pg_ref[...], first)
        o_ref[...] = (jax.nn.silu(yg) * ya).astype(o_ref.dtype)

    main = lambda off: pl.BlockSpec((None, tr, c), lambda j, i: (j + off, i, 0))
    halo = lambda off: pl.BlockSpec((None, HALO, c), lambda j, i: (j + off, jnp.maximum(i * (tr // HALO) - 1, 0), 0))
    par = lambda off: pl.BlockSpec((None, 8, c), lambda j, i: (j + off, 0, 0))
    return pl.pallas_call(
        body, grid=(half, t // tr),
        in_specs=[main(0), halo(0), main(half), halo(half), par(0), par(half)],
        out_specs=pl.BlockSpec((None, tr, c), lambda j, i: (j, i, 0)),
        out_shape=jax.ShapeDtypeStruct((half, t, c), BF16), name="ffn_gate_fwd",
        compiler_params=_params(("parallel", "parallel")),
    )(h3, h3, h3, h3, p3, p3)


def ffn_gate_bwd(h3, dgated3, p3, tr=ROW_BLOCK):
    _, t, c = h3.shape
    half = NDEV // 2

    def body(a_ref, ah_ref, g_ref, gh_ref, dg_ref, pa_ref, pg_ref, dya_ref, dyg_ref, dpa_ref, dpg_ref):
        i = pl.program_id(1)
        first = i == 0
        a = a_ref[...].astype(F32)
        g = g_ref[...].astype(F32)
        ya, a1, a2 = _conv_rows(a, ah_ref[...].astype(F32), pa_ref[...], first)
        yg, g1, g2 = _conv_rows(g, gh_ref[...].astype(F32), pg_ref[...], first)
        d = dg_ref[...].astype(F32)
        sig = jax.nn.sigmoid(yg)
        d_ya = d * (yg * sig)
        d_yg = d * ya * (sig * (1.0 + yg * (1.0 - sig)))
        dya_ref[...] = d_ya.astype(dya_ref.dtype)
        dyg_ref[...] = d_yg.astype(dyg_ref.dtype)
        for dy, cur, s1, s2, dp_ref in ((d_ya, a, a1, a2, dpa_ref), (d_yg, g, g1, g2, dpg_ref)):
            rows = [jnp.sum(dy * s2, axis=0, keepdims=True), jnp.sum(dy * s1, axis=0, keepdims=True),
                    jnp.sum(dy * cur, axis=0, keepdims=True), jnp.sum(dy, axis=0, keepdims=True)]
            dp = jnp.concatenate(rows + [jnp.zeros((4, c), F32)], axis=0)

            @pl.when(first)
            def _(dp_ref=dp_ref, dp=dp):
                dp_ref[...] = dp

            @pl.when(i > 0)
            def _(dp_ref=dp_ref, dp=dp):
                dp_ref[...] += dp

    main = lambda off: pl.BlockSpec((None, tr, c), lambda j, i: (j + off, i, 0))
    halo = lambda off: pl.BlockSpec((None, HALO, c), lambda j, i: (j + off, jnp.maximum(i * (tr // HALO) - 1, 0), 0))
    par = lambda off: pl.BlockSpec((None, 8, c), lambda j, i: (j + off, 0, 0))
    return pl.pallas_call(
        body, grid=(half, t // tr),
        in_specs=[main(0), halo(0), main(half), halo(half), main(0), par(0), par(half)],
        out_specs=[main(0), main(0), par(0), par(0)],
        out_shape=[jax.ShapeDtypeStruct((half, t, c), BF16)] * 2 + [jax.ShapeDtypeStruct((half, 8, c), F32)] * 2,
        name="ffn_gate_bwd", compiler_params=_params(("parallel", "arbitrary")),
    )(h3, h3, h3, h3, dgated3, p3, p3)


def ffn_conv_t(dy_a, dy_g, p3, tr=2 * ROW_BLOCK):
    half, t, c = dy_a.shape
    tr = min(tr, t)
    nblk = t // tr

    def body(a_ref, ah_ref, g_ref, gh_ref, p_ref, o_ref):
        is_a = pl.program_id(0) < half
        last = pl.program_id(1) == nblk - 1
        cur = jnp.where(is_a, a_ref[...], g_ref[...]).astype(F32)
        nxt = jnp.where(is_a, ah_ref[...], gh_ref[...]).astype(F32)
        ext = jnp.concatenate([cur, jnp.where(last, 0.0, nxt)], axis=0)
        n = tr + HALO
        s1 = pltpu.roll(ext, n - 1, 0)[:tr]
        s2 = pltpu.roll(ext, n - 2, 0)[:tr]
        p = p_ref[...]
        o_ref[...] = (p[2:3] * cur + p[1:2] * s1 + p[0:1] * s2).astype(o_ref.dtype)

    main = pl.BlockSpec((None, tr, c), lambda j, i: (j % half, i, 0))
    halo = pl.BlockSpec((None, HALO, c), lambda j, i: (j % half, jnp.minimum((i + 1) * (tr // HALO), t // HALO - 1), 0))
    return pl.pallas_call(
        body, grid=(NDEV, nblk),
        in_specs=[main, halo, main, halo, pl.BlockSpec((None, 8, c), lambda j, i: (j, 0, 0))],
        out_specs=pl.BlockSpec((None, tr, c), lambda j, i: (j, i, 0)),
        out_shape=jax.ShapeDtypeStruct((NDEV, t, c), BF16), name="ffn_conv_t",
        compiler_params=_params(("parallel", "parallel")),
    )(dy_a, dy_a, dy_g, dy_g, p3)


def _att_consts(bq, bk):
    lane = lax.broadcasted_iota(jnp.int32, (1, LANES), 1)
    heads = (lane < HEAD_DIM, lane >= HEAD_DIM)
    rr = lax.broadcasted_iota(jnp.int32, (bq, bk), 0)
    cc = lax.broadcasted_iota(jnp.int32, (bq, bk), 1)
    kr = lax.broadcasted_iota(jnp.int32, (bk, bk), 0)
    kc = lax.broadcasted_iota(jnp.int32, (bk, bk), 1)
    return heads, rr, cc, kr, kc


def _split_dot(x, tri, parts):
    out = None
    for _ in range(parts):
        piece = x.astype(BF16)
        x = x - piece.astype(F32)
        term = jnp.dot(piece, tri, preferred_element_type=F32)
        out = term if out is None else out + term
    return out


def _att_logits(qh, k):
    z = lax.dot_general(qh, k, _MM_TB, preferred_element_type=F32)
    lsp = jnp.minimum(z, 0.0) - jnp.log(1.0 + jnp.exp(-jnp.abs(z)))
    return lsp, lsp - z


def _per_head(heads, a, b):
    return jnp.where(heads[0], a, b)


def sb_attn_fwd(qkv):
    t, d3 = qkv.shape
    d = d3 // 3
    npair = d // LANES
    bq, bk = min(ATT_BQ, t), min(ATT_BK, t)
    kpq = bq // bk

    def body(q_ref, k_ref, v_ref, o_ref, lt_ref, acc_ref):
        heads, rr, cc, kr, kc = _att_consts(bq, bk)
        suffix = (kr > kc).astype(BF16)

        def trip(qh, k0, r0, runs):
            k = k_ref[pl.ds(k0, bk), :]
            v = v_ref[pl.ds(k0, bk), :]
            diag, r0 = r0 is not None, r0 or 0
            valid = cc[:bq - r0] < rr[:bq - r0]
            new_runs = []
            for h in range(2):
                lsp, lraw = _att_logits(qh[h][r0:], k)
                lm = jnp.where(valid, lraw, 0.0) if diag else lraw
                w = jnp.exp(lsp + _split_dot(lm, suffix, 2) + runs[h][r0:])
                if diag:
                    w = jnp.where(valid, w, 0.0)
                acc_ref[h, r0:, :] += jnp.dot(w.astype(BF16), v, preferred_element_type=F32)
                below = runs[h][r0:] + jnp.sum(lm, axis=1, keepdims=True)
                new_runs.append(jnp.concatenate([runs[h][:r0], below], axis=0) if r0 else below)
            return tuple(new_runs)

        def q_loop(qb, _):
            q0 = pl.multiple_of(qb * bq, bq)
            q = q_ref[pl.ds(q0, bq), :] * 0.125
            qh = [jnp.where(hm, q, 0.0).astype(BF16) for hm in heads]
            acc_ref[...] = jnp.zeros_like(acc_ref)
            runs = (jnp.zeros((bq, 1), F32),) * 2
            for dblk in reversed(range(kpq)):
                runs = trip(qh, pl.multiple_of(q0 + dblk * bk, bk), dblk * bk, runs)
            nleft = qb * kpq
            runs = lax.fori_loop(
                0, nleft, lambda i, r: trip(qh, pl.multiple_of((nleft - 1 - i) * bk, bk), None, r), runs)
            o_ref[pl.ds(q0, bq), :] = _per_head(heads, acc_ref[0], acc_ref[1])
            lt_ref[pl.ds(q0, bq), :] = _per_head(heads, runs[0], runs[1])
            return 0

        lax.fori_loop(0, t // bq, q_loop, 0)

    col = lambda off: pl.BlockSpec((t, LANES), lambda p: (0, p + off))
    return pl.pallas_call(
        body, grid=(npair,), in_specs=[col(0), col(npair), col(2 * npair)], out_specs=[col(0), col(0)],
        out_shape=[jax.ShapeDtypeStruct((t, d), F32)] * 2, scratch_shapes=[pltpu.VMEM((2, bq, LANES), F32)],
        name="sb_attn_fwd", compiler_params=_params(("parallel",)),
    )(qkv, qkv, qkv)


def sb_attn_bwd(qkv, ltot, do):
    t, d3 = qkv.shape
    d = d3 // 3
    npair = d // LANES
    bq, bk = min(ATT_BQ, t), min(ATT_BK, t)
    kpq = bq // bk

    def body(q_ref, k_ref, v_ref, lt_ref, do_ref, d_ref, dk_acc, dv_acc, dq_acc):
        heads, rr, cc, kr, kc = _att_consts(bq, bk)
        prefix_incl = (kr <= kc).astype(BF16)
        prefix_excl = (kr < kc).astype(BF16)
        dk_acc[...] = jnp.zeros_like(dk_acc)
        dv_acc[...] = jnp.zeros_like(dv_acc)

        def trip(qh, doh, lt, k0, r0, carry, hs=(0, 1)):
            lruns, gruns = carry
            k = k_ref[pl.ds(k0, bk), :]
            v = v_ref[pl.ds(k0, bk), :]
            diag, r0 = r0 is not None, r0 or 0
            valid = cc[:bq - r0] < rr[:bq - r0]
            new_lruns, new_gruns = list(lruns), list(gruns)
            dk_blk = jnp.zeros((bk, LANES), F32)
            dv_blk = jnp.zeros((bk, LANES), F32)
            for h in hs:
                q_rows, do_rows = qh[h][r0:], doh[h][r0:]
                lsp, lraw = _att_logits(q_rows, k)
                lm = jnp.where(valid, lraw, 0.0) if diag else lraw
                right = lt[h][r0:] - (lruns[h][r0:] + _split_dot(lm, prefix_incl, 2))
                w = jnp.exp(lsp + right)
                if diag:
                    w = jnp.where(valid, w, 0.0)
                g = lax.dot_general(do_rows, v, _MM_TB, preferred_element_type=F32) * w
                left = gruns[h][r0:] + _split_dot(g, prefix_excl, 1)
                dz = g * jnp.exp(lraw) - jnp.exp(lsp) * left
                if diag:
                    dz = jnp.where(valid, dz, 0.0)
                dz = dz.astype(BF16)
                kh = jnp.where(heads[h], k, 0.0).astype(BF16)
                dq_acc[h, r0:, :] += jnp.dot(dz, kh, preferred_element_type=F32)
                dk_blk = dk_blk + lax.dot_general(dz, q_rows, _MM_TA, preferred_element_type=F32)
                dv_blk = dv_blk + lax.dot_general(w.astype(BF16), do_rows, _MM_TA, preferred_element_type=F32)
                l_below = lruns[h][r0:] + jnp.sum(lm, axis=1, keepdims=True)
                g_below = gruns[h][r0:] + jnp.sum(g, axis=1, keepdims=True)
                new_lruns[h] = jnp.concatenate([lruns[h][:r0], l_below], axis=0) if r0 else l_below
                new_gruns[h] = jnp.concatenate([gruns[h][:r0], g_below], axis=0) if r0 else g_below
            dk_acc[pl.ds(k0, bk), :] += dk_blk
            dv_acc[pl.ds(k0, bk), :] += dv_blk
            return tuple(new_lruns), tuple(new_gruns)

        def q_loop(qb, _):
            q0 = pl.multiple_of(qb * bq, bq)
            q = q_ref[pl.ds(q0, bq), :] * 0.125
            dout = do_ref[pl.ds(q0, bq), :]
            lt2 = lt_ref[pl.ds(q0, bq), :]
            qh = [jnp.where(hm, q, 0.0).astype(BF16) for hm in heads]
            doh = [jnp.where(hm, dout, 0.0).astype(BF16) for hm in heads]
            lt = [jnp.max(jnp.where(hm, lt2, -jnp.inf), axis=1, keepdims=True) for hm in heads]
            dq_acc[...] = jnp.zeros_like(dq_acc)
            col = (jnp.zeros((bq, 1), F32),) * 2
            carry = lax.fori_loop(
                0, qb * kpq, lambda kb, c: trip(qh, doh, lt, pl.multiple_of(kb * bk, bk), None, c), (col, col))
            for h in range(2):
                for dblk in range(kpq):
                    carry = trip(qh, doh, lt, pl.multiple_of(q0 + dblk * bk, bk), dblk * bk, carry, hs=(h,))
            d_ref[0, pl.ds(q0, bq), :] = ((dq_acc[0] + dq_acc[1]) * 0.125).astype(d_ref.dtype)
            return 0

        lax.fori_loop(0, t // bq, q_loop, 0)
        d_ref[1] = dk_acc[...].astype(d_ref.dtype)
        d_ref[2] = dv_acc[...].astype(d_ref.dtype)

    col = lambda off: pl.BlockSpec((t, LANES), lambda p: (0, p + off))
    return pl.pallas_call(
        body, grid=(npair,), in_specs=[col(0), col(npair), col(2 * npair), col(0), col(0)],
        out_specs=pl.BlockSpec((3, t, LANES), lambda p: (0, 0, p)),
        out_shape=jax.ShapeDtypeStruct((3, t, d), BF16),
        scratch_shapes=[pltpu.VMEM((t, LANES), F32), pltpu.VMEM((t, LANES), F32), pltpu.VMEM((2, bq, LANES), F32)],
        name="sb_attn_bwd", compiler_params=_params(("parallel",)),
    )(qkv, qkv, qkv, ltot, do)


def _sgu_parts(hin, g, ws_ref, bf_ref):
    width = hin.shape[1] // 2
    h = jax.nn.gelu(hin)
    u, v = h[:, :width], h[:, width:]
    r = lax.rsqrt(jnp.mean(v * v, axis=-1, keepdims=True) + EPS)
    vn = v * r * g
    rr = lax.broadcasted_iota(jnp.int32, (CHUNK, CHUNK), 0)
    cc = lax.broadcasted_iota(jnp.int32, (CHUNK, CHUNK), 1)
    causal = cc <= rr
    wcs = [jnp.where(causal, ws_ref[gi], 0.0).astype(BF16) for gi in range(SG_GROUPS)]
    sv = jnp.concatenate(
        [jnp.dot(wcs[gi], vn[:, gi * CHUNK:(gi + 1) * CHUNK].astype(BF16), preferred_element_type=F32) + bf_ref[gi]
         for gi in range(SG_GROUPS)], axis=1)
    return u, v, r, vn, wcs, sv, causal


def sgu_fwd(hin, g, ws, bfull):
    t, w2 = hin.shape
    width = w2 // 2

    def body(h_ref, g_ref, ws_ref, bf_ref, o_ref):
        u, _, _, _, _, sv, _ = _sgu_parts(h_ref[...].astype(F32), g_ref[...], ws_ref, bf_ref)
        o_ref[...] = (u * sv).astype(o_ref.dtype)

    full = lambda a: pl.BlockSpec(a.shape, lambda i, nd=a.ndim: (0,) * nd)
    return pl.pallas_call(
        body, grid=(t // CHUNK,), in_specs=[pl.BlockSpec((CHUNK, w2), lambda i: (i, 0)), full(g), full(ws), full(bfull)],
        out_specs=pl.BlockSpec((CHUNK, width), lambda i: (i, 0)), out_shape=jax.ShapeDtypeStruct((t, width), BF16),
        name="sgu_fwd", compiler_params=_params(("parallel",)),
    )(hin, g, ws, bfull)


def sgu_bwd(hin, dp, g, ws, bfull):
    t, w2 = hin.shape
    width = w2 // 2

    def body(h_ref, dp_ref, g_ref, ws_ref, bf_ref, dh_ref, dws_ref, dbf_ref, dg_ref):
        i = pl.program_id(0)
        hin_v = h_ref[...].astype(F32)
        gv = g_ref[...]
        u, v, r, vn, wcs, sv, causal = _sgu_parts(hin_v, gv, ws_ref, bf_ref)
        dpv = dp_ref[...].astype(F32)
        du = dpv * sv
        dsv = dpv * u
        dvn_parts, dws_parts, dbf_parts = [], [], []
        for gi in range(SG_GROUPS):
            dsv_g = dsv[:, gi * CHUNK:(gi + 1) * CHUNK]
            dsv_b = dsv_g.astype(BF16)
            dvn_parts.append(lax.dot_general(wcs[gi], dsv_b, _MM_TA, preferred_element_type=F32))
            vn_b = vn[:, gi * CHUNK:(gi + 1) * CHUNK].astype(BF16)
            dws_parts.append(jnp.where(causal, lax.dot_general(dsv_b, vn_b, _MM_TB, preferred_element_type=F32), 0.0))
            dbf_parts.append(jnp.broadcast_to(jnp.sum(dsv_g, axis=1, keepdims=True), (CHUNK, CHUNK)))
        dvn = jnp.concatenate(dvn_parts, axis=1)
        dgain = jnp.sum(dvn * v * r, axis=0, keepdims=True)
        gvv = dvn * gv
        dv = r * gvv - v * (r * r * r) * jnp.mean(v * gvv, axis=-1, keepdims=True)
        _, vjp = jax.vjp(jax.nn.gelu, hin_v)
        dh_ref[...] = vjp(jnp.concatenate([du, dv], axis=1))[0].astype(dh_ref.dtype)

        @pl.when(i == 0)
        def _():
            for gi in range(SG_GROUPS):
                dws_ref[gi] = dws_parts[gi]
                dbf_ref[gi] = dbf_parts[gi]
            dg_ref[...] = dgain

        @pl.when(i > 0)
        def _():
            for gi in range(SG_GROUPS):
                dws_ref[gi] += dws_parts[gi]
                dbf_ref[gi] += dbf_parts[gi]
            dg_ref[...] += dgain

    full = lambda a: pl.BlockSpec(a.shape, lambda i, nd=a.ndim: (0,) * nd)
    sq = (SG_GROUPS, CHUNK, CHUNK)
    return pl.pallas_call(
        body, grid=(t // CHUNK,),
        in_specs=[pl.BlockSpec((CHUNK, w2), lambda i: (i, 0)), pl.BlockSpec((CHUNK, width), lambda i: (i, 0)),
                  full(g), full(ws), full(bfull)],
        out_specs=[pl.BlockSpec((CHUNK, w2), lambda i: (i, 0)), pl.BlockSpec(sq, lambda i: (0, 0, 0)),
                   pl.BlockSpec(sq, lambda i: (0, 0, 0)), pl.BlockSpec((1, width), lambda i: (0, 0))],
        out_shape=[jax.ShapeDtypeStruct((t, w2), BF16), jax.ShapeDtypeStruct(sq, F32), jax.ShapeDtypeStruct(sq, F32),
                   jax.ShapeDtypeStruct((1, width), F32)],
        name="sgu_bwd", compiler_params=_params(("arbitrary",)),
    )(hin, dp, g, ws, bfull)


def _disc1(lam_re, lam_im, log_dt):
    lr = jnp.minimum(lam_re, -1e-4)
    li = lam_im
    dt = jnp.exp(log_dt)
    mag = jnp.exp(dt * lr)
    ar = mag * jnp.cos(dt * li)
    ai = mag * jnp.sin(dt * li)
    den = lr * lr + li * li
    return ar, ai, ((ar - 1.0) * lr + ai * li) / den, (ai * lr - (ar - 1.0) * li) / den


def _disc2(cre, cim, b_re, b_im):
    return cre * b_re - cim * b_im, cre * b_im + cim * b_re


def _single(fn, ins, out_shapes, name):
    n = len(ins)

    def body(*refs):
        vals = fn(*[r[...] for r in refs[:n]])
        for ref, val in zip(refs[n:], vals):
            ref[...] = val

    return pl.pallas_call(body, out_shape=[jax.ShapeDtypeStruct(s, F32) for s in out_shapes], name=name)(*ins)


SCAN_SEGMENTS = 8
REORDER_STEPS = 64


def s5_reorder(x, to_steps):
    t, d = x.shape
    ns = SCAN_SEGMENTS
    seg = t // ns
    ts = min(REORDER_STEPS, seg)
    by_segment = ((ns, seg, d), pl.BlockSpec((ns, ts, d), lambda i: (0, i, 0)))
    by_step = ((seg, ns, d), pl.BlockSpec((ts, ns, d), lambda i: (i, 0, 0)))
    (in_shape, in_spec), (out_shape, out_spec) = (by_segment, by_step) if to_steps else (by_step, by_segment)

    def body(x_ref, o_ref):
        o_ref[...] = jnp.swapaxes(x_ref[...], 0, 1)

    out = pl.pallas_call(
        body, grid=(seg // ts,), in_specs=[in_spec], out_specs=out_spec,
        out_shape=jax.ShapeDtypeStruct(out_shape, x.dtype), name="s5_reorder", compiler_params=_params(("parallel",)),
    )(x.reshape(in_shape))
    return out.reshape(t, d)


def _cpow(ar, ai, n):
    rr, ri = None, None
    while n:
        if n & 1:
            rr, ri = (ar, ai) if rr is None else (rr * ar - ri * ai, rr * ai + ri * ar)
        ar, ai = ar * ar - ai * ai, 2.0 * ar * ai
        n >>= 1
    return rr, ri


def _edge_states(er, ei, pr, pi, reverse):
    ns = SCAN_SEGMENTS
    zero = jnp.zeros_like(er[0:1])
    rows_r, rows_i = [None] * ns, [None] * ns
    order = range(ns - 1, -1, -1) if reverse else range(ns)
    prev = None
    for s in order:
        if prev is None:
            rows_r[s], rows_i[s] = zero, zero
        else:
            cr, ci = rows_r[prev], rows_i[prev]
            rows_r[s] = er[prev:prev + 1] + pr * cr - pi * ci
            rows_i[s] = ei[prev:prev + 1] + pr * ci + pi * cr
        prev = s
    return jnp.concatenate(rows_r, axis=0), jnp.concatenate(rows_i, axis=0)


def s5_scan_fwd(bu2, a2):
    _, t, n = bu2.shape
    cb, ns = SCAN_COLS, SCAN_SEGMENTS
    seg = t // ns

    def body(bu_ref, a_ref, x_ref):
        ar, ai = a_ref[0:1, :], a_ref[1:2, :]

        def local(i, carry):
            xr, xi = carry
            xr, xi = ar * xr - ai * xi + bu_ref[0, i], ar * xi + ai * xr + bu_ref[1, i]
            x_ref[0, i] = xr
            x_ref[1, i] = xi
            return xr, xi

        zero = jnp.zeros((ns, cb), F32)
        er, ei = lax.fori_loop(0, seg, local, (zero, zero))
        cr, ci = _edge_states(er, ei, *_cpow(ar, ai, seg), reverse=False)

        def fix(i, carry):
            wr, wi = carry
            wr, wi = wr * ar - wi * ai, wr * ai + wi * ar
            x_ref[0, i] += wr * cr - wi * ci
            x_ref[1, i] += wr * ci + wi * cr
            return wr, wi

        lax.fori_loop(0, seg, fix, (jnp.ones((1, cb), F32), jnp.zeros((1, cb), F32)))

    blk = pl.BlockSpec((2, seg, ns, cb), lambda j: (0, 0, 0, j))
    out = pl.pallas_call(
        body, grid=(n // cb,), in_specs=[blk, pl.BlockSpec((2, cb), lambda j: (0, j))], out_specs=blk,
        out_shape=jax.ShapeDtypeStruct((2, seg, ns, n), F32), name="s5_scan_fwd", compiler_params=_params(("parallel",)),
    )(bu2.reshape(2, seg, ns, n), a2)
    return out.reshape(2, t, n)


def s5_scan_bwd(dx2, x2, a2):
    _, t, n = dx2.shape
    cb, ns = SCAN_COLS, SCAN_SEGMENTS
    seg = t // ns

    def body(dx_ref, x_ref, a_ref, g_ref, da_ref):
        ar, ai = a_ref[0:1, :], a_ref[1:2, :]

        def local(s, carry):
            gr, gi = carry
            i = seg - 1 - s
            gr, gi = dx_ref[0, i] + ar * gr + ai * gi, dx_ref[1, i] - ai * gr + ar * gi
            g_ref[0, i] = gr
            g_ref[1, i] = gi
            return gr, gi

        zero = jnp.zeros((ns, cb), F32)
        er, ei = lax.fori_loop(0, seg, local, (zero, zero))
        cr, ci = _edge_states(er, ei, *_cpow(ar, -ai, seg), reverse=True)
        row = lax.broadcasted_iota(jnp.int32, (ns, cb), 0)
        before_r = jnp.where(row == 0, 0.0, pltpu.roll(x_ref[0, seg - 1], 1, 0))
        before_i = jnp.where(row == 0, 0.0, pltpu.roll(x_ref[1, seg - 1], 1, 0))

        def fix(s, carry):
            wr, wi, dar, dai = carry
            i = seg - 1 - s
            wr, wi = wr * ar + wi * ai, wi * ar - wr * ai
            gr = g_ref[0, i] + wr * cr - wi * ci
            gi = g_ref[1, i] + wr * ci + wi * cr
            g_ref[0, i] = gr
            g_ref[1, i] = gi
            ip = jnp.maximum(i - 1, 0)
            xpr = jnp.where(i == 0, before_r, x_ref[0, ip])
            xpi = jnp.where(i == 0, before_i, x_ref[1, ip])
            return wr, wi, dar + gr * xpr + gi * xpi, dai + gi * xpr - gr * xpi

        one, z1 = jnp.ones((1, cb), F32), jnp.zeros((1, cb), F32)
        _, _, dar, dai = lax.fori_loop(0, seg, fix, (one, z1, zero, zero))
        da_ref[0:1, :] = jnp.sum(dar, axis=0, keepdims=True)
        da_ref[1:2, :] = jnp.sum(dai, axis=0, keepdims=True)

    blk = pl.BlockSpec((2, seg, ns, cb), lambda j: (0, 0, 0, j))
    vec = pl.BlockSpec((2, cb), lambda j: (0, j))
    g4, da = pl.pallas_call(
        body, grid=(n // cb,), in_specs=[blk, blk, vec], out_specs=[blk, vec],
        out_shape=[jax.ShapeDtypeStruct((2, seg, ns, n), F32), jax.ShapeDtypeStruct((2, n), F32)],
        name="s5_scan_bwd", compiler_params=_params(("parallel",)),
    )(dx2.reshape(2, seg, ns, n), x2.reshape(2, seg, ns, n), a2)
    return g4.reshape(2, t, n), da


_SP_U = SSM_PACK * SSM_GROUP
_SP_X = SSM_PACK * SSM_STATE
_NKB = SSM_GROUPS // SSM_PACK


def mm_s5(kind, a, b, m, name, res=None, tm=2048):
    tm = min(tm, m)
    kw = dict(passes=S5_PASSES, name=name)
    xblk = lambda row, sel, col: ((None, tm, _SP_X), lambda *g: (sel(*g), row(*g), col(*g)))
    if kind == "bu":
        o_blk, o_map = xblk(lambda g, i, k: i, lambda g, i, k: g // _NKB, lambda g, i, k: g % _NKB)
        return _mm(a, b, grid=(2 * _NKB, m // tm, 1), a_blk=(tm, _SP_U), a_map=lambda g, i, k: (i, g % _NKB),
                   b_blk=(None, None, _SP_U, _SP_X), b_map=lambda g, i, k: (g // _NKB, g % _NKB, 0, 0),
                   o_blk=o_blk, o_map=o_map, out_shape=(2, m, _NKB * _SP_X), out_dtype=F32, **kw)
    if kind == "yc":
        a_blk, a_map = xblk(lambda j, i, k: i, lambda j, i, k: k, lambda j, i, k: j)
        return _mm(a, b, grid=(_NKB, m // tm, 2), a_blk=a_blk, a_map=a_map,
                   b_blk=(None, None, _SP_X, _SP_U), b_map=lambda j, i, k: (k, j, 0, 0),
                   o_blk=(tm, _SP_U), o_map=lambda j, i, k: (i, j), out_shape=(m, _NKB * _SP_U), out_dtype=F32,
                   acc_2d=(tm, _SP_U), **kw)
    if kind == "dx":
        o_blk, o_map = xblk(lambda g, i, k: i, lambda g, i, k: g // _NKB, lambda g, i, k: g % _NKB)
        return _mm(a, b, grid=(2 * _NKB, m // tm, 1), a_blk=(tm, _SP_U), a_map=lambda g, i, k: (i, g % _NKB),
                   b_blk=(None, None, _SP_X, _SP_U), b_map=lambda g, i, k: (g // _NKB, g % _NKB, 0, 0),
                   o_blk=o_blk, o_map=o_map, out_shape=(2, m, _NKB * _SP_X), out_dtype=F32, dims=_MM_TB, **kw)
    if kind == "dcd":
        a_blk, a_map = xblk(lambda g, _, k: k, lambda g, _, k: g // _NKB, lambda g, _, k: g % _NKB)
        return _mm(a, b, grid=(2 * _NKB, 1, m // tm), a_blk=a_blk, a_map=a_map,
                   b_blk=(tm, _SP_U), b_map=lambda g, _, k: (k, g % _NKB),
                   o_blk=(None, None, _SP_X, _SP_U), o_map=lambda g, _, k: (g // _NKB, g % _NKB, 0, 0),
                   out_shape=(2, _NKB, _SP_X, _SP_U), out_dtype=F32, dims=_MM_TA, acc_2d=(_SP_X, _SP_U), **kw)
    if kind == "du":
        a_blk, a_map = xblk(lambda j, i, k: i, lambda j, i, k: k, lambda j, i, k: j)
        return _mm(a, b, grid=(_NKB, m // tm, 2), a_blk=a_blk, a_map=a_map,
                   b_blk=(None, None, _SP_U, _SP_X), b_map=lambda j, i, k: (k, j, 0, 0),
                   o_blk=(tm, _SP_U), o_map=lambda j, i, k: (i, j), out_shape=(m, _NKB * _SP_U), out_dtype=F32,
                   dims=_MM_TB, acc_2d=(tm, _SP_U), res=res, res_blk=(tm, _SP_U), res_map=lambda j, i, k: (i, j), **kw)
    assert kind == "dbd"
    b_blk, b_map = xblk(lambda g, _, k: k, lambda g, _, k: g // _NKB, lambda g, _, k: g % _NKB)
    return _mm(a, b, grid=(2 * _NKB, 1, m // tm), a_blk=(tm, _SP_U), a_map=lambda g, _, k: (k, g % _NKB),
               b_blk=b_blk, b_map=b_map,
               o_blk=(None, None, _SP_U, _SP_X), o_map=lambda g, _, k: (g // _NKB, g % _NKB, 0, 0),
               out_shape=(2, _NKB, _SP_U, _SP_X), out_dtype=F32, dims=_MM_TA, acc_2d=(_SP_U, _SP_X), **kw)


def _block_diag(w):
    g, a, b = w.shape
    eye = jnp.eye(SSM_PACK, dtype=w.dtype)
    wp = w.reshape(g // SSM_PACK, SSM_PACK, a, b)
    return jnp.einsum("kgab,gh->kgahb", wp, eye).reshape(g // SSM_PACK, SSM_PACK * a, SSM_PACK * b)


def _block_diag_t(d, a, b):
    k = d.shape[0]
    eye = jnp.eye(SSM_PACK, dtype=d.dtype)
    dp = d.reshape(k, SSM_PACK, a, SSM_PACK, b)
    return jnp.einsum("kgahb,gh->kgab", dp, eye).reshape(k * SSM_PACK, a, b)


def _coords():
    return lax.axis_index("x"), lax.axis_index("y"), lax.axis_index("c")


def all_gather(tensors, name, scatter=None):
    n = len(tensors)
    ns = 0 if scatter is None else 1
    any_spec = pl.BlockSpec(memory_space=pl.ANY)

    def body(*refs):
        ins, outs = refs[:n], refs[n + ns:2 * n + ns]
        send, recv, local = refs[2 * (n + ns):2 * (n + ns) + 3]
        x, y, c = _coords()
        me, sibling = (x, y, c), (x, y, 1 - c)
        chips = [(1 - x, y), (x, 1 - y), (1 - x, 1 - y)]

        def slot(p):
            return 4 * p[0] + 2 * p[1] + p[2]

        def copy(t, k, block, to, src=None):
            dst = outs[t].at[slot(block)]
            return pltpu.make_async_remote_copy(
                src_ref=dst if src is None else src, dst_ref=dst, send_sem=send.at[7 * t + k],
                recv_sem=recv.at[7 * t + k], device_id=to, device_id_type=pl.DeviceIdType.MESH)

        own, sent, landing = [], [], []
        for t in range(n):
            mine = pltpu.make_async_copy(ins[t], outs[t].at[slot(me)], local.at[t])
            mine.start()
            own.append(mine)
            first = [copy(t, 0, me, sibling, src=ins[t])]
            first += [copy(t, 1 + j, me, (*chip, c), src=ins[t]) for j, chip in enumerate(chips)]
            for cp in first:
                cp.start()
            sent += first
        if ns:
            src, dst = refs[n], refs[2 * n + ns]
            s_send, s_recv = refs[2 * (n + ns) + 3:]
            my_slot, peers = _me_and_peers()
            mine = pltpu.make_async_copy(src.at[my_slot], dst.at[my_slot], local.at[n])
            mine.start()
            own.append(mine)
            for k, (dev, peer_slot) in enumerate(peers):
                def piece(dst_slot, k=k, dev=dev, peer_slot=peer_slot):
                    return pltpu.make_async_remote_copy(
                        src_ref=src.at[peer_slot], dst_ref=dst.at[dst_slot], send_sem=s_send.at[k],
                        recv_sem=s_recv.at[k], device_id=dev, device_id_type=pl.DeviceIdType.MESH)

                cp = piece(my_slot)
                cp.start()
                sent.append(cp)
                landing.append(piece(peer_slot))
        for t in range(n):
            for j, chip in enumerate(chips):
                copy(t, 1 + j, (*chip, c), me).wait_recv()
                passed = copy(t, 4 + j, (*chip, c), sibling)
                passed.start()
                sent.append(passed)
        for t in range(n):
            copy(t, 0, sibling, me).wait_recv()
            for j, chip in enumerate(chips):
                copy(t, 4 + j, (*chip, 1 - c), me).wait_recv()
        for cp in landing:
            cp.wait_recv()
        for cp in sent:
            cp.wait_send()
        for cp in own:
            cp.wait()

    scratch = [pltpu.SemaphoreType.DMA((7 * n,)), pltpu.SemaphoreType.DMA((7 * n,)), pltpu.SemaphoreType.DMA((n + ns,))]
    out_shape = [jax.ShapeDtypeStruct((NDEV,) + a.shape, a.dtype) for a in tensors]
    args = list(tensors)
    if ns:
        scratch += [pltpu.SemaphoreType.DMA((NDEV - 1,)), pltpu.SemaphoreType.DMA((NDEV - 1,))]
        out_shape.append(jax.ShapeDtypeStruct(scatter.shape, scatter.dtype))
        args.append(scatter)
    return pl.pallas_call(
        body, in_specs=[any_spec] * (n + ns), out_specs=[any_spec] * (n + ns), out_shape=out_shape,
        scratch_shapes=scratch, name=name,
    )(*args)


_HBM_SPEC = pl.BlockSpec(memory_space=pltpu.HBM)
_SEM_SPEC = pl.BlockSpec(memory_space=pltpu.SEMAPHORE)
_NPEER = NDEV - 1


def _split_copy_params():
    return pltpu.CompilerParams(has_side_effects=pltpu.SideEffectType.DATAFLOW_SIDE_EFFECTING)


def _me_and_peers():
    x, y, c = _coords()
    peers = []
    for rel in range(1, NDEV):
        p = (1 - x if rel & 4 else x, 1 - y if rel & 2 else y, 1 - c if rel & 1 else c)
        peers.append((p, 4 * p[0] + 2 * p[1] + p[2]))
    return 4 * x + 2 * y + c, peers


def _hbm(a):
    return pltpu.with_memory_space_constraint(a, pltpu.HBM)


def gather_start(bufs, name):
    n = len(bufs)

    def body(*refs):
        ins, outs = refs[:n], refs[n:]
        me, peers = _me_and_peers()
        for t in range(n):
            for k, (dev, _) in enumerate(peers):
                pltpu.make_async_remote_copy(
                    src_ref=ins[t].at[me], dst_ref=ins[t].at[me], send_sem=outs[3 * t].at[k],
                    recv_sem=outs[3 * t + 1].at[k], device_id=dev, device_id_type=pl.DeviceIdType.MESH).start()
        outs[3 * n][...] = jnp.zeros_like(outs[3 * n])

    out_shape, out_specs = [], []
    for b in bufs:
        out_shape += [pltpu.SemaphoreType.DMA((_NPEER,)), pltpu.SemaphoreType.DMA((_NPEER,)), pltpu.HBM(b.shape, b.dtype)]
        out_specs += [_SEM_SPEC, _SEM_SPEC, _HBM_SPEC]
    out_shape.append(jax.ShapeDtypeStruct((8, LANES), F32))
    out_specs.append(pl.BlockSpec(memory_space=pltpu.VMEM))
    res = pl.pallas_call(
        body, name=name, out_shape=tuple(out_shape), in_specs=[_HBM_SPEC] * n, out_specs=tuple(out_specs),
        input_output_aliases={t: 3 * t + 2 for t in range(n)}, compiler_params=_split_copy_params(),
    )(*[_hbm(b) for b in bufs])
    return [tuple(res[3 * t:3 * t + 3]) for t in range(n)], res[3 * n]


def gather_wait(started, after, name):
    n = len(started)

    def body(*refs):
        bufs, sems = refs[:n], refs[n:3 * n]
        me, peers = _me_and_peers()
        for t in range(n):
            for k, (dev, slot) in enumerate(peers):
                cp = pltpu.make_async_remote_copy(
                    src_ref=bufs[t].at[me], dst_ref=bufs[t].at[slot], send_sem=sems[2 * t].at[k],
                    recv_sem=sems[2 * t + 1].at[k], device_id=dev, device_id_type=pl.DeviceIdType.MESH)
                cp.wait_recv()
                cp.wait_send()

    args = [s[2] for s in started] + [sem for s in started for sem in s[:2]] + [after]
    res = pl.pallas_call(
        body, name=name, out_shape=tuple(pltpu.HBM(s[2].shape, s[2].dtype) for s in started),
        in_specs=[_HBM_SPEC] * n + [_SEM_SPEC] * (2 * n) + [pl.BlockSpec(memory_space=pl.ANY)],
        out_specs=tuple([_HBM_SPEC] * n), input_output_aliases={t: t for t in range(n)},
        compiler_params=_split_copy_params(),
    )(*args)
    return list(res)


def scatter_start(srcs, name):
    n = len(srcs)
    lands = [lax.empty((_NPEER,) + s.shape[1:], s.dtype) for s in srcs]

    def body(*refs):
        ins, land_refs, outs = refs[:n], refs[n:2 * n], refs[2 * n:]
        _, peers = _me_and_peers()
        for t in range(n):
            for k, (dev, slot) in enumerate(peers):
                pltpu.make_async_remote_copy(
                    src_ref=ins[t].at[slot], dst_ref=land_refs[t].at[k], send_sem=outs[4 * t].at[k],
                    recv_sem=outs[4 * t + 1].at[k], device_id=dev, device_id_type=pl.DeviceIdType.MESH).start()
        outs[4 * n][...] = jnp.zeros_like(outs[4 * n])

    out_shape, out_specs = [], []
    for s, land in zip(srcs, lands):
        out_shape += [pltpu.SemaphoreType.DMA((_NPEER,)), pltpu.SemaphoreType.DMA((_NPEER,)),
                      pltpu.HBM(s.shape, s.dtype), pltpu.HBM(land.shape, land.dtype)]
        out_specs += [_SEM_SPEC, _SEM_SPEC, _HBM_SPEC, _HBM_SPEC]
    out_shape.append(jax.ShapeDtypeStruct((8, LANES), F32))
    out_specs.append(pl.BlockSpec(memory_space=pltpu.VMEM))
    aliases = {t: 4 * t + 2 for t in range(n)}
    aliases.update({n + t: 4 * t + 3 for t in range(n)})
    res = pl.pallas_call(
        body, name=name, out_shape=tuple(out_shape), in_specs=[_HBM_SPEC] * (2 * n), out_specs=tuple(out_specs),
        input_output_aliases=aliases, compiler_params=_split_copy_params(),
    )(*[_hbm(s) for s in srcs], *[_hbm(land) for land in lands])
    return [tuple(res[4 * t:4 * t + 4]) for t in range(n)], res[4 * n]


def scatter_wait(started, after, name):
    n = len(started)

    def body(*refs):
        srcs, land_refs, sems = refs[:n], refs[n:2 * n], refs[2 * n:4 * n]
        _, peers = _me_and_peers()
        for t in range(n):
            for k, (dev, slot) in enumerate(peers):
                cp = pltpu.make_async_remote_copy(
                    src_ref=srcs[t].at[slot], dst_ref=land_refs[t].at[k], send_sem=sems[2 * t].at[k],
                    recv_sem=sems[2 * t + 1].at[k], device_id=dev, device_id_type=pl.DeviceIdType.MESH)
                cp.wait_recv()
                cp.wait_send()

    args = [s[2] for s in started] + [s[3] for s in started] + [sem for s in started for sem in s[:2]] + [after]
    res = pl.pallas_call(
        body, name=name,
        out_shape=tuple([pltpu.HBM(s[2].shape, s[2].dtype) for s in started]
                        + [pltpu.HBM(s[3].shape, s[3].dtype) for s in started]),
        in_specs=[_HBM_SPEC] * (2 * n) + [_SEM_SPEC] * (2 * n) + [pl.BlockSpec(memory_space=pl.ANY)],
        out_specs=tuple([_HBM_SPEC] * (2 * n)), input_output_aliases={t: t for t in range(2 * n)},
        compiler_params=_split_copy_params(),
    )(*args)
    return [(res[t], res[n + t]) for t in range(n)]


_PACK_QUANTUM = 8 * LANES


def _pack(parts, lead=0):
    out = []
    for p in parts:
        head = p.shape[:lead]
        f = p.astype(F32).reshape(head + (-1,))
        pad = (-f.shape[-1]) % _PACK_QUANTUM
        if pad:
            f = jnp.concatenate([f, jnp.zeros(head + (pad,), F32)], axis=-1)
        out.append(f.reshape(head + (-1, LANES)))
    return jnp.concatenate(out, axis=lead)


def _unpack(buf, shapes):
    head = buf.shape[:-2]
    out, r = [], 0
    for s in shapes:
        n = 1
        for v in s:
            n *= v
        nr = -(-n // _PACK_QUANTUM) * 8
        flat = buf[..., r:r + nr, :].reshape(head + (nr * LANES,))[..., :n]
        out.append(flat.reshape(head + tuple(s)))
        r += nr
    return out


BIG = ("sb_w_qkv", "sb_w_o", "sg_w_in", "sg_w_o", "ssm_w_in", "ssm_w_glu", "ffn_w_up", "ffn_w_down")
SMALL_SHARDED = ("norm_g", "ssm_d", "ffn_conv_w")
REPLICATED = ("final_norm_g", "sg_norm_g", "sg_w_s", "sg_b", "ssm_lam_re", "ssm_lam_im", "ssm_log_dt",
              "ssm_b_re", "ssm_b_im", "ssm_c_re", "ssm_c_im", "ffn_conv_b")
WEIGHTS = ("norm_g", "final_norm_g", "sb_w_qkv", "sb_w_o", "sg_w_in", "sg_norm_g", "sg_w_s", "sg_b", "sg_w_o",
           "ssm_w_in", "ssm_lam_re", "ssm_lam_im", "ssm_log_dt", "ssm_b_re", "ssm_b_im", "ssm_c_re", "ssm_c_im",
           "ssm_d", "ssm_w_glu", "ffn_w_up", "ffn_conv_w", "ffn_conv_b", "ffn_w_down")


def _step(x, loss_target, w, m, v):
    t, d = x.shape[1], x.shape[2]
    depth = w["norm_g"].shape[0]
    x0 = x.reshape(t, d)
    tgt = loss_target.reshape(t, d)

    mx, my, mc = _coords()
    me = (4 * mx + 2 * my + mc).astype(jnp.int32).reshape(1)
    shard_pack = _pack([w[k] for k in SMALL_SHARDED])
    gathered_small, = all_gather([shard_pack], name="gather_small_weights")
    mixer_weights = (("sb_w_qkv", "sb_w_o"), ("sg_w_in", "sg_w_o"), ("ssm_w_in", "ssm_w_glu"))
    order = []
    for i in range(depth):
        order += [(k, i // 3) for k in mixer_weights[i % 3]] + [("ffn_w_up", i), ("ffn_w_down", i)]
    as_kept = lambda tree, k: jnp.swapaxes(tree[k], 1, 2) if k == "ffn_w_up" else tree[k]
    pending, token = {}, None
    for group in (order[:4], order[4:]):
        started, token = gather_start([cast_into_slot(as_kept(w, k), l, me) for k, l in group], "gather_weights_start")
        pending.update(zip(group, started))
    wg = {}

    def weights(keys, after):
        missing = [key for key in keys if key not in wg]
        if missing:
            for key, buf in zip(missing, gather_wait([pending[key] for key in missing], after, "gather_weights_wait")):
                wg[key] = buf[:, None]
        return [wg[key] for key in keys]

    ng, sd, cw = _unpack(gathered_small, [w[k].shape for k in SMALL_SHARDED])
    norm_full = jnp.transpose(ng, (1, 2, 0, 3)).reshape(depth, 2, d)
    ssm_d_full = jnp.transpose(sd, (1, 0, 2)).reshape(1, d)
    nc = cw.shape[-1]
    conv_b3 = w["ffn_conv_b"].reshape(depth, NDEV, nc)
    p3 = [jnp.concatenate([cw[:, l], conv_b3[l][:, None, :], jnp.zeros((NDEV, 8 - CONV_K - 1, nc), F32)], axis=1)
          for l in range(depth)]

    g_, p_, h_ = SSM_GROUPS, SSM_STATE, SSM_GROUP
    lam_re, lam_im = w["ssm_lam_re"][0], w["ssm_lam_im"][0]
    log_dt = w["ssm_log_dt"][0].reshape(g_, 1)
    b_re, b_im = w["ssm_b_re"][0].reshape(g_ * p_, h_), w["ssm_b_im"][0].reshape(g_ * p_, h_)
    ar, ai, cre, cim = _single(_disc1, [lam_re, lam_im, log_dt], [(g_, p_)] * 4, "s5_disc1")
    cre_c, cim_c = cre.reshape(g_ * p_, 1), cim.reshape(g_ * p_, 1)
    bbr, bbi = _single(_disc2, [cre_c, cim_c, b_re, b_im], [(g_ * p_, h_)] * 2, "s5_disc2")
    per_group_t = lambda a, r, c: jnp.swapaxes(a.reshape(g_, r, c), 1, 2)
    bd = jnp.stack([_block_diag(per_group_t(bbr, p_, h_)), _block_diag(per_group_t(bbi, p_, h_))])
    cd = jnp.stack([_block_diag(per_group_t(w["ssm_c_re"][0], h_, p_)),
                    -_block_diag(per_group_t(w["ssm_c_im"][0], h_, p_))])
    a2 = jnp.stack([ar.reshape(g_ * p_), ai.reshape(g_ * p_)])

    sg_gain = w["sg_norm_g"]
    sg_ws = w["sg_w_s"][0]
    sg_bfull = jnp.broadcast_to(w["sg_b"][0][:, :, None], sg_ws.shape)

    acts = []
    xc = x0
    xn = rms_fwd(xc, norm_full[0, 0][None], "rms_fwd")
    for i in range(depth):
        mixer, j = i % 3, i // 3
        st = {"x": xc, "xn": xn}
        g0, g1 = norm_full[i, 0][None], norm_full[i, 1][None]
        g_next = norm_full[i + 1, 0][None] if i + 1 < depth else None
        k_in, k_out = [(k, j) for k in mixer_weights[mixer]]
        w_in, = weights([k_in], token if i == 0 else xn)
        if mixer == 0:
            qkv = mm_cs_fwd(xn, w_in, 0, BF16, "qkv_fwd")
            o, ltot = sb_attn_fwd(qkv)
            w_out, = weights([k_out], o)
            x1, xn2 = mm_rs_fwd(o, w_out, 0, xc, F32, "attn_out_fwd", norm=g1)
            st.update(qkv=qkv, o=o, ltot=ltot)
        elif mixer == 1:
            hin = mm_cs_fwd(xn, w_in, 0, BF16, "sg_in_fwd")
            p = sgu_fwd(hin, sg_gain, sg_ws, sg_bfull)
            w_out, = weights([k_out], p)
            x1, xn2 = mm_rs_fwd(p, w_out, 0, xc, F32, "sg_out_fwd", norm=g1)
            st.update(hin=hin, p=p)
        else:
            u = mm_rs_fwd(xn, w_in, 0, None, F32, "ssm_in_fwd")
            u_s = s5_reorder(u, True)
            x2 = s5_scan_fwd(mm_s5("bu", u_s, bd, t, "s5_bu"), a2)
            yc_s = mm_s5("yc", x2, cd, t, "s5_yc")
            yg = s5_post_fwd(s5_reorder(yc_s, False), u, ssm_d_full)
            w_out, = weights([k_out], yg)
            hg = mm_cs_fwd(yg, w_out, 0, BF16, "ssm_glu_fwd")
            x1, xn2 = glu_fwd(hg, xc, g1)
            st.update(u_s=u_s, x2=x2, yc_s=yc_s, yg=yg, hg=hg)
        w_up, = weights([("ffn_w_up", i)], xn2)
        h3 = mm_up_fwd(xn2, w_up, 0, BF16, "ffn_up_fwd")
        gated = ffn_gate_fwd(h3, p3[i])
        w_down, = weights([("ffn_w_down", i)], gated)
        if g_next is None:
            xc = mm_down_fwd(gated, w_down, 0, x1, "ffn_down_fwd")
        else:
            xc, xn = mm_down_fwd(gated, w_down, 0, x1, "ffn_down_fwd", norm=g_next)
        st.update(x1=x1, xn2=xn2, h3=h3, gated=gated, g0=g0, g1=g1)
        acts.append(st)

    dx, loss_lanes, d_final_g = loss_head(xc, w["final_norm_g"][None], tgt)
    loss = lax.psum(loss_lanes[0, 0], MESH_AXES)

    scattering = {}
    d_norm = [[None, None] for _ in range(depth)]
    d_p3 = [None] * depth
    rep = {}
    d_ssm_d = None
    token = None

    def scatter(grads_by_key):
        keys = list(grads_by_key)
        started, tok = scatter_start([grads_by_key[key] for key in keys], "scatter_grads_start")
        scattering[tuple(keys)] = started
        return tok

    for i in reversed(range(depth)):
        mixer, j = i % 3, i // 3
        st = acts[i]
        k_in, k_out = [(k, j) for k in mixer_weights[mixer]]
        w_in, w_out, w_up, w_down = weights([k_in, k_out, ("ffn_w_up", i), ("ffn_w_down", i)], None)
        dgated = mm_down_da(dx, w_down, 0, "ffn_down_da", dep=token)
        g_down = mm_down_dw(st["gated"], dx, "ffn_down_dw")
        dy_a, dy_g, dp_a, dp_g = ffn_gate_bwd(st["h3"], dgated, p3[i])
        d_p3[i] = jnp.concatenate([dp_a, dp_g], axis=0)
        dh3 = ffn_conv_t(dy_a, dy_g, p3[i])
        dxn2 = mm_up_da(dh3, w_up, 0, "ffn_up_da")
        g_up = mm_up_dw(st["xn2"], dh3, "ffn_up_dw")
        dx1, d_norm[i][1] = rms_bwd(st["x1"], st["g1"], dxn2, dx, "rms_bwd")
        token = scatter({("ffn_w_down", i): g_down, ("ffn_w_up", i): g_up})
        if mixer == 0:
            do = mm_rs_da(dx1, w_out, 0, BF16, "attn_out_da", dep=token)
            g_out = mm_rs_dw(st["o"], dx1, "attn_out_dw")
            d3 = sb_attn_bwd(st["qkv"], st["ltot"], do)
            g_in = mm_qkv_dw(st["xn"], d3, w_in.shape[3], "qkv_dw")
            token = scatter({k_in: g_in, k_out: g_out})
            dxn = mm_qkv_da(d3, w_in, 0, "qkv_da", dep=token)
        elif mixer == 1:
            dp = mm_rs_da(dx1, w_out, 0, BF16, "sg_out_da", dep=token)
            g_out = mm_rs_dw(st["p"], dx1, "sg_out_dw")
            dhin, d_ws, d_bfull, d_gain = sgu_bwd(st["hin"], dp, sg_gain, sg_ws, sg_bfull)
            rep.update(sg_w_s=d_ws[None], sg_b=d_bfull[None, :, :, 0], sg_norm_g=d_gain)
            dxn = mm_cs_da(dhin, w_in, 0, t, "sg_in_da")
            g_in = mm_cs_dw(st["xn"], dhin, w_in.shape[3], "sg_in_dw")
        else:
            dhg = glu_bwd(st["hg"], dx1, token)
            dyg = mm_cs_da(dhg, w_out, 0, t, "ssm_glu_da")
            g_out = mm_cs_dw(st["yg"], dhg, w_out.shape[3], "ssm_glu_dw")
            dyc_s, du_skip_s, d_ssm_d = s5_post_bwd(st["yc_s"], st["u_s"], ssm_d_full, s5_reorder(dyg, True))
            dx2 = mm_s5("dx", dyc_s, cd, t, "s5_dx")
            dcd = mm_s5("dcd", st["x2"], dyc_s, t, "s5_dcd")
            g2, da2 = s5_scan_bwd(dx2, st["x2"], a2)
            du = s5_reorder(mm_s5("du", g2, bd, t, "s5_du", res=du_skip_s), False)
            dbd = mm_s5("dbd", st["u_s"], g2, t, "s5_dbd")
            from_bd = lambda blk: jnp.swapaxes(_block_diag_t(blk, h_, p_), 1, 2).reshape(g_ * p_, h_)

            def disc2_bwd(c1, c2, b1, b2, t1, t2):
                return jax.vjp(_disc2, c1, c2, b1, b2)[1]((t1, t2))

            d_cre, d_cim, d_b_re, d_b_im = _single(
                disc2_bwd, [cre_c, cim_c, b_re, b_im, from_bd(dbd[0]), from_bd(dbd[1])],
                [(g_ * p_, 1)] * 2 + [(g_ * p_, h_)] * 2, "s5_disc2_bwd")

            def disc1_bwd(l1, l2, ld, t1, t2, t3, t4):
                return jax.vjp(_disc1, l1, l2, ld)[1]((t1, t2, t3, t4))

            d_lam_re, d_lam_im, d_log_dt = _single(
                disc1_bwd, [lam_re, lam_im, log_dt, da2[0].reshape(g_, p_), da2[1].reshape(g_, p_),
                            d_cre.reshape(g_, p_), d_cim.reshape(g_, p_)],
                [(g_, p_), (g_, p_), (g_, 1)], "s5_disc1_bwd")
            from_cd = lambda blk: jnp.swapaxes(_block_diag_t(blk, p_, h_), 1, 2)
            rep.update(ssm_lam_re=d_lam_re[None], ssm_lam_im=d_lam_im[None], ssm_log_dt=d_log_dt.reshape(1, g_),
                       ssm_b_re=d_b_re.reshape(1, g_, p_, h_), ssm_b_im=d_b_im.reshape(1, g_, p_, h_),
                       ssm_c_re=from_cd(dcd[0])[None], ssm_c_im=-from_cd(dcd[1])[None])
            dxn = mm_rs_da(du, w_in, 0, F32, "ssm_in_da")
            g_in = mm_rs_dw(st["xn"], du, "ssm_in_dw")
        dx, d_norm[i][0] = rms_bwd(st["x"], st["g0"], dxn, dx1, "rms_bwd")
        if mixer != 0:
            token = scatter({k_in: g_in, k_out: g_out})

    rep["final_norm_g"] = d_final_g.reshape(d)
    rep["ffn_conv_b"] = jnp.stack([d_p3[l][:, CONV_K, :].reshape(NDEV * nc) for l in range(depth)])

    d_norm_full = jnp.stack([jnp.concatenate(pair, axis=0) for pair in d_norm])
    d_norm_pieces = jnp.transpose(d_norm_full.reshape(depth, 2, NDEV, d // NDEV), (2, 0, 1, 3))
    d_ssm_d_pieces = jnp.transpose(d_ssm_d.reshape(1, NDEV, d // NDEV), (1, 0, 2))
    d_conv_w_pieces = jnp.stack([d_p3[l][:, :CONV_K, :] for l in range(depth)], axis=1)
    small_pieces = _pack([d_norm_pieces, d_ssm_d_pieces, d_conv_w_pieces], lead=1)
    rep_parts, small_received = all_gather([_pack([rep[k] for k in REPLICATED]).astype(BF16)],
                                           name="exchange_small_grads", scatter=small_pieces)
    own, landed = {}, {}
    for keys, started in scattering.items():
        for key, (src, land) in zip(keys, scatter_wait(started, dx, "scatter_grads_wait")):
            own[key], landed[key] = src, land

    grads, deltas, new_m, new_v = {}, {}, {}, {}
    for k in BIG:
        layers = range(w[k].shape[0])
        res = adamw_layers(as_kept(w, k), as_kept(m, k), as_kept(v, k), [landed[(k, l)] for l in layers],
                           [own[(k, l)] for l in layers], me, "adamw")
        grads[k], deltas[k], new_m[k], new_v[k] = [jnp.swapaxes(r, 1, 2) if k == "ffn_w_up" else r for r in res]
    for names, parts in ((SMALL_SHARDED, small_received), (REPLICATED, rep_parts)):
        res = adamw(_pack([w[k] for k in names]), _pack([m[k] for k in names]), _pack([v[k] for k in names]), parts,
                    "adamw_small")
        for tree, buf in zip((grads, deltas, new_m, new_v), res):
            for k, val in zip(names, _unpack(buf, [w[k].shape for k in names])):
                tree[k] = val
    grad_x = dx.reshape(x.shape)
    return (loss, grad_x, *[grads[k] for k in WEIGHTS], *[deltas[k] for k in WEIGHTS],
            *[new_m[k] for k in WEIGHTS], *[new_v[k] for k in WEIGHTS])


def kernel(x, norm_g, final_norm_g, sb_w_qkv, sb_w_o, sg_w_in, sg_norm_g, sg_w_s, sg_b, sg_w_o, ssm_w_in, ssm_lam_re, ssm_lam_im, ssm_log_dt, ssm_b_re, ssm_b_im, ssm_c_re, ssm_c_im, ssm_d, ssm_w_glu, ffn_w_up, ffn_conv_w, ffn_conv_b, ffn_w_down, loss_target, m_norm_g, m_final_norm_g, m_sb_w_qkv, m_sb_w_o, m_sg_w_in, m_sg_norm_g, m_sg_w_s, m_sg_b, m_sg_w_o, m_ssm_w_in, m_ssm_lam_re, m_ssm_lam_im, m_ssm_log_dt, m_ssm_b_re, m_ssm_b_im, m_ssm_c_re, m_ssm_c_im, m_ssm_d, m_ssm_w_glu, m_ffn_w_up, m_ffn_conv_w, m_ffn_conv_b, m_ffn_w_down, v_norm_g, v_final_norm_g, v_sb_w_qkv, v_sb_w_o, v_sg_w_in, v_sg_norm_g, v_sg_w_s, v_sg_b, v_sg_w_o, v_ssm_w_in, v_ssm_lam_re, v_ssm_lam_im, v_ssm_log_dt, v_ssm_b_re, v_ssm_b_im, v_ssm_c_re, v_ssm_c_im, v_ssm_d, v_ssm_w_glu, v_ffn_w_up, v_ffn_conv_w, v_ffn_conv_b, v_ffn_w_down):
    w = dict(zip(WEIGHTS, (norm_g, final_norm_g, sb_w_qkv, sb_w_o, sg_w_in, sg_norm_g, sg_w_s, sg_b, sg_w_o, ssm_w_in,
                           ssm_lam_re, ssm_lam_im, ssm_log_dt, ssm_b_re, ssm_b_im, ssm_c_re, ssm_c_im, ssm_d, ssm_w_glu,
                           ffn_w_up, ffn_conv_w, ffn_conv_b, ffn_w_down)))
    m = dict(zip(WEIGHTS, (m_norm_g, m_final_norm_g, m_sb_w_qkv, m_sb_w_o, m_sg_w_in, m_sg_norm_g, m_sg_w_s, m_sg_b,
                           m_sg_w_o, m_ssm_w_in, m_ssm_lam_re, m_ssm_lam_im, m_ssm_log_dt, m_ssm_b_re, m_ssm_b_im,
                           m_ssm_c_re, m_ssm_c_im, m_ssm_d, m_ssm_w_glu, m_ffn_w_up, m_ffn_conv_w, m_ffn_conv_b,
                           m_ffn_w_down)))
    v = dict(zip(WEIGHTS, (v_norm_g, v_final_norm_g, v_sb_w_qkv, v_sb_w_o, v_sg_w_in, v_sg_norm_g, v_sg_w_s, v_sg_b,
                           v_sg_w_o, v_ssm_w_in, v_ssm_lam_re, v_ssm_lam_im, v_ssm_log_dt, v_ssm_b_re, v_ssm_b_im,
                           v_ssm_c_re, v_ssm_c_im, v_ssm_d, v_ssm_w_glu, v_ffn_w_up, v_ffn_conv_w, v_ffn_conv_b,
                           v_ffn_w_down)))
    return _step(x, loss_target, w, m, v)
```
